```python
import math
import jax, jax.numpy as jnp
from jax import lax
import numpy as np

D_MODEL = 1024
BATCH = 4
SEQ = 4096
DEPTH = 4

HEAD_DIM = 64
ROPE_THETA = 10000.0
NORM_EPS = 1e-6
NEG_INF = -1e30

DIFF_HEADS = 4
DIFF_QK_DIM = HEAD_DIM // 2
DIFF_V_DIM = HEAD_DIM
DENSE_BLOCK = 128
SWA_Q_HEADS = 8
SWA_KV_HEADS = 2
SWA_REP = SWA_Q_HEADS // SWA_KV_HEADS
SWA_WINDOW = 128
DIL_HEADS = 4
DIL_PATTERNS = ((128, 1), (512, 4), (2048, 16))
BAND_BLOCK = 128

A_QK = DIFF_HEADS * DIFF_QK_DIM
A_WIDTH = 4 * A_QK + DIFF_HEADS * DIFF_V_DIM
B_WIDTH = (SWA_Q_HEADS + 2 * SWA_KV_HEADS) * HEAD_DIM
C_WIDTH = 3 * DIL_HEADS * HEAD_DIM
IN_WIDTH = A_WIDTH + B_WIDTH + C_WIDTH
MIX_WIDTH = DIFF_HEADS * DIFF_V_DIM + SWA_Q_HEADS * HEAD_DIM + DIL_HEADS * HEAD_DIM

N_GROUPS = 4
EXPERTS_PER_GROUP = 4
N_EXPERTS = N_GROUPS * EXPERTS_PER_GROUP
TOP_K_IN_GROUP = 2
EXPERT_FF = 512
N_ADA = 6

kernel_name = "hymba_style_diff_swa_dilated_hmoe"

F32 = jnp.float32


def rmsnorm(x, g):
    xf = x.astype(F32)
    y = xf * lax.rsqrt(jnp.mean(xf * xf, axis=-1, keepdims=True) + NORM_EPS)
    return (y * g.astype(F32)).astype(x.dtype)


def rope_tables(positions, dim):
    inv_freq = ROPE_THETA ** (-jnp.arange(0, dim, 2, dtype=F32) / dim)
    ang = positions.astype(F32)[:, None, :, None] * inv_freq
    return jnp.cos(ang), jnp.sin(ang)


def rope(t, cos, sin):
    tf = t.astype(F32)
    h = tf.shape[-1] // 2
    t1, t2 = tf[..., :h], tf[..., h:]
    return jnp.concatenate([t1 * cos - t2 * sin, t1 * sin + t2 * cos], axis=-1).astype(t.dtype)


def split_heads(t, n, dh):
    b, s, _ = t.shape
    return t.reshape(b, s, n, dh).transpose(0, 2, 1, 3)


def merge_heads(t):
    b, h, s, dh = t.shape
    return t.transpose(0, 2, 1, 3).reshape(b, s, h * dh)


def banded_attention(q, k, v, max_dist, sink=None):
    n, g, r, length, dh = q.shape
    blk = min(BAND_BLOCK, length)
    nb = -(-length // blk)
    pad = nb * blk - length
    qb = jnp.pad(q, ((0, 0), (0, 0), (0, 0), (0, pad), (0, 0))).reshape(n, g, r, nb, blk, dh)
    kp = jnp.pad(k, ((0, 0), (0, 0), (blk, pad), (0, 0))).reshape(n, g, nb + 1, blk, dh)
    vp = jnp.pad(v, ((0, 0), (0, 0), (blk, pad), (0, 0))).reshape(n, g, nb + 1, blk, dh)
    kw = jnp.concatenate([kp[:, :, :-1], kp[:, :, 1:]], axis=3)
    vw = jnp.concatenate([vp[:, :, :-1], vp[:, :, 1:]], axis=3).astype(F32)
    s = jnp.einsum('ngrbqd,ngbkd->ngrbqk', qb, kw).astype(F32) * (dh ** -0.5)
    qi = jnp.arange(blk)[:, None] + blk
    kj = jnp.arange(2 * blk)[None, :]
    dist = qi - kj
    key_abs = (jnp.arange(nb) * blk)[:, None, None] + kj[None] - blk
    valid = (dist >= 0) & (dist <= max_dist) & (key_abs >= 0)
    s = jnp.where(valid, s, NEG_INF)
    lse = jax.nn.logsumexp(s, axis=-1)
    denom = lse if sink is None else jnp.logaddexp(lse, sink.astype(F32)[None, :, :, None, None])
    p = jnp.exp(s - denom[..., None])
    out = jnp.einsum('ngrbqk,ngbkd->ngrbqd', p, vw)
    out = out.reshape(n, g, r, nb * blk, dh)[:, :, :, :length]
    lse = lse.reshape(n, g, r, nb * blk)[..., :length]
    return out, lse


def differential_attention(q1, q2, k1, k2, v, lam):
    b, h, s, dqk = q1.shape
    nb = s // DENSE_BLOCK
    scale = dqk ** -0.5
    vf = v.astype(F32)
    kpos = jnp.arange(s)

    def to_blocks(t):
        return t.reshape(b, h, nb, DENSE_BLOCK, dqk).transpose(2, 0, 1, 3, 4)

    def one_block(args):
        q1b, q2b, start = args
        causal = (start + jnp.arange(DENSE_BLOCK))[:, None] >= kpos[None, :]
        s1 = jnp.einsum('bhqd,bhkd->bhqk', q1b, k1).astype(F32) * scale
        s2 = jnp.einsum('bhqd,bhkd->bhqk', q2b, k2).astype(F32) * scale
        p1 = jax.nn.softmax(jnp.where(causal, s1, NEG_INF), axis=-1)
        p2 = jax.nn.softmax(jnp.where(causal, s2, NEG_INF), axis=-1)
        return jnp.einsum('bhqk,bhkd->bhqd', p1 - lam * p2, vf)

    out = lax.map(one_block, (to_blocks(q1), to_blocks(q2), jnp.arange(nb) * DENSE_BLOCK))
    return out.transpose(1, 2, 0, 3, 4).reshape(b, h, s, v.shape[-1])


def dilated_branch(q, k, v, window, dilation):
    b, h, s, dh = q.shape
    length = s // dilation

    def gather(t):
        return t.reshape(b, h, length, dilation, dh).transpose(0, 3, 1, 2, 4).reshape(b * dilation, h, length, dh)

    out, lse = banded_attention(gather(q)[:, :, None], gather(k), gather(v), window // dilation)
    out = out.reshape(b, dilation, h, length, dh).transpose(0, 2, 3, 1, 4).reshape(b, h, s, dh)
    lse = lse.reshape(b, dilation, h, length).transpose(0, 2, 3, 1).reshape(b, h, s)
    return out, lse


def dilated_mixture(q, k, v):
    outs, lses = [], []
    for window, dilation in DIL_PATTERNS:
        o, l = dilated_branch(q, k, v, window, dilation)
        outs.append(o)
        lses.append(l)
    w = jax.nn.softmax(jnp.stack(lses, axis=0), axis=0)
    return jnp.sum(w[..., None] * jnp.stack(outs, axis=0), axis=0)


def hierarchical_moe(h, wg, bg, we, be, w_gate, w_up, w_down):
    b, s, d = h.shape
    t = h.reshape(-1, d)
    g_prob = jax.nn.softmax((t @ wg + bg).astype(F32), axis=-1)
    g_w, g_idx = lax.top_k(g_prob, 1)
    e_logits = (t @ we + be).astype(F32).reshape(-1, N_GROUPS, EXPERTS_PER_GROUP)
    e_sel = jnp.take_along_axis(e_logits, g_idx[:, :, None], axis=1)[:, 0]
    top_w, top_i = lax.top_k(jax.nn.softmax(e_sel, axis=-1), TOP_K_IN_GROUP)
    top_w = top_w / jnp.sum(top_w, axis=-1, keepdims=True)
    weights = g_w * top_w
    expert_id = g_idx * EXPERTS_PER_GROUP + top_i
    gates = jnp.einsum('nk,nke->ne', weights, jax.nn.one_hot(expert_id, N_EXPERTS, dtype=F32))
    hg = jnp.einsum('nd,edf->nef', t, w_gate)
    hu = jnp.einsum('nd,edf->nef', t, w_up)
    act = jax.nn.silu(hg) * hu * gates[:, :, None].astype(t.dtype)
    y = jnp.einsum('nef,efd->nd', act, w_down)
    return y.reshape(b, s, d)


def setup_inputs(seed: int = 0) -> dict:
    key = jax.random.key(seed)
    ks = jax.random.split(key, 24)

    def nrm(k, shape, scale):
        return jax.random.normal(k, shape, F32) * scale

    x = nrm(ks[0], (BATCH, SEQ, D_MODEL), 1.0)
    c = nrm(ks[1], (BATCH, D_MODEL), 1.0)
    positions = (jax.random.randint(ks[2], (BATCH, 1), 0, 4096, dtype=jnp.int32)
                 + jnp.arange(SEQ, dtype=jnp.int32)[None, :])
    return {
        "x": x,
        "c": c,
        "positions": positions,
        "ada_w": nrm(ks[3], (DEPTH, D_MODEL, N_ADA * D_MODEL), 0.5 * D_MODEL ** -0.5),
        "ada_b": nrm(ks[4], (DEPTH, N_ADA * D_MODEL), 0.02),
        "norm_mix_g": 1.0 + nrm(ks[5], (DEPTH, D_MODEL), 0.02),
        "norm_ffn_g": 1.0 + nrm(ks[6], (DEPTH, D_MODEL), 0.02),
        "w_in": nrm(ks[7], (DEPTH, D_MODEL, IN_WIDTH), D_MODEL ** -0.5),
        "w_out": nrm(ks[8], (DEPTH, MIX_WIDTH, D_MODEL), MIX_WIDTH ** -0.5),
        "diff_lambda_q1": nrm(ks[9], (DEPTH, DIFF_QK_DIM), 0.1),
        "diff_lambda_k1": nrm(ks[10], (DEPTH, DIFF_QK_DIM), 0.1),
        "diff_lambda_q2": nrm(ks[11], (DEPTH, DIFF_QK_DIM), 0.1),
        "diff_lambda_k2": nrm(ks[12], (DEPTH, DIFF_QK_DIM), 0.1),
        "diff_subln_g": 1.0 + nrm(ks[13], (DEPTH, DIFF_V_DIM), 0.02),
        "swa_sinks": nrm(ks[14], (DEPTH, SWA_Q_HEADS), 0.5),
        "router_group_w": nrm(ks[15], (DEPTH, D_MODEL, N_GROUPS), D_MODEL ** -0.5),
        "router_group_b": nrm(ks[16], (DEPTH, N_GROUPS), 0.01),
        "router_expert_w": nrm(ks[17], (DEPTH, D_MODEL, N_EXPERTS), D_MODEL ** -0.5),
        "router_expert_b": nrm(ks[18], (DEPTH, N_EXPERTS), 0.01),
        "expert_w_gate": nrm(ks[19], (DEPTH, N_EXPERTS, D_MODEL, EXPERT_FF), D_MODEL ** -0.5),
        "expert_w_up": nrm(ks[20], (DEPTH, N_EXPERTS, D_MODEL, EXPERT_FF), D_MODEL ** -0.5),
        "expert_w_down": nrm(ks[21], (DEPTH, N_EXPERTS, EXPERT_FF, D_MODEL), EXPERT_FF ** -0.5),
        "final_norm_g": 1.0 + nrm(ks[22], (D_MODEL,), 0.02),
    }


def reference(x, c, positions, ada_w, ada_b, norm_mix_g, norm_ffn_g, w_in, w_out,
              diff_lambda_q1, diff_lambda_k1, diff_lambda_q2, diff_lambda_k2, diff_subln_g,
              swa_sinks, router_group_w, router_group_b, router_expert_w, router_expert_b,
              expert_w_gate, expert_w_up, expert_w_down, final_norm_g):
    b, s, _ = x.shape
    cos64, sin64 = rope_tables(positions, HEAD_DIM)
    cos32, sin32 = rope_tables(positions, DIFF_QK_DIM)
    c_act = jax.nn.silu(c)

    for l in range(DEPTH):
        mod = (c_act @ ada_w[l] + ada_b[l]).astype(x.dtype)
        sh1, sc1, g1, sh2, sc2, g2 = [m[:, None, :] for m in jnp.split(mod, N_ADA, axis=-1)]

        h = rmsnorm(x, norm_mix_g[l]) * (1 + sc1) + sh1
        proj = h @ w_in[l]
        pa, pb, pc = jnp.split(proj, [A_WIDTH, A_WIDTH + B_WIDTH], axis=-1)

        qa1, qa2, ka1, ka2, va = jnp.split(pa, [A_QK, 2 * A_QK, 3 * A_QK, 4 * A_QK], axis=-1)
        qa1, qa2, ka1, ka2 = [rope(split_heads(t, DIFF_HEADS, DIFF_QK_DIM), cos32, sin32)
                              for t in (qa1, qa2, ka1, ka2)]
        va = split_heads(va, DIFF_HEADS, DIFF_V_DIM)
        lambda_init = 0.8 - 0.6 * math.exp(-0.3 * l)
        lam = (jnp.exp(jnp.sum(diff_lambda_q1[l].astype(F32) * diff_lambda_k1[l].astype(F32)))
               - jnp.exp(jnp.sum(diff_lambda_q2[l].astype(F32) * diff_lambda_k2[l].astype(F32)))
               + lambda_init)
        oa = differential_attention(qa1, qa2, ka1, ka2, va, lam)
        oa = rmsnorm(oa, diff_subln_g[l]) * (1.0 - lambda_init)
        oa = merge_heads(oa.astype(x.dtype))

        qb, kb, vb = jnp.split(pb, [SWA_Q_HEADS * HEAD_DIM, (SWA_Q_HEADS + SWA_KV_HEADS) * HEAD_DIM], axis=-1)
        qb = rope(split_heads(qb, SWA_Q_HEADS, HEAD_DIM), cos64, sin64)
        kb = rope(split_heads(kb, SWA_KV_HEADS, HEAD_DIM), cos64, sin64)
        vb = split_heads(vb, SWA_KV_HEADS, HEAD_DIM)
        qb = qb.reshape(b, SWA_KV_HEADS, SWA_REP, s, HEAD_DIM)
        ob, _ = banded_attention(qb, kb, vb, SWA_WINDOW - 1,
                                 sink=swa_sinks[l].reshape(SWA_KV_HEADS, SWA_REP))
        ob = merge_heads(ob.reshape(b, SWA_Q_HEADS, s, HEAD_DIM).astype(x.dtype))

        qc, kc, vc = jnp.split(pc, 3, axis=-1)
        qc = rope(split_heads(qc, DIL_HEADS, HEAD_DIM), cos64, sin64)
        kc = rope(split_heads(kc, DIL_HEADS, HEAD_DIM), cos64, sin64)
        vc = split_heads(vc, DIL_HEADS, HEAD_DIM)
        oc = merge_heads(dilated_mixture(qc, kc, vc).astype(x.dtype))

        mix = jnp.concatenate([oa, ob, oc], axis=-1) @ w_out[l]
        x = x + g1 * mix

        h = rmsnorm(x, norm_ffn_g[l]) * (1 + sc2) + sh2
        y = hierarchical_moe(h, router_group_w[l], router_group_b[l], router_expert_w[l],
                             router_expert_b[l], expert_w_gate[l], expert_w_up[l], expert_w_down[l])
        x = x + g2 * y

    return rmsnorm(x, final_norm_g)
```

```python
import functools
import math

import numpy as np
import jax
import jax.numpy as jnp
from jax import lax
from jax.experimental import pallas as pl
from jax.experimental.pallas import tpu as pltpu

F32 = jnp.float32
BF16 = jnp.bfloat16
I32 = jnp.int32

HEAD_DIM = 64
ROPE_THETA = 10000.0
NORM_EPS = 1e-6
NEG_INF = -1e30
DIFF_HEADS = 4
DIFF_QK_DIM = 32
SWA_Q_HEADS = 8
SWA_KV_HEADS = 2
SWA_WINDOW = 128
DIL_PATTERNS = ((128, 1), (512, 4), (2048, 16))
N_GROUPS = 4
EXPERTS_PER_GROUP = 4
N_EXPERTS = 16
EXPERT_FF = 512
N_ADA = 6
IN_WIDTH = 2304

LANES = 128
BF16_ROWS = 16
BAND = 128

TM = 256
CHUNK = BF16_ROWS
SLOTS = 2 * TM + N_EXPERTS * CHUNK
CHUNKS_PER_TILE = SLOTS // CHUNK
FFN_ROWS = 256
FFN_CHUNKS = FFN_ROWS // CHUNK
ZERO_CHUNKS = 6
TQ = 256
BAND_ROWS = 512

_SWA_HEAD_ORDER = (0, 4, 1, 5, 2, 6, 3, 7)
COL_QB, COL_KB, COL_VB = 0, 512, 640
COL_Q1, COL_Q2, COL_K1, COL_K2, COL_VA = 768, 896, 1024, 1152, 1280
COL_QC, COL_KC, COL_VC = 1536, 1792, 2048
_ROPE_KIND = (64, 64, 64, 64, 64, 0, 32, 32, 32, 32, 0, 0, 64, 64, 64, 64, 0, 0)


def _in_col_perm():
    a0, b0, c0 = 0, 768, 1536
    qb = [b0 + h * 64 + j for h in _SWA_HEAD_ORDER for j in range(64)]
    kb = list(range(b0 + 512, b0 + 640))
    vb = list(range(b0 + 640, b0 + 768))
    a = list(range(a0, a0 + 768))
    c = list(range(c0, c0 + 768))
    return np.asarray(qb + kb + vb + a + c, dtype=np.int32)


def _out_row_perm():
    oa = list(range(0, 256))
    ob = [256 + h * 64 + j for h in _SWA_HEAD_ORDER for j in range(64)]
    oc = list(range(768, 1024))
    return np.asarray(oa + ob + oc, dtype=np.int32)


def _residue_perm(tm, d):
    p = np.zeros((tm, tm), np.float32)
    per = tm // d
    for l in range(per):
        for r in range(d):
            p[r * per + l, l * d + r] = 1.0
    return p


def _dot(a, b, **kw):
    return jnp.dot(a, b, preferred_element_type=F32, **kw)


def _dot_nt(a, b):
    return lax.dot_general(a, b, (((1,), (1,)), ((), ())), preferred_element_type=F32)


def _modulated_norm(x, g, sc, sh):
    y = x * lax.rsqrt(jnp.mean(x * x, axis=-1, keepdims=True) + NORM_EPS)
    return (y * g) * (1.0 + sc) + sh


def _ada_kernel(c_ref, w_ref, b_ref, o_ref):
    c = c_ref[...]
    ca = c / (1.0 + jnp.exp(-c))
    o_ref[0] = _dot(ca, w_ref[0], precision=lax.Precision.HIGHEST) + b_ref[0]


def _ada_call(c_pad, ada_w, ada_b):
    depth, d, n = ada_w.shape
    tn = 1536
    return pl.pallas_call(
        _ada_kernel,
        grid=(depth, n // tn),
        in_specs=[
            pl.BlockSpec((c_pad.shape[0], d), lambda l, j: (0, 0)),
            pl.BlockSpec((1, d, tn), lambda l, j: (l, 0, j)),
            pl.BlockSpec((1, 1, tn), lambda l, j: (l, 0, j)),
        ],
        out_specs=pl.BlockSpec((1, c_pad.shape[0], tn), lambda l, j: (l, 0, j)),
        out_shape=jax.ShapeDtypeStruct((depth, c_pad.shape[0], n), F32),
        compiler_params=pltpu.CompilerParams(vmem_limit_bytes=40 * 1024 * 1024),
        name="ada_mod",
    )(c_pad, ada_w, ada_b.reshape(depth, 1, n))


def _rope(t, cos, sin_signed, first_half, half):
    rot = jnp.where(first_half, pltpu.roll(t, LANES - half, 1), pltpu.roll(t, half, 1))
    return t * cos + rot * sin_signed


def _inproj_kernel(x_ref, sc_ref, sh_ref, g_ref, w_ref, cs64_ref, sn64_ref, cs32_ref, sn32_ref,
                   p4_ref, p16_ref, proj_ref, c4_ref, c16_ref):
    h = _modulated_norm(x_ref[0], g_ref[...], sc_ref[0], sh_ref[0])
    hb = h.astype(BF16)
    lane = lax.broadcasted_iota(I32, (1, LANES), 1)
    first64 = (lane % 64) < 32
    first32 = (lane % 32) < 16
    for cb in range(IN_WIDTH // 256):
        acc = _dot(hb, w_ref[:, cb * 256:(cb + 1) * 256])
        for half in range(2):
            blk = cb * 2 + half
            t = acc[:, half * LANES:(half + 1) * LANES]
            if _ROPE_KIND[blk] == 64:
                t = _rope(t, cs64_ref[0], sn64_ref[0], first64, 32)
            elif _ROPE_KIND[blk] == 32:
                t = _rope(t, cs32_ref[0], sn32_ref[0], first32, 16)
            proj_ref[0, :, blk * LANES:(blk + 1) * LANES] = t.astype(BF16)
    cc = proj_ref[0, :, COL_QC:]
    tm = cc.shape[0]
    c4 = _dot(p4_ref[...], cc).astype(BF16)
    for r in range(4):
        c4_ref[0, r] = c4[r * (tm // 4):(r + 1) * (tm // 4)]
    c16 = _dot(p16_ref[...], cc).astype(BF16)
    for r in range(16):
        c16_ref[0, r] = c16[r * (tm // 16):(r + 1) * (tm // 16)]


def _inproj_call(x, sc, sh, g, w, tabs, p4, p16):
    b, s, d = x.shape
    tm = TM
    row = lambda bi, i: (bi, i, 0)
    per_b = lambda bi, i: (bi, 0, 0)
    const2 = lambda bi, i: (0, 0)
    return pl.pallas_call(
        _inproj_kernel,
        grid=(b, s // tm),
        in_specs=[
            pl.BlockSpec((1, tm, d), row),
            pl.BlockSpec((1, 1, d), per_b),
            pl.BlockSpec((1, 1, d), per_b),
            pl.BlockSpec((1, d), const2),
            pl.BlockSpec((d, IN_WIDTH), const2),
            pl.BlockSpec((1, tm, LANES), row),
            pl.BlockSpec((1, tm, LANES), row),
            pl.BlockSpec((1, tm, LANES), row),
            pl.BlockSpec((1, tm, LANES), row),
            pl.BlockSpec((tm, tm), const2),
            pl.BlockSpec((tm, tm), const2),
        ],
        out_specs=[
            pl.BlockSpec((1, tm, IN_WIDTH), row),
            pl.BlockSpec((1, 4, tm // 4, 768), lambda bi, i: (bi, 0, i, 0)),
            pl.BlockSpec((1, 16, tm // 16, 768), lambda bi, i: (bi, 0, i, 0)),
        ],
        out_shape=[
            jax.ShapeDtypeStruct((b, s, IN_WIDTH), BF16),
            jax.ShapeDtypeStruct((b, 4, s // 4, 768), BF16),
            jax.ShapeDtypeStruct((b, 16, s // 16, 768), BF16),
        ],
        compiler_params=pltpu.CompilerParams(vmem_limit_bytes=48 * 1024 * 1024),
        name="in_proj",
    )(x, sc, sh, g, w, *tabs, p4, p16)


def _diff_attn_kernel(lam_ref, q1_ref, q2_ref, k1_ref, k2_ref, v_ref, g_ref, o_ref, *, lambda_init):
    tq = q1_ref.shape[1]
    qi = pl.program_id(1)
    lam = lam_ref[0]
    scale = DIFF_QK_DIM ** -0.5
    lane = lax.broadcasted_iota(I32, (1, LANES), 1)
    lo_half = lane < 64
    q1 = q1_ref[0].astype(F32) * scale
    q2 = q2_ref[0].astype(F32) * scale
    causal = (lax.broadcasted_iota(I32, (tq, tq), 0) >= lax.broadcasted_iota(I32, (tq, tq), 1))

    heads = []
    for h in range(DIFF_HEADS):
        hm = (lane // DIFF_QK_DIM) == h
        q1h = jnp.where(hm, q1, 0.0).astype(BF16)
        q2h = jnp.where(hm, q2, 0.0).astype(BF16)
        vb = h // 2

        def softmax_step(s, vt, m, l, a):
            mn = jnp.maximum(m, jnp.max(s, axis=1, keepdims=True))
            p = jnp.exp(s - mn)
            al = jnp.exp(m - mn)
            l = al * l + jnp.sum(p, axis=1, keepdims=True)
            a = al * a + _dot(p.astype(BF16), vt)
            return mn, l, a

        def step(j, carry, masked, q1h=q1h, q2h=q2h, vb=vb):
            m1, l1, a1, m2, l2, a2 = carry
            start = pl.multiple_of(j * tq, tq)
            k1t = k1_ref[0, pl.ds(start, tq), :]
            k2t = k2_ref[0, pl.ds(start, tq), :]
            vt = v_ref[0, pl.ds(start, tq), vb * LANES:(vb + 1) * LANES]
            s1 = _dot_nt(q1h, k1t)
            s2 = _dot_nt(q2h, k2t)
            if masked:
                s1 = jnp.where(causal, s1, NEG_INF)
                s2 = jnp.where(causal, s2, NEG_INF)
            m1, l1, a1 = softmax_step(s1, vt, m1, l1, a1)
            m2, l2, a2 = softmax_step(s2, vt, m2, l2, a2)
            return m1, l1, a1, m2, l2, a2

        init = (jnp.full((tq, 1), NEG_INF, F32), jnp.zeros((tq, 1), F32), jnp.zeros((tq, LANES), F32),
                jnp.full((tq, 1), NEG_INF, F32), jnp.zeros((tq, 1), F32), jnp.zeros((tq, LANES), F32))
        carry = lax.fori_loop(0, qi, functools.partial(step, masked=False), init)
        m1, l1, a1, m2, l2, a2 = step(qi, carry, True)
        heads.append(a1 / l1 - lam * (a2 / l2))

    g = g_ref[...]
    for pair in range(DIFF_HEADS // 2):
        o = jnp.where(lo_half, heads[2 * pair], heads[2 * pair + 1])
        sq = o * o
        s_lo = jnp.sum(jnp.where(lo_half, sq, 0.0), axis=1, keepdims=True)
        s_hi = jnp.sum(jnp.where(lo_half, 0.0, sq), axis=1, keepdims=True)
        ms = jnp.where(lo_half, s_lo, s_hi) * (1.0 / HEAD_DIM)
        y = (o * lax.rsqrt(ms + NORM_EPS)) * g * (1.0 - lambda_init)
        o_ref[0, :, pair * LANES:(pair + 1) * LANES] = y.astype(BF16)


def _diff_attn_call(proj, lam, g_tiled, lambda_init):
    b, s, _ = proj.shape
    tq = TQ
    qspec = lambda cb: pl.BlockSpec((1, tq, LANES), lambda bi, i, cb=cb: (bi, i, cb))
    kspec = lambda cb: pl.BlockSpec((1, s, LANES), lambda bi, i, cb=cb: (bi, 0, cb))
    return pl.pallas_call(
        functools.partial(_diff_attn_kernel, lambda_init=lambda_init),
        grid=(b, s // tq),
        in_specs=[
            pl.BlockSpec(memory_space=pltpu.SMEM),
            qspec(COL_Q1 // LANES), qspec(COL_Q2 // LANES),
            kspec(COL_K1 // LANES), kspec(COL_K2 // LANES),
            pl.BlockSpec((1, s, 256), lambda bi, i: (bi, 0, COL_VA // 256)),
            pl.BlockSpec((1, LANES), lambda bi, i: (0, 0)),
        ],
        out_specs=pl.BlockSpec((1, tq, 256), lambda bi, i: (bi, i, 0)),
        out_shape=jax.ShapeDtypeStruct((b, s, 256), BF16),
        compiler_params=pltpu.CompilerParams(vmem_limit_bytes=48 * 1024 * 1024),
        name="diff_attn",
    )(lam, proj, proj, proj, proj, proj, g_tiled)


def _band_kernel(*refs, nqb, nkb, max_dist, has_sink, want_lse):
    it = iter(refs)
    sink_ref = next(it) if has_sink else None
    q_ref, kp_ref, kc_ref, vp_ref, vc_ref = (next(it) for _ in range(5))
    o_ref = next(it)
    lse_ref = next(it) if want_lse else None
    rows = q_ref.shape[1]
    i = pl.program_id(1)
    kfull = jnp.concatenate([kp_ref[0], kc_ref[0]], axis=0)
    vfull = jnp.concatenate([vp_ref[0], vc_ref[0]], axis=0)
    lane = lax.broadcasted_iota(I32, (1, LANES), 1)
    lo_half = lane < 64
    r_io = lax.broadcasted_iota(I32, (BAND, 2 * BAND), 0)
    c_io = lax.broadcasted_iota(I32, (BAND, 2 * BAND), 1)
    dist = BAND + r_io - c_io
    band = (dist >= 0) & (dist <= max_dist)
    first_pen = jnp.where(c_io < BAND, jnp.where(i == 0, NEG_INF, 0.0), 0.0)
    qscale = jnp.asarray(HEAD_DIM ** -0.5, BF16)
    for sb in range(rows // BAND):
        r0 = sb * BAND
        for qb in range(nqb):
            kb = qb if nkb > 1 else 0
            q = q_ref[0, r0:r0 + BAND, qb * LANES:(qb + 1) * LANES] * qscale
            kw = kfull[r0:r0 + 2 * BAND, kb * LANES:(kb + 1) * LANES]
            vw = vfull[r0:r0 + 2 * BAND, kb * LANES:(kb + 1) * LANES]
            outs, lses = [], []
            for hh in range(2):
                hm = lo_half if hh == 0 else jnp.logical_not(lo_half)
                qh = jnp.where(hm, q, jnp.zeros_like(q))
                s = _dot_nt(qh, kw)
                s = jnp.where(band, s, NEG_INF)
                if sb == 0:
                    s = s + first_pen
                m = jnp.max(s, axis=1, keepdims=True)
                if has_sink:
                    snk = sink_ref[qb * 2 + hh]
                    m = jnp.maximum(m, snk)
                p = jnp.exp(s - m)
                l = jnp.sum(p, axis=1, keepdims=True)
                pv = _dot(p.astype(BF16), vw)
                if want_lse:
                    lses.append(m + jnp.log(l))
                if has_sink:
                    l = l + jnp.exp(snk - m)
                outs.append(pv / l)
            o = jnp.where(lo_half, outs[0], outs[1])
            o_ref[0, r0:r0 + BAND, qb * LANES:(qb + 1) * LANES] = o.astype(BF16)
            if want_lse:
                ls = jnp.where(lo_half, lses[0], lses[1])
                hi = ls.astype(BF16)
                lo = (ls - hi.astype(F32)).astype(BF16)
                lse_ref[0, r0:r0 + BAND, qb * LANES:(qb + 1) * LANES] = hi
                lse_ref[0, r0:r0 + BAND, (nqb + qb) * LANES:(nqb + qb + 1) * LANES] = lo


def _band_call(arr, q_col, k_col, v_col, nqb, nkb, max_dist, sinks=None, want_lse=False, name="band"):
    ns, length, _ = arr.shape
    rows = min(BAND_ROWS, length)
    wq, wk = nqb * LANES, nkb * LANES
    rpb = rows // BAND
    cur = lambda col, w: pl.BlockSpec((1, rows, w), lambda n, i, c=col // w: (n, i, c))
    prev = lambda col, w: pl.BlockSpec(
        (1, BAND, w), lambda n, i, c=col // w: (n, jnp.maximum(i * rpb - 1, 0), c))
    in_specs = [cur(q_col, wq), prev(k_col, wk), cur(k_col, wk), prev(v_col, wk), cur(v_col, wk)]
    args = [arr] * 5
    if sinks is not None:
        in_specs = [pl.BlockSpec(memory_space=pltpu.SMEM)] + in_specs
        args = [sinks] + args
    out_specs = [pl.BlockSpec((1, rows, wq), lambda n, i: (n, i, 0))]
    out_shape = [jax.ShapeDtypeStruct((ns, length, wq), BF16)]
    if want_lse:
        out_specs.append(pl.BlockSpec((1, rows, 2 * wq), lambda n, i: (n, i, 0)))
        out_shape.append(jax.ShapeDtypeStruct((ns, length, 2 * wq), BF16))
    return pl.pallas_call(
        functools.partial(_band_kernel, nqb=nqb, nkb=nkb, max_dist=max_dist,
                          has_sink=sinks is not None, want_lse=want_lse),
        grid=(ns, length // rows),
        in_specs=in_specs,
        out_specs=out_specs,
        out_shape=out_shape,
        name=name,
    )(*args)


def _outproj_kernel(oa_ref, ob_ref, o1_ref, l1_ref, o4_ref, l4_ref, o16_ref, l16_ref, p4t_ref, p16t_ref,
                    w_ref, x_ref, g1_ref, xo_ref):
    tm = x_ref.shape[1]
    hw = o1_ref.shape[2]

    def lse_of(v):
        return v[:, :hw] + v[:, hw:]

    o1 = o1_ref[0].astype(F32)
    ls1 = lse_of(l1_ref[0].astype(F32))
    o4 = _dot(p4t_ref[...], o4_ref[0].reshape(tm, hw))
    ls4 = lse_of(_dot(p4t_ref[...], l4_ref[0].reshape(tm, 2 * hw)))
    o16 = _dot(p16t_ref[...], o16_ref[0].reshape(tm, hw))
    ls16 = lse_of(_dot(p16t_ref[...], l16_ref[0].reshape(tm, 2 * hw)))
    mx = jnp.maximum(jnp.maximum(ls1, ls4), ls16)
    e1, e4, e16 = jnp.exp(ls1 - mx), jnp.exp(ls4 - mx), jnp.exp(ls16 - mx)
    oc = (e1 * o1 + e4 * o4 + e16 * o16) / (e1 + e4 + e16)
    mix = (_dot(oa_ref[0], w_ref[0:256, :]) + _dot(ob_ref[0], w_ref[256:768, :])
           + _dot(oc.astype(BF16), w_ref[768:1024, :]))
    xo_ref[0] = x_ref[0] + g1_ref[0] * mix


def _outproj_call(oa, ob, o1, l1, o4, l4, o16, l16, p4t, p16t, w, x, g1):
    b, s, d = x.shape
    tm = TM
    row = lambda w_: pl.BlockSpec((1, tm, w_), lambda bi, i: (bi, i, 0))
    res = lambda dd, w_: pl.BlockSpec((1, dd, tm // dd, w_), lambda bi, i: (bi, 0, i, 0))
    const2 = lambda bi, i: (0, 0)
    return pl.pallas_call(
        _outproj_kernel,
        grid=(b, s // tm),
        in_specs=[
            row(256), row(512), row(256), row(512),
            res(4, 256), res(4, 512), res(16, 256), res(16, 512),
            pl.BlockSpec((tm, tm), const2), pl.BlockSpec((tm, tm), const2),
            pl.BlockSpec((d, d), const2),
            row(d),
            pl.BlockSpec((1, 1, d), lambda bi, i: (bi, 0, 0)),
        ],
        out_specs=row(d),
        out_shape=jax.ShapeDtypeStruct((b, s, d), F32),
        compiler_params=pltpu.CompilerParams(vmem_limit_bytes=48 * 1024 * 1024),
        name="out_proj",
    )(oa, ob, o1, l1, o4, l4, o16, l16, p4t, p16t, w, x, g1)


ROUTER_COLS = LANES


def _router_kernel(x_ref, sc_ref, sh_ref, g_ref, wr_ref, br_ref, tri_ref, hs_ref, meta_ref, nch_ref):
    tm = x_ref.shape[1]
    h = _modulated_norm(x_ref[0], g_ref[...], sc_ref[0], sh_ref[0])
    hb = h.astype(BF16)
    logits = _dot(h, wr_ref[...], precision=lax.Precision.HIGHEST) + br_ref[...]
    lt = logits.T
    glog = lt[0:8]
    elog = lt[8:8 + N_EXPERTS]
    r8 = lax.broadcasted_iota(I32, (8, tm), 0)
    r16 = lax.broadcasted_iota(I32, (N_EXPERTS, tm), 0)

    gmax = jnp.max(glog, axis=0, keepdims=True)
    g_w = 1.0 / jnp.sum(jnp.exp(glog - gmax), axis=0, keepdims=True)
    g_idx = jnp.min(jnp.where(glog == gmax, r8, 99), axis=0, keepdims=True)

    el = jnp.where((r16 // EXPERTS_PER_GROUP) == g_idx, elog, NEG_INF)
    emax = jnp.max(el, axis=0, keepdims=True)
    e1 = jnp.min(jnp.where(el == emax, r16, 99), axis=0, keepdims=True)
    el2 = jnp.where(r16 == e1, NEG_INF, el)
    emax2 = jnp.max(el2, axis=0, keepdims=True)
    e2 = jnp.min(jnp.where(el2 == emax2, r16, 99), axis=0, keepdims=True)
    p2 = jnp.exp(emax2 - emax)
    wt1 = g_w / (1.0 + p2)
    wt2 = g_w * p2 / (1.0 + p2)

    oh1 = r16 == e1
    oh2 = r16 == e2
    onehot = jnp.where(oh1, 1.0, 0.0) + jnp.where(oh2, 1.0, 0.0)
    cnt = jnp.sum(onehot, axis=1, keepdims=True)
    nch = jnp.floor((cnt + (CHUNK - 1)) * (1.0 / CHUNK))
    nchb = jnp.broadcast_to(nch, (N_EXPERTS, LANES))
    rl = lax.broadcasted_iota(I32, (N_EXPERTS, LANES), 0)
    incl = nchb
    for sft in (1, 2, 4, 8):
        incl = incl + jnp.where(rl >= sft, pltpu.roll(incl, sft, 0), 0.0)
    off = (incl - nchb)[:, 0:1] * float(CHUNK)
    rank = _dot(onehot.astype(BF16), tri_ref[...])
    slot_of = off + rank
    pos1 = jnp.sum(jnp.where(oh1, slot_of, 0.0), axis=0, keepdims=True)
    pos2 = jnp.sum(jnp.where(oh2, slot_of, 0.0), axis=0, keepdims=True)

    slot = lax.broadcasted_iota(I32, (SLOTS, tm), 0)
    sel = jnp.where(slot == pos1.astype(I32), 1.0, jnp.where(slot == pos2.astype(I32), 1.0, 0.0))
    hs_ref[...] = _dot(sel.astype(BF16), hb).astype(BF16)

    meta_ref[0] = jnp.concatenate([pos1, pos2, wt1, wt2, jnp.zeros((4, tm), F32)], axis=0)
    nch_ref[0] = nchb.astype(I32)


def _router_call(x2d, sc, sh, g, wr, br, tri, tiles_per_batch):
    n, d = x2d.shape
    tm = TM
    nt = n // tm
    x3 = x2d.reshape(nt, tm, d)
    per_b = lambda t: (t // tiles_per_batch, 0, 0)
    return pl.pallas_call(
        _router_kernel,
        grid=(nt,),
        in_specs=[
            pl.BlockSpec((1, tm, d), lambda t: (t, 0, 0)),
            pl.BlockSpec((1, 1, d), per_b),
            pl.BlockSpec((1, 1, d), per_b),
            pl.BlockSpec((1, d), lambda t: (0, 0)),
            pl.BlockSpec((d, ROUTER_COLS), lambda t: (0, 0)),
            pl.BlockSpec((1, ROUTER_COLS), lambda t: (0, 0)),
            pl.BlockSpec((tm, tm), lambda t: (0, 0)),
        ],
        out_specs=[
            pl.BlockSpec((SLOTS, d), lambda t: (t, 0)),
            pl.BlockSpec((1, 8, tm), lambda t: (t, 0, 0)),
            pl.BlockSpec((1, N_EXPERTS, LANES), lambda t: (t, 0, 0)),
        ],
        out_shape=[
            jax.ShapeDtypeStruct((nt * SLOTS, d), BF16),
            jax.ShapeDtypeStruct((nt, 8, tm), F32),
            jax.ShapeDtypeStruct((nt, N_EXPERTS, LANES), I32),
        ],
        compiler_params=pltpu.CompilerParams(vmem_limit_bytes=48 * 1024 * 1024),
        name="router_sort",
    )(x3, sc, sh, g, wr, br, tri)


def _ffn_schedule(nch, max_tiles):
    nt = nch.shape[0]
    cend = jnp.cumsum(nch, axis=1)
    coff = cend - nch
    tot = jnp.sum(nch, axis=0)
    pad = ((tot + FFN_CHUNKS - 1) // FFN_CHUNKS) * FFN_CHUNKS
    eend = jnp.cumsum(pad)
    estart = eend - pad
    before = jnp.cumsum(nch, axis=0) - nch
    c = jnp.arange(CHUNKS_PER_TILE, dtype=I32)[None, :]
    e_of = jnp.sum((cend[:, None, :] <= c[:, :, None]).astype(I32), axis=2)
    valid = e_of < N_EXPERTS
    e_c = jnp.minimum(e_of, N_EXPERTS - 1)
    take = lambda a: jnp.take_along_axis(a, e_c, axis=1)
    dest = estart[e_c] + take(before) + (c - take(coff))
    total = max_tiles * FFN_CHUNKS
    dest = jnp.where(valid, dest, total)
    src = jnp.arange(nt, dtype=I32)[:, None] * CHUNKS_PER_TILE + c
    chunk_src = jnp.full((total,), -1, I32).at[dest.reshape(-1)].set(src.reshape(-1), mode="drop")
    n_tiles = (eend[-1] // FFN_CHUNKS).astype(I32)
    first_chunk = jnp.arange(max_tiles, dtype=I32) * FFN_CHUNKS
    tile_expert = jnp.sum((eend[None, :] <= first_chunk[:, None]).astype(I32), axis=1)
    tile_expert = jnp.minimum(tile_expert, N_EXPERTS - 1)
    unused = jnp.logical_not(valid).reshape(-1)
    zdest = jnp.where(unused, jnp.cumsum(unused.astype(I32)) - 1, max_tiles * ZERO_CHUNKS)
    zero_chunks = jnp.full((max_tiles * ZERO_CHUNKS,), -1, I32).at[zdest].set(src.reshape(-1), mode="drop")
    return tile_expert, chunk_src, n_tiles.reshape(1), zero_chunks


def _ffn_kernel(te_ref, cs_ref, nt_ref, zc_ref, hs_hbm, wg_ref, wu_ref, wd_ref, ys_hbm,
                xbuf, ybuf, zbuf, in_sem, out_sem, zero_sem):
    j = pl.program_id(0)
    nt = nt_ref[0]

    def for_zero_chunks(action):
        for k in range(ZERO_CHUNKS):
            c = zc_ref[j * ZERO_CHUNKS + k]
            dst = ys_hbm.at[pl.ds(pl.multiple_of(jnp.maximum(c, 0) * CHUNK, CHUNK), CHUNK), :]
            cp = pltpu.make_async_copy(zbuf, dst, zero_sem)

            @pl.when(c >= 0)
            def _(cp=cp):
                getattr(cp, action)()

    @pl.when(j == 0)
    def _():
        zbuf[...] = jnp.zeros_like(zbuf)

    for_zero_chunks("start")

    def in_copy(step, slot, k):
        c = cs_ref[step * FFN_CHUNKS + k]
        src = hs_hbm.at[pl.ds(pl.multiple_of(jnp.maximum(c, 0) * CHUNK, CHUNK), CHUNK), :]
        return c, pltpu.make_async_copy(src, xbuf.at[slot, pl.ds(k * CHUNK, CHUNK), :], in_sem.at[slot])

    def out_copy(step, slot, k):
        c = cs_ref[step * FFN_CHUNKS + k]
        dst = ys_hbm.at[pl.ds(pl.multiple_of(jnp.maximum(c, 0) * CHUNK, CHUNK), CHUNK), :]
        return c, pltpu.make_async_copy(ybuf.at[slot, pl.ds(k * CHUNK, CHUNK), :], dst, out_sem.at[slot])

    def for_chunks(make, step, slot, action):
        for k in range(FFN_CHUNKS):
            c, cp = make(step, slot, k)

            @pl.when(c >= 0)
            def _(cp=cp):
                getattr(cp, action)()

    @pl.when(j < nt)
    def _():
        slot = j % 2

        @pl.when(j == 0)
        def _():
            xbuf[...] = jnp.zeros_like(xbuf)
            for_chunks(in_copy, 0, 0, "start")

        @pl.when(j + 1 < nt)
        def _():
            for_chunks(in_copy, j + 1, 1 - slot, "start")

        for_chunks(in_copy, j, slot, "wait")

        @pl.when(j >= 2)
        def _():
            for_chunks(out_copy, j - 2, slot, "wait")

        x = xbuf[slot]
        hg = _dot(x, wg_ref[0])
        hu = _dot(x, wu_ref[0])
        act = (hg / (1.0 + jnp.exp(-hg))) * hu
        ybuf[slot] = _dot(act.astype(BF16), wd_ref[0]).astype(BF16)
        for_chunks(out_copy, j, slot, "start")

        @pl.when(j == nt - 1)
        def _():
            for_chunks(out_copy, j, slot, "wait")

            @pl.when(j >= 1)
            def _():
                for_chunks(out_copy, j - 1, 1 - slot, "wait")

    for_zero_chunks("wait")


def _ffn_call(tile_expert, chunk_src, n_tiles, zero_chunks, hs, wg, wu, wd, max_tiles):
    rows, d = hs.shape
    ff = wg.shape[-1]
    wmap = lambda j, te, cs, nt, zc: (te[j], 0, 0)
    grid_spec = pltpu.PrefetchScalarGridSpec(
        num_scalar_prefetch=4,
        grid=(max_tiles,),
        in_specs=[
            pl.BlockSpec(memory_space=pl.ANY),
            pl.BlockSpec((1, d, ff), wmap),
            pl.BlockSpec((1, d, ff), wmap),
            pl.BlockSpec((1, ff, d), wmap),
        ],
        out_specs=pl.BlockSpec(memory_space=pl.ANY),
        scratch_shapes=[
            pltpu.VMEM((2, FFN_ROWS, d), BF16),
            pltpu.VMEM((2, FFN_ROWS, d), BF16),
            pltpu.VMEM((CHUNK, d), BF16),
            pltpu.SemaphoreType.DMA((2,)),
            pltpu.SemaphoreType.DMA((2,)),
            pltpu.SemaphoreType.DMA(()),
        ],
    )
    return pl.pallas_call(
        _ffn_kernel,
        grid_spec=grid_spec,
        out_shape=jax.ShapeDtypeStruct((rows, d), BF16),
        compiler_params=pltpu.CompilerParams(vmem_limit_bytes=48 * 1024 * 1024),
        name="expert_ffn",
    )(tile_expert, chunk_src, n_tiles, zero_chunks, hs, wg, wu, wd)


def _combine_kernel(ys_ref, meta_ref, x_ref, g2_ref, gf_ref, xo_ref, *, final_norm):
    tm = x_ref.shape[1]
    meta = meta_ref[0]
    eye = (lax.broadcasted_iota(I32, (tm, tm), 0) == lax.broadcasted_iota(I32, (tm, tm), 1))

    def as_col(row):
        return jnp.sum(jnp.where(eye, row, 0.0), axis=1, keepdims=True)

    pos1, pos2 = as_col(meta[0:1]), as_col(meta[1:2])
    w1, w2 = as_col(meta[2:3]), as_col(meta[3:4])
    slot = lax.broadcasted_iota(I32, (tm, SLOTS), 1).astype(F32)
    gate = jnp.where(slot == pos1, w1, 0.0) + jnp.where(slot == pos2, w2, 0.0)
    y = _dot(gate.astype(BF16), ys_ref[...])
    xo = x_ref[0] + g2_ref[0] * y
    if final_norm:
        xo = xo * lax.rsqrt(jnp.mean(xo * xo, axis=-1, keepdims=True) + NORM_EPS) * gf_ref[...]
    xo_ref[0] = xo


def _combine_call(ys, meta, x2d, g2, gf, tiles_per_batch, final_norm):
    n, d = x2d.shape
    tm = TM
    nt = n // tm
    return pl.pallas_call(
        functools.partial(_combine_kernel, final_norm=final_norm),
        grid=(nt,),
        in_specs=[
            pl.BlockSpec((SLOTS, d), lambda t: (t, 0)),
            pl.BlockSpec((1, 8, tm), lambda t: (t, 0, 0)),
            pl.BlockSpec((1, tm, d), lambda t: (t, 0, 0)),
            pl.BlockSpec((1, 1, d), lambda t: (t // tiles_per_batch, 0, 0)),
            pl.BlockSpec((1, d), lambda t: (0, 0)),
        ],
        out_specs=pl.BlockSpec((1, tm, d), lambda t: (t, 0, 0)),
        out_shape=jax.ShapeDtypeStruct((nt, tm, d), F32),
        compiler_params=pltpu.CompilerParams(vmem_limit_bytes=48 * 1024 * 1024),
        name="moe_combine",
    )(ys, meta, x2d.reshape(nt, tm, d), g2, gf)


def _rope_tables(positions):
    pos = positions.astype(F32)[..., None]

    def table(dim):
        inv = ROPE_THETA ** (-jnp.arange(0, dim, 2, dtype=F32) / dim)
        ang = pos * inv
        cos, sin = jnp.cos(ang), jnp.sin(ang)
        reps = LANES // dim
        return (jnp.tile(jnp.concatenate([cos, cos], -1), (1, 1, reps)),
                jnp.tile(jnp.concatenate([-sin, sin], -1), (1, 1, reps)))

    c64, s64 = table(HEAD_DIM)
    c32, s32 = table(DIFF_QK_DIM)
    return c64, s64, c32, s32


def kernel(x, c, positions, ada_w, ada_b, norm_mix_g, norm_ffn_g, w_in, w_out, diff_lambda_q1, diff_lambda_k1,
           diff_lambda_q2, diff_lambda_k2, diff_subln_g, swa_sinks, router_group_w, router_group_b,
           router_expert_w, router_expert_b, expert_w_gate, expert_w_up, expert_w_down, final_norm_g):
    b, s, d = x.shape
    depth = ada_w.shape[0]
    n = b * s
    nt = n // TM
    tiles_per_batch = s // TM
    max_tiles = (nt * CHUNKS_PER_TILE + N_EXPERTS * (FFN_CHUNKS - 1)) // FFN_CHUNKS + 1

    tabs = _rope_tables(positions)
    p4 = _residue_perm(TM, 4)
    p16 = _residue_perm(TM, 16)
    p4_b, p16_b = jnp.asarray(p4, BF16), jnp.asarray(p16, BF16)
    p4t_b, p16t_b = jnp.asarray(p4.T, BF16), jnp.asarray(p16.T, BF16)
    tri = jnp.asarray(np.triu(np.ones((TM, TM), np.float32), 1), BF16)

    c_pad = jnp.pad(c, ((0, 8 - b), (0, 0)))
    mod = _ada_call(c_pad, ada_w, ada_b)[:, :b]

    in_perm = _in_col_perm()
    out_perm = _out_row_perm()
    sink_order = np.asarray(_SWA_HEAD_ORDER, np.int32)

    for l in range(depth):
        sh1, sc1, g1, sh2, sc2, g2 = [mod[l, :, k * d:(k + 1) * d].reshape(b, 1, d) for k in range(N_ADA)]
        w_in_l = w_in[l][:, in_perm].astype(BF16)
        w_out_l = w_out[l][out_perm, :].astype(BF16)

        proj, qkv4, qkv16 = _inproj_call(x, sc1, sh1, norm_mix_g[l].reshape(1, d), w_in_l, tabs, p4_b, p16_b)

        lambda_init = 0.8 - 0.6 * math.exp(-0.3 * l)
        lam = (jnp.exp(jnp.sum(diff_lambda_q1[l] * diff_lambda_k1[l]))
               - jnp.exp(jnp.sum(diff_lambda_q2[l] * diff_lambda_k2[l])) + lambda_init).reshape(1)
        oa = _diff_attn_call(proj, lam, jnp.tile(diff_subln_g[l], 2).reshape(1, LANES), lambda_init)

        ob = _band_call(proj, COL_QB, COL_KB, COL_VB, nqb=4, nkb=1, max_dist=SWA_WINDOW - 1,
                        sinks=swa_sinks[l][sink_order], name="swa")[0]
        o1, l1 = _band_call(proj, COL_QC, COL_KC, COL_VC, nqb=2, nkb=2, max_dist=BAND, want_lse=True,
                            name="dil1")
        o4, l4 = _band_call(qkv4.reshape(b * 4, s // 4, 768), 0, 256, 512, nqb=2, nkb=2, max_dist=BAND,
                            want_lse=True, name="dil4")
        o16, l16 = _band_call(qkv16.reshape(b * 16, s // 16, 768), 0, 256, 512, nqb=2, nkb=2, max_dist=BAND,
                              want_lse=True, name="dil16")
        x = _outproj_call(oa, ob, o1, l1,
                          o4.reshape(b, 4, s // 4, 256), l4.reshape(b, 4, s // 4, 512),
                          o16.reshape(b, 16, s // 16, 256), l16.reshape(b, 16, s // 16, 512),
                          p4t_b, p16t_b, w_out_l, x, g1)

        wr = jnp.zeros((d, ROUTER_COLS), F32)
        wr = wr.at[:, 0:N_GROUPS].set(router_group_w[l]).at[:, 8:8 + N_EXPERTS].set(router_expert_w[l])
        br = jnp.zeros((1, ROUTER_COLS), F32).at[0, N_GROUPS:8].set(NEG_INF)
        br = br.at[0, 0:N_GROUPS].set(router_group_b[l]).at[0, 8:8 + N_EXPERTS].set(router_expert_b[l])
        x2d = x.reshape(n, d)
        hs, meta, nch = _router_call(x2d, sc2, sh2, norm_ffn_g[l].reshape(1, d), wr, br, tri, tiles_per_batch)
        tile_expert, chunk_src, n_tiles, zero_chunks = _ffn_schedule(nch[:, :, 0], max_tiles)
        ys = _ffn_call(tile_expert, chunk_src, n_tiles, zero_chunks, hs, expert_w_gate[l].astype(BF16),
                       expert_w_up[l].astype(BF16), expert_w_down[l].astype(BF16), max_tiles)
        x = _combine_call(ys, meta, x2d, g2, final_norm_g.reshape(1, d), tiles_per_batch,
                          final_norm=(l == depth - 1)).reshape(b, s, d)
    return x
```

```python
import functools
import math

import numpy as np
import jax
import jax.numpy as jnp
from jax import lax
from jax.experimental import pallas as pl
from jax.experimental.pallas import tpu as pltpu

F32 = jnp.float32
BF16 = jnp.bfloat16
I32 = jnp.int32

HEAD_DIM = 64
ROPE_THETA = 10000.0
NORM_EPS = 1e-6
NEG_INF = -1e30
DIFF_HEADS = 4
DIFF_QK_DIM = 32
SWA_Q_HEADS = 8
SWA_KV_HEADS = 2
SWA_WINDOW = 128
DIL_PATTERNS = ((128, 1), (512, 4), (2048, 16))
N_GROUPS = 4
EXPERTS_PER_GROUP = 4
N_EXPERTS = 16
EXPERT_FF = 512
N_ADA = 6
IN_WIDTH = 2304

LANES = 128
BF16_ROWS = 16
BAND = 128

TM = 256
CHUNK = BF16_ROWS
SLOTS = 2 * TM + N_EXPERTS * CHUNK
CHUNKS_PER_TILE = SLOTS // CHUNK
FFN_ROWS = 256
FFN_CHUNKS = FFN_ROWS // CHUNK
ZERO_CHUNKS = 6
TQ = 256
BAND_ROWS = 512

_SWA_HEAD_ORDER = (0, 4, 1, 5, 2, 6, 3, 7)
COL_QB, COL_KB, COL_VB = 0, 512, 640
COL_Q1, COL_Q2, COL_K1, COL_K2, COL_VA = 768, 896, 1024, 1152, 1280
COL_QC, COL_KC, COL_VC = 1536, 1792, 2048
_ROPE_KIND = (64, 64, 64, 64, 64, 0, 32, 32, 32, 32, 0, 0, 64, 64, 64, 64, 0, 0)


def _in_col_perm():
    a0, b0, c0 = 0, 768, 1536
    qb = [b0 + h * 64 + j for h in _SWA_HEAD_ORDER for j in range(64)]
    kb = list(range(b0 + 512, b0 + 640))
    vb = list(range(b0 + 640, b0 + 768))
    a = list(range(a0, a0 + 768))
    c = list(range(c0, c0 + 768))
    return np.asarray(qb + kb + vb + a + c, dtype=np.int32)


def _out_row_perm():
    oa = list(range(0, 256))
    ob = [256 + h * 64 + j for h in _SWA_HEAD_ORDER for j in range(64)]
    oc = list(range(768, 1024))
    return np.asarray(oa + ob + oc, dtype=np.int32)


def _residue_perm(tm, d):
    p = np.zeros((tm, tm), np.float32)
    per = tm // d
    for l in range(per):
        for r in range(d):
            p[r * per + l, l * d + r] = 1.0
    return p


def _dot(a, b, **kw):
    return jnp.dot(a, b, preferred_element_type=F32, **kw)


def _dot_nt(a, b):
    return lax.dot_general(a, b, (((1,), (1,)), ((), ())), preferred_element_type=F32)


def _modulated_norm(x, g, sc, sh):
    y = x * lax.rsqrt(jnp.mean(x * x, axis=-1, keepdims=True) + NORM_EPS)
    return (y * g) * (1.0 + sc) + sh


def _ada_kernel(c_ref, w_ref, b_ref, o_ref):
    c = c_ref[...]
    ca = c / (1.0 + jnp.exp(-c))
    o_ref[0] = _dot(ca, w_ref[0], precision=lax.Precision.HIGHEST) + b_ref[0]


def _ada_call(c_pad, ada_w, ada_b):
    depth, d, n = ada_w.shape
    tn = 1536
    return pl.pallas_call(
        _ada_kernel,
        grid=(depth, n // tn),
        in_specs=[
            pl.BlockSpec((c_pad.shape[0], d), lambda l, j: (0, 0)),
            pl.BlockSpec((1, d, tn), lambda l, j: (l, 0, j)),
            pl.BlockSpec((1, 1, tn), lambda l, j: (l, 0, j)),
        ],
        out_specs=pl.BlockSpec((1, c_pad.shape[0], tn), lambda l, j: (l, 0, j)),
        out_shape=jax.ShapeDtypeStruct((depth, c_pad.shape[0], n), F32),
        compiler_params=pltpu.CompilerParams(vmem_limit_bytes=40 * 1024 * 1024),
        name="ada_mod",
    )(c_pad, ada_w, ada_b.reshape(depth, 1, n))


def _rope(t, cos, sin_signed, first_half, half):
    rot = jnp.where(first_half, pltpu.roll(t, LANES - half, 1), pltpu.roll(t, half, 1))
    return t * cos + rot * sin_signed


def _inproj_kernel(x_ref, sc_ref, sh_ref, g_ref, w_ref, cs64_ref, sn64_ref, cs32_ref, sn32_ref,
                   p4_ref, p16_ref, proj_ref, vat_ref, c4_ref, c16_ref):
    h = _modulated_norm(x_ref[0], g_ref[...], sc_ref[0], sh_ref[0])
    hb = h.astype(BF16)
    lane = lax.broadcasted_iota(I32, (1, LANES), 1)
    first64 = (lane % 64) < 32
    first32 = (lane % 32) < 16
    for cb in range(IN_WIDTH // 256):
        acc = _dot(hb, w_ref[:, cb * 256:(cb + 1) * 256])
        if cb == COL_VA // 256:
            vat_ref[0] = acc.T.astype(BF16)
        for half in range(2):
            blk = cb * 2 + half
            t = acc[:, half * LANES:(half + 1) * LANES]
            if _ROPE_KIND[blk] == 64:
                t = _rope(t, cs64_ref[0], sn64_ref[0], first64, 32)
            elif _ROPE_KIND[blk] == 32:
                t = _rope(t, cs32_ref[0], sn32_ref[0], first32, 16)
            proj_ref[0, :, blk * LANES:(blk + 1) * LANES] = t.astype(BF16)
    cc = proj_ref[0, :, COL_QC:]
    tm = cc.shape[0]
    c4 = _dot(p4_ref[...], cc).astype(BF16)
    for r in range(4):
        c4_ref[0, r] = c4[r * (tm // 4):(r + 1) * (tm // 4)]
    c16 = _dot(p16_ref[...], cc).astype(BF16)
    for r in range(16):
        c16_ref[0, r] = c16[r * (tm // 16):(r + 1) * (tm // 16)]


def _inproj_call(x, sc, sh, g, w, tabs, p4, p16):
    b, s, d = x.shape
    tm = TM
    row = lambda bi, i: (bi, i, 0)
    per_b = lambda bi, i: (bi, 0, 0)
    const2 = lambda bi, i: (0, 0)
    return pl.pallas_call(
        _inproj_kernel,
        grid=(b, s // tm),
        in_specs=[
            pl.BlockSpec((1, tm, d), row),
            pl.BlockSpec((1, 1, d), per_b),
            pl.BlockSpec((1, 1, d), per_b),
            pl.BlockSpec((1, d), const2),
            pl.BlockSpec((d, IN_WIDTH), const2),
            pl.BlockSpec((1, tm, LANES), row),
            pl.BlockSpec((1, tm, LANES), row),
            pl.BlockSpec((1, tm, LANES), row),
            pl.BlockSpec((1, tm, LANES), row),
            pl.BlockSpec((tm, tm), const2),
            pl.BlockSpec((tm, tm), const2),
        ],
        out_specs=[
            pl.BlockSpec((1, tm, IN_WIDTH), row),
            pl.BlockSpec((1, 256, tm), lambda bi, i: (bi, 0, i)),
            pl.BlockSpec((1, 4, tm // 4, 768), lambda bi, i: (bi, 0, i, 0)),
            pl.BlockSpec((1, 16, tm // 16, 768), lambda bi, i: (bi, 0, i, 0)),
        ],
        out_shape=[
            jax.ShapeDtypeStruct((b, s, IN_WIDTH), BF16),
            jax.ShapeDtypeStruct((b, 256, s), BF16),
            jax.ShapeDtypeStruct((b, 4, s // 4, 768), BF16),
            jax.ShapeDtypeStruct((b, 16, s // 16, 768), BF16),
        ],
        compiler_params=pltpu.CompilerParams(vmem_limit_bytes=48 * 1024 * 1024),
        name="in_proj",
    )(x, sc, sh, g, w, *tabs, p4, p16)


def _diff_attn_kernel(lam_ref, q1_ref, q2_ref, k1_ref, k2_ref, vt_ref, g_ref, o_ref, m_sc, l_sc, acc_sc, s_sc,
                      *, lambda_init):
    tq = q1_ref.shape[1]
    qi = pl.program_id(1)
    lam = lam_ref[0]
    to_log2 = DIFF_QK_DIM ** -0.5 * math.log2(math.e)
    lane = lax.broadcasted_iota(I32, (1, LANES), 1)
    q1 = q1_ref[0].astype(F32) * to_log2
    q2 = q2_ref[0].astype(F32) * to_log2
    qh = []
    for h in range(DIFF_HEADS):
        hm = (lane // DIFF_QK_DIM) == h
        qh.append((jnp.where(hm, q1, 0.0).astype(BF16), jnp.where(hm, q2, 0.0).astype(BF16)))
    causal = (lax.broadcasted_iota(I32, (tq, tq), 0) <= lax.broadcasted_iota(I32, (tq, tq), 1))

    m_sc[...] = jnp.full(m_sc.shape, NEG_INF, F32)
    l_sc[...] = jnp.zeros(l_sc.shape, F32)
    acc_sc[...] = jnp.zeros(acc_sc.shape, F32)

    n_chain = 2 * DIFF_HEADS

    def scores(ch, tile):
        start = pl.multiple_of(tile * tq, tq)
        k_ref = k1_ref if ch % 2 == 0 else k2_ref
        return _dot_nt(k_ref[0, pl.ds(start, tq), :], qh[ch // 2][ch % 2])

    for ch in range(n_chain):
        s_sc[ch] = scores(ch, 0)

    def step(j, masked):
        start = pl.multiple_of(j * tq, tq)
        for ch in range(n_chain):
            st = s_sc[ch]
            if not masked:
                s_sc[ch] = scores(ch, j + 1)
            h = ch // 2
            vt = vt_ref[0, h * HEAD_DIM:(h + 1) * HEAD_DIM, pl.ds(start, tq)]
            if masked:
                st = jnp.where(causal, st, NEG_INF)
            m_old = m_sc[ch]
            m_new = jnp.maximum(m_old, jnp.max(st, axis=0, keepdims=True))
            p = jnp.exp2(st - m_new)
            al = jnp.exp2(m_old - m_new)
            l_sc[ch] = al * l_sc[ch] + jnp.sum(p, axis=0, keepdims=True)
            acc_sc[ch] = al * acc_sc[ch] + _dot(vt, p.astype(BF16))
            m_sc[ch] = m_new

    def body(j, carry):
        step(j, False)
        return carry

    lax.fori_loop(0, qi, body, 0)
    step(qi, True)

    g = g_ref[...]
    outs = []
    for h in range(DIFF_HEADS):
        o = acc_sc[2 * h] / l_sc[2 * h] - lam * (acc_sc[2 * h + 1] / l_sc[2 * h + 1])
        ms = jnp.mean(o * o, axis=0, keepdims=True)
        outs.append((o * lax.rsqrt(ms + NORM_EPS)) * g * (1.0 - lambda_init))
    o_ref[0] = jnp.concatenate(outs, axis=0).T.astype(BF16)


def _diff_attn_call(proj, vat, lam, g_cols, lambda_init):
    b, s, _ = proj.shape
    tq = TQ
    qspec = lambda cb: pl.BlockSpec((1, tq, LANES), lambda bi, i, cb=cb: (bi, i, cb))
    kspec = lambda cb: pl.BlockSpec((1, s, LANES), lambda bi, i, cb=cb: (bi, 0, cb))
    n_chain = 2 * DIFF_HEADS
    return pl.pallas_call(
        functools.partial(_diff_attn_kernel, lambda_init=lambda_init),
        grid=(b, s // tq),
        in_specs=[
            pl.BlockSpec(memory_space=pltpu.SMEM),
            qspec(COL_Q1 // LANES), qspec(COL_Q2 // LANES),
            kspec(COL_K1 // LANES), kspec(COL_K2 // LANES),
            pl.BlockSpec((1, DIFF_HEADS * HEAD_DIM, s), lambda bi, i: (bi, 0, 0)),
            pl.BlockSpec((HEAD_DIM, tq), lambda bi, i: (0, 0)),
        ],
        out_specs=pl.BlockSpec((1, tq, 256), lambda bi, i: (bi, i, 0)),
        out_shape=jax.ShapeDtypeStruct((b, s, 256), BF16),
        scratch_shapes=[
            pltpu.VMEM((n_chain, 1, tq), F32),
            pltpu.VMEM((n_chain, 1, tq), F32),
            pltpu.VMEM((n_chain, HEAD_DIM, tq), F32),
            pltpu.VMEM((n_chain, tq, tq), F32),
        ],
        compiler_params=pltpu.CompilerParams(vmem_limit_bytes=48 * 1024 * 1024),
        name="diff_attn",
    )(lam, proj, proj, proj, proj, vat, g_cols)


def _band_kernel(*refs, nqb, nkb, max_dist, has_sink, want_lse):
    it = iter(refs)
    sink_ref = next(it) if has_sink else None
    q_ref, kp_ref, kc_ref, vp_ref, vc_ref = (next(it) for _ in range(5))
    o_ref = next(it)
    lse_ref = next(it) if want_lse else None
    rows = q_ref.shape[1]
    i = pl.program_id(1)
    kfull = jnp.concatenate([kp_ref[0], kc_ref[0]], axis=0)
    vfull = jnp.concatenate([vp_ref[0], vc_ref[0]], axis=0)
    lane = lax.broadcasted_iota(I32, (1, LANES), 1)
    lo_half = lane < 64
    r_io = lax.broadcasted_iota(I32, (BAND, 2 * BAND), 0)
    c_io = lax.broadcasted_iota(I32, (BAND, 2 * BAND), 1)
    dist = BAND + r_io - c_io
    band = (dist >= 0) & (dist <= max_dist)
    first_pen = jnp.where(c_io < BAND, jnp.where(i == 0, NEG_INF, 0.0), 0.0)
    qscale = jnp.asarray(HEAD_DIM ** -0.5, BF16)
    for sb in range(rows // BAND):
        r0 = sb * BAND
        for qb in range(nqb):
            kb = qb if nkb > 1 else 0
            q = q_ref[0, r0:r0 + BAND, qb * LANES:(qb + 1) * LANES] * qscale
            kw = kfull[r0:r0 + 2 * BAND, kb * LANES:(kb + 1) * LANES]
            vw = vfull[r0:r0 + 2 * BAND, kb * LANES:(kb + 1) * LANES]
            outs, lses = [], []
            for hh in range(2):
                hm = lo_half if hh == 0 else jnp.logical_not(lo_half)
                qh = jnp.where(hm, q, jnp.zeros_like(q))
                s = _dot_nt(qh, kw)
                s = jnp.where(band, s, NEG_INF)
                if sb == 0:
                    s = s + first_pen
                m = jnp.max(s, axis=1, keepdims=True)
                if has_sink:
                    snk = sink_ref[qb * 2 + hh]
                    m = jnp.maximum(m, snk)
                p = jnp.exp(s - m)
                l = jnp.sum(p, axis=1, keepdims=True)
                pv = _dot(p.astype(BF16), vw)
                if want_lse:
                    lses.append(m + jnp.log(l))
                if has_sink:
                    l = l + jnp.exp(snk - m)
                outs.append(pv / l)
            o = jnp.where(lo_half, outs[0], outs[1])
            o_ref[0, r0:r0 + BAND, qb * LANES:(qb + 1) * LANES] = o.astype(BF16)
            if want_lse:
                ls = jnp.where(lo_half, lses[0], lses[1])
                hi = ls.astype(BF16)
                lo = (ls - hi.astype(F32)).astype(BF16)
                lse_ref[0, r0:r0 + BAND, qb * LANES:(qb + 1) * LANES] = hi
                lse_ref[0, r0:r0 + BAND, (nqb + qb) * LANES:(nqb + qb + 1) * LANES] = lo


def _band_call(arr, q_col, k_col, v_col, nqb, nkb, max_dist, sinks=None, want_lse=False, name="band"):
    ns, length, _ = arr.shape
    rows = min(BAND_ROWS, length)
    wq, wk = nqb * LANES, nkb * LANES
    rpb = rows // BAND
    cur = lambda col, w: pl.BlockSpec((1, rows, w), lambda n, i, c=col // w: (n, i, c))
    prev = lambda col, w: pl.BlockSpec(
        (1, BAND, w), lambda n, i, c=col // w: (n, jnp.maximum(i * rpb - 1, 0), c))
    in_specs = [cur(q_col, wq), prev(k_col, wk), cur(k_col, wk), prev(v_col, wk), cur(v_col, wk)]
    args = [arr] * 5
    if sinks is not None:
        in_specs = [pl.BlockSpec(memory_space=pltpu.SMEM)] + in_specs
        args = [sinks] + args
    out_specs = [pl.BlockSpec((1, rows, wq), lambda n, i: (n, i, 0))]
    out_shape = [jax.ShapeDtypeStruct((ns, length, wq), BF16)]
    if want_lse:
        out_specs.append(pl.BlockSpec((1, rows, 2 * wq), lambda n, i: (n, i, 0)))
        out_shape.append(jax.ShapeDtypeStruct((ns, length, 2 * wq), BF16))
    return pl.pallas_call(
        functools.partial(_band_kernel, nqb=nqb, nkb=nkb, max_dist=max_dist,
                          has_sink=sinks is not None, want_lse=want_lse),
        grid=(ns, length // rows),
        in_specs=in_specs,
        out_specs=out_specs,
        out_shape=out_shape,
        name=name,
    )(*args)


def _outproj_kernel(oa_ref, ob_ref, o1_ref, l1_ref, o4_ref, l4_ref, o16_ref, l16_ref, p4t_ref, p16t_ref,
                    w_ref, x_ref, g1_ref, xo_ref):
    tm = x_ref.shape[1]
    hw = o1_ref.shape[2]

    def lse_of(v):
        return v[:, :hw] + v[:, hw:]

    o1 = o1_ref[0].astype(F32)
    ls1 = lse_of(l1_ref[0].astype(F32))
    o4 = _dot(p4t_ref[...], o4_ref[0].reshape(tm, hw))
    ls4 = lse_of(_dot(p4t_ref[...], l4_ref[0].reshape(tm, 2 * hw)))
    o16 = _dot(p16t_ref[...], o16_ref[0].reshape(tm, hw))
    ls16 = lse_of(_dot(p16t_ref[...], l16_ref[0].reshape(tm, 2 * hw)))
    mx = jnp.maximum(jnp.maximum(ls1, ls4), ls16)
    e1, e4, e16 = jnp.exp(ls1 - mx), jnp.exp(ls4 - mx), jnp.exp(ls16 - mx)
    oc = (e1 * o1 + e4 * o4 + e16 * o16) / (e1 + e4 + e16)
    mix = (_dot(oa_ref[0], w_ref[0:256, :]) + _dot(ob_ref[0], w_ref[256:768, :])
           + _dot(oc.astype(BF16), w_ref[768:1024, :]))
    xo_ref[0] = x_ref[0] + g1_ref[0] * mix


def _outproj_call(oa, ob, o1, l1, o4, l4, o16, l16, p4t, p16t, w, x, g1):
    b, s, d = x.shape
    tm = TM
    row = lambda w_: pl.BlockSpec((1, tm, w_), lambda bi, i: (bi, i, 0))
    res = lambda dd, w_: pl.BlockSpec((1, dd, tm // dd, w_), lambda bi, i: (bi, 0, i, 0))
    const2 = lambda bi, i: (0, 0)
    return pl.pallas_call(
        _outproj_kernel,
        grid=(b, s // tm),
        in_specs=[
            row(256), row(512), row(256), row(512),
            res(4, 256), res(4, 512), res(16, 256), res(16, 512),
            pl.BlockSpec((tm, tm), const2), pl.BlockSpec((tm, tm), const2),
            pl.BlockSpec((d, d), const2),
            row(d),
            pl.BlockSpec((1, 1, d), lambda bi, i: (bi, 0, 0)),
        ],
        out_specs=row(d),
        out_shape=jax.ShapeDtypeStruct((b, s, d), F32),
        compiler_params=pltpu.CompilerParams(vmem_limit_bytes=48 * 1024 * 1024),
        name="out_proj",
    )(oa, ob, o1, l1, o4, l4, o16, l16, p4t, p16t, w, x, g1)


ROUTER_COLS = LANES


def _router_kernel(x_ref, sc_ref, sh_ref, g_ref, wr_ref, br_ref, tri_ref, hs_ref, meta_ref, nch_ref):
    tm = x_ref.shape[1]
    h = _modulated_norm(x_ref[0], g_ref[...], sc_ref[0], sh_ref[0])
    hb = h.astype(BF16)
    logits = _dot(h, wr_ref[...], precision=lax.Precision.HIGHEST) + br_ref[...]
    lt = logits.T
    glog = lt[0:8]
    elog = lt[8:8 + N_EXPERTS]
    r8 = lax.broadcasted_iota(I32, (8, tm), 0)
    r16 = lax.broadcasted_iota(I32, (N_EXPERTS, tm), 0)

    gmax = jnp.max(glog, axis=0, keepdims=True)
    g_w = 1.0 / jnp.sum(jnp.exp(glog - gmax), axis=0, keepdims=True)
    g_idx = jnp.min(jnp.where(glog == gmax, r8, 99), axis=0, keepdims=True)

    el = jnp.where((r16 // EXPERTS_PER_GROUP) == g_idx, elog, NEG_INF)
    emax = jnp.max(el, axis=0, keepdims=True)
    e1 = jnp.min(jnp.where(el == emax, r16, 99), axis=0, keepdims=True)
    el2 = jnp.where(r16 == e1, NEG_INF, el)
    emax2 = jnp.max(el2, axis=0, keepdims=True)
    e2 = jnp.min(jnp.where(el2 == emax2, r16, 99), axis=0, keepdims=True)
    p2 = jnp.exp(emax2 - emax)
    wt1 = g_w / (1.0 + p2)
    wt2 = g_w * p2 / (1.0 + p2)

    oh1 = r16 == e1
    oh2 = r16 == e2
    onehot = jnp.where(oh1, 1.0, 0.0) + jnp.where(oh2, 1.0, 0.0)
    cnt = jnp.sum(onehot, axis=1, keepdims=True)
    nch = jnp.floor((cnt + (CHUNK - 1)) * (1.0 / CHUNK))
    nchb = jnp.broadcast_to(nch, (N_EXPERTS, LANES))
    rl = lax.broadcasted_iota(I32, (N_EXPERTS, LANES), 0)
    incl = nchb
    for sft in (1, 2, 4, 8):
        incl = incl + jnp.where(rl >= sft, pltpu.roll(incl, sft, 0), 0.0)
    off = (incl - nchb)[:, 0:1] * float(CHUNK)
    rank = _dot(onehot.astype(BF16), tri_ref[...])
    slot_of = off + rank
    pos1 = jnp.sum(jnp.where(oh1, slot_of, 0.0), axis=0, keepdims=True)
    pos2 = jnp.sum(jnp.where(oh2, slot_of, 0.0), axis=0, keepdims=True)

    slot = lax.broadcasted_iota(I32, (SLOTS, tm), 0)
    sel = jnp.where(slot == pos1.astype(I32), 1.0, jnp.where(slot == pos2.astype(I32), 1.0, 0.0))
    hs_ref[...] = _dot(sel.astype(BF16), hb).astype(BF16)

    meta_ref[0] = jnp.concatenate([pos1, pos2, wt1, wt2, jnp.zeros((4, tm), F32)], axis=0)
    nch_ref[0] = nchb.astype(I32)


def _router_call(x2d, sc, sh, g, wr, br, tri, tiles_per_batch):
    n, d = x2d.shape
    tm = TM
    nt = n // tm
    x3 = x2d.reshape(nt, tm, d)
    per_b = lambda t: (t // tiles_per_batch, 0, 0)
    return pl.pallas_call(
        _router_kernel,
        grid=(nt,),
        in_specs=[
            pl.BlockSpec((1, tm, d), lambda t: (t, 0, 0)),
            pl.BlockSpec((1, 1, d), per_b),
            pl.BlockSpec((1, 1, d), per_b),
            pl.BlockSpec((1, d), lambda t: (0, 0)),
            pl.BlockSpec((d, ROUTER_COLS), lambda t: (0, 0)),
            pl.BlockSpec((1, ROUTER_COLS), lambda t: (0, 0)),
            pl.BlockSpec((tm, tm), lambda t: (0, 0)),
        ],
        out_specs=[
            pl.BlockSpec((SLOTS, d), lambda t: (t, 0)),
            pl.BlockSpec((1, 8, tm), lambda t: (t, 0, 0)),
            pl.BlockSpec((1, N_EXPERTS, LANES), lambda t: (t, 0, 0)),
        ],
        out_shape=[
            jax.ShapeDtypeStruct((nt * SLOTS, d), BF16),
            jax.ShapeDtypeStruct((nt, 8, tm), F32),
            jax.ShapeDtypeStruct((nt, N_EXPERTS, LANES), I32),
        ],
        compiler_params=pltpu.CompilerParams(vmem_limit_bytes=48 * 1024 * 1024),
        name="router_sort",
    )(x3, sc, sh, g, wr, br, tri)


def _ffn_schedule(nch, max_tiles):
    nt = nch.shape[0]
    cend = jnp.cumsum(nch, axis=1)
    coff = cend - nch
    tot = jnp.sum(nch, axis=0)
    pad = ((tot + FFN_CHUNKS - 1) // FFN_CHUNKS) * FFN_CHUNKS
    eend = jnp.cumsum(pad)
    estart = eend - pad
    before = jnp.cumsum(nch, axis=0) - nch
    c = jnp.arange(CHUNKS_PER_TILE, dtype=I32)[None, :]
    e_of = jnp.sum((cend[:, None, :] <= c[:, :, None]).astype(I32), axis=2)
    valid = e_of < N_EXPERTS
    e_c = jnp.minimum(e_of, N_EXPERTS - 1)
    take = lambda a: jnp.take_along_axis(a, e_c, axis=1)
    dest = estart[e_c] + take(before) + (c - take(coff))
    total = max_tiles * FFN_CHUNKS
    dest = jnp.where(valid, dest, total)
    src = jnp.arange(nt, dtype=I32)[:, None] * CHUNKS_PER_TILE + c
    chunk_src = jnp.full((total,), -1, I32).at[dest.reshape(-1)].set(src.reshape(-1), mode="drop")
    n_tiles = (eend[-1] // FFN_CHUNKS).astype(I32)
    first_chunk = jnp.arange(max_tiles, dtype=I32) * FFN_CHUNKS
    tile_expert = jnp.sum((eend[None, :] <= first_chunk[:, None]).astype(I32), axis=1)
    tile_expert = jnp.minimum(tile_expert, N_EXPERTS - 1)
    unused = jnp.logical_not(valid).reshape(-1)
    zdest = jnp.where(unused, jnp.cumsum(unused.astype(I32)) - 1, max_tiles * ZERO_CHUNKS)
    zero_chunks = jnp.full((max_tiles * ZERO_CHUNKS,), -1, I32).at[zdest].set(src.reshape(-1), mode="drop")
    return tile_expert, chunk_src, n_tiles.reshape(1), zero_chunks


def _ffn_kernel(te_ref, cs_ref, nt_ref, zc_ref, hs_hbm, wg_ref, wu_ref, wd_ref, ys_hbm,
                xbuf, ybuf, zbuf, in_sem, out_sem, zero_sem):
    j = pl.program_id(0)
    nt = nt_ref[0]

    def for_zero_chunks(action):
        for k in range(ZERO_CHUNKS):
            c = zc_ref[j * ZERO_CHUNKS + k]
            dst = ys_hbm.at[pl.ds(pl.multiple_of(jnp.maximum(c, 0) * CHUNK, CHUNK), CHUNK), :]
            cp = pltpu.make_async_copy(zbuf, dst, zero_sem)

            @pl.when(c >= 0)
            def _(cp=cp):
                getattr(cp, action)()

    @pl.when(j == 0)
    def _():
        zbuf[...] = jnp.zeros_like(zbuf)

    for_zero_chunks("start")

    def in_copy(step, slot, k):
        c = cs_ref[step * FFN_CHUNKS + k]
        src = hs_hbm.at[pl.ds(pl.multiple_of(jnp.maximum(c, 0) * CHUNK, CHUNK), CHUNK), :]
        return c, pltpu.make_async_copy(src, xbuf.at[slot, pl.ds(k * CHUNK, CHUNK), :], in_sem.at[slot])

    def out_copy(step, slot, k):
        c = cs_ref[step * FFN_CHUNKS + k]
        dst = ys_hbm.at[pl.ds(pl.multiple_of(jnp.maximum(c, 0) * CHUNK, CHUNK), CHUNK), :]
        return c, pltpu.make_async_copy(ybuf.at[slot, pl.ds(k * CHUNK, CHUNK), :], dst, out_sem.at[slot])

    def for_chunks(make, step, slot, action):
        for k in range(FFN_CHUNKS):
            c, cp = make(step, slot, k)

            @pl.when(c >= 0)
            def _(cp=cp):
                getattr(cp, action)()

    @pl.when(j < nt)
    def _():
        slot = j % 2

        @pl.when(j == 0)
        def _():
            xbuf[...] = jnp.zeros_like(xbuf)
            for_chunks(in_copy, 0, 0, "start")

        @pl.when(j + 1 < nt)
        def _():
            for_chunks(in_copy, j + 1, 1 - slot, "start")

        for_chunks(in_copy, j, slot, "wait")

        @pl.when(j >= 2)
        def _():
            for_chunks(out_copy, j - 2, slot, "wait")

        x = xbuf[slot]
        hg = _dot(x, wg_ref[0])
        hu = _dot(x, wu_ref[0])
        act = (hg / (1.0 + jnp.exp(-hg))) * hu
        ybuf[slot] = _dot(act.astype(BF16), wd_ref[0]).astype(BF16)
        for_chunks(out_copy, j, slot, "start")

        @pl.when(j == nt - 1)
        def _():
            for_chunks(out_copy, j, slot, "wait")

            @pl.when(j >= 1)
            def _():
                for_chunks(out_copy, j - 1, 1 - slot, "wait")

    for_zero_chunks("wait")


def _ffn_call(tile_expert, chunk_src, n_tiles, zero_chunks, hs, wg, wu, wd, max_tiles):
    rows, d = hs.shape
    ff = wg.shape[-1]
    wmap = lambda j, te, cs, nt, zc: (te[j], 0, 0)
    grid_spec = pltpu.PrefetchScalarGridSpec(
        num_scalar_prefetch=4,
        grid=(max_tiles,),
        in_specs=[
            pl.BlockSpec(memory_space=pl.ANY),
            pl.BlockSpec((1, d, ff), wmap),
            pl.BlockSpec((1, d, ff), wmap),
            pl.BlockSpec((1, ff, d), wmap),
        ],
        out_specs=pl.BlockSpec(memory_space=pl.ANY),
        scratch_shapes=[
            pltpu.VMEM((2, FFN_ROWS, d), BF16),
            pltpu.VMEM((2, FFN_ROWS, d), BF16),
            pltpu.VMEM((CHUNK, d), BF16),
            pltpu.SemaphoreType.DMA((2,)),
            pltpu.SemaphoreType.DMA((2,)),
            pltpu.SemaphoreType.DMA(()),
        ],
    )
    return pl.pallas_call(
        _ffn_kernel,
        grid_spec=grid_spec,
        out_shape=jax.ShapeDtypeStruct((rows, d), BF16),
        compiler_params=pltpu.CompilerParams(vmem_limit_bytes=48 * 1024 * 1024),
        name="expert_ffn",
    )(tile_expert, chunk_src, n_tiles, zero_chunks, hs, wg, wu, wd)


def _combine_kernel(ys_ref, meta_ref, x_ref, g2_ref, gf_ref, xo_ref, *, final_norm):
    tm = x_ref.shape[1]
    meta = meta_ref[0]
    eye = (lax.broadcasted_iota(I32, (tm, tm), 0) == lax.broadcasted_iota(I32, (tm, tm), 1))

    def as_col(row):
        return jnp.sum(jnp.where(eye, row, 0.0), axis=1, keepdims=True)

    pos1, pos2 = as_col(meta[0:1]), as_col(meta[1:2])
    w1, w2 = as_col(meta[2:3]), as_col(meta[3:4])
    slot = lax.broadcasted_iota(I32, (tm, SLOTS), 1).astype(F32)
    gate = jnp.where(slot == pos1, w1, 0.0) + jnp.where(slot == pos2, w2, 0.0)
    y = _dot(gate.astype(BF16), ys_ref[...])
    xo = x_ref[0] + g2_ref[0] * y
    if final_norm:
        xo = xo * lax.rsqrt(jnp.mean(xo * xo, axis=-1, keepdims=True) + NORM_EPS) * gf_ref[...]
    xo_ref[0] = xo


def _combine_call(ys, meta, x2d, g2, gf, tiles_per_batch, final_norm):
    n, d = x2d.shape
    tm = TM
    nt = n // tm
    return pl.pallas_call(
        functools.partial(_combine_kernel, final_norm=final_norm),
        grid=(nt,),
        in_specs=[
            pl.BlockSpec((SLOTS, d), lambda t: (t, 0)),
            pl.BlockSpec((1, 8, tm), lambda t: (t, 0, 0)),
            pl.BlockSpec((1, tm, d), lambda t: (t, 0, 0)),
            pl.BlockSpec((1, 1, d), lambda t: (t // tiles_per_batch, 0, 0)),
            pl.BlockSpec((1, d), lambda t: (0, 0)),
        ],
        out_specs=pl.BlockSpec((1, tm, d), lambda t: (t, 0, 0)),
        out_shape=jax.ShapeDtypeStruct((nt, tm, d), F32),
        compiler_params=pltpu.CompilerParams(vmem_limit_bytes=48 * 1024 * 1024),
        name="moe_combine",
    )(ys, meta, x2d.reshape(nt, tm, d), g2, gf)


def _rope_tables(positions):
    pos = positions.astype(F32)[..., None]

    def table(dim):
        inv = ROPE_THETA ** (-jnp.arange(0, dim, 2, dtype=F32) / dim)
        ang = pos * inv
        cos, sin = jnp.cos(ang), jnp.sin(ang)
        reps = LANES // dim
        return (jnp.tile(jnp.concatenate([cos, cos], -1), (1, 1, reps)),
                jnp.tile(jnp.concatenate([-sin, sin], -1), (1, 1, reps)))

    c64, s64 = table(HEAD_DIM)
    c32, s32 = table(DIFF_QK_DIM)
    return c64, s64, c32, s32


def kernel(x, c, positions, ada_w, ada_b, norm_mix_g, norm_ffn_g, w_in, w_out, diff_lambda_q1, diff_lambda_k1,
           diff_lambda_q2, diff_lambda_k2, diff_subln_g, swa_sinks, router_group_w, router_group_b,
           router_expert_w, router_expert_b, expert_w_gate, expert_w_up, expert_w_down, final_norm_g):
    b, s, d = x.shape
    depth = ada_w.shape[0]
    n = b * s
    nt = n // TM
    tiles_per_batch = s // TM
    max_tiles = (nt * CHUNKS_PER_TILE + N_EXPERTS * (FFN_CHUNKS - 1)) // FFN_CHUNKS + 1

    tabs = _rope_tables(positions)
    p4 = _residue_perm(TM, 4)
    p16 = _residue_perm(TM, 16)
    p4_b, p16_b = jnp.asarray(p4, BF16), jnp.asarray(p16, BF16)
    p4t_b, p16t_b = jnp.asarray(p4.T, BF16), jnp.asarray(p16.T, BF16)
    tri = jnp.asarray(np.triu(np.ones((TM, TM), np.float32), 1), BF16)

    c_pad = jnp.pad(c, ((0, 8 - b), (0, 0)))
    mod = _ada_call(c_pad, ada_w, ada_b)[:, :b]

    in_perm = _in_col_perm()
    out_perm = _out_row_perm()
    sink_order = np.asarray(_SWA_HEAD_ORDER, np.int32)

    for l in range(depth):
        sh1, sc1, g1, sh2, sc2, g2 = [mod[l, :, k * d:(k + 1) * d].reshape(b, 1, d) for k in range(N_ADA)]
        w_in_l = w_in[l][:, in_perm].astype(BF16)
        w_out_l = w_out[l][out_perm, :].astype(BF16)

        proj, vat, qkv4, qkv16 = _inproj_call(x, sc1, sh1, norm_mix_g[l].reshape(1, d), w_in_l, tabs,
                                              p4_b, p16_b)

        lambda_init = 0.8 - 0.6 * math.exp(-0.3 * l)
        lam = (jnp.exp(jnp.sum(diff_lambda_q1[l] * diff_lambda_k1[l]))
               - jnp.exp(jnp.sum(diff_lambda_q2[l] * diff_lambda_k2[l])) + lambda_init).reshape(1)
        g_cols = jnp.broadcast_to(diff_subln_g[l][:, None], (HEAD_DIM, TQ))
        oa = _diff_attn_call(proj, vat, lam, g_cols, lambda_init)

        ob = _band_call(proj, COL_QB, COL_KB, COL_VB, nqb=4, nkb=1, max_dist=SWA_WINDOW - 1,
                        sinks=swa_sinks[l][sink_order], name="swa")[0]
        o1, l1 = _band_call(proj, COL_QC, COL_KC, COL_VC, nqb=2, nkb=2, max_dist=BAND, want_lse=True,
                            name="dil1")
        o4, l4 = _band_call(qkv4.reshape(b * 4, s // 4, 768), 0, 256, 512, nqb=2, nkb=2, max_dist=BAND,
                            want_lse=True, name="dil4")
        o16, l16 = _band_call(qkv16.reshape(b * 16, s // 16, 768), 0, 256, 512, nqb=2, nkb=2, max_dist=BAND,
                              want_lse=True, name="dil16")
        x = _outproj_call(oa, ob, o1, l1,
                          o4.reshape(b, 4, s // 4, 256), l4.reshape(b, 4, s // 4, 512),
                          o16.reshape(b, 16, s // 16, 256), l16.reshape(b, 16, s // 16, 512),
                          p4t_b, p16t_b, w_out_l, x, g1)

        wr = jnp.zeros((d, ROUTER_COLS), F32)
        wr = wr.at[:, 0:N_GROUPS].set(router_group_w[l]).at[:, 8:8 + N_EXPERTS].set(router_expert_w[l])
        br = jnp.zeros((1, ROUTER_COLS), F32).at[0, N_GROUPS:8].set(NEG_INF)
        br = br.at[0, 0:N_GROUPS].set(router_group_b[l]).at[0, 8:8 + N_EXPERTS].set(router_expert_b[l])
        x2d = x.reshape(n, d)
        hs, meta, nch = _router_call(x2d, sc2, sh2, norm_ffn_g[l].reshape(1, d), wr, br, tri, tiles_per_batch)
        tile_expert, chunk_src, n_tiles, zero_chunks = _ffn_schedule(nch[:, :, 0], max_tiles)
        ys = _ffn_call(tile_expert, chunk_src, n_tiles, zero_chunks, hs, expert_w_gate[l].astype(BF16),
                       expert_w_up[l].astype(BF16), expert_w_down[l].astype(BF16), max_tiles)
        x = _combine_call(ys, meta, x2d, g2, final_norm_g.reshape(1, d), tiles_per_batch,
                          final_norm=(l == depth - 1)).reshape(b, s, d)
    return x
```

```python
import functools
import math

import numpy as np
import jax
import jax.numpy as jnp
from jax import lax
from jax.experimental import pallas as pl
from jax.experimental.pallas import tpu as pltpu

F32 = jnp.float32
BF16 = jnp.bfloat16
I32 = jnp.int32

HEAD_DIM = 64
ROPE_THETA = 10000.0
NORM_EPS = 1e-6
NEG_INF = -1e30
DIFF_HEADS = 4
DIFF_QK_DIM = 32
SWA_Q_HEADS = 8
SWA_KV_HEADS = 2
SWA_WINDOW = 128
DIL_PATTERNS = ((128, 1), (512, 4), (2048, 16))
N_GROUPS = 4
EXPERTS_PER_GROUP = 4
N_EXPERTS = 16
EXPERT_FF = 512
N_ADA = 6
IN_WIDTH = 2304

LANES = 128
BF16_ROWS = 16
BAND = 128

TM = 256
CHUNK = BF16_ROWS
SLOTS = 2 * TM + N_EXPERTS * CHUNK
CHUNKS_PER_TILE = SLOTS // CHUNK
FFN_ROWS = 256
FFN_CHUNKS = FFN_ROWS // CHUNK
ZERO_CHUNKS = 6
DUMP_CHUNKS = -(-(2 * FFN_CHUNKS + ZERO_CHUNKS) // ZERO_CHUNKS) * ZERO_CHUNKS
TQ = 256
BAND_ROWS = 512

_SWA_HEAD_ORDER = (0, 4, 1, 5, 2, 6, 3, 7)
COL_QB, COL_KB, COL_VB = 0, 512, 640
COL_Q1, COL_Q2, COL_K1, COL_K2, COL_VA = 768, 896, 1024, 1152, 1280
COL_QC, COL_KC, COL_VC = 1536, 1792, 2048
_SRC_ROPE = (32, 32, 32, 32, 0, 0, 64, 64, 64, 64, 64, 0, 64, 64, 64, 64, 0, 0)
_SRC_DEST = (6, 7, 8, 9, 10, 11, None, None, None, None, 4, 5, 12, 13, 14, 15, 16, 17)
_SRC_VA_CHUNK = 2


def _residue_perm(tm, d):
    p = np.zeros((tm, tm), np.float32)
    per = tm // d
    for l in range(per):
        for r in range(d):
            p[r * per + l, l * d + r] = 1.0
    return p


def _dot(a, b, **kw):
    return jnp.dot(a, b, preferred_element_type=F32, **kw)


def _dot_nt(a, b):
    return lax.dot_general(a, b, (((1,), (1,)), ((), ())), preferred_element_type=F32)


def _modulated_norm(x, g, sc, sh):
    y = x * lax.rsqrt(jnp.mean(x * x, axis=-1, keepdims=True) + NORM_EPS)
    return (y * g) * (1.0 + sc) + sh


def _ada_kernel(c_ref, w_ref, b_ref, o_ref):
    c = c_ref[...]
    ca = c / (1.0 + jnp.exp(-c))
    o_ref[0] = _dot(ca, w_ref[0], precision=lax.Precision.HIGHEST) + b_ref[0]


def _ada_call(c_pad, ada_w, ada_b):
    depth, d, n = ada_w.shape
    tn = 1536
    return pl.pallas_call(
        _ada_kernel,
        grid=(depth, n // tn),
        in_specs=[
            pl.BlockSpec((c_pad.shape[0], d), lambda l, j: (0, 0)),
            pl.BlockSpec((1, d, tn), lambda l, j: (l, 0, j)),
            pl.BlockSpec((1, 1, tn), lambda l, j: (l, 0, j)),
        ],
        out_specs=pl.BlockSpec((1, c_pad.shape[0], tn), lambda l, j: (l, 0, j)),
        out_shape=jax.ShapeDtypeStruct((depth, c_pad.shape[0], n), F32),
        compiler_params=pltpu.CompilerParams(vmem_limit_bytes=40 * 1024 * 1024),
        name="ada_mod",
    )(c_pad, ada_w, ada_b.reshape(depth, 1, n))


def _rope(t, cos, sin_signed, first_half, half):
    rot = jnp.where(first_half, pltpu.roll(t, LANES - half, 1), pltpu.roll(t, half, 1))
    return t * cos + rot * sin_signed


def _inproj_kernel(x_ref, sc_ref, sh_ref, g_ref, w_ref, cs64_ref, sn64_ref, cs32_ref, sn32_ref,
                   p4_ref, p16_ref, proj_ref, vat_ref, c4_ref, c16_ref, wb):
    @pl.when((pl.program_id(0) == 0) & (pl.program_id(1) == 0))
    def _():
        wb[...] = w_ref[...].astype(BF16)

    h = _modulated_norm(x_ref[0], g_ref[...], sc_ref[0], sh_ref[0])
    hb = h.astype(BF16)
    lane = lax.broadcasted_iota(I32, (1, LANES), 1)
    first64 = (lane % 64) < 32
    first32 = (lane % 32) < 16
    lo_half = lane < 64
    swa_q = []
    for cb in range(IN_WIDTH // 256):
        acc = _dot(hb, wb[:, cb * 256:(cb + 1) * 256])
        if cb == _SRC_VA_CHUNK:
            vat_ref[0] = acc.T.astype(BF16)
        for half in range(2):
            src = cb * 2 + half
            t = acc[:, half * LANES:(half + 1) * LANES]
            if _SRC_ROPE[src] == 64:
                t = _rope(t, cs64_ref[0], sn64_ref[0], first64, 32)
            elif _SRC_ROPE[src] == 32:
                t = _rope(t, cs32_ref[0], sn32_ref[0], first32, 16)
            dst = _SRC_DEST[src]
            if dst is None:
                swa_q.append(t)
            else:
                proj_ref[0, :, dst * LANES:(dst + 1) * LANES] = t.astype(BF16)
    for jb in range(SWA_Q_HEADS // 2):
        a, c = swa_q[jb // 2], swa_q[2 + jb // 2]
        if jb % 2 == 0:
            blk = jnp.where(lo_half, a, pltpu.roll(c, 64, 1))
        else:
            blk = jnp.where(lo_half, pltpu.roll(a, 64, 1), c)
        proj_ref[0, :, jb * LANES:(jb + 1) * LANES] = blk.astype(BF16)
    cc = proj_ref[0, :, COL_QC:]
    tm = cc.shape[0]
    c4 = _dot(p4_ref[...], cc).astype(BF16)
    for r in range(4):
        c4_ref[0, r] = c4[r * (tm // 4):(r + 1) * (tm // 4)]
    c16 = _dot(p16_ref[...], cc).astype(BF16)
    for r in range(16):
        c16_ref[0, r] = c16[r * (tm // 16):(r + 1) * (tm // 16)]


def _inproj_call(x, sc, sh, g, w_in, layer, tabs, p4, p16):
    b, s, d = x.shape
    tm = TM
    row = lambda bi, i: (bi, i, 0)
    per_b = lambda bi, i: (bi, 0, 0)
    const2 = lambda bi, i: (0, 0)
    return pl.pallas_call(
        _inproj_kernel,
        grid=(b, s // tm),
        in_specs=[
            pl.BlockSpec((1, tm, d), row),
            pl.BlockSpec((1, 1, d), per_b),
            pl.BlockSpec((1, 1, d), per_b),
            pl.BlockSpec((1, d), const2),
            pl.BlockSpec((None, d, IN_WIDTH), lambda bi, i: (layer, 0, 0)),
            pl.BlockSpec((1, tm, LANES), row),
            pl.BlockSpec((1, tm, LANES), row),
            pl.BlockSpec((1, tm, LANES), row),
            pl.BlockSpec((1, tm, LANES), row),
            pl.BlockSpec((tm, tm), const2),
            pl.BlockSpec((tm, tm), const2),
        ],
        out_specs=[
            pl.BlockSpec((1, tm, IN_WIDTH), row),
            pl.BlockSpec((1, 256, tm), lambda bi, i: (bi, 0, i)),
            pl.BlockSpec((1, 4, tm // 4, 768), lambda bi, i: (bi, 0, i, 0)),
            pl.BlockSpec((1, 16, tm // 16, 768), lambda bi, i: (bi, 0, i, 0)),
        ],
        out_shape=[
            jax.ShapeDtypeStruct((b, s, IN_WIDTH), BF16),
            jax.ShapeDtypeStruct((b, 256, s), BF16),
            jax.ShapeDtypeStruct((b, 4, s // 4, 768), BF16),
            jax.ShapeDtypeStruct((b, 16, s // 16, 768), BF16),
        ],
        scratch_shapes=[pltpu.VMEM((d, IN_WIDTH), BF16)],
        compiler_params=pltpu.CompilerParams(vmem_limit_bytes=56 * 1024 * 1024),
        name="in_proj",
    )(x, sc, sh, g, w_in, *tabs, p4, p16)


def _diff_attn_kernel(lam_ref, q1_ref, q2_ref, k1_ref, k2_ref, vt_ref, g_ref, o_ref, m_sc, l_sc, acc_sc, s_sc,
                      *, lambda_init):
    tq = q1_ref.shape[1]
    qi = pl.program_id(1)
    lam = lam_ref[0]
    to_log2 = DIFF_QK_DIM ** -0.5 * math.log2(math.e)
    lane = lax.broadcasted_iota(I32, (1, LANES), 1)
    q1 = q1_ref[0].astype(F32) * to_log2
    q2 = q2_ref[0].astype(F32) * to_log2
    qh = []
    for h in range(DIFF_HEADS):
        hm = (lane // DIFF_QK_DIM) == h
        qh.append((jnp.where(hm, q1, 0.0).astype(BF16), jnp.where(hm, q2, 0.0).astype(BF16)))
    causal = (lax.broadcasted_iota(I32, (tq, tq), 0) <= lax.broadcasted_iota(I32, (tq, tq), 1))

    m_sc[...] = jnp.full(m_sc.shape, NEG_INF, F32)
    l_sc[...] = jnp.zeros(l_sc.shape, F32)
    acc_sc[...] = jnp.zeros(acc_sc.shape, F32)

    n_chain = 2 * DIFF_HEADS

    def scores(ch, tile):
        start = pl.multiple_of(tile * tq, tq)
        k_ref = k1_ref if ch % 2 == 0 else k2_ref
        return _dot_nt(k_ref[0, pl.ds(start, tq), :], qh[ch // 2][ch % 2])

    for ch in range(n_chain):
        s_sc[ch] = scores(ch, 0)

    def step(j, masked):
        start = pl.multiple_of(j * tq, tq)
        for ch in range(n_chain):
            st = s_sc[ch]
            if not masked:
                s_sc[ch] = scores(ch, j + 1)
            h = ch // 2
            vt = vt_ref[0, h * HEAD_DIM:(h + 1) * HEAD_DIM, pl.ds(start, tq)]
            if masked:
                st = jnp.where(causal, st, NEG_INF)
            m_old = m_sc[ch]
            m_new = jnp.maximum(m_old, jnp.max(st, axis=0, keepdims=True))
            p = jnp.exp2(st - m_new)
            al = jnp.exp2(m_old - m_new)
            l_sc[ch] = al * l_sc[ch] + jnp.sum(p, axis=0, keepdims=True)
            acc_sc[ch] = al * acc_sc[ch] + _dot(vt, p.astype(BF16))
            m_sc[ch] = m_new

    def body(j, carry):
        step(j, False)
        return carry

    lax.fori_loop(0, qi, body, 0)
    step(qi, True)

    g = g_ref[...]
    outs = []
    for h in range(DIFF_HEADS):
        o = acc_sc[2 * h] / l_sc[2 * h] - lam * (acc_sc[2 * h + 1] / l_sc[2 * h + 1])
        ms = jnp.mean(o * o, axis=0, keepdims=True)
        outs.append((o * lax.rsqrt(ms + NORM_EPS)) * g * (1.0 - lambda_init))
    o_ref[0] = jnp.concatenate(outs, axis=0).T.astype(BF16)


def _diff_attn_call(proj, vat, lam, g_cols, lambda_init):
    b, s, _ = proj.shape
    tq = TQ
    qspec = lambda cb: pl.BlockSpec((1, tq, LANES), lambda bi, i, cb=cb: (bi, i, cb))
    kspec = lambda cb: pl.BlockSpec((1, s, LANES), lambda bi, i, cb=cb: (bi, 0, cb))
    n_chain = 2 * DIFF_HEADS
    return pl.pallas_call(
        functools.partial(_diff_attn_kernel, lambda_init=lambda_init),
        grid=(b, s // tq),
        in_specs=[
            pl.BlockSpec(memory_space=pltpu.SMEM),
            qspec(COL_Q1 // LANES), qspec(COL_Q2 // LANES),
            kspec(COL_K1 // LANES), kspec(COL_K2 // LANES),
            pl.BlockSpec((1, DIFF_HEADS * HEAD_DIM, s), lambda bi, i: (bi, 0, 0)),
            pl.BlockSpec((HEAD_DIM, tq), lambda bi, i: (0, 0)),
        ],
        out_specs=pl.BlockSpec((1, tq, 256), lambda bi, i: (bi, i, 0)),
        out_shape=jax.ShapeDtypeStruct((b, s, 256), BF16),
        scratch_shapes=[
            pltpu.VMEM((n_chain, 1, tq), F32),
            pltpu.VMEM((n_chain, 1, tq), F32),
            pltpu.VMEM((n_chain, HEAD_DIM, tq), F32),
            pltpu.VMEM((n_chain, tq, tq), F32),
        ],
        compiler_params=pltpu.CompilerParams(vmem_limit_bytes=48 * 1024 * 1024),
        name="diff_attn",
    )(lam, proj, proj, proj, proj, vat, g_cols)


def _band_kernel(*refs, nqb, nkb, max_dist, has_sink, want_lse):
    it = iter(refs)
    sink_ref = next(it) if has_sink else None
    q_ref, kp_ref, kc_ref, vp_ref, vc_ref = (next(it) for _ in range(5))
    o_ref = next(it)
    lse_ref = next(it) if want_lse else None
    kbuf, vbuf = next(it), next(it)
    rows = q_ref.shape[1]
    i = pl.program_id(1)
    kbuf[0:BAND, :] = kp_ref[0]
    kbuf[BAND:, :] = kc_ref[0]
    for kb in range(nkb):
        vbuf[0:BAND, kb * 256:kb * 256 + LANES] = vp_ref[0, :, kb * LANES:(kb + 1) * LANES]
        vbuf[BAND:, kb * 256:kb * 256 + LANES] = vc_ref[0, :, kb * LANES:(kb + 1) * LANES]
        vbuf[:, kb * 256 + LANES:(kb + 1) * 256] = jnp.ones((BAND + rows, LANES), BF16)
    lane = lax.broadcasted_iota(I32, (1, LANES), 1)
    lo_half = lane < 64
    r_io = lax.broadcasted_iota(I32, (BAND, 2 * BAND), 0)
    c_io = lax.broadcasted_iota(I32, (BAND, 2 * BAND), 1)
    dist = BAND + r_io - c_io
    band = (dist >= 0) & (dist <= max_dist)
    band_first = band & ((c_io >= BAND) | (i > 0))
    col0 = lax.broadcasted_iota(I32, (1, 2 * BAND), 1) == 0
    vr = lax.broadcasted_iota(I32, (2 * BAND, 2 * LANES), 0)
    vc = lax.broadcasted_iota(I32, (2 * BAND, 2 * LANES), 1)
    sink_row = (vr == 0) & (vc < LANES)
    to_log2 = HEAD_DIM ** -0.5 * math.log2(math.e)
    units = [(sb, qb) for sb in range(rows // BAND) for qb in range(nqb)]

    def scores(u):
        sb, qb = units[u]
        kb = qb if nkb > 1 else 0
        q = q_ref[0, sb * BAND:(sb + 1) * BAND, qb * LANES:(qb + 1) * LANES].astype(F32) * to_log2
        q2 = jnp.concatenate([jnp.where(lo_half, q, 0.0), jnp.where(lo_half, 0.0, q)], axis=0).astype(BF16)
        return _dot_nt(q2, kbuf[sb * BAND:(sb + 2) * BAND, kb * LANES:(kb + 1) * LANES])

    ahead = 2
    pending = [scores(u) for u in range(min(ahead, len(units)))]
    for u, (sb, qb) in enumerate(units):
        if u + ahead < len(units):
            pending.append(scores(u + ahead))
        s2 = pending[u]
        pending[u] = None
        kb = qb if nkb > 1 else 0
        r0 = sb * BAND
        msk = band_first if sb == 0 else band
        halves = []
        for hh in range(2):
            if has_sink:
                fill = jnp.where(col0, sink_ref[qb * 2 + hh] * math.log2(math.e), NEG_INF)
            else:
                fill = NEG_INF
            halves.append(jnp.where(msk, s2[hh * BAND:(hh + 1) * BAND], fill))
        s2 = jnp.concatenate(halves, axis=0)
        m = jnp.max(s2, axis=1, keepdims=True)
        p = jnp.exp2(s2 - m).astype(BF16)
        vw = vbuf[r0:r0 + 2 * BAND, kb * 256:(kb + 1) * 256]
        if has_sink:
            vw = jnp.where(sink_row, jnp.zeros_like(vw), vw)
        pv = _dot(p, vw)
        den = pv[:, LANES:]
        out = pv[:, :LANES] / den
        o = jnp.where(lo_half, out[:BAND], out[BAND:])
        o_ref[0, r0:r0 + BAND, qb * LANES:(qb + 1) * LANES] = o.astype(BF16)
        if want_lse:
            lse2 = m + jnp.log2(den)
            ls = jnp.where(lo_half, lse2[:BAND], lse2[BAND:])
            hi = ls.astype(BF16)
            lo = (ls - hi.astype(F32)).astype(BF16)
            lse_ref[0, r0:r0 + BAND, qb * LANES:(qb + 1) * LANES] = hi
            lse_ref[0, r0:r0 + BAND, (nqb + qb) * LANES:(nqb + qb + 1) * LANES] = lo


def _band_call(arr, q_col, k_col, v_col, nqb, nkb, max_dist, sinks=None, want_lse=False, name="band"):
    ns, length, _ = arr.shape
    rows = min(BAND_ROWS, length)
    wq, wk = nqb * LANES, nkb * LANES
    rpb = rows // BAND
    cur = lambda col, w: pl.BlockSpec((1, rows, w), lambda n, i, c=col // w: (n, i, c))
    prev = lambda col, w: pl.BlockSpec(
        (1, BAND, w), lambda n, i, c=col // w: (n, jnp.maximum(i * rpb - 1, 0), c))
    in_specs = [cur(q_col, wq), prev(k_col, wk), cur(k_col, wk), prev(v_col, wk), cur(v_col, wk)]
    args = [arr] * 5
    if sinks is not None:
        in_specs = [pl.BlockSpec(memory_space=pltpu.SMEM)] + in_specs
        args = [sinks] + args
    out_specs = [pl.BlockSpec((1, rows, wq), lambda n, i: (n, i, 0))]
    out_shape = [jax.ShapeDtypeStruct((ns, length, wq), BF16)]
    if want_lse:
        out_specs.append(pl.BlockSpec((1, rows, 2 * wq), lambda n, i: (n, i, 0)))
        out_shape.append(jax.ShapeDtypeStruct((ns, length, 2 * wq), BF16))
    return pl.pallas_call(
        functools.partial(_band_kernel, nqb=nqb, nkb=nkb, max_dist=max_dist,
                          has_sink=sinks is not None, want_lse=want_lse),
        grid=(ns, length // rows),
        in_specs=in_specs,
        out_specs=out_specs,
        out_shape=out_shape,
        scratch_shapes=[pltpu.VMEM((BAND + rows, wk), BF16), pltpu.VMEM((BAND + rows, 2 * wk), BF16)],
        name=name,
    )(*args)


def _outproj_kernel(oa_ref, ob_ref, o1_ref, l1_ref, o4_ref, l4_ref, o16_ref, l16_ref, p4t_ref, p16t_ref,
                    w_ref, x_ref, g1_ref, xo_ref, wb):
    @pl.when((pl.program_id(0) == 0) & (pl.program_id(1) == 0))
    def _():
        wb[0:256, :] = w_ref[0:256, :].astype(BF16)
        for pos, head in enumerate(_SWA_HEAD_ORDER):
            wb[256 + pos * HEAD_DIM:256 + (pos + 1) * HEAD_DIM, :] = (
                w_ref[256 + head * HEAD_DIM:256 + (head + 1) * HEAD_DIM, :].astype(BF16))
        wb[768:1024, :] = w_ref[768:1024, :].astype(BF16)

    tm = x_ref.shape[1]
    hw = o1_ref.shape[2]

    def lse_of(v):
        return v[:, :hw] + v[:, hw:]

    o1 = o1_ref[0].astype(F32)
    ls1 = lse_of(l1_ref[0].astype(F32))
    o4 = _dot(p4t_ref[...], o4_ref[0].reshape(tm, hw))
    ls4 = lse_of(_dot(p4t_ref[...], l4_ref[0].reshape(tm, 2 * hw)))
    o16 = _dot(p16t_ref[...], o16_ref[0].reshape(tm, hw))
    ls16 = lse_of(_dot(p16t_ref[...], l16_ref[0].reshape(tm, 2 * hw)))
    mx = jnp.maximum(jnp.maximum(ls1, ls4), ls16)
    e1, e4, e16 = jnp.exp2(ls1 - mx), jnp.exp2(ls4 - mx), jnp.exp2(ls16 - mx)
    oc = (e1 * o1 + e4 * o4 + e16 * o16) / (e1 + e4 + e16)
    mix = (_dot(oa_ref[0], wb[0:256, :]) + _dot(ob_ref[0], wb[256:768, :])
           + _dot(oc.astype(BF16), wb[768:1024, :]))
    xo_ref[0] = x_ref[0] + g1_ref[0] * mix


def _outproj_call(oa, ob, o1, l1, o4, l4, o16, l16, p4t, p16t, w_out, layer, x, g1):
    b, s, d = x.shape
    tm = TM
    row = lambda w_: pl.BlockSpec((1, tm, w_), lambda bi, i: (bi, i, 0))
    res = lambda dd, w_: pl.BlockSpec((1, dd, tm // dd, w_), lambda bi, i: (bi, 0, i, 0))
    const2 = lambda bi, i: (0, 0)
    return pl.pallas_call(
        _outproj_kernel,
        grid=(b, s // tm),
        in_specs=[
            row(256), row(512), row(256), row(512),
            res(4, 256), res(4, 512), res(16, 256), res(16, 512),
            pl.BlockSpec((tm, tm), const2), pl.BlockSpec((tm, tm), const2),
            pl.BlockSpec((None, d, d), lambda bi, i: (layer, 0, 0)),
            row(d),
            pl.BlockSpec((1, 1, d), lambda bi, i: (bi, 0, 0)),
        ],
        out_specs=row(d),
        out_shape=jax.ShapeDtypeStruct((b, s, d), F32),
        scratch_shapes=[pltpu.VMEM((d, d), BF16)],
        compiler_params=pltpu.CompilerParams(vmem_limit_bytes=48 * 1024 * 1024),
        name="out_proj",
    )(oa, ob, o1, l1, o4, l4, o16, l16, p4t, p16t, w_out, x, g1)


ROUTER_COLS = LANES


def _router_kernel(x_ref, sc_ref, sh_ref, g_ref, wr_ref, br_ref, tri_ref, hs_ref, meta_ref, nch_ref):
    tm = x_ref.shape[1]
    h = _modulated_norm(x_ref[0], g_ref[...], sc_ref[0], sh_ref[0])
    hb = h.astype(BF16)
    logits = _dot(h, wr_ref[...], precision=lax.Precision.HIGHEST) + br_ref[...]
    lt = logits.T
    glog = lt[0:8]
    elog = lt[8:8 + N_EXPERTS]
    r8 = lax.broadcasted_iota(I32, (8, tm), 0)
    r16 = lax.broadcasted_iota(I32, (N_EXPERTS, tm), 0)

    gmax = jnp.max(glog, axis=0, keepdims=True)
    g_w = 1.0 / jnp.sum(jnp.exp(glog - gmax), axis=0, keepdims=True)
    g_idx = jnp.min(jnp.where(glog == gmax, r8, 99), axis=0, keepdims=True)

    el = jnp.where((r16 // EXPERTS_PER_GROUP) == g_idx, elog, NEG_INF)
    emax = jnp.max(el, axis=0, keepdims=True)
    e1 = jnp.min(jnp.where(el == emax, r16, 99), axis=0, keepdims=True)
    el2 = jnp.where(r16 == e1, NEG_INF, el)
    emax2 = jnp.max(el2, axis=0, keepdims=True)
    e2 = jnp.min(jnp.where(el2 == emax2, r16, 99), axis=0, keepdims=True)
    p2 = jnp.exp(emax2 - emax)
    wt1 = g_w / (1.0 + p2)
    wt2 = g_w * p2 / (1.0 + p2)

    oh1 = r16 == e1
    oh2 = r16 == e2
    onehot = jnp.where(oh1, 1.0, 0.0) + jnp.where(oh2, 1.0, 0.0)
    cnt = jnp.sum(onehot, axis=1, keepdims=True)
    nch = jnp.floor((cnt + (CHUNK - 1)) * (1.0 / CHUNK))
    nchb = jnp.broadcast_to(nch, (N_EXPERTS, LANES))
    rl = lax.broadcasted_iota(I32, (N_EXPERTS, LANES), 0)
    incl = nchb
    for sft in (1, 2, 4, 8):
        incl = incl + jnp.where(rl >= sft, pltpu.roll(incl, sft, 0), 0.0)
    off = (incl - nchb)[:, 0:1] * float(CHUNK)
    rank = _dot(onehot.astype(BF16), tri_ref[...])
    slot_of = off + rank
    pos1 = jnp.sum(jnp.where(oh1, slot_of, 0.0), axis=0, keepdims=True)
    pos2 = jnp.sum(jnp.where(oh2, slot_of, 0.0), axis=0, keepdims=True)

    slot = lax.broadcasted_iota(I32, (SLOTS, tm), 0)
    sel = jnp.where(slot == pos1.astype(I32), 1.0, jnp.where(slot == pos2.astype(I32), 1.0, 0.0))
    hs_ref[...] = _dot(sel.astype(BF16), hb).astype(BF16)

    meta_ref[0] = jnp.concatenate([pos1, pos2, wt1, wt2, jnp.zeros((4, tm), F32)], axis=0)
    nch_ref[0] = nchb.astype(I32)


def _router_call(x2d, sc, sh, g, wr, br, layer, tri, tiles_per_batch):
    n, d = x2d.shape
    tm = TM
    nt = n // tm
    x3 = x2d.reshape(nt, tm, d)
    per_b = lambda t: (t // tiles_per_batch, 0, 0)
    return pl.pallas_call(
        _router_kernel,
        grid=(nt,),
        in_specs=[
            pl.BlockSpec((1, tm, d), lambda t: (t, 0, 0)),
            pl.BlockSpec((1, 1, d), per_b),
            pl.BlockSpec((1, 1, d), per_b),
            pl.BlockSpec((1, d), lambda t: (0, 0)),
            pl.BlockSpec((None, d, ROUTER_COLS), lambda t: (layer, 0, 0)),
            pl.BlockSpec((None, 1, ROUTER_COLS), lambda t: (layer, 0, 0)),
            pl.BlockSpec((tm, tm), lambda t: (0, 0)),
        ],
        out_specs=[
            pl.BlockSpec((SLOTS, d), lambda t: (t, 0)),
            pl.BlockSpec((1, 8, tm), lambda t: (t, 0, 0)),
            pl.BlockSpec((1, N_EXPERTS, LANES), lambda t: (t, 0, 0)),
        ],
        out_shape=[
            jax.ShapeDtypeStruct((nt * SLOTS, d), BF16),
            jax.ShapeDtypeStruct((nt, 8, tm), F32),
            jax.ShapeDtypeStruct((nt, N_EXPERTS, LANES), I32),
        ],
        compiler_params=pltpu.CompilerParams(vmem_limit_bytes=48 * 1024 * 1024),
        name="router_sort",
    )(x3, sc, sh, g, wr, br, tri)


def _ffn_schedule(nch, max_tiles):
    nt = nch.shape[0]
    cend = jnp.cumsum(nch, axis=1)
    coff = cend - nch
    tcum = jnp.cumsum(nch, axis=0)
    before = tcum - nch
    tot = tcum[-1]
    pad = ((tot + FFN_CHUNKS - 1) // FFN_CHUNKS) * FFN_CHUNKS
    eend = jnp.cumsum(pad)
    estart = eend - pad
    n_tiles = (eend[-1] // FFN_CHUNKS).astype(I32)
    first_chunk = jnp.arange(max_tiles, dtype=I32) * FFN_CHUNKS
    tile_expert = jnp.sum((eend[None, :] <= first_chunk[:, None]).astype(I32), axis=1)
    tile_expert = jnp.minimum(tile_expert, N_EXPERTS - 1)
    pos = jnp.arange(max_tiles * FFN_CHUNKS, dtype=I32)
    e_s = jnp.repeat(tile_expert, FFN_CHUNKS)
    idx = pos - estart[e_s]
    real = (idx >= 0) & (idx < tot[e_s])
    t_s = jnp.sum((tcum.T[e_s] <= idx[:, None]).astype(I32), axis=1)
    t_s = jnp.minimum(t_s, nt - 1)
    flat = t_s * N_EXPERTS + e_s
    c_s = coff.reshape(-1)[flat] + idx - before.reshape(-1)[flat]
    chunk_src = jnp.where(real, t_s * CHUNKS_PER_TILE + c_s, -1)
    used = cend[:, -1]
    ucum = jnp.cumsum(CHUNKS_PER_TILE - used)
    z = jnp.arange(max_tiles * ZERO_CHUNKS, dtype=I32)
    tz = jnp.minimum(jnp.sum((ucum[None, :] <= z[:, None]).astype(I32), axis=1), nt - 1)
    cz = used[tz] + z - (ucum[tz] - (CHUNKS_PER_TILE - used[tz]))
    zero_chunks = jnp.where(z < ucum[-1], tz * CHUNKS_PER_TILE + cz, -1)
    return tile_expert, chunk_src, n_tiles.reshape(1), zero_chunks


def _ffn_kernel(te_ref, cs_ref, nt_ref, zc_ref, hs_hbm, wg_ref, wu_ref, wd_ref, ys_hbm,
                xbuf, ybuf, zbuf, wgb, wub, wdb, in_sem, out_sem, zero_sem, *, dump_base):
    j = pl.program_id(0)
    nt = nt_ref[0]
    half_ff = EXPERT_FF // 2

    def rows_of(c):
        return pl.ds(pl.multiple_of(c * CHUNK, CHUNK), CHUNK)

    def in_copy(step, slot, k):
        c = jnp.maximum(cs_ref[step * FFN_CHUNKS + k], 0)
        return pltpu.make_async_copy(hs_hbm.at[rows_of(c), :], xbuf.at[slot, pl.ds(k * CHUNK, CHUNK), :],
                                     in_sem.at[slot])

    def out_copy(step, slot, k):
        c = cs_ref[step * FFN_CHUNKS + k]
        c = jnp.where(c >= 0, c, dump_base + slot * FFN_CHUNKS + k)
        return pltpu.make_async_copy(ybuf.at[slot, pl.ds(k * CHUNK, CHUNK), :], ys_hbm.at[rows_of(c), :],
                                     out_sem.at[slot])

    def zero_copy(k):
        c = zc_ref[j * ZERO_CHUNKS + k]
        c = jnp.where(c >= 0, c, dump_base + 2 * FFN_CHUNKS + k)
        return pltpu.make_async_copy(zbuf.at[pl.ds(k * CHUNK, CHUNK), :], ys_hbm.at[rows_of(c), :], zero_sem)

    @pl.when(j == 0)
    def _():
        zbuf[...] = jnp.zeros_like(zbuf)
        fills = [pltpu.make_async_copy(
            zbuf, ys_hbm.at[pl.ds((dump_base + r * ZERO_CHUNKS) * CHUNK, ZERO_CHUNKS * CHUNK), :], zero_sem)
            for r in range(DUMP_CHUNKS // ZERO_CHUNKS)]
        for cp in fills:
            cp.start()
        for cp in fills:
            cp.wait()

    for k in range(ZERO_CHUNKS):
        zero_copy(k).start()

    @pl.when(j < nt)
    def _():
        slot = j % 2

        @pl.when(j == 0)
        def _():
            for k in range(FFN_CHUNKS):
                in_copy(0, 0, k).start()

        @pl.when(j + 1 < nt)
        def _():
            for k in range(FFN_CHUNKS):
                in_copy(j + 1, 1 - slot, k).start()

        @pl.when((j == 0) | (te_ref[j] != te_ref[jnp.maximum(j - 1, 0)]))
        def _():
            wgb[...] = wg_ref[0].astype(BF16)
            wub[...] = wu_ref[0].astype(BF16)
            wdb[...] = wd_ref[0].astype(BF16)

        for k in range(FFN_CHUNKS):
            in_copy(j, slot, k).wait()

        @pl.when(j >= 2)
        def _():
            for k in range(FFN_CHUNKS):
                out_copy(j - 2, slot, k).wait()

        x = xbuf[slot]
        hg = [_dot(x, wgb[:, h * half_ff:(h + 1) * half_ff]) for h in range(2)]
        hu = [_dot(x, wub[:, h * half_ff:(h + 1) * half_ff]) for h in range(2)]
        y = None
        for h in range(2):
            act = ((hg[h] / (1.0 + jnp.exp(-hg[h]))) * hu[h]).astype(BF16)
            part = _dot(act, wdb[h * half_ff:(h + 1) * half_ff, :])
            y = part if y is None else y + part
        ybuf[slot] = y.astype(BF16)
        for k in range(FFN_CHUNKS):
            out_copy(j, slot, k).start()

        @pl.when(j == nt - 1)
        def _():
            for k in range(FFN_CHUNKS):
                out_copy(j, slot, k).wait()

            @pl.when(j >= 1)
            def _():
                for k in range(FFN_CHUNKS):
                    out_copy(j - 1, 1 - slot, k).wait()

    for k in range(ZERO_CHUNKS):
        zero_copy(k).wait()


def _ffn_call(tile_expert, chunk_src, n_tiles, zero_chunks, hs, wg, wu, wd, layer, max_tiles):
    rows, d = hs.shape
    ff = wg.shape[-1]
    wmap = lambda j, te, cs, nt, zc: (layer, te[j], 0, 0)
    grid_spec = pltpu.PrefetchScalarGridSpec(
        num_scalar_prefetch=4,
        grid=(max_tiles,),
        in_specs=[
            pl.BlockSpec(memory_space=pl.ANY),
            pl.BlockSpec((None, 1, d, ff), wmap),
            pl.BlockSpec((None, 1, d, ff), wmap),
            pl.BlockSpec((None, 1, ff, d), wmap),
        ],
        out_specs=pl.BlockSpec(memory_space=pl.ANY),
        scratch_shapes=[
            pltpu.VMEM((2, FFN_ROWS, d), BF16),
            pltpu.VMEM((2, FFN_ROWS, d), BF16),
            pltpu.VMEM((ZERO_CHUNKS * CHUNK, d), BF16),
            pltpu.VMEM((d, ff), BF16),
            pltpu.VMEM((d, ff), BF16),
            pltpu.VMEM((ff, d), BF16),
            pltpu.SemaphoreType.DMA((2,)),
            pltpu.SemaphoreType.DMA((2,)),
            pltpu.SemaphoreType.DMA(()),
        ],
    )
    return pl.pallas_call(
        functools.partial(_ffn_kernel, dump_base=rows // CHUNK),
        grid_spec=grid_spec,
        out_shape=jax.ShapeDtypeStruct((rows + DUMP_CHUNKS * CHUNK, d), BF16),
        compiler_params=pltpu.CompilerParams(vmem_limit_bytes=48 * 1024 * 1024),
        name="expert_ffn",
    )(tile_expert, chunk_src, n_tiles, zero_chunks, hs, wg, wu, wd)


def _combine_kernel(ys_ref, meta_ref, x_ref, g2_ref, gf_ref, xo_ref, *, final_norm):
    tm = x_ref.shape[1]
    meta = meta_ref[0]
    eye = (lax.broadcasted_iota(I32, (tm, tm), 0) == lax.broadcasted_iota(I32, (tm, tm), 1))

    def as_col(row):
        return jnp.sum(jnp.where(eye, row, 0.0), axis=1, keepdims=True)

    pos1, pos2 = as_col(meta[0:1]), as_col(meta[1:2])
    w1, w2 = as_col(meta[2:3]), as_col(meta[3:4])
    slot = lax.broadcasted_iota(I32, (tm, SLOTS), 1).astype(F32)
    gate = jnp.where(slot == pos1, w1, 0.0) + jnp.where(slot == pos2, w2, 0.0)
    y = _dot(gate.astype(BF16), ys_ref[...])
    xo = x_ref[0] + g2_ref[0] * y
    if final_norm:
        xo = xo * lax.rsqrt(jnp.mean(xo * xo, axis=-1, keepdims=True) + NORM_EPS) * gf_ref[...]
    xo_ref[0] = xo


def _combine_call(ys, meta, x2d, g2, gf, tiles_per_batch, final_norm):
    n, d = x2d.shape
    tm = TM
    nt = n // tm
    return pl.pallas_call(
        functools.partial(_combine_kernel, final_norm=final_norm),
        grid=(nt,),
        in_specs=[
            pl.BlockSpec((SLOTS, d), lambda t: (t, 0)),
            pl.BlockSpec((1, 8, tm), lambda t: (t, 0, 0)),
            pl.BlockSpec((1, tm, d), lambda t: (t, 0, 0)),
            pl.BlockSpec((1, 1, d), lambda t: (t // tiles_per_batch, 0, 0)),
            pl.BlockSpec((1, d), lambda t: (0, 0)),
        ],
        out_specs=pl.BlockSpec((1, tm, d), lambda t: (t, 0, 0)),
        out_shape=jax.ShapeDtypeStruct((nt, tm, d), F32),
        compiler_params=pltpu.CompilerParams(vmem_limit_bytes=48 * 1024 * 1024),
        name="moe_combine",
    )(ys, meta, x2d.reshape(nt, tm, d), g2, gf)


def _rope_tables(positions):
    pos = positions.astype(F32)[..., None]

    def table(dim):
        inv = ROPE_THETA ** (-jnp.arange(0, dim, 2, dtype=F32) / dim)
        ang = pos * inv
        cos, sin = jnp.cos(ang), jnp.sin(ang)
        reps = LANES // dim
        return (jnp.tile(jnp.concatenate([cos, cos], -1), (1, 1, reps)),
                jnp.tile(jnp.concatenate([-sin, sin], -1), (1, 1, reps)))

    c64, s64 = table(HEAD_DIM)
    c32, s32 = table(DIFF_QK_DIM)
    return c64, s64, c32, s32


def kernel(x, c, positions, ada_w, ada_b, norm_mix_g, norm_ffn_g, w_in, w_out, diff_lambda_q1, diff_lambda_k1,
           diff_lambda_q2, diff_lambda_k2, diff_subln_g, swa_sinks, router_group_w, router_group_b,
           router_expert_w, router_expert_b, expert_w_gate, expert_w_up, expert_w_down, final_norm_g):
    b, s, d = x.shape
    depth = ada_w.shape[0]
    n = b * s
    nt = n // TM
    tiles_per_batch = s // TM
    max_tiles = (nt * CHUNKS_PER_TILE + N_EXPERTS * (FFN_CHUNKS - 1)) // FFN_CHUNKS + 1

    tabs = _rope_tables(positions)
    p4 = _residue_perm(TM, 4)
    p16 = _residue_perm(TM, 16)
    p4_b, p16_b = jnp.asarray(p4, BF16), jnp.asarray(p16, BF16)
    p4t_b, p16t_b = jnp.asarray(p4.T, BF16), jnp.asarray(p16.T, BF16)
    tri = jnp.asarray(np.triu(np.ones((TM, TM), np.float32), 1), BF16)

    c_pad = jnp.pad(c, ((0, 8 - b), (0, 0)))
    mod = _ada_call(c_pad, ada_w, ada_b)[:, :b]

    sink_order = np.asarray(_SWA_HEAD_ORDER, np.int32)
    zpad = lambda k: jnp.zeros((depth, d, k), F32)
    wr = jnp.concatenate([router_group_w, zpad(8 - N_GROUPS), router_expert_w,
                          zpad(ROUTER_COLS - 8 - N_EXPERTS)], axis=-1)
    br = jnp.concatenate([router_group_b, jnp.full((depth, 8 - N_GROUPS), NEG_INF, F32), router_expert_b,
                          jnp.zeros((depth, ROUTER_COLS - 8 - N_EXPERTS), F32)], axis=-1).reshape(depth, 1, -1)

    for l in range(depth):
        sh1, sc1, g1, sh2, sc2, g2 = [mod[l, :, k * d:(k + 1) * d].reshape(b, 1, d) for k in range(N_ADA)]
        proj, vat, qkv4, qkv16 = _inproj_call(x, sc1, sh1, norm_mix_g[l].reshape(1, d), w_in, l, tabs,
                                              p4_b, p16_b)

        lambda_init = 0.8 - 0.6 * math.exp(-0.3 * l)
        lam = (jnp.exp(jnp.sum(diff_lambda_q1[l] * diff_lambda_k1[l]))
               - jnp.exp(jnp.sum(diff_lambda_q2[l] * diff_lambda_k2[l])) + lambda_init).reshape(1)
        g_cols = jnp.broadcast_to(diff_subln_g[l][:, None], (HEAD_DIM, TQ))
        oa = _diff_attn_call(proj, vat, lam, g_cols, lambda_init)

        ob = _band_call(proj, COL_QB, COL_KB, COL_VB, nqb=4, nkb=1, max_dist=SWA_WINDOW - 1,
                        sinks=swa_sinks[l][sink_order], name="swa")[0]
        o1, l1 = _band_call(proj, COL_QC, COL_KC, COL_VC, nqb=2, nkb=2, max_dist=BAND, want_lse=True,
                            name="dil1")
        o4, l4 = _band_call(qkv4.reshape(b * 4, s // 4, 768), 0, 256, 512, nqb=2, nkb=2, max_dist=BAND,
                            want_lse=True, name="dil4")
        o16, l16 = _band_call(qkv16.reshape(b * 16, s // 16, 768), 0, 256, 512, nqb=2, nkb=2, max_dist=BAND,
                              want_lse=True, name="dil16")
        x = _outproj_call(oa, ob, o1, l1,
                          o4.reshape(b, 4, s // 4, 256), l4.reshape(b, 4, s // 4, 512),
                          o16.reshape(b, 16, s // 16, 256), l16.reshape(b, 16, s // 16, 512),
                          p4t_b, p16t_b, w_out, l, x, g1)

        x2d = x.reshape(n, d)
        hs, meta, nch = _router_call(x2d, sc2, sh2, norm_ffn_g[l].reshape(1, d), wr, br, l, tri,
                                     tiles_per_batch)
        tile_expert, chunk_src, n_tiles, zero_chunks = _ffn_schedule(nch[:, :, 0], max_tiles)
        ys = _ffn_call(tile_expert, chunk_src, n_tiles, zero_chunks, hs, expert_w_gate, expert_w_up,
                       expert_w_down, l, max_tiles)
        x = _combine_call(ys, meta, x2d, g2, final_norm_g.reshape(1, d), tiles_per_batch,
                          final_norm=(l == depth - 1)).reshape(b, s, d)
    return x
```

```python
import functools
import math

import numpy as np
import jax
import jax.numpy as jnp
from jax import lax
from jax.experimental import pallas as pl
from jax.experimental.pallas import tpu as pltpu

F32 = jnp.float32
BF16 = jnp.bfloat16
I32 = jnp.int32

HEAD_DIM = 64
ROPE_THETA = 10000.0
NORM_EPS = 1e-6
NEG_INF = -1e30
DIFF_HEADS = 4
DIFF_QK_DIM = 32
SWA_Q_HEADS = 8
SWA_KV_HEADS = 2
SWA_WINDOW = 128
DIL_PATTERNS = ((128, 1), (512, 4), (2048, 16))
N_GROUPS = 4
EXPERTS_PER_GROUP = 4
N_EXPERTS = 16
EXPERT_FF = 512
N_ADA = 6
IN_WIDTH = 2304

LANES = 128
BF16_ROWS = 16
BAND = 128

TM = 256
CHUNK = BF16_ROWS
SLOTS = 2 * TM + N_EXPERTS * CHUNK
CHUNKS_PER_TILE = SLOTS // CHUNK
FFN_ROWS = 256
FFN_CHUNKS = FFN_ROWS // CHUNK
ZERO_CHUNKS = 6
DUMP_CHUNKS = -(-(2 * FFN_CHUNKS + ZERO_CHUNKS) // ZERO_CHUNKS) * ZERO_CHUNKS
TQ = 256
BAND_ROWS = 512

_SWA_HEAD_ORDER = (0, 4, 1, 5, 2, 6, 3, 7)
COL_QB, COL_KB, COL_VB = 0, 512, 640
COL_Q1, COL_Q2, COL_K1, COL_K2, COL_VA = 768, 896, 1024, 1152, 1280
COL_QC, COL_KC, COL_VC = 1536, 1792, 2048
_SRC_ROPE = (32, 32, 32, 32, 0, 0, 64, 64, 64, 64, 64, 0, 64, 64, 64, 64, 0, 0)
_SRC_DEST = (6, 7, 8, 9, 10, 11, None, None, None, None, 4, 5, 12, 13, 14, 15, 16, 17)
VT_ROWS = HEAD_DIM + BF16_ROWS
_SRC_VA_CHUNK = 2


def _residue_perm(tm, d):
    p = np.zeros((tm, tm), np.float32)
    per = tm // d
    for l in range(per):
        for r in range(d):
            p[r * per + l, l * d + r] = 1.0
    return p


def _dot(a, b, **kw):
    return jnp.dot(a, b, preferred_element_type=F32, **kw)


def _dot_nt(a, b):
    return lax.dot_general(a, b, (((1,), (1,)), ((), ())), preferred_element_type=F32)


def _modulated_norm(x, g, sc, sh):
    y = x * lax.rsqrt(jnp.mean(x * x, axis=-1, keepdims=True) + NORM_EPS)
    return (y * g) * (1.0 + sc) + sh


def _ada_kernel(c_ref, w_ref, b_ref, o_ref):
    c = c_ref[...]
    ca = c / (1.0 + jnp.exp(-c))
    o_ref[0] = _dot(ca, w_ref[0], precision=lax.Precision.HIGHEST) + b_ref[0]


def _ada_call(c_pad, ada_w, ada_b):
    depth, d, n = ada_w.shape
    tn = 1536
    return pl.pallas_call(
        _ada_kernel,
        grid=(depth, n // tn),
        in_specs=[
            pl.BlockSpec((c_pad.shape[0], d), lambda l, j: (0, 0)),
            pl.BlockSpec((1, d, tn), lambda l, j: (l, 0, j)),
            pl.BlockSpec((1, 1, tn), lambda l, j: (l, 0, j)),
        ],
        out_specs=pl.BlockSpec((1, c_pad.shape[0], tn), lambda l, j: (l, 0, j)),
        out_shape=jax.ShapeDtypeStruct((depth, c_pad.shape[0], n), F32),
        compiler_params=pltpu.CompilerParams(vmem_limit_bytes=40 * 1024 * 1024),
        name="ada_mod",
    )(c_pad, ada_w, ada_b.reshape(depth, 1, n))


def _rope(t, cos, sin_signed, first_half, half):
    rot = jnp.where(first_half, pltpu.roll(t, LANES - half, 1), pltpu.roll(t, half, 1))
    return t * cos + rot * sin_signed


def _inproj_kernel(x_ref, sc_ref, sh_ref, g_ref, w_ref, cs64_ref, sn64_ref, cs32_ref, sn32_ref,
                   p4_ref, p16_ref, proj_ref, vat_ref, c4_ref, c16_ref, wb):
    @pl.when((pl.program_id(0) == 0) & (pl.program_id(1) == 0))
    def _():
        wb[...] = w_ref[...].astype(BF16)

    h = _modulated_norm(x_ref[0], g_ref[...], sc_ref[0], sh_ref[0])
    hb = h.astype(BF16)
    lane = lax.broadcasted_iota(I32, (1, LANES), 1)
    first64 = (lane % 64) < 32
    first32 = (lane % 32) < 16
    lo_half = lane < 64
    swa_q = []
    for cb in range(IN_WIDTH // 256):
        acc = _dot(hb, wb[:, cb * 256:(cb + 1) * 256])
        if cb == _SRC_VA_CHUNK:
            acc_t = acc.T.astype(BF16)
            for hd in range(DIFF_HEADS):
                vat_ref[0, hd * VT_ROWS:hd * VT_ROWS + HEAD_DIM, :] = acc_t[hd * HEAD_DIM:(hd + 1) * HEAD_DIM]
                vat_ref[0, hd * VT_ROWS + HEAD_DIM:(hd + 1) * VT_ROWS, :] = jnp.ones(
                    (BF16_ROWS, acc_t.shape[1]), BF16)
        for half in range(2):
            src = cb * 2 + half
            t = acc[:, half * LANES:(half + 1) * LANES]
            if _SRC_ROPE[src] == 64:
                t = _rope(t, cs64_ref[0], sn64_ref[0], first64, 32)
            elif _SRC_ROPE[src] == 32:
                t = _rope(t, cs32_ref[0], sn32_ref[0], first32, 16)
            dst = _SRC_DEST[src]
            if dst is None:
                swa_q.append(t)
            else:
                proj_ref[0, :, dst * LANES:(dst + 1) * LANES] = t.astype(BF16)
    for jb in range(SWA_Q_HEADS // 2):
        a, c = swa_q[jb // 2], swa_q[2 + jb // 2]
        if jb % 2 == 0:
            blk = jnp.where(lo_half, a, pltpu.roll(c, 64, 1))
        else:
            blk = jnp.where(lo_half, pltpu.roll(a, 64, 1), c)
        proj_ref[0, :, jb * LANES:(jb + 1) * LANES] = blk.astype(BF16)
    cc = proj_ref[0, :, COL_QC:]
    tm = cc.shape[0]
    c4 = _dot(p4_ref[...], cc).astype(BF16)
    for r in range(4):
        c4_ref[0, r] = c4[r * (tm // 4):(r + 1) * (tm // 4)]
    c16 = _dot(p16_ref[...], cc).astype(BF16)
    for r in range(16):
        c16_ref[0, r] = c16[r * (tm // 16):(r + 1) * (tm // 16)]


def _inproj_call(x, sc, sh, g, w_in, layer, tabs, p4, p16):
    b, s, d = x.shape
    tm = TM
    row = lambda bi, i: (bi, i, 0)
    per_b = lambda bi, i: (bi, 0, 0)
    const2 = lambda bi, i: (0, 0)
    return pl.pallas_call(
        _inproj_kernel,
        grid=(b, s // tm),
        in_specs=[
            pl.BlockSpec((1, tm, d), row),
            pl.BlockSpec((1, 1, d), per_b),
            pl.BlockSpec((1, 1, d), per_b),
            pl.BlockSpec((1, d), const2),
            pl.BlockSpec((None, d, IN_WIDTH), lambda bi, i: (layer, 0, 0)),
            pl.BlockSpec((1, tm, LANES), row),
            pl.BlockSpec((1, tm, LANES), row),
            pl.BlockSpec((1, tm, LANES), row),
            pl.BlockSpec((1, tm, LANES), row),
            pl.BlockSpec((tm, tm), const2),
            pl.BlockSpec((tm, tm), const2),
        ],
        out_specs=[
            pl.BlockSpec((1, tm, IN_WIDTH), row),
            pl.BlockSpec((1, DIFF_HEADS * VT_ROWS, tm), lambda bi, i: (bi, 0, i)),
            pl.BlockSpec((1, 4, tm // 4, 768), lambda bi, i: (bi, 0, i, 0)),
            pl.BlockSpec((1, 16, tm // 16, 768), lambda bi, i: (bi, 0, i, 0)),
        ],
        out_shape=[
            jax.ShapeDtypeStruct((b, s, IN_WIDTH), BF16),
            jax.ShapeDtypeStruct((b, DIFF_HEADS * VT_ROWS, s), BF16),
            jax.ShapeDtypeStruct((b, 4, s // 4, 768), BF16),
            jax.ShapeDtypeStruct((b, 16, s // 16, 768), BF16),
        ],
        scratch_shapes=[pltpu.VMEM((d, IN_WIDTH), BF16)],
        compiler_params=pltpu.CompilerParams(vmem_limit_bytes=56 * 1024 * 1024),
        name="in_proj",
    )(x, sc, sh, g, w_in, *tabs, p4, p16)


def _diff_attn_kernel(lam_ref, q1_ref, q2_ref, k1_ref, k2_ref, vt_ref, g_ref, o_ref, m_sc, acc_sc, s_sc,
                      *, lambda_init):
    tq = q1_ref.shape[1]
    qi = pl.program_id(1)
    lam = lam_ref[0]
    to_log2 = DIFF_QK_DIM ** -0.5 * math.log2(math.e)
    lane = lax.broadcasted_iota(I32, (1, LANES), 1)
    q1 = q1_ref[0].astype(F32) * to_log2
    q2 = q2_ref[0].astype(F32) * to_log2
    qh = []
    for h in range(DIFF_HEADS):
        hm = (lane // DIFF_QK_DIM) == h
        qh.append((jnp.where(hm, q1, 0.0).astype(BF16), jnp.where(hm, q2, 0.0).astype(BF16)))
    causal = (lax.broadcasted_iota(I32, (tq, tq), 0) <= lax.broadcasted_iota(I32, (tq, tq), 1))

    m_sc[...] = jnp.full(m_sc.shape, NEG_INF, F32)
    acc_sc[...] = jnp.zeros(acc_sc.shape, F32)

    n_chain = 2 * DIFF_HEADS

    def scores(ch, tile):
        start = pl.multiple_of(tile * tq, tq)
        k_ref = k1_ref if ch % 2 == 0 else k2_ref
        return _dot_nt(k_ref[0, pl.ds(start, tq), :], qh[ch // 2][ch % 2])

    for ch in range(n_chain):
        s_sc[ch] = scores(ch, 0)

    def step(j, masked):
        start = pl.multiple_of(j * tq, tq)
        for ch in range(n_chain):
            st = s_sc[ch]
            if not masked:
                s_sc[ch] = scores(ch, j + 1)
            h = ch // 2
            vt = vt_ref[0, h * VT_ROWS:(h + 1) * VT_ROWS, pl.ds(start, tq)]
            if masked:
                st = jnp.where(causal, st, NEG_INF)
            m_old = m_sc[ch]
            m_new = jnp.maximum(m_old, jnp.max(st, axis=0, keepdims=True))
            p = jnp.exp2(st - m_new)
            al = jnp.exp2(m_old - m_new)
            acc_sc[ch] = al * acc_sc[ch] + _dot(vt, p.astype(BF16))
            m_sc[ch] = m_new

    def body(j, carry):
        step(j, False)
        return carry

    lax.fori_loop(0, qi, body, 0)
    step(qi, True)

    g = g_ref[...]
    outs = []
    for h in range(DIFF_HEADS):
        a1, a2 = acc_sc[2 * h], acc_sc[2 * h + 1]
        o = (a1[:HEAD_DIM] / a1[HEAD_DIM:HEAD_DIM + 1]
             - lam * (a2[:HEAD_DIM] / a2[HEAD_DIM:HEAD_DIM + 1]))
        ms = jnp.mean(o * o, axis=0, keepdims=True)
        outs.append((o * lax.rsqrt(ms + NORM_EPS)) * g * (1.0 - lambda_init))
    o_ref[0] = jnp.concatenate(outs, axis=0).T.astype(BF16)


def _diff_attn_call(proj, vat, lam, g_cols, lambda_init):
    b, s, _ = proj.shape
    tq = TQ
    qspec = lambda cb: pl.BlockSpec((1, tq, LANES), lambda bi, i, cb=cb: (bi, i, cb))
    kspec = lambda cb: pl.BlockSpec((1, s, LANES), lambda bi, i, cb=cb: (bi, 0, cb))
    n_chain = 2 * DIFF_HEADS
    return pl.pallas_call(
        functools.partial(_diff_attn_kernel, lambda_init=lambda_init),
        grid=(b, s // tq),
        in_specs=[
            pl.BlockSpec(memory_space=pltpu.SMEM),
            qspec(COL_Q1 // LANES), qspec(COL_Q2 // LANES),
            kspec(COL_K1 // LANES), kspec(COL_K2 // LANES),
            pl.BlockSpec((1, DIFF_HEADS * VT_ROWS, s), lambda bi, i: (bi, 0, 0)),
            pl.BlockSpec((HEAD_DIM, tq), lambda bi, i: (0, 0)),
        ],
        out_specs=pl.BlockSpec((1, tq, 256), lambda bi, i: (bi, i, 0)),
        out_shape=jax.ShapeDtypeStruct((b, s, 256), BF16),
        scratch_shapes=[
            pltpu.VMEM((n_chain, 1, tq), F32),
            pltpu.VMEM((n_chain, VT_ROWS, tq), F32),
            pltpu.VMEM((n_chain, tq, tq), F32),
        ],
        compiler_params=pltpu.CompilerParams(vmem_limit_bytes=48 * 1024 * 1024),
        name="diff_attn",
    )(lam, proj, proj, proj, proj, vat, g_cols)


def _band_kernel(*refs, nqb, nkb, max_dist, has_sink, want_lse):
    it = iter(refs)
    sink_ref = next(it) if has_sink else None
    q_ref, kp_ref, kc_ref, vp_ref, vc_ref = (next(it) for _ in range(5))
    o_ref = next(it)
    lse_ref = next(it) if want_lse else None
    kbuf, vbuf = next(it), next(it)
    rows = q_ref.shape[1]
    i = pl.program_id(1)
    kbuf[0:BAND, :] = kp_ref[0]
    kbuf[BAND:, :] = kc_ref[0]
    for kb in range(nkb):
        vbuf[0:BAND, kb * 256:kb * 256 + LANES] = vp_ref[0, :, kb * LANES:(kb + 1) * LANES]
        vbuf[BAND:, kb * 256:kb * 256 + LANES] = vc_ref[0, :, kb * LANES:(kb + 1) * LANES]
        vbuf[:, kb * 256 + LANES:(kb + 1) * 256] = jnp.ones((BAND + rows, LANES), BF16)
    lane = lax.broadcasted_iota(I32, (1, LANES), 1)
    lo_half = lane < 64
    r_io = lax.broadcasted_iota(I32, (BAND, 2 * BAND), 0)
    c_io = lax.broadcasted_iota(I32, (BAND, 2 * BAND), 1)
    dist = BAND + r_io - c_io
    band = (dist >= 0) & (dist <= max_dist)
    band_first = band & ((c_io >= BAND) | (i > 0))
    col0 = lax.broadcasted_iota(I32, (1, 2 * BAND), 1) == 0
    vr = lax.broadcasted_iota(I32, (2 * BAND, 2 * LANES), 0)
    vc = lax.broadcasted_iota(I32, (2 * BAND, 2 * LANES), 1)
    sink_row = (vr == 0) & (vc < LANES)
    to_log2 = HEAD_DIM ** -0.5 * math.log2(math.e)
    units = [(sb, qb) for sb in range(rows // BAND) for qb in range(nqb)]

    def scores(u):
        sb, qb = units[u]
        kb = qb if nkb > 1 else 0
        q = q_ref[0, sb * BAND:(sb + 1) * BAND, qb * LANES:(qb + 1) * LANES].astype(F32) * to_log2
        q2 = jnp.concatenate([jnp.where(lo_half, q, 0.0), jnp.where(lo_half, 0.0, q)], axis=0).astype(BF16)
        return _dot_nt(q2, kbuf[sb * BAND:(sb + 2) * BAND, kb * LANES:(kb + 1) * LANES])

    ahead = 2
    pending = [scores(u) for u in range(min(ahead, len(units)))]
    for u, (sb, qb) in enumerate(units):
        if u + ahead < len(units):
            pending.append(scores(u + ahead))
        s2 = pending[u]
        pending[u] = None
        kb = qb if nkb > 1 else 0
        r0 = sb * BAND
        msk = band_first if sb == 0 else band
        halves = []
        for hh in range(2):
            if has_sink:
                fill = jnp.where(col0, sink_ref[qb * 2 + hh] * math.log2(math.e), NEG_INF)
            else:
                fill = NEG_INF
            halves.append(jnp.where(msk, s2[hh * BAND:(hh + 1) * BAND], fill))
        s2 = jnp.concatenate(halves, axis=0)
        m = jnp.max(s2, axis=1, keepdims=True)
        p = jnp.exp2(s2 - m).astype(BF16)
        vw = vbuf[r0:r0 + 2 * BAND, kb * 256:(kb + 1) * 256]
        if has_sink:
            vw = jnp.where(sink_row, jnp.zeros_like(vw), vw)
        pv = _dot(p, vw)
        den = pv[:, LANES:]
        out = pv[:, :LANES] / den
        o = jnp.where(lo_half, out[:BAND], out[BAND:])
        o_ref[0, r0:r0 + BAND, qb * LANES:(qb + 1) * LANES] = o.astype(BF16)
        if want_lse:
            lse2 = m + jnp.log2(den)
            ls = jnp.where(lo_half, lse2[:BAND], lse2[BAND:])
            hi = ls.astype(BF16)
            lo = (ls - hi.astype(F32)).astype(BF16)
            lse_ref[0, r0:r0 + BAND, qb * LANES:(qb + 1) * LANES] = hi
            lse_ref[0, r0:r0 + BAND, (nqb + qb) * LANES:(nqb + qb + 1) * LANES] = lo


def _band_call(arr, q_col, k_col, v_col, nqb, nkb, max_dist, sinks=None, want_lse=False, name="band"):
    ns, length, _ = arr.shape
    rows = min(BAND_ROWS, length)
    wq, wk = nqb * LANES, nkb * LANES
    rpb = rows // BAND
    cur = lambda col, w: pl.BlockSpec((1, rows, w), lambda n, i, c=col // w: (n, i, c))
    prev = lambda col, w: pl.BlockSpec(
        (1, BAND, w), lambda n, i, c=col // w: (n, jnp.maximum(i * rpb - 1, 0), c))
    in_specs = [cur(q_col, wq), prev(k_col, wk), cur(k_col, wk), prev(v_col, wk), cur(v_col, wk)]
    args = [arr] * 5
    if sinks is not None:
        in_specs = [pl.BlockSpec(memory_space=pltpu.SMEM)] + in_specs
        args = [sinks] + args
    out_specs = [pl.BlockSpec((1, rows, wq), lambda n, i: (n, i, 0))]
    out_shape = [jax.ShapeDtypeStruct((ns, length, wq), BF16)]
    if want_lse:
        out_specs.append(pl.BlockSpec((1, rows, 2 * wq), lambda n, i: (n, i, 0)))
        out_shape.append(jax.ShapeDtypeStruct((ns, length, 2 * wq), BF16))
    return pl.pallas_call(
        functools.partial(_band_kernel, nqb=nqb, nkb=nkb, max_dist=max_dist,
                          has_sink=sinks is not None, want_lse=want_lse),
        grid=(ns, length // rows),
        in_specs=in_specs,
        out_specs=out_specs,
        out_shape=out_shape,
        scratch_shapes=[pltpu.VMEM((BAND + rows, wk), BF16), pltpu.VMEM((BAND + rows, 2 * wk), BF16)],
        name=name,
    )(*args)


def _outproj_kernel(oa_ref, ob_ref, o1_ref, l1_ref, o4_ref, l4_ref, o16_ref, l16_ref, p4t_ref, p16t_ref,
                    w_ref, x_ref, g1_ref, xo_ref, wb):
    @pl.when((pl.program_id(0) == 0) & (pl.program_id(1) == 0))
    def _():
        wb[0:256, :] = w_ref[0:256, :].astype(BF16)
        for pos, head in enumerate(_SWA_HEAD_ORDER):
            wb[256 + pos * HEAD_DIM:256 + (pos + 1) * HEAD_DIM, :] = (
                w_ref[256 + head * HEAD_DIM:256 + (head + 1) * HEAD_DIM, :].astype(BF16))
        wb[768:1024, :] = w_ref[768:1024, :].astype(BF16)

    tm = x_ref.shape[1]
    hw = o1_ref.shape[2]

    def lse_of(v):
        return v[:, :hw] + v[:, hw:]

    o1 = o1_ref[0].astype(F32)
    ls1 = lse_of(l1_ref[0].astype(F32))
    o4 = _dot(p4t_ref[...], o4_ref[0].reshape(tm, hw))
    ls4 = lse_of(_dot(p4t_ref[...], l4_ref[0].reshape(tm, 2 * hw)))
    o16 = _dot(p16t_ref[...], o16_ref[0].reshape(tm, hw))
    ls16 = lse_of(_dot(p16t_ref[...], l16_ref[0].reshape(tm, 2 * hw)))
    mx = jnp.maximum(jnp.maximum(ls1, ls4), ls16)
    e1, e4, e16 = jnp.exp2(ls1 - mx), jnp.exp2(ls4 - mx), jnp.exp2(ls16 - mx)
    oc = (e1 * o1 + e4 * o4 + e16 * o16) / (e1 + e4 + e16)
    mix = (_dot(oa_ref[0], wb[0:256, :]) + _dot(ob_ref[0], wb[256:768, :])
           + _dot(oc.astype(BF16), wb[768:1024, :]))
    xo_ref[0] = x_ref[0] + g1_ref[0] * mix


def _outproj_call(oa, ob, o1, l1, o4, l4, o16, l16, p4t, p16t, w_out, layer, x, g1):
    b, s, d = x.shape
    tm = TM
    row = lambda w_: pl.BlockSpec((1, tm, w_), lambda bi, i: (bi, i, 0))
    res = lambda dd, w_: pl.BlockSpec((1, dd, tm // dd, w_), lambda bi, i: (bi, 0, i, 0))
    const2 = lambda bi, i: (0, 0)
    return pl.pallas_call(
        _outproj_kernel,
        grid=(b, s // tm),
        in_specs=[
            row(256), row(512), row(256), row(512),
            res(4, 256), res(4, 512), res(16, 256), res(16, 512),
            pl.BlockSpec((tm, tm), const2), pl.BlockSpec((tm, tm), const2),
            pl.BlockSpec((None, d, d), lambda bi, i: (layer, 0, 0)),
            row(d),
            pl.BlockSpec((1, 1, d), lambda bi, i: (bi, 0, 0)),
        ],
        out_specs=row(d),
        out_shape=jax.ShapeDtypeStruct((b, s, d), F32),
        scratch_shapes=[pltpu.VMEM((d, d), BF16)],
        compiler_params=pltpu.CompilerParams(vmem_limit_bytes=48 * 1024 * 1024),
        name="out_proj",
    )(oa, ob, o1, l1, o4, l4, o16, l16, p4t, p16t, w_out, x, g1)


ROUTER_COLS = LANES


def _router_kernel(x_ref, sc_ref, sh_ref, g_ref, wr_ref, br_ref, tri_ref, hs_ref, meta_ref, nch_ref):
    tm = x_ref.shape[1]
    h = _modulated_norm(x_ref[0], g_ref[...], sc_ref[0], sh_ref[0])
    hb = h.astype(BF16)
    h_lo = (h - hb.astype(F32)).astype(BF16)
    part = _dot(hb, wr_ref[...]) + _dot(h_lo, wr_ref[...])
    logits = part[:, :ROUTER_COLS] + part[:, ROUTER_COLS:] + br_ref[...]
    lt = logits.T
    glog = lt[0:8]
    elog = lt[8:8 + N_EXPERTS]
    r8 = lax.broadcasted_iota(I32, (8, tm), 0)
    r16 = lax.broadcasted_iota(I32, (N_EXPERTS, tm), 0)

    gmax = jnp.max(glog, axis=0, keepdims=True)
    g_w = 1.0 / jnp.sum(jnp.exp(glog - gmax), axis=0, keepdims=True)
    g_idx = jnp.min(jnp.where(glog == gmax, r8, 99), axis=0, keepdims=True)

    el = jnp.where((r16 // EXPERTS_PER_GROUP) == g_idx, elog, NEG_INF)
    emax = jnp.max(el, axis=0, keepdims=True)
    e1 = jnp.min(jnp.where(el == emax, r16, 99), axis=0, keepdims=True)
    el2 = jnp.where(r16 == e1, NEG_INF, el)
    emax2 = jnp.max(el2, axis=0, keepdims=True)
    e2 = jnp.min(jnp.where(el2 == emax2, r16, 99), axis=0, keepdims=True)
    p2 = jnp.exp(emax2 - emax)
    wt1 = g_w / (1.0 + p2)
    wt2 = g_w * p2 / (1.0 + p2)

    oh1 = r16 == e1
    oh2 = r16 == e2
    onehot = jnp.where(oh1, 1.0, 0.0) + jnp.where(oh2, 1.0, 0.0)
    cnt = jnp.sum(onehot, axis=1, keepdims=True)
    nch = jnp.floor((cnt + (CHUNK - 1)) * (1.0 / CHUNK))
    nchb = jnp.broadcast_to(nch, (N_EXPERTS, LANES))
    rl = lax.broadcasted_iota(I32, (N_EXPERTS, LANES), 0)
    incl = nchb
    for sft in (1, 2, 4, 8):
        incl = incl + jnp.where(rl >= sft, pltpu.roll(incl, sft, 0), 0.0)
    off = (incl - nchb)[:, 0:1] * float(CHUNK)
    rank = _dot(onehot.astype(BF16), tri_ref[...])
    slot_of = off + rank
    pos1 = jnp.sum(jnp.where(oh1, slot_of, 0.0), axis=0, keepdims=True)
    pos2 = jnp.sum(jnp.where(oh2, slot_of, 0.0), axis=0, keepdims=True)

    slot = lax.broadcasted_iota(I32, (SLOTS, tm), 0)
    sel = jnp.where(slot == pos1.astype(I32), 1.0, jnp.where(slot == pos2.astype(I32), 1.0, 0.0))
    hs_ref[...] = _dot(sel.astype(BF16), hb).astype(BF16)

    meta_ref[0] = jnp.concatenate([pos1, pos2, wt1, wt2, jnp.zeros((4, tm), F32)], axis=0)
    nch_ref[0] = nchb.astype(I32)


def _router_call(x2d, sc, sh, g, wr, br, layer, tri, tiles_per_batch):
    n, d = x2d.shape
    tm = TM
    nt = n // tm
    x3 = x2d.reshape(nt, tm, d)
    per_b = lambda t: (t // tiles_per_batch, 0, 0)
    return pl.pallas_call(
        _router_kernel,
        grid=(nt,),
        in_specs=[
            pl.BlockSpec((1, tm, d), lambda t: (t, 0, 0)),
            pl.BlockSpec((1, 1, d), per_b),
            pl.BlockSpec((1, 1, d), per_b),
            pl.BlockSpec((1, d), lambda t: (0, 0)),
            pl.BlockSpec((None, d, 2 * ROUTER_COLS), lambda t: (layer, 0, 0)),
            pl.BlockSpec((None, 1, ROUTER_COLS), lambda t: (layer, 0, 0)),
            pl.BlockSpec((tm, tm), lambda t: (0, 0)),
        ],
        out_specs=[
            pl.BlockSpec((SLOTS, d), lambda t: (t, 0)),
            pl.BlockSpec((1, 8, tm), lambda t: (t, 0, 0)),
            pl.BlockSpec((1, N_EXPERTS, LANES), lambda t: (t, 0, 0)),
        ],
        out_shape=[
            jax.ShapeDtypeStruct((nt * SLOTS, d), BF16),
            jax.ShapeDtypeStruct((nt, 8, tm), F32),
            jax.ShapeDtypeStruct((nt, N_EXPERTS, LANES), I32),
        ],
        compiler_params=pltpu.CompilerParams(vmem_limit_bytes=48 * 1024 * 1024),
        name="router_sort",
    )(x3, sc, sh, g, wr, br, tri)


def _ffn_schedule(nch, max_tiles):
    nt = nch.shape[0]
    cend = jnp.cumsum(nch, axis=1)
    coff = cend - nch
    tcum = jnp.cumsum(nch, axis=0)
    before = tcum - nch
    tot = tcum[-1]
    pad = ((tot + FFN_CHUNKS - 1) // FFN_CHUNKS) * FFN_CHUNKS
    eend = jnp.cumsum(pad)
    estart = eend - pad
    n_tiles = (eend[-1] // FFN_CHUNKS).astype(I32)
    first_chunk = jnp.arange(max_tiles, dtype=I32) * FFN_CHUNKS
    tile_expert = jnp.sum((eend[None, :] <= first_chunk[:, None]).astype(I32), axis=1)
    tile_expert = jnp.minimum(tile_expert, N_EXPERTS - 1)
    hp = lax.Precision.HIGHEST
    pos = jnp.arange(max_tiles * FFN_CHUNKS, dtype=I32)
    e_s = jnp.minimum(jnp.sum((eend[None, :] <= (pos // FFN_CHUNKS * FFN_CHUNKS)[:, None]).astype(I32), axis=1),
                      N_EXPERTS - 1)
    oh_e = (e_s[:, None] == jnp.arange(N_EXPERTS, dtype=I32)[None, :]).astype(F32)
    idx = pos - jnp.dot(oh_e, estart.astype(F32), precision=hp).astype(I32)
    run_end = jnp.dot(oh_e, tcum.T.astype(F32), precision=hp).astype(I32)
    run_beg = jnp.dot(oh_e, before.T.astype(F32), precision=hp).astype(I32)
    run_off = jnp.dot(oh_e, coff.T.astype(F32), precision=hp).astype(I32)
    in_run = (idx[:, None] >= run_beg) & (idx[:, None] < run_end)
    tile_base = jnp.arange(nt, dtype=I32)[None, :] * CHUNKS_PER_TILE
    src = jnp.sum(jnp.where(in_run, tile_base + run_off + idx[:, None] - run_beg, 0), axis=1)
    chunk_src = jnp.where(jnp.any(in_run, axis=1), src, -1)
    used = cend[:, -1]
    ucum = jnp.cumsum(CHUNKS_PER_TILE - used)
    ubeg = ucum - (CHUNKS_PER_TILE - used)
    z = jnp.arange(max_tiles * ZERO_CHUNKS, dtype=I32)[:, None]
    in_gap = (z >= ubeg[None, :]) & (z < ucum[None, :])
    zsrc = jnp.sum(jnp.where(in_gap, tile_base + used[None, :] + z - ubeg[None, :], 0), axis=1)
    zero_chunks = jnp.where(jnp.any(in_gap, axis=1), zsrc, -1)
    return tile_expert, chunk_src, n_tiles.reshape(1), zero_chunks


def _ffn_kernel(te_ref, cs_ref, nt_ref, zc_ref, hs_hbm, wg_ref, wu_ref, wd_ref, ys_hbm,
                xbuf, ybuf, zbuf, wgb, wub, wdb, in_sem, out_sem, zero_sem, *, dump_base):
    j = pl.program_id(0)
    nt = nt_ref[0]
    half_ff = EXPERT_FF // 2

    def rows_of(c):
        return pl.ds(pl.multiple_of(c * CHUNK, CHUNK), CHUNK)

    def in_copy(step, slot, k):
        c = jnp.maximum(cs_ref[step * FFN_CHUNKS + k], 0)
        return pltpu.make_async_copy(hs_hbm.at[rows_of(c), :], xbuf.at[slot, pl.ds(k * CHUNK, CHUNK), :],
                                     in_sem.at[slot])

    def out_copy(step, slot, k):
        c = cs_ref[step * FFN_CHUNKS + k]
        c = jnp.where(c >= 0, c, dump_base + slot * FFN_CHUNKS + k)
        return pltpu.make_async_copy(ybuf.at[slot, pl.ds(k * CHUNK, CHUNK), :], ys_hbm.at[rows_of(c), :],
                                     out_sem.at[slot])

    def zero_copy(k):
        c = zc_ref[j * ZERO_CHUNKS + k]
        c = jnp.where(c >= 0, c, dump_base + 2 * FFN_CHUNKS + k)
        return pltpu.make_async_copy(zbuf.at[pl.ds(k * CHUNK, CHUNK), :], ys_hbm.at[rows_of(c), :], zero_sem)

    @pl.when(j == 0)
    def _():
        zbuf[...] = jnp.zeros_like(zbuf)
        fills = [pltpu.make_async_copy(
            zbuf, ys_hbm.at[pl.ds((dump_base + r * ZERO_CHUNKS) * CHUNK, ZERO_CHUNKS * CHUNK), :], zero_sem)
            for r in range(DUMP_CHUNKS // ZERO_CHUNKS)]
        for cp in fills:
            cp.start()
        for cp in fills:
            cp.wait()

    for k in range(ZERO_CHUNKS):
        zero_copy(k).start()

    @pl.when(j < nt)
    def _():
        slot = j % 2

        @pl.when(j == 0)
        def _():
            for k in range(FFN_CHUNKS):
                in_copy(0, 0, k).start()

        @pl.when(j + 1 < nt)
        def _():
            for k in range(FFN_CHUNKS):
                in_copy(j + 1, 1 - slot, k).start()

        @pl.when((j == 0) | (te_ref[j] != te_ref[jnp.maximum(j - 1, 0)]))
        def _():
            wgb[...] = wg_ref[0].astype(BF16)
            wub[...] = wu_ref[0].astype(BF16)
            wdb[...] = wd_ref[0].astype(BF16)

        for k in range(FFN_CHUNKS):
            in_copy(j, slot, k).wait()

        @pl.when(j >= 2)
        def _():
            for k in range(FFN_CHUNKS):
                out_copy(j - 2, slot, k).wait()

        x = xbuf[slot]
        hg = [_dot(x, wgb[:, h * half_ff:(h + 1) * half_ff]) for h in range(2)]
        hu = [_dot(x, wub[:, h * half_ff:(h + 1) * half_ff]) for h in range(2)]
        y = None
        for h in range(2):
            act = ((hg[h] / (1.0 + jnp.exp(-hg[h]))) * hu[h]).astype(BF16)
            part = _dot(act, wdb[h * half_ff:(h + 1) * half_ff, :])
            y = part if y is None else y + part
        ybuf[slot] = y.astype(BF16)
        for k in range(FFN_CHUNKS):
            out_copy(j, slot, k).start()

        @pl.when(j == nt - 1)
        def _():
            for k in range(FFN_CHUNKS):
                out_copy(j, slot, k).wait()

            @pl.when(j >= 1)
            def _():
                for k in range(FFN_CHUNKS):
                    out_copy(j - 1, 1 - slot, k).wait()

    for k in range(ZERO_CHUNKS):
        zero_copy(k).wait()


def _ffn_call(tile_expert, chunk_src, n_tiles, zero_chunks, hs, wg, wu, wd, layer, max_tiles):
    rows, d = hs.shape
    ff = wg.shape[-1]
    wmap = lambda j, te, cs, nt, zc: (layer, te[j], 0, 0)
    grid_spec = pltpu.PrefetchScalarGridSpec(
        num_scalar_prefetch=4,
        grid=(max_tiles,),
        in_specs=[
            pl.BlockSpec(memory_space=pl.ANY),
            pl.BlockSpec((None, 1, d, ff), wmap),
            pl.BlockSpec((None, 1, d, ff), wmap),
            pl.BlockSpec((None, 1, ff, d), wmap),
        ],
        out_specs=pl.BlockSpec(memory_space=pl.ANY),
        scratch_shapes=[
            pltpu.VMEM((2, FFN_ROWS, d), BF16),
            pltpu.VMEM((2, FFN_ROWS, d), BF16),
            pltpu.VMEM((ZERO_CHUNKS * CHUNK, d), BF16),
            pltpu.VMEM((d, ff), BF16),
            pltpu.VMEM((d, ff), BF16),
            pltpu.VMEM((ff, d), BF16),
            pltpu.SemaphoreType.DMA((2,)),
            pltpu.SemaphoreType.DMA((2,)),
            pltpu.SemaphoreType.DMA(()),
        ],
    )
    return pl.pallas_call(
        functools.partial(_ffn_kernel, dump_base=rows // CHUNK),
        grid_spec=grid_spec,
        out_shape=jax.ShapeDtypeStruct((rows + DUMP_CHUNKS * CHUNK, d), BF16),
        compiler_params=pltpu.CompilerParams(vmem_limit_bytes=48 * 1024 * 1024),
        name="expert_ffn",
    )(tile_expert, chunk_src, n_tiles, zero_chunks, hs, wg, wu, wd)


def _combine_kernel(ys_ref, meta_ref, x_ref, g2_ref, gf_ref, xo_ref, *, final_norm):
    tm = x_ref.shape[1]
    meta = meta_ref[0]
    eye = (lax.broadcasted_iota(I32, (tm, tm), 0) == lax.broadcasted_iota(I32, (tm, tm), 1))

    def as_col(row):
        return jnp.sum(jnp.where(eye, row, 0.0), axis=1, keepdims=True)

    pos1, pos2 = as_col(meta[0:1]), as_col(meta[1:2])
    w1, w2 = as_col(meta[2:3]), as_col(meta[3:4])
    slot = lax.broadcasted_iota(I32, (tm, SLOTS), 1).astype(F32)
    gate = jnp.where(slot == pos1, w1, 0.0) + jnp.where(slot == pos2, w2, 0.0)
    y = _dot(gate.astype(BF16), ys_ref[...])
    xo = x_ref[0] + g2_ref[0] * y
    if final_norm:
        xo = xo * lax.rsqrt(jnp.mean(xo * xo, axis=-1, keepdims=True) + NORM_EPS) * gf_ref[...]
    xo_ref[0] = xo


def _combine_call(ys, meta, x2d, g2, gf, tiles_per_batch, final_norm):
    n, d = x2d.shape
    tm = TM
    nt = n // tm
    return pl.pallas_call(
        functools.partial(_combine_kernel, final_norm=final_norm),
        grid=(nt,),
        in_specs=[
            pl.BlockSpec((SLOTS, d), lambda t: (t, 0)),
            pl.BlockSpec((1, 8, tm), lambda t: (t, 0, 0)),
            pl.BlockSpec((1, tm, d), lambda t: (t, 0, 0)),
            pl.BlockSpec((1, 1, d), lambda t: (t // tiles_per_batch, 0, 0)),
            pl.BlockSpec((1, d), lambda t: (0, 0)),
        ],
        out_specs=pl.BlockSpec((1, tm, d), lambda t: (t, 0, 0)),
        out_shape=jax.ShapeDtypeStruct((nt, tm, d), F32),
        compiler_params=pltpu.CompilerParams(vmem_limit_bytes=48 * 1024 * 1024),
        name="moe_combine",
    )(ys, meta, x2d.reshape(nt, tm, d), g2, gf)


def _rope_tables(positions):
    pos = positions.astype(F32)[..., None]

    def table(dim):
        inv = ROPE_THETA ** (-jnp.arange(0, dim, 2, dtype=F32) / dim)
        ang = pos * inv
        cos, sin = jnp.cos(ang), jnp.sin(ang)
        reps = LANES // dim
        return (jnp.tile(jnp.concatenate([cos, cos], -1), (1, 1, reps)),
                jnp.tile(jnp.concatenate([-sin, sin], -1), (1, 1, reps)))

    c64, s64 = table(HEAD_DIM)
    c32, s32 = table(DIFF_QK_DIM)
    return c64, s64, c32, s32


def kernel(x, c, positions, ada_w, ada_b, norm_mix_g, norm_ffn_g, w_in, w_out, diff_lambda_q1, diff_lambda_k1,
           diff_lambda_q2, diff_lambda_k2, diff_subln_g, swa_sinks, router_group_w, router_group_b,
           router_expert_w, router_expert_b, expert_w_gate, expert_w_up, expert_w_down, final_norm_g):
    b, s, d = x.shape
    depth = ada_w.shape[0]
    n = b * s
    nt = n // TM
    tiles_per_batch = s // TM
    max_tiles = (nt * CHUNKS_PER_TILE + N_EXPERTS * (FFN_CHUNKS - 1)) // FFN_CHUNKS + 1

    tabs = _rope_tables(positions)
    p4 = _residue_perm(TM, 4)
    p16 = _residue_perm(TM, 16)
    p4_b, p16_b = jnp.asarray(p4, BF16), jnp.asarray(p16, BF16)
    p4t_b, p16t_b = jnp.asarray(p4.T, BF16), jnp.asarray(p16.T, BF16)
    tri = jnp.asarray(np.triu(np.ones((TM, TM), np.float32), 1), BF16)

    c_pad = jnp.pad(c, ((0, 8 - b), (0, 0)))
    mod = _ada_call(c_pad, ada_w, ada_b)[:, :b]

    sink_order = np.asarray(_SWA_HEAD_ORDER, np.int32)
    zpad = lambda k: jnp.zeros((depth, d, k), F32)
    wr = jnp.concatenate([router_group_w, zpad(8 - N_GROUPS), router_expert_w,
                          zpad(ROUTER_COLS - 8 - N_EXPERTS)], axis=-1)
    wr_hi = wr.astype(BF16)
    wr = jnp.concatenate([wr_hi, (wr - wr_hi.astype(F32)).astype(BF16)], axis=-1)
    br = jnp.concatenate([router_group_b, jnp.full((depth, 8 - N_GROUPS), NEG_INF, F32), router_expert_b,
                          jnp.zeros((depth, ROUTER_COLS - 8 - N_EXPERTS), F32)], axis=-1).reshape(depth, 1, -1)

    for l in range(depth):
        sh1, sc1, g1, sh2, sc2, g2 = [mod[l, :, k * d:(k + 1) * d].reshape(b, 1, d) for k in range(N_ADA)]
        proj, vat, qkv4, qkv16 = _inproj_call(x, sc1, sh1, norm_mix_g[l].reshape(1, d), w_in, l, tabs,
                                              p4_b, p16_b)

        lambda_init = 0.8 - 0.6 * math.exp(-0.3 * l)
        lam = (jnp.exp(jnp.sum(diff_lambda_q1[l] * diff_lambda_k1[l]))
               - jnp.exp(jnp.sum(diff_lambda_q2[l] * diff_lambda_k2[l])) + lambda_init).reshape(1)
        g_cols = jnp.broadcast_to(diff_subln_g[l][:, None], (HEAD_DIM, TQ))
        oa = _diff_attn_call(proj, vat, lam, g_cols, lambda_init)

        ob = _band_call(proj, COL_QB, COL_KB, COL_VB, nqb=4, nkb=1, max_dist=SWA_WINDOW - 1,
                        sinks=swa_sinks[l][sink_order], name="swa")[0]
        o1, l1 = _band_call(proj, COL_QC, COL_KC, COL_VC, nqb=2, nkb=2, max_dist=BAND, want_lse=True,
                            name="dil1")
        o4, l4 = _band_call(qkv4.reshape(b * 4, s // 4, 768), 0, 256, 512, nqb=2, nkb=2, max_dist=BAND,
                            want_lse=True, name="dil4")
        o16, l16 = _band_call(qkv16.reshape(b * 16, s // 16, 768), 0, 256, 512, nqb=2, nkb=2, max_dist=BAND,
                              want_lse=True, name="dil16")
        x = _outproj_call(oa, ob, o1, l1,
                          o4.reshape(b, 4, s // 4, 256), l4.reshape(b, 4, s // 4, 512),
                          o16.reshape(b, 16, s // 16, 256), l16.reshape(b, 16, s // 16, 512),
                          p4t_b, p16t_b, w_out, l, x, g1)

        x2d = x.reshape(n, d)
        hs, meta, nch = _router_call(x2d, sc2, sh2, norm_ffn_g[l].reshape(1, d), wr, br, l, tri,
                                     tiles_per_batch)
        tile_expert, chunk_src, n_tiles, zero_chunks = _ffn_schedule(nch[:, :, 0], max_tiles)
        ys = _ffn_call(tile_expert, chunk_src, n_tiles, zero_chunks, hs, expert_w_gate, expert_w_up,
                       expert_w_down, l, max_tiles)
        x = _combine_call(ys, meta, x2d, g2, final_norm_g.reshape(1, d), tiles_per_batch,
                          final_norm=(l == depth - 1)).reshape(b, s, d)
    return x
```

```python
import functools
import math

import numpy as np
import jax
import jax.numpy as jnp
from jax import lax
from jax.experimental import pallas as pl
from jax.experimental.pallas import tpu as pltpu

F32 = jnp.float32
BF16 = jnp.bfloat16
I32 = jnp.int32

HEAD_DIM = 64
ROPE_THETA = 10000.0
NORM_EPS = 1e-6
NEG_INF = -1e30
DIFF_HEADS = 4
DIFF_QK_DIM = 32
SWA_Q_HEADS = 8
SWA_KV_HEADS = 2
SWA_WINDOW = 128
DIL_PATTERNS = ((128, 1), (512, 4), (2048, 16))
N_GROUPS = 4
EXPERTS_PER_GROUP = 4
N_EXPERTS = 16
EXPERT_FF = 512
N_ADA = 6
IN_WIDTH = 2304

LANES = 128
BF16_ROWS = 16
BAND = 128

TM = 256
CHUNK = BF16_ROWS
SLOTS = 2 * TM + N_EXPERTS * CHUNK
CHUNKS_PER_TILE = SLOTS // CHUNK
FFN_ROWS = 256
FFN_CHUNKS = FFN_ROWS // CHUNK
ZERO_CHUNKS = 6
DUMP_CHUNKS = -(-(2 * FFN_CHUNKS + ZERO_CHUNKS) // ZERO_CHUNKS) * ZERO_CHUNKS
TQ = 512
BAND_ROWS = 512

_SWA_HEAD_ORDER = (0, 4, 1, 5, 2, 6, 3, 7)
COL_QB, COL_KB, COL_VB = 0, 512, 640
COL_Q1, COL_Q2, COL_K1, COL_K2, COL_VA = 768, 896, 1024, 1152, 1280
COL_QC, COL_KC, COL_VC = 1536, 1792, 2048
_SRC_ROPE = (32, 32, 32, 32, 0, 0, 64, 64, 64, 64, 64, 0, 64, 64, 64, 64, 0, 0)
_SRC_DEST = (6, 7, 8, 9, 10, 11, None, None, None, None, 4, 5, 12, 13, 14, 15, 16, 17)
VT_ROWS = HEAD_DIM + BF16_ROWS
_SRC_VA_CHUNK = 2


def _residue_perm(tm, d):
    p = np.zeros((tm, tm), np.float32)
    per = tm // d
    for l in range(per):
        for r in range(d):
            p[r * per + l, l * d + r] = 1.0
    return p


def _dot(a, b, **kw):
    return jnp.dot(a, b, preferred_element_type=F32, **kw)


def _dot_nt(a, b):
    return lax.dot_general(a, b, (((1,), (1,)), ((), ())), preferred_element_type=F32)


def _modulated_norm(x, g, sc, sh):
    y = x * lax.rsqrt(jnp.mean(x * x, axis=-1, keepdims=True) + NORM_EPS)
    return (y * g) * (1.0 + sc) + sh


def _ada_kernel(c_ref, w_ref, b_ref, o_ref):
    c = c_ref[...]
    ca = c / (1.0 + jnp.exp(-c))
    o_ref[0] = _dot(ca, w_ref[0], precision=lax.Precision.HIGHEST) + b_ref[0]


def _ada_call(c_pad, ada_w, ada_b):
    depth, d, n = ada_w.shape
    tn = 1536
    return pl.pallas_call(
        _ada_kernel,
        grid=(depth, n // tn),
        in_specs=[
            pl.BlockSpec((c_pad.shape[0], d), lambda l, j: (0, 0)),
            pl.BlockSpec((1, d, tn), lambda l, j: (l, 0, j)),
            pl.BlockSpec((1, 1, tn), lambda l, j: (l, 0, j)),
        ],
        out_specs=pl.BlockSpec((1, c_pad.shape[0], tn), lambda l, j: (l, 0, j)),
        out_shape=jax.ShapeDtypeStruct((depth, c_pad.shape[0], n), F32),
        compiler_params=pltpu.CompilerParams(vmem_limit_bytes=40 * 1024 * 1024),
        name="ada_mod",
    )(c_pad, ada_w, ada_b.reshape(depth, 1, n))


def _rope(t, cos, sin_signed, first_half, half):
    rot = jnp.where(first_half, pltpu.roll(t, LANES - half, 1), pltpu.roll(t, half, 1))
    return t * cos + rot * sin_signed


def _inproj_body(x, sc_ref, sh_ref, g_ref, w_ref, cs64_ref, sn64_ref, cs32_ref, sn32_ref,
                 p4_ref, p16_ref, proj_ref, vat_ref, c4_ref, c16_ref, wb):
    @pl.when((pl.program_id(0) == 0) & (pl.program_id(1) == 0))
    def _():
        wb[...] = w_ref[...].astype(BF16)

    h = _modulated_norm(x, g_ref[...], sc_ref[0], sh_ref[0])
    hb = h.astype(BF16)
    lane = lax.broadcasted_iota(I32, (1, LANES), 1)
    first64 = (lane % 64) < 32
    first32 = (lane % 32) < 16
    lo_half = lane < 64
    swa_q = []
    for cb in range(IN_WIDTH // 256):
        acc = _dot(hb, wb[:, cb * 256:(cb + 1) * 256])
        if cb == _SRC_VA_CHUNK:
            acc_t = acc.T.astype(BF16)
            for hd in range(DIFF_HEADS):
                vat_ref[0, hd * VT_ROWS:hd * VT_ROWS + HEAD_DIM, :] = acc_t[hd * HEAD_DIM:(hd + 1) * HEAD_DIM]
                vat_ref[0, hd * VT_ROWS + HEAD_DIM:(hd + 1) * VT_ROWS, :] = jnp.ones(
                    (BF16_ROWS, acc_t.shape[1]), BF16)
        for half in range(2):
            src = cb * 2 + half
            t = acc[:, half * LANES:(half + 1) * LANES]
            if _SRC_ROPE[src] == 64:
                t = _rope(t, cs64_ref[0], sn64_ref[0], first64, 32)
            elif _SRC_ROPE[src] == 32:
                t = _rope(t, cs32_ref[0], sn32_ref[0], first32, 16)
            dst = _SRC_DEST[src]
            if dst is None:
                swa_q.append(t)
            else:
                proj_ref[0, :, dst * LANES:(dst + 1) * LANES] = t.astype(BF16)
    for jb in range(SWA_Q_HEADS // 2):
        a, c = swa_q[jb // 2], swa_q[2 + jb // 2]
        if jb % 2 == 0:
            blk = jnp.where(lo_half, a, pltpu.roll(c, 64, 1))
        else:
            blk = jnp.where(lo_half, pltpu.roll(a, 64, 1), c)
        proj_ref[0, :, jb * LANES:(jb + 1) * LANES] = blk.astype(BF16)
    cc = proj_ref[0, :, COL_QC:]
    tm = cc.shape[0]
    c4 = _dot(p4_ref[...], cc).astype(BF16)
    for r in range(4):
        c4_ref[0, r] = c4[r * (tm // 4):(r + 1) * (tm // 4)]
    c16 = _dot(p16_ref[...], cc).astype(BF16)
    for r in range(16):
        c16_ref[0, r] = c16[r * (tm // 16):(r + 1) * (tm // 16)]


def _inproj_kernel(x_ref, *refs):
    _inproj_body(x_ref[0], *refs)


def _inproj_specs(b, s, d, layer):
    tm = TM
    row = lambda bi, i: (bi, i, 0)
    per_b = lambda bi, i: (bi, 0, 0)
    const2 = lambda bi, i: (0, 0)
    in_specs = [
        pl.BlockSpec((1, 1, d), per_b),
        pl.BlockSpec((1, 1, d), per_b),
        pl.BlockSpec((1, d), const2),
        pl.BlockSpec((None, d, IN_WIDTH), lambda bi, i: (layer, 0, 0)),
        pl.BlockSpec((1, tm, LANES), row),
        pl.BlockSpec((1, tm, LANES), row),
        pl.BlockSpec((1, tm, LANES), row),
        pl.BlockSpec((1, tm, LANES), row),
        pl.BlockSpec((tm, tm), const2),
        pl.BlockSpec((tm, tm), const2),
    ]
    out_specs = [
        pl.BlockSpec((1, tm, IN_WIDTH), row),
        pl.BlockSpec((1, DIFF_HEADS * VT_ROWS, tm), lambda bi, i: (bi, 0, i)),
        pl.BlockSpec((1, 4, tm // 4, 768), lambda bi, i: (bi, 0, i, 0)),
        pl.BlockSpec((1, 16, tm // 16, 768), lambda bi, i: (bi, 0, i, 0)),
    ]
    out_shape = [
        jax.ShapeDtypeStruct((b, s, IN_WIDTH), BF16),
        jax.ShapeDtypeStruct((b, DIFF_HEADS * VT_ROWS, s), BF16),
        jax.ShapeDtypeStruct((b, 4, s // 4, 768), BF16),
        jax.ShapeDtypeStruct((b, 16, s // 16, 768), BF16),
    ]
    return in_specs, out_specs, out_shape


def _inproj_call(x, sc, sh, g, w_in, layer, tabs, p4, p16):
    b, s, d = x.shape
    in_specs, out_specs, out_shape = _inproj_specs(b, s, d, layer)
    return pl.pallas_call(
        _inproj_kernel,
        grid=(b, s // TM),
        in_specs=[pl.BlockSpec((1, TM, d), lambda bi, i: (bi, i, 0))] + in_specs,
        out_specs=out_specs,
        out_shape=out_shape,
        scratch_shapes=[pltpu.VMEM((d, IN_WIDTH), BF16)],
        compiler_params=pltpu.CompilerParams(vmem_limit_bytes=56 * 1024 * 1024),
        name="in_proj",
    )(x, sc, sh, g, w_in, *tabs, p4, p16)


def _diff_attn_kernel(lam_ref, q1_ref, q2_ref, q1n_ref, q2n_ref, k1_ref, k2_ref, vt_ref, g_ref, o_ref,
                      m_sc, acc_sc, s_sc, *, lambda_init):
    tq = q1_ref.shape[1]
    qi = pl.program_id(1)
    lam = lam_ref[0]
    to_log2 = DIFF_QK_DIM ** -0.5 * math.log2(math.e)
    lane = lax.broadcasted_iota(I32, (1, LANES), 1)

    def head_queries(qa_ref, qb_ref):
        qa = qa_ref[0].astype(F32) * to_log2
        qb = qb_ref[0].astype(F32) * to_log2
        out = []
        for h in range(DIFF_HEADS):
            hm = (lane // DIFF_QK_DIM) == h
            out.append((jnp.where(hm, qa, 0.0).astype(BF16), jnp.where(hm, qb, 0.0).astype(BF16)))
        return out

    qh = head_queries(q1_ref, q2_ref)
    qh_next = head_queries(q1n_ref, q2n_ref)
    causal = (lax.broadcasted_iota(I32, (tq, tq), 0) <= lax.broadcasted_iota(I32, (tq, tq), 1))

    m_sc[...] = jnp.full(m_sc.shape, NEG_INF, F32)
    acc_sc[...] = jnp.zeros(acc_sc.shape, F32)

    n_chain = 2 * DIFF_HEADS

    def scores(ch, tile, queries):
        start = pl.multiple_of(tile * tq, tq)
        k_ref = k1_ref if ch % 2 == 0 else k2_ref
        return _dot_nt(k_ref[0, pl.ds(start, tq), :], queries[ch // 2][ch % 2])

    @pl.when(qi == 0)
    def _():
        for ch in range(n_chain):
            s_sc[ch] = scores(ch, 0, qh)

    def step(j, last):
        start = pl.multiple_of(j * tq, tq)
        for ch in range(n_chain):
            st = s_sc[ch]
            s_sc[ch] = scores(ch, 0, qh_next) if last else scores(ch, j + 1, qh)
            h = ch // 2
            vt = vt_ref[0, h * VT_ROWS:(h + 1) * VT_ROWS, pl.ds(start, tq)]
            if last:
                st = jnp.where(causal, st, NEG_INF)
            m_old = m_sc[ch]
            m_new = jnp.maximum(m_old, jnp.max(st, axis=0, keepdims=True))
            p = jnp.exp2(st - m_new)
            al = jnp.exp2(m_old - m_new)
            acc_sc[ch] = al * acc_sc[ch] + _dot(vt, p.astype(BF16))
            m_sc[ch] = m_new

    def body(j, carry):
        step(j, False)
        return carry

    lax.fori_loop(0, qi, body, 0)
    step(qi, True)

    g = g_ref[...]
    outs = []
    for h in range(DIFF_HEADS):
        a1, a2 = acc_sc[2 * h], acc_sc[2 * h + 1]
        o = (a1[:HEAD_DIM] / a1[HEAD_DIM:HEAD_DIM + 1]
             - lam * (a2[:HEAD_DIM] / a2[HEAD_DIM:HEAD_DIM + 1]))
        ms = jnp.mean(o * o, axis=0, keepdims=True)
        outs.append((o * lax.rsqrt(ms + NORM_EPS)) * g * (1.0 - lambda_init))
    o_ref[0] = jnp.concatenate(outs, axis=0).T.astype(BF16)


def _diff_attn_call(proj, vat, lam, g_cols, lambda_init):
    b, s, _ = proj.shape
    tq = TQ
    qspec = lambda cb: pl.BlockSpec((1, tq, LANES), lambda bi, i, cb=cb: (bi, i, cb))
    last_q = s // tq - 1
    qnext = lambda cb: pl.BlockSpec((1, tq, LANES), lambda bi, i, cb=cb: (bi, jnp.minimum(i + 1, last_q), cb))
    kspec = lambda cb: pl.BlockSpec((1, s, LANES), lambda bi, i, cb=cb: (bi, 0, cb))
    n_chain = 2 * DIFF_HEADS
    return pl.pallas_call(
        functools.partial(_diff_attn_kernel, lambda_init=lambda_init),
        grid=(b, s // tq),
        in_specs=[
            pl.BlockSpec(memory_space=pltpu.SMEM),
            qspec(COL_Q1 // LANES), qspec(COL_Q2 // LANES),
            qnext(COL_Q1 // LANES), qnext(COL_Q2 // LANES),
            kspec(COL_K1 // LANES), kspec(COL_K2 // LANES),
            pl.BlockSpec((1, DIFF_HEADS * VT_ROWS, s), lambda bi, i: (bi, 0, 0)),
            pl.BlockSpec((HEAD_DIM, tq), lambda bi, i: (0, 0)),
        ],
        out_specs=pl.BlockSpec((1, tq, 256), lambda bi, i: (bi, i, 0)),
        out_shape=jax.ShapeDtypeStruct((b, s, 256), BF16),
        scratch_shapes=[
            pltpu.VMEM((n_chain, 1, tq), F32),
            pltpu.VMEM((n_chain, VT_ROWS, tq), F32),
            pltpu.VMEM((n_chain, tq, tq), F32),
        ],
        compiler_params=pltpu.CompilerParams(vmem_limit_bytes=48 * 1024 * 1024),
        name="diff_attn",
    )(lam, proj, proj, proj, proj, proj, proj, vat, g_cols)


def _band_kernel(*refs, nqb, nkb, max_dist, has_sink, want_lse):
    it = iter(refs)
    sink_ref = next(it) if has_sink else None
    q_ref, kp_ref, kc_ref, vp_ref, vc_ref = (next(it) for _ in range(5))
    o_ref = next(it)
    lse_ref = next(it) if want_lse else None
    kbuf, vbuf = next(it), next(it)
    rows = q_ref.shape[1]
    i = pl.program_id(1)
    kbuf[0:BAND, :] = kp_ref[0]
    kbuf[BAND:, :] = kc_ref[0]
    for kb in range(nkb):
        vbuf[0:BAND, kb * 256:kb * 256 + LANES] = vp_ref[0, :, kb * LANES:(kb + 1) * LANES]
        vbuf[BAND:, kb * 256:kb * 256 + LANES] = vc_ref[0, :, kb * LANES:(kb + 1) * LANES]
        vbuf[:, kb * 256 + LANES:(kb + 1) * 256] = jnp.ones((BAND + rows, LANES), BF16)
    lane = lax.broadcasted_iota(I32, (1, LANES), 1)
    lo_half = lane < 64
    r_io = lax.broadcasted_iota(I32, (BAND, 2 * BAND), 0)
    c_io = lax.broadcasted_iota(I32, (BAND, 2 * BAND), 1)
    dist = BAND + r_io - c_io
    band = (dist >= 0) & (dist <= max_dist)
    band_first = band & ((c_io >= BAND) | (i > 0))
    col0 = lax.broadcasted_iota(I32, (1, 2 * BAND), 1) == 0
    vr = lax.broadcasted_iota(I32, (2 * BAND, 2 * LANES), 0)
    vc = lax.broadcasted_iota(I32, (2 * BAND, 2 * LANES), 1)
    sink_row = (vr == 0) & (vc < LANES)
    to_log2 = HEAD_DIM ** -0.5 * math.log2(math.e)
    units = [(sb, qb) for sb in range(rows // BAND) for qb in range(nqb)]

    def scores(u):
        sb, qb = units[u]
        kb = qb if nkb > 1 else 0
        q = q_ref[0, sb * BAND:(sb + 1) * BAND, qb * LANES:(qb + 1) * LANES].astype(F32) * to_log2
        q2 = jnp.concatenate([jnp.where(lo_half, q, 0.0), jnp.where(lo_half, 0.0, q)], axis=0).astype(BF16)
        return _dot_nt(q2, kbuf[sb * BAND:(sb + 2) * BAND, kb * LANES:(kb + 1) * LANES])

    ahead = 2
    pending = [scores(u) for u in range(min(ahead, len(units)))]
    for u, (sb, qb) in enumerate(units):
        if u + ahead < len(units):
            pending.append(scores(u + ahead))
        s2 = pending[u]
        pending[u] = None
        kb = qb if nkb > 1 else 0
        r0 = sb * BAND
        msk = band_first if sb == 0 else band
        halves = []
        for hh in range(2):
            if has_sink:
                fill = jnp.where(col0, sink_ref[qb * 2 + hh] * math.log2(math.e), NEG_INF)
            else:
                fill = NEG_INF
            halves.append(jnp.where(msk, s2[hh * BAND:(hh + 1) * BAND], fill))
        s2 = jnp.concatenate(halves, axis=0)
        m = jnp.max(s2, axis=1, keepdims=True)
        p = jnp.exp2(s2 - m).astype(BF16)
        vw = vbuf[r0:r0 + 2 * BAND, kb * 256:(kb + 1) * 256]
        if has_sink:
            vw = jnp.where(sink_row, jnp.zeros_like(vw), vw)
        pv = _dot(p, vw)
        den = pv[:, LANES:]
        out = pv[:, :LANES] / den
        o = jnp.where(lo_half, out[:BAND], out[BAND:])
        o_ref[0, r0:r0 + BAND, qb * LANES:(qb + 1) * LANES] = o.astype(BF16)
        if want_lse:
            lse2 = m + jnp.log2(den)
            ls = jnp.where(lo_half, lse2[:BAND], lse2[BAND:])
            hi = ls.astype(BF16)
            lo = (ls - hi.astype(F32)).astype(BF16)
            lse_ref[0, r0:r0 + BAND, qb * LANES:(qb + 1) * LANES] = hi
            lse_ref[0, r0:r0 + BAND, (nqb + qb) * LANES:(nqb + qb + 1) * LANES] = lo


def _band_call(arr, q_col, k_col, v_col, nqb, nkb, max_dist, sinks=None, want_lse=False, name="band"):
    ns, length, _ = arr.shape
    rows = min(BAND_ROWS, length)
    wq, wk = nqb * LANES, nkb * LANES
    rpb = rows // BAND
    cur = lambda col, w: pl.BlockSpec((1, rows, w), lambda n, i, c=col // w: (n, i, c))
    prev = lambda col, w: pl.BlockSpec(
        (1, BAND, w), lambda n, i, c=col // w: (n, jnp.maximum(i * rpb - 1, 0), c))
    in_specs = [cur(q_col, wq), prev(k_col, wk), cur(k_col, wk), prev(v_col, wk), cur(v_col, wk)]
    args = [arr] * 5
    if sinks is not None:
        in_specs = [pl.BlockSpec(memory_space=pltpu.SMEM)] + in_specs
        args = [sinks] + args
    out_specs = [pl.BlockSpec((1, rows, wq), lambda n, i: (n, i, 0))]
    out_shape = [jax.ShapeDtypeStruct((ns, length, wq), BF16)]
    if want_lse:
        out_specs.append(pl.BlockSpec((1, rows, 2 * wq), lambda n, i: (n, i, 0)))
        out_shape.append(jax.ShapeDtypeStruct((ns, length, 2 * wq), BF16))
    return pl.pallas_call(
        functools.partial(_band_kernel, nqb=nqb, nkb=nkb, max_dist=max_dist,
                          has_sink=sinks is not None, want_lse=want_lse),
        grid=(ns, length // rows),
        in_specs=in_specs,
        out_specs=out_specs,
        out_shape=out_shape,
        scratch_shapes=[pltpu.VMEM((BAND + rows, wk), BF16), pltpu.VMEM((BAND + rows, 2 * wk), BF16)],
        name=name,
    )(*args)


def _outproj_router_kernel(oa_ref, ob_ref, o1_ref, l1_ref, o4_ref, l4_ref, o16_ref, l16_ref, p4t_ref, p16t_ref,
                           w_ref, x_ref, g1_ref, sc_ref, sh_ref, g_ref, wr_ref, br_ref, tri_ref,
                           xo_ref, hs_ref, meta_ref, nch_ref, wb):
    @pl.when((pl.program_id(0) == 0) & (pl.program_id(1) == 0))
    def _():
        wb[0:256, :] = w_ref[0:256, :].astype(BF16)
        for pos, head in enumerate(_SWA_HEAD_ORDER):
            wb[256 + pos * HEAD_DIM:256 + (pos + 1) * HEAD_DIM, :] = (
                w_ref[256 + head * HEAD_DIM:256 + (head + 1) * HEAD_DIM, :].astype(BF16))
        wb[768:1024, :] = w_ref[768:1024, :].astype(BF16)

    tm = x_ref.shape[1]
    hw = o1_ref.shape[2]

    def lse_of(v):
        return v[:, :hw] + v[:, hw:]

    o1 = o1_ref[0].astype(F32)
    ls1 = lse_of(l1_ref[0].astype(F32))
    o4 = _dot(p4t_ref[...], o4_ref[0].reshape(tm, hw))
    ls4 = lse_of(_dot(p4t_ref[...], l4_ref[0].reshape(tm, 2 * hw)))
    o16 = _dot(p16t_ref[...], o16_ref[0].reshape(tm, hw))
    ls16 = lse_of(_dot(p16t_ref[...], l16_ref[0].reshape(tm, 2 * hw)))
    mx = jnp.maximum(jnp.maximum(ls1, ls4), ls16)
    e1, e4, e16 = jnp.exp2(ls1 - mx), jnp.exp2(ls4 - mx), jnp.exp2(ls16 - mx)
    oc = (e1 * o1 + e4 * o4 + e16 * o16) / (e1 + e4 + e16)
    mix = (_dot(oa_ref[0], wb[0:256, :]) + _dot(ob_ref[0], wb[256:768, :])
           + _dot(oc.astype(BF16), wb[768:1024, :]))
    x1 = x_ref[0] + g1_ref[0] * mix
    xo_ref[0] = x1
    _router_body(x1, sc_ref, sh_ref, g_ref, wr_ref, br_ref, tri_ref, hs_ref, meta_ref, nch_ref)


def _outproj_router_call(oa, ob, o1, l1, o4, l4, o16, l16, p4t, p16t, w_out, layer, x, g1, sc, sh, g, wr, br, tri):
    b, s, d = x.shape
    tm = TM
    tpb = s // tm
    nt = b * tpb
    row = lambda w_: pl.BlockSpec((1, tm, w_), lambda bi, i: (bi, i, 0))
    res = lambda dd, w_: pl.BlockSpec((1, dd, tm // dd, w_), lambda bi, i: (bi, 0, i, 0))
    const2 = lambda bi, i: (0, 0)
    per_b = lambda bi, i: (bi, 0, 0)
    flat = lambda bi, i: bi * tpb + i
    return pl.pallas_call(
        _outproj_router_kernel,
        grid=(b, tpb),
        in_specs=[
            row(256), row(512), row(256), row(512),
            res(4, 256), res(4, 512), res(16, 256), res(16, 512),
            pl.BlockSpec((tm, tm), const2), pl.BlockSpec((tm, tm), const2),
            pl.BlockSpec((None, d, d), lambda bi, i: (layer, 0, 0)),
            row(d),
            pl.BlockSpec((1, 1, d), per_b),
            pl.BlockSpec((1, 1, d), per_b),
            pl.BlockSpec((1, 1, d), per_b),
            pl.BlockSpec((1, d), const2),
            pl.BlockSpec((None, d, 2 * ROUTER_COLS), lambda bi, i: (layer, 0, 0)),
            pl.BlockSpec((None, 1, ROUTER_COLS), lambda bi, i: (layer, 0, 0)),
            pl.BlockSpec((tm, tm), const2),
        ],
        out_specs=[
            row(d),
            pl.BlockSpec((SLOTS, d), lambda bi, i: (flat(bi, i), 0)),
            pl.BlockSpec((1, 8, tm), lambda bi, i: (flat(bi, i), 0, 0)),
            pl.BlockSpec((1, N_EXPERTS, LANES), lambda bi, i: (flat(bi, i), 0, 0)),
        ],
        out_shape=[
            jax.ShapeDtypeStruct((b, s, d), F32),
            jax.ShapeDtypeStruct((nt * SLOTS, d), BF16),
            jax.ShapeDtypeStruct((nt, 8, tm), F32),
            jax.ShapeDtypeStruct((nt, N_EXPERTS, LANES), I32),
        ],
        scratch_shapes=[pltpu.VMEM((d, d), BF16)],
        compiler_params=pltpu.CompilerParams(vmem_limit_bytes=56 * 1024 * 1024),
        name="out_proj_router",
    )(oa, ob, o1, l1, o4, l4, o16, l16, p4t, p16t, w_out, x, g1, sc, sh, g, wr, br, tri)


ROUTER_COLS = LANES


def _router_body(x, sc_ref, sh_ref, g_ref, wr_ref, br_ref, tri_ref, hs_ref, meta_ref, nch_ref):
    tm = x.shape[0]
    h = _modulated_norm(x, g_ref[...], sc_ref[0], sh_ref[0])
    hb = h.astype(BF16)
    h_lo = (h - hb.astype(F32)).astype(BF16)
    part = _dot(hb, wr_ref[...]) + _dot(h_lo, wr_ref[...])
    logits = part[:, :ROUTER_COLS] + part[:, ROUTER_COLS:] + br_ref[...]
    lt = logits.T
    glog = lt[0:8]
    elog = lt[8:8 + N_EXPERTS]
    r8 = lax.broadcasted_iota(I32, (8, tm), 0)
    r16 = lax.broadcasted_iota(I32, (N_EXPERTS, tm), 0)

    gmax = jnp.max(glog, axis=0, keepdims=True)
    g_w = 1.0 / jnp.sum(jnp.exp(glog - gmax), axis=0, keepdims=True)
    g_idx = jnp.min(jnp.where(glog == gmax, r8, 99), axis=0, keepdims=True)

    el = jnp.where((r16 // EXPERTS_PER_GROUP) == g_idx, elog, NEG_INF)
    emax = jnp.max(el, axis=0, keepdims=True)
    e1 = jnp.min(jnp.where(el == emax, r16, 99), axis=0, keepdims=True)
    el2 = jnp.where(r16 == e1, NEG_INF, el)
    emax2 = jnp.max(el2, axis=0, keepdims=True)
    e2 = jnp.min(jnp.where(el2 == emax2, r16, 99), axis=0, keepdims=True)
    p2 = jnp.exp(emax2 - emax)
    wt1 = g_w / (1.0 + p2)
    wt2 = g_w * p2 / (1.0 + p2)

    oh1 = r16 == e1
    oh2 = r16 == e2
    onehot = jnp.where(oh1, 1.0, 0.0) + jnp.where(oh2, 1.0, 0.0)
    cnt = jnp.sum(onehot, axis=1, keepdims=True)
    nch = jnp.floor((cnt + (CHUNK - 1)) * (1.0 / CHUNK))
    nchb = jnp.broadcast_to(nch, (N_EXPERTS, LANES))
    rl = lax.broadcasted_iota(I32, (N_EXPERTS, LANES), 0)
    incl = nchb
    for sft in (1, 2, 4, 8):
        incl = incl + jnp.where(rl >= sft, pltpu.roll(incl, sft, 0), 0.0)
    off = (incl - nchb)[:, 0:1] * float(CHUNK)
    rank = _dot(onehot.astype(BF16), tri_ref[...])
    slot_of = off + rank
    pos1 = jnp.sum(jnp.where(oh1, slot_of, 0.0), axis=0, keepdims=True)
    pos2 = jnp.sum(jnp.where(oh2, slot_of, 0.0), axis=0, keepdims=True)

    slot = lax.broadcasted_iota(I32, (SLOTS, tm), 0)
    sel = jnp.where(slot == pos1.astype(I32), 1.0, jnp.where(slot == pos2.astype(I32), 1.0, 0.0))
    hs_ref[...] = _dot(sel.astype(BF16), hb).astype(BF16)

    meta_ref[0] = jnp.concatenate([pos1, pos2, wt1, wt2, jnp.zeros((4, tm), F32)], axis=0)
    nch_ref[0] = nchb.astype(I32)


def _ffn_schedule(nch, max_tiles, dump_base):
    nt = nch.shape[0]
    cend = jnp.cumsum(nch, axis=1)
    coff = cend - nch
    tcum = jnp.cumsum(nch, axis=0)
    before = tcum - nch
    tot = tcum[-1]
    pad = ((tot + FFN_CHUNKS - 1) // FFN_CHUNKS) * FFN_CHUNKS
    eend = jnp.cumsum(pad)
    estart = eend - pad
    n_tiles = (eend[-1] // FFN_CHUNKS).astype(I32)
    first_chunk = jnp.arange(max_tiles, dtype=I32) * FFN_CHUNKS
    tile_expert = jnp.sum((eend[None, :] <= first_chunk[:, None]).astype(I32), axis=1)
    tile_expert = jnp.minimum(tile_expert, N_EXPERTS - 1)
    hp = lax.Precision.HIGHEST
    pos = jnp.arange(max_tiles * FFN_CHUNKS, dtype=I32)
    e_s = jnp.minimum(jnp.sum((eend[None, :] <= (pos // FFN_CHUNKS * FFN_CHUNKS)[:, None]).astype(I32), axis=1),
                      N_EXPERTS - 1)
    oh_e = (e_s[:, None] == jnp.arange(N_EXPERTS, dtype=I32)[None, :]).astype(F32)
    idx = pos - jnp.dot(oh_e, estart.astype(F32), precision=hp).astype(I32)
    run_end = jnp.dot(oh_e, tcum.T.astype(F32), precision=hp).astype(I32)
    run_beg = jnp.dot(oh_e, before.T.astype(F32), precision=hp).astype(I32)
    run_off = jnp.dot(oh_e, coff.T.astype(F32), precision=hp).astype(I32)
    in_run = (idx[:, None] >= run_beg) & (idx[:, None] < run_end)
    tile_base = jnp.arange(nt, dtype=I32)[None, :] * CHUNKS_PER_TILE
    src = jnp.sum(jnp.where(in_run, tile_base + run_off + idx[:, None] - run_beg, 0), axis=1)
    real = jnp.any(in_run, axis=1)
    dump = dump_base + (pos // FFN_CHUNKS % 2) * FFN_CHUNKS + pos % FFN_CHUNKS
    src_rows = jnp.where(real, src, 0) * CHUNK
    dst_rows = jnp.where(real, src, dump) * CHUNK
    used = cend[:, -1]
    ucum = jnp.cumsum(CHUNKS_PER_TILE - used)
    ubeg = ucum - (CHUNKS_PER_TILE - used)
    z = jnp.arange(max_tiles * ZERO_CHUNKS, dtype=I32)[:, None]
    in_gap = (z >= ubeg[None, :]) & (z < ucum[None, :])
    zsrc = jnp.sum(jnp.where(in_gap, tile_base + used[None, :] + z - ubeg[None, :], 0), axis=1)
    zdump = dump_base + 2 * FFN_CHUNKS + z[:, 0] % ZERO_CHUNKS
    zero_rows = jnp.where(jnp.any(in_gap, axis=1), zsrc, zdump) * CHUNK
    return tile_expert, src_rows, dst_rows, n_tiles.reshape(1), zero_rows


def _ffn_kernel(te_ref, sr_ref, dr_ref, nt_ref, zr_ref, hs_hbm, wg_ref, wu_ref, wd_ref, ys_hbm,
                xbuf, ybuf, zbuf, wgb, wub, wdb, in_sem, out_sem, zero_sem, *, dump_base):
    j = pl.program_id(0)
    nt = nt_ref[0]
    half_ff = EXPERT_FF // 2

    def rows_at(r):
        return pl.ds(pl.multiple_of(r, CHUNK), CHUNK)

    def in_copy(step, slot, k, wait=False):
        r = 0 if wait else sr_ref[step * FFN_CHUNKS + k]
        return pltpu.make_async_copy(hs_hbm.at[rows_at(r), :], xbuf.at[slot, pl.ds(k * CHUNK, CHUNK), :],
                                     in_sem.at[slot])

    def out_copy(step, slot, k, wait=False):
        r = 0 if wait else dr_ref[step * FFN_CHUNKS + k]
        return pltpu.make_async_copy(ybuf.at[slot, pl.ds(k * CHUNK, CHUNK), :], ys_hbm.at[rows_at(r), :],
                                     out_sem.at[slot])

    def zero_copy(k, wait=False):
        r = 0 if wait else zr_ref[j * ZERO_CHUNKS + k]
        return pltpu.make_async_copy(zbuf.at[pl.ds(k * CHUNK, CHUNK), :], ys_hbm.at[rows_at(r), :], zero_sem)

    @pl.when(j == 0)
    def _():
        zbuf[...] = jnp.zeros_like(zbuf)
        fills = [pltpu.make_async_copy(
            zbuf, ys_hbm.at[pl.ds((dump_base + r * ZERO_CHUNKS) * CHUNK, ZERO_CHUNKS * CHUNK), :], zero_sem)
            for r in range(DUMP_CHUNKS // ZERO_CHUNKS)]
        for cp in fills:
            cp.start()
        for cp in fills:
            cp.wait()

    for k in range(ZERO_CHUNKS):
        zero_copy(k).start()

    @pl.when(j < nt)
    def _():
        slot = j % 2

        @pl.when(j == 0)
        def _():
            for k in range(FFN_CHUNKS):
                in_copy(0, 0, k).start()

        @pl.when(j + 1 < nt)
        def _():
            for k in range(FFN_CHUNKS):
                in_copy(j + 1, 1 - slot, k).start()

        @pl.when((j == 0) | (te_ref[j] != te_ref[jnp.maximum(j - 1, 0)]))
        def _():
            wgb[...] = wg_ref[0].astype(BF16)
            wub[...] = wu_ref[0].astype(BF16)
            wdb[...] = wd_ref[0].astype(BF16)

        for k in range(FFN_CHUNKS):
            in_copy(j, slot, k, wait=True).wait()

        @pl.when(j >= 2)
        def _():
            for k in range(FFN_CHUNKS):
                out_copy(j - 2, slot, k, wait=True).wait()

        x = xbuf[slot]
        hg = [_dot(x, wgb[:, h * half_ff:(h + 1) * half_ff]) for h in range(2)]
        hu = [_dot(x, wub[:, h * half_ff:(h + 1) * half_ff]) for h in range(2)]
        y = None
        for h in range(2):
            act = ((hg[h] / (1.0 + jnp.exp(-hg[h]))) * hu[h]).astype(BF16)
            part = _dot(act, wdb[h * half_ff:(h + 1) * half_ff, :])
            y = part if y is None else y + part
        ybuf[slot] = y.astype(BF16)
        for k in range(FFN_CHUNKS):
            out_copy(j, slot, k).start()

        @pl.when(j == nt - 1)
        def _():
            for k in range(FFN_CHUNKS):
                out_copy(j, slot, k, wait=True).wait()

            @pl.when(j >= 1)
            def _():
                for k in range(FFN_CHUNKS):
                    out_copy(j - 1, 1 - slot, k, wait=True).wait()

    for k in range(ZERO_CHUNKS):
        zero_copy(k, wait=True).wait()


def _ffn_call(tile_expert, src_rows, dst_rows, n_tiles, zero_rows, hs, wg, wu, wd, layer, max_tiles):
    rows, d = hs.shape
    ff = wg.shape[-1]
    wmap = lambda j, te, sr, dr, nt, zr: (layer, te[j], 0, 0)
    grid_spec = pltpu.PrefetchScalarGridSpec(
        num_scalar_prefetch=5,
        grid=(max_tiles,),
        in_specs=[
            pl.BlockSpec(memory_space=pl.ANY),
            pl.BlockSpec((None, 1, d, ff), wmap),
            pl.BlockSpec((None, 1, d, ff), wmap),
            pl.BlockSpec((None, 1, ff, d), wmap),
        ],
        out_specs=pl.BlockSpec(memory_space=pl.ANY),
        scratch_shapes=[
            pltpu.VMEM((2, FFN_ROWS, d), BF16),
            pltpu.VMEM((2, FFN_ROWS, d), BF16),
            pltpu.VMEM((ZERO_CHUNKS * CHUNK, d), BF16),
            pltpu.VMEM((d, ff), BF16),
            pltpu.VMEM((d, ff), BF16),
            pltpu.VMEM((ff, d), BF16),
            pltpu.SemaphoreType.DMA((2,)),
            pltpu.SemaphoreType.DMA((2,)),
            pltpu.SemaphoreType.DMA(()),
        ],
    )
    return pl.pallas_call(
        functools.partial(_ffn_kernel, dump_base=rows // CHUNK),
        grid_spec=grid_spec,
        out_shape=jax.ShapeDtypeStruct((rows + DUMP_CHUNKS * CHUNK, d), BF16),
        compiler_params=pltpu.CompilerParams(vmem_limit_bytes=48 * 1024 * 1024),
        name="expert_ffn",
    )(tile_expert, src_rows, dst_rows, n_tiles, zero_rows, hs, wg, wu, wd)


def _combine_body(ys_ref, meta_ref, x_ref, g2_ref):
    tm = x_ref.shape[1]
    meta = meta_ref[0]
    eye = (lax.broadcasted_iota(I32, (tm, tm), 0) == lax.broadcasted_iota(I32, (tm, tm), 1))

    def as_col(row):
        return jnp.sum(jnp.where(eye, row, 0.0), axis=1, keepdims=True)

    pos1, pos2 = as_col(meta[0:1]), as_col(meta[1:2])
    w1, w2 = as_col(meta[2:3]), as_col(meta[3:4])
    slot = lax.broadcasted_iota(I32, (tm, SLOTS), 1).astype(F32)
    gate = jnp.where(slot == pos1, w1, 0.0) + jnp.where(slot == pos2, w2, 0.0)
    y = _dot(gate.astype(BF16), ys_ref[...])
    return x_ref[0] + g2_ref[0] * y


def _combine_final_kernel(ys_ref, meta_ref, x_ref, g2_ref, gf_ref, xo_ref):
    xo = _combine_body(ys_ref, meta_ref, x_ref, g2_ref)
    xo_ref[0] = xo * lax.rsqrt(jnp.mean(xo * xo, axis=-1, keepdims=True) + NORM_EPS) * gf_ref[...]


def _combine_inproj_kernel(ys_ref, meta_ref, x_ref, g2_ref, *refs):
    n_in = 10
    xo_ref = refs[n_in]
    xo = _combine_body(ys_ref, meta_ref, x_ref, g2_ref)
    xo_ref[0] = xo
    _inproj_body(xo, *refs[:n_in], *refs[n_in + 1:])


def _combine_specs(d, tiles_per_batch):
    flat = lambda bi, i: bi * tiles_per_batch + i
    return [
        pl.BlockSpec((SLOTS, d), lambda bi, i: (flat(bi, i), 0)),
        pl.BlockSpec((1, 8, TM), lambda bi, i: (flat(bi, i), 0, 0)),
        pl.BlockSpec((1, TM, d), lambda bi, i: (bi, i, 0)),
        pl.BlockSpec((1, 1, d), lambda bi, i: (bi, 0, 0)),
    ]


def _combine_final_call(ys, meta, x, g2, gf):
    b, s, d = x.shape
    return pl.pallas_call(
        _combine_final_kernel,
        grid=(b, s // TM),
        in_specs=_combine_specs(d, s // TM) + [pl.BlockSpec((1, d), lambda bi, i: (0, 0))],
        out_specs=pl.BlockSpec((1, TM, d), lambda bi, i: (bi, i, 0)),
        out_shape=jax.ShapeDtypeStruct((b, s, d), F32),
        compiler_params=pltpu.CompilerParams(vmem_limit_bytes=48 * 1024 * 1024),
        name="moe_combine",
    )(ys, meta, x, g2, gf)


def _combine_inproj_call(ys, meta, x, g2, sc, sh, g, w_in, layer, tabs, p4, p16):
    b, s, d = x.shape
    in_specs, out_specs, out_shape = _inproj_specs(b, s, d, layer)
    xspec = pl.BlockSpec((1, TM, d), lambda bi, i: (bi, i, 0))
    return pl.pallas_call(
        _combine_inproj_kernel,
        grid=(b, s // TM),
        in_specs=_combine_specs(d, s // TM) + in_specs,
        out_specs=[xspec] + out_specs,
        out_shape=[jax.ShapeDtypeStruct((b, s, d), F32)] + out_shape,
        scratch_shapes=[pltpu.VMEM((d, IN_WIDTH), BF16)],
        compiler_params=pltpu.CompilerParams(vmem_limit_bytes=56 * 1024 * 1024),
        name="combine_in_proj",
    )(ys, meta, x, g2, sc, sh, g, w_in, *tabs, p4, p16)


def _rope_tables(positions):
    pos = positions.astype(F32)[..., None]

    def table(dim):
        inv = ROPE_THETA ** (-jnp.arange(0, dim, 2, dtype=F32) / dim)
        ang = pos * inv
        cos, sin = jnp.cos(ang), jnp.sin(ang)
        reps = LANES // dim
        return (jnp.tile(jnp.concatenate([cos, cos], -1), (1, 1, reps)),
                jnp.tile(jnp.concatenate([-sin, sin], -1), (1, 1, reps)))

    c64, s64 = table(HEAD_DIM)
    c32, s32 = table(DIFF_QK_DIM)
    return c64, s64, c32, s32


def kernel(x, c, positions, ada_w, ada_b, norm_mix_g, norm_ffn_g, w_in, w_out, diff_lambda_q1, diff_lambda_k1,
           diff_lambda_q2, diff_lambda_k2, diff_subln_g, swa_sinks, router_group_w, router_group_b,
           router_expert_w, router_expert_b, expert_w_gate, expert_w_up, expert_w_down, final_norm_g):
    b, s, d = x.shape
    depth = ada_w.shape[0]
    n = b * s
    nt = n // TM
    max_tiles = (nt * CHUNKS_PER_TILE + N_EXPERTS * (FFN_CHUNKS - 1)) // FFN_CHUNKS + 1

    tabs = _rope_tables(positions)
    p4 = _residue_perm(TM, 4)
    p16 = _residue_perm(TM, 16)
    p4_b, p16_b = jnp.asarray(p4, BF16), jnp.asarray(p16, BF16)
    p4t_b, p16t_b = jnp.asarray(p4.T, BF16), jnp.asarray(p16.T, BF16)
    tri = jnp.asarray(np.triu(np.ones((TM, TM), np.float32), 1), BF16)

    c_pad = jnp.pad(c, ((0, 8 - b), (0, 0)))
    mod = _ada_call(c_pad, ada_w, ada_b)[:, :b]

    sink_order = np.asarray(_SWA_HEAD_ORDER, np.int32)
    zpad = lambda k: jnp.zeros((depth, d, k), F32)
    wr = jnp.concatenate([router_group_w, zpad(8 - N_GROUPS), router_expert_w,
                          zpad(ROUTER_COLS - 8 - N_EXPERTS)], axis=-1)
    wr_hi = wr.astype(BF16)
    wr = jnp.concatenate([wr_hi, (wr - wr_hi.astype(F32)).astype(BF16)], axis=-1)
    br = jnp.concatenate([router_group_b, jnp.full((depth, 8 - N_GROUPS), NEG_INF, F32), router_expert_b,
                          jnp.zeros((depth, ROUTER_COLS - 8 - N_EXPERTS), F32)], axis=-1).reshape(depth, 1, -1)

    lam_init = [0.8 - 0.6 * math.exp(-0.3 * l) for l in range(depth)]
    mods = [[mod[l, :, k * d:(k + 1) * d].reshape(b, 1, d) for k in range(N_ADA)] for l in range(depth)]
    sh1, sc1 = mods[0][0], mods[0][1]
    proj, vat, qkv4, qkv16 = _inproj_call(x, sc1, sh1, norm_mix_g[0].reshape(1, d), w_in, 0, tabs, p4_b, p16_b)
    for l in range(depth):
        _, _, g1, sh2, sc2, g2 = mods[l]
        lam = (jnp.exp(jnp.sum(diff_lambda_q1[l] * diff_lambda_k1[l]))
               - jnp.exp(jnp.sum(diff_lambda_q2[l] * diff_lambda_k2[l])) + lam_init[l]).reshape(1)
        g_cols = jnp.broadcast_to(diff_subln_g[l][:, None], (HEAD_DIM, TQ))
        oa = _diff_attn_call(proj, vat, lam, g_cols, lam_init[l])

        ob = _band_call(proj, COL_QB, COL_KB, COL_VB, nqb=4, nkb=1, max_dist=SWA_WINDOW - 1,
                        sinks=swa_sinks[l][sink_order], name="swa")[0]
        o1, l1 = _band_call(proj, COL_QC, COL_KC, COL_VC, nqb=2, nkb=2, max_dist=BAND, want_lse=True,
                            name="dil1")
        o4, l4 = _band_call(qkv4.reshape(b * 4, s // 4, 768), 0, 256, 512, nqb=2, nkb=2, max_dist=BAND,
                            want_lse=True, name="dil4")
        o16, l16 = _band_call(qkv16.reshape(b * 16, s // 16, 768), 0, 256, 512, nqb=2, nkb=2, max_dist=BAND,
                              want_lse=True, name="dil16")
        x, hs, meta, nch = _outproj_router_call(
            oa, ob, o1, l1, o4.reshape(b, 4, s // 4, 256), l4.reshape(b, 4, s // 4, 512),
            o16.reshape(b, 16, s // 16, 256), l16.reshape(b, 16, s // 16, 512), p4t_b, p16t_b, w_out, l, x, g1,
            sc2, sh2, norm_ffn_g[l].reshape(1, d), wr, br, tri)
        sched = _ffn_schedule(nch[:, :, 0], max_tiles, nt * CHUNKS_PER_TILE)
        ys = _ffn_call(*sched, hs, expert_w_gate, expert_w_up, expert_w_down, l, max_tiles)
        if l + 1 < depth:
            sh1, sc1 = mods[l + 1][0], mods[l + 1][1]
            x, proj, vat, qkv4, qkv16 = _combine_inproj_call(
                ys, meta, x, g2, sc1, sh1, norm_mix_g[l + 1].reshape(1, d), w_in, l + 1, tabs, p4_b, p16_b)
        else:
            x = _combine_final_call(ys, meta, x, g2, final_norm_g.reshape(1, d))
    return x
```

```python
import functools
import math

import numpy as np
import jax
import jax.numpy as jnp
from jax import lax
from jax.experimental import pallas as pl
from jax.experimental.pallas import tpu as pltpu

F32 = jnp.float32
BF16 = jnp.bfloat16
I32 = jnp.int32

HEAD_DIM = 64
ROPE_THETA = 10000.0
NORM_EPS = 1e-6
NEG_INF = -1e30
DIFF_HEADS = 4
DIFF_QK_DIM = 32
SWA_Q_HEADS = 8
SWA_KV_HEADS = 2
SWA_WINDOW = 128
DIL_PATTERNS = ((128, 1), (512, 4), (2048, 16))
N_GROUPS = 4
EXPERTS_PER_GROUP = 4
N_EXPERTS = 16
EXPERT_FF = 512
N_ADA = 6
IN_WIDTH = 2304

LANES = 128
BF16_ROWS = 16
BAND = 128

TM = 256
CHUNK = BF16_ROWS
SLOTS = 2 * TM + N_EXPERTS * CHUNK
CHUNKS_PER_TILE = SLOTS // CHUNK
FFN_ROWS = 512
FFN_CHUNKS = FFN_ROWS // CHUNK
ZERO_CHUNKS = -(-N_EXPERTS * FFN_CHUNKS // CHUNKS_PER_TILE)
DUMP_CHUNKS = -(-(2 * FFN_CHUNKS + ZERO_CHUNKS) // ZERO_CHUNKS) * ZERO_CHUNKS
TQ = 512
BAND_ROWS = 512

_SWA_HEAD_ORDER = (0, 4, 1, 5, 2, 6, 3, 7)
COL_QB, COL_KB, COL_VB = 0, 512, 640
COL_Q1, COL_Q2, COL_K1, COL_K2, COL_VA = 768, 896, 1024, 1152, 1280
COL_QC, COL_KC, COL_VC = 1536, 1792, 2048
_SRC_ROPE = (32, 32, 32, 32, 0, 0, 64, 64, 64, 64, 64, 0, 64, 64, 64, 64, 0, 0)
_SRC_DEST = (6, 7, 8, 9, 10, 11, None, None, None, None, 4, 5, 12, 13, 14, 15, 16, 17)
VT_ROWS = HEAD_DIM + BF16_ROWS
_SRC_VA_CHUNK = 2


def _residue_perm(tm, d):
    p = np.zeros((tm, tm), np.float32)
    per = tm // d
    for l in range(per):
        for r in range(d):
            p[r * per + l, l * d + r] = 1.0
    return p


def _dot(a, b, **kw):
    return jnp.dot(a, b, preferred_element_type=F32, **kw)


def _dot_nt(a, b):
    return lax.dot_general(a, b, (((1,), (1,)), ((), ())), preferred_element_type=F32)


def _batch_row(ref):
    return ref[pl.ds(pl.program_id(0), 1), :]


def _modulated_norm(x, g, sc, sh):
    y = x * lax.rsqrt(jnp.mean(x * x, axis=-1, keepdims=True) + NORM_EPS)
    return (y * g) * (1.0 + sc) + sh


def _ada_kernel(c_ref, w_ref, b_ref, o_ref):
    c = c_ref[...]
    ca = c / (1.0 + jnp.exp(-c))
    o_ref[0] = _dot(ca, w_ref[0], precision=lax.Precision.HIGHEST) + b_ref[0]


def _ada_call(c_pad, ada_w, ada_b):
    depth, d, n = ada_w.shape
    tn = 1536
    return pl.pallas_call(
        _ada_kernel,
        grid=(depth, n // tn),
        in_specs=[
            pl.BlockSpec((c_pad.shape[0], d), lambda l, j: (0, 0)),
            pl.BlockSpec((1, d, tn), lambda l, j: (l, 0, j)),
            pl.BlockSpec((1, 1, tn), lambda l, j: (l, 0, j)),
        ],
        out_specs=pl.BlockSpec((1, c_pad.shape[0], tn), lambda l, j: (l, 0, j)),
        out_shape=jax.ShapeDtypeStruct((depth, c_pad.shape[0], n), F32),
        compiler_params=pltpu.CompilerParams(vmem_limit_bytes=40 * 1024 * 1024),
        name="ada_mod",
    )(c_pad, ada_w, ada_b.reshape(depth, 1, n))


def _rope(t, cos, sin_signed, first_half, half):
    rot = jnp.where(first_half, pltpu.roll(t, LANES - half, 1), pltpu.roll(t, half, 1))
    return t * cos + rot * sin_signed


def _inproj_body(x, sc_ref, sh_ref, g_ref, w_ref, cs64_ref, sn64_ref, cs32_ref, sn32_ref,
                 p4_ref, p16_ref, proj_ref, vat_ref, c4_ref, c16_ref, wb):
    @pl.when((pl.program_id(0) == 0) & (pl.program_id(1) == 0))
    def _():
        wb[...] = w_ref[...].astype(BF16)

    h = _modulated_norm(x, g_ref[...], _batch_row(sc_ref), _batch_row(sh_ref))
    hb = h.astype(BF16)
    lane = lax.broadcasted_iota(I32, (1, LANES), 1)
    first64 = (lane % 64) < 32
    first32 = (lane % 32) < 16
    lo_half = lane < 64
    swa_q = []
    for cb in range(IN_WIDTH // 256):
        acc = _dot(hb, wb[:, cb * 256:(cb + 1) * 256])
        if cb == _SRC_VA_CHUNK:
            acc_t = acc.T.astype(BF16)
            for hd in range(DIFF_HEADS):
                vat_ref[0, hd * VT_ROWS:hd * VT_ROWS + HEAD_DIM, :] = acc_t[hd * HEAD_DIM:(hd + 1) * HEAD_DIM]
                vat_ref[0, hd * VT_ROWS + HEAD_DIM:(hd + 1) * VT_ROWS, :] = jnp.ones(
                    (BF16_ROWS, acc_t.shape[1]), BF16)
        for half in range(2):
            src = cb * 2 + half
            t = acc[:, half * LANES:(half + 1) * LANES]
            if _SRC_ROPE[src] == 64:
                t = _rope(t, cs64_ref[0], sn64_ref[0], first64, 32)
            elif _SRC_ROPE[src] == 32:
                t = _rope(t, cs32_ref[0], sn32_ref[0], first32, 16)
            dst = _SRC_DEST[src]
            if dst is None:
                swa_q.append(t)
            else:
                proj_ref[0, :, dst * LANES:(dst + 1) * LANES] = t.astype(BF16)
    for jb in range(SWA_Q_HEADS // 2):
        a, c = swa_q[jb // 2], swa_q[2 + jb // 2]
        if jb % 2 == 0:
            blk = jnp.where(lo_half, a, pltpu.roll(c, 64, 1))
        else:
            blk = jnp.where(lo_half, pltpu.roll(a, 64, 1), c)
        proj_ref[0, :, jb * LANES:(jb + 1) * LANES] = blk.astype(BF16)
    cc = proj_ref[0, :, COL_QC:]
    tm = cc.shape[0]
    c4 = _dot(p4_ref[...], cc).astype(BF16)
    for r in range(4):
        c4_ref[0, r] = c4[r * (tm // 4):(r + 1) * (tm // 4)]
    c16 = _dot(p16_ref[...], cc).astype(BF16)
    for r in range(16):
        c16_ref[0, r] = c16[r * (tm // 16):(r + 1) * (tm // 16)]


def _inproj_kernel(x_ref, *refs):
    _inproj_body(x_ref[0], *refs)


def _mod_spec(mod, layer, k):
    return pl.BlockSpec((None, mod.shape[1], mod.shape[2] // N_ADA), lambda bi, i: (layer, 0, k))


def _inproj_specs(b, s, d, layer, mod):
    tm = TM
    row = lambda bi, i: (bi, i, 0)
    const2 = lambda bi, i: (0, 0)
    in_specs = [
        _mod_spec(mod, layer, 1),
        _mod_spec(mod, layer, 0),
        pl.BlockSpec((None, 1, d), lambda bi, i: (layer, 0, 0)),
        pl.BlockSpec((None, d, IN_WIDTH), lambda bi, i: (layer, 0, 0)),
        pl.BlockSpec((1, tm, LANES), row),
        pl.BlockSpec((1, tm, LANES), row),
        pl.BlockSpec((1, tm, LANES), row),
        pl.BlockSpec((1, tm, LANES), row),
        pl.BlockSpec((tm, tm), const2),
        pl.BlockSpec((tm, tm), const2),
    ]
    out_specs = [
        pl.BlockSpec((1, tm, IN_WIDTH), row),
        pl.BlockSpec((1, DIFF_HEADS * VT_ROWS, tm), lambda bi, i: (bi, 0, i)),
        pl.BlockSpec((1, 4, tm // 4, 768), lambda bi, i: (bi, 0, i, 0)),
        pl.BlockSpec((1, 16, tm // 16, 768), lambda bi, i: (bi, 0, i, 0)),
    ]
    out_shape = [
        jax.ShapeDtypeStruct((b, s, IN_WIDTH), BF16),
        jax.ShapeDtypeStruct((b, DIFF_HEADS * VT_ROWS, s), BF16),
        jax.ShapeDtypeStruct((b, 4, s // 4, 768), BF16),
        jax.ShapeDtypeStruct((b, 16, s // 16, 768), BF16),
    ]
    return in_specs, out_specs, out_shape


def _inproj_call(x, mod, g, w_in, layer, tabs, p4, p16):
    b, s, d = x.shape
    in_specs, out_specs, out_shape = _inproj_specs(b, s, d, layer, mod)
    return pl.pallas_call(
        _inproj_kernel,
        grid=(b, s // TM),
        in_specs=[pl.BlockSpec((1, TM, d), lambda bi, i: (bi, i, 0))] + in_specs,
        out_specs=out_specs,
        out_shape=out_shape,
        scratch_shapes=[pltpu.VMEM((d, IN_WIDTH), BF16)],
        compiler_params=pltpu.CompilerParams(vmem_limit_bytes=56 * 1024 * 1024),
        name="in_proj",
    )(x, mod, mod, g, w_in, *tabs, p4, p16)


def _diff_attn_kernel(lam_ref, q1_ref, q2_ref, q1n_ref, q2n_ref, k1_ref, k2_ref, vt_ref, g_ref, o_ref,
                      m_sc, acc_sc, s_sc, *, lambda_init, layer):
    tq = q1_ref.shape[1]
    qi = pl.program_id(1)
    lam = lam_ref[layer]
    to_log2 = DIFF_QK_DIM ** -0.5 * math.log2(math.e)
    lane = lax.broadcasted_iota(I32, (1, LANES), 1)

    def head_queries(qa_ref, qb_ref):
        qa = qa_ref[0].astype(F32) * to_log2
        qb = qb_ref[0].astype(F32) * to_log2
        out = []
        for h in range(DIFF_HEADS):
            hm = (lane // DIFF_QK_DIM) == h
            out.append((jnp.where(hm, qa, 0.0).astype(BF16), jnp.where(hm, qb, 0.0).astype(BF16)))
        return out

    qh = head_queries(q1_ref, q2_ref)
    qh_next = head_queries(q1n_ref, q2n_ref)
    causal = (lax.broadcasted_iota(I32, (tq, tq), 0) <= lax.broadcasted_iota(I32, (tq, tq), 1))

    m_sc[...] = jnp.full(m_sc.shape, NEG_INF, F32)
    acc_sc[...] = jnp.zeros(acc_sc.shape, F32)

    n_chain = 2 * DIFF_HEADS

    def scores(ch, tile, queries):
        start = pl.multiple_of(tile * tq, tq)
        k_ref = k1_ref if ch % 2 == 0 else k2_ref
        return _dot_nt(k_ref[0, pl.ds(start, tq), :], queries[ch // 2][ch % 2])

    @pl.when(qi == 0)
    def _():
        for ch in range(n_chain):
            s_sc[ch] = scores(ch, 0, qh)

    def step(j, last):
        start = pl.multiple_of(j * tq, tq)
        for ch in range(n_chain):
            st = s_sc[ch]
            s_sc[ch] = scores(ch, 0, qh_next) if last else scores(ch, j + 1, qh)
            h = ch // 2
            vt = vt_ref[0, h * VT_ROWS:(h + 1) * VT_ROWS, pl.ds(start, tq)]
            if last:
                st = jnp.where(causal, st, NEG_INF)
            m_old = m_sc[ch]
            m_new = jnp.maximum(m_old, jnp.max(st, axis=0, keepdims=True))
            p = jnp.exp2(st - m_new)
            al = jnp.exp2(m_old - m_new)
            acc_sc[ch] = al * acc_sc[ch] + _dot(vt, p.astype(BF16))
            m_sc[ch] = m_new

    def body(j, carry):
        step(j, False)
        return carry

    lax.fori_loop(0, qi, body, 0)
    step(qi, True)

    g = g_ref[...]
    outs = []
    for h in range(DIFF_HEADS):
        a1, a2 = acc_sc[2 * h], acc_sc[2 * h + 1]
        o = (a1[:HEAD_DIM] / a1[HEAD_DIM:HEAD_DIM + 1]
             - lam * (a2[:HEAD_DIM] / a2[HEAD_DIM:HEAD_DIM + 1]))
        ms = jnp.mean(o * o, axis=0, keepdims=True)
        outs.append((o * lax.rsqrt(ms + NORM_EPS)) * g * (1.0 - lambda_init))
    o_ref[0] = jnp.concatenate(outs, axis=0).T.astype(BF16)


def _diff_attn_call(proj, vat, lam, g_cols, lambda_init, layer):
    b, s, _ = proj.shape
    tq = TQ
    qspec = lambda cb: pl.BlockSpec((1, tq, LANES), lambda bi, i, cb=cb: (bi, i, cb))
    last_q = s // tq - 1
    qnext = lambda cb: pl.BlockSpec((1, tq, LANES), lambda bi, i, cb=cb: (bi, jnp.minimum(i + 1, last_q), cb))
    kspec = lambda cb: pl.BlockSpec((1, s, LANES), lambda bi, i, cb=cb: (bi, 0, cb))
    n_chain = 2 * DIFF_HEADS
    return pl.pallas_call(
        functools.partial(_diff_attn_kernel, lambda_init=lambda_init, layer=layer),
        grid=(b, s // tq),
        in_specs=[
            pl.BlockSpec(memory_space=pltpu.SMEM),
            qspec(COL_Q1 // LANES), qspec(COL_Q2 // LANES),
            qnext(COL_Q1 // LANES), qnext(COL_Q2 // LANES),
            kspec(COL_K1 // LANES), kspec(COL_K2 // LANES),
            pl.BlockSpec((1, DIFF_HEADS * VT_ROWS, s), lambda bi, i: (bi, 0, 0)),
            pl.BlockSpec((None, HEAD_DIM, tq), lambda bi, i: (layer, 0, 0)),
        ],
        out_specs=pl.BlockSpec((1, tq, 256), lambda bi, i: (bi, i, 0)),
        out_shape=jax.ShapeDtypeStruct((b, s, 256), BF16),
        scratch_shapes=[
            pltpu.VMEM((n_chain, 1, tq), F32),
            pltpu.VMEM((n_chain, VT_ROWS, tq), F32),
            pltpu.VMEM((n_chain, tq, tq), F32),
        ],
        compiler_params=pltpu.CompilerParams(vmem_limit_bytes=48 * 1024 * 1024),
        name="diff_attn",
    )(lam, proj, proj, proj, proj, proj, proj, vat, g_cols)


def _band_kernel(*refs, nqb, nkb, max_dist, has_sink, want_lse, sink_base):
    it = iter(refs)
    sink_ref = next(it) if has_sink else None
    q_ref, kp_ref, kc_ref, vp_ref, vc_ref = (next(it) for _ in range(5))
    o_ref = next(it)
    lse_ref = next(it) if want_lse else None
    kbuf, vbuf = next(it), next(it)
    rows = q_ref.shape[1]
    i = pl.program_id(1)
    kbuf[0:BAND, :] = kp_ref[0]
    kbuf[BAND:, :] = kc_ref[0]
    for kb in range(nkb):
        vbuf[0:BAND, kb * 256:kb * 256 + LANES] = vp_ref[0, :, kb * LANES:(kb + 1) * LANES]
        vbuf[BAND:, kb * 256:kb * 256 + LANES] = vc_ref[0, :, kb * LANES:(kb + 1) * LANES]
        vbuf[:, kb * 256 + LANES:(kb + 1) * 256] = jnp.ones((BAND + rows, LANES), BF16)
    lane = lax.broadcasted_iota(I32, (1, LANES), 1)
    lo_half = lane < 64
    r_io = lax.broadcasted_iota(I32, (BAND, 2 * BAND), 0)
    c_io = lax.broadcasted_iota(I32, (BAND, 2 * BAND), 1)
    dist = BAND + r_io - c_io
    band = (dist >= 0) & (dist <= max_dist)
    band_first = band & ((c_io >= BAND) | (i > 0))
    col0 = lax.broadcasted_iota(I32, (1, 2 * BAND), 1) == 0
    vr = lax.broadcasted_iota(I32, (2 * BAND, 2 * LANES), 0)
    vc = lax.broadcasted_iota(I32, (2 * BAND, 2 * LANES), 1)
    sink_row = (vr == 0) & (vc < LANES)
    to_log2 = HEAD_DIM ** -0.5 * math.log2(math.e)
    units = [(sb, qb) for sb in range(rows // BAND) for qb in range(nqb)]

    def scores(u):
        sb, qb = units[u]
        kb = qb if nkb > 1 else 0
        q = q_ref[0, sb * BAND:(sb + 1) * BAND, qb * LANES:(qb + 1) * LANES].astype(F32) * to_log2
        q2 = jnp.concatenate([jnp.where(lo_half, q, 0.0), jnp.where(lo_half, 0.0, q)], axis=0).astype(BF16)
        return _dot_nt(q2, kbuf[sb * BAND:(sb + 2) * BAND, kb * LANES:(kb + 1) * LANES])

    ahead = 2
    pending = [scores(u) for u in range(min(ahead, len(units)))]
    for u, (sb, qb) in enumerate(units):
        if u + ahead < len(units):
            pending.append(scores(u + ahead))
        s2 = pending[u]
        pending[u] = None
        kb = qb if nkb > 1 else 0
        r0 = sb * BAND
        msk = band_first if sb == 0 else band
        halves = []
        for hh in range(2):
            if has_sink:
                fill = jnp.where(col0, sink_ref[sink_base + qb * 2 + hh] * math.log2(math.e), NEG_INF)
            else:
                fill = NEG_INF
            halves.append(jnp.where(msk, s2[hh * BAND:(hh + 1) * BAND], fill))
        s2 = jnp.concatenate(halves, axis=0)
        m = jnp.max(s2, axis=1, keepdims=True)
        p = jnp.exp2(s2 - m).astype(BF16)
        vw = vbuf[r0:r0 + 2 * BAND, kb * 256:(kb + 1) * 256]
        if has_sink:
            vw = jnp.where(sink_row, jnp.zeros_like(vw), vw)
        pv = _dot(p, vw)
        den = pv[:, LANES:]
        out = pv[:, :LANES] / den
        o = jnp.where(lo_half, out[:BAND], out[BAND:])
        o_ref[0, r0:r0 + BAND, qb * LANES:(qb + 1) * LANES] = o.astype(BF16)
        if want_lse:
            lse2 = m + jnp.log2(den)
            ls = jnp.where(lo_half, lse2[:BAND], lse2[BAND:])
            hi = ls.astype(BF16)
            lo = (ls - hi.astype(F32)).astype(BF16)
            lse_ref[0, r0:r0 + BAND, qb * LANES:(qb + 1) * LANES] = hi
            lse_ref[0, r0:r0 + BAND, (nqb + qb) * LANES:(nqb + qb + 1) * LANES] = lo


def _band_call(arr, q_col, k_col, v_col, nqb, nkb, max_dist, sinks=None, sink_base=0, want_lse=False,
               name="band"):
    ns, length, _ = arr.shape
    rows = min(BAND_ROWS, length)
    wq, wk = nqb * LANES, nkb * LANES
    rpb = rows // BAND
    cur = lambda col, w: pl.BlockSpec((1, rows, w), lambda n, i, c=col // w: (n, i, c))
    prev = lambda col, w: pl.BlockSpec(
        (1, BAND, w), lambda n, i, c=col // w: (n, jnp.maximum(i * rpb - 1, 0), c))
    in_specs = [cur(q_col, wq), prev(k_col, wk), cur(k_col, wk), prev(v_col, wk), cur(v_col, wk)]
    args = [arr] * 5
    if sinks is not None:
        in_specs = [pl.BlockSpec(memory_space=pltpu.SMEM)] + in_specs
        args = [sinks] + args
    out_specs = [pl.BlockSpec((1, rows, wq), lambda n, i: (n, i, 0))]
    out_shape = [jax.ShapeDtypeStruct((ns, length, wq), BF16)]
    if want_lse:
        out_specs.append(pl.BlockSpec((1, rows, 2 * wq), lambda n, i: (n, i, 0)))
        out_shape.append(jax.ShapeDtypeStruct((ns, length, 2 * wq), BF16))
    return pl.pallas_call(
        functools.partial(_band_kernel, nqb=nqb, nkb=nkb, max_dist=max_dist,
                          has_sink=sinks is not None, want_lse=want_lse, sink_base=sink_base),
        grid=(ns, length // rows),
        in_specs=in_specs,
        out_specs=out_specs,
        out_shape=out_shape,
        scratch_shapes=[pltpu.VMEM((BAND + rows, wk), BF16), pltpu.VMEM((BAND + rows, 2 * wk), BF16)],
        name=name,
    )(*args)


def _outproj_router_kernel(oa_ref, ob_ref, o1_ref, l1_ref, o4_ref, l4_ref, o16_ref, l16_ref, p4t_ref, p16t_ref,
                           w_ref, x_ref, g1_ref, sc_ref, sh_ref, g_ref, wr_ref, br_ref, tri_ref,
                           xo_ref, hs_ref, meta_ref, nch_ref, wb):
    @pl.when((pl.program_id(0) == 0) & (pl.program_id(1) == 0))
    def _():
        wb[0:256, :] = w_ref[0:256, :].astype(BF16)
        for pos, head in enumerate(_SWA_HEAD_ORDER):
            wb[256 + pos * HEAD_DIM:256 + (pos + 1) * HEAD_DIM, :] = (
                w_ref[256 + head * HEAD_DIM:256 + (head + 1) * HEAD_DIM, :].astype(BF16))
        wb[768:1024, :] = w_ref[768:1024, :].astype(BF16)

    tm = x_ref.shape[1]
    hw = o1_ref.shape[2]

    def lse_of(v):
        return v[:, :hw] + v[:, hw:]

    o1 = o1_ref[0].astype(F32)
    ls1 = lse_of(l1_ref[0].astype(F32))
    o4 = _dot(p4t_ref[...], o4_ref[0].reshape(tm, hw))
    ls4 = lse_of(_dot(p4t_ref[...], l4_ref[0].reshape(tm, 2 * hw)))
    o16 = _dot(p16t_ref[...], o16_ref[0].reshape(tm, hw))
    ls16 = lse_of(_dot(p16t_ref[...], l16_ref[0].reshape(tm, 2 * hw)))
    mx = jnp.maximum(jnp.maximum(ls1, ls4), ls16)
    e1, e4, e16 = jnp.exp2(ls1 - mx), jnp.exp2(ls4 - mx), jnp.exp2(ls16 - mx)
    oc = (e1 * o1 + e4 * o4 + e16 * o16) / (e1 + e4 + e16)
    mix = (_dot(oa_ref[0], wb[0:256, :]) + _dot(ob_ref[0], wb[256:768, :])
           + _dot(oc.astype(BF16), wb[768:1024, :]))
    x1 = x_ref[0] + _batch_row(g1_ref) * mix
    xo_ref[0] = x1
    _router_body(x1, sc_ref, sh_ref, g_ref, wr_ref, br_ref, tri_ref, hs_ref, meta_ref, nch_ref)


def _outproj_router_call(oa, ob, o1, l1, o4, l4, o16, l16, p4t, p16t, w_out, layer, x, mod, g, wr, br, tri):
    b, s, d = x.shape
    tm = TM
    tpb = s // tm
    nt = b * tpb
    row = lambda w_: pl.BlockSpec((1, tm, w_), lambda bi, i: (bi, i, 0))
    res = lambda dd, w_: pl.BlockSpec((1, dd, tm // dd, w_), lambda bi, i: (bi, 0, i, 0))
    const2 = lambda bi, i: (0, 0)
    flat = lambda bi, i: bi * tpb + i
    return pl.pallas_call(
        _outproj_router_kernel,
        grid=(b, tpb),
        in_specs=[
            row(256), row(512), row(256), row(512),
            res(4, 256), res(4, 512), res(16, 256), res(16, 512),
            pl.BlockSpec((tm, tm), const2), pl.BlockSpec((tm, tm), const2),
            pl.BlockSpec((None, d, d), lambda bi, i: (layer, 0, 0)),
            row(d),
            _mod_spec(mod, layer, 2),
            _mod_spec(mod, layer, 4),
            _mod_spec(mod, layer, 3),
            pl.BlockSpec((None, 1, d), lambda bi, i: (layer, 0, 0)),
            pl.BlockSpec((None, d, 2 * ROUTER_COLS), lambda bi, i: (layer, 0, 0)),
            pl.BlockSpec((None, 1, ROUTER_COLS), lambda bi, i: (layer, 0, 0)),
            pl.BlockSpec((tm, tm), const2),
        ],
        out_specs=[
            row(d),
            pl.BlockSpec((SLOTS, d), lambda bi, i: (flat(bi, i), 0)),
            pl.BlockSpec((1, 8, tm), lambda bi, i: (flat(bi, i), 0, 0)),
            pl.BlockSpec((1, N_EXPERTS, LANES), lambda bi, i: (flat(bi, i), 0, 0)),
        ],
        out_shape=[
            jax.ShapeDtypeStruct((b, s, d), F32),
            jax.ShapeDtypeStruct((nt * SLOTS, d), BF16),
            jax.ShapeDtypeStruct((nt, 8, tm), F32),
            jax.ShapeDtypeStruct((nt, N_EXPERTS, LANES), I32),
        ],
        scratch_shapes=[pltpu.VMEM((d, d), BF16)],
        compiler_params=pltpu.CompilerParams(vmem_limit_bytes=56 * 1024 * 1024),
        name="out_proj_router",
    )(oa, ob, o1, l1, o4, l4, o16, l16, p4t, p16t, w_out, x, mod, mod, mod, g, wr, br, tri)


ROUTER_COLS = LANES


def _router_body(x, sc_ref, sh_ref, g_ref, wr_ref, br_ref, tri_ref, hs_ref, meta_ref, nch_ref):
    tm = x.shape[0]
    h = _modulated_norm(x, g_ref[...], _batch_row(sc_ref), _batch_row(sh_ref))
    hb = h.astype(BF16)
    h_lo = (h - hb.astype(F32)).astype(BF16)
    part = _dot(hb, wr_ref[...]) + _dot(h_lo, wr_ref[...])
    logits = part[:, :ROUTER_COLS] + part[:, ROUTER_COLS:] + br_ref[...]
    lt = logits.T
    glog = lt[0:8]
    elog = lt[8:8 + N_EXPERTS]
    r8 = lax.broadcasted_iota(I32, (8, tm), 0)
    r16 = lax.broadcasted_iota(I32, (N_EXPERTS, tm), 0)

    gmax = jnp.max(glog, axis=0, keepdims=True)
    g_w = 1.0 / jnp.sum(jnp.exp(glog - gmax), axis=0, keepdims=True)
    g_idx = jnp.min(jnp.where(glog == gmax, r8, 99), axis=0, keepdims=True)

    el = jnp.where((r16 // EXPERTS_PER_GROUP) == g_idx, elog, NEG_INF)
    emax = jnp.max(el, axis=0, keepdims=True)
    e1 = jnp.min(jnp.where(el == emax, r16, 99), axis=0, keepdims=True)
    el2 = jnp.where(r16 == e1, NEG_INF, el)
    emax2 = jnp.max(el2, axis=0, keepdims=True)
    e2 = jnp.min(jnp.where(el2 == emax2, r16, 99), axis=0, keepdims=True)
    p2 = jnp.exp(emax2 - emax)
    wt1 = g_w / (1.0 + p2)
    wt2 = g_w * p2 / (1.0 + p2)

    oh1 = r16 == e1
    oh2 = r16 == e2
    onehot = jnp.where(oh1, 1.0, 0.0) + jnp.where(oh2, 1.0, 0.0)
    cnt = jnp.sum(onehot, axis=1, keepdims=True)
    nch = jnp.floor((cnt + (CHUNK - 1)) * (1.0 / CHUNK))
    nchb = jnp.broadcast_to(nch, (N_EXPERTS, LANES))
    rl = lax.broadcasted_iota(I32, (N_EXPERTS, LANES), 0)
    incl = nchb
    for sft in (1, 2, 4, 8):
        incl = incl + jnp.where(rl >= sft, pltpu.roll(incl, sft, 0), 0.0)
    off = (incl - nchb)[:, 0:1] * float(CHUNK)
    rank = _dot(onehot.astype(BF16), tri_ref[...])
    slot_of = off + rank
    pos1 = jnp.sum(jnp.where(oh1, slot_of, 0.0), axis=0, keepdims=True)
    pos2 = jnp.sum(jnp.where(oh2, slot_of, 0.0), axis=0, keepdims=True)

    slot = lax.broadcasted_iota(I32, (SLOTS, tm), 0)
    sel = jnp.where(slot == pos1.astype(I32), 1.0, jnp.where(slot == pos2.astype(I32), 1.0, 0.0))
    hs_ref[...] = _dot(sel.astype(BF16), hb).astype(BF16)

    meta_ref[0] = jnp.concatenate([pos1, pos2, wt1, wt2, jnp.zeros((4, tm), F32)], axis=0)
    nch_ref[0] = nchb.astype(I32)


def _ffn_schedule(nch, max_tiles, dump_base):
    nt = nch.shape[0]
    cend = jnp.cumsum(nch, axis=1)
    coff = cend - nch
    tcum = jnp.cumsum(nch, axis=0)
    before = tcum - nch
    tot = tcum[-1]
    pad = ((tot + FFN_CHUNKS - 1) // FFN_CHUNKS) * FFN_CHUNKS
    eend = jnp.cumsum(pad)
    estart = eend - pad
    n_tiles = (eend[-1] // FFN_CHUNKS).astype(I32)
    first_chunk = jnp.arange(max_tiles, dtype=I32) * FFN_CHUNKS
    tile_expert = jnp.sum((eend[None, :] <= first_chunk[:, None]).astype(I32), axis=1)
    tile_expert = jnp.minimum(tile_expert, N_EXPERTS - 1)
    hp = lax.Precision.HIGHEST
    pos = jnp.arange(max_tiles * FFN_CHUNKS, dtype=I32)
    e_s = jnp.minimum(jnp.sum((eend[None, :] <= (pos // FFN_CHUNKS * FFN_CHUNKS)[:, None]).astype(I32), axis=1),
                      N_EXPERTS - 1)
    oh_e = (e_s[:, None] == jnp.arange(N_EXPERTS, dtype=I32)[None, :]).astype(F32)
    idx = pos - jnp.dot(oh_e, estart.astype(F32), precision=hp).astype(I32)
    run_end = jnp.dot(oh_e, tcum.T.astype(F32), precision=hp).astype(I32)
    run_beg = jnp.dot(oh_e, before.T.astype(F32), precision=hp).astype(I32)
    run_off = jnp.dot(oh_e, coff.T.astype(F32), precision=hp).astype(I32)
    in_run = (idx[:, None] >= run_beg) & (idx[:, None] < run_end)
    tile_base = jnp.arange(nt, dtype=I32)[None, :] * CHUNKS_PER_TILE
    src = jnp.sum(jnp.where(in_run, tile_base + run_off + idx[:, None] - run_beg, 0), axis=1)
    real = jnp.any(in_run, axis=1)
    dump = dump_base + (pos // FFN_CHUNKS % 2) * FFN_CHUNKS + pos % FFN_CHUNKS
    src_rows = jnp.where(real, src, 0) * CHUNK
    dst_rows = jnp.where(real, src, dump) * CHUNK
    used = cend[:, -1]
    ucum = jnp.cumsum(CHUNKS_PER_TILE - used)
    ubeg = ucum - (CHUNKS_PER_TILE - used)
    z = jnp.arange(max_tiles * ZERO_CHUNKS, dtype=I32)[:, None]
    in_gap = (z >= ubeg[None, :]) & (z < ucum[None, :])
    zsrc = jnp.sum(jnp.where(in_gap, tile_base + used[None, :] + z - ubeg[None, :], 0), axis=1)
    zdump = dump_base + 2 * FFN_CHUNKS + z[:, 0] % ZERO_CHUNKS
    zero_rows = jnp.where(jnp.any(in_gap, axis=1), zsrc, zdump) * CHUNK
    return tile_expert, src_rows, dst_rows, n_tiles.reshape(1), zero_rows


def _ffn_kernel(te_ref, sr_ref, dr_ref, nt_ref, zr_ref, hs_hbm, wg_ref, wu_ref, wd_ref, ys_hbm,
                xbuf, ybuf, zbuf, wgb, wub, wdb, in_sem, out_sem, zero_sem, *, dump_base):
    j = pl.program_id(0)
    nt = nt_ref[0]
    half_ff = EXPERT_FF // 2

    def rows_at(r):
        return pl.ds(pl.multiple_of(r, CHUNK), CHUNK)

    def in_copy(step, slot, k, wait=False):
        r = 0 if wait else sr_ref[step * FFN_CHUNKS + k]
        return pltpu.make_async_copy(hs_hbm.at[rows_at(r), :], xbuf.at[slot, pl.ds(k * CHUNK, CHUNK), :],
                                     in_sem.at[slot])

    def out_copy(step, slot, k, wait=False):
        r = 0 if wait else dr_ref[step * FFN_CHUNKS + k]
        return pltpu.make_async_copy(ybuf.at[slot, pl.ds(k * CHUNK, CHUNK), :], ys_hbm.at[rows_at(r), :],
                                     out_sem.at[slot])

    def zero_copy(k, wait=False):
        r = 0 if wait else zr_ref[j * ZERO_CHUNKS + k]
        return pltpu.make_async_copy(zbuf.at[pl.ds(k * CHUNK, CHUNK), :], ys_hbm.at[rows_at(r), :], zero_sem)

    @pl.when(j == 0)
    def _():
        zbuf[...] = jnp.zeros_like(zbuf)
        fills = [pltpu.make_async_copy(
            zbuf, ys_hbm.at[pl.ds((dump_base + r * ZERO_CHUNKS) * CHUNK, ZERO_CHUNKS * CHUNK), :], zero_sem)
            for r in range(DUMP_CHUNKS // ZERO_CHUNKS)]
        for cp in fills:
            cp.start()
        for cp in fills:
            cp.wait()

    @pl.when(j >= nt)
    def _():
        for k in range(ZERO_CHUNKS):
            zero_copy(k).start()

    @pl.when(j < nt)
    def _():
        slot = j % 2

        @pl.when(j == 0)
        def _():
            for k in range(FFN_CHUNKS):
                in_copy(0, 0, k).start()

        @pl.when((j == 0) | (te_ref[j] != te_ref[jnp.maximum(j - 1, 0)]))
        def _():
            wgb[...] = wg_ref[0].astype(BF16)
            wub[...] = wu_ref[0].astype(BF16)
            wdb[...] = wd_ref[0].astype(BF16)

        @pl.when(j >= 2)
        def _():
            for k in range(FFN_CHUNKS):
                out_copy(j - 2, slot, k, wait=True).wait()

        for k in range(FFN_CHUNKS):
            in_copy(j, slot, k, wait=True).wait()

        x = xbuf[slot]
        hg = [_dot(x, wgb[:, h * half_ff:(h + 1) * half_ff]) for h in range(2)]
        hu = [_dot(x, wub[:, h * half_ff:(h + 1) * half_ff]) for h in range(2)]
        nxt = jnp.minimum(j + 1, nt - 1)
        for k in range(FFN_CHUNKS):
            in_copy(nxt, 1 - slot, k).start()
        for k in range(ZERO_CHUNKS):
            zero_copy(k).start()
        y = None
        for h in range(2):
            act = ((hg[h] / (1.0 + jnp.exp(-hg[h]))) * hu[h]).astype(BF16)
            part = _dot(act, wdb[h * half_ff:(h + 1) * half_ff, :])
            y = part if y is None else y + part
        ybuf[slot] = y.astype(BF16)
        for k in range(FFN_CHUNKS):
            out_copy(j, slot, k).start()

        @pl.when(j == nt - 1)
        def _():
            for k in range(FFN_CHUNKS):
                in_copy(j, 1 - slot, k, wait=True).wait()
            for k in range(FFN_CHUNKS):
                out_copy(j, slot, k, wait=True).wait()

            @pl.when(j >= 1)
            def _():
                for k in range(FFN_CHUNKS):
                    out_copy(j - 1, 1 - slot, k, wait=True).wait()

    for k in range(ZERO_CHUNKS):
        zero_copy(k, wait=True).wait()


def _ffn_call(tile_expert, src_rows, dst_rows, n_tiles, zero_rows, hs, wg, wu, wd, layer, max_tiles):
    rows, d = hs.shape
    ff = wg.shape[-1]
    wmap = lambda j, te, sr, dr, nt, zr: (layer, te[j], 0, 0)
    grid_spec = pltpu.PrefetchScalarGridSpec(
        num_scalar_prefetch=5,
        grid=(max_tiles,),
        in_specs=[
            pl.BlockSpec(memory_space=pl.ANY),
            pl.BlockSpec((None, 1, d, ff), wmap),
            pl.BlockSpec((None, 1, d, ff), wmap),
            pl.BlockSpec((None, 1, ff, d), wmap),
        ],
        out_specs=pl.BlockSpec(memory_space=pl.ANY),
        scratch_shapes=[
            pltpu.VMEM((2, FFN_ROWS, d), BF16),
            pltpu.VMEM((2, FFN_ROWS, d), BF16),
            pltpu.VMEM((ZERO_CHUNKS * CHUNK, d), BF16),
            pltpu.VMEM((d, ff), BF16),
            pltpu.VMEM((d, ff), BF16),
            pltpu.VMEM((ff, d), BF16),
            pltpu.SemaphoreType.DMA((2,)),
            pltpu.SemaphoreType.DMA((2,)),
            pltpu.SemaphoreType.DMA(()),
        ],
    )
    return pl.pallas_call(
        functools.partial(_ffn_kernel, dump_base=rows // CHUNK),
        grid_spec=grid_spec,
        out_shape=jax.ShapeDtypeStruct((rows + DUMP_CHUNKS * CHUNK, d), BF16),
        compiler_params=pltpu.CompilerParams(vmem_limit_bytes=48 * 1024 * 1024),
        name="expert_ffn",
    )(tile_expert, src_rows, dst_rows, n_tiles, zero_rows, hs, wg, wu, wd)


def _combine_body(ys_ref, meta_ref, x_ref, g2_ref):
    tm = x_ref.shape[1]
    meta = meta_ref[0]
    eye = (lax.broadcasted_iota(I32, (tm, tm), 0) == lax.broadcasted_iota(I32, (tm, tm), 1))

    def as_col(row):
        return jnp.sum(jnp.where(eye, row, 0.0), axis=1, keepdims=True)

    pos1, pos2 = as_col(meta[0:1]), as_col(meta[1:2])
    w1, w2 = as_col(meta[2:3]), as_col(meta[3:4])
    slot = lax.broadcasted_iota(I32, (tm, SLOTS), 1).astype(F32)
    gate = jnp.where(slot == pos1, w1, 0.0) + jnp.where(slot == pos2, w2, 0.0)
    y = _dot(gate.astype(BF16), ys_ref[...])
    return x_ref[0] + _batch_row(g2_ref) * y


def _combine_final_kernel(ys_ref, meta_ref, x_ref, g2_ref, gf_ref, xo_ref):
    xo = _combine_body(ys_ref, meta_ref, x_ref, g2_ref)
    xo_ref[0] = xo * lax.rsqrt(jnp.mean(xo * xo, axis=-1, keepdims=True) + NORM_EPS) * gf_ref[...]


def _combine_inproj_kernel(ys_ref, meta_ref, x_ref, g2_ref, *refs):
    n_in = 10
    xo_ref = refs[n_in]
    xo = _combine_body(ys_ref, meta_ref, x_ref, g2_ref)
    xo_ref[0] = xo
    _inproj_body(xo, *refs[:n_in], *refs[n_in + 1:])


def _combine_specs(d, tiles_per_batch, mod, layer):
    flat = lambda bi, i: bi * tiles_per_batch + i
    return [
        pl.BlockSpec((SLOTS, d), lambda bi, i: (flat(bi, i), 0)),
        pl.BlockSpec((1, 8, TM), lambda bi, i: (flat(bi, i), 0, 0)),
        pl.BlockSpec((1, TM, d), lambda bi, i: (bi, i, 0)),
        _mod_spec(mod, layer, 5),
    ]


def _combine_final_call(ys, meta, x, mod, layer, gf):
    b, s, d = x.shape
    return pl.pallas_call(
        _combine_final_kernel,
        grid=(b, s // TM),
        in_specs=_combine_specs(d, s // TM, mod, layer) + [pl.BlockSpec((1, d), lambda bi, i: (0, 0))],
        out_specs=pl.BlockSpec((1, TM, d), lambda bi, i: (bi, i, 0)),
        out_shape=jax.ShapeDtypeStruct((b, s, d), F32),
        compiler_params=pltpu.CompilerParams(vmem_limit_bytes=48 * 1024 * 1024),
        name="moe_combine",
    )(ys, meta, x, mod, gf)


def _combine_inproj_call(ys, meta, x, mod, g, w_in, layer, tabs, p4, p16):
    b, s, d = x.shape
    in_specs, out_specs, out_shape = _inproj_specs(b, s, d, layer, mod)
    xspec = pl.BlockSpec((1, TM, d), lambda bi, i: (bi, i, 0))
    return pl.pallas_call(
        _combine_inproj_kernel,
        grid=(b, s // TM),
        in_specs=_combine_specs(d, s // TM, mod, layer - 1) + in_specs,
        out_specs=[xspec] + out_specs,
        out_shape=[jax.ShapeDtypeStruct((b, s, d), F32)] + out_shape,
        scratch_shapes=[pltpu.VMEM((d, IN_WIDTH), BF16)],
        compiler_params=pltpu.CompilerParams(vmem_limit_bytes=56 * 1024 * 1024),
        name="combine_in_proj",
    )(ys, meta, x, mod, mod, mod, g, w_in, *tabs, p4, p16)


def _rope_tables(positions):
    pos = positions.astype(F32)[..., None]

    def table(dim):
        inv = ROPE_THETA ** (-jnp.arange(0, dim, 2, dtype=F32) / dim)
        ang = pos * inv
        cos, sin = jnp.cos(ang), jnp.sin(ang)
        reps = LANES // dim
        return (jnp.tile(jnp.concatenate([cos, cos], -1), (1, 1, reps)),
                jnp.tile(jnp.concatenate([-sin, sin], -1), (1, 1, reps)))

    c64, s64 = table(HEAD_DIM)
    c32, s32 = table(DIFF_QK_DIM)
    return c64, s64, c32, s32


def kernel(x, c, positions, ada_w, ada_b, norm_mix_g, norm_ffn_g, w_in, w_out, diff_lambda_q1, diff_lambda_k1,
           diff_lambda_q2, diff_lambda_k2, diff_subln_g, swa_sinks, router_group_w, router_group_b,
           router_expert_w, router_expert_b, expert_w_gate, expert_w_up, expert_w_down, final_norm_g):
    b, s, d = x.shape
    depth = ada_w.shape[0]
    n = b * s
    nt = n // TM
    max_tiles = (nt * CHUNKS_PER_TILE + N_EXPERTS * (FFN_CHUNKS - 1)) // FFN_CHUNKS + 1

    tabs = _rope_tables(positions)
    p4 = _residue_perm(TM, 4)
    p16 = _residue_perm(TM, 16)
    p4_b, p16_b = jnp.asarray(p4, BF16), jnp.asarray(p16, BF16)
    p4t_b, p16t_b = jnp.asarray(p4.T, BF16), jnp.asarray(p16.T, BF16)
    tri = jnp.asarray(np.triu(np.ones((TM, TM), np.float32), 1), BF16)

    c_pad = jnp.pad(c, ((0, 8 - b), (0, 0)))
    mod = _ada_call(c_pad, ada_w, ada_b)

    sink_order = np.asarray(_SWA_HEAD_ORDER, np.int32)
    zpad = lambda k: jnp.zeros((depth, d, k), F32)
    wr = jnp.concatenate([router_group_w, zpad(8 - N_GROUPS), router_expert_w,
                          zpad(ROUTER_COLS - 8 - N_EXPERTS)], axis=-1)
    wr_hi = wr.astype(BF16)
    wr = jnp.concatenate([wr_hi, (wr - wr_hi.astype(F32)).astype(BF16)], axis=-1)
    br = jnp.concatenate([router_group_b, jnp.full((depth, 8 - N_GROUPS), NEG_INF, F32), router_expert_b,
                          jnp.zeros((depth, ROUTER_COLS - 8 - N_EXPERTS), F32)], axis=-1).reshape(depth, 1, -1)

    lam_init = [0.8 - 0.6 * math.exp(-0.3 * l) for l in range(depth)]
    lam = (jnp.exp(jnp.sum(diff_lambda_q1 * diff_lambda_k1, axis=-1))
           - jnp.exp(jnp.sum(diff_lambda_q2 * diff_lambda_k2, axis=-1)) + jnp.asarray(lam_init, F32))
    g_cols = jnp.broadcast_to(diff_subln_g[:, :, None], (depth, HEAD_DIM, TQ))
    sinks = swa_sinks[:, sink_order].reshape(-1)
    g_mix = norm_mix_g.reshape(depth, 1, d)
    g_ffn = norm_ffn_g.reshape(depth, 1, d)

    proj, vat, qkv4, qkv16 = _inproj_call(x, mod, g_mix, w_in, 0, tabs, p4_b, p16_b)
    for l in range(depth):
        oa = _diff_attn_call(proj, vat, lam, g_cols, lam_init[l], l)
        ob = _band_call(proj, COL_QB, COL_KB, COL_VB, nqb=4, nkb=1, max_dist=SWA_WINDOW - 1,
                        sinks=sinks, sink_base=l * SWA_Q_HEADS, name="swa")[0]
        o1, l1 = _band_call(proj, COL_QC, COL_KC, COL_VC, nqb=2, nkb=2, max_dist=BAND, want_lse=True,
                            name="dil1")
        o4, l4 = _band_call(qkv4.reshape(b * 4, s // 4, 768), 0, 256, 512, nqb=2, nkb=2, max_dist=BAND,
                            want_lse=True, name="dil4")
        o16, l16 = _band_call(qkv16.reshape(b * 16, s // 16, 768), 0, 256, 512, nqb=2, nkb=2, max_dist=BAND,
                              want_lse=True, name="dil16")
        x, hs, meta, nch = _outproj_router_call(
            oa, ob, o1, l1, o4.reshape(b, 4, s // 4, 256), l4.reshape(b, 4, s // 4, 512),
            o16.reshape(b, 16, s // 16, 256), l16.reshape(b, 16, s // 16, 512), p4t_b, p16t_b, w_out, l, x, mod,
            g_ffn, wr, br, tri)
        sched = _ffn_schedule(nch[:, :, 0], max_tiles, nt * CHUNKS_PER_TILE)
        ys = _ffn_call(*sched, hs, expert_w_gate, expert_w_up, expert_w_down, l, max_tiles)
        if l + 1 < depth:
            x, proj, vat, qkv4, qkv16 = _combine_inproj_call(ys, meta, x, mod, g_mix, w_in, l + 1, tabs,
                                                             p4_b, p16_b)
        else:
            x = _combine_final_call(ys, meta, x, mod, l, final_norm_g.reshape(1, d))
    return x
```

```python
import functools
import math

import numpy as np
import jax
import jax.numpy as jnp
from jax import lax
from jax.experimental import pallas as pl
from jax.experimental.pallas import tpu as pltpu

F32 = jnp.float32
BF16 = jnp.bfloat16
I32 = jnp.int32

HEAD_DIM = 64
ROPE_THETA = 10000.0
NORM_EPS = 1e-6
NEG_INF = -1e30
DIFF_HEADS = 4
DIFF_QK_DIM = 32
SWA_Q_HEADS = 8
SWA_KV_HEADS = 2
SWA_WINDOW = 128
DIL_PATTERNS = ((128, 1), (512, 4), (2048, 16))
N_GROUPS = 4
EXPERTS_PER_GROUP = 4
N_EXPERTS = 16
EXPERT_FF = 512
N_ADA = 6
IN_WIDTH = 2304

LANES = 128
BF16_ROWS = 16
BAND = 128

TM = 256
CHUNK = BF16_ROWS
SLOTS = 2 * TM + N_EXPERTS * CHUNK
CHUNKS_PER_TILE = SLOTS // CHUNK
FFN_ROWS = 256
FFN_CHUNKS = FFN_ROWS // CHUNK
ZERO_CHUNKS = -(-N_EXPERTS * FFN_CHUNKS // CHUNKS_PER_TILE)
DUMP_CHUNKS = -(-(2 * FFN_CHUNKS + ZERO_CHUNKS) // ZERO_CHUNKS) * ZERO_CHUNKS
TQ = 512
BAND_ROWS = 512

_SWA_HEAD_ORDER = (0, 4, 1, 5, 2, 6, 3, 7)
COL_QB, COL_KB, COL_VB = 0, 512, 640
COL_Q1, COL_Q2, COL_K1, COL_K2, COL_VA = 768, 896, 1024, 1152, 1280
COL_QC, COL_KC, COL_VC = 1536, 1792, 2048
_SRC_ROPE = (32, 32, 32, 32, 0, 0, 64, 64, 64, 64, 64, 0, 64, 64, 64, 64, 0, 0)
_SRC_DEST = (6, 7, 8, 9, 10, 11, None, None, None, None, 4, 5, 12, 13, 14, 15, 16, 17)
VT_ROWS = HEAD_DIM + BF16_ROWS
_SRC_VA_CHUNK = 2


def _residue_perm(tm, d):
    p = np.zeros((tm, tm), np.float32)
    per = tm // d
    for l in range(per):
        for r in range(d):
            p[r * per + l, l * d + r] = 1.0
    return p


def _dot(a, b, **kw):
    return jnp.dot(a, b, preferred_element_type=F32, **kw)


def _dot_nt(a, b):
    return lax.dot_general(a, b, (((1,), (1,)), ((), ())), preferred_element_type=F32)


def _batch_row(ref):
    return ref[pl.ds(pl.program_id(0), 1), :]


def _modulated_norm(x, g, sc, sh):
    y = x * lax.rsqrt(jnp.mean(x * x, axis=-1, keepdims=True) + NORM_EPS)
    return (y * g) * (1.0 + sc) + sh


def _ada_kernel(c_ref, w_ref, b_ref, o_ref):
    c = c_ref[...]
    ca = c / (1.0 + jnp.exp(-c))
    o_ref[0] = _dot(ca, w_ref[0], precision=lax.Precision.HIGHEST) + b_ref[0]


def _ada_call(c_pad, ada_w, ada_b):
    depth, d, n = ada_w.shape
    tn = 1536
    return pl.pallas_call(
        _ada_kernel,
        grid=(depth, n // tn),
        in_specs=[
            pl.BlockSpec((c_pad.shape[0], d), lambda l, j: (0, 0)),
            pl.BlockSpec((1, d, tn), lambda l, j: (l, 0, j)),
            pl.BlockSpec((1, 1, tn), lambda l, j: (l, 0, j)),
        ],
        out_specs=pl.BlockSpec((1, c_pad.shape[0], tn), lambda l, j: (l, 0, j)),
        out_shape=jax.ShapeDtypeStruct((depth, c_pad.shape[0], n), F32),
        compiler_params=pltpu.CompilerParams(vmem_limit_bytes=40 * 1024 * 1024),
        name="ada_mod",
    )(c_pad, ada_w, ada_b.reshape(depth, 1, n))


def _rope(t, cos, sin_signed, first_half, half):
    rot = jnp.where(first_half, pltpu.roll(t, LANES - half, 1), pltpu.roll(t, half, 1))
    return t * cos + rot * sin_signed


def _inproj_body(x, sc_ref, sh_ref, g_ref, w_ref, cs64_ref, sn64_ref, cs32_ref, sn32_ref,
                 p4_ref, p16_ref, proj_ref, vat_ref, c4_ref, c16_ref, wb):
    @pl.when((pl.program_id(0) == 0) & (pl.program_id(1) == 0))
    def _():
        wb[...] = w_ref[...].astype(BF16)

    h = _modulated_norm(x, g_ref[...], _batch_row(sc_ref), _batch_row(sh_ref))
    hb = h.astype(BF16)
    lane = lax.broadcasted_iota(I32, (1, LANES), 1)
    first64 = (lane % 64) < 32
    first32 = (lane % 32) < 16
    lo_half = lane < 64
    swa_q = []
    for cb in range(IN_WIDTH // 256):
        acc = _dot(hb, wb[:, cb * 256:(cb + 1) * 256])
        if cb == _SRC_VA_CHUNK:
            acc_t = acc.T.astype(BF16)
            for hd in range(DIFF_HEADS):
                vat_ref[0, hd * VT_ROWS:hd * VT_ROWS + HEAD_DIM, :] = acc_t[hd * HEAD_DIM:(hd + 1) * HEAD_DIM]
                vat_ref[0, hd * VT_ROWS + HEAD_DIM:(hd + 1) * VT_ROWS, :] = jnp.ones(
                    (BF16_ROWS, acc_t.shape[1]), BF16)
        for half in range(2):
            src = cb * 2 + half
            t = acc[:, half * LANES:(half + 1) * LANES]
            if _SRC_ROPE[src] == 64:
                t = _rope(t, cs64_ref[0], sn64_ref[0], first64, 32)
            elif _SRC_ROPE[src] == 32:
                t = _rope(t, cs32_ref[0], sn32_ref[0], first32, 16)
            dst = _SRC_DEST[src]
            if dst is None:
                swa_q.append(t)
            else:
                proj_ref[0, :, dst * LANES:(dst + 1) * LANES] = t.astype(BF16)
    for jb in range(SWA_Q_HEADS // 2):
        a, c = swa_q[jb // 2], swa_q[2 + jb // 2]
        if jb % 2 == 0:
            blk = jnp.where(lo_half, a, pltpu.roll(c, 64, 1))
        else:
            blk = jnp.where(lo_half, pltpu.roll(a, 64, 1), c)
        proj_ref[0, :, jb * LANES:(jb + 1) * LANES] = blk.astype(BF16)
    cc = proj_ref[0, :, COL_QC:]
    tm = cc.shape[0]
    c4 = _dot(p4_ref[...], cc).astype(BF16)
    for r in range(4):
        c4_ref[0, r] = c4[r * (tm // 4):(r + 1) * (tm // 4)]
    c16 = _dot(p16_ref[...], cc).astype(BF16)
    for r in range(16):
        c16_ref[0, r] = c16[r * (tm // 16):(r + 1) * (tm // 16)]


def _inproj_kernel(x_ref, *refs):
    _inproj_body(x_ref[0], *refs)


def _mod_spec(mod, layer, k):
    return pl.BlockSpec((None, mod.shape[1], mod.shape[2] // N_ADA), lambda bi, i: (layer, 0, k))


def _inproj_specs(b, s, d, layer, mod):
    tm = TM
    row = lambda bi, i: (bi, i, 0)
    const2 = lambda bi, i: (0, 0)
    in_specs = [
        _mod_spec(mod, layer, 1),
        _mod_spec(mod, layer, 0),
        pl.BlockSpec((None, 1, d), lambda bi, i: (layer, 0, 0)),
        pl.BlockSpec((None, d, IN_WIDTH), lambda bi, i: (layer, 0, 0)),
        pl.BlockSpec((1, tm, LANES), row),
        pl.BlockSpec((1, tm, LANES), row),
        pl.BlockSpec((1, tm, LANES), row),
        pl.BlockSpec((1, tm, LANES), row),
        pl.BlockSpec((tm, tm), const2),
        pl.BlockSpec((tm, tm), const2),
    ]
    out_specs = [
        pl.BlockSpec((1, tm, IN_WIDTH), row),
        pl.BlockSpec((1, DIFF_HEADS * VT_ROWS, tm), lambda bi, i: (bi, 0, i)),
        pl.BlockSpec((1, 4, tm // 4, 768), lambda bi, i: (bi, 0, i, 0)),
        pl.BlockSpec((1, 16, tm // 16, 768), lambda bi, i: (bi, 0, i, 0)),
    ]
    out_shape = [
        jax.ShapeDtypeStruct((b, s, IN_WIDTH), BF16),
        jax.ShapeDtypeStruct((b, DIFF_HEADS * VT_ROWS, s), BF16),
        jax.ShapeDtypeStruct((b, 4, s // 4, 768), BF16),
        jax.ShapeDtypeStruct((b, 16, s // 16, 768), BF16),
    ]
    return in_specs, out_specs, out_shape


def _inproj_call(x, mod, g, w_in, layer, tabs, p4, p16):
    b, s, d = x.shape
    in_specs, out_specs, out_shape = _inproj_specs(b, s, d, layer, mod)
    return pl.pallas_call(
        _inproj_kernel,
        grid=(b, s // TM),
        in_specs=[pl.BlockSpec((1, TM, d), lambda bi, i: (bi, i, 0))] + in_specs,
        out_specs=out_specs,
        out_shape=out_shape,
        scratch_shapes=[pltpu.VMEM((d, IN_WIDTH), BF16)],
        compiler_params=pltpu.CompilerParams(vmem_limit_bytes=56 * 1024 * 1024),
        name="in_proj",
    )(x, mod, mod, g, w_in, *tabs, p4, p16)


def _diff_attn_kernel(lam_ref, q1_ref, q2_ref, q1n_ref, q2n_ref, k1_ref, k2_ref, vt_ref, g_ref, o_ref,
                      m_sc, acc_sc, s_sc, *, lambda_init, layer):
    tq = q1_ref.shape[1]
    qi = pl.program_id(1)
    lam = lam_ref[layer]
    to_log2 = DIFF_QK_DIM ** -0.5 * math.log2(math.e)
    lane = lax.broadcasted_iota(I32, (1, LANES), 1)

    def head_queries(qa_ref, qb_ref):
        qa = qa_ref[0].astype(F32) * to_log2
        qb = qb_ref[0].astype(F32) * to_log2
        out = []
        for h in range(DIFF_HEADS):
            hm = (lane // DIFF_QK_DIM) == h
            out.append((jnp.where(hm, qa, 0.0).astype(BF16), jnp.where(hm, qb, 0.0).astype(BF16)))
        return out

    qh = head_queries(q1_ref, q2_ref)
    qh_next = head_queries(q1n_ref, q2n_ref)
    causal = (lax.broadcasted_iota(I32, (tq, tq), 0) <= lax.broadcasted_iota(I32, (tq, tq), 1))

    m_sc[...] = jnp.full(m_sc.shape, NEG_INF, F32)
    acc_sc[...] = jnp.zeros(acc_sc.shape, F32)

    n_chain = 2 * DIFF_HEADS

    def scores(ch, tile, queries):
        start = pl.multiple_of(tile * tq, tq)
        k_ref = k1_ref if ch % 2 == 0 else k2_ref
        return _dot_nt(k_ref[0, pl.ds(start, tq), :], queries[ch // 2][ch % 2])

    @pl.when(qi == 0)
    def _():
        for ch in range(n_chain):
            s_sc[ch] = scores(ch, 0, qh)

    def step(j, last):
        start = pl.multiple_of(j * tq, tq)
        for ch in range(n_chain):
            st = s_sc[ch]
            s_sc[ch] = scores(ch, 0, qh_next) if last else scores(ch, j + 1, qh)
            h = ch // 2
            vt = vt_ref[0, h * VT_ROWS:(h + 1) * VT_ROWS, pl.ds(start, tq)]
            if last:
                st = jnp.where(causal, st, NEG_INF)
            m_old = m_sc[ch]
            m_new = jnp.maximum(m_old, jnp.max(st, axis=0, keepdims=True))
            p = jnp.exp2(st - m_new)
            al = jnp.exp2(m_old - m_new)
            acc_sc[ch] = al * acc_sc[ch] + _dot(vt, p.astype(BF16))
            m_sc[ch] = m_new

    def body(j, carry):
        step(j, False)
        return carry

    lax.fori_loop(0, qi, body, 0)
    step(qi, True)

    g = g_ref[...]
    outs = []
    for h in range(DIFF_HEADS):
        a1, a2 = acc_sc[2 * h], acc_sc[2 * h + 1]
        o = (a1[:HEAD_DIM] / a1[HEAD_DIM:HEAD_DIM + 1]
             - lam * (a2[:HEAD_DIM] / a2[HEAD_DIM:HEAD_DIM + 1]))
        ms = jnp.mean(o * o, axis=0, keepdims=True)
        outs.append((o * lax.rsqrt(ms + NORM_EPS)) * g * (1.0 - lambda_init))
    o_ref[0] = jnp.concatenate(outs, axis=0).T.astype(BF16)


def _diff_attn_call(proj, vat, lam, g_cols, lambda_init, layer):
    b, s, _ = proj.shape
    tq = TQ
    qspec = lambda cb: pl.BlockSpec((1, tq, LANES), lambda bi, i, cb=cb: (bi, i, cb))
    last_q = s // tq - 1
    qnext = lambda cb: pl.BlockSpec((1, tq, LANES), lambda bi, i, cb=cb: (bi, jnp.minimum(i + 1, last_q), cb))
    kspec = lambda cb: pl.BlockSpec((1, s, LANES), lambda bi, i, cb=cb: (bi, 0, cb))
    n_chain = 2 * DIFF_HEADS
    return pl.pallas_call(
        functools.partial(_diff_attn_kernel, lambda_init=lambda_init, layer=layer),
        grid=(b, s // tq),
        in_specs=[
            pl.BlockSpec(memory_space=pltpu.SMEM),
            qspec(COL_Q1 // LANES), qspec(COL_Q2 // LANES),
            qnext(COL_Q1 // LANES), qnext(COL_Q2 // LANES),
            kspec(COL_K1 // LANES), kspec(COL_K2 // LANES),
            pl.BlockSpec((1, DIFF_HEADS * VT_ROWS, s), lambda bi, i: (bi, 0, 0)),
            pl.BlockSpec((None, HEAD_DIM, tq), lambda bi, i: (layer, 0, 0)),
        ],
        out_specs=pl.BlockSpec((1, tq, 256), lambda bi, i: (bi, i, 0)),
        out_shape=jax.ShapeDtypeStruct((b, s, 256), BF16),
        scratch_shapes=[
            pltpu.VMEM((n_chain, 1, tq), F32),
            pltpu.VMEM((n_chain, VT_ROWS, tq), F32),
            pltpu.VMEM((n_chain, tq, tq), F32),
        ],
        compiler_params=pltpu.CompilerParams(vmem_limit_bytes=48 * 1024 * 1024),
        name="diff_attn",
    )(lam, proj, proj, proj, proj, proj, proj, vat, g_cols)


def _band_kernel(*refs, nqb, nkb, max_dist, has_sink, want_lse, sink_base):
    it = iter(refs)
    sink_ref = next(it) if has_sink else None
    q_ref, kp_ref, kc_ref, vp_ref, vc_ref = (next(it) for _ in range(5))
    o_ref = next(it)
    lse_ref = next(it) if want_lse else None
    kbuf, vbuf = next(it), next(it)
    nseq, rows = q_ref.shape[0], q_ref.shape[1]
    i = pl.program_id(1)
    for sq in range(nseq):
        kbuf[sq, 0:BAND, :] = kp_ref[sq]
        kbuf[sq, BAND:, :] = kc_ref[sq]
        for kb in range(nkb):
            vbuf[sq, 0:BAND, kb * 256:kb * 256 + LANES] = vp_ref[sq, :, kb * LANES:(kb + 1) * LANES]
            vbuf[sq, BAND:, kb * 256:kb * 256 + LANES] = vc_ref[sq, :, kb * LANES:(kb + 1) * LANES]
            vbuf[sq, :, kb * 256 + LANES:(kb + 1) * 256] = jnp.ones((BAND + rows, LANES), BF16)
    lane = lax.broadcasted_iota(I32, (1, LANES), 1)
    lo_half = lane < 64
    r_io = lax.broadcasted_iota(I32, (BAND, 2 * BAND), 0)
    c_io = lax.broadcasted_iota(I32, (BAND, 2 * BAND), 1)
    dist = BAND + r_io - c_io
    band = (dist >= 0) & (dist <= max_dist)
    band_first = band & ((c_io >= BAND) | (i > 0))
    col0 = lax.broadcasted_iota(I32, (1, 2 * BAND), 1) == 0
    vr = lax.broadcasted_iota(I32, (2 * BAND, 2 * LANES), 0)
    vc = lax.broadcasted_iota(I32, (2 * BAND, 2 * LANES), 1)
    sink_row = (vr == 0) & (vc < LANES)
    to_log2 = HEAD_DIM ** -0.5 * math.log2(math.e)
    units = [(sq, sb, qb) for sq in range(nseq) for sb in range(rows // BAND) for qb in range(nqb)]

    def scores(u):
        sq, sb, qb = units[u]
        kb = qb if nkb > 1 else 0
        q = q_ref[sq, sb * BAND:(sb + 1) * BAND, qb * LANES:(qb + 1) * LANES].astype(F32) * to_log2
        q2 = jnp.concatenate([jnp.where(lo_half, q, 0.0), jnp.where(lo_half, 0.0, q)], axis=0).astype(BF16)
        return _dot_nt(q2, kbuf[sq, sb * BAND:(sb + 2) * BAND, kb * LANES:(kb + 1) * LANES])

    ahead = 2
    pending = [scores(u) for u in range(min(ahead, len(units)))]
    for u, (sq, sb, qb) in enumerate(units):
        if u + ahead < len(units):
            pending.append(scores(u + ahead))
        s2 = pending[u]
        pending[u] = None
        kb = qb if nkb > 1 else 0
        r0 = sb * BAND
        msk = band_first if sb == 0 else band
        halves = []
        for hh in range(2):
            if has_sink:
                fill = jnp.where(col0, sink_ref[sink_base + qb * 2 + hh] * math.log2(math.e), NEG_INF)
            else:
                fill = NEG_INF
            halves.append(jnp.where(msk, s2[hh * BAND:(hh + 1) * BAND], fill))
        s2 = jnp.concatenate(halves, axis=0)
        m = jnp.max(s2, axis=1, keepdims=True)
        p = jnp.exp2(s2 - m).astype(BF16)
        vw = vbuf[sq, r0:r0 + 2 * BAND, kb * 256:(kb + 1) * 256]
        if has_sink:
            vw = jnp.where(sink_row, jnp.zeros_like(vw), vw)
        pv = _dot(p, vw)
        den = pv[:, LANES:]
        out = pv[:, :LANES] / den
        o = jnp.where(lo_half, out[:BAND], out[BAND:])
        o_ref[sq, r0:r0 + BAND, qb * LANES:(qb + 1) * LANES] = o.astype(BF16)
        if want_lse:
            lse2 = m + jnp.log2(den)
            ls = jnp.where(lo_half, lse2[:BAND], lse2[BAND:])
            hi = ls.astype(BF16)
            lo = (ls - hi.astype(F32)).astype(BF16)
            lse_ref[sq, r0:r0 + BAND, qb * LANES:(qb + 1) * LANES] = hi
            lse_ref[sq, r0:r0 + BAND, (nqb + qb) * LANES:(nqb + qb + 1) * LANES] = lo


def _band_call(arr, q_col, k_col, v_col, nqb, nkb, max_dist, sinks=None, sink_base=0, want_lse=False,
               name="band"):
    ns, length, _ = arr.shape
    rows = min(BAND_ROWS, length)
    nseq = BAND_ROWS // rows
    wq, wk = nqb * LANES, nkb * LANES
    rpb = rows // BAND
    cur = lambda col, w: pl.BlockSpec((nseq, rows, w), lambda n, i, c=col // w: (n, i, c))
    prev = lambda col, w: pl.BlockSpec(
        (nseq, BAND, w), lambda n, i, c=col // w: (n, jnp.maximum(i * rpb - 1, 0), c))
    in_specs = [cur(q_col, wq), prev(k_col, wk), cur(k_col, wk), prev(v_col, wk), cur(v_col, wk)]
    args = [arr] * 5
    if sinks is not None:
        in_specs = [pl.BlockSpec(memory_space=pltpu.SMEM)] + in_specs
        args = [sinks] + args
    out_specs = [pl.BlockSpec((nseq, rows, wq), lambda n, i: (n, i, 0))]
    out_shape = [jax.ShapeDtypeStruct((ns, length, wq), BF16)]
    if want_lse:
        out_specs.append(pl.BlockSpec((nseq, rows, 2 * wq), lambda n, i: (n, i, 0)))
        out_shape.append(jax.ShapeDtypeStruct((ns, length, 2 * wq), BF16))
    return pl.pallas_call(
        functools.partial(_band_kernel, nqb=nqb, nkb=nkb, max_dist=max_dist,
                          has_sink=sinks is not None, want_lse=want_lse, sink_base=sink_base),
        grid=(ns // nseq, length // rows),
        in_specs=in_specs,
        out_specs=out_specs,
        out_shape=out_shape,
        scratch_shapes=[pltpu.VMEM((nseq, BAND + rows, wk), BF16), pltpu.VMEM((nseq, BAND + rows, 2 * wk), BF16)],
        name=name,
    )(*args)


def _outproj_router_kernel(oa_ref, ob_ref, o1_ref, l1_ref, o4_ref, l4_ref, o16_ref, l16_ref, p4t_ref, p16t_ref,
                           w_ref, x_ref, g1_ref, sc_ref, sh_ref, g_ref, wr_ref, br_ref, tri_ref,
                           xo_ref, hs_ref, meta_ref, nch_ref, wb):
    @pl.when((pl.program_id(0) == 0) & (pl.program_id(1) == 0))
    def _():
        wb[0:256, :] = w_ref[0:256, :].astype(BF16)
        for pos, head in enumerate(_SWA_HEAD_ORDER):
            wb[256 + pos * HEAD_DIM:256 + (pos + 1) * HEAD_DIM, :] = (
                w_ref[256 + head * HEAD_DIM:256 + (head + 1) * HEAD_DIM, :].astype(BF16))
        wb[768:1024, :] = w_ref[768:1024, :].astype(BF16)

    tm = x_ref.shape[1]
    hw = o1_ref.shape[2]

    def lse_of(v):
        return v[:, :hw] + v[:, hw:]

    o1 = o1_ref[0].astype(F32)
    ls1 = lse_of(l1_ref[0].astype(F32))
    o4 = _dot(p4t_ref[...], o4_ref[0].reshape(tm, hw))
    ls4 = lse_of(_dot(p4t_ref[...], l4_ref[0].reshape(tm, 2 * hw)))
    o16 = _dot(p16t_ref[...], o16_ref[0].reshape(tm, hw))
    ls16 = lse_of(_dot(p16t_ref[...], l16_ref[0].reshape(tm, 2 * hw)))
    mx = jnp.maximum(jnp.maximum(ls1, ls4), ls16)
    e1, e4, e16 = jnp.exp2(ls1 - mx), jnp.exp2(ls4 - mx), jnp.exp2(ls16 - mx)
    oc = (e1 * o1 + e4 * o4 + e16 * o16) / (e1 + e4 + e16)
    mix = (_dot(oa_ref[0], wb[0:256, :]) + _dot(ob_ref[0], wb[256:768, :])
           + _dot(oc.astype(BF16), wb[768:1024, :]))
    x1 = x_ref[0] + _batch_row(g1_ref) * mix
    xo_ref[0] = x1
    _router_body(x1, sc_ref, sh_ref, g_ref, wr_ref, br_ref, tri_ref, hs_ref, meta_ref, nch_ref)


def _outproj_router_call(oa, ob, o1, l1, o4, l4, o16, l16, p4t, p16t, w_out, layer, x, mod, g, wr, br, tri):
    b, s, d = x.shape
    tm = TM
    tpb = s // tm
    nt = b * tpb
    row = lambda w_: pl.BlockSpec((1, tm, w_), lambda bi, i: (bi, i, 0))
    res = lambda dd, w_: pl.BlockSpec((1, dd, tm // dd, w_), lambda bi, i: (bi, 0, i, 0))
    const2 = lambda bi, i: (0, 0)
    flat = lambda bi, i: bi * tpb + i
    return pl.pallas_call(
        _outproj_router_kernel,
        grid=(b, tpb),
        in_specs=[
            row(256), row(512), row(256), row(512),
            res(4, 256), res(4, 512), res(16, 256), res(16, 512),
            pl.BlockSpec((tm, tm), const2), pl.BlockSpec((tm, tm), const2),
            pl.BlockSpec((None, d, d), lambda bi, i: (layer, 0, 0)),
            row(d),
            _mod_spec(mod, layer, 2),
            _mod_spec(mod, layer, 4),
            _mod_spec(mod, layer, 3),
            pl.BlockSpec((None, 1, d), lambda bi, i: (layer, 0, 0)),
            pl.BlockSpec((None, d, 2 * ROUTER_COLS), lambda bi, i: (layer, 0, 0)),
            pl.BlockSpec((None, 1, ROUTER_COLS), lambda bi, i: (layer, 0, 0)),
            pl.BlockSpec((tm, tm), const2),
        ],
        out_specs=[
            row(d),
            pl.BlockSpec((SLOTS, d), lambda bi, i: (flat(bi, i), 0)),
            pl.BlockSpec((1, 8, tm), lambda bi, i: (flat(bi, i), 0, 0)),
            pl.BlockSpec((1, N_EXPERTS, LANES), lambda bi, i: (flat(bi, i), 0, 0)),
        ],
        out_shape=[
            jax.ShapeDtypeStruct((b, s, d), F32),
            jax.ShapeDtypeStruct((nt * SLOTS, d), BF16),
            jax.ShapeDtypeStruct((nt, 8, tm), F32),
            jax.ShapeDtypeStruct((nt, N_EXPERTS, LANES), I32),
        ],
        scratch_shapes=[pltpu.VMEM((d, d), BF16)],
        compiler_params=pltpu.CompilerParams(vmem_limit_bytes=56 * 1024 * 1024),
        name="out_proj_router",
    )(oa, ob, o1, l1, o4, l4, o16, l16, p4t, p16t, w_out, x, mod, mod, mod, g, wr, br, tri)


ROUTER_COLS = LANES


def _router_body(x, sc_ref, sh_ref, g_ref, wr_ref, br_ref, tri_ref, hs_ref, meta_ref, nch_ref):
    tm = x.shape[0]
    h = _modulated_norm(x, g_ref[...], _batch_row(sc_ref), _batch_row(sh_ref))
    hb = h.astype(BF16)
    h_lo = (h - hb.astype(F32)).astype(BF16)
    part = _dot(hb, wr_ref[...]) + _dot(h_lo, wr_ref[...])
    logits = part[:, :ROUTER_COLS] + part[:, ROUTER_COLS:] + br_ref[...]
    lt = logits.T
    glog = lt[0:8]
    elog = lt[8:8 + N_EXPERTS]
    r8 = lax.broadcasted_iota(I32, (8, tm), 0)
    r16 = lax.broadcasted_iota(I32, (N_EXPERTS, tm), 0)

    gmax = jnp.max(glog, axis=0, keepdims=True)
    g_w = 1.0 / jnp.sum(jnp.exp(glog - gmax), axis=0, keepdims=True)
    g_idx = jnp.min(jnp.where(glog == gmax, r8, 99), axis=0, keepdims=True)

    el = jnp.where((r16 // EXPERTS_PER_GROUP) == g_idx, elog, NEG_INF)
    emax = jnp.max(el, axis=0, keepdims=True)
    e1 = jnp.min(jnp.where(el == emax, r16, 99), axis=0, keepdims=True)
    el2 = jnp.where(r16 == e1, NEG_INF, el)
    emax2 = jnp.max(el2, axis=0, keepdims=True)
    e2 = jnp.min(jnp.where(el2 == emax2, r16, 99), axis=0, keepdims=True)
    p2 = jnp.exp(emax2 - emax)
    wt1 = g_w / (1.0 + p2)
    wt2 = g_w * p2 / (1.0 + p2)

    oh1 = r16 == e1
    oh2 = r16 == e2
    onehot = jnp.where(oh1, 1.0, 0.0) + jnp.where(oh2, 1.0, 0.0)
    cnt = jnp.sum(onehot, axis=1, keepdims=True)
    nch = jnp.floor((cnt + (CHUNK - 1)) * (1.0 / CHUNK))
    nchb = jnp.broadcast_to(nch, (N_EXPERTS, LANES))
    rl = lax.broadcasted_iota(I32, (N_EXPERTS, LANES), 0)
    incl = nchb
    for sft in (1, 2, 4, 8):
        incl = incl + jnp.where(rl >= sft, pltpu.roll(incl, sft, 0), 0.0)
    off = (incl - nchb)[:, 0:1] * float(CHUNK)
    rank = _dot(onehot.astype(BF16), tri_ref[...])
    slot_of = off + rank
    pos1 = jnp.sum(jnp.where(oh1, slot_of, 0.0), axis=0, keepdims=True)
    pos2 = jnp.sum(jnp.where(oh2, slot_of, 0.0), axis=0, keepdims=True)

    slot = lax.broadcasted_iota(I32, (SLOTS, tm), 0)
    sel = jnp.where(slot == pos1.astype(I32), 1.0, jnp.where(slot == pos2.astype(I32), 1.0, 0.0))
    hs_ref[...] = _dot(sel.astype(BF16), hb).astype(BF16)

    meta_ref[0] = jnp.concatenate([pos1, pos2, wt1, wt2, jnp.zeros((4, tm), F32)], axis=0)
    nch_ref[0] = nchb.astype(I32)


def _ffn_schedule(nch, max_tiles, dump_base):
    nt = nch.shape[0]
    cend = jnp.cumsum(nch, axis=1)
    coff = cend - nch
    tcum = jnp.cumsum(nch, axis=0)
    before = tcum - nch
    tot = tcum[-1]
    pad = ((tot + FFN_CHUNKS - 1) // FFN_CHUNKS) * FFN_CHUNKS
    eend = jnp.cumsum(pad)
    estart = eend - pad
    n_tiles = (eend[-1] // FFN_CHUNKS).astype(I32)
    first_chunk = jnp.arange(max_tiles, dtype=I32) * FFN_CHUNKS
    tile_expert = jnp.sum((eend[None, :] <= first_chunk[:, None]).astype(I32), axis=1)
    tile_expert = jnp.minimum(tile_expert, N_EXPERTS - 1)
    hp = lax.Precision.HIGHEST
    pos = jnp.arange(max_tiles * FFN_CHUNKS, dtype=I32)
    e_s = jnp.minimum(jnp.sum((eend[None, :] <= (pos // FFN_CHUNKS * FFN_CHUNKS)[:, None]).astype(I32), axis=1),
                      N_EXPERTS - 1)
    oh_e = (e_s[:, None] == jnp.arange(N_EXPERTS, dtype=I32)[None, :]).astype(F32)
    idx = pos - jnp.dot(oh_e, estart.astype(F32), precision=hp).astype(I32)
    run_end = jnp.dot(oh_e, tcum.T.astype(F32), precision=hp).astype(I32)
    run_beg = jnp.dot(oh_e, before.T.astype(F32), precision=hp).astype(I32)
    run_off = jnp.dot(oh_e, coff.T.astype(F32), precision=hp).astype(I32)
    in_run = (idx[:, None] >= run_beg) & (idx[:, None] < run_end)
    tile_base = jnp.arange(nt, dtype=I32)[None, :] * CHUNKS_PER_TILE
    src = jnp.sum(jnp.where(in_run, tile_base + run_off + idx[:, None] - run_beg, 0), axis=1)
    real = jnp.any(in_run, axis=1)
    dump = dump_base + (pos // FFN_CHUNKS % 2) * FFN_CHUNKS + pos % FFN_CHUNKS
    src_rows = jnp.where(real, src, 0) * CHUNK
    dst_rows = jnp.where(real, src, dump) * CHUNK
    used = cend[:, -1]
    ucum = jnp.cumsum(CHUNKS_PER_TILE - used)
    ubeg = ucum - (CHUNKS_PER_TILE - used)
    z = jnp.arange(max_tiles * ZERO_CHUNKS, dtype=I32)[:, None]
    in_gap = (z >= ubeg[None, :]) & (z < ucum[None, :])
    zsrc = jnp.sum(jnp.where(in_gap, tile_base + used[None, :] + z - ubeg[None, :], 0), axis=1)
    zdump = dump_base + 2 * FFN_CHUNKS + z[:, 0] % ZERO_CHUNKS
    zero_rows = jnp.where(jnp.any(in_gap, axis=1), zsrc, zdump) * CHUNK
    return tile_expert, src_rows, dst_rows, n_tiles.reshape(1), zero_rows


def _ffn_kernel(te_ref, sr_ref, dr_ref, nt_ref, zr_ref, hs_hbm, wg_ref, wu_ref, wd_ref, ys_hbm,
                xbuf, ybuf, zbuf, wgb, wub, wdb, in_sem, out_sem, zero_sem, *, dump_base):
    j = pl.program_id(0)
    nt = nt_ref[0]
    half_ff = EXPERT_FF // 2

    def rows_at(r):
        return pl.ds(pl.multiple_of(r, CHUNK), CHUNK)

    def in_copy(step, slot, k, wait=False):
        r = 0 if wait else sr_ref[step * FFN_CHUNKS + k]
        return pltpu.make_async_copy(hs_hbm.at[rows_at(r), :], xbuf.at[slot, pl.ds(k * CHUNK, CHUNK), :],
                                     in_sem.at[slot])

    def out_copy(step, slot, k, wait=False):
        r = 0 if wait else dr_ref[step * FFN_CHUNKS + k]
        return pltpu.make_async_copy(ybuf.at[slot, pl.ds(k * CHUNK, CHUNK), :], ys_hbm.at[rows_at(r), :],
                                     out_sem.at[slot])

    def zero_copy(k, wait=False):
        r = 0 if wait else zr_ref[j * ZERO_CHUNKS + k]
        return pltpu.make_async_copy(zbuf.at[pl.ds(k * CHUNK, CHUNK), :], ys_hbm.at[rows_at(r), :], zero_sem)

    @pl.when(j == 0)
    def _():
        zbuf[...] = jnp.zeros_like(zbuf)
        fills = [pltpu.make_async_copy(
            zbuf, ys_hbm.at[pl.ds((dump_base + r * ZERO_CHUNKS) * CHUNK, ZERO_CHUNKS * CHUNK), :], zero_sem)
            for r in range(DUMP_CHUNKS // ZERO_CHUNKS)]
        for cp in fills:
            cp.start()
        for cp in fills:
            cp.wait()

    for k in range(ZERO_CHUNKS):
        zero_copy(k).start()

    @pl.when(j < nt)
    def _():
        slot = j % 2

        @pl.when(j == 0)
        def _():
            for k in range(FFN_CHUNKS):
                in_copy(0, 0, k).start()

        @pl.when(j + 1 < nt)
        def _():
            for k in range(FFN_CHUNKS):
                in_copy(j + 1, 1 - slot, k).start()

        @pl.when((j == 0) | (te_ref[j] != te_ref[jnp.maximum(j - 1, 0)]))
        def _():
            wgb[...] = wg_ref[0].astype(BF16)
            wub[...] = wu_ref[0].astype(BF16)
            wdb[...] = wd_ref[0].astype(BF16)

        for k in range(FFN_CHUNKS):
            in_copy(j, slot, k, wait=True).wait()

        @pl.when(j >= 2)
        def _():
            for k in range(FFN_CHUNKS):
                out_copy(j - 2, slot, k, wait=True).wait()

        x = xbuf[slot]
        hg = [_dot(x, wgb[:, h * half_ff:(h + 1) * half_ff]) for h in range(2)]
        hu = [_dot(x, wub[:, h * half_ff:(h + 1) * half_ff]) for h in range(2)]
        y = None
        for h in range(2):
            act = ((hg[h] / (1.0 + jnp.exp(-hg[h]))) * hu[h]).astype(BF16)
            part = _dot(act, wdb[h * half_ff:(h + 1) * half_ff, :])
            y = part if y is None else y + part
        ybuf[slot] = y.astype(BF16)
        for k in range(FFN_CHUNKS):
            out_copy(j, slot, k).start()

        @pl.when(j == nt - 1)
        def _():
            for k in range(FFN_CHUNKS):
                out_copy(j, slot, k, wait=True).wait()

            @pl.when(j >= 1)
            def _():
                for k in range(FFN_CHUNKS):
                    out_copy(j - 1, 1 - slot, k, wait=True).wait()

    for k in range(ZERO_CHUNKS):
        zero_copy(k, wait=True).wait()


def _ffn_call(tile_expert, src_rows, dst_rows, n_tiles, zero_rows, hs, wg, wu, wd, layer, max_tiles):
    rows, d = hs.shape
    ff = wg.shape[-1]
    wmap = lambda j, te, sr, dr, nt, zr: (layer, te[j], 0, 0)
    grid_spec = pltpu.PrefetchScalarGridSpec(
        num_scalar_prefetch=5,
        grid=(max_tiles,),
        in_specs=[
            pl.BlockSpec(memory_space=pl.ANY),
            pl.BlockSpec((None, 1, d, ff), wmap),
            pl.BlockSpec((None, 1, d, ff), wmap),
            pl.BlockSpec((None, 1, ff, d), wmap),
        ],
        out_specs=pl.BlockSpec(memory_space=pl.ANY),
        scratch_shapes=[
            pltpu.VMEM((2, FFN_ROWS, d), BF16),
            pltpu.VMEM((2, FFN_ROWS, d), BF16),
            pltpu.VMEM((ZERO_CHUNKS * CHUNK, d), BF16),
            pltpu.VMEM((d, ff), BF16),
            pltpu.VMEM((d, ff), BF16),
            pltpu.VMEM((ff, d), BF16),
            pltpu.SemaphoreType.DMA((2,)),
            pltpu.SemaphoreType.DMA((2,)),
            pltpu.SemaphoreType.DMA(()),
        ],
    )
    return pl.pallas_call(
        functools.partial(_ffn_kernel, dump_base=rows // CHUNK),
        grid_spec=grid_spec,
        out_shape=jax.ShapeDtypeStruct((rows + DUMP_CHUNKS * CHUNK, d), BF16),
        compiler_params=pltpu.CompilerParams(vmem_limit_bytes=48 * 1024 * 1024),
        name="expert_ffn",
    )(tile_expert, src_rows, dst_rows, n_tiles, zero_rows, hs, wg, wu, wd)


def _combine_body(ys_ref, meta_ref, x_ref, g2_ref):
    tm = x_ref.shape[1]
    meta = meta_ref[0]
    eye = (lax.broadcasted_iota(I32, (tm, tm), 0) == lax.broadcasted_iota(I32, (tm, tm), 1))

    def as_col(row):
        return jnp.sum(jnp.where(eye, row, 0.0), axis=1, keepdims=True)

    pos1, pos2 = as_col(meta[0:1]), as_col(meta[1:2])
    w1, w2 = as_col(meta[2:3]), as_col(meta[3:4])
    slot = lax.broadcasted_iota(I32, (tm, SLOTS), 1).astype(F32)
    gate = jnp.where(slot == pos1, w1, 0.0) + jnp.where(slot == pos2, w2, 0.0)
    y = _dot(gate.astype(BF16), ys_ref[...])
    return x_ref[0] + _batch_row(g2_ref) * y


def _combine_final_kernel(ys_ref, meta_ref, x_ref, g2_ref, gf_ref, xo_ref):
    xo = _combine_body(ys_ref, meta_ref, x_ref, g2_ref)
    xo_ref[0] = xo * lax.rsqrt(jnp.mean(xo * xo, axis=-1, keepdims=True) + NORM_EPS) * gf_ref[...]


def _combine_inproj_kernel(ys_ref, meta_ref, x_ref, g2_ref, *refs):
    n_in = 10
    xo_ref = refs[n_in]
    xo = _combine_body(ys_ref, meta_ref, x_ref, g2_ref)
    xo_ref[0] = xo
    _inproj_body(xo, *refs[:n_in], *refs[n_in + 1:])


def _combine_specs(d, tiles_per_batch, mod, layer):
    flat = lambda bi, i: bi * tiles_per_batch + i
    return [
        pl.BlockSpec((SLOTS, d), lambda bi, i: (flat(bi, i), 0)),
        pl.BlockSpec((1, 8, TM), lambda bi, i: (flat(bi, i), 0, 0)),
        pl.BlockSpec((1, TM, d), lambda bi, i: (bi, i, 0)),
        _mod_spec(mod, layer, 5),
    ]


def _combine_final_call(ys, meta, x, mod, layer, gf):
    b, s, d = x.shape
    return pl.pallas_call(
        _combine_final_kernel,
        grid=(b, s // TM),
        in_specs=_combine_specs(d, s // TM, mod, layer) + [pl.BlockSpec((1, d), lambda bi, i: (0, 0))],
        out_specs=pl.BlockSpec((1, TM, d), lambda bi, i: (bi, i, 0)),
        out_shape=jax.ShapeDtypeStruct((b, s, d), F32),
        compiler_params=pltpu.CompilerParams(vmem_limit_bytes=48 * 1024 * 1024),
        name="moe_combine",
    )(ys, meta, x, mod, gf)


def _combine_inproj_call(ys, meta, x, mod, g, w_in, layer, tabs, p4, p16):
    b, s, d = x.shape
    in_specs, out_specs, out_shape = _inproj_specs(b, s, d, layer, mod)
    xspec = pl.BlockSpec((1, TM, d), lambda bi, i: (bi, i, 0))
    return pl.pallas_call(
        _combine_inproj_kernel,
        grid=(b, s // TM),
        in_specs=_combine_specs(d, s // TM, mod, layer - 1) + in_specs,
        out_specs=[xspec] + out_specs,
        out_shape=[jax.ShapeDtypeStruct((b, s, d), F32)] + out_shape,
        scratch_shapes=[pltpu.VMEM((d, IN_WIDTH), BF16)],
        compiler_params=pltpu.CompilerParams(vmem_limit_bytes=56 * 1024 * 1024),
        name="combine_in_proj",
    )(ys, meta, x, mod, mod, mod, g, w_in, *tabs, p4, p16)


def _rope_tables(positions):
    pos = positions.astype(F32)[..., None]

    def table(dim):
        inv = ROPE_THETA ** (-jnp.arange(0, dim, 2, dtype=F32) / dim)
        ang = pos * inv
        cos, sin = jnp.cos(ang), jnp.sin(ang)
        reps = LANES // dim
        return (jnp.tile(jnp.concatenate([cos, cos], -1), (1, 1, reps)),
                jnp.tile(jnp.concatenate([-sin, sin], -1), (1, 1, reps)))

    c64, s64 = table(HEAD_DIM)
    c32, s32 = table(DIFF_QK_DIM)
    return c64, s64, c32, s32


def kernel(x, c, positions, ada_w, ada_b, norm_mix_g, norm_ffn_g, w_in, w_out, diff_lambda_q1, diff_lambda_k1,
           diff_lambda_q2, diff_lambda_k2, diff_subln_g, swa_sinks, router_group_w, router_group_b,
           router_expert_w, router_expert_b, expert_w_gate, expert_w_up, expert_w_down, final_norm_g):
    b, s, d = x.shape
    depth = ada_w.shape[0]
    n = b * s
    nt = n // TM
    max_tiles = (nt * CHUNKS_PER_TILE + N_EXPERTS * (FFN_CHUNKS - 1)) // FFN_CHUNKS + 1

    tabs = _rope_tables(positions)
    p4 = _residue_perm(TM, 4)
    p16 = _residue_perm(TM, 16)
    p4_b, p16_b = jnp.asarray(p4, BF16), jnp.asarray(p16, BF16)
    p4t_b, p16t_b = jnp.asarray(p4.T, BF16), jnp.asarray(p16.T, BF16)
    tri = jnp.asarray(np.triu(np.ones((TM, TM), np.float32), 1), BF16)

    c_pad = jnp.pad(c, ((0, 8 - b), (0, 0)))
    mod = _ada_call(c_pad, ada_w, ada_b)

    sink_order = np.asarray(_SWA_HEAD_ORDER, np.int32)
    zpad = lambda k: jnp.zeros((depth, d, k), F32)
    wr = jnp.concatenate([router_group_w, zpad(8 - N_GROUPS), router_expert_w,
                          zpad(ROUTER_COLS - 8 - N_EXPERTS)], axis=-1)
    wr_hi = wr.astype(BF16)
    wr = jnp.concatenate([wr_hi, (wr - wr_hi.astype(F32)).astype(BF16)], axis=-1)
    br = jnp.concatenate([router_group_b, jnp.full((depth, 8 - N_GROUPS), NEG_INF, F32), router_expert_b,
                          jnp.zeros((depth, ROUTER_COLS - 8 - N_EXPERTS), F32)], axis=-1).reshape(depth, 1, -1)

    lam_init = [0.8 - 0.6 * math.exp(-0.3 * l) for l in range(depth)]
    lam = (jnp.exp(jnp.sum(diff_lambda_q1 * diff_lambda_k1, axis=-1))
           - jnp.exp(jnp.sum(diff_lambda_q2 * diff_lambda_k2, axis=-1)) + jnp.asarray(lam_init, F32))
    g_cols = jnp.broadcast_to(diff_subln_g[:, :, None], (depth, HEAD_DIM, TQ))
    sinks = swa_sinks[:, sink_order].reshape(-1)
    g_mix = norm_mix_g.reshape(depth, 1, d)
    g_ffn = norm_ffn_g.reshape(depth, 1, d)

    proj, vat, qkv4, qkv16 = _inproj_call(x, mod, g_mix, w_in, 0, tabs, p4_b, p16_b)
    for l in range(depth):
        oa = _diff_attn_call(proj, vat, lam, g_cols, lam_init[l], l)
        ob = _band_call(proj, COL_QB, COL_KB, COL_VB, nqb=4, nkb=1, max_dist=SWA_WINDOW - 1,
                        sinks=sinks, sink_base=l * SWA_Q_HEADS, name="swa")[0]
        o1, l1 = _band_call(proj, COL_QC, COL_KC, COL_VC, nqb=2, nkb=2, max_dist=BAND, want_lse=True,
                            name="dil1")
        o4, l4 = _band_call(qkv4.reshape(b * 4, s // 4, 768), 0, 256, 512, nqb=2, nkb=2, max_dist=BAND,
                            want_lse=True, name="dil4")
        o16, l16 = _band_call(qkv16.reshape(b * 16, s // 16, 768), 0, 256, 512, nqb=2, nkb=2, max_dist=BAND,
                              want_lse=True, name="dil16")
        x, hs, meta, nch = _outproj_router_call(
            oa, ob, o1, l1, o4.reshape(b, 4, s // 4, 256), l4.reshape(b, 4, s // 4, 512),
            o16.reshape(b, 16, s // 16, 256), l16.reshape(b, 16, s // 16, 512), p4t_b, p16t_b, w_out, l, x, mod,
            g_ffn, wr, br, tri)
        sched = _ffn_schedule(nch[:, :, 0], max_tiles, nt * CHUNKS_PER_TILE)
        ys = _ffn_call(*sched, hs, expert_w_gate, expert_w_up, expert_w_down, l, max_tiles)
        if l + 1 < depth:
            x, proj, vat, qkv4, qkv16 = _combine_inproj_call(ys, meta, x, mod, g_mix, w_in, l + 1, tabs,
                                                             p4_b, p16_b)
        else:
            x = _combine_final_call(ys, meta, x, mod, l, final_norm_g.reshape(1, d))
    return x
```

```python
import functools
import math

import numpy as np
import jax
import jax.numpy as jnp
from jax import lax
from jax.experimental import pallas as pl
from jax.experimental.pallas import tpu as pltpu

F32 = jnp.float32
BF16 = jnp.bfloat16
I32 = jnp.int32

HEAD_DIM = 64
ROPE_THETA = 10000.0
NORM_EPS = 1e-6
NEG_INF = -1e30
DIFF_HEADS = 4
DIFF_QK_DIM = 32
SWA_Q_HEADS = 8
SWA_KV_HEADS = 2
SWA_WINDOW = 128
DIL_PATTERNS = ((128, 1), (512, 4), (2048, 16))
N_GROUPS = 4
EXPERTS_PER_GROUP = 4
N_EXPERTS = 16
EXPERT_FF = 512
N_ADA = 6
IN_WIDTH = 2304

LANES = 128
BF16_ROWS = 16
BAND = 128

TM = 256
CHUNK = BF16_ROWS
SLOTS = 2 * TM + N_EXPERTS * CHUNK
CHUNKS_PER_TILE = SLOTS // CHUNK
FFN_ROWS = 512
FFN_CHUNKS = FFN_ROWS // CHUNK
ZERO_CHUNKS = -(-N_EXPERTS * FFN_CHUNKS // CHUNKS_PER_TILE)
DUMP_CHUNKS = -(-(2 * FFN_CHUNKS + ZERO_CHUNKS) // ZERO_CHUNKS) * ZERO_CHUNKS
TQ = 512
BAND_ROWS = 1024

_SWA_HEAD_ORDER = (0, 4, 1, 5, 2, 6, 3, 7)
COL_QB, COL_KB, COL_VB = 0, 512, 640
COL_Q1, COL_Q2, COL_K1, COL_K2, COL_VA = 768, 896, 1024, 1152, 1280
COL_QC, COL_KC, COL_VC = 1536, 1792, 2048
_SRC_ROPE = (32, 32, 32, 32, 0, 0, 64, 64, 64, 64, 64, 0, 64, 64, 64, 64, 0, 0)
_SRC_DEST = (6, 7, 8, 9, 10, 11, None, None, None, None, 4, 5, 12, 13, 14, 15, 16, 17)
VT_ROWS = HEAD_DIM + BF16_ROWS
_SRC_VA_CHUNK = 2


def _residue_perm(tm, d):
    p = np.zeros((tm, tm), np.float32)
    per = tm // d
    for l in range(per):
        for r in range(d):
            p[r * per + l, l * d + r] = 1.0
    return p


def _dot(a, b, **kw):
    return jnp.dot(a, b, preferred_element_type=F32, **kw)


def _dot_nt(a, b):
    return lax.dot_general(a, b, (((1,), (1,)), ((), ())), preferred_element_type=F32)


def _batch_row(ref):
    return ref[pl.ds(pl.program_id(0), 1), :]


def _modulated_norm(x, g, sc, sh):
    y = x * lax.rsqrt(jnp.mean(x * x, axis=-1, keepdims=True) + NORM_EPS)
    return (y * g) * (1.0 + sc) + sh


def _ada_kernel(c_ref, w_ref, b_ref, o_ref):
    c = c_ref[...]
    ca = c / (1.0 + jnp.exp(-c))
    o_ref[0] = _dot(ca, w_ref[0], precision=lax.Precision.HIGHEST) + b_ref[0]


def _ada_call(c_pad, ada_w, ada_b):
    depth, d, n = ada_w.shape
    tn = 1536
    return pl.pallas_call(
        _ada_kernel,
        grid=(depth, n // tn),
        in_specs=[
            pl.BlockSpec((c_pad.shape[0], d), lambda l, j: (0, 0)),
            pl.BlockSpec((1, d, tn), lambda l, j: (l, 0, j)),
            pl.BlockSpec((1, 1, tn), lambda l, j: (l, 0, j)),
        ],
        out_specs=pl.BlockSpec((1, c_pad.shape[0], tn), lambda l, j: (l, 0, j)),
        out_shape=jax.ShapeDtypeStruct((depth, c_pad.shape[0], n), F32),
        compiler_params=pltpu.CompilerParams(vmem_limit_bytes=40 * 1024 * 1024),
        name="ada_mod",
    )(c_pad, ada_w, ada_b.reshape(depth, 1, n))


def _rope(t, cos, sin_signed, first_half, half):
    rot = jnp.where(first_half, pltpu.roll(t, LANES - half, 1), pltpu.roll(t, half, 1))
    return t * cos + rot * sin_signed


def _inproj_body(x, sc_ref, sh_ref, g_ref, w_ref, cs64_ref, sn64_ref, cs32_ref, sn32_ref,
                 p4_ref, p16_ref, proj_ref, vat_ref, c4_ref, c16_ref, wb):
    @pl.when((pl.program_id(0) == 0) & (pl.program_id(1) == 0))
    def _():
        wb[...] = w_ref[...].astype(BF16)

    h = _modulated_norm(x, g_ref[...], _batch_row(sc_ref), _batch_row(sh_ref))
    hb = h.astype(BF16)
    lane = lax.broadcasted_iota(I32, (1, LANES), 1)
    first64 = (lane % 64) < 32
    first32 = (lane % 32) < 16
    lo_half = lane < 64
    swa_q = []
    for cb in range(IN_WIDTH // 256):
        acc = _dot(hb, wb[:, cb * 256:(cb + 1) * 256])
        if cb == _SRC_VA_CHUNK:
            acc_t = acc.T.astype(BF16)
            for hd in range(DIFF_HEADS):
                vat_ref[0, hd * VT_ROWS:hd * VT_ROWS + HEAD_DIM, :] = acc_t[hd * HEAD_DIM:(hd + 1) * HEAD_DIM]
                vat_ref[0, hd * VT_ROWS + HEAD_DIM:(hd + 1) * VT_ROWS, :] = jnp.ones(
                    (BF16_ROWS, acc_t.shape[1]), BF16)
        for half in range(2):
            src = cb * 2 + half
            t = acc[:, half * LANES:(half + 1) * LANES]
            if _SRC_ROPE[src] == 64:
                t = _rope(t, cs64_ref[0], sn64_ref[0], first64, 32)
            elif _SRC_ROPE[src] == 32:
                t = _rope(t, cs32_ref[0], sn32_ref[0], first32, 16)
            dst = _SRC_DEST[src]
            if dst is None:
                swa_q.append(t)
            else:
                proj_ref[0, :, dst * LANES:(dst + 1) * LANES] = t.astype(BF16)
    for jb in range(SWA_Q_HEADS // 2):
        a, c = swa_q[jb // 2], swa_q[2 + jb // 2]
        if jb % 2 == 0:
            blk = jnp.where(lo_half, a, pltpu.roll(c, 64, 1))
        else:
            blk = jnp.where(lo_half, pltpu.roll(a, 64, 1), c)
        proj_ref[0, :, jb * LANES:(jb + 1) * LANES] = blk.astype(BF16)
    cc = proj_ref[0, :, COL_QC:]
    tm = cc.shape[0]
    c4 = _dot(p4_ref[...], cc).astype(BF16)
    for r in range(4):
        c4_ref[0, r] = c4[r * (tm // 4):(r + 1) * (tm // 4)]
    c16 = _dot(p16_ref[...], cc).astype(BF16)
    for r in range(16):
        c16_ref[0, r] = c16[r * (tm // 16):(r + 1) * (tm // 16)]


def _inproj_kernel(x_ref, *refs):
    _inproj_body(x_ref[0], *refs)


def _mod_spec(mod, layer, k):
    return pl.BlockSpec((None, mod.shape[1], mod.shape[2] // N_ADA), lambda bi, i: (layer, 0, k))


def _inproj_specs(b, s, d, layer, mod):
    tm = TM
    row = lambda bi, i: (bi, i, 0)
    const2 = lambda bi, i: (0, 0)
    in_specs = [
        _mod_spec(mod, layer, 1),
        _mod_spec(mod, layer, 0),
        pl.BlockSpec((None, 1, d), lambda bi, i: (layer, 0, 0)),
        pl.BlockSpec((None, d, IN_WIDTH), lambda bi, i: (layer, 0, 0)),
        pl.BlockSpec((1, tm, LANES), row),
        pl.BlockSpec((1, tm, LANES), row),
        pl.BlockSpec((1, tm, LANES), row),
        pl.BlockSpec((1, tm, LANES), row),
        pl.BlockSpec((tm, tm), const2),
        pl.BlockSpec((tm, tm), const2),
    ]
    out_specs = [
        pl.BlockSpec((1, tm, IN_WIDTH), row),
        pl.BlockSpec((1, DIFF_HEADS * VT_ROWS, tm), lambda bi, i: (bi, 0, i)),
        pl.BlockSpec((1, 4, tm // 4, 768), lambda bi, i: (bi, 0, i, 0)),
        pl.BlockSpec((1, 16, tm // 16, 768), lambda bi, i: (bi, 0, i, 0)),
    ]
    out_shape = [
        jax.ShapeDtypeStruct((b, s, IN_WIDTH), BF16),
        jax.ShapeDtypeStruct((b, DIFF_HEADS * VT_ROWS, s), BF16),
        jax.ShapeDtypeStruct((b, 4, s // 4, 768), BF16),
        jax.ShapeDtypeStruct((b, 16, s // 16, 768), BF16),
    ]
    return in_specs, out_specs, out_shape


def _inproj_call(x, mod, g, w_in, layer, tabs, p4, p16):
    b, s, d = x.shape
    in_specs, out_specs, out_shape = _inproj_specs(b, s, d, layer, mod)
    return pl.pallas_call(
        _inproj_kernel,
        grid=(b, s // TM),
        in_specs=[pl.BlockSpec((1, TM, d), lambda bi, i: (bi, i, 0))] + in_specs,
        out_specs=out_specs,
        out_shape=out_shape,
        scratch_shapes=[pltpu.VMEM((d, IN_WIDTH), BF16)],
        compiler_params=pltpu.CompilerParams(vmem_limit_bytes=56 * 1024 * 1024),
        name="in_proj",
    )(x, mod, mod, g, w_in, *tabs, p4, p16)


def _diff_attn_kernel(lam_ref, q1_ref, q2_ref, q1n_ref, q2n_ref, k1_ref, k2_ref, vt_ref, g_ref, o_ref,
                      m_sc, acc_sc, s_sc, *, lambda_init, layer):
    tq = q1_ref.shape[1]
    qi = pl.program_id(1)
    lam = lam_ref[layer]
    to_log2 = DIFF_QK_DIM ** -0.5 * math.log2(math.e)
    lane = lax.broadcasted_iota(I32, (1, LANES), 1)

    def head_queries(qa_ref, qb_ref):
        qa = qa_ref[0].astype(F32) * to_log2
        qb = qb_ref[0].astype(F32) * to_log2
        out = []
        for h in range(DIFF_HEADS):
            hm = (lane // DIFF_QK_DIM) == h
            out.append((jnp.where(hm, qa, 0.0).astype(BF16), jnp.where(hm, qb, 0.0).astype(BF16)))
        return out

    qh = head_queries(q1_ref, q2_ref)
    qh_next = head_queries(q1n_ref, q2n_ref)
    causal = (lax.broadcasted_iota(I32, (tq, tq), 0) <= lax.broadcasted_iota(I32, (tq, tq), 1))

    m_sc[...] = jnp.full(m_sc.shape, NEG_INF, F32)
    acc_sc[...] = jnp.zeros(acc_sc.shape, F32)

    n_chain = 2 * DIFF_HEADS

    def scores(ch, tile, queries):
        start = pl.multiple_of(tile * tq, tq)
        k_ref = k1_ref if ch % 2 == 0 else k2_ref
        return _dot_nt(k_ref[0, pl.ds(start, tq), :], queries[ch // 2][ch % 2])

    @pl.when(qi == 0)
    def _():
        for ch in range(n_chain):
            s_sc[ch] = scores(ch, 0, qh)

    def step(j, last):
        start = pl.multiple_of(j * tq, tq)
        for ch in range(n_chain):
            st = s_sc[ch]
            s_sc[ch] = scores(ch, 0, qh_next) if last else scores(ch, j + 1, qh)
            h = ch // 2
            vt = vt_ref[0, h * VT_ROWS:(h + 1) * VT_ROWS, pl.ds(start, tq)]
            if last:
                st = jnp.where(causal, st, NEG_INF)
            m_old = m_sc[ch]
            m_new = jnp.maximum(m_old, jnp.max(st, axis=0, keepdims=True))
            p = jnp.exp2(st - m_new)
            al = jnp.exp2(m_old - m_new)
            acc_sc[ch] = al * acc_sc[ch] + _dot(vt, p.astype(BF16))
            m_sc[ch] = m_new

    def body(j, carry):
        step(j, False)
        return carry

    lax.fori_loop(0, qi, body, 0)
    step(qi, True)

    g = g_ref[...]
    outs = []
    for h in range(DIFF_HEADS):
        a1, a2 = acc_sc[2 * h], acc_sc[2 * h + 1]
        o = (a1[:HEAD_DIM] / a1[HEAD_DIM:HEAD_DIM + 1]
             - lam * (a2[:HEAD_DIM] / a2[HEAD_DIM:HEAD_DIM + 1]))
        ms = jnp.mean(o * o, axis=0, keepdims=True)
        outs.append((o * lax.rsqrt(ms + NORM_EPS)) * g * (1.0 - lambda_init))
    o_ref[0] = jnp.concatenate(outs, axis=0).T.astype(BF16)


def _diff_attn_call(proj, vat, lam, g_cols, lambda_init, layer):
    b, s, _ = proj.shape
    tq = TQ
    qspec = lambda cb: pl.BlockSpec((1, tq, LANES), lambda bi, i, cb=cb: (bi, i, cb))
    last_q = s // tq - 1
    qnext = lambda cb: pl.BlockSpec((1, tq, LANES), lambda bi, i, cb=cb: (bi, jnp.minimum(i + 1, last_q), cb))
    kspec = lambda cb: pl.BlockSpec((1, s, LANES), lambda bi, i, cb=cb: (bi, 0, cb))
    n_chain = 2 * DIFF_HEADS
    return pl.pallas_call(
        functools.partial(_diff_attn_kernel, lambda_init=lambda_init, layer=layer),
        grid=(b, s // tq),
        in_specs=[
            pl.BlockSpec(memory_space=pltpu.SMEM),
            qspec(COL_Q1 // LANES), qspec(COL_Q2 // LANES),
            qnext(COL_Q1 // LANES), qnext(COL_Q2 // LANES),
            kspec(COL_K1 // LANES), kspec(COL_K2 // LANES),
            pl.BlockSpec((1, DIFF_HEADS * VT_ROWS, s), lambda bi, i: (bi, 0, 0)),
            pl.BlockSpec((None, HEAD_DIM, tq), lambda bi, i: (layer, 0, 0)),
        ],
        out_specs=pl.BlockSpec((1, tq, 256), lambda bi, i: (bi, i, 0)),
        out_shape=jax.ShapeDtypeStruct((b, s, 256), BF16),
        scratch_shapes=[
            pltpu.VMEM((n_chain, 1, tq), F32),
            pltpu.VMEM((n_chain, VT_ROWS, tq), F32),
            pltpu.VMEM((n_chain, tq, tq), F32),
        ],
        compiler_params=pltpu.CompilerParams(vmem_limit_bytes=48 * 1024 * 1024),
        name="diff_attn",
    )(lam, proj, proj, proj, proj, proj, proj, vat, g_cols)


def _band_kernel(*refs, nqb, nkb, max_dist, has_sink, want_lse, sink_base):
    it = iter(refs)
    sink_ref = next(it) if has_sink else None
    q_ref, kp_ref, kc_ref, vp_ref, vc_ref = (next(it) for _ in range(5))
    o_ref = next(it)
    lse_ref = next(it) if want_lse else None
    kbuf, vbuf = next(it), next(it)
    nseq, rows = q_ref.shape[0], q_ref.shape[1]
    i = pl.program_id(1)
    for sq in range(nseq):
        kbuf[sq, 0:BAND, :] = kp_ref[sq]
        kbuf[sq, BAND:, :] = kc_ref[sq]
        for kb in range(nkb):
            vbuf[sq, 0:BAND, kb * 256:kb * 256 + LANES] = vp_ref[sq, :, kb * LANES:(kb + 1) * LANES]
            vbuf[sq, BAND:, kb * 256:kb * 256 + LANES] = vc_ref[sq, :, kb * LANES:(kb + 1) * LANES]
            vbuf[sq, :, kb * 256 + LANES:(kb + 1) * 256] = jnp.ones((BAND + rows, LANES), BF16)
    lane = lax.broadcasted_iota(I32, (1, LANES), 1)
    lo_half = lane < 64
    r_io = lax.broadcasted_iota(I32, (BAND, 2 * BAND), 0)
    c_io = lax.broadcasted_iota(I32, (BAND, 2 * BAND), 1)
    dist = BAND + r_io - c_io
    band = (dist >= 0) & (dist <= max_dist)
    band_first = band & ((c_io >= BAND) | (i > 0))
    col0 = lax.broadcasted_iota(I32, (1, 2 * BAND), 1) == 0
    vr = lax.broadcasted_iota(I32, (2 * BAND, 2 * LANES), 0)
    vc = lax.broadcasted_iota(I32, (2 * BAND, 2 * LANES), 1)
    sink_row = (vr == 0) & (vc < LANES)
    to_log2 = HEAD_DIM ** -0.5 * math.log2(math.e)
    units = [(sq, sb, qb) for sq in range(nseq) for sb in range(rows // BAND) for qb in range(nqb)]

    def scores(u):
        sq, sb, qb = units[u]
        kb = qb if nkb > 1 else 0
        q = q_ref[sq, sb * BAND:(sb + 1) * BAND, qb * LANES:(qb + 1) * LANES].astype(F32) * to_log2
        q2 = jnp.concatenate([jnp.where(lo_half, q, 0.0), jnp.where(lo_half, 0.0, q)], axis=0).astype(BF16)
        return _dot_nt(q2, kbuf[sq, sb * BAND:(sb + 2) * BAND, kb * LANES:(kb + 1) * LANES])

    ahead = 2
    pending = [scores(u) for u in range(min(ahead, len(units)))]
    for u, (sq, sb, qb) in enumerate(units):
        if u + ahead < len(units):
            pending.append(scores(u + ahead))
        s2 = pending[u]
        pending[u] = None
        kb = qb if nkb > 1 else 0
        r0 = sb * BAND
        msk = band_first if sb == 0 else band
        halves = []
        for hh in range(2):
            if has_sink:
                fill = jnp.where(col0, sink_ref[sink_base + qb * 2 + hh] * math.log2(math.e), NEG_INF)
            else:
                fill = NEG_INF
            halves.append(jnp.where(msk, s2[hh * BAND:(hh + 1) * BAND], fill))
        s2 = jnp.concatenate(halves, axis=0)
        m = jnp.max(s2, axis=1, keepdims=True)
        p = jnp.exp2(s2 - m).astype(BF16)
        vw = vbuf[sq, r0:r0 + 2 * BAND, kb * 256:(kb + 1) * 256]
        if has_sink:
            vw = jnp.where(sink_row, jnp.zeros_like(vw), vw)
        pv = _dot(p, vw)
        den = pv[:, LANES:]
        out = pv[:, :LANES] / den
        o = jnp.where(lo_half, out[:BAND], out[BAND:])
        o_ref[sq, r0:r0 + BAND, qb * LANES:(qb + 1) * LANES] = o.astype(BF16)
        if want_lse:
            lse2 = m + jnp.log2(den)
            ls = jnp.where(lo_half, lse2[:BAND], lse2[BAND:])
            hi = ls.astype(BF16)
            lo = (ls - hi.astype(F32)).astype(BF16)
            lse_ref[sq, r0:r0 + BAND, qb * LANES:(qb + 1) * LANES] = hi
            lse_ref[sq, r0:r0 + BAND, (nqb + qb) * LANES:(nqb + qb + 1) * LANES] = lo


def _band_call(arr, q_col, k_col, v_col, nqb, nkb, max_dist, sinks=None, sink_base=0, want_lse=False,
               name="band"):
    ns, length, _ = arr.shape
    rows = min(BAND_ROWS, length)
    nseq = BAND_ROWS // rows
    wq, wk = nqb * LANES, nkb * LANES
    rpb = rows // BAND
    cur = lambda col, w: pl.BlockSpec((nseq, rows, w), lambda n, i, c=col // w: (n, i, c))
    prev = lambda col, w: pl.BlockSpec(
        (nseq, BAND, w), lambda n, i, c=col // w: (n, jnp.maximum(i * rpb - 1, 0), c))
    in_specs = [cur(q_col, wq), prev(k_col, wk), cur(k_col, wk), prev(v_col, wk), cur(v_col, wk)]
    args = [arr] * 5
    if sinks is not None:
        in_specs = [pl.BlockSpec(memory_space=pltpu.SMEM)] + in_specs
        args = [sinks] + args
    out_specs = [pl.BlockSpec((nseq, rows, wq), lambda n, i: (n, i, 0))]
    out_shape = [jax.ShapeDtypeStruct((ns, length, wq), BF16)]
    if want_lse:
        out_specs.append(pl.BlockSpec((nseq, rows, 2 * wq), lambda n, i: (n, i, 0)))
        out_shape.append(jax.ShapeDtypeStruct((ns, length, 2 * wq), BF16))
    return pl.pallas_call(
        functools.partial(_band_kernel, nqb=nqb, nkb=nkb, max_dist=max_dist,
                          has_sink=sinks is not None, want_lse=want_lse, sink_base=sink_base),
        grid=(ns // nseq, length // rows),
        in_specs=in_specs,
        out_specs=out_specs,
        out_shape=out_shape,
        scratch_shapes=[pltpu.VMEM((nseq, BAND + rows, wk), BF16), pltpu.VMEM((nseq, BAND + rows, 2 * wk), BF16)],
        name=name,
    )(*args)


def _outproj_router_kernel(oa_ref, ob_ref, o1_ref, l1_ref, o4_ref, l4_ref, o16_ref, l16_ref, p4t_ref, p16t_ref,
                           w_ref, x_ref, g1_ref, sc_ref, sh_ref, g_ref, wr_ref, br_ref, tri_ref,
                           xo_ref, hs_ref, meta_ref, nch_ref, wb):
    @pl.when((pl.program_id(0) == 0) & (pl.program_id(1) == 0))
    def _():
        wb[0:256, :] = w_ref[0:256, :].astype(BF16)
        for pos, head in enumerate(_SWA_HEAD_ORDER):
            wb[256 + pos * HEAD_DIM:256 + (pos + 1) * HEAD_DIM, :] = (
                w_ref[256 + head * HEAD_DIM:256 + (head + 1) * HEAD_DIM, :].astype(BF16))
        wb[768:1024, :] = w_ref[768:1024, :].astype(BF16)

    tm = x_ref.shape[1]
    hw = o1_ref.shape[2]

    def lse_of(v):
        return v[:, :hw] + v[:, hw:]

    o1 = o1_ref[0].astype(F32)
    ls1 = lse_of(l1_ref[0].astype(F32))
    o4 = _dot(p4t_ref[...], o4_ref[0].reshape(tm, hw))
    ls4 = lse_of(_dot(p4t_ref[...], l4_ref[0].reshape(tm, 2 * hw)))
    o16 = _dot(p16t_ref[...], o16_ref[0].reshape(tm, hw))
    ls16 = lse_of(_dot(p16t_ref[...], l16_ref[0].reshape(tm, 2 * hw)))
    mx = jnp.maximum(jnp.maximum(ls1, ls4), ls16)
    e1, e4, e16 = jnp.exp2(ls1 - mx), jnp.exp2(ls4 - mx), jnp.exp2(ls16 - mx)
    oc = (e1 * o1 + e4 * o4 + e16 * o16) / (e1 + e4 + e16)
    mix = (_dot(oa_ref[0], wb[0:256, :]) + _dot(ob_ref[0], wb[256:768, :])
           + _dot(oc.astype(BF16), wb[768:1024, :]))
    x1 = x_ref[0] + _batch_row(g1_ref) * mix
    xo_ref[0] = x1
    _router_body(x1, sc_ref, sh_ref, g_ref, wr_ref, br_ref, tri_ref, hs_ref, meta_ref, nch_ref)


def _outproj_router_call(oa, ob, o1, l1, o4, l4, o16, l16, p4t, p16t, w_out, layer, x, mod, g, wr, br, tri):
    b, s, d = x.shape
    tm = TM
    tpb = s // tm
    nt = b * tpb
    row = lambda w_: pl.BlockSpec((1, tm, w_), lambda bi, i: (bi, i, 0))
    res = lambda dd, w_: pl.BlockSpec((1, dd, tm // dd, w_), lambda bi, i: (bi, 0, i, 0))
    const2 = lambda bi, i: (0, 0)
    flat = lambda bi, i: bi * tpb + i
    return pl.pallas_call(
        _outproj_router_kernel,
        grid=(b, tpb),
        in_specs=[
            row(256), row(512), row(256), row(512),
            res(4, 256), res(4, 512), res(16, 256), res(16, 512),
            pl.BlockSpec((tm, tm), const2), pl.BlockSpec((tm, tm), const2),
            pl.BlockSpec((None, d, d), lambda bi, i: (layer, 0, 0)),
            row(d),
            _mod_spec(mod, layer, 2),
            _mod_spec(mod, layer, 4),
            _mod_spec(mod, layer, 3),
            pl.BlockSpec((None, 1, d), lambda bi, i: (layer, 0, 0)),
            pl.BlockSpec((None, d, 2 * ROUTER_COLS), lambda bi, i: (layer, 0, 0)),
            pl.BlockSpec((None, 1, ROUTER_COLS), lambda bi, i: (layer, 0, 0)),
            pl.BlockSpec((tm, tm), const2),
        ],
        out_specs=[
            row(d),
            pl.BlockSpec((SLOTS, d), lambda bi, i: (flat(bi, i), 0)),
            pl.BlockSpec((1, 8, tm), lambda bi, i: (flat(bi, i), 0, 0)),
            pl.BlockSpec((1, N_EXPERTS, LANES), lambda bi, i: (flat(bi, i), 0, 0)),
        ],
        out_shape=[
            jax.ShapeDtypeStruct((b, s, d), F32),
            jax.ShapeDtypeStruct((nt * SLOTS, d), BF16),
            jax.ShapeDtypeStruct((nt, 8, tm), F32),
            jax.ShapeDtypeStruct((nt, N_EXPERTS, LANES), I32),
        ],
        scratch_shapes=[pltpu.VMEM((d, d), BF16)],
        compiler_params=pltpu.CompilerParams(vmem_limit_bytes=56 * 1024 * 1024),
        name="out_proj_router",
    )(oa, ob, o1, l1, o4, l4, o16, l16, p4t, p16t, w_out, x, mod, mod, mod, g, wr, br, tri)


ROUTER_COLS = LANES


def _router_body(x, sc_ref, sh_ref, g_ref, wr_ref, br_ref, tri_ref, hs_ref, meta_ref, nch_ref):
    tm = x.shape[0]
    h = _modulated_norm(x, g_ref[...], _batch_row(sc_ref), _batch_row(sh_ref))
    hb = h.astype(BF16)
    h_lo = (h - hb.astype(F32)).astype(BF16)
    part = _dot(hb, wr_ref[...]) + _dot(h_lo, wr_ref[...])
    logits = part[:, :ROUTER_COLS] + part[:, ROUTER_COLS:] + br_ref[...]
    lt = logits.T
    glog = lt[0:8]
    elog = lt[8:8 + N_EXPERTS]
    r8 = lax.broadcasted_iota(I32, (8, tm), 0)
    r16 = lax.broadcasted_iota(I32, (N_EXPERTS, tm), 0)

    gmax = jnp.max(glog, axis=0, keepdims=True)
    g_w = 1.0 / jnp.sum(jnp.exp(glog - gmax), axis=0, keepdims=True)
    g_idx = jnp.min(jnp.where(glog == gmax, r8, 99), axis=0, keepdims=True)

    el = jnp.where((r16 // EXPERTS_PER_GROUP) == g_idx, elog, NEG_INF)
    emax = jnp.max(el, axis=0, keepdims=True)
    e1 = jnp.min(jnp.where(el == emax, r16, 99), axis=0, keepdims=True)
    el2 = jnp.where(r16 == e1, NEG_INF, el)
    emax2 = jnp.max(el2, axis=0, keepdims=True)
    e2 = jnp.min(jnp.where(el2 == emax2, r16, 99), axis=0, keepdims=True)
    p2 = jnp.exp(emax2 - emax)
    wt1 = g_w / (1.0 + p2)
    wt2 = g_w * p2 / (1.0 + p2)

    oh1 = r16 == e1
    oh2 = r16 == e2
    onehot = jnp.where(oh1, 1.0, 0.0) + jnp.where(oh2, 1.0, 0.0)
    cnt = jnp.sum(onehot, axis=1, keepdims=True)
    nch = jnp.floor((cnt + (CHUNK - 1)) * (1.0 / CHUNK))
    nchb = jnp.broadcast_to(nch, (N_EXPERTS, LANES))
    rl = lax.broadcasted_iota(I32, (N_EXPERTS, LANES), 0)
    incl = nchb
    for sft in (1, 2, 4, 8):
        incl = incl + jnp.where(rl >= sft, pltpu.roll(incl, sft, 0), 0.0)
    off = (incl - nchb)[:, 0:1] * float(CHUNK)
    rank = _dot(onehot.astype(BF16), tri_ref[...])
    slot_of = off + rank
    pos1 = jnp.sum(jnp.where(oh1, slot_of, 0.0), axis=0, keepdims=True)
    pos2 = jnp.sum(jnp.where(oh2, slot_of, 0.0), axis=0, keepdims=True)

    slot = lax.broadcasted_iota(I32, (SLOTS, tm), 0)
    sel = jnp.where(slot == pos1.astype(I32), 1.0, jnp.where(slot == pos2.astype(I32), 1.0, 0.0))
    hs_ref[...] = _dot(sel.astype(BF16), hb).astype(BF16)

    meta_ref[0] = jnp.concatenate([pos1, pos2, wt1, wt2, jnp.zeros((4, tm), F32)], axis=0)
    nch_ref[0] = nchb.astype(I32)


def _ffn_schedule(nch, max_tiles, dump_base):
    nt = nch.shape[0]
    cend = jnp.cumsum(nch, axis=1)
    coff = cend - nch
    tcum = jnp.cumsum(nch, axis=0)
    before = tcum - nch
    tot = tcum[-1]
    pad = ((tot + FFN_CHUNKS - 1) // FFN_CHUNKS) * FFN_CHUNKS
    eend = jnp.cumsum(pad)
    estart = eend - pad
    n_tiles = (eend[-1] // FFN_CHUNKS).astype(I32)
    first_chunk = jnp.arange(max_tiles, dtype=I32) * FFN_CHUNKS
    tile_expert = jnp.sum((eend[None, :] <= first_chunk[:, None]).astype(I32), axis=1)
    tile_expert = jnp.minimum(tile_expert, N_EXPERTS - 1)
    hp = lax.Precision.HIGHEST
    pos = jnp.arange(max_tiles * FFN_CHUNKS, dtype=I32)
    e_s = jnp.minimum(jnp.sum((eend[None, :] <= (pos // FFN_CHUNKS * FFN_CHUNKS)[:, None]).astype(I32), axis=1),
                      N_EXPERTS - 1)
    oh_e = (e_s[:, None] == jnp.arange(N_EXPERTS, dtype=I32)[None, :]).astype(F32)
    idx = pos - jnp.dot(oh_e, estart.astype(F32), precision=hp).astype(I32)
    run_end = jnp.dot(oh_e, tcum.T.astype(F32), precision=hp).astype(I32)
    run_beg = jnp.dot(oh_e, before.T.astype(F32), precision=hp).astype(I32)
    run_off = jnp.dot(oh_e, coff.T.astype(F32), precision=hp).astype(I32)
    in_run = (idx[:, None] >= run_beg) & (idx[:, None] < run_end)
    tile_base = jnp.arange(nt, dtype=I32)[None, :] * CHUNKS_PER_TILE
    src = jnp.sum(jnp.where(in_run, tile_base + run_off + idx[:, None] - run_beg, 0), axis=1)
    real = jnp.any(in_run, axis=1)
    dump = dump_base + (pos // FFN_CHUNKS % 2) * FFN_CHUNKS + pos % FFN_CHUNKS
    src_rows = jnp.where(real, src, 0) * CHUNK
    dst_rows = jnp.where(real, src, dump) * CHUNK
    used = cend[:, -1]
    ucum = jnp.cumsum(CHUNKS_PER_TILE - used)
    ubeg = ucum - (CHUNKS_PER_TILE - used)
    z = jnp.arange(max_tiles * ZERO_CHUNKS, dtype=I32)[:, None]
    in_gap = (z >= ubeg[None, :]) & (z < ucum[None, :])
    zsrc = jnp.sum(jnp.where(in_gap, tile_base + used[None, :] + z - ubeg[None, :], 0), axis=1)
    zdump = dump_base + 2 * FFN_CHUNKS + z[:, 0] % ZERO_CHUNKS
    zero_rows = jnp.where(jnp.any(in_gap, axis=1), zsrc, zdump) * CHUNK
    return tile_expert, src_rows, dst_rows, n_tiles.reshape(1), zero_rows


def _ffn_kernel(te_ref, sr_ref, dr_ref, nt_ref, zr_ref, hs_hbm, wg_ref, wu_ref, wd_ref, ys_hbm,
                xbuf, ybuf, zbuf, wgb, wub, wdb, in_sem, out_sem, zero_sem, *, dump_base):
    j = pl.program_id(0)
    nt = nt_ref[0]
    half_ff = EXPERT_FF // 2

    def rows_at(r):
        return pl.ds(pl.multiple_of(r, CHUNK), CHUNK)

    def in_copy(step, slot, k, wait=False):
        r = 0 if wait else sr_ref[step * FFN_CHUNKS + k]
        return pltpu.make_async_copy(hs_hbm.at[rows_at(r), :], xbuf.at[slot, pl.ds(k * CHUNK, CHUNK), :],
                                     in_sem.at[slot])

    def out_copy(step, slot, k, wait=False):
        r = 0 if wait else dr_ref[step * FFN_CHUNKS + k]
        return pltpu.make_async_copy(ybuf.at[slot, pl.ds(k * CHUNK, CHUNK), :], ys_hbm.at[rows_at(r), :],
                                     out_sem.at[slot])

    def zero_copy(k, wait=False):
        r = 0 if wait else zr_ref[j * ZERO_CHUNKS + k]
        return pltpu.make_async_copy(zbuf.at[pl.ds(k * CHUNK, CHUNK), :], ys_hbm.at[rows_at(r), :], zero_sem)

    @pl.when(j == 0)
    def _():
        zbuf[...] = jnp.zeros_like(zbuf)
        fills = [pltpu.make_async_copy(
            zbuf, ys_hbm.at[pl.ds((dump_base + r * ZERO_CHUNKS) * CHUNK, ZERO_CHUNKS * CHUNK), :], zero_sem)
            for r in range(DUMP_CHUNKS // ZERO_CHUNKS)]
        for cp in fills:
            cp.start()
        for cp in fills:
            cp.wait()

    for k in range(ZERO_CHUNKS):
        zero_copy(k).start()

    @pl.when(j < nt)
    def _():
        slot = j % 2

        @pl.when(j == 0)
        def _():
            for k in range(FFN_CHUNKS):
                in_copy(0, 0, k).start()

        @pl.when(j + 1 < nt)
        def _():
            for k in range(FFN_CHUNKS):
                in_copy(j + 1, 1 - slot, k).start()

        @pl.when((j == 0) | (te_ref[j] != te_ref[jnp.maximum(j - 1, 0)]))
        def _():
            wgb[...] = wg_ref[0].astype(BF16)
            wub[...] = wu_ref[0].astype(BF16)
            wdb[...] = wd_ref[0].astype(BF16)

        for k in range(FFN_CHUNKS):
            in_copy(j, slot, k, wait=True).wait()

        @pl.when(j >= 2)
        def _():
            for k in range(FFN_CHUNKS):
                out_copy(j - 2, slot, k, wait=True).wait()

        x = xbuf[slot]
        hg = [_dot(x, wgb[:, h * half_ff:(h + 1) * half_ff]) for h in range(2)]
        hu = [_dot(x, wub[:, h * half_ff:(h + 1) * half_ff]) for h in range(2)]
        y = None
        for h in range(2):
            act = ((hg[h] / (1.0 + jnp.exp(-hg[h]))) * hu[h]).astype(BF16)
            part = _dot(act, wdb[h * half_ff:(h + 1) * half_ff, :])
            y = part if y is None else y + part
        ybuf[slot] = y.astype(BF16)
        for k in range(FFN_CHUNKS):
            out_copy(j, slot, k).start()

        @pl.when(j == nt - 1)
        def _():
            for k in range(FFN_CHUNKS):
                out_copy(j, slot, k, wait=True).wait()

            @pl.when(j >= 1)
            def _():
                for k in range(FFN_CHUNKS):
                    out_copy(j - 1, 1 - slot, k, wait=True).wait()

    for k in range(ZERO_CHUNKS):
        zero_copy(k, wait=True).wait()


def _ffn_call(tile_expert, src_rows, dst_rows, n_tiles, zero_rows, hs, wg, wu, wd, layer, max_tiles):
    rows, d = hs.shape
    ff = wg.shape[-1]
    wmap = lambda j, te, sr, dr, nt, zr: (layer, te[j], 0, 0)
    grid_spec = pltpu.PrefetchScalarGridSpec(
        num_scalar_prefetch=5,
        grid=(max_tiles,),
        in_specs=[
            pl.BlockSpec(memory_space=pl.ANY),
            pl.BlockSpec((None, 1, d, ff), wmap),
            pl.BlockSpec((None, 1, d, ff), wmap),
            pl.BlockSpec((None, 1, ff, d), wmap),
        ],
        out_specs=pl.BlockSpec(memory_space=pl.ANY),
        scratch_shapes=[
            pltpu.VMEM((2, FFN_ROWS, d), BF16),
            pltpu.VMEM((2, FFN_ROWS, d), BF16),
            pltpu.VMEM((ZERO_CHUNKS * CHUNK, d), BF16),
            pltpu.VMEM((d, ff), BF16),
            pltpu.VMEM((d, ff), BF16),
            pltpu.VMEM((ff, d), BF16),
            pltpu.SemaphoreType.DMA((2,)),
            pltpu.SemaphoreType.DMA((2,)),
            pltpu.SemaphoreType.DMA(()),
        ],
    )
    return pl.pallas_call(
        functools.partial(_ffn_kernel, dump_base=rows // CHUNK),
        grid_spec=grid_spec,
        out_shape=jax.ShapeDtypeStruct((rows + DUMP_CHUNKS * CHUNK, d), BF16),
        compiler_params=pltpu.CompilerParams(vmem_limit_bytes=48 * 1024 * 1024),
        name="expert_ffn",
    )(tile_expert, src_rows, dst_rows, n_tiles, zero_rows, hs, wg, wu, wd)


def _combine_body(ys_ref, meta_ref, x_ref, g2_ref):
    tm = x_ref.shape[1]
    meta = meta_ref[0]
    eye = (lax.broadcasted_iota(I32, (tm, tm), 0) == lax.broadcasted_iota(I32, (tm, tm), 1))

    def as_col(row):
        return jnp.sum(jnp.where(eye, row, 0.0), axis=1, keepdims=True)

    pos1, pos2 = as_col(meta[0:1]), as_col(meta[1:2])
    w1, w2 = as_col(meta[2:3]), as_col(meta[3:4])
    slot = lax.broadcasted_iota(I32, (tm, SLOTS), 1).astype(F32)
    gate = jnp.where(slot == pos1, w1, 0.0) + jnp.where(slot == pos2, w2, 0.0)
    y = _dot(gate.astype(BF16), ys_ref[...])
    return x_ref[0] + _batch_row(g2_ref) * y


def _combine_final_kernel(ys_ref, meta_ref, x_ref, g2_ref, gf_ref, xo_ref):
    xo = _combine_body(ys_ref, meta_ref, x_ref, g2_ref)
    xo_ref[0] = xo * lax.rsqrt(jnp.mean(xo * xo, axis=-1, keepdims=True) + NORM_EPS) * gf_ref[...]


def _combine_inproj_kernel(ys_ref, meta_ref, x_ref, g2_ref, *refs):
    n_in = 10
    xo_ref = refs[n_in]
    xo = _combine_body(ys_ref, meta_ref, x_ref, g2_ref)
    xo_ref[0] = xo
    _inproj_body(xo, *refs[:n_in], *refs[n_in + 1:])


def _combine_specs(d, tiles_per_batch, mod, layer):
    flat = lambda bi, i: bi * tiles_per_batch + i
    return [
        pl.BlockSpec((SLOTS, d), lambda bi, i: (flat(bi, i), 0)),
        pl.BlockSpec((1, 8, TM), lambda bi, i: (flat(bi, i), 0, 0)),
        pl.BlockSpec((1, TM, d), lambda bi, i: (bi, i, 0)),
        _mod_spec(mod, layer, 5),
    ]


def _combine_final_call(ys, meta, x, mod, layer, gf):
    b, s, d = x.shape
    return pl.pallas_call(
        _combine_final_kernel,
        grid=(b, s // TM),
        in_specs=_combine_specs(d, s // TM, mod, layer) + [pl.BlockSpec((1, d), lambda bi, i: (0, 0))],
        out_specs=pl.BlockSpec((1, TM, d), lambda bi, i: (bi, i, 0)),
        out_shape=jax.ShapeDtypeStruct((b, s, d), F32),
        compiler_params=pltpu.CompilerParams(vmem_limit_bytes=48 * 1024 * 1024),
        name="moe_combine",
    )(ys, meta, x, mod, gf)


def _combine_inproj_call(ys, meta, x, mod, g, w_in, layer, tabs, p4, p16):
    b, s, d = x.shape
    in_specs, out_specs, out_shape = _inproj_specs(b, s, d, layer, mod)
    xspec = pl.BlockSpec((1, TM, d), lambda bi, i: (bi, i, 0))
    return pl.pallas_call(
        _combine_inproj_kernel,
        grid=(b, s // TM),
        in_specs=_combine_specs(d, s // TM, mod, layer - 1) + in_specs,
        out_specs=[xspec] + out_specs,
        out_shape=[jax.ShapeDtypeStruct((b, s, d), F32)] + out_shape,
        scratch_shapes=[pltpu.VMEM((d, IN_WIDTH), BF16)],
        compiler_params=pltpu.CompilerParams(vmem_limit_bytes=56 * 1024 * 1024),
        name="combine_in_proj",
    )(ys, meta, x, mod, mod, mod, g, w_in, *tabs, p4, p16)


def _rope_tables(positions):
    pos = positions.astype(F32)[..., None]

    def table(dim):
        inv = ROPE_THETA ** (-jnp.arange(0, dim, 2, dtype=F32) / dim)
        ang = pos * inv
        cos, sin = jnp.cos(ang), jnp.sin(ang)
        reps = LANES // dim
        return (jnp.tile(jnp.concatenate([cos, cos], -1), (1, 1, reps)),
                jnp.tile(jnp.concatenate([-sin, sin], -1), (1, 1, reps)))

    c64, s64 = table(HEAD_DIM)
    c32, s32 = table(DIFF_QK_DIM)
    return c64, s64, c32, s32


def kernel(x, c, positions, ada_w, ada_b, norm_mix_g, norm_ffn_g, w_in, w_out, diff_lambda_q1, diff_lambda_k1,
           diff_lambda_q2, diff_lambda_k2, diff_subln_g, swa_sinks, router_group_w, router_group_b,
           router_expert_w, router_expert_b, expert_w_gate, expert_w_up, expert_w_down, final_norm_g):
    b, s, d = x.shape
    depth = ada_w.shape[0]
    n = b * s
    nt = n // TM
    max_tiles = (nt * CHUNKS_PER_TILE + N_EXPERTS * (FFN_CHUNKS - 1)) // FFN_CHUNKS + 1

    tabs = _rope_tables(positions)
    p4 = _residue_perm(TM, 4)
    p16 = _residue_perm(TM, 16)
    p4_b, p16_b = jnp.asarray(p4, BF16), jnp.asarray(p16, BF16)
    p4t_b, p16t_b = jnp.asarray(p4.T, BF16), jnp.asarray(p16.T, BF16)
    tri = jnp.asarray(np.triu(np.ones((TM, TM), np.float32), 1), BF16)

    c_pad = jnp.pad(c, ((0, 8 - b), (0, 0)))
    mod = _ada_call(c_pad, ada_w, ada_b)

    sink_order = np.asarray(_SWA_HEAD_ORDER, np.int32)
    zpad = lambda k: jnp.zeros((depth, d, k), F32)
    wr = jnp.concatenate([router_group_w, zpad(8 - N_GROUPS), router_expert_w,
                          zpad(ROUTER_COLS - 8 - N_EXPERTS)], axis=-1)
    wr_hi = wr.astype(BF16)
    wr = jnp.concatenate([wr_hi, (wr - wr_hi.astype(F32)).astype(BF16)], axis=-1)
    br = jnp.concatenate([router_group_b, jnp.full((depth, 8 - N_GROUPS), NEG_INF, F32), router_expert_b,
                          jnp.zeros((depth, ROUTER_COLS - 8 - N_EXPERTS), F32)], axis=-1).reshape(depth, 1, -1)

    lam_init = [0.8 - 0.6 * math.exp(-0.3 * l) for l in range(depth)]
    lam = (jnp.exp(jnp.sum(diff_lambda_q1 * diff_lambda_k1, axis=-1))
           - jnp.exp(jnp.sum(diff_lambda_q2 * diff_lambda_k2, axis=-1)) + jnp.asarray(lam_init, F32))
    g_cols = jnp.broadcast_to(diff_subln_g[:, :, None], (depth, HEAD_DIM, TQ))
    sinks = swa_sinks[:, sink_order].reshape(-1)
    g_mix = norm_mix_g.reshape(depth, 1, d)
    g_ffn = norm_ffn_g.reshape(depth, 1, d)

    proj, vat, qkv4, qkv16 = _inproj_call(x, mod, g_mix, w_in, 0, tabs, p4_b, p16_b)
    for l in range(depth):
        oa = _diff_attn_call(proj, vat, lam, g_cols, lam_init[l], l)
        ob = _band_call(proj, COL_QB, COL_KB, COL_VB, nqb=4, nkb=1, max_dist=SWA_WINDOW - 1,
                        sinks=sinks, sink_base=l * SWA_Q_HEADS, name="swa")[0]
        o1, l1 = _band_call(proj, COL_QC, COL_KC, COL_VC, nqb=2, nkb=2, max_dist=BAND, want_lse=True,
                            name="dil1")
        o4, l4 = _band_call(qkv4.reshape(b * 4, s // 4, 768), 0, 256, 512, nqb=2, nkb=2, max_dist=BAND,
                            want_lse=True, name="dil4")
        o16, l16 = _band_call(qkv16.reshape(b * 16, s // 16, 768), 0, 256, 512, nqb=2, nkb=2, max_dist=BAND,
                              want_lse=True, name="dil16")
        x, hs, meta, nch = _outproj_router_call(
            oa, ob, o1, l1, o4.reshape(b, 4, s // 4, 256), l4.reshape(b, 4, s // 4, 512),
            o16.reshape(b, 16, s // 16, 256), l16.reshape(b, 16, s // 16, 512), p4t_b, p16t_b, w_out, l, x, mod,
            g_ffn, wr, br, tri)
        sched = _ffn_schedule(nch[:, :, 0], max_tiles, nt * CHUNKS_PER_TILE)
        ys = _ffn_call(*sched, hs, expert_w_gate, expert_w_up, expert_w_down, l, max_tiles)
        if l + 1 < depth:
            x, proj, vat, qkv4, qkv16 = _combine_inproj_call(ys, meta, x, mod, g_mix, w_in, l + 1, tabs,
                                                             p4_b, p16_b)
        else:
            x = _combine_final_call(ys, meta, x, mod, l, final_norm_g.reshape(1, d))
    return x
```

```python
import functools
import math

import numpy as np
import jax
import jax.numpy as jnp
from jax import lax
from jax.experimental import pallas as pl
from jax.experimental.pallas import tpu as pltpu

F32 = jnp.float32
BF16 = jnp.bfloat16
I32 = jnp.int32

HEAD_DIM = 64
ROPE_THETA = 10000.0
NORM_EPS = 1e-6
NEG_INF = -1e30
DIFF_HEADS = 4
DIFF_QK_DIM = 32
SWA_Q_HEADS = 8
SWA_KV_HEADS = 2
SWA_WINDOW = 128
DIL_PATTERNS = ((128, 1), (512, 4), (2048, 16))
N_GROUPS = 4
EXPERTS_PER_GROUP = 4
N_EXPERTS = 16
EXPERT_FF = 512
N_ADA = 6
IN_WIDTH = 2304

LANES = 128
BF16_ROWS = 16
BAND = 128

TM = 256
ROWS = 2 * TM
CHUNK = BF16_ROWS
SLOTS = 2 * TM + N_EXPERTS * CHUNK
CHUNKS_PER_TILE = SLOTS // CHUNK
FFN_ROWS = 512
FFN_CHUNKS = FFN_ROWS // CHUNK
ZERO_CHUNKS = -(-N_EXPERTS * FFN_CHUNKS // CHUNKS_PER_TILE)
DUMP_CHUNKS = -(-(2 * FFN_CHUNKS + ZERO_CHUNKS) // ZERO_CHUNKS) * ZERO_CHUNKS
TQ = 512
BAND_ROWS = 1024

_SWA_HEAD_ORDER = (0, 4, 1, 5, 2, 6, 3, 7)
COL_QB, COL_KB, COL_VB = 0, 512, 640
COL_Q1, COL_Q2, COL_K1, COL_K2, COL_VA = 768, 896, 1024, 1152, 1280
COL_QC, COL_KC, COL_VC = 1536, 1792, 2048
_SRC_ROPE = (32, 32, 32, 32, 0, 0, 64, 64, 64, 64, 64, 0, 64, 64, 64, 64, 0, 0)
_SRC_DEST = (6, 7, 8, 9, 10, 11, None, None, None, None, 4, 5, 12, 13, 14, 15, 16, 17)
VT_ROWS = HEAD_DIM + BF16_ROWS
_SRC_VA_CHUNK = 2


def _residue_perm(tm, d):
    p = np.zeros((tm, tm), np.float32)
    per = tm // d
    for l in range(per):
        for r in range(d):
            p[r * per + l, l * d + r] = 1.0
    return p


def _dot(a, b, **kw):
    return jnp.dot(a, b, preferred_element_type=F32, **kw)


def _dot_nt(a, b):
    return lax.dot_general(a, b, (((1,), (1,)), ((), ())), preferred_element_type=F32)


def _batch_row(ref):
    return ref[pl.ds(pl.program_id(0), 1), :]


def _modulated_norm(x, g, sc, sh):
    y = x * lax.rsqrt(jnp.mean(x * x, axis=-1, keepdims=True) + NORM_EPS)
    return (y * g) * (1.0 + sc) + sh


def _ada_kernel(c_ref, w_ref, b_ref, o_ref):
    c = c_ref[...]
    ca = c / (1.0 + jnp.exp(-c))
    o_ref[0] = _dot(ca, w_ref[0], precision=lax.Precision.HIGHEST) + b_ref[0]


def _ada_call(c_pad, ada_w, ada_b):
    depth, d, n = ada_w.shape
    tn = 1536
    return pl.pallas_call(
        _ada_kernel,
        grid=(depth, n // tn),
        in_specs=[
            pl.BlockSpec((c_pad.shape[0], d), lambda l, j: (0, 0)),
            pl.BlockSpec((1, d, tn), lambda l, j: (l, 0, j)),
            pl.BlockSpec((1, 1, tn), lambda l, j: (l, 0, j)),
        ],
        out_specs=pl.BlockSpec((1, c_pad.shape[0], tn), lambda l, j: (l, 0, j)),
        out_shape=jax.ShapeDtypeStruct((depth, c_pad.shape[0], n), F32),
        compiler_params=pltpu.CompilerParams(vmem_limit_bytes=40 * 1024 * 1024),
        name="ada_mod",
    )(c_pad, ada_w, ada_b.reshape(depth, 1, n))


def _rope(t, cos, sin_signed, first_half, half):
    rot = jnp.where(first_half, pltpu.roll(t, LANES - half, 1), pltpu.roll(t, half, 1))
    return t * cos + rot * sin_signed


def _inproj_body(x, sc_ref, sh_ref, g_ref, w_ref, cs64_ref, sn64_ref, cs32_ref, sn32_ref,
                 p4_ref, p16_ref, proj_ref, vat_ref, c4_ref, c16_ref, wb):
    @pl.when((pl.program_id(0) == 0) & (pl.program_id(1) == 0))
    def _():
        wb[...] = w_ref[...].astype(BF16)

    h = _modulated_norm(x, g_ref[...], _batch_row(sc_ref), _batch_row(sh_ref))
    hb = h.astype(BF16)
    lane = lax.broadcasted_iota(I32, (1, LANES), 1)
    first64 = (lane % 64) < 32
    first32 = (lane % 32) < 16
    lo_half = lane < 64
    swa_q = []
    for cb in range(IN_WIDTH // 256):
        acc = _dot(hb, wb[:, cb * 256:(cb + 1) * 256])
        if cb == _SRC_VA_CHUNK:
            acc_t = acc.T.astype(BF16)
            for hd in range(DIFF_HEADS):
                vat_ref[0, hd * VT_ROWS:hd * VT_ROWS + HEAD_DIM, :] = acc_t[hd * HEAD_DIM:(hd + 1) * HEAD_DIM]
                vat_ref[0, hd * VT_ROWS + HEAD_DIM:(hd + 1) * VT_ROWS, :] = jnp.ones(
                    (BF16_ROWS, acc_t.shape[1]), BF16)
        for half in range(2):
            src = cb * 2 + half
            t = acc[:, half * LANES:(half + 1) * LANES]
            if _SRC_ROPE[src] == 64:
                t = _rope(t, cs64_ref[0], sn64_ref[0], first64, 32)
            elif _SRC_ROPE[src] == 32:
                t = _rope(t, cs32_ref[0], sn32_ref[0], first32, 16)
            dst = _SRC_DEST[src]
            if dst is None:
                swa_q.append(t)
            else:
                proj_ref[0, :, dst * LANES:(dst + 1) * LANES] = t.astype(BF16)
    for jb in range(SWA_Q_HEADS // 2):
        a, c = swa_q[jb // 2], swa_q[2 + jb // 2]
        if jb % 2 == 0:
            blk = jnp.where(lo_half, a, pltpu.roll(c, 64, 1))
        else:
            blk = jnp.where(lo_half, pltpu.roll(a, 64, 1), c)
        proj_ref[0, :, jb * LANES:(jb + 1) * LANES] = blk.astype(BF16)
    tm = TM
    for blk in range(x.shape[0] // tm):
        cc = proj_ref[0, blk * tm:(blk + 1) * tm, COL_QC:]
        c4 = _dot(p4_ref[...], cc).astype(BF16)
        for r in range(4):
            c4_ref[0, r, blk * (tm // 4):(blk + 1) * (tm // 4)] = c4[r * (tm // 4):(r + 1) * (tm // 4)]
        c16 = _dot(p16_ref[...], cc).astype(BF16)
        for r in range(16):
            c16_ref[0, r, blk * (tm // 16):(blk + 1) * (tm // 16)] = c16[r * (tm // 16):(r + 1) * (tm // 16)]


def _inproj_kernel(x_ref, *refs):
    _inproj_body(x_ref[0], *refs)


def _mod_spec(mod, layer, k):
    return pl.BlockSpec((None, mod.shape[1], mod.shape[2] // N_ADA), lambda bi, i: (layer, 0, k))


def _const_spec(shape, index_map):
    return pl.BlockSpec(shape, index_map, pipeline_mode=pl.Buffered(1))


def _inproj_specs(b, s, d, layer, mod):
    rows = ROWS
    row = lambda bi, i: (bi, i, 0)
    const2 = lambda bi, i: (0, 0)
    in_specs = [
        _mod_spec(mod, layer, 1),
        _mod_spec(mod, layer, 0),
        pl.BlockSpec((None, 1, d), lambda bi, i: (layer, 0, 0)),
        _const_spec((None, d, IN_WIDTH), lambda bi, i: (layer, 0, 0)),
        pl.BlockSpec((1, rows, LANES), row),
        pl.BlockSpec((1, rows, LANES), row),
        pl.BlockSpec((1, rows, LANES), row),
        pl.BlockSpec((1, rows, LANES), row),
        _const_spec((TM, TM), const2),
        _const_spec((TM, TM), const2),
    ]
    out_specs = [
        pl.BlockSpec((1, rows, IN_WIDTH), row),
        pl.BlockSpec((1, DIFF_HEADS * VT_ROWS, rows), lambda bi, i: (bi, 0, i)),
        pl.BlockSpec((1, 4, rows // 4, 768), lambda bi, i: (bi, 0, i, 0)),
        pl.BlockSpec((1, 16, rows // 16, 768), lambda bi, i: (bi, 0, i, 0)),
    ]
    out_shape = [
        jax.ShapeDtypeStruct((b, s, IN_WIDTH), BF16),
        jax.ShapeDtypeStruct((b, DIFF_HEADS * VT_ROWS, s), BF16),
        jax.ShapeDtypeStruct((b, 4, s // 4, 768), BF16),
        jax.ShapeDtypeStruct((b, 16, s // 16, 768), BF16),
    ]
    return in_specs, out_specs, out_shape


def _inproj_call(x, mod, g, w_in, layer, tabs, p4, p16):
    b, s, d = x.shape
    in_specs, out_specs, out_shape = _inproj_specs(b, s, d, layer, mod)
    return pl.pallas_call(
        _inproj_kernel,
        grid=(b, s // ROWS),
        in_specs=[pl.BlockSpec((1, ROWS, d), lambda bi, i: (bi, i, 0))] + in_specs,
        out_specs=out_specs,
        out_shape=out_shape,
        scratch_shapes=[pltpu.VMEM((d, IN_WIDTH), BF16)],
        compiler_params=pltpu.CompilerParams(vmem_limit_bytes=56 * 1024 * 1024),
        name="in_proj",
    )(x, mod, mod, g, w_in, *tabs, p4, p16)


def _diff_attn_kernel(lam_ref, q1_ref, q2_ref, q1n_ref, q2n_ref, k1_ref, k2_ref, vt_ref, g_ref, o_ref,
                      m_sc, acc_sc, s_sc, *, lambda_init, layer):
    tq = q1_ref.shape[1]
    qi = pl.program_id(1)
    lam = lam_ref[layer]
    to_log2 = DIFF_QK_DIM ** -0.5 * math.log2(math.e)
    lane = lax.broadcasted_iota(I32, (1, LANES), 1)

    def head_queries(qa_ref, qb_ref):
        qa = qa_ref[0].astype(F32) * to_log2
        qb = qb_ref[0].astype(F32) * to_log2
        out = []
        for h in range(DIFF_HEADS):
            hm = (lane // DIFF_QK_DIM) == h
            out.append((jnp.where(hm, qa, 0.0).astype(BF16), jnp.where(hm, qb, 0.0).astype(BF16)))
        return out

    qh = head_queries(q1_ref, q2_ref)
    qh_next = head_queries(q1n_ref, q2n_ref)
    causal = (lax.broadcasted_iota(I32, (tq, tq), 0) <= lax.broadcasted_iota(I32, (tq, tq), 1))

    m_sc[...] = jnp.full(m_sc.shape, NEG_INF, F32)
    acc_sc[...] = jnp.zeros(acc_sc.shape, F32)

    n_chain = 2 * DIFF_HEADS

    def scores(ch, tile, queries):
        start = pl.multiple_of(tile * tq, tq)
        k_ref = k1_ref if ch % 2 == 0 else k2_ref
        return _dot_nt(k_ref[0, pl.ds(start, tq), :], queries[ch // 2][ch % 2])

    @pl.when(qi == 0)
    def _():
        for ch in range(n_chain):
            s_sc[ch] = scores(ch, 0, qh)

    def step(j, last):
        start = pl.multiple_of(j * tq, tq)
        for ch in range(n_chain):
            st = s_sc[ch]
            s_sc[ch] = scores(ch, 0, qh_next) if last else scores(ch, j + 1, qh)
            h = ch // 2
            vt = vt_ref[0, h * VT_ROWS:(h + 1) * VT_ROWS, pl.ds(start, tq)]
            if last:
                st = jnp.where(causal, st, NEG_INF)
            m_old = m_sc[ch]
            m_new = jnp.maximum(m_old, jnp.max(st, axis=0, keepdims=True))
            p = jnp.exp2(st - m_new)
            al = jnp.exp2(m_old - m_new)
            acc_sc[ch] = al * acc_sc[ch] + _dot(vt, p.astype(BF16))
            m_sc[ch] = m_new

    def body(j, carry):
        step(j, False)
        return carry

    lax.fori_loop(0, qi, body, 0)
    step(qi, True)

    g = g_ref[...]
    outs = []
    for h in range(DIFF_HEADS):
        a1, a2 = acc_sc[2 * h], acc_sc[2 * h + 1]
        o = (a1[:HEAD_DIM] / a1[HEAD_DIM:HEAD_DIM + 1]
             - lam * (a2[:HEAD_DIM] / a2[HEAD_DIM:HEAD_DIM + 1]))
        ms = jnp.mean(o * o, axis=0, keepdims=True)
        outs.append((o * lax.rsqrt(ms + NORM_EPS)) * g * (1.0 - lambda_init))
    o_ref[0] = jnp.concatenate(outs, axis=0).T.astype(BF16)


def _diff_attn_call(proj, vat, lam, g_cols, lambda_init, layer):
    b, s, _ = proj.shape
    tq = TQ
    qspec = lambda cb: pl.BlockSpec((1, tq, LANES), lambda bi, i, cb=cb: (bi, i, cb))
    last_q = s // tq - 1
    qnext = lambda cb: pl.BlockSpec((1, tq, LANES), lambda bi, i, cb=cb: (bi, jnp.minimum(i + 1, last_q), cb))
    kspec = lambda cb: pl.BlockSpec((1, s, LANES), lambda bi, i, cb=cb: (bi, 0, cb))
    n_chain = 2 * DIFF_HEADS
    return pl.pallas_call(
        functools.partial(_diff_attn_kernel, lambda_init=lambda_init, layer=layer),
        grid=(b, s // tq),
        in_specs=[
            pl.BlockSpec(memory_space=pltpu.SMEM),
            qspec(COL_Q1 // LANES), qspec(COL_Q2 // LANES),
            qnext(COL_Q1 // LANES), qnext(COL_Q2 // LANES),
            kspec(COL_K1 // LANES), kspec(COL_K2 // LANES),
            pl.BlockSpec((1, DIFF_HEADS * VT_ROWS, s), lambda bi, i: (bi, 0, 0)),
            pl.BlockSpec((None, HEAD_DIM, tq), lambda bi, i: (layer, 0, 0)),
        ],
        out_specs=pl.BlockSpec((1, tq, 256), lambda bi, i: (bi, i, 0)),
        out_shape=jax.ShapeDtypeStruct((b, s, 256), BF16),
        scratch_shapes=[
            pltpu.VMEM((n_chain, 1, tq), F32),
            pltpu.VMEM((n_chain, VT_ROWS, tq), F32),
            pltpu.VMEM((n_chain, tq, tq), F32),
        ],
        compiler_params=pltpu.CompilerParams(vmem_limit_bytes=48 * 1024 * 1024),
        name="diff_attn",
    )(lam, proj, proj, proj, proj, proj, proj, vat, g_cols)


def _band_kernel(*refs, nqb, nkb, max_dist, has_sink, want_lse, sink_base):
    it = iter(refs)
    sink_ref = next(it) if has_sink else None
    q_ref, kp_ref, kc_ref, vp_ref, vc_ref = (next(it) for _ in range(5))
    o_ref = next(it)
    lse_ref = next(it) if want_lse else None
    kbuf, vbuf = next(it), next(it)
    nseq, rows = q_ref.shape[0], q_ref.shape[1]
    i = pl.program_id(1)
    for sq in range(nseq):
        kbuf[sq, 0:BAND, :] = kp_ref[sq]
        kbuf[sq, BAND:, :] = kc_ref[sq]
        for kb in range(nkb):
            vbuf[sq, 0:BAND, kb * 256:kb * 256 + LANES] = vp_ref[sq, :, kb * LANES:(kb + 1) * LANES]
            vbuf[sq, BAND:, kb * 256:kb * 256 + LANES] = vc_ref[sq, :, kb * LANES:(kb + 1) * LANES]
            vbuf[sq, :, kb * 256 + LANES:(kb + 1) * 256] = jnp.ones((BAND + rows, LANES), BF16)
    lane = lax.broadcasted_iota(I32, (1, LANES), 1)
    lo_half = lane < 64
    r_io = lax.broadcasted_iota(I32, (BAND, 2 * BAND), 0)
    c_io = lax.broadcasted_iota(I32, (BAND, 2 * BAND), 1)
    dist = BAND + r_io - c_io
    band = (dist >= 0) & (dist <= max_dist)
    band_first = band & ((c_io >= BAND) | (i > 0))
    col0 = lax.broadcasted_iota(I32, (1, 2 * BAND), 1) == 0
    vr = lax.broadcasted_iota(I32, (2 * BAND, 2 * LANES), 0)
    vc = lax.broadcasted_iota(I32, (2 * BAND, 2 * LANES), 1)
    sink_row = (vr == 0) & (vc < LANES)
    to_log2 = HEAD_DIM ** -0.5 * math.log2(math.e)
    units = [(sq, sb, qb) for sq in range(nseq) for sb in range(rows // BAND) for qb in range(nqb)]

    def scores(u):
        sq, sb, qb = units[u]
        kb = qb if nkb > 1 else 0
        q = q_ref[sq, sb * BAND:(sb + 1) * BAND, qb * LANES:(qb + 1) * LANES].astype(F32) * to_log2
        q2 = jnp.concatenate([jnp.where(lo_half, q, 0.0), jnp.where(lo_half, 0.0, q)], axis=0).astype(BF16)
        return _dot_nt(q2, kbuf[sq, sb * BAND:(sb + 2) * BAND, kb * LANES:(kb + 1) * LANES])

    ahead = 2
    pending = [scores(u) for u in range(min(ahead, len(units)))]
    for u, (sq, sb, qb) in enumerate(units):
        if u + ahead < len(units):
            pending.append(scores(u + ahead))
        s2 = pending[u]
        pending[u] = None
        kb = qb if nkb > 1 else 0
        r0 = sb * BAND
        msk = band_first if sb == 0 else band
        halves = []
        for hh in range(2):
            if has_sink:
                fill = jnp.where(col0, sink_ref[sink_base + qb * 2 + hh] * math.log2(math.e), NEG_INF)
            else:
                fill = NEG_INF
            halves.append(jnp.where(msk, s2[hh * BAND:(hh + 1) * BAND], fill))
        s2 = jnp.concatenate(halves, axis=0)
        m = jnp.max(s2, axis=1, keepdims=True)
        p = jnp.exp2(s2 - m).astype(BF16)
        vw = vbuf[sq, r0:r0 + 2 * BAND, kb * 256:(kb + 1) * 256]
        if has_sink:
            vw = jnp.where(sink_row, jnp.zeros_like(vw), vw)
        pv = _dot(p, vw)
        den = pv[:, LANES:]
        out = pv[:, :LANES] / den
        o = jnp.where(lo_half, out[:BAND], out[BAND:])
        o_ref[sq, r0:r0 + BAND, qb * LANES:(qb + 1) * LANES] = o.astype(BF16)
        if want_lse:
            lse2 = m + jnp.log2(den)
            ls = jnp.where(lo_half, lse2[:BAND], lse2[BAND:])
            hi = ls.astype(BF16)
            lo = (ls - hi.astype(F32)).astype(BF16)
            lse_ref[sq, r0:r0 + BAND, qb * LANES:(qb + 1) * LANES] = hi
            lse_ref[sq, r0:r0 + BAND, (nqb + qb) * LANES:(nqb + qb + 1) * LANES] = lo


def _band_call(arr, q_col, k_col, v_col, nqb, nkb, max_dist, sinks=None, sink_base=0, want_lse=False,
               name="band"):
    ns, length, _ = arr.shape
    rows = min(BAND_ROWS, length)
    nseq = BAND_ROWS // rows
    wq, wk = nqb * LANES, nkb * LANES
    rpb = rows // BAND
    cur = lambda col, w: pl.BlockSpec((nseq, rows, w), lambda n, i, c=col // w: (n, i, c))
    prev = lambda col, w: pl.BlockSpec(
        (nseq, BAND, w), lambda n, i, c=col // w: (n, jnp.maximum(i * rpb - 1, 0), c))
    in_specs = [cur(q_col, wq), prev(k_col, wk), cur(k_col, wk), prev(v_col, wk), cur(v_col, wk)]
    args = [arr] * 5
    if sinks is not None:
        in_specs = [pl.BlockSpec(memory_space=pltpu.SMEM)] + in_specs
        args = [sinks] + args
    out_specs = [pl.BlockSpec((nseq, rows, wq), lambda n, i: (n, i, 0))]
    out_shape = [jax.ShapeDtypeStruct((ns, length, wq), BF16)]
    if want_lse:
        out_specs.append(pl.BlockSpec((nseq, rows, 2 * wq), lambda n, i: (n, i, 0)))
        out_shape.append(jax.ShapeDtypeStruct((ns, length, 2 * wq), BF16))
    return pl.pallas_call(
        functools.partial(_band_kernel, nqb=nqb, nkb=nkb, max_dist=max_dist,
                          has_sink=sinks is not None, want_lse=want_lse, sink_base=sink_base),
        grid=(ns // nseq, length // rows),
        in_specs=in_specs,
        out_specs=out_specs,
        out_shape=out_shape,
        scratch_shapes=[pltpu.VMEM((nseq, BAND + rows, wk), BF16), pltpu.VMEM((nseq, BAND + rows, 2 * wk), BF16)],
        name=name,
    )(*args)


def _outproj_router_kernel(oa_ref, ob_ref, o1_ref, l1_ref, o4_ref, l4_ref, o16_ref, l16_ref, p4t_ref, p16t_ref,
                           w_ref, x_ref, g1_ref, sc_ref, sh_ref, g_ref, wr_ref, br_ref, tri_ref,
                           xo_ref, hs_ref, meta_ref, nch_ref, wb):
    @pl.when((pl.program_id(0) == 0) & (pl.program_id(1) == 0))
    def _():
        wb[0:256, :] = w_ref[0:256, :].astype(BF16)
        for pos, head in enumerate(_SWA_HEAD_ORDER):
            wb[256 + pos * HEAD_DIM:256 + (pos + 1) * HEAD_DIM, :] = (
                w_ref[256 + head * HEAD_DIM:256 + (head + 1) * HEAD_DIM, :].astype(BF16))
        wb[768:1024, :] = w_ref[768:1024, :].astype(BF16)

    hw = o1_ref.shape[2]
    nblk = x_ref.shape[1] // TM

    def lse_of(v):
        return v[:, :hw] + v[:, hw:]

    def unpermute(pt_ref, src_ref, width):
        per = src_ref.shape[2] // nblk
        return jnp.concatenate(
            [_dot(pt_ref[...], src_ref[0, :, k * per:(k + 1) * per, :].reshape(TM, width)) for k in range(nblk)],
            axis=0)

    o1 = o1_ref[0].astype(F32)
    ls1 = lse_of(l1_ref[0].astype(F32))
    o4 = unpermute(p4t_ref, o4_ref, hw)
    ls4 = lse_of(unpermute(p4t_ref, l4_ref, 2 * hw))
    o16 = unpermute(p16t_ref, o16_ref, hw)
    ls16 = lse_of(unpermute(p16t_ref, l16_ref, 2 * hw))
    mx = jnp.maximum(jnp.maximum(ls1, ls4), ls16)
    e1, e4, e16 = jnp.exp2(ls1 - mx), jnp.exp2(ls4 - mx), jnp.exp2(ls16 - mx)
    oc = (e1 * o1 + e4 * o4 + e16 * o16) / (e1 + e4 + e16)
    mix = (_dot(oa_ref[0], wb[0:256, :]) + _dot(ob_ref[0], wb[256:768, :])
           + _dot(oc.astype(BF16), wb[768:1024, :]))
    x1 = x_ref[0] + _batch_row(g1_ref) * mix
    xo_ref[0] = x1
    _router_body(x1, sc_ref, sh_ref, g_ref, wr_ref, br_ref, tri_ref, hs_ref, meta_ref, nch_ref)


def _outproj_router_call(oa, ob, o1, l1, o4, l4, o16, l16, p4t, p16t, w_out, layer, x, mod, g, wr, br, tri):
    b, s, d = x.shape
    rows = ROWS
    nblk = rows // TM
    steps = s // rows
    nt = b * s // TM
    row = lambda w_: pl.BlockSpec((1, rows, w_), lambda bi, i: (bi, i, 0))
    res = lambda dd, w_: pl.BlockSpec((1, dd, rows // dd, w_), lambda bi, i: (bi, 0, i, 0))
    const2 = lambda bi, i: (0, 0)
    flat = lambda bi, i: bi * steps + i
    return pl.pallas_call(
        _outproj_router_kernel,
        grid=(b, steps),
        in_specs=[
            row(256), row(512), row(256), row(512),
            res(4, 256), res(4, 512), res(16, 256), res(16, 512),
            _const_spec((TM, TM), const2), _const_spec((TM, TM), const2),
            _const_spec((None, d, d), lambda bi, i: (layer, 0, 0)),
            row(d),
            _mod_spec(mod, layer, 2),
            _mod_spec(mod, layer, 4),
            _mod_spec(mod, layer, 3),
            pl.BlockSpec((None, 1, d), lambda bi, i: (layer, 0, 0)),
            _const_spec((None, d, 2 * ROUTER_COLS), lambda bi, i: (layer, 0, 0)),
            pl.BlockSpec((None, 1, ROUTER_COLS), lambda bi, i: (layer, 0, 0)),
            _const_spec((TM, TM), const2),
        ],
        out_specs=[
            row(d),
            pl.BlockSpec((nblk * SLOTS, d), lambda bi, i: (flat(bi, i), 0)),
            pl.BlockSpec((nblk, 8, TM), lambda bi, i: (flat(bi, i), 0, 0)),
            pl.BlockSpec((nblk, N_EXPERTS, LANES), lambda bi, i: (flat(bi, i), 0, 0)),
        ],
        out_shape=[
            jax.ShapeDtypeStruct((b, s, d), F32),
            jax.ShapeDtypeStruct((nt * SLOTS, d), BF16),
            jax.ShapeDtypeStruct((nt, 8, TM), F32),
            jax.ShapeDtypeStruct((nt, N_EXPERTS, LANES), I32),
        ],
        scratch_shapes=[pltpu.VMEM((d, d), BF16)],
        compiler_params=pltpu.CompilerParams(vmem_limit_bytes=56 * 1024 * 1024),
        name="out_proj_router",
    )(oa, ob, o1, l1, o4, l4, o16, l16, p4t, p16t, w_out, x, mod, mod, mod, g, wr, br, tri)


ROUTER_COLS = LANES


def _router_body(x, sc_ref, sh_ref, g_ref, wr_ref, br_ref, tri_ref, hs_ref, meta_ref, nch_ref):
    tm = TM
    h = _modulated_norm(x, g_ref[...], _batch_row(sc_ref), _batch_row(sh_ref))
    hb = h.astype(BF16)
    h_lo = (h - hb.astype(F32)).astype(BF16)
    part = _dot(hb, wr_ref[...]) + _dot(h_lo, wr_ref[...])
    logits = part[:, :ROUTER_COLS] + part[:, ROUTER_COLS:] + br_ref[...]
    r8 = lax.broadcasted_iota(I32, (8, tm), 0)
    r16 = lax.broadcasted_iota(I32, (N_EXPERTS, tm), 0)
    rl = lax.broadcasted_iota(I32, (N_EXPERTS, LANES), 0)
    slot = lax.broadcasted_iota(I32, (SLOTS, tm), 0)
    for blk in range(x.shape[0] // tm):
        lt = logits[blk * tm:(blk + 1) * tm].T
        glog = lt[0:8]
        elog = lt[8:8 + N_EXPERTS]

        gmax = jnp.max(glog, axis=0, keepdims=True)
        g_w = 1.0 / jnp.sum(jnp.exp(glog - gmax), axis=0, keepdims=True)
        g_idx = jnp.min(jnp.where(glog == gmax, r8, 99), axis=0, keepdims=True)

        el = jnp.where((r16 // EXPERTS_PER_GROUP) == g_idx, elog, NEG_INF)
        emax = jnp.max(el, axis=0, keepdims=True)
        e1 = jnp.min(jnp.where(el == emax, r16, 99), axis=0, keepdims=True)
        el2 = jnp.where(r16 == e1, NEG_INF, el)
        emax2 = jnp.max(el2, axis=0, keepdims=True)
        e2 = jnp.min(jnp.where(el2 == emax2, r16, 99), axis=0, keepdims=True)
        p2 = jnp.exp(emax2 - emax)
        wt1 = g_w / (1.0 + p2)
        wt2 = g_w * p2 / (1.0 + p2)

        oh1 = r16 == e1
        oh2 = r16 == e2
        onehot = jnp.where(oh1, 1.0, 0.0) + jnp.where(oh2, 1.0, 0.0)
        cnt = jnp.sum(onehot, axis=1, keepdims=True)
        nch = jnp.floor((cnt + (CHUNK - 1)) * (1.0 / CHUNK))
        nchb = jnp.broadcast_to(nch, (N_EXPERTS, LANES))
        incl = nchb
        for sft in (1, 2, 4, 8):
            incl = incl + jnp.where(rl >= sft, pltpu.roll(incl, sft, 0), 0.0)
        off = (incl - nchb)[:, 0:1] * float(CHUNK)
        rank = _dot(onehot.astype(BF16), tri_ref[...])
        slot_of = off + rank
        pos1 = jnp.sum(jnp.where(oh1, slot_of, 0.0), axis=0, keepdims=True)
        pos2 = jnp.sum(jnp.where(oh2, slot_of, 0.0), axis=0, keepdims=True)

        sel = jnp.where(slot == pos1.astype(I32), 1.0, jnp.where(slot == pos2.astype(I32), 1.0, 0.0))
        hs_ref[blk * SLOTS:(blk + 1) * SLOTS, :] = _dot(sel.astype(BF16), hb[blk * tm:(blk + 1) * tm]).astype(BF16)

        meta_ref[blk] = jnp.concatenate([pos1, pos2, wt1, wt2, jnp.zeros((4, tm), F32)], axis=0)
        nch_ref[blk] = nchb.astype(I32)


def _ffn_schedule(nch, max_tiles, dump_base):
    nt = nch.shape[0]
    cend = jnp.cumsum(nch, axis=1)
    coff = cend - nch
    tcum = jnp.cumsum(nch, axis=0)
    before = tcum - nch
    tot = tcum[-1]
    pad = ((tot + FFN_CHUNKS - 1) // FFN_CHUNKS) * FFN_CHUNKS
    eend = jnp.cumsum(pad)
    estart = eend - pad
    n_tiles = (eend[-1] // FFN_CHUNKS).astype(I32)
    first_chunk = jnp.arange(max_tiles, dtype=I32) * FFN_CHUNKS
    tile_expert = jnp.sum((eend[None, :] <= first_chunk[:, None]).astype(I32), axis=1)
    tile_expert = jnp.minimum(tile_expert, N_EXPERTS - 1)
    hp = lax.Precision.HIGHEST
    pos = jnp.arange(max_tiles * FFN_CHUNKS, dtype=I32)
    e_s = jnp.minimum(jnp.sum((eend[None, :] <= (pos // FFN_CHUNKS * FFN_CHUNKS)[:, None]).astype(I32), axis=1),
                      N_EXPERTS - 1)
    oh_e = (e_s[:, None] == jnp.arange(N_EXPERTS, dtype=I32)[None, :]).astype(F32)
    idx = pos - jnp.dot(oh_e, estart.astype(F32), precision=hp).astype(I32)
    run_end = jnp.dot(oh_e, tcum.T.astype(F32), precision=hp).astype(I32)
    run_beg = jnp.dot(oh_e, before.T.astype(F32), precision=hp).astype(I32)
    run_off = jnp.dot(oh_e, coff.T.astype(F32), precision=hp).astype(I32)
    in_run = (idx[:, None] >= run_beg) & (idx[:, None] < run_end)
    tile_base = jnp.arange(nt, dtype=I32)[None, :] * CHUNKS_PER_TILE
    src = jnp.sum(jnp.where(in_run, tile_base + run_off + idx[:, None] - run_beg, 0), axis=1)
    real = jnp.any(in_run, axis=1)
    dump = dump_base + (pos // FFN_CHUNKS % 2) * FFN_CHUNKS + pos % FFN_CHUNKS
    src_rows = jnp.where(real, src, 0) * CHUNK
    dst_rows = jnp.where(real, src, dump) * CHUNK
    used = cend[:, -1]
    ucum = jnp.cumsum(CHUNKS_PER_TILE - used)
    ubeg = ucum - (CHUNKS_PER_TILE - used)
    z = jnp.arange(max_tiles * ZERO_CHUNKS, dtype=I32)[:, None]
    in_gap = (z >= ubeg[None, :]) & (z < ucum[None, :])
    zsrc = jnp.sum(jnp.where(in_gap, tile_base + used[None, :] + z - ubeg[None, :], 0), axis=1)
    zdump = dump_base + 2 * FFN_CHUNKS + z[:, 0] % ZERO_CHUNKS
    zero_rows = jnp.where(jnp.any(in_gap, axis=1), zsrc, zdump) * CHUNK
    return tile_expert, src_rows, dst_rows, n_tiles.reshape(1), zero_rows


def _ffn_kernel(te_ref, sr_ref, dr_ref, nt_ref, zr_ref, hs_hbm, wg_ref, wu_ref, wd_ref, ys_hbm,
                xbuf, ybuf, zbuf, wgb, wub, wdb, in_sem, out_sem, zero_sem, *, dump_base):
    j = pl.program_id(0)
    nt = nt_ref[0]
    half_ff = EXPERT_FF // 2

    def rows_at(r):
        return pl.ds(pl.multiple_of(r, CHUNK), CHUNK)

    def in_copy(step, slot, k, wait=False):
        r = 0 if wait else sr_ref[step * FFN_CHUNKS + k]
        return pltpu.make_async_copy(hs_hbm.at[rows_at(r), :], xbuf.at[slot, pl.ds(k * CHUNK, CHUNK), :],
                                     in_sem.at[slot])

    def out_copy(step, slot, k, wait=False):
        r = 0 if wait else dr_ref[step * FFN_CHUNKS + k]
        return pltpu.make_async_copy(ybuf.at[slot, pl.ds(k * CHUNK, CHUNK), :], ys_hbm.at[rows_at(r), :],
                                     out_sem.at[slot])

    def zero_copy(k, wait=False):
        r = 0 if wait else zr_ref[j * ZERO_CHUNKS + k]
        return pltpu.make_async_copy(zbuf.at[pl.ds(k * CHUNK, CHUNK), :], ys_hbm.at[rows_at(r), :], zero_sem)

    @pl.when(j == 0)
    def _():
        zbuf[...] = jnp.zeros_like(zbuf)
        fills = [pltpu.make_async_copy(
            zbuf, ys_hbm.at[pl.ds((dump_base + r * ZERO_CHUNKS) * CHUNK, ZERO_CHUNKS * CHUNK), :], zero_sem)
            for r in range(DUMP_CHUNKS // ZERO_CHUNKS)]
        for cp in fills:
            cp.start()
        for cp in fills:
            cp.wait()

    for k in range(ZERO_CHUNKS):
        zero_copy(k).start()

    @pl.when(j < nt)
    def _():
        slot = j % 2

        @pl.when(j == 0)
        def _():
            for k in range(FFN_CHUNKS):
                in_copy(0, 0, k).start()

        @pl.when(j + 1 < nt)
        def _():
            for k in range(FFN_CHUNKS):
                in_copy(j + 1, 1 - slot, k).start()

        @pl.when((j == 0) | (te_ref[j] != te_ref[jnp.maximum(j - 1, 0)]))
        def _():
            wgb[...] = wg_ref[0].astype(BF16)
            wub[...] = wu_ref[0].astype(BF16)
            wdb[...] = wd_ref[0].astype(BF16)

        for k in range(FFN_CHUNKS):
            in_copy(j, slot, k, wait=True).wait()

        @pl.when(j >= 2)
        def _():
            for k in range(FFN_CHUNKS):
                out_copy(j - 2, slot, k, wait=True).wait()

        x = xbuf[slot]
        hg = [_dot(x, wgb[:, h * half_ff:(h + 1) * half_ff]) for h in range(2)]
        hu = [_dot(x, wub[:, h * half_ff:(h + 1) * half_ff]) for h in range(2)]
        y = None
        for h in range(2):
            act = ((hg[h] / (1.0 + jnp.exp(-hg[h]))) * hu[h]).astype(BF16)
            part = _dot(act, wdb[h * half_ff:(h + 1) * half_ff, :])
            y = part if y is None else y + part
        ybuf[slot] = y.astype(BF16)
        for k in range(FFN_CHUNKS):
            out_copy(j, slot, k).start()

        @pl.when(j == nt - 1)
        def _():
            for k in range(FFN_CHUNKS):
                out_copy(j, slot, k, wait=True).wait()

            @pl.when(j >= 1)
            def _():
                for k in range(FFN_CHUNKS):
                    out_copy(j - 1, 1 - slot, k, wait=True).wait()

    for k in range(ZERO_CHUNKS):
        zero_copy(k, wait=True).wait()


def _ffn_call(tile_expert, src_rows, dst_rows, n_tiles, zero_rows, hs, wg, wu, wd, layer, max_tiles):
    rows, d = hs.shape
    ff = wg.shape[-1]
    wmap = lambda j, te, sr, dr, nt, zr: (layer, te[j], 0, 0)
    grid_spec = pltpu.PrefetchScalarGridSpec(
        num_scalar_prefetch=5,
        grid=(max_tiles,),
        in_specs=[
            pl.BlockSpec(memory_space=pl.ANY),
            pl.BlockSpec((None, 1, d, ff), wmap),
            pl.BlockSpec((None, 1, d, ff), wmap),
            pl.BlockSpec((None, 1, ff, d), wmap),
        ],
        out_specs=pl.BlockSpec(memory_space=pl.ANY),
        scratch_shapes=[
            pltpu.VMEM((2, FFN_ROWS, d), BF16),
            pltpu.VMEM((2, FFN_ROWS, d), BF16),
            pltpu.VMEM((ZERO_CHUNKS * CHUNK, d), BF16),
            pltpu.VMEM((d, ff), BF16),
            pltpu.VMEM((d, ff), BF16),
            pltpu.VMEM((ff, d), BF16),
            pltpu.SemaphoreType.DMA((2,)),
            pltpu.SemaphoreType.DMA((2,)),
            pltpu.SemaphoreType.DMA(()),
        ],
    )
    return pl.pallas_call(
        functools.partial(_ffn_kernel, dump_base=rows // CHUNK),
        grid_spec=grid_spec,
        out_shape=jax.ShapeDtypeStruct((rows + DUMP_CHUNKS * CHUNK, d), BF16),
        compiler_params=pltpu.CompilerParams(vmem_limit_bytes=48 * 1024 * 1024),
        name="expert_ffn",
    )(tile_expert, src_rows, dst_rows, n_tiles, zero_rows, hs, wg, wu, wd)


def _combine_body(ys_ref, meta_ref, x_ref, g2_ref):
    tm = TM
    eye = (lax.broadcasted_iota(I32, (tm, tm), 0) == lax.broadcasted_iota(I32, (tm, tm), 1))
    slot = lax.broadcasted_iota(I32, (tm, SLOTS), 1).astype(F32)

    def as_col(row):
        return jnp.sum(jnp.where(eye, row, 0.0), axis=1, keepdims=True)

    ys = []
    for blk in range(x_ref.shape[1] // tm):
        meta = meta_ref[blk]
        pos1, pos2 = as_col(meta[0:1]), as_col(meta[1:2])
        w1, w2 = as_col(meta[2:3]), as_col(meta[3:4])
        gate = jnp.where(slot == pos1, w1, 0.0) + jnp.where(slot == pos2, w2, 0.0)
        ys.append(_dot(gate.astype(BF16), ys_ref[blk * SLOTS:(blk + 1) * SLOTS, :]))
    return x_ref[0] + _batch_row(g2_ref) * jnp.concatenate(ys, axis=0)


def _combine_final_kernel(ys_ref, meta_ref, x_ref, g2_ref, gf_ref, xo_ref):
    xo = _combine_body(ys_ref, meta_ref, x_ref, g2_ref)
    xo_ref[0] = xo * lax.rsqrt(jnp.mean(xo * xo, axis=-1, keepdims=True) + NORM_EPS) * gf_ref[...]


def _combine_inproj_kernel(ys_ref, meta_ref, x_ref, g2_ref, *refs):
    n_in = 10
    xo_ref = refs[n_in]
    xo = _combine_body(ys_ref, meta_ref, x_ref, g2_ref)
    xo_ref[0] = xo
    _inproj_body(xo, *refs[:n_in], *refs[n_in + 1:])


def _combine_specs(d, steps, mod, layer):
    nblk = ROWS // TM
    flat = lambda bi, i: bi * steps + i
    return [
        pl.BlockSpec((nblk * SLOTS, d), lambda bi, i: (flat(bi, i), 0)),
        pl.BlockSpec((nblk, 8, TM), lambda bi, i: (flat(bi, i), 0, 0)),
        pl.BlockSpec((1, ROWS, d), lambda bi, i: (bi, i, 0)),
        _mod_spec(mod, layer, 5),
    ]


def _combine_final_call(ys, meta, x, mod, layer, gf):
    b, s, d = x.shape
    return pl.pallas_call(
        _combine_final_kernel,
        grid=(b, s // ROWS),
        in_specs=_combine_specs(d, s // ROWS, mod, layer) + [pl.BlockSpec((1, d), lambda bi, i: (0, 0))],
        out_specs=pl.BlockSpec((1, ROWS, d), lambda bi, i: (bi, i, 0)),
        out_shape=jax.ShapeDtypeStruct((b, s, d), F32),
        compiler_params=pltpu.CompilerParams(vmem_limit_bytes=48 * 1024 * 1024),
        name="moe_combine",
    )(ys, meta, x, mod, gf)


def _combine_inproj_call(ys, meta, x, mod, g, w_in, layer, tabs, p4, p16):
    b, s, d = x.shape
    in_specs, out_specs, out_shape = _inproj_specs(b, s, d, layer, mod)
    xspec = pl.BlockSpec((1, ROWS, d), lambda bi, i: (bi, i, 0))
    return pl.pallas_call(
        _combine_inproj_kernel,
        grid=(b, s // ROWS),
        in_specs=_combine_specs(d, s // ROWS, mod, layer - 1) + in_specs,
        out_specs=[xspec] + out_specs,
        out_shape=[jax.ShapeDtypeStruct((b, s, d), F32)] + out_shape,
        scratch_shapes=[pltpu.VMEM((d, IN_WIDTH), BF16)],
        compiler_params=pltpu.CompilerParams(vmem_limit_bytes=56 * 1024 * 1024),
        name="combine_in_proj",
    )(ys, meta, x, mod, mod, mod, g, w_in, *tabs, p4, p16)


def _rope_tables(positions):
    pos = positions.astype(F32)[..., None]

    def table(dim):
        inv = ROPE_THETA ** (-jnp.arange(0, dim, 2, dtype=F32) / dim)
        ang = pos * inv
        cos, sin = jnp.cos(ang), jnp.sin(ang)
        reps = LANES // dim
        return (jnp.tile(jnp.concatenate([cos, cos], -1), (1, 1, reps)),
                jnp.tile(jnp.concatenate([-sin, sin], -1), (1, 1, reps)))

    c64, s64 = table(HEAD_DIM)
    c32, s32 = table(DIFF_QK_DIM)
    return c64, s64, c32, s32


def kernel(x, c, positions, ada_w, ada_b, norm_mix_g, norm_ffn_g, w_in, w_out, diff_lambda_q1, diff_lambda_k1,
           diff_lambda_q2, diff_lambda_k2, diff_subln_g, swa_sinks, router_group_w, router_group_b,
           router_expert_w, router_expert_b, expert_w_gate, expert_w_up, expert_w_down, final_norm_g):
    b, s, d = x.shape
    depth = ada_w.shape[0]
    n = b * s
    nt = n // TM
    max_tiles = (nt * CHUNKS_PER_TILE + N_EXPERTS * (FFN_CHUNKS - 1)) // FFN_CHUNKS + 1

    tabs = _rope_tables(positions)
    p4 = _residue_perm(TM, 4)
    p16 = _residue_perm(TM, 16)
    p4_b, p16_b = jnp.asarray(p4, BF16), jnp.asarray(p16, BF16)
    p4t_b, p16t_b = jnp.asarray(p4.T, BF16), jnp.asarray(p16.T, BF16)
    tri = jnp.asarray(np.triu(np.ones((TM, TM), np.float32), 1), BF16)

    c_pad = jnp.pad(c, ((0, 8 - b), (0, 0)))
    mod = _ada_call(c_pad, ada_w, ada_b)

    sink_order = np.asarray(_SWA_HEAD_ORDER, np.int32)
    zpad = lambda k: jnp.zeros((depth, d, k), F32)
    wr = jnp.concatenate([router_group_w, zpad(8 - N_GROUPS), router_expert_w,
                          zpad(ROUTER_COLS - 8 - N_EXPERTS)], axis=-1)
    wr_hi = wr.astype(BF16)
    wr = jnp.concatenate([wr_hi, (wr - wr_hi.astype(F32)).astype(BF16)], axis=-1)
    br = jnp.concatenate([router_group_b, jnp.full((depth, 8 - N_GROUPS), NEG_INF, F32), router_expert_b,
                          jnp.zeros((depth, ROUTER_COLS - 8 - N_EXPERTS), F32)], axis=-1).reshape(depth, 1, -1)

    lam_init = [0.8 - 0.6 * math.exp(-0.3 * l) for l in range(depth)]
    lam = (jnp.exp(jnp.sum(diff_lambda_q1 * diff_lambda_k1, axis=-1))
           - jnp.exp(jnp.sum(diff_lambda_q2 * diff_lambda_k2, axis=-1)) + jnp.asarray(lam_init, F32))
    g_cols = jnp.broadcast_to(diff_subln_g[:, :, None], (depth, HEAD_DIM, TQ))
    sinks = swa_sinks[:, sink_order].reshape(-1)
    g_mix = norm_mix_g.reshape(depth, 1, d)
    g_ffn = norm_ffn_g.reshape(depth, 1, d)

    proj, vat, qkv4, qkv16 = _inproj_call(x, mod, g_mix, w_in, 0, tabs, p4_b, p16_b)
    for l in range(depth):
        oa = _diff_attn_call(proj, vat, lam, g_cols, lam_init[l], l)
        ob = _band_call(proj, COL_QB, COL_KB, COL_VB, nqb=4, nkb=1, max_dist=SWA_WINDOW - 1,
                        sinks=sinks, sink_base=l * SWA_Q_HEADS, name="swa")[0]
        o1, l1 = _band_call(proj, COL_QC, COL_KC, COL_VC, nqb=2, nkb=2, max_dist=BAND, want_lse=True,
                            name="dil1")
        o4, l4 = _band_call(qkv4.reshape(b * 4, s // 4, 768), 0, 256, 512, nqb=2, nkb=2, max_dist=BAND,
                            want_lse=True, name="dil4")
        o16, l16 = _band_call(qkv16.reshape(b * 16, s // 16, 768), 0, 256, 512, nqb=2, nkb=2, max_dist=BAND,
                              want_lse=True, name="dil16")
        x, hs, meta, nch = _outproj_router_call(
            oa, ob, o1, l1, o4.reshape(b, 4, s // 4, 256), l4.reshape(b, 4, s // 4, 512),
            o16.reshape(b, 16, s // 16, 256), l16.reshape(b, 16, s // 16, 512), p4t_b, p16t_b, w_out, l, x, mod,
            g_ffn, wr, br, tri)
        sched = _ffn_schedule(nch[:, :, 0], max_tiles, nt * CHUNKS_PER_TILE)
        ys = _ffn_call(*sched, hs, expert_w_gate, expert_w_up, expert_w_down, l, max_tiles)
        if l + 1 < depth:
            x, proj, vat, qkv4, qkv16 = _combine_inproj_call(ys, meta, x, mod, g_mix, w_in, l + 1, tabs,
                                                             p4_b, p16_b)
        else:
            x = _combine_final_call(ys, meta, x, mod, l, final_norm_g.reshape(1, d))
    return x
```

```python
import functools
import math

import numpy as np
import jax
import jax.numpy as jnp
from jax import lax
from jax.experimental import pallas as pl
from jax.experimental.pallas import tpu as pltpu

F32 = jnp.float32
BF16 = jnp.bfloat16
I32 = jnp.int32

HEAD_DIM = 64
ROPE_THETA = 10000.0
NORM_EPS = 1e-6
NEG_INF = -1e30
DIFF_HEADS = 4
DIFF_QK_DIM = 32
SWA_Q_HEADS = 8
SWA_KV_HEADS = 2
SWA_WINDOW = 128
DIL_PATTERNS = ((128, 1), (512, 4), (2048, 16))
N_GROUPS = 4
EXPERTS_PER_GROUP = 4
N_EXPERTS = 16
EXPERT_FF = 512
N_ADA = 6
IN_WIDTH = 2304

LANES = 128
BF16_ROWS = 16
BAND = 128

TM = 256
ROWS = 2 * TM
CHUNK = BF16_ROWS
SLOTS = 2 * TM + N_EXPERTS * CHUNK
CHUNKS_PER_TILE = SLOTS // CHUNK
FFN_ROWS = 1024
FFN_CHUNKS = FFN_ROWS // CHUNK
ZERO_CHUNKS = -(-N_EXPERTS * FFN_CHUNKS // CHUNKS_PER_TILE)
DUMP_CHUNKS = -(-(2 * FFN_CHUNKS + ZERO_CHUNKS) // ZERO_CHUNKS) * ZERO_CHUNKS
TQ = 512
BAND_ROWS = 2048

_SWA_HEAD_ORDER = (0, 4, 1, 5, 2, 6, 3, 7)
COL_QB, COL_KB, COL_VB = 0, 512, 640
COL_Q1, COL_Q2, COL_K1, COL_K2, COL_VA = 768, 896, 1024, 1152, 1280
COL_QC, COL_KC, COL_VC = 1536, 1792, 2048
_SRC_ROPE = (32, 32, 32, 32, 0, 0, 64, 64, 64, 64, 64, 0, 64, 64, 64, 64, 0, 0)
_SRC_DEST = (6, 7, 8, 9, 10, 11, None, None, None, None, 4, 5, 12, 13, 14, 15, 16, 17)
VT_ROWS = HEAD_DIM + BF16_ROWS
_SRC_VA_CHUNK = 2


def _residue_perm(tm, d):
    p = np.zeros((tm, tm), np.float32)
    per = tm // d
    for l in range(per):
        for r in range(d):
            p[r * per + l, l * d + r] = 1.0
    return p


def _dot(a, b, **kw):
    return jnp.dot(a, b, preferred_element_type=F32, **kw)


def _dot_nt(a, b):
    return lax.dot_general(a, b, (((1,), (1,)), ((), ())), preferred_element_type=F32)


def _batch_row(ref):
    return ref[pl.ds(pl.program_id(0), 1), :]


def _modulated_norm(x, g, sc, sh):
    y = x * lax.rsqrt(jnp.mean(x * x, axis=-1, keepdims=True) + NORM_EPS)
    return (y * g) * (1.0 + sc) + sh


def _ada_kernel(c_ref, w_ref, b_ref, o_ref):
    c = c_ref[...]
    ca = c / (1.0 + jnp.exp(-c))
    o_ref[0] = _dot(ca, w_ref[0], precision=lax.Precision.HIGHEST) + b_ref[0]


def _ada_call(c_pad, ada_w, ada_b):
    depth, d, n = ada_w.shape
    tn = 1536
    return pl.pallas_call(
        _ada_kernel,
        grid=(depth, n // tn),
        in_specs=[
            pl.BlockSpec((c_pad.shape[0], d), lambda l, j: (0, 0)),
            pl.BlockSpec((1, d, tn), lambda l, j: (l, 0, j)),
            pl.BlockSpec((1, 1, tn), lambda l, j: (l, 0, j)),
        ],
        out_specs=pl.BlockSpec((1, c_pad.shape[0], tn), lambda l, j: (l, 0, j)),
        out_shape=jax.ShapeDtypeStruct((depth, c_pad.shape[0], n), F32),
        compiler_params=pltpu.CompilerParams(vmem_limit_bytes=40 * 1024 * 1024),
        name="ada_mod",
    )(c_pad, ada_w, ada_b.reshape(depth, 1, n))


def _rope(t, cos, sin_signed, first_half, half):
    rot = jnp.where(first_half, pltpu.roll(t, LANES - half, 1), pltpu.roll(t, half, 1))
    return t * cos + rot * sin_signed


def _inproj_body(x, sc_ref, sh_ref, g_ref, w_ref, cs64_ref, sn64_ref, cs32_ref, sn32_ref,
                 p4_ref, p16_ref, proj_ref, vat_ref, c4_ref, c16_ref, wb):
    @pl.when((pl.program_id(0) == 0) & (pl.program_id(1) == 0))
    def _():
        wb[...] = w_ref[...].astype(BF16)

    h = _modulated_norm(x, g_ref[...], _batch_row(sc_ref), _batch_row(sh_ref))
    hb = h.astype(BF16)
    lane = lax.broadcasted_iota(I32, (1, LANES), 1)
    first64 = (lane % 64) < 32
    first32 = (lane % 32) < 16
    lo_half = lane < 64
    swa_q = []
    for cb in range(IN_WIDTH // 256):
        acc = _dot(hb, wb[:, cb * 256:(cb + 1) * 256])
        if cb == _SRC_VA_CHUNK:
            acc_t = acc.T.astype(BF16)
            for hd in range(DIFF_HEADS):
                vat_ref[0, hd * VT_ROWS:hd * VT_ROWS + HEAD_DIM, :] = acc_t[hd * HEAD_DIM:(hd + 1) * HEAD_DIM]
                vat_ref[0, hd * VT_ROWS + HEAD_DIM:(hd + 1) * VT_ROWS, :] = jnp.ones(
                    (BF16_ROWS, acc_t.shape[1]), BF16)
        for half in range(2):
            src = cb * 2 + half
            t = acc[:, half * LANES:(half + 1) * LANES]
            if _SRC_ROPE[src] == 64:
                t = _rope(t, cs64_ref[0], sn64_ref[0], first64, 32)
            elif _SRC_ROPE[src] == 32:
                t = _rope(t, cs32_ref[0], sn32_ref[0], first32, 16)
            dst = _SRC_DEST[src]
            if dst is None:
                swa_q.append(t)
            else:
                proj_ref[0, :, dst * LANES:(dst + 1) * LANES] = t.astype(BF16)
    for jb in range(SWA_Q_HEADS // 2):
        a, c = swa_q[jb // 2], swa_q[2 + jb // 2]
        if jb % 2 == 0:
            blk = jnp.where(lo_half, a, pltpu.roll(c, 64, 1))
        else:
            blk = jnp.where(lo_half, pltpu.roll(a, 64, 1), c)
        proj_ref[0, :, jb * LANES:(jb + 1) * LANES] = blk.astype(BF16)
    tm = TM
    for blk in range(x.shape[0] // tm):
        cc = proj_ref[0, blk * tm:(blk + 1) * tm, COL_QC:]
        c4 = _dot(p4_ref[...], cc).astype(BF16)
        for r in range(4):
            c4_ref[0, r, blk * (tm // 4):(blk + 1) * (tm // 4)] = c4[r * (tm // 4):(r + 1) * (tm // 4)]
        c16 = _dot(p16_ref[...], cc).astype(BF16)
        for r in range(16):
            c16_ref[0, r, blk * (tm // 16):(blk + 1) * (tm // 16)] = c16[r * (tm // 16):(r + 1) * (tm // 16)]


def _inproj_kernel(x_ref, *refs):
    _inproj_body(x_ref[0], *refs)


def _mod_spec(mod, layer, k):
    return pl.BlockSpec((None, mod.shape[1], mod.shape[2] // N_ADA), lambda bi, i: (layer, 0, k))


def _const_spec(shape, index_map):
    return pl.BlockSpec(shape, index_map, pipeline_mode=pl.Buffered(1))


def _inproj_specs(b, s, d, layer, mod):
    rows = ROWS
    row = lambda bi, i: (bi, i, 0)
    const2 = lambda bi, i: (0, 0)
    in_specs = [
        _mod_spec(mod, layer, 1),
        _mod_spec(mod, layer, 0),
        pl.BlockSpec((None, 1, d), lambda bi, i: (layer, 0, 0)),
        _const_spec((None, d, IN_WIDTH), lambda bi, i: (layer, 0, 0)),
        pl.BlockSpec((1, rows, LANES), row),
        pl.BlockSpec((1, rows, LANES), row),
        pl.BlockSpec((1, rows, LANES), row),
        pl.BlockSpec((1, rows, LANES), row),
        _const_spec((TM, TM), const2),
        _const_spec((TM, TM), const2),
    ]
    out_specs = [
        pl.BlockSpec((1, rows, IN_WIDTH), row),
        pl.BlockSpec((1, DIFF_HEADS * VT_ROWS, rows), lambda bi, i: (bi, 0, i)),
        pl.BlockSpec((1, 4, rows // 4, 768), lambda bi, i: (bi, 0, i, 0)),
        pl.BlockSpec((1, 16, rows // 16, 768), lambda bi, i: (bi, 0, i, 0)),
    ]
    out_shape = [
        jax.ShapeDtypeStruct((b, s, IN_WIDTH), BF16),
        jax.ShapeDtypeStruct((b, DIFF_HEADS * VT_ROWS, s), BF16),
        jax.ShapeDtypeStruct((b, 4, s // 4, 768), BF16),
        jax.ShapeDtypeStruct((b, 16, s // 16, 768), BF16),
    ]
    return in_specs, out_specs, out_shape


def _inproj_call(x, mod, g, w_in, layer, tabs, p4, p16):
    b, s, d = x.shape
    in_specs, out_specs, out_shape = _inproj_specs(b, s, d, layer, mod)
    return pl.pallas_call(
        _inproj_kernel,
        grid=(b, s // ROWS),
        in_specs=[pl.BlockSpec((1, ROWS, d), lambda bi, i: (bi, i, 0))] + in_specs,
        out_specs=out_specs,
        out_shape=out_shape,
        scratch_shapes=[pltpu.VMEM((d, IN_WIDTH), BF16)],
        compiler_params=pltpu.CompilerParams(vmem_limit_bytes=56 * 1024 * 1024),
        name="in_proj",
    )(x, mod, mod, g, w_in, *tabs, p4, p16)


def _diff_attn_kernel(lam_ref, q1_ref, q2_ref, q1n_ref, q2n_ref, k1_ref, k2_ref, vt_ref, g_ref, o_ref,
                      m_sc, acc_sc, s_sc, *, lambda_init, layer):
    tq = q1_ref.shape[1]
    qi = pl.program_id(1)
    lam = lam_ref[layer]
    to_log2 = DIFF_QK_DIM ** -0.5 * math.log2(math.e)
    lane = lax.broadcasted_iota(I32, (1, LANES), 1)

    def head_queries(qa_ref, qb_ref):
        qa = qa_ref[0].astype(F32) * to_log2
        qb = qb_ref[0].astype(F32) * to_log2
        out = []
        for h in range(DIFF_HEADS):
            hm = (lane // DIFF_QK_DIM) == h
            out.append((jnp.where(hm, qa, 0.0).astype(BF16), jnp.where(hm, qb, 0.0).astype(BF16)))
        return out

    qh = head_queries(q1_ref, q2_ref)
    qh_next = head_queries(q1n_ref, q2n_ref)
    causal = (lax.broadcasted_iota(I32, (tq, tq), 0) <= lax.broadcasted_iota(I32, (tq, tq), 1))

    m_sc[...] = jnp.full(m_sc.shape, NEG_INF, F32)
    acc_sc[...] = jnp.zeros(acc_sc.shape, F32)

    n_chain = 2 * DIFF_HEADS

    def scores(ch, tile, queries):
        start = pl.multiple_of(tile * tq, tq)
        k_ref = k1_ref if ch % 2 == 0 else k2_ref
        return _dot_nt(k_ref[0, pl.ds(start, tq), :], queries[ch // 2][ch % 2])

    @pl.when(qi == 0)
    def _():
        for ch in range(n_chain):
            s_sc[ch] = scores(ch, 0, qh)

    def step(j, last):
        start = pl.multiple_of(j * tq, tq)
        for ch in range(n_chain):
            st = s_sc[ch]
            s_sc[ch] = scores(ch, 0, qh_next) if last else scores(ch, j + 1, qh)
            h = ch // 2
            vt = vt_ref[0, h * VT_ROWS:(h + 1) * VT_ROWS, pl.ds(start, tq)]
            if last:
                st = jnp.where(causal, st, NEG_INF)
            m_old = m_sc[ch]
            m_new = jnp.maximum(m_old, jnp.max(st, axis=0, keepdims=True))
            p = jnp.exp2(st - m_new)
            al = jnp.exp2(m_old - m_new)
            acc_sc[ch] = al * acc_sc[ch] + _dot(vt, p.astype(BF16))
            m_sc[ch] = m_new

    def body(j, carry):
        step(j, False)
        return carry

    lax.fori_loop(0, qi, body, 0)
    step(qi, True)

    g = g_ref[...]
    outs = []
    for h in range(DIFF_HEADS):
        a1, a2 = acc_sc[2 * h], acc_sc[2 * h + 1]
        o = (a1[:HEAD_DIM] / a1[HEAD_DIM:HEAD_DIM + 1]
             - lam * (a2[:HEAD_DIM] / a2[HEAD_DIM:HEAD_DIM + 1]))
        ms = jnp.mean(o * o, axis=0, keepdims=True)
        outs.append((o * lax.rsqrt(ms + NORM_EPS)) * g * (1.0 - lambda_init))
    o_ref[0] = jnp.concatenate(outs, axis=0).T.astype(BF16)


def _diff_attn_call(proj, vat, lam, g_cols, lambda_init, layer):
    b, s, _ = proj.shape
    tq = TQ
    qspec = lambda cb: pl.BlockSpec((1, tq, LANES), lambda bi, i, cb=cb: (bi, i, cb))
    last_q = s // tq - 1
    qnext = lambda cb: pl.BlockSpec((1, tq, LANES), lambda bi, i, cb=cb: (bi, jnp.minimum(i + 1, last_q), cb))
    kspec = lambda cb: pl.BlockSpec((1, s, LANES), lambda bi, i, cb=cb: (bi, 0, cb))
    n_chain = 2 * DIFF_HEADS
    return pl.pallas_call(
        functools.partial(_diff_attn_kernel, lambda_init=lambda_init, layer=layer),
        grid=(b, s // tq),
        in_specs=[
            pl.BlockSpec(memory_space=pltpu.SMEM),
            qspec(COL_Q1 // LANES), qspec(COL_Q2 // LANES),
            qnext(COL_Q1 // LANES), qnext(COL_Q2 // LANES),
            kspec(COL_K1 // LANES), kspec(COL_K2 // LANES),
            pl.BlockSpec((1, DIFF_HEADS * VT_ROWS, s), lambda bi, i: (bi, 0, 0)),
            pl.BlockSpec((None, HEAD_DIM, tq), lambda bi, i: (layer, 0, 0)),
        ],
        out_specs=pl.BlockSpec((1, tq, 256), lambda bi, i: (bi, i, 0)),
        out_shape=jax.ShapeDtypeStruct((b, s, 256), BF16),
        scratch_shapes=[
            pltpu.VMEM((n_chain, 1, tq), F32),
            pltpu.VMEM((n_chain, VT_ROWS, tq), F32),
            pltpu.VMEM((n_chain, tq, tq), F32),
        ],
        compiler_params=pltpu.CompilerParams(vmem_limit_bytes=48 * 1024 * 1024),
        name="diff_attn",
    )(lam, proj, proj, proj, proj, proj, proj, vat, g_cols)


def _band_kernel(*refs, nqb, nkb, max_dist, has_sink, want_lse, sink_base):
    it = iter(refs)
    sink_ref = next(it) if has_sink else None
    q_ref, kp_ref, kc_ref, vp_ref, vc_ref = (next(it) for _ in range(5))
    o_ref = next(it)
    lse_ref = next(it) if want_lse else None
    kbuf, vbuf = next(it), next(it)
    nseq, rows = q_ref.shape[0], q_ref.shape[1]
    i = pl.program_id(1)
    for sq in range(nseq):
        kbuf[sq, 0:BAND, :] = kp_ref[sq]
        kbuf[sq, BAND:, :] = kc_ref[sq]
        for kb in range(nkb):
            vbuf[sq, 0:BAND, kb * 256:kb * 256 + LANES] = vp_ref[sq, :, kb * LANES:(kb + 1) * LANES]
            vbuf[sq, BAND:, kb * 256:kb * 256 + LANES] = vc_ref[sq, :, kb * LANES:(kb + 1) * LANES]
            vbuf[sq, :, kb * 256 + LANES:(kb + 1) * 256] = jnp.ones((BAND + rows, LANES), BF16)
    lane = lax.broadcasted_iota(I32, (1, LANES), 1)
    lo_half = lane < 64
    r_io = lax.broadcasted_iota(I32, (BAND, 2 * BAND), 0)
    c_io = lax.broadcasted_iota(I32, (BAND, 2 * BAND), 1)
    dist = BAND + r_io - c_io
    band = (dist >= 0) & (dist <= max_dist)
    band_first = band & ((c_io >= BAND) | (i > 0))
    col0 = lax.broadcasted_iota(I32, (1, 2 * BAND), 1) == 0
    vr = lax.broadcasted_iota(I32, (2 * BAND, 2 * LANES), 0)
    vc = lax.broadcasted_iota(I32, (2 * BAND, 2 * LANES), 1)
    sink_row = (vr == 0) & (vc < LANES)
    to_log2 = HEAD_DIM ** -0.5 * math.log2(math.e)
    units = [(sq, sb, qb) for sq in range(nseq) for sb in range(rows // BAND) for qb in range(nqb)]

    def scores(u):
        sq, sb, qb = units[u]
        kb = qb if nkb > 1 else 0
        q = q_ref[sq, sb * BAND:(sb + 1) * BAND, qb * LANES:(qb + 1) * LANES].astype(F32) * to_log2
        q2 = jnp.concatenate([jnp.where(lo_half, q, 0.0), jnp.where(lo_half, 0.0, q)], axis=0).astype(BF16)
        return _dot_nt(q2, kbuf[sq, sb * BAND:(sb + 2) * BAND, kb * LANES:(kb + 1) * LANES])

    ahead = 2
    pending = [scores(u) for u in range(min(ahead, len(units)))]
    for u, (sq, sb, qb) in enumerate(units):
        if u + ahead < len(units):
            pending.append(scores(u + ahead))
        s2 = pending[u]
        pending[u] = None
        kb = qb if nkb > 1 else 0
        r0 = sb * BAND
        msk = band_first if sb == 0 else band
        halves = []
        for hh in range(2):
            if has_sink:
                fill = jnp.where(col0, sink_ref[sink_base + qb * 2 + hh] * math.log2(math.e), NEG_INF)
            else:
                fill = NEG_INF
            halves.append(jnp.where(msk, s2[hh * BAND:(hh + 1) * BAND], fill))
        s2 = jnp.concatenate(halves, axis=0)
        m = jnp.max(s2, axis=1, keepdims=True)
        p = jnp.exp2(s2 - m).astype(BF16)
        vw = vbuf[sq, r0:r0 + 2 * BAND, kb * 256:(kb + 1) * 256]
        if has_sink:
            vw = jnp.where(sink_row, jnp.zeros_like(vw), vw)
        pv = _dot(p, vw)
        den = pv[:, LANES:]
        out = pv[:, :LANES] / den
        o = jnp.where(lo_half, out[:BAND], out[BAND:])
        o_ref[sq, r0:r0 + BAND, qb * LANES:(qb + 1) * LANES] = o.astype(BF16)
        if want_lse:
            lse2 = m + jnp.log2(den)
            ls = jnp.where(lo_half, lse2[:BAND], lse2[BAND:])
            hi = ls.astype(BF16)
            lo = (ls - hi.astype(F32)).astype(BF16)
            lse_ref[sq, r0:r0 + BAND, qb * LANES:(qb + 1) * LANES] = hi
            lse_ref[sq, r0:r0 + BAND, (nqb + qb) * LANES:(nqb + qb + 1) * LANES] = lo


def _band_call(arr, q_col, k_col, v_col, nqb, nkb, max_dist, sinks=None, sink_base=0, want_lse=False,
               name="band"):
    ns, length, _ = arr.shape
    rows = min(BAND_ROWS, length)
    nseq = BAND_ROWS // rows
    wq, wk = nqb * LANES, nkb * LANES
    rpb = rows // BAND
    cur = lambda col, w: pl.BlockSpec((nseq, rows, w), lambda n, i, c=col // w: (n, i, c))
    prev = lambda col, w: pl.BlockSpec(
        (nseq, BAND, w), lambda n, i, c=col // w: (n, jnp.maximum(i * rpb - 1, 0), c))
    in_specs = [cur(q_col, wq), prev(k_col, wk), cur(k_col, wk), prev(v_col, wk), cur(v_col, wk)]
    args = [arr] * 5
    if sinks is not None:
        in_specs = [pl.BlockSpec(memory_space=pltpu.SMEM)] + in_specs
        args = [sinks] + args
    out_specs = [pl.BlockSpec((nseq, rows, wq), lambda n, i: (n, i, 0))]
    out_shape = [jax.ShapeDtypeStruct((ns, length, wq), BF16)]
    if want_lse:
        out_specs.append(pl.BlockSpec((nseq, rows, 2 * wq), lambda n, i: (n, i, 0)))
        out_shape.append(jax.ShapeDtypeStruct((ns, length, 2 * wq), BF16))
    return pl.pallas_call(
        functools.partial(_band_kernel, nqb=nqb, nkb=nkb, max_dist=max_dist,
                          has_sink=sinks is not None, want_lse=want_lse, sink_base=sink_base),
        grid=(ns // nseq, length // rows),
        in_specs=in_specs,
        out_specs=out_specs,
        out_shape=out_shape,
        scratch_shapes=[pltpu.VMEM((nseq, BAND + rows, wk), BF16), pltpu.VMEM((nseq, BAND + rows, 2 * wk), BF16)],
        name=name,
    )(*args)


def _outproj_router_kernel(oa_ref, ob_ref, o1_ref, l1_ref, o4_ref, l4_ref, o16_ref, l16_ref, p4t_ref, p16t_ref,
                           w_ref, x_ref, g1_ref, sc_ref, sh_ref, g_ref, wr_ref, br_ref, tri_ref,
                           xo_ref, hs_ref, meta_ref, nch_ref, wb):
    @pl.when((pl.program_id(0) == 0) & (pl.program_id(1) == 0))
    def _():
        wb[0:256, :] = w_ref[0:256, :].astype(BF16)
        for pos, head in enumerate(_SWA_HEAD_ORDER):
            wb[256 + pos * HEAD_DIM:256 + (pos + 1) * HEAD_DIM, :] = (
                w_ref[256 + head * HEAD_DIM:256 + (head + 1) * HEAD_DIM, :].astype(BF16))
        wb[768:1024, :] = w_ref[768:1024, :].astype(BF16)

    hw = o1_ref.shape[2]
    nblk = x_ref.shape[1] // TM

    def lse_of(v):
        return v[:, :hw] + v[:, hw:]

    def unpermute(pt_ref, src_ref, width):
        per = src_ref.shape[2] // nblk
        return jnp.concatenate(
            [_dot(pt_ref[...], src_ref[0, :, k * per:(k + 1) * per, :].reshape(TM, width)) for k in range(nblk)],
            axis=0)

    o1 = o1_ref[0].astype(F32)
    ls1 = lse_of(l1_ref[0].astype(F32))
    o4 = unpermute(p4t_ref, o4_ref, hw)
    ls4 = lse_of(unpermute(p4t_ref, l4_ref, 2 * hw))
    o16 = unpermute(p16t_ref, o16_ref, hw)
    ls16 = lse_of(unpermute(p16t_ref, l16_ref, 2 * hw))
    mx = jnp.maximum(jnp.maximum(ls1, ls4), ls16)
    e1, e4, e16 = jnp.exp2(ls1 - mx), jnp.exp2(ls4 - mx), jnp.exp2(ls16 - mx)
    oc = (e1 * o1 + e4 * o4 + e16 * o16) / (e1 + e4 + e16)
    mix = (_dot(oa_ref[0], wb[0:256, :]) + _dot(ob_ref[0], wb[256:768, :])
           + _dot(oc.astype(BF16), wb[768:1024, :]))
    x1 = x_ref[0] + _batch_row(g1_ref) * mix
    xo_ref[0] = x1
    _router_body(x1, sc_ref, sh_ref, g_ref, wr_ref, br_ref, tri_ref, hs_ref, meta_ref, nch_ref)


def _outproj_router_call(oa, ob, o1, l1, o4, l4, o16, l16, p4t, p16t, w_out, layer, x, mod, g, wr, br, tri):
    b, s, d = x.shape
    rows = ROWS
    nblk = rows // TM
    steps = s // rows
    nt = b * s // TM
    row = lambda w_: pl.BlockSpec((1, rows, w_), lambda bi, i: (bi, i, 0))
    res = lambda dd, w_: pl.BlockSpec((1, dd, rows // dd, w_), lambda bi, i: (bi, 0, i, 0))
    const2 = lambda bi, i: (0, 0)
    flat = lambda bi, i: bi * steps + i
    return pl.pallas_call(
        _outproj_router_kernel,
        grid=(b, steps),
        in_specs=[
            row(256), row(512), row(256), row(512),
            res(4, 256), res(4, 512), res(16, 256), res(16, 512),
            _const_spec((TM, TM), const2), _const_spec((TM, TM), const2),
            _const_spec((None, d, d), lambda bi, i: (layer, 0, 0)),
            row(d),
            _mod_spec(mod, layer, 2),
            _mod_spec(mod, layer, 4),
            _mod_spec(mod, layer, 3),
            pl.BlockSpec((None, 1, d), lambda bi, i: (layer, 0, 0)),
            _const_spec((None, d, 2 * ROUTER_COLS), lambda bi, i: (layer, 0, 0)),
            pl.BlockSpec((None, 1, ROUTER_COLS), lambda bi, i: (layer, 0, 0)),
            _const_spec((TM, TM), const2),
        ],
        out_specs=[
            row(d),
            pl.BlockSpec((nblk * SLOTS, d), lambda bi, i: (flat(bi, i), 0)),
            pl.BlockSpec((nblk, 8, TM), lambda bi, i: (flat(bi, i), 0, 0)),
            pl.BlockSpec((nblk, N_EXPERTS, LANES), lambda bi, i: (flat(bi, i), 0, 0)),
        ],
        out_shape=[
            jax.ShapeDtypeStruct((b, s, d), F32),
            jax.ShapeDtypeStruct((nt * SLOTS, d), BF16),
            jax.ShapeDtypeStruct((nt, 8, TM), F32),
            jax.ShapeDtypeStruct((nt, N_EXPERTS, LANES), I32),
        ],
        scratch_shapes=[pltpu.VMEM((d, d), BF16)],
        compiler_params=pltpu.CompilerParams(vmem_limit_bytes=56 * 1024 * 1024),
        name="out_proj_router",
    )(oa, ob, o1, l1, o4, l4, o16, l16, p4t, p16t, w_out, x, mod, mod, mod, g, wr, br, tri)


ROUTER_COLS = LANES


def _router_body(x, sc_ref, sh_ref, g_ref, wr_ref, br_ref, tri_ref, hs_ref, meta_ref, nch_ref):
    tm = TM
    h = _modulated_norm(x, g_ref[...], _batch_row(sc_ref), _batch_row(sh_ref))
    hb = h.astype(BF16)
    h_lo = (h - hb.astype(F32)).astype(BF16)
    part = _dot(hb, wr_ref[...]) + _dot(h_lo, wr_ref[...])
    logits = part[:, :ROUTER_COLS] + part[:, ROUTER_COLS:] + br_ref[...]
    r8 = lax.broadcasted_iota(I32, (8, tm), 0)
    r16 = lax.broadcasted_iota(I32, (N_EXPERTS, tm), 0)
    rl = lax.broadcasted_iota(I32, (N_EXPERTS, LANES), 0)
    slot = lax.broadcasted_iota(I32, (SLOTS, tm), 0)
    for blk in range(x.shape[0] // tm):
        lt = logits[blk * tm:(blk + 1) * tm].T
        glog = lt[0:8]
        elog = lt[8:8 + N_EXPERTS]

        gmax = jnp.max(glog, axis=0, keepdims=True)
        g_w = 1.0 / jnp.sum(jnp.exp(glog - gmax), axis=0, keepdims=True)
        g_idx = jnp.min(jnp.where(glog == gmax, r8, 99), axis=0, keepdims=True)

        el = jnp.where((r16 // EXPERTS_PER_GROUP) == g_idx, elog, NEG_INF)
        emax = jnp.max(el, axis=0, keepdims=True)
        e1 = jnp.min(jnp.where(el == emax, r16, 99), axis=0, keepdims=True)
        el2 = jnp.where(r16 == e1, NEG_INF, el)
        emax2 = jnp.max(el2, axis=0, keepdims=True)
        e2 = jnp.min(jnp.where(el2 == emax2, r16, 99), axis=0, keepdims=True)
        p2 = jnp.exp(emax2 - emax)
        wt1 = g_w / (1.0 + p2)
        wt2 = g_w * p2 / (1.0 + p2)

        oh1 = r16 == e1
        oh2 = r16 == e2
        onehot = jnp.where(oh1, 1.0, 0.0) + jnp.where(oh2, 1.0, 0.0)
        cnt = jnp.sum(onehot, axis=1, keepdims=True)
        nch = jnp.floor((cnt + (CHUNK - 1)) * (1.0 / CHUNK))
        nchb = jnp.broadcast_to(nch, (N_EXPERTS, LANES))
        incl = nchb
        for sft in (1, 2, 4, 8):
            incl = incl + jnp.where(rl >= sft, pltpu.roll(incl, sft, 0), 0.0)
        off = (incl - nchb)[:, 0:1] * float(CHUNK)
        rank = _dot(onehot.astype(BF16), tri_ref[...])
        slot_of = off + rank
        pos1 = jnp.sum(jnp.where(oh1, slot_of, 0.0), axis=0, keepdims=True)
        pos2 = jnp.sum(jnp.where(oh2, slot_of, 0.0), axis=0, keepdims=True)

        sel = jnp.where(slot == pos1.astype(I32), 1.0, jnp.where(slot == pos2.astype(I32), 1.0, 0.0))
        hs_ref[blk * SLOTS:(blk + 1) * SLOTS, :] = _dot(sel.astype(BF16), hb[blk * tm:(blk + 1) * tm]).astype(BF16)

        meta_ref[blk] = jnp.concatenate([pos1, pos2, wt1, wt2, jnp.zeros((4, tm), F32)], axis=0)
        nch_ref[blk] = nchb.astype(I32)


def _ffn_schedule(nch, max_tiles, dump_base):
    nt = nch.shape[0]
    cend = jnp.cumsum(nch, axis=1)
    coff = cend - nch
    tcum = jnp.cumsum(nch, axis=0)
    before = tcum - nch
    tot = tcum[-1]
    pad = ((tot + FFN_CHUNKS - 1) // FFN_CHUNKS) * FFN_CHUNKS
    eend = jnp.cumsum(pad)
    estart = eend - pad
    n_tiles = (eend[-1] // FFN_CHUNKS).astype(I32)
    first_chunk = jnp.arange(max_tiles, dtype=I32) * FFN_CHUNKS
    tile_expert = jnp.sum((eend[None, :] <= first_chunk[:, None]).astype(I32), axis=1)
    tile_expert = jnp.minimum(tile_expert, N_EXPERTS - 1)
    hp = lax.Precision.HIGHEST
    pos = jnp.arange(max_tiles * FFN_CHUNKS, dtype=I32)
    e_s = jnp.minimum(jnp.sum((eend[None, :] <= (pos // FFN_CHUNKS * FFN_CHUNKS)[:, None]).astype(I32), axis=1),
                      N_EXPERTS - 1)
    oh_e = (e_s[:, None] == jnp.arange(N_EXPERTS, dtype=I32)[None, :]).astype(F32)
    idx = pos - jnp.dot(oh_e, estart.astype(F32), precision=hp).astype(I32)
    run_end = jnp.dot(oh_e, tcum.T.astype(F32), precision=hp).astype(I32)
    run_beg = jnp.dot(oh_e, before.T.astype(F32), precision=hp).astype(I32)
    run_off = jnp.dot(oh_e, coff.T.astype(F32), precision=hp).astype(I32)
    in_run = (idx[:, None] >= run_beg) & (idx[:, None] < run_end)
    tile_base = jnp.arange(nt, dtype=I32)[None, :] * CHUNKS_PER_TILE
    src = jnp.sum(jnp.where(in_run, tile_base + run_off + idx[:, None] - run_beg, 0), axis=1)
    real = jnp.any(in_run, axis=1)
    dump = dump_base + (pos // FFN_CHUNKS % 2) * FFN_CHUNKS + pos % FFN_CHUNKS
    src_rows = jnp.where(real, src, 0) * CHUNK
    dst_rows = jnp.where(real, src, dump) * CHUNK
    used = cend[:, -1]
    ucum = jnp.cumsum(CHUNKS_PER_TILE - used)
    ubeg = ucum - (CHUNKS_PER_TILE - used)
    z = jnp.arange(max_tiles * ZERO_CHUNKS, dtype=I32)[:, None]
    in_gap = (z >= ubeg[None, :]) & (z < ucum[None, :])
    zsrc = jnp.sum(jnp.where(in_gap, tile_base + used[None, :] + z - ubeg[None, :], 0), axis=1)
    zdump = dump_base + 2 * FFN_CHUNKS + z[:, 0] % ZERO_CHUNKS
    zero_rows = jnp.where(jnp.any(in_gap, axis=1), zsrc, zdump) * CHUNK
    return tile_expert, src_rows, dst_rows, n_tiles.reshape(1), zero_rows


def _ffn_kernel(te_ref, sr_ref, dr_ref, nt_ref, zr_ref, hs_hbm, wg_ref, wu_ref, wd_ref, ys_hbm,
                xbuf, ybuf, zbuf, wgb, wub, wdb, in_sem, out_sem, zero_sem, *, dump_base):
    j = pl.program_id(0)
    nt = nt_ref[0]
    half_ff = EXPERT_FF // 2

    def rows_at(r):
        return pl.ds(pl.multiple_of(r, CHUNK), CHUNK)

    def in_copy(step, slot, k, wait=False):
        r = 0 if wait else sr_ref[step * FFN_CHUNKS + k]
        return pltpu.make_async_copy(hs_hbm.at[rows_at(r), :], xbuf.at[slot, pl.ds(k * CHUNK, CHUNK), :],
                                     in_sem.at[slot])

    def out_copy(step, slot, k, wait=False):
        r = 0 if wait else dr_ref[step * FFN_CHUNKS + k]
        return pltpu.make_async_copy(ybuf.at[slot, pl.ds(k * CHUNK, CHUNK), :], ys_hbm.at[rows_at(r), :],
                                     out_sem.at[slot])

    def zero_copy(k, wait=False):
        r = 0 if wait else zr_ref[j * ZERO_CHUNKS + k]
        return pltpu.make_async_copy(zbuf.at[pl.ds(k * CHUNK, CHUNK), :], ys_hbm.at[rows_at(r), :], zero_sem)

    @pl.when(j == 0)
    def _():
        zbuf[...] = jnp.zeros_like(zbuf)
        fills = [pltpu.make_async_copy(
            zbuf, ys_hbm.at[pl.ds((dump_base + r * ZERO_CHUNKS) * CHUNK, ZERO_CHUNKS * CHUNK), :], zero_sem)
            for r in range(DUMP_CHUNKS // ZERO_CHUNKS)]
        for cp in fills:
            cp.start()
        for cp in fills:
            cp.wait()

    for k in range(ZERO_CHUNKS):
        zero_copy(k).start()

    @pl.when(j < nt)
    def _():
        slot = j % 2

        @pl.when(j == 0)
        def _():
            for k in range(FFN_CHUNKS):
                in_copy(0, 0, k).start()

        @pl.when(j + 1 < nt)
        def _():
            for k in range(FFN_CHUNKS):
                in_copy(j + 1, 1 - slot, k).start()

        @pl.when((j == 0) | (te_ref[j] != te_ref[jnp.maximum(j - 1, 0)]))
        def _():
            wgb[...] = wg_ref[0].astype(BF16)
            wub[...] = wu_ref[0].astype(BF16)
            wdb[...] = wd_ref[0].astype(BF16)

        for k in range(FFN_CHUNKS):
            in_copy(j, slot, k, wait=True).wait()

        @pl.when(j >= 2)
        def _():
            for k in range(FFN_CHUNKS):
                out_copy(j - 2, slot, k, wait=True).wait()

        x = xbuf[slot]
        hg = [_dot(x, wgb[:, h * half_ff:(h + 1) * half_ff]) for h in range(2)]
        hu = [_dot(x, wub[:, h * half_ff:(h + 1) * half_ff]) for h in range(2)]
        y = None
        for h in range(2):
            act = ((hg[h] / (1.0 + jnp.exp(-hg[h]))) * hu[h]).astype(BF16)
            part = _dot(act, wdb[h * half_ff:(h + 1) * half_ff, :])
            y = part if y is None else y + part
        ybuf[slot] = y.astype(BF16)
        for k in range(FFN_CHUNKS):
            out_copy(j, slot, k).start()

        @pl.when(j == nt - 1)
        def _():
            for k in range(FFN_CHUNKS):
                out_copy(j, slot, k, wait=True).wait()

            @pl.when(j >= 1)
            def _():
                for k in range(FFN_CHUNKS):
                    out_copy(j - 1, 1 - slot, k, wait=True).wait()

    for k in range(ZERO_CHUNKS):
        zero_copy(k, wait=True).wait()


def _ffn_call(tile_expert, src_rows, dst_rows, n_tiles, zero_rows, hs, wg, wu, wd, layer, max_tiles):
    rows, d = hs.shape
    ff = wg.shape[-1]
    wmap = lambda j, te, sr, dr, nt, zr: (layer, te[j], 0, 0)
    grid_spec = pltpu.PrefetchScalarGridSpec(
        num_scalar_prefetch=5,
        grid=(max_tiles,),
        in_specs=[
            pl.BlockSpec(memory_space=pl.ANY),
            pl.BlockSpec((None, 1, d, ff), wmap),
            pl.BlockSpec((None, 1, d, ff), wmap),
            pl.BlockSpec((None, 1, ff, d), wmap),
        ],
        out_specs=pl.BlockSpec(memory_space=pl.ANY),
        scratch_shapes=[
            pltpu.VMEM((2, FFN_ROWS, d), BF16),
            pltpu.VMEM((2, FFN_ROWS, d), BF16),
            pltpu.VMEM((ZERO_CHUNKS * CHUNK, d), BF16),
            pltpu.VMEM((d, ff), BF16),
            pltpu.VMEM((d, ff), BF16),
            pltpu.VMEM((ff, d), BF16),
            pltpu.SemaphoreType.DMA((2,)),
            pltpu.SemaphoreType.DMA((2,)),
            pltpu.SemaphoreType.DMA(()),
        ],
    )
    return pl.pallas_call(
        functools.partial(_ffn_kernel, dump_base=rows // CHUNK),
        grid_spec=grid_spec,
        out_shape=jax.ShapeDtypeStruct((rows + DUMP_CHUNKS * CHUNK, d), BF16),
        compiler_params=pltpu.CompilerParams(vmem_limit_bytes=48 * 1024 * 1024),
        name="expert_ffn",
    )(tile_expert, src_rows, dst_rows, n_tiles, zero_rows, hs, wg, wu, wd)


def _combine_body(ys_ref, meta_ref, x_ref, g2_ref):
    tm = TM
    eye = (lax.broadcasted_iota(I32, (tm, tm), 0) == lax.broadcasted_iota(I32, (tm, tm), 1))
    slot = lax.broadcasted_iota(I32, (tm, SLOTS), 1).astype(F32)

    def as_col(row):
        return jnp.sum(jnp.where(eye, row, 0.0), axis=1, keepdims=True)

    ys = []
    for blk in range(x_ref.shape[1] // tm):
        meta = meta_ref[blk]
        pos1, pos2 = as_col(meta[0:1]), as_col(meta[1:2])
        w1, w2 = as_col(meta[2:3]), as_col(meta[3:4])
        gate = jnp.where(slot == pos1, w1, 0.0) + jnp.where(slot == pos2, w2, 0.0)
        ys.append(_dot(gate.astype(BF16), ys_ref[blk * SLOTS:(blk + 1) * SLOTS, :]))
    return x_ref[0] + _batch_row(g2_ref) * jnp.concatenate(ys, axis=0)


def _combine_final_kernel(ys_ref, meta_ref, x_ref, g2_ref, gf_ref, xo_ref):
    xo = _combine_body(ys_ref, meta_ref, x_ref, g2_ref)
    xo_ref[0] = xo * lax.rsqrt(jnp.mean(xo * xo, axis=-1, keepdims=True) + NORM_EPS) * gf_ref[...]


def _combine_inproj_kernel(ys_ref, meta_ref, x_ref, g2_ref, *refs):
    n_in = 10
    xo_ref = refs[n_in]
    xo = _combine_body(ys_ref, meta_ref, x_ref, g2_ref)
    xo_ref[0] = xo
    _inproj_body(xo, *refs[:n_in], *refs[n_in + 1:])


def _combine_specs(d, steps, mod, layer):
    nblk = ROWS // TM
    flat = lambda bi, i: bi * steps + i
    return [
        pl.BlockSpec((nblk * SLOTS, d), lambda bi, i: (flat(bi, i), 0)),
        pl.BlockSpec((nblk, 8, TM), lambda bi, i: (flat(bi, i), 0, 0)),
        pl.BlockSpec((1, ROWS, d), lambda bi, i: (bi, i, 0)),
        _mod_spec(mod, layer, 5),
    ]


def _combine_final_call(ys, meta, x, mod, layer, gf):
    b, s, d = x.shape
    return pl.pallas_call(
        _combine_final_kernel,
        grid=(b, s // ROWS),
        in_specs=_combine_specs(d, s // ROWS, mod, layer) + [pl.BlockSpec((1, d), lambda bi, i: (0, 0))],
        out_specs=pl.BlockSpec((1, ROWS, d), lambda bi, i: (bi, i, 0)),
        out_shape=jax.ShapeDtypeStruct((b, s, d), F32),
        compiler_params=pltpu.CompilerParams(vmem_limit_bytes=48 * 1024 * 1024),
        name="moe_combine",
    )(ys, meta, x, mod, gf)


def _combine_inproj_call(ys, meta, x, mod, g, w_in, layer, tabs, p4, p16):
    b, s, d = x.shape
    in_specs, out_specs, out_shape = _inproj_specs(b, s, d, layer, mod)
    xspec = pl.BlockSpec((1, ROWS, d), lambda bi, i: (bi, i, 0))
    return pl.pallas_call(
        _combine_inproj_kernel,
        grid=(b, s // ROWS),
        in_specs=_combine_specs(d, s // ROWS, mod, layer - 1) + in_specs,
        out_specs=[xspec] + out_specs,
        out_shape=[jax.ShapeDtypeStruct((b, s, d), F32)] + out_shape,
        scratch_shapes=[pltpu.VMEM((d, IN_WIDTH), BF16)],
        compiler_params=pltpu.CompilerParams(vmem_limit_bytes=56 * 1024 * 1024),
        name="combine_in_proj",
    )(ys, meta, x, mod, mod, mod, g, w_in, *tabs, p4, p16)


def _rope_tables(positions):
    pos = positions.astype(F32)[..., None]

    def table(dim):
        inv = ROPE_THETA ** (-jnp.arange(0, dim, 2, dtype=F32) / dim)
        ang = pos * inv
        cos, sin = jnp.cos(ang), jnp.sin(ang)
        reps = LANES // dim
        return (jnp.tile(jnp.concatenate([cos, cos], -1), (1, 1, reps)),
                jnp.tile(jnp.concatenate([-sin, sin], -1), (1, 1, reps)))

    c64, s64 = table(HEAD_DIM)
    c32, s32 = table(DIFF_QK_DIM)
    return c64, s64, c32, s32


def kernel(x, c, positions, ada_w, ada_b, norm_mix_g, norm_ffn_g, w_in, w_out, diff_lambda_q1, diff_lambda_k1,
           diff_lambda_q2, diff_lambda_k2, diff_subln_g, swa_sinks, router_group_w, router_group_b,
           router_expert_w, router_expert_b, expert_w_gate, expert_w_up, expert_w_down, final_norm_g):
    b, s, d = x.shape
    depth = ada_w.shape[0]
    n = b * s
    nt = n // TM
    max_tiles = (nt * CHUNKS_PER_TILE + N_EXPERTS * (FFN_CHUNKS - 1)) // FFN_CHUNKS + 1

    tabs = _rope_tables(positions)
    p4 = _residue_perm(TM, 4)
    p16 = _residue_perm(TM, 16)
    p4_b, p16_b = jnp.asarray(p4, BF16), jnp.asarray(p16, BF16)
    p4t_b, p16t_b = jnp.asarray(p4.T, BF16), jnp.asarray(p16.T, BF16)
    tri = jnp.asarray(np.triu(np.ones((TM, TM), np.float32), 1), BF16)

    c_pad = jnp.pad(c, ((0, 8 - b), (0, 0)))
    mod = _ada_call(c_pad, ada_w, ada_b)

    sink_order = np.asarray(_SWA_HEAD_ORDER, np.int32)
    zpad = lambda k: jnp.zeros((depth, d, k), F32)
    wr = jnp.concatenate([router_group_w, zpad(8 - N_GROUPS), router_expert_w,
                          zpad(ROUTER_COLS - 8 - N_EXPERTS)], axis=-1)
    wr_hi = wr.astype(BF16)
    wr = jnp.concatenate([wr_hi, (wr - wr_hi.astype(F32)).astype(BF16)], axis=-1)
    br = jnp.concatenate([router_group_b, jnp.full((depth, 8 - N_GROUPS), NEG_INF, F32), router_expert_b,
                          jnp.zeros((depth, ROUTER_COLS - 8 - N_EXPERTS), F32)], axis=-1).reshape(depth, 1, -1)

    lam_init = [0.8 - 0.6 * math.exp(-0.3 * l) for l in range(depth)]
    lam = (jnp.exp(jnp.sum(diff_lambda_q1 * diff_lambda_k1, axis=-1))
           - jnp.exp(jnp.sum(diff_lambda_q2 * diff_lambda_k2, axis=-1)) + jnp.asarray(lam_init, F32))
    g_cols = jnp.broadcast_to(diff_subln_g[:, :, None], (depth, HEAD_DIM, TQ))
    sinks = swa_sinks[:, sink_order].reshape(-1)
    g_mix = norm_mix_g.reshape(depth, 1, d)
    g_ffn = norm_ffn_g.reshape(depth, 1, d)

    proj, vat, qkv4, qkv16 = _inproj_call(x, mod, g_mix, w_in, 0, tabs, p4_b, p16_b)
    for l in range(depth):
        oa = _diff_attn_call(proj, vat, lam, g_cols, lam_init[l], l)
        ob = _band_call(proj, COL_QB, COL_KB, COL_VB, nqb=4, nkb=1, max_dist=SWA_WINDOW - 1,
                        sinks=sinks, sink_base=l * SWA_Q_HEADS, name="swa")[0]
        o1, l1 = _band_call(proj, COL_QC, COL_KC, COL_VC, nqb=2, nkb=2, max_dist=BAND, want_lse=True,
                            name="dil1")
        o4, l4 = _band_call(qkv4.reshape(b * 4, s // 4, 768), 0, 256, 512, nqb=2, nkb=2, max_dist=BAND,
                            want_lse=True, name="dil4")
        o16, l16 = _band_call(qkv16.reshape(b * 16, s // 16, 768), 0, 256, 512, nqb=2, nkb=2, max_dist=BAND,
                              want_lse=True, name="dil16")
        x, hs, meta, nch = _outproj_router_call(
            oa, ob, o1, l1, o4.reshape(b, 4, s // 4, 256), l4.reshape(b, 4, s // 4, 512),
            o16.reshape(b, 16, s // 16, 256), l16.reshape(b, 16, s // 16, 512), p4t_b, p16t_b, w_out, l, x, mod,
            g_ffn, wr, br, tri)
        sched = _ffn_schedule(nch[:, :, 0], max_tiles, nt * CHUNKS_PER_TILE)
        ys = _ffn_call(*sched, hs, expert_w_gate, expert_w_up, expert_w_down, l, max_tiles)
        if l + 1 < depth:
            x, proj, vat, qkv4, qkv16 = _combine_inproj_call(ys, meta, x, mod, g_mix, w_in, l + 1, tabs,
                                                             p4_b, p16_b)
        else:
            x = _combine_final_call(ys, meta, x, mod, l, final_norm_g.reshape(1, d))
    return x
```

```python
import functools
import math

import numpy as np
import jax
import jax.numpy as jnp
from jax import lax
from jax.experimental import pallas as pl
from jax.experimental.pallas import tpu as pltpu

F32 = jnp.float32
BF16 = jnp.bfloat16
I32 = jnp.int32

HEAD_DIM = 64
ROPE_THETA = 10000.0
NORM_EPS = 1e-6
NEG_INF = -1e30
DIFF_HEADS = 4
DIFF_QK_DIM = 32
SWA_Q_HEADS = 8
SWA_KV_HEADS = 2
SWA_WINDOW = 128
DIL_PATTERNS = ((128, 1), (512, 4), (2048, 16))
N_GROUPS = 4
EXPERTS_PER_GROUP = 4
N_EXPERTS = 16
EXPERT_FF = 512
N_ADA = 6
IN_WIDTH = 2304

LANES = 128
BF16_ROWS = 16
BAND = 128

TM = 256
ROWS = 2 * TM
CHUNK = BF16_ROWS
SLOTS = 2 * TM + N_EXPERTS * CHUNK
CHUNKS_PER_TILE = SLOTS // CHUNK
FFN_ROWS = 1024
FFN_CHUNKS = FFN_ROWS // CHUNK
ZERO_CHUNKS = -(-N_EXPERTS * FFN_CHUNKS // CHUNKS_PER_TILE)
DUMP_CHUNKS = -(-(2 * FFN_CHUNKS + ZERO_CHUNKS) // ZERO_CHUNKS) * ZERO_CHUNKS
TQ = 512
BAND_ROWS = 2048

_SWA_HEAD_ORDER = (0, 4, 1, 5, 2, 6, 3, 7)
COL_QB, COL_KB, COL_VB = 0, 512, 640
COL_Q1, COL_Q2, COL_K1, COL_K2, COL_VA = 768, 896, 1024, 1152, 1280
COL_QC, COL_KC, COL_VC = 1536, 1792, 2048
_SRC_ROPE = (32, 32, 32, 32, 0, 0, 64, 64, 64, 64, 64, 0, 64, 64, 64, 64, 0, 0)
_SRC_DEST = (6, 7, 8, 9, 10, 11, None, None, None, None, 4, 5, 12, 13, 14, 15, 16, 17)
VT_ROWS = HEAD_DIM + BF16_ROWS
_SRC_VA_CHUNK = 2


def _residue_perm(tm, d):
    p = np.zeros((tm, tm), np.float32)
    per = tm // d
    for l in range(per):
        for r in range(d):
            p[r * per + l, l * d + r] = 1.0
    return p


def _dot(a, b, **kw):
    return jnp.dot(a, b, preferred_element_type=F32, **kw)


def _dot_nt(a, b):
    return lax.dot_general(a, b, (((1,), (1,)), ((), ())), preferred_element_type=F32)


def _batch_row(ref):
    return ref[pl.ds(pl.program_id(0), 1), :]


def _modulated_norm(x, g, sc, sh):
    y = x * lax.rsqrt(jnp.mean(x * x, axis=-1, keepdims=True) + NORM_EPS)
    return (y * g) * (1.0 + sc) + sh


def _ada_kernel(c_ref, w_ref, b_ref, o_ref):
    c = c_ref[...]
    ca = c / (1.0 + jnp.exp(-c))
    rows = ca.shape[0]
    c_hi = ca.astype(BF16)
    c_lo = (ca - c_hi.astype(F32)).astype(BF16)
    w = w_ref[0]
    w_hi = w.astype(BF16)
    w_lo = (w - w_hi.astype(F32)).astype(BF16)
    main = _dot(jnp.concatenate([c_hi, c_lo], axis=0), w_hi)
    o_ref[0] = main[:rows] + main[rows:] + _dot(c_hi, w_lo) + b_ref[0]


def _ada_call(c_pad, ada_w, ada_b):
    depth, d, n = ada_w.shape
    tn = n // 2
    return pl.pallas_call(
        _ada_kernel,
        grid=(depth, n // tn),
        in_specs=[
            pl.BlockSpec((c_pad.shape[0], d), lambda l, j: (0, 0)),
            pl.BlockSpec((1, d, tn), lambda l, j: (l, 0, j)),
            pl.BlockSpec((1, 1, tn), lambda l, j: (l, 0, j)),
        ],
        out_specs=pl.BlockSpec((1, c_pad.shape[0], tn), lambda l, j: (l, 0, j)),
        out_shape=jax.ShapeDtypeStruct((depth, c_pad.shape[0], n), F32),
        compiler_params=pltpu.CompilerParams(vmem_limit_bytes=40 * 1024 * 1024),
        name="ada_mod",
    )(c_pad, ada_w, ada_b.reshape(depth, 1, n))


def _rope(t, cos, sin_signed, first_half, half):
    rot = jnp.where(first_half, pltpu.roll(t, LANES - half, 1), pltpu.roll(t, half, 1))
    return t * cos + rot * sin_signed


def _inproj_body(x, sc_ref, sh_ref, g_ref, w_ref, cs64_ref, sn64_ref, cs32_ref, sn32_ref,
                 p4_ref, p16_ref, proj_ref, vat_ref, c4_ref, c16_ref, wb):
    @pl.when((pl.program_id(0) == 0) & (pl.program_id(1) == 0))
    def _():
        wb[...] = w_ref[...].astype(BF16)

    h = _modulated_norm(x, g_ref[...], _batch_row(sc_ref), _batch_row(sh_ref))
    hb = h.astype(BF16)
    lane = lax.broadcasted_iota(I32, (1, LANES), 1)
    first64 = (lane % 64) < 32
    first32 = (lane % 32) < 16
    lo_half = lane < 64
    swa_q = []
    for cb in range(IN_WIDTH // 256):
        acc = _dot(hb, wb[:, cb * 256:(cb + 1) * 256])
        if cb == _SRC_VA_CHUNK:
            acc_t = acc.T.astype(BF16)
            for hd in range(DIFF_HEADS):
                vat_ref[0, hd * VT_ROWS:hd * VT_ROWS + HEAD_DIM, :] = acc_t[hd * HEAD_DIM:(hd + 1) * HEAD_DIM]
                vat_ref[0, hd * VT_ROWS + HEAD_DIM:(hd + 1) * VT_ROWS, :] = jnp.ones(
                    (BF16_ROWS, acc_t.shape[1]), BF16)
        for half in range(2):
            src = cb * 2 + half
            t = acc[:, half * LANES:(half + 1) * LANES]
            if _SRC_ROPE[src] == 64:
                t = _rope(t, cs64_ref[0], sn64_ref[0], first64, 32)
            elif _SRC_ROPE[src] == 32:
                t = _rope(t, cs32_ref[0], sn32_ref[0], first32, 16)
            dst = _SRC_DEST[src]
            if dst is None:
                swa_q.append(t)
            else:
                proj_ref[0, :, dst * LANES:(dst + 1) * LANES] = t.astype(BF16)
    for jb in range(SWA_Q_HEADS // 2):
        a, c = swa_q[jb // 2], swa_q[2 + jb // 2]
        if jb % 2 == 0:
            blk = jnp.where(lo_half, a, pltpu.roll(c, 64, 1))
        else:
            blk = jnp.where(lo_half, pltpu.roll(a, 64, 1), c)
        proj_ref[0, :, jb * LANES:(jb + 1) * LANES] = blk.astype(BF16)
    tm = TM
    for blk in range(x.shape[0] // tm):
        cc = proj_ref[0, blk * tm:(blk + 1) * tm, COL_QC:]
        c4 = _dot(p4_ref[...], cc).astype(BF16)
        for r in range(4):
            c4_ref[0, r, blk * (tm // 4):(blk + 1) * (tm // 4)] = c4[r * (tm // 4):(r + 1) * (tm // 4)]
        c16 = _dot(p16_ref[...], cc).astype(BF16)
        for r in range(16):
            c16_ref[0, r, blk * (tm // 16):(blk + 1) * (tm // 16)] = c16[r * (tm // 16):(r + 1) * (tm // 16)]


def _inproj_kernel(x_ref, *refs):
    _inproj_body(x_ref[0], *refs)


def _mod_spec(mod, layer, k):
    return pl.BlockSpec((None, mod.shape[1], mod.shape[2] // N_ADA), lambda bi, i: (layer, 0, k))


def _const_spec(shape, index_map):
    return pl.BlockSpec(shape, index_map, pipeline_mode=pl.Buffered(1))


def _inproj_specs(b, s, d, layer, mod):
    rows = ROWS
    row = lambda bi, i: (bi, i, 0)
    const2 = lambda bi, i: (0, 0)
    in_specs = [
        _mod_spec(mod, layer, 1),
        _mod_spec(mod, layer, 0),
        pl.BlockSpec((None, 1, d), lambda bi, i: (layer, 0, 0)),
        _const_spec((None, d, IN_WIDTH), lambda bi, i: (layer, 0, 0)),
        pl.BlockSpec((1, rows, LANES), row),
        pl.BlockSpec((1, rows, LANES), row),
        pl.BlockSpec((1, rows, LANES), row),
        pl.BlockSpec((1, rows, LANES), row),
        _const_spec((TM, TM), const2),
        _const_spec((TM, TM), const2),
    ]
    out_specs = [
        pl.BlockSpec((1, rows, IN_WIDTH), row),
        pl.BlockSpec((1, DIFF_HEADS * VT_ROWS, rows), lambda bi, i: (bi, 0, i)),
        pl.BlockSpec((1, 4, rows // 4, 768), lambda bi, i: (bi, 0, i, 0)),
        pl.BlockSpec((1, 16, rows // 16, 768), lambda bi, i: (bi, 0, i, 0)),
    ]
    out_shape = [
        jax.ShapeDtypeStruct((b, s, IN_WIDTH), BF16),
        jax.ShapeDtypeStruct((b, DIFF_HEADS * VT_ROWS, s), BF16),
        jax.ShapeDtypeStruct((b, 4, s // 4, 768), BF16),
        jax.ShapeDtypeStruct((b, 16, s // 16, 768), BF16),
    ]
    return in_specs, out_specs, out_shape


def _inproj_call(x, mod, g, w_in, layer, tabs, p4, p16):
    b, s, d = x.shape
    in_specs, out_specs, out_shape = _inproj_specs(b, s, d, layer, mod)
    return pl.pallas_call(
        _inproj_kernel,
        grid=(b, s // ROWS),
        in_specs=[pl.BlockSpec((1, ROWS, d), lambda bi, i: (bi, i, 0))] + in_specs,
        out_specs=out_specs,
        out_shape=out_shape,
        scratch_shapes=[pltpu.VMEM((d, IN_WIDTH), BF16)],
        compiler_params=pltpu.CompilerParams(vmem_limit_bytes=56 * 1024 * 1024),
        name="in_proj",
    )(x, mod, mod, g, w_in, *tabs, p4, p16)


def _diff_attn_kernel(lam_ref, q1_ref, q2_ref, q1n_ref, q2n_ref, k1_ref, k2_ref, vt_ref, g_ref, o_ref,
                      m_sc, acc_sc, s_sc, *, lambda_init, layer):
    tq = q1_ref.shape[1]
    qi = pl.program_id(1)
    lam = lam_ref[layer]
    to_log2 = DIFF_QK_DIM ** -0.5 * math.log2(math.e)
    lane = lax.broadcasted_iota(I32, (1, LANES), 1)

    def head_queries(qa_ref, qb_ref):
        qa = qa_ref[0].astype(F32) * to_log2
        qb = qb_ref[0].astype(F32) * to_log2
        out = []
        for h in range(DIFF_HEADS):
            hm = (lane // DIFF_QK_DIM) == h
            out.append((jnp.where(hm, qa, 0.0).astype(BF16), jnp.where(hm, qb, 0.0).astype(BF16)))
        return out

    qh = head_queries(q1_ref, q2_ref)
    qh_next = head_queries(q1n_ref, q2n_ref)
    causal = (lax.broadcasted_iota(I32, (tq, tq), 0) <= lax.broadcasted_iota(I32, (tq, tq), 1))

    m_sc[...] = jnp.full(m_sc.shape, NEG_INF, F32)
    acc_sc[...] = jnp.zeros(acc_sc.shape, F32)

    n_chain = 2 * DIFF_HEADS

    def scores(ch, tile, queries):
        start = pl.multiple_of(tile * tq, tq)
        k_ref = k1_ref if ch % 2 == 0 else k2_ref
        return _dot_nt(k_ref[0, pl.ds(start, tq), :], queries[ch // 2][ch % 2])

    @pl.when(qi == 0)
    def _():
        for ch in range(n_chain):
            s_sc[ch] = scores(ch, 0, qh)

    def step(j, last):
        start = pl.multiple_of(j * tq, tq)
        for ch in range(n_chain):
            st = s_sc[ch]
            s_sc[ch] = scores(ch, 0, qh_next) if last else scores(ch, j + 1, qh)
            h = ch // 2
            vt = vt_ref[0, h * VT_ROWS:(h + 1) * VT_ROWS, pl.ds(start, tq)]
            if last:
                st = jnp.where(causal, st, NEG_INF)
            m_old = m_sc[ch]
            m_new = jnp.maximum(m_old, jnp.max(st, axis=0, keepdims=True))
            p = jnp.exp2(st - m_new)
            al = jnp.exp2(m_old - m_new)
            acc_sc[ch] = al * acc_sc[ch] + _dot(vt, p.astype(BF16))
            m_sc[ch] = m_new

    def body(j, carry):
        step(j, False)
        return carry

    lax.fori_loop(0, qi, body, 0)
    step(qi, True)

    g = g_ref[...]
    outs = []
    for h in range(DIFF_HEADS):
        a1, a2 = acc_sc[2 * h], acc_sc[2 * h + 1]
        o = (a1[:HEAD_DIM] / a1[HEAD_DIM:HEAD_DIM + 1]
             - lam * (a2[:HEAD_DIM] / a2[HEAD_DIM:HEAD_DIM + 1]))
        ms = jnp.mean(o * o, axis=0, keepdims=True)
        outs.append((o * lax.rsqrt(ms + NORM_EPS)) * g * (1.0 - lambda_init))
    o_ref[0] = jnp.concatenate(outs, axis=0).T.astype(BF16)


def _diff_attn_call(proj, vat, lam, g_cols, lambda_init, layer):
    b, s, _ = proj.shape
    tq = TQ
    qspec = lambda cb: pl.BlockSpec((1, tq, LANES), lambda bi, i, cb=cb: (bi, i, cb))
    last_q = s // tq - 1
    qnext = lambda cb: pl.BlockSpec((1, tq, LANES), lambda bi, i, cb=cb: (bi, jnp.minimum(i + 1, last_q), cb))
    kspec = lambda cb: pl.BlockSpec((1, s, LANES), lambda bi, i, cb=cb: (bi, 0, cb))
    n_chain = 2 * DIFF_HEADS
    return pl.pallas_call(
        functools.partial(_diff_attn_kernel, lambda_init=lambda_init, layer=layer),
        grid=(b, s // tq),
        in_specs=[
            pl.BlockSpec(memory_space=pltpu.SMEM),
            qspec(COL_Q1 // LANES), qspec(COL_Q2 // LANES),
            qnext(COL_Q1 // LANES), qnext(COL_Q2 // LANES),
            kspec(COL_K1 // LANES), kspec(COL_K2 // LANES),
            pl.BlockSpec((1, DIFF_HEADS * VT_ROWS, s), lambda bi, i: (bi, 0, 0)),
            pl.BlockSpec((None, HEAD_DIM, 1), lambda bi, i: (layer, 0, 0)),
        ],
        out_specs=pl.BlockSpec((1, tq, 256), lambda bi, i: (bi, i, 0)),
        out_shape=jax.ShapeDtypeStruct((b, s, 256), BF16),
        scratch_shapes=[
            pltpu.VMEM((n_chain, 1, tq), F32),
            pltpu.VMEM((n_chain, VT_ROWS, tq), F32),
            pltpu.VMEM((n_chain, tq, tq), F32),
        ],
        compiler_params=pltpu.CompilerParams(vmem_limit_bytes=48 * 1024 * 1024),
        name="diff_attn",
    )(lam, proj, proj, proj, proj, proj, proj, vat, g_cols)


def _band_kernel(*refs, nqb, nkb, max_dist, has_sink, want_lse, sink_base):
    it = iter(refs)
    sink_ref = next(it) if has_sink else None
    q_ref, kp_ref, kc_ref, vp_ref, vc_ref = (next(it) for _ in range(5))
    o_ref = next(it)
    lse_ref = next(it) if want_lse else None
    kbuf, vbuf = next(it), next(it)
    nseq, rows = q_ref.shape[0], q_ref.shape[1]
    i = pl.program_id(1)
    for sq in range(nseq):
        kbuf[sq, 0:BAND, :] = kp_ref[sq]
        kbuf[sq, BAND:, :] = kc_ref[sq]
        for kb in range(nkb):
            vbuf[sq, 0:BAND, kb * 256:kb * 256 + LANES] = vp_ref[sq, :, kb * LANES:(kb + 1) * LANES]
            vbuf[sq, BAND:, kb * 256:kb * 256 + LANES] = vc_ref[sq, :, kb * LANES:(kb + 1) * LANES]
            vbuf[sq, :, kb * 256 + LANES:(kb + 1) * 256] = jnp.ones((BAND + rows, LANES), BF16)
    lane = lax.broadcasted_iota(I32, (1, LANES), 1)
    lo_half = lane < 64
    r_io = lax.broadcasted_iota(I32, (BAND, 2 * BAND), 0)
    c_io = lax.broadcasted_iota(I32, (BAND, 2 * BAND), 1)
    dist = BAND + r_io - c_io
    band = (dist >= 0) & (dist <= max_dist)
    band_first = band & ((c_io >= BAND) | (i > 0))
    col0 = lax.broadcasted_iota(I32, (1, 2 * BAND), 1) == 0
    vr = lax.broadcasted_iota(I32, (2 * BAND, 2 * LANES), 0)
    vc = lax.broadcasted_iota(I32, (2 * BAND, 2 * LANES), 1)
    sink_row = (vr == 0) & (vc < LANES)
    to_log2 = HEAD_DIM ** -0.5 * math.log2(math.e)
    units = [(sq, sb, qb) for sq in range(nseq) for sb in range(rows // BAND) for qb in range(nqb)]

    def scores(u):
        sq, sb, qb = units[u]
        kb = qb if nkb > 1 else 0
        q = q_ref[sq, sb * BAND:(sb + 1) * BAND, qb * LANES:(qb + 1) * LANES].astype(F32) * to_log2
        q2 = jnp.concatenate([jnp.where(lo_half, q, 0.0), jnp.where(lo_half, 0.0, q)], axis=0).astype(BF16)
        return _dot_nt(q2, kbuf[sq, sb * BAND:(sb + 2) * BAND, kb * LANES:(kb + 1) * LANES])

    ahead = 2
    pending = [scores(u) for u in range(min(ahead, len(units)))]
    for u, (sq, sb, qb) in enumerate(units):
        if u + ahead < len(units):
            pending.append(scores(u + ahead))
        s2 = pending[u]
        pending[u] = None
        kb = qb if nkb > 1 else 0
        r0 = sb * BAND
        msk = band_first if sb == 0 else band
        halves = []
        for hh in range(2):
            if has_sink:
                fill = jnp.where(col0, sink_ref[sink_base + qb * 2 + hh] * math.log2(math.e), NEG_INF)
            else:
                fill = NEG_INF
            halves.append(jnp.where(msk, s2[hh * BAND:(hh + 1) * BAND], fill))
        s2 = jnp.concatenate(halves, axis=0)
        m = jnp.max(s2, axis=1, keepdims=True)
        p = jnp.exp2(s2 - m).astype(BF16)
        vw = vbuf[sq, r0:r0 + 2 * BAND, kb * 256:(kb + 1) * 256]
        if has_sink:
            vw = jnp.where(sink_row, jnp.zeros_like(vw), vw)
        pv = _dot(p, vw)
        den = pv[:, LANES:]
        out = pv[:, :LANES] / den
        o = jnp.where(lo_half, out[:BAND], out[BAND:])
        o_ref[sq, r0:r0 + BAND, qb * LANES:(qb + 1) * LANES] = o.astype(BF16)
        if want_lse:
            lse2 = m + jnp.log2(den)
            ls = jnp.where(lo_half, lse2[:BAND], lse2[BAND:])
            hi = ls.astype(BF16)
            lo = (ls - hi.astype(F32)).astype(BF16)
            lse_ref[sq, r0:r0 + BAND, qb * LANES:(qb + 1) * LANES] = hi
            lse_ref[sq, r0:r0 + BAND, (nqb + qb) * LANES:(nqb + qb + 1) * LANES] = lo


def _band_call(arr, q_col, k_col, v_col, nqb, nkb, max_dist, sinks=None, sink_base=0, want_lse=False,
               name="band"):
    ns, length, _ = arr.shape
    assert sinks is None or max_dist < BAND
    rows = min(BAND_ROWS, length)
    assert length % rows == 0 and BAND_ROWS % rows == 0 and ns % (BAND_ROWS // rows) == 0
    nseq = BAND_ROWS // rows
    wq, wk = nqb * LANES, nkb * LANES
    rpb = rows // BAND
    cur = lambda col, w: pl.BlockSpec((nseq, rows, w), lambda n, i, c=col // w: (n, i, c))
    prev = lambda col, w: pl.BlockSpec(
        (nseq, BAND, w), lambda n, i, c=col // w: (n, jnp.maximum(i * rpb - 1, 0), c))
    in_specs = [cur(q_col, wq), prev(k_col, wk), cur(k_col, wk), prev(v_col, wk), cur(v_col, wk)]
    args = [arr] * 5
    if sinks is not None:
        in_specs = [pl.BlockSpec(memory_space=pltpu.SMEM)] + in_specs
        args = [sinks] + args
    out_specs = [pl.BlockSpec((nseq, rows, wq), lambda n, i: (n, i, 0))]
    out_shape = [jax.ShapeDtypeStruct((ns, length, wq), BF16)]
    if want_lse:
        out_specs.append(pl.BlockSpec((nseq, rows, 2 * wq), lambda n, i: (n, i, 0)))
        out_shape.append(jax.ShapeDtypeStruct((ns, length, 2 * wq), BF16))
    return pl.pallas_call(
        functools.partial(_band_kernel, nqb=nqb, nkb=nkb, max_dist=max_dist,
                          has_sink=sinks is not None, want_lse=want_lse, sink_base=sink_base),
        grid=(ns // nseq, length // rows),
        in_specs=in_specs,
        out_specs=out_specs,
        out_shape=out_shape,
        scratch_shapes=[pltpu.VMEM((nseq, BAND + rows, wk), BF16), pltpu.VMEM((nseq, BAND + rows, 2 * wk), BF16)],
        name=name,
    )(*args)


def _outproj_router_kernel(oa_ref, ob_ref, o1_ref, l1_ref, o4_ref, l4_ref, o16_ref, l16_ref, p4t_ref, p16t_ref,
                           w_ref, x_ref, g1_ref, sc_ref, sh_ref, g_ref, wr_ref, br_ref, tri_ref,
                           xo_ref, hs_ref, meta_ref, nch_ref, wb):
    @pl.when((pl.program_id(0) == 0) & (pl.program_id(1) == 0))
    def _():
        wb[0:256, :] = w_ref[0:256, :].astype(BF16)
        for pos, head in enumerate(_SWA_HEAD_ORDER):
            wb[256 + pos * HEAD_DIM:256 + (pos + 1) * HEAD_DIM, :] = (
                w_ref[256 + head * HEAD_DIM:256 + (head + 1) * HEAD_DIM, :].astype(BF16))
        wb[768:1024, :] = w_ref[768:1024, :].astype(BF16)

    hw = o1_ref.shape[2]
    nblk = x_ref.shape[1] // TM

    def lse_of(v):
        return v[:, :hw] + v[:, hw:]

    def unpermute(pt_ref, src_ref, width):
        per = src_ref.shape[2] // nblk
        return jnp.concatenate(
            [_dot(pt_ref[...], src_ref[0, :, k * per:(k + 1) * per, :].reshape(TM, width)) for k in range(nblk)],
            axis=0)

    o1 = o1_ref[0].astype(F32)
    ls1 = lse_of(l1_ref[0].astype(F32))
    o4 = unpermute(p4t_ref, o4_ref, hw)
    ls4 = lse_of(unpermute(p4t_ref, l4_ref, 2 * hw))
    o16 = unpermute(p16t_ref, o16_ref, hw)
    ls16 = lse_of(unpermute(p16t_ref, l16_ref, 2 * hw))
    mx = jnp.maximum(jnp.maximum(ls1, ls4), ls16)
    e1, e4, e16 = jnp.exp2(ls1 - mx), jnp.exp2(ls4 - mx), jnp.exp2(ls16 - mx)
    oc = (e1 * o1 + e4 * o4 + e16 * o16) / (e1 + e4 + e16)
    mix = (_dot(oa_ref[0], wb[0:256, :]) + _dot(ob_ref[0], wb[256:768, :])
           + _dot(oc.astype(BF16), wb[768:1024, :]))
    x1 = x_ref[0] + _batch_row(g1_ref) * mix
    xo_ref[0] = x1
    _router_body(x1, sc_ref, sh_ref, g_ref, wr_ref, br_ref, tri_ref, hs_ref, meta_ref, nch_ref)


def _outproj_router_call(oa, ob, o1, l1, o4, l4, o16, l16, p4t, p16t, w_out, layer, x, mod, g, wr, br, tri):
    b, s, d = x.shape
    rows = ROWS
    nblk = rows // TM
    steps = s // rows
    nt = b * s // TM
    row = lambda w_: pl.BlockSpec((1, rows, w_), lambda bi, i: (bi, i, 0))
    res = lambda dd, w_: pl.BlockSpec((1, dd, rows // dd, w_), lambda bi, i: (bi, 0, i, 0))
    const2 = lambda bi, i: (0, 0)
    flat = lambda bi, i: bi * steps + i
    return pl.pallas_call(
        _outproj_router_kernel,
        grid=(b, steps),
        in_specs=[
            row(256), row(512), row(256), row(512),
            res(4, 256), res(4, 512), res(16, 256), res(16, 512),
            _const_spec((TM, TM), const2), _const_spec((TM, TM), const2),
            _const_spec((None, d, d), lambda bi, i: (layer, 0, 0)),
            row(d),
            _mod_spec(mod, layer, 2),
            _mod_spec(mod, layer, 4),
            _mod_spec(mod, layer, 3),
            pl.BlockSpec((None, 1, d), lambda bi, i: (layer, 0, 0)),
            _const_spec((None, d, 2 * ROUTER_COLS), lambda bi, i: (layer, 0, 0)),
            pl.BlockSpec((None, 1, ROUTER_COLS), lambda bi, i: (layer, 0, 0)),
            _const_spec((TM, TM), const2),
        ],
        out_specs=[
            row(d),
            pl.BlockSpec((nblk * SLOTS, d), lambda bi, i: (flat(bi, i), 0)),
            pl.BlockSpec((nblk, 8, TM), lambda bi, i: (flat(bi, i), 0, 0)),
            pl.BlockSpec((nblk, N_EXPERTS, LANES), lambda bi, i: (flat(bi, i), 0, 0)),
        ],
        out_shape=[
            jax.ShapeDtypeStruct((b, s, d), F32),
            jax.ShapeDtypeStruct((nt * SLOTS, d), BF16),
            jax.ShapeDtypeStruct((nt, 8, TM), F32),
            jax.ShapeDtypeStruct((nt, N_EXPERTS, LANES), I32),
        ],
        scratch_shapes=[pltpu.VMEM((d, d), BF16)],
        compiler_params=pltpu.CompilerParams(vmem_limit_bytes=56 * 1024 * 1024),
        name="out_proj_router",
    )(oa, ob, o1, l1, o4, l4, o16, l16, p4t, p16t, w_out, x, mod, mod, mod, g, wr, br, tri)


ROUTER_COLS = LANES


def _router_body(x, sc_ref, sh_ref, g_ref, wr_ref, br_ref, tri_ref, hs_ref, meta_ref, nch_ref):
    tm = TM
    h = _modulated_norm(x, g_ref[...], _batch_row(sc_ref), _batch_row(sh_ref))
    hb = h.astype(BF16)
    h_lo = (h - hb.astype(F32)).astype(BF16)
    part = _dot(hb, wr_ref[...]) + _dot(h_lo, wr_ref[...])
    logits = part[:, :ROUTER_COLS] + part[:, ROUTER_COLS:] + br_ref[...]
    r8 = lax.broadcasted_iota(I32, (8, tm), 0)
    r16 = lax.broadcasted_iota(I32, (N_EXPERTS, tm), 0)
    rl = lax.broadcasted_iota(I32, (N_EXPERTS, LANES), 0)
    slot = lax.broadcasted_iota(I32, (SLOTS, tm), 0)
    for blk in range(x.shape[0] // tm):
        lt = logits[blk * tm:(blk + 1) * tm].T
        glog = lt[0:8]
        elog = lt[8:8 + N_EXPERTS]

        gmax = jnp.max(glog, axis=0, keepdims=True)
        g_w = 1.0 / jnp.sum(jnp.exp(glog - gmax), axis=0, keepdims=True)
        g_idx = jnp.min(jnp.where(glog == gmax, r8, 99), axis=0, keepdims=True)

        el = jnp.where((r16 // EXPERTS_PER_GROUP) == g_idx, elog, NEG_INF)
        emax = jnp.max(el, axis=0, keepdims=True)
        e1 = jnp.min(jnp.where(el == emax, r16, 99), axis=0, keepdims=True)
        el2 = jnp.where(r16 == e1, NEG_INF, el)
        emax2 = jnp.max(el2, axis=0, keepdims=True)
        e2 = jnp.min(jnp.where(el2 == emax2, r16, 99), axis=0, keepdims=True)
        p2 = jnp.exp(emax2 - emax)
        wt1 = g_w / (1.0 + p2)
        wt2 = g_w * p2 / (1.0 + p2)

        oh1 = r16 == e1
        oh2 = r16 == e2
        onehot = jnp.where(oh1, 1.0, 0.0) + jnp.where(oh2, 1.0, 0.0)
        cnt = jnp.sum(onehot, axis=1, keepdims=True)
        nch = jnp.floor((cnt + (CHUNK - 1)) * (1.0 / CHUNK))
        nchb = jnp.broadcast_to(nch, (N_EXPERTS, LANES))
        incl = nchb
        for sft in (1, 2, 4, 8):
            incl = incl + jnp.where(rl >= sft, pltpu.roll(incl, sft, 0), 0.0)
        off = (incl - nchb)[:, 0:1] * float(CHUNK)
        rank = _dot(onehot.astype(BF16), tri_ref[...])
        slot_of = off + rank
        pos1 = jnp.sum(jnp.where(oh1, slot_of, 0.0), axis=0, keepdims=True)
        pos2 = jnp.sum(jnp.where(oh2, slot_of, 0.0), axis=0, keepdims=True)

        sel = jnp.where(slot == pos1.astype(I32), 1.0, jnp.where(slot == pos2.astype(I32), 1.0, 0.0))
        hs_ref[blk * SLOTS:(blk + 1) * SLOTS, :] = _dot(sel.astype(BF16), hb[blk * tm:(blk + 1) * tm]).astype(BF16)

        meta_ref[blk] = jnp.concatenate([pos1, pos2, wt1, wt2, jnp.zeros((4, tm), F32)], axis=0)
        nch_ref[blk] = nchb.astype(I32)


def _ffn_schedule(nch, max_tiles, dump_base):
    nt = nch.shape[0]
    cend = jnp.cumsum(nch, axis=1)
    coff = cend - nch
    tcum = jnp.cumsum(nch, axis=0)
    before = tcum - nch
    tot = tcum[-1]
    pad = ((tot + FFN_CHUNKS - 1) // FFN_CHUNKS) * FFN_CHUNKS
    eend = jnp.cumsum(pad)
    estart = eend - pad
    n_tiles = (eend[-1] // FFN_CHUNKS).astype(I32)
    first_chunk = jnp.arange(max_tiles, dtype=I32) * FFN_CHUNKS
    tile_expert = jnp.sum((eend[None, :] <= first_chunk[:, None]).astype(I32), axis=1)
    tile_expert = jnp.minimum(tile_expert, N_EXPERTS - 1)
    hp = lax.Precision.HIGHEST
    pos = jnp.arange(max_tiles * FFN_CHUNKS, dtype=I32)
    e_s = jnp.minimum(jnp.sum((eend[None, :] <= (pos // FFN_CHUNKS * FFN_CHUNKS)[:, None]).astype(I32), axis=1),
                      N_EXPERTS - 1)
    oh_e = (e_s[:, None] == jnp.arange(N_EXPERTS, dtype=I32)[None, :]).astype(F32)
    idx = pos - jnp.dot(oh_e, estart.astype(F32), precision=hp).astype(I32)
    run_end = jnp.dot(oh_e, tcum.T.astype(F32), precision=hp).astype(I32)
    run_beg = jnp.dot(oh_e, before.T.astype(F32), precision=hp).astype(I32)
    run_off = jnp.dot(oh_e, coff.T.astype(F32), precision=hp).astype(I32)
    in_run = (idx[:, None] >= run_beg) & (idx[:, None] < run_end)
    tile_base = jnp.arange(nt, dtype=I32)[None, :] * CHUNKS_PER_TILE
    src = jnp.sum(jnp.where(in_run, tile_base + run_off + idx[:, None] - run_beg, 0), axis=1)
    real = jnp.any(in_run, axis=1)
    dump = dump_base + (pos // FFN_CHUNKS % 2) * FFN_CHUNKS + pos % FFN_CHUNKS
    src_rows = jnp.where(real, src, 0) * CHUNK
    dst_rows = jnp.where(real, src, dump) * CHUNK
    used = cend[:, -1]
    ucum = jnp.cumsum(CHUNKS_PER_TILE - used)
    ubeg = ucum - (CHUNKS_PER_TILE - used)
    z = jnp.arange(max_tiles * ZERO_CHUNKS, dtype=I32)[:, None]
    in_gap = (z >= ubeg[None, :]) & (z < ucum[None, :])
    zsrc = jnp.sum(jnp.where(in_gap, tile_base + used[None, :] + z - ubeg[None, :], 0), axis=1)
    zdump = dump_base + 2 * FFN_CHUNKS + z[:, 0] % ZERO_CHUNKS
    zero_rows = jnp.where(jnp.any(in_gap, axis=1), zsrc, zdump) * CHUNK
    return tile_expert, src_rows, dst_rows, n_tiles.reshape(1), zero_rows


def _ffn_kernel(te_ref, sr_ref, dr_ref, nt_ref, zr_ref, hs_hbm, wg_ref, wu_ref, wd_ref, ys_hbm,
                xbuf, ybuf, zbuf, wgb, wub, wdb, in_sem, out_sem, zero_sem, *, dump_base):
    j = pl.program_id(0)
    nt = nt_ref[0]
    half_ff = EXPERT_FF // 2

    def rows_at(r):
        return pl.ds(pl.multiple_of(r, CHUNK), CHUNK)

    def in_copy(step, slot, k, wait=False):
        r = 0 if wait else sr_ref[step * FFN_CHUNKS + k]
        return pltpu.make_async_copy(hs_hbm.at[rows_at(r), :], xbuf.at[slot, pl.ds(k * CHUNK, CHUNK), :],
                                     in_sem.at[slot])

    def out_copy(step, slot, k, wait=False):
        r = 0 if wait else dr_ref[step * FFN_CHUNKS + k]
        return pltpu.make_async_copy(ybuf.at[slot, pl.ds(k * CHUNK, CHUNK), :], ys_hbm.at[rows_at(r), :],
                                     out_sem.at[slot])

    def zero_copy(k, wait=False):
        r = 0 if wait else zr_ref[j * ZERO_CHUNKS + k]
        return pltpu.make_async_copy(zbuf.at[pl.ds(k * CHUNK, CHUNK), :], ys_hbm.at[rows_at(r), :], zero_sem)

    @pl.when(j == 0)
    def _():
        zbuf[...] = jnp.zeros_like(zbuf)
        fills = [pltpu.make_async_copy(
            zbuf, ys_hbm.at[pl.ds((dump_base + r * ZERO_CHUNKS) * CHUNK, ZERO_CHUNKS * CHUNK), :], zero_sem)
            for r in range(DUMP_CHUNKS // ZERO_CHUNKS)]
        for cp in fills:
            cp.start()
        for cp in fills:
            cp.wait()

    for k in range(ZERO_CHUNKS):
        zero_copy(k).start()

    @pl.when(j < nt)
    def _():
        slot = j % 2

        @pl.when(j == 0)
        def _():
            for k in range(FFN_CHUNKS):
                in_copy(0, 0, k).start()

        @pl.when(j + 1 < nt)
        def _():
            for k in range(FFN_CHUNKS):
                in_copy(j + 1, 1 - slot, k).start()

        @pl.when((j == 0) | (te_ref[j] != te_ref[jnp.maximum(j - 1, 0)]))
        def _():
            wgb[...] = wg_ref[0].astype(BF16)
            wub[...] = wu_ref[0].astype(BF16)
            wdb[...] = wd_ref[0].astype(BF16)

        for k in range(FFN_CHUNKS):
            in_copy(j, slot, k, wait=True).wait()

        @pl.when(j >= 2)
        def _():
            for k in range(FFN_CHUNKS):
                out_copy(j - 2, slot, k, wait=True).wait()

        x = xbuf[slot]
        hg = [_dot(x, wgb[:, h * half_ff:(h + 1) * half_ff]) for h in range(2)]
        hu = [_dot(x, wub[:, h * half_ff:(h + 1) * half_ff]) for h in range(2)]
        y = None
        for h in range(2):
            act = ((hg[h] / (1.0 + jnp.exp(-hg[h]))) * hu[h]).astype(BF16)
            part = _dot(act, wdb[h * half_ff:(h + 1) * half_ff, :])
            y = part if y is None else y + part
        ybuf[slot] = y.astype(BF16)
        for k in range(FFN_CHUNKS):
            out_copy(j, slot, k).start()

        @pl.when(j == nt - 1)
        def _():
            for k in range(FFN_CHUNKS):
                out_copy(j, slot, k, wait=True).wait()

            @pl.when(j >= 1)
            def _():
                for k in range(FFN_CHUNKS):
                    out_copy(j - 1, 1 - slot, k, wait=True).wait()

    for k in range(ZERO_CHUNKS):
        zero_copy(k, wait=True).wait()


def _ffn_call(tile_expert, src_rows, dst_rows, n_tiles, zero_rows, hs, wg, wu, wd, layer, max_tiles):
    rows, d = hs.shape
    ff = wg.shape[-1]
    wmap = lambda j, te, sr, dr, nt, zr: (layer, te[j], 0, 0)
    grid_spec = pltpu.PrefetchScalarGridSpec(
        num_scalar_prefetch=5,
        grid=(max_tiles,),
        in_specs=[
            pl.BlockSpec(memory_space=pl.ANY),
            pl.BlockSpec((None, 1, d, ff), wmap),
            pl.BlockSpec((None, 1, d, ff), wmap),
            pl.BlockSpec((None, 1, ff, d), wmap),
        ],
        out_specs=pl.BlockSpec(memory_space=pl.ANY),
        scratch_shapes=[
            pltpu.VMEM((2, FFN_ROWS, d), BF16),
            pltpu.VMEM((2, FFN_ROWS, d), BF16),
            pltpu.VMEM((ZERO_CHUNKS * CHUNK, d), BF16),
            pltpu.VMEM((d, ff), BF16),
            pltpu.VMEM((d, ff), BF16),
            pltpu.VMEM((ff, d), BF16),
            pltpu.SemaphoreType.DMA((2,)),
            pltpu.SemaphoreType.DMA((2,)),
            pltpu.SemaphoreType.DMA(()),
        ],
    )
    return pl.pallas_call(
        functools.partial(_ffn_kernel, dump_base=rows // CHUNK),
        grid_spec=grid_spec,
        out_shape=jax.ShapeDtypeStruct((rows + DUMP_CHUNKS * CHUNK, d), BF16),
        compiler_params=pltpu.CompilerParams(vmem_limit_bytes=48 * 1024 * 1024),
        name="expert_ffn",
    )(tile_expert, src_rows, dst_rows, n_tiles, zero_rows, hs, wg, wu, wd)


def _combine_body(ys_ref, meta_ref, x_ref, g2_ref):
    tm = TM
    eye = (lax.broadcasted_iota(I32, (tm, tm), 0) == lax.broadcasted_iota(I32, (tm, tm), 1))
    slot = lax.broadcasted_iota(I32, (tm, SLOTS), 1).astype(F32)

    def as_col(row):
        return jnp.sum(jnp.where(eye, row, 0.0), axis=1, keepdims=True)

    ys = []
    for blk in range(x_ref.shape[1] // tm):
        meta = meta_ref[blk]
        pos1, pos2 = as_col(meta[0:1]), as_col(meta[1:2])
        w1, w2 = as_col(meta[2:3]), as_col(meta[3:4])
        gate = jnp.where(slot == pos1, w1, 0.0) + jnp.where(slot == pos2, w2, 0.0)
        ys.append(_dot(gate.astype(BF16), ys_ref[blk * SLOTS:(blk + 1) * SLOTS, :]))
    return x_ref[0] + _batch_row(g2_ref) * jnp.concatenate(ys, axis=0)


def _combine_final_kernel(ys_ref, meta_ref, x_ref, g2_ref, gf_ref, xo_ref):
    xo = _combine_body(ys_ref, meta_ref, x_ref, g2_ref)
    xo_ref[0] = xo * lax.rsqrt(jnp.mean(xo * xo, axis=-1, keepdims=True) + NORM_EPS) * gf_ref[...]


def _combine_inproj_kernel(ys_ref, meta_ref, x_ref, g2_ref, *refs):
    n_in = 10
    xo_ref = refs[n_in]
    xo = _combine_body(ys_ref, meta_ref, x_ref, g2_ref)
    xo_ref[0] = xo
    _inproj_body(xo, *refs[:n_in], *refs[n_in + 1:])


def _combine_specs(d, steps, mod, layer):
    nblk = ROWS // TM
    flat = lambda bi, i: bi * steps + i
    return [
        pl.BlockSpec((nblk * SLOTS, d), lambda bi, i: (flat(bi, i), 0)),
        pl.BlockSpec((nblk, 8, TM), lambda bi, i: (flat(bi, i), 0, 0)),
        pl.BlockSpec((1, ROWS, d), lambda bi, i: (bi, i, 0)),
        _mod_spec(mod, layer, 5),
    ]


def _combine_final_call(ys, meta, x, mod, layer, gf):
    b, s, d = x.shape
    return pl.pallas_call(
        _combine_final_kernel,
        grid=(b, s // ROWS),
        in_specs=_combine_specs(d, s // ROWS, mod, layer) + [pl.BlockSpec((1, d), lambda bi, i: (0, 0))],
        out_specs=pl.BlockSpec((1, ROWS, d), lambda bi, i: (bi, i, 0)),
        out_shape=jax.ShapeDtypeStruct((b, s, d), F32),
        compiler_params=pltpu.CompilerParams(vmem_limit_bytes=48 * 1024 * 1024),
        name="moe_combine",
    )(ys, meta, x, mod, gf)


def _combine_inproj_call(ys, meta, x, mod, g, w_in, layer, tabs, p4, p16):
    b, s, d = x.shape
    in_specs, out_specs, out_shape = _inproj_specs(b, s, d, layer, mod)
    xspec = pl.BlockSpec((1, ROWS, d), lambda bi, i: (bi, i, 0))
    return pl.pallas_call(
        _combine_inproj_kernel,
        grid=(b, s // ROWS),
        in_specs=_combine_specs(d, s // ROWS, mod, layer - 1) + in_specs,
        out_specs=[xspec] + out_specs,
        out_shape=[jax.ShapeDtypeStruct((b, s, d), F32)] + out_shape,
        scratch_shapes=[pltpu.VMEM((d, IN_WIDTH), BF16)],
        compiler_params=pltpu.CompilerParams(vmem_limit_bytes=56 * 1024 * 1024),
        name="combine_in_proj",
    )(ys, meta, x, mod, mod, mod, g, w_in, *tabs, p4, p16)


def _rope_tables(positions):
    pos = positions.astype(F32)[..., None]

    def table(dim):
        inv = ROPE_THETA ** (-jnp.arange(0, dim, 2, dtype=F32) / dim)
        ang = pos * inv
        cos, sin = jnp.cos(ang), jnp.sin(ang)
        reps = LANES // dim
        return (jnp.tile(jnp.concatenate([cos, cos], -1), (1, 1, reps)),
                jnp.tile(jnp.concatenate([-sin, sin], -1), (1, 1, reps)))

    c64, s64 = table(HEAD_DIM)
    c32, s32 = table(DIFF_QK_DIM)
    return c64, s64, c32, s32


def kernel(x, c, positions, ada_w, ada_b, norm_mix_g, norm_ffn_g, w_in, w_out, diff_lambda_q1, diff_lambda_k1,
           diff_lambda_q2, diff_lambda_k2, diff_subln_g, swa_sinks, router_group_w, router_group_b,
           router_expert_w, router_expert_b, expert_w_gate, expert_w_up, expert_w_down, final_norm_g):
    b, s, d = x.shape
    depth = ada_w.shape[0]
    assert b <= BF16_ROWS and s % TQ == 0 and s % ROWS == 0 and s % (16 * BAND) == 0 and d == 8 * LANES
    n = b * s
    nt = n // TM
    max_tiles = (nt * CHUNKS_PER_TILE + N_EXPERTS * (FFN_CHUNKS - 1)) // FFN_CHUNKS + 1

    tabs = _rope_tables(positions)
    p4 = _residue_perm(TM, 4)
    p16 = _residue_perm(TM, 16)
    p4_b, p16_b = jnp.asarray(p4, BF16), jnp.asarray(p16, BF16)
    p4t_b, p16t_b = jnp.asarray(p4.T, BF16), jnp.asarray(p16.T, BF16)
    tri = jnp.asarray(np.triu(np.ones((TM, TM), np.float32), 1), BF16)

    c_pad = jnp.pad(c, ((0, BF16_ROWS - b), (0, 0)))
    mod = _ada_call(c_pad, ada_w, ada_b)

    sink_order = np.asarray(_SWA_HEAD_ORDER, np.int32)
    zpad = lambda k: jnp.zeros((depth, d, k), F32)
    wr = jnp.concatenate([router_group_w, zpad(8 - N_GROUPS), router_expert_w,
                          zpad(ROUTER_COLS - 8 - N_EXPERTS)], axis=-1)
    wr_hi = wr.astype(BF16)
    wr = jnp.concatenate([wr_hi, (wr - wr_hi.astype(F32)).astype(BF16)], axis=-1)
    br = jnp.concatenate([router_group_b, jnp.full((depth, 8 - N_GROUPS), NEG_INF, F32), router_expert_b,
                          jnp.zeros((depth, ROUTER_COLS - 8 - N_EXPERTS), F32)], axis=-1).reshape(depth, 1, -1)

    lam_init = [0.8 - 0.6 * math.exp(-0.3 * l) for l in range(depth)]
    lam = (jnp.exp(jnp.sum(diff_lambda_q1 * diff_lambda_k1, axis=-1))
           - jnp.exp(jnp.sum(diff_lambda_q2 * diff_lambda_k2, axis=-1)) + jnp.asarray(lam_init, F32))
    g_cols = diff_subln_g[:, :, None]
    sinks = swa_sinks[:, sink_order].reshape(-1)
    g_mix = norm_mix_g.reshape(depth, 1, d)
    g_ffn = norm_ffn_g.reshape(depth, 1, d)

    proj, vat, qkv4, qkv16 = _inproj_call(x, mod, g_mix, w_in, 0, tabs, p4_b, p16_b)
    for l in range(depth):
        oa = _diff_attn_call(proj, vat, lam, g_cols, lam_init[l], l)
        ob = _band_call(proj, COL_QB, COL_KB, COL_VB, nqb=4, nkb=1, max_dist=SWA_WINDOW - 1,
                        sinks=sinks, sink_base=l * SWA_Q_HEADS, name="swa")[0]
        o1, l1 = _band_call(proj, COL_QC, COL_KC, COL_VC, nqb=2, nkb=2, max_dist=BAND, want_lse=True,
                            name="dil1")
        o4, l4 = _band_call(qkv4.reshape(b * 4, s // 4, 768), 0, 256, 512, nqb=2, nkb=2, max_dist=BAND,
                            want_lse=True, name="dil4")
        o16, l16 = _band_call(qkv16.reshape(b * 16, s // 16, 768), 0, 256, 512, nqb=2, nkb=2, max_dist=BAND,
                              want_lse=True, name="dil16")
        x, hs, meta, nch = _outproj_router_call(
            oa, ob, o1, l1, o4.reshape(b, 4, s // 4, 256), l4.reshape(b, 4, s // 4, 512),
            o16.reshape(b, 16, s // 16, 256), l16.reshape(b, 16, s // 16, 512), p4t_b, p16t_b, w_out, l, x, mod,
            g_ffn, wr, br, tri)
        sched = _ffn_schedule(nch[:, :, 0], max_tiles, nt * CHUNKS_PER_TILE)
        ys = _ffn_call(*sched, hs, expert_w_gate, expert_w_up, expert_w_down, l, max_tiles)
        if l + 1 < depth:
            x, proj, vat, qkv4, qkv16 = _combine_inproj_call(ys, meta, x, mod, g_mix, w_in, l + 1, tabs,
                                                             p4_b, p16_b)
        else:
            x = _combine_final_call(ys, meta, x, mod, l, final_norm_g.reshape(1, d))
    return x
```

```python
import functools
import math

import numpy as np
import jax
import jax.numpy as jnp
from jax import lax
from jax.experimental import pallas as pl
from jax.experimental.pallas import tpu as pltpu

F32 = jnp.float32
BF16 = jnp.bfloat16
I32 = jnp.int32

HEAD_DIM = 64
ROPE_THETA = 10000.0
NORM_EPS = 1e-6
NEG_INF = -1e30
DIFF_HEADS = 4
DIFF_QK_DIM = 32
SWA_Q_HEADS = 8
SWA_KV_HEADS = 2
SWA_WINDOW = 128
DIL_PATTERNS = ((128, 1), (512, 4), (2048, 16))
N_GROUPS = 4
EXPERTS_PER_GROUP = 4
N_EXPERTS = 16
EXPERT_FF = 512
N_ADA = 6
IN_WIDTH = 2304

LANES = 128
BF16_ROWS = 16
BAND = 128

TM = 256
ROWS = 2 * TM
CHUNK = BF16_ROWS
SLOTS = 2 * TM + N_EXPERTS * CHUNK
CHUNKS_PER_TILE = SLOTS // CHUNK
FFN_ROWS = 1024
FFN_CHUNKS = FFN_ROWS // CHUNK
ZERO_CHUNKS = -(-N_EXPERTS * FFN_CHUNKS // CHUNKS_PER_TILE)
DUMP_CHUNKS = -(-(2 * FFN_CHUNKS + ZERO_CHUNKS) // ZERO_CHUNKS) * ZERO_CHUNKS
TQ = 512
BAND_ROWS = 2048

_SWA_HEAD_ORDER = (0, 4, 1, 5, 2, 6, 3, 7)
COL_QB, COL_KB, COL_VB = 0, 512, 640
COL_Q1, COL_Q2, COL_K1, COL_K2, COL_VA = 768, 896, 1024, 1152, 1280
COL_QC, COL_KC, COL_VC = 1536, 1792, 2048
_SRC_ROPE = (32, 32, 32, 32, 0, 0, 64, 64, 64, 64, 64, 0, 64, 64, 64, 64, 0, 0)
_SRC_DEST = (6, 7, 8, 9, 10, 11, None, None, None, None, 4, 5, 12, 13, 14, 15, 16, 17)
VT_ROWS = HEAD_DIM + BF16_ROWS
_SRC_VA_CHUNK = 2


def _residue_perm(tm, d):
    p = np.zeros((tm, tm), np.float32)
    per = tm // d
    for l in range(per):
        for r in range(d):
            p[r * per + l, l * d + r] = 1.0
    return p


def _dot(a, b, **kw):
    return jnp.dot(a, b, preferred_element_type=F32, **kw)


def _dot_nt(a, b):
    return lax.dot_general(a, b, (((1,), (1,)), ((), ())), preferred_element_type=F32)


def _batch_row(ref):
    return ref[pl.ds(pl.program_id(0), 1), :]


def _modulated_norm(x, g, sc, sh):
    y = x * lax.rsqrt(jnp.mean(x * x, axis=-1, keepdims=True) + NORM_EPS)
    return (y * g) * (1.0 + sc) + sh


def _ada_kernel(c_ref, w_ref, b_ref, o_ref):
    c = c_ref[...]
    ca = c / (1.0 + jnp.exp(-c))
    rows = ca.shape[0]
    c_hi = ca.astype(BF16)
    c_lo = (ca - c_hi.astype(F32)).astype(BF16)
    w = w_ref[0]
    w_hi = w.astype(BF16)
    w_lo = (w - w_hi.astype(F32)).astype(BF16)
    main = _dot(jnp.concatenate([c_hi, c_lo], axis=0), w_hi)
    o_ref[0] = main[:rows] + main[rows:] + _dot(c_hi, w_lo) + b_ref[0]


def _ada_call(c_pad, ada_w, ada_b):
    depth, d, n = ada_w.shape
    tn = n // 2
    return pl.pallas_call(
        _ada_kernel,
        grid=(depth, n // tn),
        in_specs=[
            pl.BlockSpec((c_pad.shape[0], d), lambda l, j: (0, 0)),
            pl.BlockSpec((1, d, tn), lambda l, j: (l, 0, j)),
            pl.BlockSpec((1, 1, tn), lambda l, j: (l, 0, j)),
        ],
        out_specs=pl.BlockSpec((1, c_pad.shape[0], tn), lambda l, j: (l, 0, j)),
        out_shape=jax.ShapeDtypeStruct((depth, c_pad.shape[0], n), F32),
        compiler_params=pltpu.CompilerParams(vmem_limit_bytes=40 * 1024 * 1024),
        name="ada_mod",
    )(c_pad, ada_w, ada_b.reshape(depth, 1, n))


def _rope(t, cos, sin_signed, first_half, half):
    rot = jnp.where(first_half, pltpu.roll(t, LANES - half, 1), pltpu.roll(t, half, 1))
    return t * cos + rot * sin_signed


def _inproj_body(x, sc_ref, sh_ref, g_ref, w_ref, cs64_ref, sn64_ref, cs32_ref, sn32_ref,
                 p4_ref, p16_ref, proj_ref, vat_ref, c4_ref, c16_ref, wb):
    @pl.when((pl.program_id(0) == 0) & (pl.program_id(1) == 0))
    def _():
        wb[...] = w_ref[...].astype(BF16)

    h = _modulated_norm(x, g_ref[...], _batch_row(sc_ref), _batch_row(sh_ref))
    hb = h.astype(BF16)
    lane = lax.broadcasted_iota(I32, (1, LANES), 1)
    first64 = (lane % 64) < 32
    first32 = (lane % 32) < 16
    lo_half = lane < 64
    swa_q = []
    for cb in range(IN_WIDTH // 256):
        acc = _dot(hb, wb[:, cb * 256:(cb + 1) * 256])
        if cb == _SRC_VA_CHUNK:
            acc_t = acc.T.astype(BF16)
            for hd in range(DIFF_HEADS):
                vat_ref[0, hd * VT_ROWS:hd * VT_ROWS + HEAD_DIM, :] = acc_t[hd * HEAD_DIM:(hd + 1) * HEAD_DIM]
                vat_ref[0, hd * VT_ROWS + HEAD_DIM:(hd + 1) * VT_ROWS, :] = jnp.ones(
                    (BF16_ROWS, acc_t.shape[1]), BF16)
        for half in range(2):
            src = cb * 2 + half
            t = acc[:, half * LANES:(half + 1) * LANES]
            if _SRC_ROPE[src] == 64:
                t = _rope(t, cs64_ref[0], sn64_ref[0], first64, 32)
            elif _SRC_ROPE[src] == 32:
                t = _rope(t, cs32_ref[0], sn32_ref[0], first32, 16)
            dst = _SRC_DEST[src]
            if dst is None:
                swa_q.append(t)
            else:
                proj_ref[0, :, dst * LANES:(dst + 1) * LANES] = t.astype(BF16)
    for jb in range(SWA_Q_HEADS // 2):
        a, c = swa_q[jb // 2], swa_q[2 + jb // 2]
        if jb % 2 == 0:
            blk = jnp.where(lo_half, a, pltpu.roll(c, 64, 1))
        else:
            blk = jnp.where(lo_half, pltpu.roll(a, 64, 1), c)
        proj_ref[0, :, jb * LANES:(jb + 1) * LANES] = blk.astype(BF16)
    tm = TM
    for blk in range(x.shape[0] // tm):
        cc = proj_ref[0, blk * tm:(blk + 1) * tm, COL_QC:]
        c4 = _dot(p4_ref[...], cc).astype(BF16)
        for r in range(4):
            c4_ref[0, r, blk * (tm // 4):(blk + 1) * (tm // 4)] = c4[r * (tm // 4):(r + 1) * (tm // 4)]
        c16 = _dot(p16_ref[...], cc).astype(BF16)
        for r in range(16):
            c16_ref[0, r, blk * (tm // 16):(blk + 1) * (tm // 16)] = c16[r * (tm // 16):(r + 1) * (tm // 16)]


def _inproj_kernel(x_ref, *refs):
    _inproj_body(x_ref[0], *refs)


def _mod_spec(mod, layer, k):
    return pl.BlockSpec((None, mod.shape[1], mod.shape[2] // N_ADA), lambda bi, i: (layer, 0, k))


def _const_spec(shape, index_map):
    return pl.BlockSpec(shape, index_map, pipeline_mode=pl.Buffered(1))


def _inproj_specs(b, s, d, layer, mod):
    rows = ROWS
    row = lambda bi, i: (bi, i, 0)
    const2 = lambda bi, i: (0, 0)
    in_specs = [
        _mod_spec(mod, layer, 1),
        _mod_spec(mod, layer, 0),
        pl.BlockSpec((None, 1, d), lambda bi, i: (layer, 0, 0)),
        _const_spec((None, d, IN_WIDTH), lambda bi, i: (layer, 0, 0)),
        pl.BlockSpec((1, rows, LANES), row),
        pl.BlockSpec((1, rows, LANES), row),
        pl.BlockSpec((1, rows, LANES), row),
        pl.BlockSpec((1, rows, LANES), row),
        _const_spec((TM, TM), const2),
        _const_spec((TM, TM), const2),
    ]
    out_specs = [
        pl.BlockSpec((1, rows, IN_WIDTH), row),
        pl.BlockSpec((1, DIFF_HEADS * VT_ROWS, rows), lambda bi, i: (bi, 0, i)),
        pl.BlockSpec((1, 4, rows // 4, 768), lambda bi, i: (bi, 0, i, 0)),
        pl.BlockSpec((1, 16, rows // 16, 768), lambda bi, i: (bi, 0, i, 0)),
    ]
    out_shape = [
        jax.ShapeDtypeStruct((b, s, IN_WIDTH), BF16),
        jax.ShapeDtypeStruct((b, DIFF_HEADS * VT_ROWS, s), BF16),
        jax.ShapeDtypeStruct((b, 4, s // 4, 768), BF16),
        jax.ShapeDtypeStruct((b, 16, s // 16, 768), BF16),
    ]
    return in_specs, out_specs, out_shape


def _inproj_call(x, mod, g, w_in, layer, tabs, p4, p16):
    b, s, d = x.shape
    in_specs, out_specs, out_shape = _inproj_specs(b, s, d, layer, mod)
    return pl.pallas_call(
        _inproj_kernel,
        grid=(b, s // ROWS),
        in_specs=[pl.BlockSpec((1, ROWS, d), lambda bi, i: (bi, i, 0))] + in_specs,
        out_specs=out_specs,
        out_shape=out_shape,
        scratch_shapes=[pltpu.VMEM((d, IN_WIDTH), BF16)],
        compiler_params=pltpu.CompilerParams(vmem_limit_bytes=56 * 1024 * 1024),
        name="in_proj",
    )(x, mod, mod, g, w_in, *tabs, p4, p16)


def _diff_attn_kernel(lam_ref, q1_ref, q2_ref, q1n_ref, q2n_ref, k1_ref, k2_ref, vt_ref, g_ref, o_ref,
                      m_sc, acc_sc, s_sc, *, lambda_init, layer):
    tq = q1_ref.shape[1]
    qi = pl.program_id(1)
    lam = lam_ref[layer]
    to_log2 = DIFF_QK_DIM ** -0.5 * math.log2(math.e)
    lane = lax.broadcasted_iota(I32, (1, LANES), 1)

    def head_queries(qa_ref, qb_ref):
        qa = qa_ref[0].astype(F32) * to_log2
        qb = qb_ref[0].astype(F32) * to_log2
        out = []
        for h in range(DIFF_HEADS):
            hm = (lane // DIFF_QK_DIM) == h
            out.append((jnp.where(hm, qa, 0.0).astype(BF16), jnp.where(hm, qb, 0.0).astype(BF16)))
        return out

    qh = head_queries(q1_ref, q2_ref)
    qh_next = head_queries(q1n_ref, q2n_ref)
    causal = (lax.broadcasted_iota(I32, (tq, tq), 0) <= lax.broadcasted_iota(I32, (tq, tq), 1))

    m_sc[...] = jnp.full(m_sc.shape, NEG_INF, F32)
    acc_sc[...] = jnp.zeros(acc_sc.shape, F32)

    n_chain = 2 * DIFF_HEADS

    def scores(ch, tile, queries):
        start = pl.multiple_of(tile * tq, tq)
        k_ref = k1_ref if ch % 2 == 0 else k2_ref
        return _dot_nt(k_ref[0, pl.ds(start, tq), :], queries[ch // 2][ch % 2])

    @pl.when(qi == 0)
    def _():
        for ch in range(n_chain):
            s_sc[ch] = scores(ch, 0, qh)

    def step(j, last):
        start = pl.multiple_of(j * tq, tq)
        for ch in range(n_chain):
            st = s_sc[ch]
            s_sc[ch] = scores(ch, 0, qh_next) if last else scores(ch, j + 1, qh)
            h = ch // 2
            vt = vt_ref[0, h * VT_ROWS:(h + 1) * VT_ROWS, pl.ds(start, tq)]
            if last:
                st = jnp.where(causal, st, NEG_INF)
            m_old = m_sc[ch]
            m_new = jnp.maximum(m_old, jnp.max(st, axis=0, keepdims=True))
            p = jnp.exp2(st - m_new)
            al = jnp.exp2(m_old - m_new)
            acc_sc[ch] = al * acc_sc[ch] + _dot(vt, p.astype(BF16))
            m_sc[ch] = m_new

    def body(j, carry):
        step(j, False)
        return carry

    lax.fori_loop(0, qi, body, 0)
    step(qi, True)

    g = g_ref[...]
    outs = []
    for h in range(DIFF_HEADS):
        a1, a2 = acc_sc[2 * h], acc_sc[2 * h + 1]
        o = (a1[:HEAD_DIM] / a1[HEAD_DIM:HEAD_DIM + 1]
             - lam * (a2[:HEAD_DIM] / a2[HEAD_DIM:HEAD_DIM + 1]))
        ms = jnp.mean(o * o, axis=0, keepdims=True)
        outs.append((o * lax.rsqrt(ms + NORM_EPS)) * g * (1.0 - lambda_init))
    o_ref[0] = jnp.concatenate(outs, axis=0).T.astype(BF16)


def _diff_attn_call(proj, vat, lam, g_cols, lambda_init, layer):
    b, s, _ = proj.shape
    tq = TQ
    qspec = lambda cb: pl.BlockSpec((1, tq, LANES), lambda bi, i, cb=cb: (bi, i, cb))
    last_q = s // tq - 1
    qnext = lambda cb: pl.BlockSpec((1, tq, LANES), lambda bi, i, cb=cb: (bi, jnp.minimum(i + 1, last_q), cb))
    kspec = lambda cb: pl.BlockSpec((1, s, LANES), lambda bi, i, cb=cb: (bi, 0, cb))
    n_chain = 2 * DIFF_HEADS
    return pl.pallas_call(
        functools.partial(_diff_attn_kernel, lambda_init=lambda_init, layer=layer),
        grid=(b, s // tq),
        in_specs=[
            pl.BlockSpec(memory_space=pltpu.SMEM),
            qspec(COL_Q1 // LANES), qspec(COL_Q2 // LANES),
            qnext(COL_Q1 // LANES), qnext(COL_Q2 // LANES),
            kspec(COL_K1 // LANES), kspec(COL_K2 // LANES),
            pl.BlockSpec((1, DIFF_HEADS * VT_ROWS, s), lambda bi, i: (bi, 0, 0)),
            pl.BlockSpec((None, HEAD_DIM, 1), lambda bi, i: (layer, 0, 0)),
        ],
        out_specs=pl.BlockSpec((1, tq, 256), lambda bi, i: (bi, i, 0)),
        out_shape=jax.ShapeDtypeStruct((b, s, 256), BF16),
        scratch_shapes=[
            pltpu.VMEM((n_chain, 1, tq), F32),
            pltpu.VMEM((n_chain, VT_ROWS, tq), F32),
            pltpu.VMEM((n_chain, tq, tq), F32),
        ],
        compiler_params=pltpu.CompilerParams(vmem_limit_bytes=48 * 1024 * 1024),
        name="diff_attn",
    )(lam, proj, proj, proj, proj, proj, proj, vat, g_cols)


def _band_kernel(*refs, nqb, nkb, max_dist, has_sink, want_lse, sink_base):
    it = iter(refs)
    sink_ref = next(it) if has_sink else None
    q_ref, kp_ref, kc_ref, vp_ref, vc_ref = (next(it) for _ in range(5))
    o_ref = next(it)
    lse_ref = next(it) if want_lse else None
    kbuf, vbuf = next(it), next(it)
    nseq, rows = q_ref.shape[0], q_ref.shape[1]
    i = pl.program_id(1)
    for sq in range(nseq):
        kbuf[sq, 0:BAND, :] = kp_ref[sq]
        kbuf[sq, BAND:, :] = kc_ref[sq]
        for kb in range(nkb):
            vbuf[sq, 0:BAND, kb * 256:kb * 256 + LANES] = vp_ref[sq, :, kb * LANES:(kb + 1) * LANES]
            vbuf[sq, BAND:, kb * 256:kb * 256 + LANES] = vc_ref[sq, :, kb * LANES:(kb + 1) * LANES]
            vbuf[sq, :, kb * 256 + LANES:(kb + 1) * 256] = jnp.ones((BAND + rows, LANES), BF16)
    lane = lax.broadcasted_iota(I32, (1, LANES), 1)
    lo_half = lane < 64
    r_io = lax.broadcasted_iota(I32, (BAND, 2 * BAND), 0)
    c_io = lax.broadcasted_iota(I32, (BAND, 2 * BAND), 1)
    dist = BAND + r_io - c_io
    band = (dist >= 0) & (dist <= max_dist)
    band_first = band & ((c_io >= BAND) | (i > 0))
    col0 = lax.broadcasted_iota(I32, (1, 2 * BAND), 1) == 0
    vr = lax.broadcasted_iota(I32, (2 * BAND, 2 * LANES), 0)
    vc = lax.broadcasted_iota(I32, (2 * BAND, 2 * LANES), 1)
    sink_row = (vr == 0) & (vc < LANES)
    to_log2 = HEAD_DIM ** -0.5 * math.log2(math.e)
    units = [(sq, sb, qb) for sq in range(nseq) for sb in range(rows // BAND) for qb in range(nqb)]

    def scores(u):
        sq, sb, qb = units[u]
        kb = qb if nkb > 1 else 0
        q = q_ref[sq, sb * BAND:(sb + 1) * BAND, qb * LANES:(qb + 1) * LANES].astype(F32) * to_log2
        q2 = jnp.concatenate([jnp.where(lo_half, q, 0.0), jnp.where(lo_half, 0.0, q)], axis=0).astype(BF16)
        return _dot_nt(q2, kbuf[sq, sb * BAND:(sb + 2) * BAND, kb * LANES:(kb + 1) * LANES])

    ahead = 2
    pending = [scores(u) for u in range(min(ahead, len(units)))]
    for u, (sq, sb, qb) in enumerate(units):
        if u + ahead < len(units):
            pending.append(scores(u + ahead))
        s2 = pending[u]
        pending[u] = None
        kb = qb if nkb > 1 else 0
        r0 = sb * BAND
        msk = band_first if sb == 0 else band
        halves = []
        for hh in range(2):
            if has_sink:
                fill = jnp.where(col0, sink_ref[sink_base + qb * 2 + hh] * math.log2(math.e), NEG_INF)
            else:
                fill = NEG_INF
            halves.append(jnp.where(msk, s2[hh * BAND:(hh + 1) * BAND], fill))
        s2 = jnp.concatenate(halves, axis=0)
        m = jnp.max(s2, axis=1, keepdims=True)
        p = jnp.exp2(s2 - m).astype(BF16)
        vw = vbuf[sq, r0:r0 + 2 * BAND, kb * 256:(kb + 1) * 256]
        if has_sink:
            vw = jnp.where(sink_row, jnp.zeros_like(vw), vw)
        pv = _dot(p, vw)
        den = pv[:, LANES:]
        out = pv[:, :LANES] / den
        o = jnp.where(lo_half, out[:BAND], out[BAND:])
        o_ref[sq, r0:r0 + BAND, qb * LANES:(qb + 1) * LANES] = o.astype(BF16)
        if want_lse:
            lse2 = m + jnp.log2(den)
            ls = jnp.where(lo_half, lse2[:BAND], lse2[BAND:])
            hi = ls.astype(BF16)
            lo = (ls - hi.astype(F32)).astype(BF16)
            lse_ref[sq, r0:r0 + BAND, qb * LANES:(qb + 1) * LANES] = hi
            lse_ref[sq, r0:r0 + BAND, (nqb + qb) * LANES:(nqb + qb + 1) * LANES] = lo


def _band_call(arr, q_col, k_col, v_col, nqb, nkb, max_dist, sinks=None, sink_base=0, want_lse=False,
               name="band"):
    ns, length, _ = arr.shape
    assert sinks is None or max_dist < BAND
    rows = min(BAND_ROWS, length)
    assert length % rows == 0 and BAND_ROWS % rows == 0 and ns % (BAND_ROWS // rows) == 0
    nseq = BAND_ROWS // rows
    wq, wk = nqb * LANES, nkb * LANES
    rpb = rows // BAND
    cur = lambda col, w: pl.BlockSpec((nseq, rows, w), lambda n, i, c=col // w: (n, i, c))
    prev = lambda col, w: pl.BlockSpec(
        (nseq, BAND, w), lambda n, i, c=col // w: (n, jnp.maximum(i * rpb - 1, 0), c))
    in_specs = [cur(q_col, wq), prev(k_col, wk), cur(k_col, wk), prev(v_col, wk), cur(v_col, wk)]
    args = [arr] * 5
    if sinks is not None:
        in_specs = [pl.BlockSpec(memory_space=pltpu.SMEM)] + in_specs
        args = [sinks] + args
    out_specs = [pl.BlockSpec((nseq, rows, wq), lambda n, i: (n, i, 0))]
    out_shape = [jax.ShapeDtypeStruct((ns, length, wq), BF16)]
    if want_lse:
        out_specs.append(pl.BlockSpec((nseq, rows, 2 * wq), lambda n, i: (n, i, 0)))
        out_shape.append(jax.ShapeDtypeStruct((ns, length, 2 * wq), BF16))
    return pl.pallas_call(
        functools.partial(_band_kernel, nqb=nqb, nkb=nkb, max_dist=max_dist,
                          has_sink=sinks is not None, want_lse=want_lse, sink_base=sink_base),
        grid=(ns // nseq, length // rows),
        in_specs=in_specs,
        out_specs=out_specs,
        out_shape=out_shape,
        scratch_shapes=[pltpu.VMEM((nseq, BAND + rows, wk), BF16), pltpu.VMEM((nseq, BAND + rows, 2 * wk), BF16)],
        name=name,
    )(*args)


def _outproj_router_kernel(oa_ref, ob_ref, o1_ref, l1_ref, o4_ref, l4_ref, o16_ref, l16_ref, p4t_ref, p16t_ref,
                           w_ref, x_ref, g1_ref, sc_ref, sh_ref, g_ref, wr_ref, br_ref, tri_ref,
                           xo_ref, hs_ref, meta_ref, nch_ref, wb):
    @pl.when((pl.program_id(0) == 0) & (pl.program_id(1) == 0))
    def _():
        wb[0:256, :] = w_ref[0:256, :].astype(BF16)
        for pos, head in enumerate(_SWA_HEAD_ORDER):
            wb[256 + pos * HEAD_DIM:256 + (pos + 1) * HEAD_DIM, :] = (
                w_ref[256 + head * HEAD_DIM:256 + (head + 1) * HEAD_DIM, :].astype(BF16))
        wb[768:1024, :] = w_ref[768:1024, :].astype(BF16)

    hw = o1_ref.shape[2]
    nblk = x_ref.shape[1] // TM

    def lse_of(v):
        return v[:, :hw] + v[:, hw:]

    def unpermute(pt_ref, src_ref, width):
        per = src_ref.shape[2] // nblk
        return jnp.concatenate(
            [_dot(pt_ref[...], src_ref[0, :, k * per:(k + 1) * per, :].reshape(TM, width)) for k in range(nblk)],
            axis=0)

    o1 = o1_ref[0].astype(F32)
    ls1 = lse_of(l1_ref[0].astype(F32))
    o4 = unpermute(p4t_ref, o4_ref, hw)
    ls4 = lse_of(unpermute(p4t_ref, l4_ref, 2 * hw))
    o16 = unpermute(p16t_ref, o16_ref, hw)
    ls16 = lse_of(unpermute(p16t_ref, l16_ref, 2 * hw))
    mx = jnp.maximum(jnp.maximum(ls1, ls4), ls16)
    e1, e4, e16 = jnp.exp2(ls1 - mx), jnp.exp2(ls4 - mx), jnp.exp2(ls16 - mx)
    oc = (e1 * o1 + e4 * o4 + e16 * o16) / (e1 + e4 + e16)
    mix = (_dot(oa_ref[0], wb[0:256, :]) + _dot(ob_ref[0], wb[256:768, :])
           + _dot(oc.astype(BF16), wb[768:1024, :]))
    x1 = x_ref[0] + _batch_row(g1_ref) * mix
    xo_ref[0] = x1
    _router_body(x1, sc_ref, sh_ref, g_ref, wr_ref, br_ref, tri_ref, hs_ref, meta_ref, nch_ref)


def _outproj_router_call(oa, ob, o1, l1, o4, l4, o16, l16, p4t, p16t, w_out, layer, x, mod, g, wr, br, tri):
    b, s, d = x.shape
    rows = ROWS
    nblk = rows // TM
    steps = s // rows
    nt = b * s // TM
    row = lambda w_: pl.BlockSpec((1, rows, w_), lambda bi, i: (bi, i, 0))
    res = lambda dd, w_: pl.BlockSpec((1, dd, rows // dd, w_), lambda bi, i: (bi, 0, i, 0))
    const2 = lambda bi, i: (0, 0)
    flat = lambda bi, i: bi * steps + i
    return pl.pallas_call(
        _outproj_router_kernel,
        grid=(b, steps),
        in_specs=[
            row(256), row(512), row(256), row(512),
            res(4, 256), res(4, 512), res(16, 256), res(16, 512),
            _const_spec((TM, TM), const2), _const_spec((TM, TM), const2),
            _const_spec((None, d, d), lambda bi, i: (layer, 0, 0)),
            row(d),
            _mod_spec(mod, layer, 2),
            _mod_spec(mod, layer, 4),
            _mod_spec(mod, layer, 3),
            pl.BlockSpec((None, 1, d), lambda bi, i: (layer, 0, 0)),
            _const_spec((None, 2 * ROUTER_COLS, d), lambda bi, i: (layer, 0, 0)),
            pl.BlockSpec((None, ROUTER_COLS, 1), lambda bi, i: (layer, 0, 0)),
            _const_spec((TM, TM), const2),
        ],
        out_specs=[
            row(d),
            pl.BlockSpec((nblk * SLOTS, d), lambda bi, i: (flat(bi, i), 0)),
            pl.BlockSpec((nblk, 8, TM), lambda bi, i: (flat(bi, i), 0, 0)),
            pl.BlockSpec((nblk, N_EXPERTS, LANES), lambda bi, i: (flat(bi, i), 0, 0)),
        ],
        out_shape=[
            jax.ShapeDtypeStruct((b, s, d), F32),
            jax.ShapeDtypeStruct((nt * SLOTS, d), BF16),
            jax.ShapeDtypeStruct((nt, 8, TM), F32),
            jax.ShapeDtypeStruct((nt, N_EXPERTS, LANES), I32),
        ],
        scratch_shapes=[pltpu.VMEM((d, d), BF16)],
        compiler_params=pltpu.CompilerParams(vmem_limit_bytes=56 * 1024 * 1024),
        name="out_proj_router",
    )(oa, ob, o1, l1, o4, l4, o16, l16, p4t, p16t, w_out, x, mod, mod, mod, g, wr, br, tri)


ROUTER_COLS = LANES


def _router_body(x, sc_ref, sh_ref, g_ref, wr_ref, br_ref, tri_ref, hs_ref, meta_ref, nch_ref):
    tm = TM
    h = _modulated_norm(x, g_ref[...], _batch_row(sc_ref), _batch_row(sh_ref))
    hb = h.astype(BF16)
    h_lo = (h - hb.astype(F32)).astype(BF16)
    part = _dot_nt(wr_ref[...], hb) + _dot_nt(wr_ref[...], h_lo)
    logits_t = part[:ROUTER_COLS] + part[ROUTER_COLS:] + br_ref[...]
    r8 = lax.broadcasted_iota(I32, (8, tm), 0)
    r16 = lax.broadcasted_iota(I32, (N_EXPERTS, tm), 0)
    rl = lax.broadcasted_iota(I32, (N_EXPERTS, LANES), 0)
    slot = lax.broadcasted_iota(I32, (SLOTS, tm), 0)
    for blk in range(x.shape[0] // tm):
        lt = logits_t[:, blk * tm:(blk + 1) * tm]
        glog = lt[0:8]
        elog = lt[8:8 + N_EXPERTS]

        gmax = jnp.max(glog, axis=0, keepdims=True)
        g_w = 1.0 / jnp.sum(jnp.exp(glog - gmax), axis=0, keepdims=True)
        g_idx = jnp.min(jnp.where(glog == gmax, r8, 99), axis=0, keepdims=True)

        el = jnp.where((r16 // EXPERTS_PER_GROUP) == g_idx, elog, NEG_INF)
        emax = jnp.max(el, axis=0, keepdims=True)
        e1 = jnp.min(jnp.where(el == emax, r16, 99), axis=0, keepdims=True)
        el2 = jnp.where(r16 == e1, NEG_INF, el)
        emax2 = jnp.max(el2, axis=0, keepdims=True)
        e2 = jnp.min(jnp.where(el2 == emax2, r16, 99), axis=0, keepdims=True)
        p2 = jnp.exp(emax2 - emax)
        wt1 = g_w / (1.0 + p2)
        wt2 = g_w * p2 / (1.0 + p2)

        oh1 = r16 == e1
        oh2 = r16 == e2
        onehot = jnp.where(oh1, 1.0, 0.0) + jnp.where(oh2, 1.0, 0.0)
        cnt = jnp.sum(onehot, axis=1, keepdims=True)
        nch = jnp.floor((cnt + (CHUNK - 1)) * (1.0 / CHUNK))
        nchb = jnp.broadcast_to(nch, (N_EXPERTS, LANES))
        incl = nchb
        for sft in (1, 2, 4, 8):
            incl = incl + jnp.where(rl >= sft, pltpu.roll(incl, sft, 0), 0.0)
        off = (incl - nchb)[:, 0:1] * float(CHUNK)
        rank = _dot(onehot.astype(BF16), tri_ref[...])
        slot_of = off + rank
        pos1 = jnp.sum(jnp.where(oh1, slot_of, 0.0), axis=0, keepdims=True)
        pos2 = jnp.sum(jnp.where(oh2, slot_of, 0.0), axis=0, keepdims=True)

        sel = jnp.where(slot == pos1.astype(I32), 1.0, jnp.where(slot == pos2.astype(I32), 1.0, 0.0))
        hs_ref[blk * SLOTS:(blk + 1) * SLOTS, :] = _dot(sel.astype(BF16), hb[blk * tm:(blk + 1) * tm]).astype(BF16)

        meta_ref[blk] = jnp.concatenate([pos1, pos2, wt1, wt2, jnp.zeros((4, tm), F32)], axis=0)
        nch_ref[blk] = nchb.astype(I32)


def _ffn_schedule(nch, max_tiles, dump_base):
    nt = nch.shape[0]
    cend = jnp.cumsum(nch, axis=1)
    coff = cend - nch
    tcum = jnp.cumsum(nch, axis=0)
    before = tcum - nch
    tot = tcum[-1]
    pad = ((tot + FFN_CHUNKS - 1) // FFN_CHUNKS) * FFN_CHUNKS
    eend = jnp.cumsum(pad)
    estart = eend - pad
    n_tiles = (eend[-1] // FFN_CHUNKS).astype(I32)
    first_chunk = jnp.arange(max_tiles, dtype=I32) * FFN_CHUNKS
    tile_expert = jnp.sum((eend[None, :] <= first_chunk[:, None]).astype(I32), axis=1)
    tile_expert = jnp.minimum(tile_expert, N_EXPERTS - 1)
    hp = lax.Precision.HIGHEST
    pos = jnp.arange(max_tiles * FFN_CHUNKS, dtype=I32)
    e_s = jnp.minimum(jnp.sum((eend[None, :] <= (pos // FFN_CHUNKS * FFN_CHUNKS)[:, None]).astype(I32), axis=1),
                      N_EXPERTS - 1)
    oh_e = (e_s[:, None] == jnp.arange(N_EXPERTS, dtype=I32)[None, :]).astype(F32)
    idx = pos - jnp.dot(oh_e, estart.astype(F32), precision=hp).astype(I32)
    run_end = jnp.dot(oh_e, tcum.T.astype(F32), precision=hp).astype(I32)
    run_beg = jnp.dot(oh_e, before.T.astype(F32), precision=hp).astype(I32)
    run_off = jnp.dot(oh_e, coff.T.astype(F32), precision=hp).astype(I32)
    in_run = (idx[:, None] >= run_beg) & (idx[:, None] < run_end)
    tile_base = jnp.arange(nt, dtype=I32)[None, :] * CHUNKS_PER_TILE
    src = jnp.sum(jnp.where(in_run, tile_base + run_off + idx[:, None] - run_beg, 0), axis=1)
    real = jnp.any(in_run, axis=1)
    dump = dump_base + (pos // FFN_CHUNKS % 2) * FFN_CHUNKS + pos % FFN_CHUNKS
    src_rows = jnp.where(real, src, 0) * CHUNK
    dst_rows = jnp.where(real, src, dump) * CHUNK
    used = cend[:, -1]
    ucum = jnp.cumsum(CHUNKS_PER_TILE - used)
    ubeg = ucum - (CHUNKS_PER_TILE - used)
    z = jnp.arange(max_tiles * ZERO_CHUNKS, dtype=I32)[:, None]
    in_gap = (z >= ubeg[None, :]) & (z < ucum[None, :])
    zsrc = jnp.sum(jnp.where(in_gap, tile_base + used[None, :] + z - ubeg[None, :], 0), axis=1)
    zdump = dump_base + 2 * FFN_CHUNKS + z[:, 0] % ZERO_CHUNKS
    zero_rows = jnp.where(jnp.any(in_gap, axis=1), zsrc, zdump) * CHUNK
    return tile_expert, src_rows, dst_rows, n_tiles.reshape(1), zero_rows


def _ffn_kernel(te_ref, sr_ref, dr_ref, nt_ref, zr_ref, hs_hbm, wg_ref, wu_ref, wd_ref, ys_hbm,
                xbuf, ybuf, zbuf, wgb, wub, wdb, in_sem, out_sem, zero_sem, *, dump_base):
    j = pl.program_id(0)
    nt = nt_ref[0]
    half_ff = EXPERT_FF // 2

    def rows_at(r):
        return pl.ds(pl.multiple_of(r, CHUNK), CHUNK)

    def in_copy(step, slot, k, wait=False):
        r = 0 if wait else sr_ref[step * FFN_CHUNKS + k]
        return pltpu.make_async_copy(hs_hbm.at[rows_at(r), :], xbuf.at[slot, pl.ds(k * CHUNK, CHUNK), :],
                                     in_sem.at[slot])

    def out_copy(step, slot, k, wait=False):
        r = 0 if wait else dr_ref[step * FFN_CHUNKS + k]
        return pltpu.make_async_copy(ybuf.at[slot, pl.ds(k * CHUNK, CHUNK), :], ys_hbm.at[rows_at(r), :],
                                     out_sem.at[slot])

    def zero_copy(k, wait=False):
        r = 0 if wait else zr_ref[j * ZERO_CHUNKS + k]
        return pltpu.make_async_copy(zbuf.at[pl.ds(k * CHUNK, CHUNK), :], ys_hbm.at[rows_at(r), :], zero_sem)

    @pl.when(j == 0)
    def _():
        zbuf[...] = jnp.zeros_like(zbuf)
        fills = [pltpu.make_async_copy(
            zbuf, ys_hbm.at[pl.ds((dump_base + r * ZERO_CHUNKS) * CHUNK, ZERO_CHUNKS * CHUNK), :], zero_sem)
            for r in range(DUMP_CHUNKS // ZERO_CHUNKS)]
        for cp in fills:
            cp.start()
        for cp in fills:
            cp.wait()

    for k in range(ZERO_CHUNKS):
        zero_copy(k).start()

    @pl.when(j < nt)
    def _():
        slot = j % 2

        @pl.when(j == 0)
        def _():
            for k in range(FFN_CHUNKS):
                in_copy(0, 0, k).start()

        @pl.when(j + 1 < nt)
        def _():
            for k in range(FFN_CHUNKS):
                in_copy(j + 1, 1 - slot, k).start()

        @pl.when((j == 0) | (te_ref[j] != te_ref[jnp.maximum(j - 1, 0)]))
        def _():
            wgb[...] = wg_ref[0].astype(BF16)
            wub[...] = wu_ref[0].astype(BF16)
            wdb[...] = wd_ref[0].astype(BF16)

        for k in range(FFN_CHUNKS):
            in_copy(j, slot, k, wait=True).wait()

        @pl.when(j >= 2)
        def _():
            for k in range(FFN_CHUNKS):
                out_copy(j - 2, slot, k, wait=True).wait()

        x = xbuf[slot]
        hg = [_dot(x, wgb[:, h * half_ff:(h + 1) * half_ff]) for h in range(2)]
        hu = [_dot(x, wub[:, h * half_ff:(h + 1) * half_ff]) for h in range(2)]
        y = None
        for h in range(2):
            act = ((hg[h] / (1.0 + jnp.exp(-hg[h]))) * hu[h]).astype(BF16)
            part = _dot(act, wdb[h * half_ff:(h + 1) * half_ff, :])
            y = part if y is None else y + part
        ybuf[slot] = y.astype(BF16)
        for k in range(FFN_CHUNKS):
            out_copy(j, slot, k).start()

        @pl.when(j == nt - 1)
        def _():
            for k in range(FFN_CHUNKS):
                out_copy(j, slot, k, wait=True).wait()

            @pl.when(j >= 1)
            def _():
                for k in range(FFN_CHUNKS):
                    out_copy(j - 1, 1 - slot, k, wait=True).wait()

    for k in range(ZERO_CHUNKS):
        zero_copy(k, wait=True).wait()


def _ffn_call(tile_expert, src_rows, dst_rows, n_tiles, zero_rows, hs, wg, wu, wd, layer, max_tiles):
    rows, d = hs.shape
    ff = wg.shape[-1]
    wmap = lambda j, te, sr, dr, nt, zr: (layer, te[j], 0, 0)
    grid_spec = pltpu.PrefetchScalarGridSpec(
        num_scalar_prefetch=5,
        grid=(max_tiles,),
        in_specs=[
            pl.BlockSpec(memory_space=pl.ANY),
            pl.BlockSpec((None, 1, d, ff), wmap),
            pl.BlockSpec((None, 1, d, ff), wmap),
            pl.BlockSpec((None, 1, ff, d), wmap),
        ],
        out_specs=pl.BlockSpec(memory_space=pl.ANY),
        scratch_shapes=[
            pltpu.VMEM((2, FFN_ROWS, d), BF16),
            pltpu.VMEM((2, FFN_ROWS, d), BF16),
            pltpu.VMEM((ZERO_CHUNKS * CHUNK, d), BF16),
            pltpu.VMEM((d, ff), BF16),
            pltpu.VMEM((d, ff), BF16),
            pltpu.VMEM((ff, d), BF16),
            pltpu.SemaphoreType.DMA((2,)),
            pltpu.SemaphoreType.DMA((2,)),
            pltpu.SemaphoreType.DMA(()),
        ],
    )
    return pl.pallas_call(
        functools.partial(_ffn_kernel, dump_base=rows // CHUNK),
        grid_spec=grid_spec,
        out_shape=jax.ShapeDtypeStruct((rows + DUMP_CHUNKS * CHUNK, d), BF16),
        compiler_params=pltpu.CompilerParams(vmem_limit_bytes=48 * 1024 * 1024),
        name="expert_ffn",
    )(tile_expert, src_rows, dst_rows, n_tiles, zero_rows, hs, wg, wu, wd)


def _combine_body(ys_ref, meta_ref, x_ref, g2_ref):
    tm = TM
    eye = (lax.broadcasted_iota(I32, (tm, tm), 0) == lax.broadcasted_iota(I32, (tm, tm), 1))
    slot = lax.broadcasted_iota(I32, (tm, SLOTS), 1).astype(F32)

    def as_col(row):
        return jnp.sum(jnp.where(eye, row, 0.0), axis=1, keepdims=True)

    ys = []
    for blk in range(x_ref.shape[1] // tm):
        meta = meta_ref[blk]
        pos1, pos2 = as_col(meta[0:1]), as_col(meta[1:2])
        w1, w2 = as_col(meta[2:3]), as_col(meta[3:4])
        gate = jnp.where(slot == pos1, w1, 0.0) + jnp.where(slot == pos2, w2, 0.0)
        ys.append(_dot(gate.astype(BF16), ys_ref[blk * SLOTS:(blk + 1) * SLOTS, :]))
    return x_ref[0] + _batch_row(g2_ref) * jnp.concatenate(ys, axis=0)


def _combine_final_kernel(ys_ref, meta_ref, x_ref, g2_ref, gf_ref, xo_ref):
    xo = _combine_body(ys_ref, meta_ref, x_ref, g2_ref)
    xo_ref[0] = xo * lax.rsqrt(jnp.mean(xo * xo, axis=-1, keepdims=True) + NORM_EPS) * gf_ref[...]


def _combine_inproj_kernel(ys_ref, meta_ref, x_ref, g2_ref, *refs):
    n_in = 10
    xo_ref = refs[n_in]
    xo = _combine_body(ys_ref, meta_ref, x_ref, g2_ref)
    xo_ref[0] = xo
    _inproj_body(xo, *refs[:n_in], *refs[n_in + 1:])


def _combine_specs(d, steps, mod, layer):
    nblk = ROWS // TM
    flat = lambda bi, i: bi * steps + i
    return [
        pl.BlockSpec((nblk * SLOTS, d), lambda bi, i: (flat(bi, i), 0)),
        pl.BlockSpec((nblk, 8, TM), lambda bi, i: (flat(bi, i), 0, 0)),
        pl.BlockSpec((1, ROWS, d), lambda bi, i: (bi, i, 0)),
        _mod_spec(mod, layer, 5),
    ]


def _combine_final_call(ys, meta, x, mod, layer, gf):
    b, s, d = x.shape
    return pl.pallas_call(
        _combine_final_kernel,
        grid=(b, s // ROWS),
        in_specs=_combine_specs(d, s // ROWS, mod, layer) + [pl.BlockSpec((1, d), lambda bi, i: (0, 0))],
        out_specs=pl.BlockSpec((1, ROWS, d), lambda bi, i: (bi, i, 0)),
        out_shape=jax.ShapeDtypeStruct((b, s, d), F32),
        compiler_params=pltpu.CompilerParams(vmem_limit_bytes=48 * 1024 * 1024),
        name="moe_combine",
    )(ys, meta, x, mod, gf)


def _combine_inproj_call(ys, meta, x, mod, g, w_in, layer, tabs, p4, p16):
    b, s, d = x.shape
    in_specs, out_specs, out_shape = _inproj_specs(b, s, d, layer, mod)
    xspec = pl.BlockSpec((1, ROWS, d), lambda bi, i: (bi, i, 0))
    return pl.pallas_call(
        _combine_inproj_kernel,
        grid=(b, s // ROWS),
        in_specs=_combine_specs(d, s // ROWS, mod, layer - 1) + in_specs,
        out_specs=[xspec] + out_specs,
        out_shape=[jax.ShapeDtypeStruct((b, s, d), F32)] + out_shape,
        scratch_shapes=[pltpu.VMEM((d, IN_WIDTH), BF16)],
        compiler_params=pltpu.CompilerParams(vmem_limit_bytes=56 * 1024 * 1024),
        name="combine_in_proj",
    )(ys, meta, x, mod, mod, mod, g, w_in, *tabs, p4, p16)


def _rope_tables(positions):
    pos = positions.astype(F32)[..., None]

    def table(dim):
        inv = ROPE_THETA ** (-jnp.arange(0, dim, 2, dtype=F32) / dim)
        ang = pos * inv
        cos, sin = jnp.cos(ang), jnp.sin(ang)
        reps = LANES // dim
        return (jnp.tile(jnp.concatenate([cos, cos], -1), (1, 1, reps)),
                jnp.tile(jnp.concatenate([-sin, sin], -1), (1, 1, reps)))

    c64, s64 = table(HEAD_DIM)
    c32, s32 = table(DIFF_QK_DIM)
    return c64, s64, c32, s32


def kernel(x, c, positions, ada_w, ada_b, norm_mix_g, norm_ffn_g, w_in, w_out, diff_lambda_q1, diff_lambda_k1,
           diff_lambda_q2, diff_lambda_k2, diff_subln_g, swa_sinks, router_group_w, router_group_b,
           router_expert_w, router_expert_b, expert_w_gate, expert_w_up, expert_w_down, final_norm_g):
    b, s, d = x.shape
    depth = ada_w.shape[0]
    assert b <= BF16_ROWS and s % TQ == 0 and s % ROWS == 0 and s % (16 * BAND) == 0 and d == 8 * LANES
    n = b * s
    nt = n // TM
    max_tiles = (nt * CHUNKS_PER_TILE + N_EXPERTS * (FFN_CHUNKS - 1)) // FFN_CHUNKS + 1

    tabs = _rope_tables(positions)
    p4 = _residue_perm(TM, 4)
    p16 = _residue_perm(TM, 16)
    p4_b, p16_b = jnp.asarray(p4, BF16), jnp.asarray(p16, BF16)
    p4t_b, p16t_b = jnp.asarray(p4.T, BF16), jnp.asarray(p16.T, BF16)
    tri = jnp.asarray(np.triu(np.ones((TM, TM), np.float32), 1), BF16)

    c_pad = jnp.pad(c, ((0, BF16_ROWS - b), (0, 0)))
    mod = _ada_call(c_pad, ada_w, ada_b)

    sink_order = np.asarray(_SWA_HEAD_ORDER, np.int32)
    zpad = lambda k: jnp.zeros((depth, d, k), F32)
    wr = jnp.concatenate([router_group_w, zpad(8 - N_GROUPS), router_expert_w,
                          zpad(ROUTER_COLS - 8 - N_EXPERTS)], axis=-1)
    wr_hi = wr.astype(BF16)
    wr = jnp.concatenate([wr_hi, (wr - wr_hi.astype(F32)).astype(BF16)], axis=-1)
    wr = jnp.swapaxes(wr, 1, 2)
    br = jnp.concatenate([router_group_b, jnp.full((depth, 8 - N_GROUPS), NEG_INF, F32), router_expert_b,
                          jnp.zeros((depth, ROUTER_COLS - 8 - N_EXPERTS), F32)], axis=-1).reshape(depth, -1, 1)

    lam_init = [0.8 - 0.6 * math.exp(-0.3 * l) for l in range(depth)]
    lam = (jnp.exp(jnp.sum(diff_lambda_q1 * diff_lambda_k1, axis=-1))
           - jnp.exp(jnp.sum(diff_lambda_q2 * diff_lambda_k2, axis=-1)) + jnp.asarray(lam_init, F32))
    g_cols = diff_subln_g[:, :, None]
    sinks = swa_sinks[:, sink_order].reshape(-1)
    g_mix = norm_mix_g.reshape(depth, 1, d)
    g_ffn = norm_ffn_g.reshape(depth, 1, d)

    proj, vat, qkv4, qkv16 = _inproj_call(x, mod, g_mix, w_in, 0, tabs, p4_b, p16_b)
    for l in range(depth):
        oa = _diff_attn_call(proj, vat, lam, g_cols, lam_init[l], l)
        ob = _band_call(proj, COL_QB, COL_KB, COL_VB, nqb=4, nkb=1, max_dist=SWA_WINDOW - 1,
                        sinks=sinks, sink_base=l * SWA_Q_HEADS, name="swa")[0]
        o1, l1 = _band_call(proj, COL_QC, COL_KC, COL_VC, nqb=2, nkb=2, max_dist=BAND, want_lse=True,
                            name="dil1")
        o4, l4 = _band_call(qkv4.reshape(b * 4, s // 4, 768), 0, 256, 512, nqb=2, nkb=2, max_dist=BAND,
                            want_lse=True, name="dil4")
        o16, l16 = _band_call(qkv16.reshape(b * 16, s // 16, 768), 0, 256, 512, nqb=2, nkb=2, max_dist=BAND,
                              want_lse=True, name="dil16")
        x, hs, meta, nch = _outproj_router_call(
            oa, ob, o1, l1, o4.reshape(b, 4, s // 4, 256), l4.reshape(b, 4, s // 4, 512),
            o16.reshape(b, 16, s // 16, 256), l16.reshape(b, 16, s // 16, 512), p4t_b, p16t_b, w_out, l, x, mod,
            g_ffn, wr, br, tri)
        sched = _ffn_schedule(nch[:, :, 0], max_tiles, nt * CHUNKS_PER_TILE)
        ys = _ffn_call(*sched, hs, expert_w_gate, expert_w_up, expert_w_down, l, max_tiles)
        if l + 1 < depth:
            x, proj, vat, qkv4, qkv16 = _combine_inproj_call(ys, meta, x, mod, g_mix, w_in, l + 1, tabs,
                                                             p4_b, p16_b)
        else:
            x = _combine_final_call(ys, meta, x, mod, l, final_norm_g.reshape(1, d))
    return x
```

```python
import functools
import math

import numpy as np
import jax
import jax.numpy as jnp
from jax import lax
from jax.experimental import pallas as pl
from jax.experimental.pallas import tpu as pltpu

F32 = jnp.float32
BF16 = jnp.bfloat16
I32 = jnp.int32

HEAD_DIM = 64
ROPE_THETA = 10000.0
NORM_EPS = 1e-6
NEG_INF = -1e30
DIFF_HEADS = 4
DIFF_QK_DIM = 32
SWA_Q_HEADS = 8
SWA_KV_HEADS = 2
SWA_WINDOW = 128
DIL_PATTERNS = ((128, 1), (512, 4), (2048, 16))
N_GROUPS = 4
EXPERTS_PER_GROUP = 4
N_EXPERTS = 16
EXPERT_FF = 512
N_ADA = 6
IN_WIDTH = 2304

LANES = 128
BF16_ROWS = 16
BAND = 128

TM = 256
ROWS = 2 * TM
CHUNK = BF16_ROWS
SLOTS = 2 * TM + N_EXPERTS * CHUNK
CHUNKS_PER_TILE = SLOTS // CHUNK
FFN_ROWS = 1024
FFN_CHUNKS = FFN_ROWS // CHUNK
ZERO_CHUNKS = -(-N_EXPERTS * FFN_CHUNKS // CHUNKS_PER_TILE)
DUMP_CHUNKS = -(-(2 * FFN_CHUNKS + ZERO_CHUNKS) // ZERO_CHUNKS) * ZERO_CHUNKS
TQ = 512
BAND_ROWS = 2048

_SWA_HEAD_ORDER = (0, 4, 1, 5, 2, 6, 3, 7)
COL_QB, COL_KB, COL_VB = 0, 512, 640
COL_Q1, COL_Q2, COL_K1, COL_K2, COL_VA = 768, 896, 1024, 1152, 1280
COL_QC, COL_KC, COL_VC = 1536, 1792, 2048
_SRC_ROPE = (32, 32, 32, 32, 0, 0, 64, 64, 64, 64, 64, 0, 64, 64, 64, 64, 0, 0)
_SRC_DEST = (6, 7, 8, 9, 10, 11, None, None, None, None, 4, 5, 12, 13, 14, 15, 16, 17)
VT_ROWS = HEAD_DIM + BF16_ROWS
_SRC_VA_CHUNK = 2


def _residue_perm(tm, d):
    p = np.zeros((tm, tm), np.float32)
    per = tm // d
    for l in range(per):
        for r in range(d):
            p[r * per + l, l * d + r] = 1.0
    return p


def _dot(a, b, **kw):
    return jnp.dot(a, b, preferred_element_type=F32, **kw)


def _dot_nt(a, b):
    return lax.dot_general(a, b, (((1,), (1,)), ((), ())), preferred_element_type=F32)


def _batch_row(ref):
    return ref[pl.ds(pl.program_id(0), 1), :]


def _modulated_norm(x, g, sc, sh):
    y = x * lax.rsqrt(jnp.mean(x * x, axis=-1, keepdims=True) + NORM_EPS)
    return (y * g) * (1.0 + sc) + sh


def _ada_kernel(c_ref, w_ref, b_ref, o_ref):
    c = c_ref[...]
    ca = c / (1.0 + jnp.exp(-c))
    rows = ca.shape[0]
    c_hi = ca.astype(BF16)
    c_lo = (ca - c_hi.astype(F32)).astype(BF16)
    w = w_ref[0]
    w_hi = w.astype(BF16)
    w_lo = (w - w_hi.astype(F32)).astype(BF16)
    main = _dot(jnp.concatenate([c_hi, c_lo], axis=0), w_hi)
    o_ref[0] = main[:rows] + main[rows:] + _dot(c_hi, w_lo) + b_ref[0]


def _ada_call(c_pad, ada_w, ada_b):
    depth, d, n = ada_w.shape
    tn = n // 2
    return pl.pallas_call(
        _ada_kernel,
        grid=(depth, n // tn),
        in_specs=[
            pl.BlockSpec((c_pad.shape[0], d), lambda l, j: (0, 0)),
            pl.BlockSpec((1, d, tn), lambda l, j: (l, 0, j)),
            pl.BlockSpec((1, 1, tn), lambda l, j: (l, 0, j)),
        ],
        out_specs=pl.BlockSpec((1, c_pad.shape[0], tn), lambda l, j: (l, 0, j)),
        out_shape=jax.ShapeDtypeStruct((depth, c_pad.shape[0], n), F32),
        compiler_params=pltpu.CompilerParams(vmem_limit_bytes=40 * 1024 * 1024),
        name="ada_mod",
    )(c_pad, ada_w, ada_b.reshape(depth, 1, n))


def _rope(t, cos, sin_signed, first_half, half):
    rot = jnp.where(first_half, pltpu.roll(t, LANES - half, 1), pltpu.roll(t, half, 1))
    return t * cos + rot * sin_signed


def _inproj_body(x, sc_ref, sh_ref, g_ref, w_ref, cs64_ref, sn64_ref, cs32_ref, sn32_ref,
                 p4_ref, p16_ref, proj_ref, vat_ref, c4_ref, c16_ref, wb):
    @pl.when((pl.program_id(0) == 0) & (pl.program_id(1) == 0))
    def _():
        wb[...] = w_ref[...].astype(BF16)

    h = _modulated_norm(x, g_ref[...], _batch_row(sc_ref), _batch_row(sh_ref))
    hb = h.astype(BF16)
    lane = lax.broadcasted_iota(I32, (1, LANES), 1)
    first64 = (lane % 64) < 32
    first32 = (lane % 32) < 16
    lo_half = lane < 64
    swa_q = []
    for cb in range(IN_WIDTH // 256):
        if cb == IN_WIDTH // 256 - 1:
            half = hb.shape[0] // 2
            acc = jnp.concatenate([_dot(hb[:half], wb[:, cb * 256:(cb + 1) * 256]),
                                   _dot(hb[half:], wb[:, cb * 256:(cb + 1) * 256])], axis=0)
        else:
            acc = _dot(hb, wb[:, cb * 256:(cb + 1) * 256])
        if cb == _SRC_VA_CHUNK:
            acc_t = acc.T.astype(BF16)
            for hd in range(DIFF_HEADS):
                vat_ref[0, hd * VT_ROWS:hd * VT_ROWS + HEAD_DIM, :] = acc_t[hd * HEAD_DIM:(hd + 1) * HEAD_DIM]
                vat_ref[0, hd * VT_ROWS + HEAD_DIM:(hd + 1) * VT_ROWS, :] = jnp.ones(
                    (BF16_ROWS, acc_t.shape[1]), BF16)
        for half in range(2):
            src = cb * 2 + half
            t = acc[:, half * LANES:(half + 1) * LANES]
            if _SRC_ROPE[src] == 64:
                t = _rope(t, cs64_ref[0], sn64_ref[0], first64, 32)
            elif _SRC_ROPE[src] == 32:
                t = _rope(t, cs32_ref[0], sn32_ref[0], first32, 16)
            dst = _SRC_DEST[src]
            if dst is None:
                swa_q.append(t)
            else:
                proj_ref[0, :, dst * LANES:(dst + 1) * LANES] = t.astype(BF16)
    for jb in range(SWA_Q_HEADS // 2):
        a, c = swa_q[jb // 2], swa_q[2 + jb // 2]
        if jb % 2 == 0:
            blk = jnp.where(lo_half, a, pltpu.roll(c, 64, 1))
        else:
            blk = jnp.where(lo_half, pltpu.roll(a, 64, 1), c)
        proj_ref[0, :, jb * LANES:(jb + 1) * LANES] = blk.astype(BF16)
    tm = TM
    for blk in range(x.shape[0] // tm):
        cc = proj_ref[0, blk * tm:(blk + 1) * tm, COL_QC:]
        c4 = _dot(p4_ref[...], cc).astype(BF16)
        for r in range(4):
            c4_ref[0, r, blk * (tm // 4):(blk + 1) * (tm // 4)] = c4[r * (tm // 4):(r + 1) * (tm // 4)]
        c16 = _dot(p16_ref[...], cc).astype(BF16)
        for r in range(16):
            c16_ref[0, r, blk * (tm // 16):(blk + 1) * (tm // 16)] = c16[r * (tm // 16):(r + 1) * (tm // 16)]


def _inproj_kernel(x_ref, *refs):
    _inproj_body(x_ref[0], *refs)


def _mod_spec(mod, layer, k):
    return pl.BlockSpec((None, mod.shape[1], mod.shape[2] // N_ADA), lambda bi, i: (layer, 0, k))


def _const_spec(shape, index_map):
    return pl.BlockSpec(shape, index_map, pipeline_mode=pl.Buffered(1))


def _inproj_specs(b, s, d, layer, mod):
    rows = ROWS
    row = lambda bi, i: (bi, i, 0)
    const2 = lambda bi, i: (0, 0)
    in_specs = [
        _mod_spec(mod, layer, 1),
        _mod_spec(mod, layer, 0),
        pl.BlockSpec((None, 1, d), lambda bi, i: (layer, 0, 0)),
        _const_spec((None, d, IN_WIDTH), lambda bi, i: (layer, 0, 0)),
        pl.BlockSpec((1, rows, LANES), row),
        pl.BlockSpec((1, rows, LANES), row),
        pl.BlockSpec((1, rows, LANES), row),
        pl.BlockSpec((1, rows, LANES), row),
        _const_spec((TM, TM), const2),
        _const_spec((TM, TM), const2),
    ]
    out_specs = [
        pl.BlockSpec((1, rows, IN_WIDTH), row),
        pl.BlockSpec((1, DIFF_HEADS * VT_ROWS, rows), lambda bi, i: (bi, 0, i)),
        pl.BlockSpec((1, 4, rows // 4, 768), lambda bi, i: (bi, 0, i, 0)),
        pl.BlockSpec((1, 16, rows // 16, 768), lambda bi, i: (bi, 0, i, 0)),
    ]
    out_shape = [
        jax.ShapeDtypeStruct((b, s, IN_WIDTH), BF16),
        jax.ShapeDtypeStruct((b, DIFF_HEADS * VT_ROWS, s), BF16),
        jax.ShapeDtypeStruct((b, 4, s // 4, 768), BF16),
        jax.ShapeDtypeStruct((b, 16, s // 16, 768), BF16),
    ]
    return in_specs, out_specs, out_shape


def _inproj_call(x, mod, g, w_in, layer, tabs, p4, p16):
    b, s, d = x.shape
    in_specs, out_specs, out_shape = _inproj_specs(b, s, d, layer, mod)
    return pl.pallas_call(
        _inproj_kernel,
        grid=(b, s // ROWS),
        in_specs=[pl.BlockSpec((1, ROWS, d), lambda bi, i: (bi, i, 0))] + in_specs,
        out_specs=out_specs,
        out_shape=out_shape,
        scratch_shapes=[pltpu.VMEM((d, IN_WIDTH), BF16)],
        compiler_params=pltpu.CompilerParams(vmem_limit_bytes=56 * 1024 * 1024),
        name="in_proj",
    )(x, mod, mod, g, w_in, *tabs, p4, p16)


def _diff_attn_kernel(lam_ref, q1_ref, q2_ref, q1n_ref, q2n_ref, k1_ref, k2_ref, vt_ref, g_ref, o_ref,
                      m_sc, acc_sc, s_sc, *, lambda_init, layer):
    tq = q1_ref.shape[1]
    qi = pl.program_id(1)
    lam = lam_ref[layer]
    to_log2 = DIFF_QK_DIM ** -0.5 * math.log2(math.e)
    lane = lax.broadcasted_iota(I32, (1, LANES), 1)

    def head_queries(qa_ref, qb_ref):
        qa = qa_ref[0].astype(F32) * to_log2
        qb = qb_ref[0].astype(F32) * to_log2
        out = []
        for h in range(DIFF_HEADS):
            hm = (lane // DIFF_QK_DIM) == h
            out.append((jnp.where(hm, qa, 0.0).astype(BF16), jnp.where(hm, qb, 0.0).astype(BF16)))
        return out

    qh = head_queries(q1_ref, q2_ref)
    qh_next = head_queries(q1n_ref, q2n_ref)
    causal = (lax.broadcasted_iota(I32, (tq, tq), 0) <= lax.broadcasted_iota(I32, (tq, tq), 1))

    m_sc[...] = jnp.full(m_sc.shape, NEG_INF, F32)
    acc_sc[...] = jnp.zeros(acc_sc.shape, F32)

    n_chain = 2 * DIFF_HEADS

    def scores(ch, tile, queries):
        start = pl.multiple_of(tile * tq, tq)
        k_ref = k1_ref if ch % 2 == 0 else k2_ref
        return _dot_nt(k_ref[0, pl.ds(start, tq), :], queries[ch // 2][ch % 2])

    @pl.when(qi == 0)
    def _():
        for ch in range(n_chain):
            s_sc[ch] = scores(ch, 0, qh)

    def step(j, last):
        start = pl.multiple_of(j * tq, tq)
        for ch in range(n_chain):
            st = s_sc[ch]
            s_sc[ch] = scores(ch, 0, qh_next) if last else scores(ch, j + 1, qh)
            h = ch // 2
            vt = vt_ref[0, h * VT_ROWS:(h + 1) * VT_ROWS, pl.ds(start, tq)]
            if last:
                st = jnp.where(causal, st, NEG_INF)
            m_old = m_sc[ch]
            m_new = jnp.maximum(m_old, jnp.max(st, axis=0, keepdims=True))
            p = jnp.exp2(st - m_new)
            al = jnp.exp2(m_old - m_new)
            acc_sc[ch] = al * acc_sc[ch] + _dot(vt, p.astype(BF16))
            m_sc[ch] = m_new

    def body(j, carry):
        step(j, False)
        return carry

    lax.fori_loop(0, qi, body, 0)
    step(qi, True)

    g = g_ref[...]
    outs = []
    for h in range(DIFF_HEADS):
        a1, a2 = acc_sc[2 * h], acc_sc[2 * h + 1]
        o = (a1[:HEAD_DIM] / a1[HEAD_DIM:HEAD_DIM + 1]
             - lam * (a2[:HEAD_DIM] / a2[HEAD_DIM:HEAD_DIM + 1]))
        ms = jnp.mean(o * o, axis=0, keepdims=True)
        outs.append((o * lax.rsqrt(ms + NORM_EPS)) * g * (1.0 - lambda_init))
    o_ref[0] = jnp.concatenate(outs, axis=0).T.astype(BF16)


def _diff_attn_call(proj, vat, lam, g_cols, lambda_init, layer):
    b, s, _ = proj.shape
    tq = TQ
    qspec = lambda cb: pl.BlockSpec((1, tq, LANES), lambda bi, i, cb=cb: (bi, i, cb))
    last_q = s // tq - 1
    qnext = lambda cb: pl.BlockSpec((1, tq, LANES), lambda bi, i, cb=cb: (bi, jnp.minimum(i + 1, last_q), cb))
    kspec = lambda cb: pl.BlockSpec((1, s, LANES), lambda bi, i, cb=cb: (bi, 0, cb))
    n_chain = 2 * DIFF_HEADS
    return pl.pallas_call(
        functools.partial(_diff_attn_kernel, lambda_init=lambda_init, layer=layer),
        grid=(b, s // tq),
        in_specs=[
            pl.BlockSpec(memory_space=pltpu.SMEM),
            qspec(COL_Q1 // LANES), qspec(COL_Q2 // LANES),
            qnext(COL_Q1 // LANES), qnext(COL_Q2 // LANES),
            kspec(COL_K1 // LANES), kspec(COL_K2 // LANES),
            pl.BlockSpec((1, DIFF_HEADS * VT_ROWS, s), lambda bi, i: (bi, 0, 0)),
            pl.BlockSpec((None, HEAD_DIM, 1), lambda bi, i: (layer, 0, 0)),
        ],
        out_specs=pl.BlockSpec((1, tq, 256), lambda bi, i: (bi, i, 0)),
        out_shape=jax.ShapeDtypeStruct((b, s, 256), BF16),
        scratch_shapes=[
            pltpu.VMEM((n_chain, 1, tq), F32),
            pltpu.VMEM((n_chain, VT_ROWS, tq), F32),
            pltpu.VMEM((n_chain, tq, tq), F32),
        ],
        compiler_params=pltpu.CompilerParams(vmem_limit_bytes=48 * 1024 * 1024),
        name="diff_attn",
    )(lam, proj, proj, proj, proj, proj, proj, vat, g_cols)


def _band_kernel(*refs, nqb, nkb, max_dist, has_sink, want_lse, sink_base):
    it = iter(refs)
    sink_ref = next(it) if has_sink else None
    q_ref, kp_ref, kc_ref, vp_ref, vc_ref = (next(it) for _ in range(5))
    o_ref = next(it)
    lse_ref = next(it) if want_lse else None
    kbuf, vbuf = next(it), next(it)
    nseq, rows = q_ref.shape[0], q_ref.shape[1]
    i = pl.program_id(1)
    for sq in range(nseq):
        kbuf[sq, 0:BAND, :] = kp_ref[sq]
        kbuf[sq, BAND:, :] = kc_ref[sq]
        for kb in range(nkb):
            vbuf[sq, 0:BAND, kb * 256:kb * 256 + LANES] = vp_ref[sq, :, kb * LANES:(kb + 1) * LANES]
            vbuf[sq, BAND:, kb * 256:kb * 256 + LANES] = vc_ref[sq, :, kb * LANES:(kb + 1) * LANES]
            vbuf[sq, :, kb * 256 + LANES:(kb + 1) * 256] = jnp.ones((BAND + rows, LANES), BF16)
    lane = lax.broadcasted_iota(I32, (1, LANES), 1)
    lo_half = lane < 64
    r_io = lax.broadcasted_iota(I32, (BAND, 2 * BAND), 0)
    c_io = lax.broadcasted_iota(I32, (BAND, 2 * BAND), 1)
    dist = BAND + r_io - c_io
    band = (dist >= 0) & (dist <= max_dist)
    band_first = band & ((c_io >= BAND) | (i > 0))
    col0 = lax.broadcasted_iota(I32, (1, 2 * BAND), 1) == 0
    vr = lax.broadcasted_iota(I32, (2 * BAND, 2 * LANES), 0)
    vc = lax.broadcasted_iota(I32, (2 * BAND, 2 * LANES), 1)
    sink_row = (vr == 0) & (vc < LANES)
    to_log2 = HEAD_DIM ** -0.5 * math.log2(math.e)
    units = [(sq, sb, qb) for sq in range(nseq) for sb in range(rows // BAND) for qb in range(nqb)]

    def scores(u):
        sq, sb, qb = units[u]
        kb = qb if nkb > 1 else 0
        q = q_ref[sq, sb * BAND:(sb + 1) * BAND, qb * LANES:(qb + 1) * LANES].astype(F32) * to_log2
        q2 = jnp.concatenate([jnp.where(lo_half, q, 0.0), jnp.where(lo_half, 0.0, q)], axis=0).astype(BF16)
        return _dot_nt(q2, kbuf[sq, sb * BAND:(sb + 2) * BAND, kb * LANES:(kb + 1) * LANES])

    ahead = 2
    pending = [scores(u) for u in range(min(ahead, len(units)))]
    for u, (sq, sb, qb) in enumerate(units):
        if u + ahead < len(units):
            pending.append(scores(u + ahead))
        s2 = pending[u]
        pending[u] = None
        kb = qb if nkb > 1 else 0
        r0 = sb * BAND
        msk = band_first if sb == 0 else band
        halves = []
        for hh in range(2):
            if has_sink:
                fill = jnp.where(col0, sink_ref[sink_base + qb * 2 + hh] * math.log2(math.e), NEG_INF)
            else:
                fill = NEG_INF
            halves.append(jnp.where(msk, s2[hh * BAND:(hh + 1) * BAND], fill))
        s2 = jnp.concatenate(halves, axis=0)
        m = jnp.max(s2, axis=1, keepdims=True)
        p = jnp.exp2(s2 - m).astype(BF16)
        vw = vbuf[sq, r0:r0 + 2 * BAND, kb * 256:(kb + 1) * 256]
        if has_sink:
            vw = jnp.where(sink_row, jnp.zeros_like(vw), vw)
        pv = _dot(p, vw)
        den = pv[:, LANES:]
        out = pv[:, :LANES] / den
        o = jnp.where(lo_half, out[:BAND], out[BAND:])
        o_ref[sq, r0:r0 + BAND, qb * LANES:(qb + 1) * LANES] = o.astype(BF16)
        if want_lse:
            lse2 = m + jnp.log2(den)
            ls = jnp.where(lo_half, lse2[:BAND], lse2[BAND:])
            hi = ls.astype(BF16)
            lo = (ls - hi.astype(F32)).astype(BF16)
            lse_ref[sq, r0:r0 + BAND, qb * LANES:(qb + 1) * LANES] = hi
            lse_ref[sq, r0:r0 + BAND, (nqb + qb) * LANES:(nqb + qb + 1) * LANES] = lo


def _band_call(arr, q_col, k_col, v_col, nqb, nkb, max_dist, sinks=None, sink_base=0, want_lse=False,
               name="band"):
    ns, length, _ = arr.shape
    assert sinks is None or max_dist < BAND
    rows = min(BAND_ROWS, length)
    assert length % rows == 0 and BAND_ROWS % rows == 0 and ns % (BAND_ROWS // rows) == 0
    nseq = BAND_ROWS // rows
    wq, wk = nqb * LANES, nkb * LANES
    rpb = rows // BAND
    cur = lambda col, w: pl.BlockSpec((nseq, rows, w), lambda n, i, c=col // w: (n, i, c))
    prev = lambda col, w: pl.BlockSpec(
        (nseq, BAND, w), lambda n, i, c=col // w: (n, jnp.maximum(i * rpb - 1, 0), c))
    in_specs = [cur(q_col, wq), prev(k_col, wk), cur(k_col, wk), prev(v_col, wk), cur(v_col, wk)]
    args = [arr] * 5
    if sinks is not None:
        in_specs = [pl.BlockSpec(memory_space=pltpu.SMEM)] + in_specs
        args = [sinks] + args
    out_specs = [pl.BlockSpec((nseq, rows, wq), lambda n, i: (n, i, 0))]
    out_shape = [jax.ShapeDtypeStruct((ns, length, wq), BF16)]
    if want_lse:
        out_specs.append(pl.BlockSpec((nseq, rows, 2 * wq), lambda n, i: (n, i, 0)))
        out_shape.append(jax.ShapeDtypeStruct((ns, length, 2 * wq), BF16))
    return pl.pallas_call(
        functools.partial(_band_kernel, nqb=nqb, nkb=nkb, max_dist=max_dist,
                          has_sink=sinks is not None, want_lse=want_lse, sink_base=sink_base),
        grid=(ns // nseq, length // rows),
        in_specs=in_specs,
        out_specs=out_specs,
        out_shape=out_shape,
        scratch_shapes=[pltpu.VMEM((nseq, BAND + rows, wk), BF16), pltpu.VMEM((nseq, BAND + rows, 2 * wk), BF16)],
        name=name,
    )(*args)


def _outproj_router_kernel(oa_ref, ob_ref, o1_ref, l1_ref, o4_ref, l4_ref, o16_ref, l16_ref, p4t_ref, p16t_ref,
                           w_ref, x_ref, g1_ref, sc_ref, sh_ref, g_ref, wr_ref, br_ref, tri_ref,
                           xo_ref, hs_ref, meta_ref, nch_ref, wb):
    @pl.when((pl.program_id(0) == 0) & (pl.program_id(1) == 0))
    def _():
        wb[0:256, :] = w_ref[0:256, :].astype(BF16)
        for pos, head in enumerate(_SWA_HEAD_ORDER):
            wb[256 + pos * HEAD_DIM:256 + (pos + 1) * HEAD_DIM, :] = (
                w_ref[256 + head * HEAD_DIM:256 + (head + 1) * HEAD_DIM, :].astype(BF16))
        wb[768:1024, :] = w_ref[768:1024, :].astype(BF16)

    hw = o1_ref.shape[2]
    nblk = x_ref.shape[1] // TM

    def lse_of(v):
        return v[:, :hw] + v[:, hw:]

    def unpermute(pt_ref, src_ref, width):
        per = src_ref.shape[2] // nblk
        return jnp.concatenate(
            [_dot(pt_ref[...], src_ref[0, :, k * per:(k + 1) * per, :].reshape(TM, width)) for k in range(nblk)],
            axis=0)

    o1 = o1_ref[0].astype(F32)
    ls1 = lse_of(l1_ref[0].astype(F32))
    o4 = unpermute(p4t_ref, o4_ref, hw)
    ls4 = lse_of(unpermute(p4t_ref, l4_ref, 2 * hw))
    o16 = unpermute(p16t_ref, o16_ref, hw)
    ls16 = lse_of(unpermute(p16t_ref, l16_ref, 2 * hw))
    mx = jnp.maximum(jnp.maximum(ls1, ls4), ls16)
    e1, e4, e16 = jnp.exp2(ls1 - mx), jnp.exp2(ls4 - mx), jnp.exp2(ls16 - mx)
    oc = (e1 * o1 + e4 * o4 + e16 * o16) / (e1 + e4 + e16)
    mix = (_dot(oa_ref[0], wb[0:256, :]) + _dot(ob_ref[0], wb[256:768, :])
           + _dot(oc.astype(BF16), wb[768:1024, :]))
    x1 = x_ref[0] + _batch_row(g1_ref) * mix
    xo_ref[0] = x1
    _router_body(x1, sc_ref, sh_ref, g_ref, wr_ref, br_ref, tri_ref, hs_ref, meta_ref, nch_ref)


def _outproj_router_call(oa, ob, o1, l1, o4, l4, o16, l16, p4t, p16t, w_out, layer, x, mod, g, wr, br, tri):
    b, s, d = x.shape
    rows = ROWS
    nblk = rows // TM
    steps = s // rows
    nt = b * s // TM
    row = lambda w_: pl.BlockSpec((1, rows, w_), lambda bi, i: (bi, i, 0))
    res = lambda dd, w_: pl.BlockSpec((1, dd, rows // dd, w_), lambda bi, i: (bi, 0, i, 0))
    const2 = lambda bi, i: (0, 0)
    flat = lambda bi, i: bi * steps + i
    return pl.pallas_call(
        _outproj_router_kernel,
        grid=(b, steps),
        in_specs=[
            row(256), row(512), row(256), row(512),
            res(4, 256), res(4, 512), res(16, 256), res(16, 512),
            _const_spec((TM, TM), const2), _const_spec((TM, TM), const2),
            _const_spec((None, d, d), lambda bi, i: (layer, 0, 0)),
            row(d),
            _mod_spec(mod, layer, 2),
            _mod_spec(mod, layer, 4),
            _mod_spec(mod, layer, 3),
            pl.BlockSpec((None, 1, d), lambda bi, i: (layer, 0, 0)),
            _const_spec((None, 2 * ROUTER_COLS, d), lambda bi, i: (layer, 0, 0)),
            pl.BlockSpec((None, ROUTER_COLS, 1), lambda bi, i: (layer, 0, 0)),
            _const_spec((TM, TM), const2),
        ],
        out_specs=[
            row(d),
            pl.BlockSpec((nblk * SLOTS, d), lambda bi, i: (flat(bi, i), 0)),
            pl.BlockSpec((nblk, 8, TM), lambda bi, i: (flat(bi, i), 0, 0)),
            pl.BlockSpec((nblk, N_EXPERTS, LANES), lambda bi, i: (flat(bi, i), 0, 0)),
        ],
        out_shape=[
            jax.ShapeDtypeStruct((b, s, d), F32),
            jax.ShapeDtypeStruct((nt * SLOTS, d), BF16),
            jax.ShapeDtypeStruct((nt, 8, TM), F32),
            jax.ShapeDtypeStruct((nt, N_EXPERTS, LANES), I32),
        ],
        scratch_shapes=[pltpu.VMEM((d, d), BF16)],
        compiler_params=pltpu.CompilerParams(vmem_limit_bytes=56 * 1024 * 1024),
        name="out_proj_router",
    )(oa, ob, o1, l1, o4, l4, o16, l16, p4t, p16t, w_out, x, mod, mod, mod, g, wr, br, tri)


ROUTER_COLS = LANES


def _router_body(x, sc_ref, sh_ref, g_ref, wr_ref, br_ref, tri_ref, hs_ref, meta_ref, nch_ref):
    tm = TM
    h = _modulated_norm(x, g_ref[...], _batch_row(sc_ref), _batch_row(sh_ref))
    hb = h.astype(BF16)
    h_lo = (h - hb.astype(F32)).astype(BF16)
    part = _dot_nt(wr_ref[...], hb) + _dot_nt(wr_ref[...], h_lo)
    logits_t = part[:ROUTER_COLS] + part[ROUTER_COLS:] + br_ref[...]
    r8 = lax.broadcasted_iota(I32, (8, tm), 0)
    r16 = lax.broadcasted_iota(I32, (N_EXPERTS, tm), 0)
    rl = lax.broadcasted_iota(I32, (N_EXPERTS, LANES), 0)
    slot = lax.broadcasted_iota(I32, (SLOTS, tm), 0)
    for blk in range(x.shape[0] // tm):
        lt = logits_t[:, blk * tm:(blk + 1) * tm]
        glog = lt[0:8]
        elog = lt[8:8 + N_EXPERTS]

        gmax = jnp.max(glog, axis=0, keepdims=True)
        g_w = 1.0 / jnp.sum(jnp.exp(glog - gmax), axis=0, keepdims=True)
        g_idx = jnp.min(jnp.where(glog == gmax, r8, 99), axis=0, keepdims=True)

        el = jnp.where((r16 // EXPERTS_PER_GROUP) == g_idx, elog, NEG_INF)
        emax = jnp.max(el, axis=0, keepdims=True)
        e1 = jnp.min(jnp.where(el == emax, r16, 99), axis=0, keepdims=True)
        el2 = jnp.where(r16 == e1, NEG_INF, el)
        emax2 = jnp.max(el2, axis=0, keepdims=True)
        e2 = jnp.min(jnp.where(el2 == emax2, r16, 99), axis=0, keepdims=True)
        p2 = jnp.exp(emax2 - emax)
        wt1 = g_w / (1.0 + p2)
        wt2 = g_w * p2 / (1.0 + p2)

        oh1 = r16 == e1
        oh2 = r16 == e2
        onehot = jnp.where(oh1, 1.0, 0.0) + jnp.where(oh2, 1.0, 0.0)
        cnt = jnp.sum(onehot, axis=1, keepdims=True)
        nch = jnp.floor((cnt + (CHUNK - 1)) * (1.0 / CHUNK))
        nchb = jnp.broadcast_to(nch, (N_EXPERTS, LANES))
        incl = nchb
        for sft in (1, 2, 4, 8):
            incl = incl + jnp.where(rl >= sft, pltpu.roll(incl, sft, 0), 0.0)
        off = (incl - nchb)[:, 0:1] * float(CHUNK)
        rank = _dot(onehot.astype(BF16), tri_ref[...])
        slot_of = off + rank
        pos1 = jnp.sum(jnp.where(oh1, slot_of, 0.0), axis=0, keepdims=True)
        pos2 = jnp.sum(jnp.where(oh2, slot_of, 0.0), axis=0, keepdims=True)

        sel = jnp.where(slot == pos1.astype(I32), 1.0, jnp.where(slot == pos2.astype(I32), 1.0, 0.0))
        hs_ref[blk * SLOTS:(blk + 1) * SLOTS, :] = _dot(sel.astype(BF16), hb[blk * tm:(blk + 1) * tm]).astype(BF16)

        meta_ref[blk] = jnp.concatenate([pos1, pos2, wt1, wt2, jnp.zeros((4, tm), F32)], axis=0)
        nch_ref[blk] = nchb.astype(I32)


def _ffn_schedule(nch, max_tiles, dump_base):
    nt = nch.shape[0]
    cend = jnp.cumsum(nch, axis=1)
    coff = cend - nch
    tcum = jnp.cumsum(nch, axis=0)
    before = tcum - nch
    tot = tcum[-1]
    pad = ((tot + FFN_CHUNKS - 1) // FFN_CHUNKS) * FFN_CHUNKS
    eend = jnp.cumsum(pad)
    estart = eend - pad
    n_tiles = (eend[-1] // FFN_CHUNKS).astype(I32)
    first_chunk = jnp.arange(max_tiles, dtype=I32) * FFN_CHUNKS
    tile_expert = jnp.sum((eend[None, :] <= first_chunk[:, None]).astype(I32), axis=1)
    tile_expert = jnp.minimum(tile_expert, N_EXPERTS - 1)
    hp = lax.Precision.HIGHEST
    pos = jnp.arange(max_tiles * FFN_CHUNKS, dtype=I32)
    e_s = jnp.minimum(jnp.sum((eend[None, :] <= (pos // FFN_CHUNKS * FFN_CHUNKS)[:, None]).astype(I32), axis=1),
                      N_EXPERTS - 1)
    oh_e = (e_s[:, None] == jnp.arange(N_EXPERTS, dtype=I32)[None, :]).astype(F32)
    idx = pos - jnp.dot(oh_e, estart.astype(F32), precision=hp).astype(I32)
    run_end = jnp.dot(oh_e, tcum.T.astype(F32), precision=hp).astype(I32)
    run_beg = jnp.dot(oh_e, before.T.astype(F32), precision=hp).astype(I32)
    run_off = jnp.dot(oh_e, coff.T.astype(F32), precision=hp).astype(I32)
    in_run = (idx[:, None] >= run_beg) & (idx[:, None] < run_end)
    tile_base = jnp.arange(nt, dtype=I32)[None, :] * CHUNKS_PER_TILE
    src = jnp.sum(jnp.where(in_run, tile_base + run_off + idx[:, None] - run_beg, 0), axis=1)
    real = jnp.any(in_run, axis=1)
    dump = dump_base + (pos // FFN_CHUNKS % 2) * FFN_CHUNKS + pos % FFN_CHUNKS
    src_rows = jnp.where(real, src, 0) * CHUNK
    dst_rows = jnp.where(real, src, dump) * CHUNK
    used = cend[:, -1]
    ucum = jnp.cumsum(CHUNKS_PER_TILE - used)
    ubeg = ucum - (CHUNKS_PER_TILE - used)
    z = jnp.arange(max_tiles * ZERO_CHUNKS, dtype=I32)[:, None]
    in_gap = (z >= ubeg[None, :]) & (z < ucum[None, :])
    zsrc = jnp.sum(jnp.where(in_gap, tile_base + used[None, :] + z - ubeg[None, :], 0), axis=1)
    zdump = dump_base + 2 * FFN_CHUNKS + z[:, 0] % ZERO_CHUNKS
    zero_rows = jnp.where(jnp.any(in_gap, axis=1), zsrc, zdump) * CHUNK
    return tile_expert, src_rows, dst_rows, n_tiles.reshape(1), zero_rows


def _ffn_kernel(te_ref, sr_ref, dr_ref, nt_ref, zr_ref, hs_hbm, wg_ref, wu_ref, wd_ref, ys_hbm,
                xbuf, ybuf, zbuf, wgb, wub, wdb, in_sem, out_sem, zero_sem, *, dump_base):
    j = pl.program_id(0)
    nt = nt_ref[0]
    half_ff = EXPERT_FF // 2

    def rows_at(r):
        return pl.ds(pl.multiple_of(r, CHUNK), CHUNK)

    def in_copy(step, slot, k, wait=False):
        r = 0 if wait else sr_ref[step * FFN_CHUNKS + k]
        return pltpu.make_async_copy(hs_hbm.at[rows_at(r), :], xbuf.at[slot, pl.ds(k * CHUNK, CHUNK), :],
                                     in_sem.at[slot])

    def out_copy(step, slot, k, wait=False):
        r = 0 if wait else dr_ref[step * FFN_CHUNKS + k]
        return pltpu.make_async_copy(ybuf.at[slot, pl.ds(k * CHUNK, CHUNK), :], ys_hbm.at[rows_at(r), :],
                                     out_sem.at[slot])

    def zero_copy(k, wait=False):
        r = 0 if wait else zr_ref[j * ZERO_CHUNKS + k]
        return pltpu.make_async_copy(zbuf.at[pl.ds(k * CHUNK, CHUNK), :], ys_hbm.at[rows_at(r), :], zero_sem)

    @pl.when(j == 0)
    def _():
        zbuf[...] = jnp.zeros_like(zbuf)
        fills = [pltpu.make_async_copy(
            zbuf, ys_hbm.at[pl.ds((dump_base + r * ZERO_CHUNKS) * CHUNK, ZERO_CHUNKS * CHUNK), :], zero_sem)
            for r in range(DUMP_CHUNKS // ZERO_CHUNKS)]
        for cp in fills:
            cp.start()
        for cp in fills:
            cp.wait()

    for k in range(ZERO_CHUNKS):
        zero_copy(k).start()

    @pl.when(j < nt)
    def _():
        slot = j % 2

        @pl.when(j == 0)
        def _():
            for k in range(FFN_CHUNKS):
                in_copy(0, 0, k).start()

        @pl.when(j + 1 < nt)
        def _():
            for k in range(FFN_CHUNKS):
                in_copy(j + 1, 1 - slot, k).start()

        @pl.when((j == 0) | (te_ref[j] != te_ref[jnp.maximum(j - 1, 0)]))
        def _():
            wgb[...] = wg_ref[0].astype(BF16)
            wub[...] = wu_ref[0].astype(BF16)
            wdb[...] = wd_ref[0].astype(BF16)

        for k in range(FFN_CHUNKS):
            in_copy(j, slot, k, wait=True).wait()

        @pl.when(j >= 2)
        def _():
            for k in range(FFN_CHUNKS):
                out_copy(j - 2, slot, k, wait=True).wait()

        x = xbuf[slot]
        hg = [_dot(x, wgb[:, h * half_ff:(h + 1) * half_ff]) for h in range(2)]
        hu = [_dot(x, wub[:, h * half_ff:(h + 1) * half_ff]) for h in range(2)]
        y = None
        for h in range(2):
            act = ((hg[h] / (1.0 + jnp.exp(-hg[h]))) * hu[h]).astype(BF16)
            part = _dot(act, wdb[h * half_ff:(h + 1) * half_ff, :])
            y = part if y is None else y + part
        ybuf[slot] = y.astype(BF16)
        for k in range(FFN_CHUNKS):
            out_copy(j, slot, k).start()

        @pl.when(j == nt - 1)
        def _():
            for k in range(FFN_CHUNKS):
                out_copy(j, slot, k, wait=True).wait()

            @pl.when(j >= 1)
            def _():
                for k in range(FFN_CHUNKS):
                    out_copy(j - 1, 1 - slot, k, wait=True).wait()

    for k in range(ZERO_CHUNKS):
        zero_copy(k, wait=True).wait()


def _ffn_call(tile_expert, src_rows, dst_rows, n_tiles, zero_rows, hs, wg, wu, wd, layer, max_tiles):
    rows, d = hs.shape
    ff = wg.shape[-1]
    wmap = lambda j, te, sr, dr, nt, zr: (layer, te[j], 0, 0)
    grid_spec = pltpu.PrefetchScalarGridSpec(
        num_scalar_prefetch=5,
        grid=(max_tiles,),
        in_specs=[
            pl.BlockSpec(memory_space=pl.ANY),
            pl.BlockSpec((None, 1, d, ff), wmap),
            pl.BlockSpec((None, 1, d, ff), wmap),
            pl.BlockSpec((None, 1, ff, d), wmap),
        ],
        out_specs=pl.BlockSpec(memory_space=pl.ANY),
        scratch_shapes=[
            pltpu.VMEM((2, FFN_ROWS, d), BF16),
            pltpu.VMEM((2, FFN_ROWS, d), BF16),
            pltpu.VMEM((ZERO_CHUNKS * CHUNK, d), BF16),
            pltpu.VMEM((d, ff), BF16),
            pltpu.VMEM((d, ff), BF16),
            pltpu.VMEM((ff, d), BF16),
            pltpu.SemaphoreType.DMA((2,)),
            pltpu.SemaphoreType.DMA((2,)),
            pltpu.SemaphoreType.DMA(()),
        ],
    )
    return pl.pallas_call(
        functools.partial(_ffn_kernel, dump_base=rows // CHUNK),
        grid_spec=grid_spec,
        out_shape=jax.ShapeDtypeStruct((rows + DUMP_CHUNKS * CHUNK, d), BF16),
        compiler_params=pltpu.CompilerParams(vmem_limit_bytes=48 * 1024 * 1024),
        name="expert_ffn",
    )(tile_expert, src_rows, dst_rows, n_tiles, zero_rows, hs, wg, wu, wd)


def _combine_body(ys_ref, meta_ref, x_ref, g2_ref):
    tm = TM
    eye = (lax.broadcasted_iota(I32, (tm, tm), 0) == lax.broadcasted_iota(I32, (tm, tm), 1))
    slot = lax.broadcasted_iota(I32, (tm, SLOTS), 1).astype(F32)

    def as_col(row):
        return jnp.sum(jnp.where(eye, row, 0.0), axis=1, keepdims=True)

    ys = []
    for blk in range(x_ref.shape[1] // tm):
        meta = meta_ref[blk]
        pos1, pos2 = as_col(meta[0:1]), as_col(meta[1:2])
        w1, w2 = as_col(meta[2:3]), as_col(meta[3:4])
        gate = jnp.where(slot == pos1, w1, 0.0) + jnp.where(slot == pos2, w2, 0.0)
        ys.append(_dot(gate.astype(BF16), ys_ref[blk * SLOTS:(blk + 1) * SLOTS, :]))
    return x_ref[0] + _batch_row(g2_ref) * jnp.concatenate(ys, axis=0)


def _combine_final_kernel(ys_ref, meta_ref, x_ref, g2_ref, gf_ref, xo_ref):
    xo = _combine_body(ys_ref, meta_ref, x_ref, g2_ref)
    xo_ref[0] = xo * lax.rsqrt(jnp.mean(xo * xo, axis=-1, keepdims=True) + NORM_EPS) * gf_ref[...]


def _combine_inproj_kernel(ys_ref, meta_ref, x_ref, g2_ref, *refs):
    n_in = 10
    xo_ref = refs[n_in]
    xo = _combine_body(ys_ref, meta_ref, x_ref, g2_ref)
    xo_ref[0] = xo
    _inproj_body(xo, *refs[:n_in], *refs[n_in + 1:])


def _combine_specs(d, steps, mod, layer):
    nblk = ROWS // TM
    flat = lambda bi, i: bi * steps + i
    return [
        pl.BlockSpec((nblk * SLOTS, d), lambda bi, i: (flat(bi, i), 0)),
        pl.BlockSpec((nblk, 8, TM), lambda bi, i: (flat(bi, i), 0, 0)),
        pl.BlockSpec((1, ROWS, d), lambda bi, i: (bi, i, 0)),
        _mod_spec(mod, layer, 5),
    ]


def _combine_final_call(ys, meta, x, mod, layer, gf):
    b, s, d = x.shape
    return pl.pallas_call(
        _combine_final_kernel,
        grid=(b, s // ROWS),
        in_specs=_combine_specs(d, s // ROWS, mod, layer) + [pl.BlockSpec((1, d), lambda bi, i: (0, 0))],
        out_specs=pl.BlockSpec((1, ROWS, d), lambda bi, i: (bi, i, 0)),
        out_shape=jax.ShapeDtypeStruct((b, s, d), F32),
        compiler_params=pltpu.CompilerParams(vmem_limit_bytes=48 * 1024 * 1024),
        name="moe_combine",
    )(ys, meta, x, mod, gf)


def _combine_inproj_call(ys, meta, x, mod, g, w_in, layer, tabs, p4, p16):
    b, s, d = x.shape
    in_specs, out_specs, out_shape = _inproj_specs(b, s, d, layer, mod)
    xspec = pl.BlockSpec((1, ROWS, d), lambda bi, i: (bi, i, 0))
    return pl.pallas_call(
        _combine_inproj_kernel,
        grid=(b, s // ROWS),
        in_specs=_combine_specs(d, s // ROWS, mod, layer - 1) + in_specs,
        out_specs=[xspec] + out_specs,
        out_shape=[jax.ShapeDtypeStruct((b, s, d), F32)] + out_shape,
        scratch_shapes=[pltpu.VMEM((d, IN_WIDTH), BF16)],
        compiler_params=pltpu.CompilerParams(vmem_limit_bytes=56 * 1024 * 1024),
        name="combine_in_proj",
    )(ys, meta, x, mod, mod, mod, g, w_in, *tabs, p4, p16)


def _rope_tables(positions):
    pos = positions.astype(F32)[..., None]

    def table(dim):
        inv = ROPE_THETA ** (-jnp.arange(0, dim, 2, dtype=F32) / dim)
        ang = pos * inv
        cos, sin = jnp.cos(ang), jnp.sin(ang)
        reps = LANES // dim
        return (jnp.tile(jnp.concatenate([cos, cos], -1), (1, 1, reps)),
                jnp.tile(jnp.concatenate([-sin, sin], -1), (1, 1, reps)))

    c64, s64 = table(HEAD_DIM)
    c32, s32 = table(DIFF_QK_DIM)
    return c64, s64, c32, s32


def kernel(x, c, positions, ada_w, ada_b, norm_mix_g, norm_ffn_g, w_in, w_out, diff_lambda_q1, diff_lambda_k1,
           diff_lambda_q2, diff_lambda_k2, diff_subln_g, swa_sinks, router_group_w, router_group_b,
           router_expert_w, router_expert_b, expert_w_gate, expert_w_up, expert_w_down, final_norm_g):
    b, s, d = x.shape
    depth = ada_w.shape[0]
    assert b <= BF16_ROWS and s % TQ == 0 and s % ROWS == 0 and s % (16 * BAND) == 0 and d == 8 * LANES
    n = b * s
    nt = n // TM
    max_tiles = (nt * CHUNKS_PER_TILE + N_EXPERTS * (FFN_CHUNKS - 1)) // FFN_CHUNKS + 1

    tabs = _rope_tables(positions)
    p4 = _residue_perm(TM, 4)
    p16 = _residue_perm(TM, 16)
    p4_b, p16_b = jnp.asarray(p4, BF16), jnp.asarray(p16, BF16)
    p4t_b, p16t_b = jnp.asarray(p4.T, BF16), jnp.asarray(p16.T, BF16)
    tri = jnp.asarray(np.triu(np.ones((TM, TM), np.float32), 1), BF16)

    c_pad = jnp.pad(c, ((0, BF16_ROWS - b), (0, 0)))
    mod = _ada_call(c_pad, ada_w, ada_b)

    sink_order = np.asarray(_SWA_HEAD_ORDER, np.int32)
    zpad = lambda k: jnp.zeros((depth, d, k), F32)
    wr = jnp.concatenate([router_group_w, zpad(8 - N_GROUPS), router_expert_w,
                          zpad(ROUTER_COLS - 8 - N_EXPERTS)], axis=-1)
    wr_hi = wr.astype(BF16)
    wr = jnp.concatenate([wr_hi, (wr - wr_hi.astype(F32)).astype(BF16)], axis=-1)
    wr = jnp.swapaxes(wr, 1, 2)
    br = jnp.concatenate([router_group_b, jnp.full((depth, 8 - N_GROUPS), NEG_INF, F32), router_expert_b,
                          jnp.zeros((depth, ROUTER_COLS - 8 - N_EXPERTS), F32)], axis=-1).reshape(depth, -1, 1)

    lam_init = [0.8 - 0.6 * math.exp(-0.3 * l) for l in range(depth)]
    lam = (jnp.exp(jnp.sum(diff_lambda_q1 * diff_lambda_k1, axis=-1))
           - jnp.exp(jnp.sum(diff_lambda_q2 * diff_lambda_k2, axis=-1)) + jnp.asarray(lam_init, F32))
    g_cols = diff_subln_g[:, :, None]
    sinks = swa_sinks[:, sink_order].reshape(-1)
    g_mix = norm_mix_g.reshape(depth, 1, d)
    g_ffn = norm_ffn_g.reshape(depth, 1, d)

    proj, vat, qkv4, qkv16 = _inproj_call(x, mod, g_mix, w_in, 0, tabs, p4_b, p16_b)
    for l in range(depth):
        oa = _diff_attn_call(proj, vat, lam, g_cols, lam_init[l], l)
        ob = _band_call(proj, COL_QB, COL_KB, COL_VB, nqb=4, nkb=1, max_dist=SWA_WINDOW - 1,
                        sinks=sinks, sink_base=l * SWA_Q_HEADS, name="swa")[0]
        o1, l1 = _band_call(proj, COL_QC, COL_KC, COL_VC, nqb=2, nkb=2, max_dist=BAND, want_lse=True,
                            name="dil1")
        o4, l4 = _band_call(qkv4.reshape(b * 4, s // 4, 768), 0, 256, 512, nqb=2, nkb=2, max_dist=BAND,
                            want_lse=True, name="dil4")
        o16, l16 = _band_call(qkv16.reshape(b * 16, s // 16, 768), 0, 256, 512, nqb=2, nkb=2, max_dist=BAND,
                              want_lse=True, name="dil16")
        x, hs, meta, nch = _outproj_router_call(
            oa, ob, o1, l1, o4.reshape(b, 4, s // 4, 256), l4.reshape(b, 4, s // 4, 512),
            o16.reshape(b, 16, s // 16, 256), l16.reshape(b, 16, s // 16, 512), p4t_b, p16t_b, w_out, l, x, mod,
            g_ffn, wr, br, tri)
        sched = _ffn_schedule(nch[:, :, 0], max_tiles, nt * CHUNKS_PER_TILE)
        ys = _ffn_call(*sched, hs, expert_w_gate, expert_w_up, expert_w_down, l, max_tiles)
        if l + 1 < depth:
            x, proj, vat, qkv4, qkv16 = _combine_inproj_call(ys, meta, x, mod, g_mix, w_in, l + 1, tabs,
                                                             p4_b, p16_b)
        else:
            x = _combine_final_call(ys, meta, x, mod, l, final_norm_g.reshape(1, d))
    return x
```

```python
import functools
import math

import numpy as np
import jax
import jax.numpy as jnp
from jax import lax
from jax.experimental import pallas as pl
from jax.experimental.pallas import tpu as pltpu

F32 = jnp.float32
BF16 = jnp.bfloat16
I32 = jnp.int32

HEAD_DIM = 64
ROPE_THETA = 10000.0
NORM_EPS = 1e-6
NEG_INF = -1e30
DIFF_HEADS = 4
DIFF_QK_DIM = 32
SWA_Q_HEADS = 8
SWA_KV_HEADS = 2
SWA_WINDOW = 128
DIL_PATTERNS = ((128, 1), (512, 4), (2048, 16))
N_GROUPS = 4
EXPERTS_PER_GROUP = 4
N_EXPERTS = 16
EXPERT_FF = 512
N_ADA = 6
IN_WIDTH = 2304

LANES = 128
BF16_ROWS = 16
BAND = 128

TM = 256
ROWS = 2 * TM
CHUNK = BF16_ROWS
SLOTS = 2 * TM + N_EXPERTS * CHUNK
CHUNKS_PER_TILE = SLOTS // CHUNK
FFN_ROWS = 1024
FFN_CHUNKS = FFN_ROWS // CHUNK
ZERO_CHUNKS = -(-N_EXPERTS * FFN_CHUNKS // CHUNKS_PER_TILE)
DUMP_CHUNKS = 2 * FFN_CHUNKS
ZBUF_ROWS = FFN_ROWS // 4
TQ = 512
BAND_ROWS = 2048

_SWA_HEAD_ORDER = (0, 4, 1, 5, 2, 6, 3, 7)
COL_QB, COL_KB, COL_VB = 0, 512, 640
COL_Q1, COL_Q2, COL_K1, COL_K2, COL_VA = 768, 896, 1024, 1152, 1280
COL_QC, COL_KC, COL_VC = 1536, 1792, 2048
_SRC_ROPE = (32, 32, 32, 32, 0, 0, 64, 64, 64, 64, 64, 0, 64, 64, 64, 64, 0, 0)
_SRC_DEST = (6, 7, 8, 9, 10, 11, None, None, None, None, 4, 5, 12, 13, 14, 15, 16, 17)
VT_ROWS = HEAD_DIM + BF16_ROWS
_SRC_VA_CHUNK = 2


def _residue_perm(tm, d):
    p = np.zeros((tm, tm), np.float32)
    per = tm // d
    for l in range(per):
        for r in range(d):
            p[r * per + l, l * d + r] = 1.0
    return p


def _dot(a, b, **kw):
    return jnp.dot(a, b, preferred_element_type=F32, **kw)


def _dot_nt(a, b):
    return lax.dot_general(a, b, (((1,), (1,)), ((), ())), preferred_element_type=F32)


def _batch_row(ref):
    return ref[pl.ds(pl.program_id(0), 1), :]


def _modulated_norm(x, g, sc, sh):
    y = x * lax.rsqrt(jnp.mean(x * x, axis=-1, keepdims=True) + NORM_EPS)
    return (y * g) * (1.0 + sc) + sh


def _ada_kernel(c_ref, w_ref, b_ref, o_ref):
    c = c_ref[...]
    ca = c / (1.0 + jnp.exp(-c))
    rows = ca.shape[0]
    c_hi = ca.astype(BF16)
    c_lo = (ca - c_hi.astype(F32)).astype(BF16)
    w = w_ref[0]
    w_hi = w.astype(BF16)
    w_lo = (w - w_hi.astype(F32)).astype(BF16)
    main = _dot(jnp.concatenate([c_hi, c_lo], axis=0), w_hi)
    o_ref[0] = main[:rows] + main[rows:] + _dot(c_hi, w_lo) + b_ref[0]


def _ada_call(c_pad, ada_w, ada_b):
    depth, d, n = ada_w.shape
    tn = n // 2
    return pl.pallas_call(
        _ada_kernel,
        grid=(depth, n // tn),
        in_specs=[
            pl.BlockSpec((c_pad.shape[0], d), lambda l, j: (0, 0)),
            pl.BlockSpec((1, d, tn), lambda l, j: (l, 0, j)),
            pl.BlockSpec((1, 1, tn), lambda l, j: (l, 0, j)),
        ],
        out_specs=pl.BlockSpec((1, c_pad.shape[0], tn), lambda l, j: (l, 0, j)),
        out_shape=jax.ShapeDtypeStruct((depth, c_pad.shape[0], n), F32),
        compiler_params=pltpu.CompilerParams(vmem_limit_bytes=40 * 1024 * 1024),
        name="ada_mod",
    )(c_pad, ada_w, ada_b.reshape(depth, 1, n))


def _rope(t, cos, sin_signed, first_half, half):
    rot = jnp.where(first_half, pltpu.roll(t, LANES - half, 1), pltpu.roll(t, half, 1))
    return t * cos + rot * sin_signed


def _inproj_body(x, sc_ref, sh_ref, g_ref, w_ref, cs64_ref, sn64_ref, cs32_ref, sn32_ref,
                 p4_ref, p16_ref, proj_ref, vat_ref, c4_ref, c16_ref, wb):
    @pl.when((pl.program_id(0) == 0) & (pl.program_id(1) == 0))
    def _():
        wb[...] = w_ref[...].astype(BF16)

    h = _modulated_norm(x, g_ref[...], _batch_row(sc_ref), _batch_row(sh_ref))
    hb = h.astype(BF16)
    lane = lax.broadcasted_iota(I32, (1, LANES), 1)
    first64 = (lane % 64) < 32
    first32 = (lane % 32) < 16
    lo_half = lane < 64
    swa_q = []
    for cb in range(IN_WIDTH // 256):
        if cb == IN_WIDTH // 256 - 1:
            half = hb.shape[0] // 2
            acc = jnp.concatenate([_dot(hb[:half], wb[:, cb * 256:(cb + 1) * 256]),
                                   _dot(hb[half:], wb[:, cb * 256:(cb + 1) * 256])], axis=0)
        else:
            acc = _dot(hb, wb[:, cb * 256:(cb + 1) * 256])
        if cb == _SRC_VA_CHUNK:
            acc_t = acc.T.astype(BF16)
            for hd in range(DIFF_HEADS):
                vat_ref[0, hd * VT_ROWS:hd * VT_ROWS + HEAD_DIM, :] = acc_t[hd * HEAD_DIM:(hd + 1) * HEAD_DIM]
                vat_ref[0, hd * VT_ROWS + HEAD_DIM:(hd + 1) * VT_ROWS, :] = jnp.ones(
                    (BF16_ROWS, acc_t.shape[1]), BF16)
        for half in range(2):
            src = cb * 2 + half
            t = acc[:, half * LANES:(half + 1) * LANES]
            if _SRC_ROPE[src] == 64:
                t = _rope(t, cs64_ref[0], sn64_ref[0], first64, 32)
            elif _SRC_ROPE[src] == 32:
                t = _rope(t, cs32_ref[0], sn32_ref[0], first32, 16)
            dst = _SRC_DEST[src]
            if dst is None:
                swa_q.append(t)
            else:
                proj_ref[0, :, dst * LANES:(dst + 1) * LANES] = t.astype(BF16)
    for jb in range(SWA_Q_HEADS // 2):
        a, c = swa_q[jb // 2], swa_q[2 + jb // 2]
        if jb % 2 == 0:
            blk = jnp.where(lo_half, a, pltpu.roll(c, 64, 1))
        else:
            blk = jnp.where(lo_half, pltpu.roll(a, 64, 1), c)
        proj_ref[0, :, jb * LANES:(jb + 1) * LANES] = blk.astype(BF16)
    tm = TM
    for blk in range(x.shape[0] // tm):
        cc = proj_ref[0, blk * tm:(blk + 1) * tm, COL_QC:]
        c4 = _dot(p4_ref[...], cc).astype(BF16)
        for r in range(4):
            c4_ref[0, r, blk * (tm // 4):(blk + 1) * (tm // 4)] = c4[r * (tm // 4):(r + 1) * (tm // 4)]
        c16 = _dot(p16_ref[...], cc).astype(BF16)
        for r in range(16):
            c16_ref[0, r, blk * (tm // 16):(blk + 1) * (tm // 16)] = c16[r * (tm // 16):(r + 1) * (tm // 16)]


def _inproj_kernel(x_ref, *refs):
    _inproj_body(x_ref[0], *refs)


def _mod_spec(mod, layer, k):
    return pl.BlockSpec((None, mod.shape[1], mod.shape[2] // N_ADA), lambda bi, i: (layer, 0, k))


def _const_spec(shape, index_map):
    return pl.BlockSpec(shape, index_map, pipeline_mode=pl.Buffered(1))


def _inproj_specs(b, s, d, layer, mod):
    rows = ROWS
    row = lambda bi, i: (bi, i, 0)
    const2 = lambda bi, i: (0, 0)
    in_specs = [
        _mod_spec(mod, layer, 1),
        _mod_spec(mod, layer, 0),
        pl.BlockSpec((None, 1, d), lambda bi, i: (layer, 0, 0)),
        _const_spec((None, d, IN_WIDTH), lambda bi, i: (layer, 0, 0)),
        pl.BlockSpec((1, rows, LANES), row),
        pl.BlockSpec((1, rows, LANES), row),
        pl.BlockSpec((1, rows, LANES), row),
        pl.BlockSpec((1, rows, LANES), row),
        _const_spec((TM, TM), const2),
        _const_spec((TM, TM), const2),
    ]
    out_specs = [
        pl.BlockSpec((1, rows, IN_WIDTH), row),
        pl.BlockSpec((1, DIFF_HEADS * VT_ROWS, rows), lambda bi, i: (bi, 0, i)),
        pl.BlockSpec((1, 4, rows // 4, 768), lambda bi, i: (bi, 0, i, 0)),
        pl.BlockSpec((1, 16, rows // 16, 768), lambda bi, i: (bi, 0, i, 0)),
    ]
    out_shape = [
        jax.ShapeDtypeStruct((b, s, IN_WIDTH), BF16),
        jax.ShapeDtypeStruct((b, DIFF_HEADS * VT_ROWS, s), BF16),
        jax.ShapeDtypeStruct((b, 4, s // 4, 768), BF16),
        jax.ShapeDtypeStruct((b, 16, s // 16, 768), BF16),
    ]
    return in_specs, out_specs, out_shape


def _inproj_call(x, mod, g, w_in, layer, tabs, p4, p16):
    b, s, d = x.shape
    in_specs, out_specs, out_shape = _inproj_specs(b, s, d, layer, mod)
    return pl.pallas_call(
        _inproj_kernel,
        grid=(b, s // ROWS),
        in_specs=[pl.BlockSpec((1, ROWS, d), lambda bi, i: (bi, i, 0))] + in_specs,
        out_specs=out_specs,
        out_shape=out_shape,
        scratch_shapes=[pltpu.VMEM((d, IN_WIDTH), BF16)],
        compiler_params=pltpu.CompilerParams(vmem_limit_bytes=56 * 1024 * 1024),
        name="in_proj",
    )(x, mod, mod, g, w_in, *tabs, p4, p16)


def _diff_attn_kernel(lam_ref, q1_ref, q2_ref, q1n_ref, q2n_ref, k1_ref, k2_ref, vt_ref, g_ref, o_ref,
                      m_sc, acc_sc, s_sc, *, lambda_init, layer):
    tq = q1_ref.shape[1]
    qi = pl.program_id(1)
    lam = lam_ref[layer]
    to_log2 = DIFF_QK_DIM ** -0.5 * math.log2(math.e)
    lane = lax.broadcasted_iota(I32, (1, LANES), 1)

    def head_queries(qa_ref, qb_ref):
        qa = qa_ref[0].astype(F32) * to_log2
        qb = qb_ref[0].astype(F32) * to_log2
        out = []
        for h in range(DIFF_HEADS):
            hm = (lane // DIFF_QK_DIM) == h
            out.append((jnp.where(hm, qa, 0.0).astype(BF16), jnp.where(hm, qb, 0.0).astype(BF16)))
        return out

    qh = head_queries(q1_ref, q2_ref)
    qh_next = head_queries(q1n_ref, q2n_ref)
    causal = (lax.broadcasted_iota(I32, (tq, tq), 0) <= lax.broadcasted_iota(I32, (tq, tq), 1))

    m_sc[...] = jnp.full(m_sc.shape, NEG_INF, F32)
    acc_sc[...] = jnp.zeros(acc_sc.shape, F32)

    n_chain = 2 * DIFF_HEADS

    def scores(ch, tile, queries):
        start = pl.multiple_of(tile * tq, tq)
        k_ref = k1_ref if ch % 2 == 0 else k2_ref
        return _dot_nt(k_ref[0, pl.ds(start, tq), :], queries[ch // 2][ch % 2])

    @pl.when(qi == 0)
    def _():
        for ch in range(n_chain):
            s_sc[ch] = scores(ch, 0, qh)

    def step(j, last):
        start = pl.multiple_of(j * tq, tq)
        for ch in range(n_chain):
            st = s_sc[ch]
            s_sc[ch] = scores(ch, 0, qh_next) if last else scores(ch, j + 1, qh)
            h = ch // 2
            vt = vt_ref[0, h * VT_ROWS:(h + 1) * VT_ROWS, pl.ds(start, tq)]
            if last:
                st = jnp.where(causal, st, NEG_INF)
            m_old = m_sc[ch]
            m_new = jnp.maximum(m_old, jnp.max(st, axis=0, keepdims=True))
            p = jnp.exp2(st - m_new)
            al = jnp.exp2(m_old - m_new)
            acc_sc[ch] = al * acc_sc[ch] + _dot(vt, p.astype(BF16))
            m_sc[ch] = m_new

    def body(j, carry):
        step(j, False)
        return carry

    lax.fori_loop(0, qi, body, 0)
    step(qi, True)

    g = g_ref[...]
    outs = []
    for h in range(DIFF_HEADS):
        a1, a2 = acc_sc[2 * h], acc_sc[2 * h + 1]
        o = (a1[:HEAD_DIM] / a1[HEAD_DIM:HEAD_DIM + 1]
             - lam * (a2[:HEAD_DIM] / a2[HEAD_DIM:HEAD_DIM + 1]))
        ms = jnp.mean(o * o, axis=0, keepdims=True)
        outs.append((o * lax.rsqrt(ms + NORM_EPS)) * g * (1.0 - lambda_init))
    o_ref[0] = jnp.concatenate(outs, axis=0).T.astype(BF16)


def _diff_attn_call(proj, vat, lam, g_cols, lambda_init, layer):
    b, s, _ = proj.shape
    tq = TQ
    qspec = lambda cb: pl.BlockSpec((1, tq, LANES), lambda bi, i, cb=cb: (bi, i, cb))
    last_q = s // tq - 1
    qnext = lambda cb: pl.BlockSpec((1, tq, LANES), lambda bi, i, cb=cb: (bi, jnp.minimum(i + 1, last_q), cb))
    kspec = lambda cb: pl.BlockSpec((1, s, LANES), lambda bi, i, cb=cb: (bi, 0, cb))
    n_chain = 2 * DIFF_HEADS
    return pl.pallas_call(
        functools.partial(_diff_attn_kernel, lambda_init=lambda_init, layer=layer),
        grid=(b, s // tq),
        in_specs=[
            pl.BlockSpec(memory_space=pltpu.SMEM),
            qspec(COL_Q1 // LANES), qspec(COL_Q2 // LANES),
            qnext(COL_Q1 // LANES), qnext(COL_Q2 // LANES),
            kspec(COL_K1 // LANES), kspec(COL_K2 // LANES),
            pl.BlockSpec((1, DIFF_HEADS * VT_ROWS, s), lambda bi, i: (bi, 0, 0)),
            pl.BlockSpec((None, HEAD_DIM, 1), lambda bi, i: (layer, 0, 0)),
        ],
        out_specs=pl.BlockSpec((1, tq, 256), lambda bi, i: (bi, i, 0)),
        out_shape=jax.ShapeDtypeStruct((b, s, 256), BF16),
        scratch_shapes=[
            pltpu.VMEM((n_chain, 1, tq), F32),
            pltpu.VMEM((n_chain, VT_ROWS, tq), F32),
            pltpu.VMEM((n_chain, tq, tq), F32),
        ],
        compiler_params=pltpu.CompilerParams(vmem_limit_bytes=48 * 1024 * 1024),
        name="diff_attn",
    )(lam, proj, proj, proj, proj, proj, proj, vat, g_cols)


def _band_kernel(*refs, nqb, nkb, max_dist, has_sink, want_lse, sink_base):
    it = iter(refs)
    sink_ref = next(it) if has_sink else None
    q_ref, kp_ref, kc_ref, vp_ref, vc_ref = (next(it) for _ in range(5))
    o_ref = next(it)
    lse_ref = next(it) if want_lse else None
    kbuf, vbuf = next(it), next(it)
    nseq, rows = q_ref.shape[0], q_ref.shape[1]
    i = pl.program_id(1)
    for sq in range(nseq):
        kbuf[sq, 0:BAND, :] = kp_ref[sq]
        kbuf[sq, BAND:, :] = kc_ref[sq]
        for kb in range(nkb):
            vbuf[sq, 0:BAND, kb * 256:kb * 256 + LANES] = vp_ref[sq, :, kb * LANES:(kb + 1) * LANES]
            vbuf[sq, BAND:, kb * 256:kb * 256 + LANES] = vc_ref[sq, :, kb * LANES:(kb + 1) * LANES]
            vbuf[sq, :, kb * 256 + LANES:(kb + 1) * 256] = jnp.ones((BAND + rows, LANES), BF16)
    lane = lax.broadcasted_iota(I32, (1, LANES), 1)
    lo_half = lane < 64
    r_io = lax.broadcasted_iota(I32, (BAND, 2 * BAND), 0)
    c_io = lax.broadcasted_iota(I32, (BAND, 2 * BAND), 1)
    dist = BAND + r_io - c_io
    band = (dist >= 0) & (dist <= max_dist)
    band_first = band & ((c_io >= BAND) | (i > 0))
    col0 = lax.broadcasted_iota(I32, (1, 2 * BAND), 1) == 0
    vr = lax.broadcasted_iota(I32, (2 * BAND, 2 * LANES), 0)
    vc = lax.broadcasted_iota(I32, (2 * BAND, 2 * LANES), 1)
    sink_row = (vr == 0) & (vc < LANES)
    to_log2 = HEAD_DIM ** -0.5 * math.log2(math.e)
    units = [(sq, sb, qb) for sq in range(nseq) for sb in range(rows // BAND) for qb in range(nqb)]

    def scores(u):
        sq, sb, qb = units[u]
        kb = qb if nkb > 1 else 0
        q = q_ref[sq, sb * BAND:(sb + 1) * BAND, qb * LANES:(qb + 1) * LANES].astype(F32) * to_log2
        q2 = jnp.concatenate([jnp.where(lo_half, q, 0.0), jnp.where(lo_half, 0.0, q)], axis=0).astype(BF16)
        return _dot_nt(q2, kbuf[sq, sb * BAND:(sb + 2) * BAND, kb * LANES:(kb + 1) * LANES])

    ahead = 2
    pending = [scores(u) for u in range(min(ahead, len(units)))]
    for u, (sq, sb, qb) in enumerate(units):
        if u + ahead < len(units):
            pending.append(scores(u + ahead))
        s2 = pending[u]
        pending[u] = None
        kb = qb if nkb > 1 else 0
        r0 = sb * BAND
        msk = band_first if sb == 0 else band
        halves = []
        for hh in range(2):
            if has_sink:
                fill = jnp.where(col0, sink_ref[sink_base + qb * 2 + hh] * math.log2(math.e), NEG_INF)
            else:
                fill = NEG_INF
            halves.append(jnp.where(msk, s2[hh * BAND:(hh + 1) * BAND], fill))
        s2 = jnp.concatenate(halves, axis=0)
        m = jnp.max(s2, axis=1, keepdims=True)
        p = jnp.exp2(s2 - m).astype(BF16)
        vw = vbuf[sq, r0:r0 + 2 * BAND, kb * 256:(kb + 1) * 256]
        if has_sink:
            vw = jnp.where(sink_row, jnp.zeros_like(vw), vw)
        pv = _dot(p, vw)
        den = pv[:, LANES:]
        out = pv[:, :LANES] / den
        o = jnp.where(lo_half, out[:BAND], out[BAND:])
        o_ref[sq, r0:r0 + BAND, qb * LANES:(qb + 1) * LANES] = o.astype(BF16)
        if want_lse:
            lse2 = m + jnp.log2(den)
            ls = jnp.where(lo_half, lse2[:BAND], lse2[BAND:])
            hi = ls.astype(BF16)
            lo = (ls - hi.astype(F32)).astype(BF16)
            lse_ref[sq, r0:r0 + BAND, qb * LANES:(qb + 1) * LANES] = hi
            lse_ref[sq, r0:r0 + BAND, (nqb + qb) * LANES:(nqb + qb + 1) * LANES] = lo


def _band_call(arr, q_col, k_col, v_col, nqb, nkb, max_dist, sinks=None, sink_base=0, want_lse=False,
               name="band"):
    ns, length, _ = arr.shape
    assert sinks is None or max_dist < BAND
    rows = min(BAND_ROWS, length)
    assert length % rows == 0 and BAND_ROWS % rows == 0 and ns % (BAND_ROWS // rows) == 0
    nseq = BAND_ROWS // rows
    wq, wk = nqb * LANES, nkb * LANES
    rpb = rows // BAND
    cur = lambda col, w: pl.BlockSpec((nseq, rows, w), lambda n, i, c=col // w: (n, i, c))
    prev = lambda col, w: pl.BlockSpec(
        (nseq, BAND, w), lambda n, i, c=col // w: (n, jnp.maximum(i * rpb - 1, 0), c))
    in_specs = [cur(q_col, wq), prev(k_col, wk), cur(k_col, wk), prev(v_col, wk), cur(v_col, wk)]
    args = [arr] * 5
    if sinks is not None:
        in_specs = [pl.BlockSpec(memory_space=pltpu.SMEM)] + in_specs
        args = [sinks] + args
    out_specs = [pl.BlockSpec((nseq, rows, wq), lambda n, i: (n, i, 0))]
    out_shape = [jax.ShapeDtypeStruct((ns, length, wq), BF16)]
    if want_lse:
        out_specs.append(pl.BlockSpec((nseq, rows, 2 * wq), lambda n, i: (n, i, 0)))
        out_shape.append(jax.ShapeDtypeStruct((ns, length, 2 * wq), BF16))
    return pl.pallas_call(
        functools.partial(_band_kernel, nqb=nqb, nkb=nkb, max_dist=max_dist,
                          has_sink=sinks is not None, want_lse=want_lse, sink_base=sink_base),
        grid=(ns // nseq, length // rows),
        in_specs=in_specs,
        out_specs=out_specs,
        out_shape=out_shape,
        scratch_shapes=[pltpu.VMEM((nseq, BAND + rows, wk), BF16), pltpu.VMEM((nseq, BAND + rows, 2 * wk), BF16)],
        name=name,
    )(*args)


def _outproj_router_kernel(oa_ref, ob_ref, o1_ref, l1_ref, o4_ref, l4_ref, o16_ref, l16_ref, p4t_ref, p16t_ref,
                           w_ref, x_ref, g1_ref, sc_ref, sh_ref, g_ref, wr_ref, br_ref, tri_ref,
                           xo_ref, hs_ref, meta_ref, nch_ref, wb):
    @pl.when((pl.program_id(0) == 0) & (pl.program_id(1) == 0))
    def _():
        wb[0:256, :] = w_ref[0:256, :].astype(BF16)
        for pos, head in enumerate(_SWA_HEAD_ORDER):
            wb[256 + pos * HEAD_DIM:256 + (pos + 1) * HEAD_DIM, :] = (
                w_ref[256 + head * HEAD_DIM:256 + (head + 1) * HEAD_DIM, :].astype(BF16))
        wb[768:1024, :] = w_ref[768:1024, :].astype(BF16)

    hw = o1_ref.shape[2]
    nblk = x_ref.shape[1] // TM

    def lse_of(v):
        return v[:, :hw] + v[:, hw:]

    def unpermute(pt_ref, src_ref, width):
        per = src_ref.shape[2] // nblk
        return jnp.concatenate(
            [_dot(pt_ref[...], src_ref[0, :, k * per:(k + 1) * per, :].reshape(TM, width)) for k in range(nblk)],
            axis=0)

    o1 = o1_ref[0].astype(F32)
    ls1 = lse_of(l1_ref[0].astype(F32))
    o4 = unpermute(p4t_ref, o4_ref, hw)
    ls4 = lse_of(unpermute(p4t_ref, l4_ref, 2 * hw))
    o16 = unpermute(p16t_ref, o16_ref, hw)
    ls16 = lse_of(unpermute(p16t_ref, l16_ref, 2 * hw))
    mx = jnp.maximum(jnp.maximum(ls1, ls4), ls16)
    e1, e4, e16 = jnp.exp2(ls1 - mx), jnp.exp2(ls4 - mx), jnp.exp2(ls16 - mx)
    oc = (e1 * o1 + e4 * o4 + e16 * o16) / (e1 + e4 + e16)
    mix = (_dot(oa_ref[0], wb[0:256, :]) + _dot(ob_ref[0], wb[256:768, :])
           + _dot(oc.astype(BF16), wb[768:1024, :]))
    x1 = x_ref[0] + _batch_row(g1_ref) * mix
    xo_ref[0] = x1
    _router_body(x1, sc_ref, sh_ref, g_ref, wr_ref, br_ref, tri_ref, hs_ref, meta_ref, nch_ref)


def _outproj_router_call(oa, ob, o1, l1, o4, l4, o16, l16, p4t, p16t, w_out, layer, x, mod, g, wr, br, tri):
    b, s, d = x.shape
    rows = ROWS
    nblk = rows // TM
    steps = s // rows
    nt = b * s // TM
    row = lambda w_: pl.BlockSpec((1, rows, w_), lambda bi, i: (bi, i, 0))
    res = lambda dd, w_: pl.BlockSpec((1, dd, rows // dd, w_), lambda bi, i: (bi, 0, i, 0))
    const2 = lambda bi, i: (0, 0)
    flat = lambda bi, i: bi * steps + i
    return pl.pallas_call(
        _outproj_router_kernel,
        grid=(b, steps),
        in_specs=[
            row(256), row(512), row(256), row(512),
            res(4, 256), res(4, 512), res(16, 256), res(16, 512),
            _const_spec((TM, TM), const2), _const_spec((TM, TM), const2),
            _const_spec((None, d, d), lambda bi, i: (layer, 0, 0)),
            row(d),
            _mod_spec(mod, layer, 2),
            _mod_spec(mod, layer, 4),
            _mod_spec(mod, layer, 3),
            pl.BlockSpec((None, 1, d), lambda bi, i: (layer, 0, 0)),
            _const_spec((None, 2 * ROUTER_COLS, d), lambda bi, i: (layer, 0, 0)),
            pl.BlockSpec((None, ROUTER_COLS, 1), lambda bi, i: (layer, 0, 0)),
            _const_spec((TM, TM), const2),
        ],
        out_specs=[
            row(d),
            pl.BlockSpec((nblk * SLOTS, d), lambda bi, i: (flat(bi, i), 0)),
            pl.BlockSpec((nblk, 8, TM), lambda bi, i: (flat(bi, i), 0, 0)),
            pl.BlockSpec((nblk, N_EXPERTS, LANES), lambda bi, i: (flat(bi, i), 0, 0)),
        ],
        out_shape=[
            jax.ShapeDtypeStruct((b, s, d), F32),
            jax.ShapeDtypeStruct((nt * SLOTS, d), BF16),
            jax.ShapeDtypeStruct((nt, 8, TM), F32),
            jax.ShapeDtypeStruct((nt, N_EXPERTS, LANES), I32),
        ],
        scratch_shapes=[pltpu.VMEM((d, d), BF16)],
        compiler_params=pltpu.CompilerParams(vmem_limit_bytes=56 * 1024 * 1024),
        name="out_proj_router",
    )(oa, ob, o1, l1, o4, l4, o16, l16, p4t, p16t, w_out, x, mod, mod, mod, g, wr, br, tri)


ROUTER_COLS = LANES


def _router_body(x, sc_ref, sh_ref, g_ref, wr_ref, br_ref, tri_ref, hs_ref, meta_ref, nch_ref):
    tm = TM
    h = _modulated_norm(x, g_ref[...], _batch_row(sc_ref), _batch_row(sh_ref))
    hb = h.astype(BF16)
    h_lo = (h - hb.astype(F32)).astype(BF16)
    part = _dot_nt(wr_ref[...], hb) + _dot_nt(wr_ref[...], h_lo)
    logits_t = part[:ROUTER_COLS] + part[ROUTER_COLS:] + br_ref[...]
    r8 = lax.broadcasted_iota(I32, (8, tm), 0)
    r16 = lax.broadcasted_iota(I32, (N_EXPERTS, tm), 0)
    rl = lax.broadcasted_iota(I32, (N_EXPERTS, LANES), 0)
    slot = lax.broadcasted_iota(I32, (SLOTS, tm), 0)
    for blk in range(x.shape[0] // tm):
        lt = logits_t[:, blk * tm:(blk + 1) * tm]
        glog = lt[0:8]
        elog = lt[8:8 + N_EXPERTS]

        gmax = jnp.max(glog, axis=0, keepdims=True)
        g_w = 1.0 / jnp.sum(jnp.exp(glog - gmax), axis=0, keepdims=True)
        g_idx = jnp.min(jnp.where(glog == gmax, r8, 99), axis=0, keepdims=True)

        el = jnp.where((r16 // EXPERTS_PER_GROUP) == g_idx, elog, NEG_INF)
        emax = jnp.max(el, axis=0, keepdims=True)
        e1 = jnp.min(jnp.where(el == emax, r16, 99), axis=0, keepdims=True)
        el2 = jnp.where(r16 == e1, NEG_INF, el)
        emax2 = jnp.max(el2, axis=0, keepdims=True)
        e2 = jnp.min(jnp.where(el2 == emax2, r16, 99), axis=0, keepdims=True)
        p2 = jnp.exp(emax2 - emax)
        wt1 = g_w / (1.0 + p2)
        wt2 = g_w * p2 / (1.0 + p2)

        oh1 = r16 == e1
        oh2 = r16 == e2
        onehot = jnp.where(oh1, 1.0, 0.0) + jnp.where(oh2, 1.0, 0.0)
        cnt = jnp.sum(onehot, axis=1, keepdims=True)
        nch = jnp.floor((cnt + (CHUNK - 1)) * (1.0 / CHUNK))
        nchb = jnp.broadcast_to(nch, (N_EXPERTS, LANES))
        incl = nchb
        for sft in (1, 2, 4, 8):
            incl = incl + jnp.where(rl >= sft, pltpu.roll(incl, sft, 0), 0.0)
        off = (incl - nchb)[:, 0:1] * float(CHUNK)
        rank = _dot(onehot.astype(BF16), tri_ref[...])
        slot_of = off + rank
        pos1 = jnp.sum(jnp.where(oh1, slot_of, 0.0), axis=0, keepdims=True)
        pos2 = jnp.sum(jnp.where(oh2, slot_of, 0.0), axis=0, keepdims=True)

        sel = jnp.where(slot == pos1.astype(I32), 1.0, jnp.where(slot == pos2.astype(I32), 1.0, 0.0))
        hs_ref[blk * SLOTS:(blk + 1) * SLOTS, :] = _dot(sel.astype(BF16), hb[blk * tm:(blk + 1) * tm]).astype(BF16)

        meta_ref[blk] = jnp.concatenate([pos1, pos2, wt1, wt2, jnp.zeros((4, tm), F32)], axis=0)
        nch_ref[blk] = nchb.astype(I32)


def _ffn_schedule(nch, max_tiles, dump_base):
    nt = nch.shape[0]
    cend = jnp.cumsum(nch, axis=1)
    coff = cend - nch
    tcum = jnp.cumsum(nch, axis=0)
    before = tcum - nch
    tot = tcum[-1]
    pad = ((tot + FFN_CHUNKS - 1) // FFN_CHUNKS) * FFN_CHUNKS
    eend = jnp.cumsum(pad)
    estart = eend - pad
    n_tiles = (eend[-1] // FFN_CHUNKS).astype(I32)
    first_chunk = jnp.arange(max_tiles, dtype=I32) * FFN_CHUNKS
    tile_expert = jnp.sum((eend[None, :] <= first_chunk[:, None]).astype(I32), axis=1)
    tile_expert = jnp.minimum(tile_expert, N_EXPERTS - 1)
    hp = lax.Precision.HIGHEST
    pos = jnp.arange(max_tiles * FFN_CHUNKS, dtype=I32)
    e_s = jnp.minimum(jnp.sum((eend[None, :] <= (pos // FFN_CHUNKS * FFN_CHUNKS)[:, None]).astype(I32), axis=1),
                      N_EXPERTS - 1)
    oh_e = (e_s[:, None] == jnp.arange(N_EXPERTS, dtype=I32)[None, :]).astype(F32)
    idx = pos - jnp.dot(oh_e, estart.astype(F32), precision=hp).astype(I32)
    run_end = jnp.dot(oh_e, tcum.T.astype(F32), precision=hp).astype(I32)
    run_beg = jnp.dot(oh_e, before.T.astype(F32), precision=hp).astype(I32)
    run_off = jnp.dot(oh_e, coff.T.astype(F32), precision=hp).astype(I32)
    in_run = (idx[:, None] >= run_beg) & (idx[:, None] < run_end)
    tile_base = jnp.arange(nt, dtype=I32)[None, :] * CHUNKS_PER_TILE
    src = jnp.sum(jnp.where(in_run, tile_base + run_off + idx[:, None] - run_beg, 0), axis=1)
    real = jnp.any(in_run, axis=1)
    dump = dump_base + (pos // FFN_CHUNKS % 2) * FFN_CHUNKS + pos % FFN_CHUNKS
    src_rows = jnp.where(real, src, 0) * CHUNK
    dst_rows = jnp.where(real, src, dump) * CHUNK
    used = cend[:, -1]
    ucum = jnp.cumsum(CHUNKS_PER_TILE - used)
    ubeg = ucum - (CHUNKS_PER_TILE - used)
    z = jnp.arange(max_tiles * ZERO_CHUNKS, dtype=I32)[:, None]
    in_gap = (z >= ubeg[None, :]) & (z < ucum[None, :])
    zero_rows = jnp.sum(jnp.where(in_gap, tile_base + used[None, :] + z - ubeg[None, :], 0), axis=1) * CHUNK
    n_zero = ucum[-1].astype(I32).reshape(1)
    return tile_expert, src_rows, dst_rows, n_tiles.reshape(1), zero_rows, n_zero


def _ffn_kernel(te_ref, sr_ref, dr_ref, nt_ref, zr_ref, nz_ref, hs_hbm, wg_ref, wu_ref, wd_ref, ys_hbm,
                xbuf, ybuf, zbuf, wgb, wub, wdb, in_sem, out_sem, zero_sem, *, dump_base):
    j = pl.program_id(0)
    nt = nt_ref[0]
    half_ff = EXPERT_FF // 2

    def rows_at(r):
        return pl.ds(pl.multiple_of(r, CHUNK), CHUNK)

    def in_copy(step, slot, k, wait=False):
        r = 0 if wait else sr_ref[step * FFN_CHUNKS + k]
        return pltpu.make_async_copy(hs_hbm.at[rows_at(r), :], xbuf.at[slot, pl.ds(k * CHUNK, CHUNK), :],
                                     in_sem.at[slot])

    def out_copy(step, slot, k, wait=False):
        r = 0 if wait else dr_ref[step * FFN_CHUNKS + k]
        return pltpu.make_async_copy(ybuf.at[slot, pl.ds(k * CHUNK, CHUNK), :], ys_hbm.at[rows_at(r), :],
                                     out_sem.at[slot])

    z_lo = jnp.minimum(j * ZERO_CHUNKS, nz_ref[0])
    z_hi = jnp.minimum((j + 1) * ZERO_CHUNKS, nz_ref[0])

    def zero_start(k, carry):
        pltpu.make_async_copy(zbuf.at[pl.ds(0, CHUNK), :], ys_hbm.at[rows_at(zr_ref[k]), :], zero_sem).start()
        return carry

    def zero_wait(k, carry):
        pltpu.make_async_copy(zbuf.at[pl.ds(0, CHUNK), :], ys_hbm.at[rows_at(0), :], zero_sem).wait()
        return carry

    @pl.when(j == 0)
    def _():
        zbuf[...] = jnp.zeros_like(zbuf)
        fills = [pltpu.make_async_copy(
            zbuf, ys_hbm.at[pl.ds(dump_base * CHUNK + r * ZBUF_ROWS, ZBUF_ROWS), :], zero_sem)
            for r in range(DUMP_CHUNKS * CHUNK // ZBUF_ROWS)]
        for cp in fills:
            cp.start()
        for cp in fills:
            cp.wait()

    lax.fori_loop(z_lo, z_hi, zero_start, 0)

    @pl.when(j < nt)
    def _():
        slot = j % 2

        @pl.when(j == 0)
        def _():
            for k in range(FFN_CHUNKS):
                in_copy(0, 0, k).start()

        @pl.when(j + 1 < nt)
        def _():
            for k in range(FFN_CHUNKS):
                in_copy(j + 1, 1 - slot, k).start()

        @pl.when((j == 0) | (te_ref[j] != te_ref[jnp.maximum(j - 1, 0)]))
        def _():
            wgb[...] = wg_ref[0].astype(BF16)
            wub[...] = wu_ref[0].astype(BF16)
            wdb[...] = wd_ref[0].astype(BF16)

        for k in range(FFN_CHUNKS):
            in_copy(j, slot, k, wait=True).wait()

        @pl.when(j >= 2)
        def _():
            for k in range(FFN_CHUNKS):
                out_copy(j - 2, slot, k, wait=True).wait()

        x = xbuf[slot]
        hg = [_dot(x, wgb[:, h * half_ff:(h + 1) * half_ff]) for h in range(2)]
        hu = [_dot(x, wub[:, h * half_ff:(h + 1) * half_ff]) for h in range(2)]
        y = None
        for h in range(2):
            act = ((hg[h] / (1.0 + jnp.exp(-hg[h]))) * hu[h]).astype(BF16)
            part = _dot(act, wdb[h * half_ff:(h + 1) * half_ff, :])
            y = part if y is None else y + part
        ybuf[slot] = y.astype(BF16)
        for k in range(FFN_CHUNKS):
            out_copy(j, slot, k).start()

        @pl.when(j == nt - 1)
        def _():
            for k in range(FFN_CHUNKS):
                out_copy(j, slot, k, wait=True).wait()

            @pl.when(j >= 1)
            def _():
                for k in range(FFN_CHUNKS):
                    out_copy(j - 1, 1 - slot, k, wait=True).wait()

    lax.fori_loop(z_lo, z_hi, zero_wait, 0)


def _ffn_call(tile_expert, src_rows, dst_rows, n_tiles, zero_rows, n_zero, hs, wg, wu, wd, layer, max_tiles):
    rows, d = hs.shape
    ff = wg.shape[-1]
    wmap = lambda j, te, sr, dr, nt, zr, nz: (layer, te[j], 0, 0)
    grid_spec = pltpu.PrefetchScalarGridSpec(
        num_scalar_prefetch=6,
        grid=(max_tiles,),
        in_specs=[
            pl.BlockSpec(memory_space=pl.ANY),
            pl.BlockSpec((None, 1, d, ff), wmap),
            pl.BlockSpec((None, 1, d, ff), wmap),
            pl.BlockSpec((None, 1, ff, d), wmap),
        ],
        out_specs=pl.BlockSpec(memory_space=pl.ANY),
        scratch_shapes=[
            pltpu.VMEM((2, FFN_ROWS, d), BF16),
            pltpu.VMEM((2, FFN_ROWS, d), BF16),
            pltpu.VMEM((ZBUF_ROWS, d), BF16),
            pltpu.VMEM((d, ff), BF16),
            pltpu.VMEM((d, ff), BF16),
            pltpu.VMEM((ff, d), BF16),
            pltpu.SemaphoreType.DMA((2,)),
            pltpu.SemaphoreType.DMA((2,)),
            pltpu.SemaphoreType.DMA(()),
        ],
    )
    return pl.pallas_call(
        functools.partial(_ffn_kernel, dump_base=rows // CHUNK),
        grid_spec=grid_spec,
        out_shape=jax.ShapeDtypeStruct((rows + DUMP_CHUNKS * CHUNK, d), BF16),
        compiler_params=pltpu.CompilerParams(vmem_limit_bytes=48 * 1024 * 1024),
        name="expert_ffn",
    )(tile_expert, src_rows, dst_rows, n_tiles, zero_rows, n_zero, hs, wg, wu, wd)


def _combine_body(ys_ref, meta_ref, x_ref, g2_ref):
    tm = TM
    eye = (lax.broadcasted_iota(I32, (tm, tm), 0) == lax.broadcasted_iota(I32, (tm, tm), 1))
    slot = lax.broadcasted_iota(I32, (tm, SLOTS), 1).astype(F32)

    def as_col(row):
        return jnp.sum(jnp.where(eye, row, 0.0), axis=1, keepdims=True)

    ys = []
    for blk in range(x_ref.shape[1] // tm):
        meta = meta_ref[blk]
        pos1, pos2 = as_col(meta[0:1]), as_col(meta[1:2])
        w1, w2 = as_col(meta[2:3]), as_col(meta[3:4])
        gate = jnp.where(slot == pos1, w1, 0.0) + jnp.where(slot == pos2, w2, 0.0)
        ys.append(_dot(gate.astype(BF16), ys_ref[blk * SLOTS:(blk + 1) * SLOTS, :]))
    return x_ref[0] + _batch_row(g2_ref) * jnp.concatenate(ys, axis=0)


def _combine_final_kernel(ys_ref, meta_ref, x_ref, g2_ref, gf_ref, xo_ref):
    xo = _combine_body(ys_ref, meta_ref, x_ref, g2_ref)
    xo_ref[0] = xo * lax.rsqrt(jnp.mean(xo * xo, axis=-1, keepdims=True) + NORM_EPS) * gf_ref[...]


def _combine_inproj_kernel(ys_ref, meta_ref, x_ref, g2_ref, *refs):
    n_in = 10
    xo_ref = refs[n_in]
    xo = _combine_body(ys_ref, meta_ref, x_ref, g2_ref)
    xo_ref[0] = xo
    _inproj_body(xo, *refs[:n_in], *refs[n_in + 1:])


def _combine_specs(d, steps, mod, layer):
    nblk = ROWS // TM
    flat = lambda bi, i: bi * steps + i
    return [
        pl.BlockSpec((nblk * SLOTS, d), lambda bi, i: (flat(bi, i), 0)),
        pl.BlockSpec((nblk, 8, TM), lambda bi, i: (flat(bi, i), 0, 0)),
        pl.BlockSpec((1, ROWS, d), lambda bi, i: (bi, i, 0)),
        _mod_spec(mod, layer, 5),
    ]


def _combine_final_call(ys, meta, x, mod, layer, gf):
    b, s, d = x.shape
    return pl.pallas_call(
        _combine_final_kernel,
        grid=(b, s // ROWS),
        in_specs=_combine_specs(d, s // ROWS, mod, layer) + [pl.BlockSpec((1, d), lambda bi, i: (0, 0))],
        out_specs=pl.BlockSpec((1, ROWS, d), lambda bi, i: (bi, i, 0)),
        out_shape=jax.ShapeDtypeStruct((b, s, d), F32),
        compiler_params=pltpu.CompilerParams(vmem_limit_bytes=48 * 1024 * 1024),
        name="moe_combine",
    )(ys, meta, x, mod, gf)


def _combine_inproj_call(ys, meta, x, mod, g, w_in, layer, tabs, p4, p16):
    b, s, d = x.shape
    in_specs, out_specs, out_shape = _inproj_specs(b, s, d, layer, mod)
    xspec = pl.BlockSpec((1, ROWS, d), lambda bi, i: (bi, i, 0))
    return pl.pallas_call(
        _combine_inproj_kernel,
        grid=(b, s // ROWS),
        in_specs=_combine_specs(d, s // ROWS, mod, layer - 1) + in_specs,
        out_specs=[xspec] + out_specs,
        out_shape=[jax.ShapeDtypeStruct((b, s, d), F32)] + out_shape,
        scratch_shapes=[pltpu.VMEM((d, IN_WIDTH), BF16)],
        compiler_params=pltpu.CompilerParams(vmem_limit_bytes=56 * 1024 * 1024),
        name="combine_in_proj",
    )(ys, meta, x, mod, mod, mod, g, w_in, *tabs, p4, p16)


def _rope_tables(positions):
    pos = positions.astype(F32)[..., None]

    def table(dim):
        inv = ROPE_THETA ** (-jnp.arange(0, dim, 2, dtype=F32) / dim)
        ang = pos * inv
        cos, sin = jnp.cos(ang), jnp.sin(ang)
        reps = LANES // dim
        return (jnp.tile(jnp.concatenate([cos, cos], -1), (1, 1, reps)),
                jnp.tile(jnp.concatenate([-sin, sin], -1), (1, 1, reps)))

    c64, s64 = table(HEAD_DIM)
    c32, s32 = table(DIFF_QK_DIM)
    return c64, s64, c32, s32


def kernel(x, c, positions, ada_w, ada_b, norm_mix_g, norm_ffn_g, w_in, w_out, diff_lambda_q1, diff_lambda_k1,
           diff_lambda_q2, diff_lambda_k2, diff_subln_g, swa_sinks, router_group_w, router_group_b,
           router_expert_w, router_expert_b, expert_w_gate, expert_w_up, expert_w_down, final_norm_g):
    b, s, d = x.shape
    depth = ada_w.shape[0]
    assert b <= BF16_ROWS and s % TQ == 0 and s % ROWS == 0 and s % (16 * BAND) == 0 and d == 8 * LANES
    n = b * s
    nt = n // TM
    max_tiles = (nt * CHUNKS_PER_TILE + N_EXPERTS * (FFN_CHUNKS - 1)) // FFN_CHUNKS + 1

    tabs = _rope_tables(positions)
    p4 = _residue_perm(TM, 4)
    p16 = _residue_perm(TM, 16)
    p4_b, p16_b = jnp.asarray(p4, BF16), jnp.asarray(p16, BF16)
    p4t_b, p16t_b = jnp.asarray(p4.T, BF16), jnp.asarray(p16.T, BF16)
    tri = jnp.asarray(np.triu(np.ones((TM, TM), np.float32), 1), BF16)

    c_pad = jnp.pad(c, ((0, BF16_ROWS - b), (0, 0)))
    mod = _ada_call(c_pad, ada_w, ada_b)

    sink_order = np.asarray(_SWA_HEAD_ORDER, np.int32)
    zpad = lambda k: jnp.zeros((depth, d, k), F32)
    wr = jnp.concatenate([router_group_w, zpad(8 - N_GROUPS), router_expert_w,
                          zpad(ROUTER_COLS - 8 - N_EXPERTS)], axis=-1)
    wr_hi = wr.astype(BF16)
    wr = jnp.concatenate([wr_hi, (wr - wr_hi.astype(F32)).astype(BF16)], axis=-1)
    wr = jnp.swapaxes(wr, 1, 2)
    br = jnp.concatenate([router_group_b, jnp.full((depth, 8 - N_GROUPS), NEG_INF, F32), router_expert_b,
                          jnp.zeros((depth, ROUTER_COLS - 8 - N_EXPERTS), F32)], axis=-1).reshape(depth, -1, 1)

    lam_init = [0.8 - 0.6 * math.exp(-0.3 * l) for l in range(depth)]
    lam = (jnp.exp(jnp.sum(diff_lambda_q1 * diff_lambda_k1, axis=-1))
           - jnp.exp(jnp.sum(diff_lambda_q2 * diff_lambda_k2, axis=-1)) + jnp.asarray(lam_init, F32))
    g_cols = diff_subln_g[:, :, None]
    sinks = swa_sinks[:, sink_order].reshape(-1)
    g_mix = norm_mix_g.reshape(depth, 1, d)
    g_ffn = norm_ffn_g.reshape(depth, 1, d)

    proj, vat, qkv4, qkv16 = _inproj_call(x, mod, g_mix, w_in, 0, tabs, p4_b, p16_b)
    for l in range(depth):
        oa = _diff_attn_call(proj, vat, lam, g_cols, lam_init[l], l)
        ob = _band_call(proj, COL_QB, COL_KB, COL_VB, nqb=4, nkb=1, max_dist=SWA_WINDOW - 1,
                        sinks=sinks, sink_base=l * SWA_Q_HEADS, name="swa")[0]
        o1, l1 = _band_call(proj, COL_QC, COL_KC, COL_VC, nqb=2, nkb=2, max_dist=BAND, want_lse=True,
                            name="dil1")
        o4, l4 = _band_call(qkv4.reshape(b * 4, s // 4, 768), 0, 256, 512, nqb=2, nkb=2, max_dist=BAND,
                            want_lse=True, name="dil4")
        o16, l16 = _band_call(qkv16.reshape(b * 16, s // 16, 768), 0, 256, 512, nqb=2, nkb=2, max_dist=BAND,
                              want_lse=True, name="dil16")
        x, hs, meta, nch = _outproj_router_call(
            oa, ob, o1, l1, o4.reshape(b, 4, s // 4, 256), l4.reshape(b, 4, s // 4, 512),
            o16.reshape(b, 16, s // 16, 256), l16.reshape(b, 16, s // 16, 512), p4t_b, p16t_b, w_out, l, x, mod,
            g_ffn, wr, br, tri)
        sched = _ffn_schedule(nch[:, :, 0], max_tiles, nt * CHUNKS_PER_TILE)
        ys = _ffn_call(*sched, hs, expert_w_gate, expert_w_up, expert_w_down, l, max_tiles)
        if l + 1 < depth:
            x, proj, vat, qkv4, qkv16 = _combine_inproj_call(ys, meta, x, mod, g_mix, w_in, l + 1, tabs,
                                                             p4_b, p16_b)
        else:
            x = _combine_final_call(ys, meta, x, mod, l, final_norm_g.reshape(1, d))
    return x
```

```python
import functools
import math

import numpy as np
import jax
import jax.numpy as jnp
from jax import lax
from jax.experimental import pallas as pl
from jax.experimental.pallas import tpu as pltpu

F32 = jnp.float32
BF16 = jnp.bfloat16
I32 = jnp.int32

HEAD_DIM = 64
ROPE_THETA = 10000.0
NORM_EPS = 1e-6
NEG_INF = -1e30
DIFF_HEADS = 4
DIFF_QK_DIM = 32
SWA_Q_HEADS = 8
SWA_KV_HEADS = 2
SWA_WINDOW = 128
DIL_PATTERNS = ((128, 1), (512, 4), (2048, 16))
N_GROUPS = 4
EXPERTS_PER_GROUP = 4
N_EXPERTS = 16
EXPERT_FF = 512
N_ADA = 6
IN_WIDTH = 2304

LANES = 128
BF16_ROWS = 16
BAND = 128

TM = 256
ROWS = 2 * TM
CHUNK = BF16_ROWS
SLOTS = 2 * TM + N_EXPERTS * CHUNK
CHUNKS_PER_TILE = SLOTS // CHUNK
FFN_ROWS = 1024
FFN_CHUNKS = FFN_ROWS // CHUNK
ZERO_CHUNKS = -(-N_EXPERTS * FFN_CHUNKS // CHUNKS_PER_TILE)
DUMP_CHUNKS = 2 * FFN_CHUNKS
ZBUF_ROWS = FFN_ROWS // 4
TQ = 512
BAND_ROWS = 2048

_SWA_HEAD_ORDER = (0, 4, 1, 5, 2, 6, 3, 7)
COL_QB, COL_KB, COL_VB = 0, 512, 640
COL_Q1, COL_Q2, COL_K1, COL_K2, COL_VA = 768, 896, 1024, 1152, 1280
COL_QC, COL_KC, COL_VC = 1536, 1792, 2048
_SRC_ROPE = (32, 32, 32, 32, 0, 0, 64, 64, 64, 64, 64, 0, 64, 64, 64, 64, 0, 0)
_SRC_DEST = (6, 7, 8, 9, 10, 11, None, None, None, None, 4, 5, 12, 13, 14, 15, 16, 17)
VT_ROWS = HEAD_DIM + BF16_ROWS
_SRC_VA_CHUNK = 2


def _residue_perm(tm, d):
    p = np.zeros((tm, tm), np.float32)
    per = tm // d
    for l in range(per):
        for r in range(d):
            p[r * per + l, l * d + r] = 1.0
    return p


def _dot(a, b, **kw):
    return jnp.dot(a, b, preferred_element_type=F32, **kw)


def _dot_nt(a, b):
    return lax.dot_general(a, b, (((1,), (1,)), ((), ())), preferred_element_type=F32)


def _batch_row(ref):
    return ref[pl.ds(pl.program_id(0), 1), :]


def _modulated_norm(x, g, sc, sh):
    y = x * lax.rsqrt(jnp.mean(x * x, axis=-1, keepdims=True) + NORM_EPS)
    return (y * g) * (1.0 + sc) + sh


def _ada_kernel(c_ref, w_ref, b_ref, o_ref):
    c = c_ref[...]
    ca = c / (1.0 + jnp.exp(-c))
    rows = ca.shape[0]
    c_hi = ca.astype(BF16)
    c_lo = (ca - c_hi.astype(F32)).astype(BF16)
    w = w_ref[0]
    w_hi = w.astype(BF16)
    w_lo = (w - w_hi.astype(F32)).astype(BF16)
    main = _dot(jnp.concatenate([c_hi, c_lo], axis=0), w_hi)
    o_ref[0] = main[:rows] + main[rows:] + _dot(c_hi, w_lo) + b_ref[0]


def _ada_call(c_pad, ada_w, ada_b):
    depth, d, n = ada_w.shape
    tn = n // 2
    return pl.pallas_call(
        _ada_kernel,
        grid=(depth, n // tn),
        in_specs=[
            pl.BlockSpec((c_pad.shape[0], d), lambda l, j: (0, 0)),
            pl.BlockSpec((1, d, tn), lambda l, j: (l, 0, j)),
            pl.BlockSpec((1, 1, tn), lambda l, j: (l, 0, j)),
        ],
        out_specs=pl.BlockSpec((1, c_pad.shape[0], tn), lambda l, j: (l, 0, j)),
        out_shape=jax.ShapeDtypeStruct((depth, c_pad.shape[0], n), F32),
        compiler_params=pltpu.CompilerParams(vmem_limit_bytes=40 * 1024 * 1024),
        name="ada_mod",
    )(c_pad, ada_w, ada_b.reshape(depth, 1, n))


def _rope(t, cos, sin_signed, first_half, half):
    rot = jnp.where(first_half, pltpu.roll(t, LANES - half, 1), pltpu.roll(t, half, 1))
    return t * cos + rot * sin_signed


def _inproj_body(x, sc_ref, sh_ref, g_ref, w_ref, cs64_ref, sn64_ref, cs32_ref, sn32_ref,
                 p4_ref, p16_ref, proj_ref, vat_ref, c4_ref, c16_ref, wb):
    @pl.when((pl.program_id(0) == 0) & (pl.program_id(1) == 0))
    def _():
        wb[...] = w_ref[...].astype(BF16)

    h = _modulated_norm(x, g_ref[...], _batch_row(sc_ref), _batch_row(sh_ref))
    hb = h.astype(BF16)
    lane = lax.broadcasted_iota(I32, (1, LANES), 1)
    first64 = (lane % 64) < 32
    first32 = (lane % 32) < 16
    lo_half = lane < 64
    swa_q = []
    for cb in range(IN_WIDTH // 256):
        if cb == IN_WIDTH // 256 - 1:
            half = hb.shape[0] // 2
            acc = jnp.concatenate([_dot(hb[:half], wb[:, cb * 256:(cb + 1) * 256]),
                                   _dot(hb[half:], wb[:, cb * 256:(cb + 1) * 256])], axis=0)
        else:
            acc = _dot(hb, wb[:, cb * 256:(cb + 1) * 256])
        if cb == _SRC_VA_CHUNK:
            acc_t = acc.T.astype(BF16)
            for hd in range(DIFF_HEADS):
                vat_ref[0, hd * VT_ROWS:hd * VT_ROWS + HEAD_DIM, :] = acc_t[hd * HEAD_DIM:(hd + 1) * HEAD_DIM]
                vat_ref[0, hd * VT_ROWS + HEAD_DIM:(hd + 1) * VT_ROWS, :] = jnp.ones(
                    (BF16_ROWS, acc_t.shape[1]), BF16)
        for half in range(2):
            src = cb * 2 + half
            t = acc[:, half * LANES:(half + 1) * LANES]
            if _SRC_ROPE[src] == 64:
                t = _rope(t, cs64_ref[0], sn64_ref[0], first64, 32)
            elif _SRC_ROPE[src] == 32:
                t = _rope(t, cs32_ref[0], sn32_ref[0], first32, 16)
            dst = _SRC_DEST[src]
            if dst is None:
                swa_q.append(t)
            else:
                proj_ref[0, :, dst * LANES:(dst + 1) * LANES] = t.astype(BF16)
    for jb in range(SWA_Q_HEADS // 2):
        a, c = swa_q[jb // 2], swa_q[2 + jb // 2]
        if jb % 2 == 0:
            blk = jnp.where(lo_half, a, pltpu.roll(c, 64, 1))
        else:
            blk = jnp.where(lo_half, pltpu.roll(a, 64, 1), c)
        proj_ref[0, :, jb * LANES:(jb + 1) * LANES] = blk.astype(BF16)
    tm = TM
    for blk in range(x.shape[0] // tm):
        cc = proj_ref[0, blk * tm:(blk + 1) * tm, COL_QC:]
        c4 = _dot(p4_ref[...], cc).astype(BF16)
        for r in range(4):
            c4_ref[0, r, blk * (tm // 4):(blk + 1) * (tm // 4)] = c4[r * (tm // 4):(r + 1) * (tm // 4)]
        c16 = _dot(p16_ref[...], cc).astype(BF16)
        for r in range(16):
            c16_ref[0, r, blk * (tm // 16):(blk + 1) * (tm // 16)] = c16[r * (tm // 16):(r + 1) * (tm // 16)]


def _inproj_kernel(x_ref, *refs):
    _inproj_body(x_ref[0], *refs)


def _mod_spec(mod, layer, k):
    return pl.BlockSpec((None, mod.shape[1], mod.shape[2] // N_ADA), lambda bi, i: (layer, 0, k))


def _const_spec(shape, index_map):
    return pl.BlockSpec(shape, index_map, pipeline_mode=pl.Buffered(1))


def _inproj_specs(b, s, d, layer, mod):
    rows = ROWS
    row = lambda bi, i: (bi, i, 0)
    const2 = lambda bi, i: (0, 0)
    in_specs = [
        _mod_spec(mod, layer, 1),
        _mod_spec(mod, layer, 0),
        pl.BlockSpec((None, 1, d), lambda bi, i: (layer, 0, 0)),
        _const_spec((None, d, IN_WIDTH), lambda bi, i: (layer, 0, 0)),
        pl.BlockSpec((1, rows, LANES), row),
        pl.BlockSpec((1, rows, LANES), row),
        pl.BlockSpec((1, rows, LANES), row),
        pl.BlockSpec((1, rows, LANES), row),
        _const_spec((TM, TM), const2),
        _const_spec((TM, TM), const2),
    ]
    out_specs = [
        pl.BlockSpec((1, rows, IN_WIDTH), row),
        pl.BlockSpec((1, DIFF_HEADS * VT_ROWS, rows), lambda bi, i: (bi, 0, i)),
        pl.BlockSpec((1, 4, rows // 4, 768), lambda bi, i: (bi, 0, i, 0)),
        pl.BlockSpec((1, 16, rows // 16, 768), lambda bi, i: (bi, 0, i, 0)),
    ]
    out_shape = [
        jax.ShapeDtypeStruct((b, s, IN_WIDTH), BF16),
        jax.ShapeDtypeStruct((b, DIFF_HEADS * VT_ROWS, s), BF16),
        jax.ShapeDtypeStruct((b, 4, s // 4, 768), BF16),
        jax.ShapeDtypeStruct((b, 16, s // 16, 768), BF16),
    ]
    return in_specs, out_specs, out_shape


def _inproj_call(x, mod, g, w_in, layer, tabs, p4, p16):
    b, s, d = x.shape
    in_specs, out_specs, out_shape = _inproj_specs(b, s, d, layer, mod)
    return pl.pallas_call(
        _inproj_kernel,
        grid=(b, s // ROWS),
        in_specs=[pl.BlockSpec((1, ROWS, d), lambda bi, i: (bi, i, 0))] + in_specs,
        out_specs=out_specs,
        out_shape=out_shape,
        scratch_shapes=[pltpu.VMEM((d, IN_WIDTH), BF16)],
        compiler_params=pltpu.CompilerParams(vmem_limit_bytes=56 * 1024 * 1024),
        name="in_proj",
    )(x, mod, mod, g, w_in, *tabs, p4, p16)


def _diff_attn_kernel(lam_ref, q1_ref, q2_ref, q1n_ref, q2n_ref, k1_ref, k2_ref, vt_ref, g_ref, o_ref,
                      m_sc, acc_sc, s_sc, *, lambda_init, layer):
    tq = q1_ref.shape[1]
    qi = pl.program_id(1)
    lam = lam_ref[layer]
    to_log2 = DIFF_QK_DIM ** -0.5 * math.log2(math.e)
    lane = lax.broadcasted_iota(I32, (1, LANES), 1)

    def head_queries(qa_ref, qb_ref):
        qa = qa_ref[0].astype(F32) * to_log2
        qb = qb_ref[0].astype(F32) * to_log2
        out = []
        for h in range(DIFF_HEADS):
            hm = (lane // DIFF_QK_DIM) == h
            out.append((jnp.where(hm, qa, 0.0).astype(BF16), jnp.where(hm, qb, 0.0).astype(BF16)))
        return out

    qh = head_queries(q1_ref, q2_ref)
    qh_next = head_queries(q1n_ref, q2n_ref)
    causal = (lax.broadcasted_iota(I32, (tq, tq), 0) <= lax.broadcasted_iota(I32, (tq, tq), 1))

    m_sc[...] = jnp.full(m_sc.shape, NEG_INF, F32)
    acc_sc[...] = jnp.zeros(acc_sc.shape, F32)

    n_chain = 2 * DIFF_HEADS

    def scores(ch, tile, queries):
        start = pl.multiple_of(tile * tq, tq)
        k_ref = k1_ref if ch % 2 == 0 else k2_ref
        return _dot_nt(k_ref[0, pl.ds(start, tq), :], queries[ch // 2][ch % 2])

    @pl.when(qi == 0)
    def _():
        for ch in range(n_chain):
            s_sc[ch] = scores(ch, 0, qh)

    def step(j, last):
        start = pl.multiple_of(j * tq, tq)
        for ch in range(n_chain):
            st = s_sc[ch]
            s_sc[ch] = scores(ch, 0, qh_next) if last else scores(ch, j + 1, qh)
            h = ch // 2
            vt = vt_ref[0, h * VT_ROWS:(h + 1) * VT_ROWS, pl.ds(start, tq)]
            if last:
                st = jnp.where(causal, st, NEG_INF)
            m_old = m_sc[ch]
            m_new = jnp.maximum(m_old, jnp.max(st, axis=0, keepdims=True))
            p = jnp.exp2(st - m_new)
            al = jnp.exp2(m_old - m_new)
            acc_sc[ch] = al * acc_sc[ch] + _dot(vt, p.astype(BF16))
            m_sc[ch] = m_new

    def body(j, carry):
        step(j, False)
        return carry

    lax.fori_loop(0, qi, body, 0)
    step(qi, True)

    g = g_ref[...]
    outs = []
    for h in range(DIFF_HEADS):
        a1, a2 = acc_sc[2 * h], acc_sc[2 * h + 1]
        o = (a1[:HEAD_DIM] / a1[HEAD_DIM:HEAD_DIM + 1]
             - lam * (a2[:HEAD_DIM] / a2[HEAD_DIM:HEAD_DIM + 1]))
        ms = jnp.mean(o * o, axis=0, keepdims=True)
        outs.append((o * lax.rsqrt(ms + NORM_EPS)) * g * (1.0 - lambda_init))
    o_ref[0] = jnp.concatenate(outs, axis=0).T.astype(BF16)


def _diff_attn_call(proj, vat, lam, g_cols, lambda_init, layer):
    b, s, _ = proj.shape
    tq = TQ
    qspec = lambda cb: pl.BlockSpec((1, tq, LANES), lambda bi, i, cb=cb: (bi, i, cb))
    last_q = s // tq - 1
    qnext = lambda cb: pl.BlockSpec((1, tq, LANES), lambda bi, i, cb=cb: (bi, jnp.minimum(i + 1, last_q), cb))
    kspec = lambda cb: pl.BlockSpec((1, s, LANES), lambda bi, i, cb=cb: (bi, 0, cb))
    n_chain = 2 * DIFF_HEADS
    return pl.pallas_call(
        functools.partial(_diff_attn_kernel, lambda_init=lambda_init, layer=layer),
        grid=(b, s // tq),
        in_specs=[
            pl.BlockSpec(memory_space=pltpu.SMEM),
            qspec(COL_Q1 // LANES), qspec(COL_Q2 // LANES),
            qnext(COL_Q1 // LANES), qnext(COL_Q2 // LANES),
            kspec(COL_K1 // LANES), kspec(COL_K2 // LANES),
            pl.BlockSpec((1, DIFF_HEADS * VT_ROWS, s), lambda bi, i: (bi, 0, 0)),
            pl.BlockSpec((None, HEAD_DIM, 1), lambda bi, i: (layer, 0, 0)),
        ],
        out_specs=pl.BlockSpec((1, tq, 256), lambda bi, i: (bi, i, 0)),
        out_shape=jax.ShapeDtypeStruct((b, s, 256), BF16),
        scratch_shapes=[
            pltpu.VMEM((n_chain, 1, tq), F32),
            pltpu.VMEM((n_chain, VT_ROWS, tq), F32),
            pltpu.VMEM((n_chain, tq, tq), F32),
        ],
        compiler_params=pltpu.CompilerParams(vmem_limit_bytes=48 * 1024 * 1024),
        name="diff_attn",
    )(lam, proj, proj, proj, proj, proj, proj, vat, g_cols)


def _band_kernel(*refs, nqb, nkb, max_dist, has_sink, want_lse, sink_base):
    it = iter(refs)
    sink_ref = next(it) if has_sink else None
    q_ref, kp_ref, kc_ref, vp_ref, vc_ref = (next(it) for _ in range(5))
    o_ref = next(it)
    lse_ref = next(it) if want_lse else None
    kbuf, vbuf = next(it), next(it)
    nseq, rows = q_ref.shape[0], q_ref.shape[1]
    i = pl.program_id(1)
    for sq in range(nseq):
        kbuf[sq, 0:BAND, :] = kp_ref[sq]
        kbuf[sq, BAND:, :] = kc_ref[sq]
        for kb in range(nkb):
            vbuf[sq, 0:BAND, kb * 256:kb * 256 + LANES] = vp_ref[sq, :, kb * LANES:(kb + 1) * LANES]
            vbuf[sq, BAND:, kb * 256:kb * 256 + LANES] = vc_ref[sq, :, kb * LANES:(kb + 1) * LANES]
            vbuf[sq, :, kb * 256 + LANES:(kb + 1) * 256] = jnp.ones((BAND + rows, LANES), BF16)
    lane = lax.broadcasted_iota(I32, (1, LANES), 1)
    lo_half = lane < 64
    r_io = lax.broadcasted_iota(I32, (BAND, 2 * BAND), 0)
    c_io = lax.broadcasted_iota(I32, (BAND, 2 * BAND), 1)
    dist = BAND + r_io - c_io
    band = (dist >= 0) & (dist <= max_dist)
    band_first = band & ((c_io >= BAND) | (i > 0))
    col0 = lax.broadcasted_iota(I32, (1, 2 * BAND), 1) == 0
    vr = lax.broadcasted_iota(I32, (2 * BAND, 2 * LANES), 0)
    vc = lax.broadcasted_iota(I32, (2 * BAND, 2 * LANES), 1)
    sink_row = (vr == 0) & (vc < LANES)
    to_log2 = HEAD_DIM ** -0.5 * math.log2(math.e)
    units = [(sq, sb, qb) for sq in range(nseq) for sb in range(rows // BAND) for qb in range(nqb)]

    def scores(u):
        sq, sb, qb = units[u]
        kb = qb if nkb > 1 else 0
        q = q_ref[sq, sb * BAND:(sb + 1) * BAND, qb * LANES:(qb + 1) * LANES].astype(F32) * to_log2
        q2 = jnp.concatenate([jnp.where(lo_half, q, 0.0), jnp.where(lo_half, 0.0, q)], axis=0).astype(BF16)
        return _dot_nt(q2, kbuf[sq, sb * BAND:(sb + 2) * BAND, kb * LANES:(kb + 1) * LANES])

    ahead = 2
    pending = [scores(u) for u in range(min(ahead, len(units)))]
    for u, (sq, sb, qb) in enumerate(units):
        if u + ahead < len(units):
            pending.append(scores(u + ahead))
        s2 = pending[u]
        pending[u] = None
        kb = qb if nkb > 1 else 0
        r0 = sb * BAND
        msk = band_first if sb == 0 else band
        halves = []
        for hh in range(2):
            if has_sink:
                fill = jnp.where(col0, sink_ref[sink_base + qb * 2 + hh] * math.log2(math.e), NEG_INF)
            else:
                fill = NEG_INF
            halves.append(jnp.where(msk, s2[hh * BAND:(hh + 1) * BAND], fill))
        s2 = jnp.concatenate(halves, axis=0)
        m = jnp.max(s2, axis=1, keepdims=True)
        p = jnp.exp2(s2 - m).astype(BF16)
        vw = vbuf[sq, r0:r0 + 2 * BAND, kb * 256:(kb + 1) * 256]
        if has_sink:
            vw = jnp.where(sink_row, jnp.zeros_like(vw), vw)
        pv = _dot(p, vw)
        den = pv[:, LANES:]
        out = pv[:, :LANES] / den
        o = jnp.where(lo_half, out[:BAND], out[BAND:])
        o_ref[sq, r0:r0 + BAND, qb * LANES:(qb + 1) * LANES] = o.astype(BF16)
        if want_lse:
            lse2 = m + jnp.log2(den)
            ls = jnp.where(lo_half, lse2[:BAND], lse2[BAND:])
            hi = ls.astype(BF16)
            lo = (ls - hi.astype(F32)).astype(BF16)
            lse_ref[sq, r0:r0 + BAND, qb * LANES:(qb + 1) * LANES] = hi
            lse_ref[sq, r0:r0 + BAND, (nqb + qb) * LANES:(nqb + qb + 1) * LANES] = lo


def _band_call(arr, q_col, k_col, v_col, nqb, nkb, max_dist, sinks=None, sink_base=0, want_lse=False,
               name="band"):
    ns, length, _ = arr.shape
    assert sinks is None or max_dist < BAND
    rows = min(BAND_ROWS, length)
    assert length % rows == 0 and BAND_ROWS % rows == 0 and ns % (BAND_ROWS // rows) == 0
    nseq = BAND_ROWS // rows
    wq, wk = nqb * LANES, nkb * LANES
    rpb = rows // BAND
    cur = lambda col, w: pl.BlockSpec((nseq, rows, w), lambda n, i, c=col // w: (n, i, c))
    prev = lambda col, w: pl.BlockSpec(
        (nseq, BAND, w), lambda n, i, c=col // w: (n, jnp.maximum(i * rpb - 1, 0), c))
    in_specs = [cur(q_col, wq), prev(k_col, wk), cur(k_col, wk), prev(v_col, wk), cur(v_col, wk)]
    args = [arr] * 5
    if sinks is not None:
        in_specs = [pl.BlockSpec(memory_space=pltpu.SMEM)] + in_specs
        args = [sinks] + args
    out_specs = [pl.BlockSpec((nseq, rows, wq), lambda n, i: (n, i, 0))]
    out_shape = [jax.ShapeDtypeStruct((ns, length, wq), BF16)]
    if want_lse:
        out_specs.append(pl.BlockSpec((nseq, rows, 2 * wq), lambda n, i: (n, i, 0)))
        out_shape.append(jax.ShapeDtypeStruct((ns, length, 2 * wq), BF16))
    return pl.pallas_call(
        functools.partial(_band_kernel, nqb=nqb, nkb=nkb, max_dist=max_dist,
                          has_sink=sinks is not None, want_lse=want_lse, sink_base=sink_base),
        grid=(ns // nseq, length // rows),
        in_specs=in_specs,
        out_specs=out_specs,
        out_shape=out_shape,
        scratch_shapes=[pltpu.VMEM((nseq, BAND + rows, wk), BF16), pltpu.VMEM((nseq, BAND + rows, 2 * wk), BF16)],
        name=name,
    )(*args)


def _outproj_router_kernel(oa_ref, ob_ref, o1_ref, l1_ref, o4_ref, l4_ref, o16_ref, l16_ref, p4t_ref, p16t_ref,
                           w_ref, x_ref, g1_ref, sc_ref, sh_ref, g_ref, wr_ref, br_ref, tri_ref,
                           xo_ref, hs_ref, meta_ref, nch_ref, wb):
    @pl.when((pl.program_id(0) == 0) & (pl.program_id(1) == 0))
    def _():
        wb[0:256, :] = w_ref[0:256, :].astype(BF16)
        for pos, head in enumerate(_SWA_HEAD_ORDER):
            wb[256 + pos * HEAD_DIM:256 + (pos + 1) * HEAD_DIM, :] = (
                w_ref[256 + head * HEAD_DIM:256 + (head + 1) * HEAD_DIM, :].astype(BF16))
        wb[768:1024, :] = w_ref[768:1024, :].astype(BF16)

    hw = o1_ref.shape[2]
    nblk = x_ref.shape[1] // TM

    def lse_of(v):
        return v[:, :hw] + v[:, hw:]

    def unpermute(pt_ref, src_ref, width):
        per = src_ref.shape[2] // nblk
        return jnp.concatenate(
            [_dot(pt_ref[...], src_ref[0, :, k * per:(k + 1) * per, :].reshape(TM, width)) for k in range(nblk)],
            axis=0)

    o1 = o1_ref[0].astype(F32)
    ls1 = lse_of(l1_ref[0].astype(F32))
    o4 = unpermute(p4t_ref, o4_ref, hw)
    ls4 = lse_of(unpermute(p4t_ref, l4_ref, 2 * hw))
    o16 = unpermute(p16t_ref, o16_ref, hw)
    ls16 = lse_of(unpermute(p16t_ref, l16_ref, 2 * hw))
    mx = jnp.maximum(jnp.maximum(ls1, ls4), ls16)
    e1, e4, e16 = jnp.exp2(ls1 - mx), jnp.exp2(ls4 - mx), jnp.exp2(ls16 - mx)
    oc = (e1 * o1 + e4 * o4 + e16 * o16) / (e1 + e4 + e16)
    mix = (_dot(oa_ref[0], wb[0:256, :]) + _dot(ob_ref[0], wb[256:768, :])
           + _dot(oc.astype(BF16), wb[768:1024, :]))
    x1 = x_ref[0] + _batch_row(g1_ref) * mix
    xo_ref[0] = x1
    _router_body(x1, sc_ref, sh_ref, g_ref, wr_ref, br_ref, tri_ref, hs_ref, meta_ref, nch_ref)


def _outproj_router_call(oa, ob, o1, l1, o4, l4, o16, l16, p4t, p16t, w_out, layer, x, mod, g, wr, br, tri):
    b, s, d = x.shape
    rows = ROWS
    nblk = rows // TM
    steps = s // rows
    nt = b * s // TM
    row = lambda w_: pl.BlockSpec((1, rows, w_), lambda bi, i: (bi, i, 0))
    res = lambda dd, w_: pl.BlockSpec((1, dd, rows // dd, w_), lambda bi, i: (bi, 0, i, 0))
    const2 = lambda bi, i: (0, 0)
    flat = lambda bi, i: bi * steps + i
    return pl.pallas_call(
        _outproj_router_kernel,
        grid=(b, steps),
        in_specs=[
            row(256), row(512), row(256), row(512),
            res(4, 256), res(4, 512), res(16, 256), res(16, 512),
            _const_spec((TM, TM), const2), _const_spec((TM, TM), const2),
            _const_spec((None, d, d), lambda bi, i: (layer, 0, 0)),
            row(d),
            _mod_spec(mod, layer, 2),
            _mod_spec(mod, layer, 4),
            _mod_spec(mod, layer, 3),
            pl.BlockSpec((None, 1, d), lambda bi, i: (layer, 0, 0)),
            _const_spec((None, 2 * ROUTER_COLS, d), lambda bi, i: (layer, 0, 0)),
            pl.BlockSpec((None, ROUTER_COLS, 1), lambda bi, i: (layer, 0, 0)),
            _const_spec((TM, TM), const2),
        ],
        out_specs=[
            row(d),
            pl.BlockSpec((nblk * SLOTS, d), lambda bi, i: (flat(bi, i), 0)),
            pl.BlockSpec((nblk, 8, TM), lambda bi, i: (flat(bi, i), 0, 0)),
            pl.BlockSpec((nblk, N_EXPERTS, LANES), lambda bi, i: (flat(bi, i), 0, 0)),
        ],
        out_shape=[
            jax.ShapeDtypeStruct((b, s, d), F32),
            jax.ShapeDtypeStruct((nt * SLOTS, d), BF16),
            jax.ShapeDtypeStruct((nt, 8, TM), F32),
            jax.ShapeDtypeStruct((nt, N_EXPERTS, LANES), I32),
        ],
        scratch_shapes=[pltpu.VMEM((d, d), BF16)],
        compiler_params=pltpu.CompilerParams(vmem_limit_bytes=56 * 1024 * 1024),
        name="out_proj_router",
    )(oa, ob, o1, l1, o4, l4, o16, l16, p4t, p16t, w_out, x, mod, mod, mod, g, wr, br, tri)


ROUTER_COLS = LANES


def _router_body(x, sc_ref, sh_ref, g_ref, wr_ref, br_ref, tri_ref, hs_ref, meta_ref, nch_ref):
    tm = TM
    h = _modulated_norm(x, g_ref[...], _batch_row(sc_ref), _batch_row(sh_ref))
    hb = h.astype(BF16)
    h_lo = (h - hb.astype(F32)).astype(BF16)
    part = _dot_nt(wr_ref[...], hb) + _dot_nt(wr_ref[...], h_lo)
    logits_t = part[:ROUTER_COLS] + part[ROUTER_COLS:] + br_ref[...]
    r8 = lax.broadcasted_iota(I32, (8, tm), 0)
    r16 = lax.broadcasted_iota(I32, (N_EXPERTS, tm), 0)
    rl = lax.broadcasted_iota(I32, (N_EXPERTS, LANES), 0)
    slot = lax.broadcasted_iota(I32, (SLOTS, tm), 0)
    for blk in range(x.shape[0] // tm):
        lt = logits_t[:, blk * tm:(blk + 1) * tm]
        glog = lt[0:8]
        elog = lt[8:8 + N_EXPERTS]

        gmax = jnp.max(glog, axis=0, keepdims=True)
        g_w = 1.0 / jnp.sum(jnp.exp(glog - gmax), axis=0, keepdims=True)
        g_idx = jnp.min(jnp.where(glog == gmax, r8, 99), axis=0, keepdims=True)

        el = jnp.where((r16 // EXPERTS_PER_GROUP) == g_idx, elog, NEG_INF)
        emax = jnp.max(el, axis=0, keepdims=True)
        e1 = jnp.min(jnp.where(el == emax, r16, 99), axis=0, keepdims=True)
        el2 = jnp.where(r16 == e1, NEG_INF, el)
        emax2 = jnp.max(el2, axis=0, keepdims=True)
        e2 = jnp.min(jnp.where(el2 == emax2, r16, 99), axis=0, keepdims=True)
        p2 = jnp.exp(emax2 - emax)
        wt1 = g_w / (1.0 + p2)
        wt2 = g_w * p2 / (1.0 + p2)

        oh1 = r16 == e1
        oh2 = r16 == e2
        onehot = jnp.where(oh1, 1.0, 0.0) + jnp.where(oh2, 1.0, 0.0)
        cnt = jnp.sum(onehot, axis=1, keepdims=True)
        nch = jnp.floor((cnt + (CHUNK - 1)) * (1.0 / CHUNK))
        nchb = jnp.broadcast_to(nch, (N_EXPERTS, LANES))
        incl = nchb
        for sft in (1, 2, 4, 8):
            incl = incl + jnp.where(rl >= sft, pltpu.roll(incl, sft, 0), 0.0)
        off = (incl - nchb)[:, 0:1] * float(CHUNK)
        rank = _dot(onehot.astype(BF16), tri_ref[...])
        slot_of = off + rank
        pos1 = jnp.sum(jnp.where(oh1, slot_of, 0.0), axis=0, keepdims=True)
        pos2 = jnp.sum(jnp.where(oh2, slot_of, 0.0), axis=0, keepdims=True)

        sel = jnp.where(slot == pos1.astype(I32), 1.0, jnp.where(slot == pos2.astype(I32), 1.0, 0.0))
        hs_ref[blk * SLOTS:(blk + 1) * SLOTS, :] = _dot(sel.astype(BF16), hb[blk * tm:(blk + 1) * tm]).astype(BF16)

        meta_ref[blk] = jnp.concatenate([pos1, pos2, wt1, wt2, jnp.zeros((4, tm), F32)], axis=0)
        nch_ref[blk] = nchb.astype(I32)


def _ffn_schedule(nch, max_tiles, dump_base):
    nt = nch.shape[0]
    cend = jnp.cumsum(nch, axis=1)
    coff = cend - nch
    tcum = jnp.cumsum(nch, axis=0)
    before = tcum - nch
    tot = tcum[-1]
    pad = ((tot + FFN_CHUNKS - 1) // FFN_CHUNKS) * FFN_CHUNKS
    eend = jnp.cumsum(pad)
    estart = eend - pad
    n_tiles = (eend[-1] // FFN_CHUNKS).astype(I32)
    first_chunk = jnp.arange(max_tiles, dtype=I32) * FFN_CHUNKS
    tile_expert = jnp.sum((eend[None, :] <= first_chunk[:, None]).astype(I32), axis=1)
    tile_expert = jnp.minimum(tile_expert, N_EXPERTS - 1)
    hp = lax.Precision.HIGHEST
    pos = jnp.arange(max_tiles * FFN_CHUNKS, dtype=I32)
    e_s = jnp.minimum(jnp.sum((eend[None, :] <= (pos // FFN_CHUNKS * FFN_CHUNKS)[:, None]).astype(I32), axis=1),
                      N_EXPERTS - 1)
    oh_e = (e_s[:, None] == jnp.arange(N_EXPERTS, dtype=I32)[None, :]).astype(F32)
    idx = pos - jnp.dot(oh_e, estart.astype(F32), precision=hp).astype(I32)
    run_end = jnp.dot(oh_e, tcum.T.astype(F32), precision=hp).astype(I32)
    run_beg = jnp.dot(oh_e, before.T.astype(F32), precision=hp).astype(I32)
    run_off = jnp.dot(oh_e, coff.T.astype(F32), precision=hp).astype(I32)
    in_run = (idx[:, None] >= run_beg) & (idx[:, None] < run_end)
    tile_base = jnp.arange(nt, dtype=I32)[None, :] * CHUNKS_PER_TILE
    src = jnp.sum(jnp.where(in_run, tile_base + run_off + idx[:, None] - run_beg, 0), axis=1)
    real = jnp.any(in_run, axis=1)
    dump = dump_base + (pos // FFN_CHUNKS % 2) * FFN_CHUNKS + pos % FFN_CHUNKS
    src_rows = jnp.where(real, src, 0) * CHUNK
    dst_rows = jnp.where(real, src, dump) * CHUNK
    used = cend[:, -1]
    ucum = jnp.cumsum(CHUNKS_PER_TILE - used)
    ubeg = ucum - (CHUNKS_PER_TILE - used)
    z = jnp.arange(max_tiles * ZERO_CHUNKS, dtype=I32)[:, None]
    in_gap = (z >= ubeg[None, :]) & (z < ucum[None, :])
    zero_rows = jnp.sum(jnp.where(in_gap, tile_base + used[None, :] + z - ubeg[None, :], 0), axis=1) * CHUNK
    n_zero = ucum[-1].astype(I32).reshape(1)
    return tile_expert, src_rows, dst_rows, n_tiles.reshape(1), zero_rows, n_zero


def _ffn_kernel(te_ref, sr_ref, dr_ref, nt_ref, zr_ref, nz_ref, hs_hbm, wg_ref, wu_ref, wd_ref, ys_hbm,
                xbuf, ybuf, zbuf, wgb, wub, wdb, in_sem, out_sem, zero_sem, *, dump_base):
    j = pl.program_id(0)
    nt = nt_ref[0]
    half_ff = EXPERT_FF // 2

    def rows_at(r):
        return pl.ds(pl.multiple_of(r, CHUNK), CHUNK)

    def in_copy(step, slot, k, wait=False):
        r = 0 if wait else sr_ref[step * FFN_CHUNKS + k]
        return pltpu.make_async_copy(hs_hbm.at[rows_at(r), :], xbuf.at[slot, pl.ds(k * CHUNK, CHUNK), :],
                                     in_sem.at[slot])

    def out_copy(step, slot, k, wait=False):
        r = 0 if wait else dr_ref[step * FFN_CHUNKS + k]
        return pltpu.make_async_copy(ybuf.at[slot, pl.ds(k * CHUNK, CHUNK), :], ys_hbm.at[rows_at(r), :],
                                     out_sem.at[slot])

    z_lo = jnp.minimum(j * ZERO_CHUNKS, nz_ref[0])
    z_hi = jnp.minimum((j + 1) * ZERO_CHUNKS, nz_ref[0])

    def zero_start(k, carry):
        pltpu.make_async_copy(zbuf.at[pl.ds(0, CHUNK), :], ys_hbm.at[rows_at(zr_ref[k]), :], zero_sem).start()
        return carry

    def zero_wait(k, carry):
        pltpu.make_async_copy(zbuf.at[pl.ds(0, CHUNK), :], ys_hbm.at[rows_at(0), :], zero_sem).wait()
        return carry

    @pl.when(j == 0)
    def _():
        zbuf[...] = jnp.zeros_like(zbuf)
        fills = [pltpu.make_async_copy(
            zbuf, ys_hbm.at[pl.ds(dump_base * CHUNK + r * ZBUF_ROWS, ZBUF_ROWS), :], zero_sem)
            for r in range(DUMP_CHUNKS * CHUNK // ZBUF_ROWS)]
        for cp in fills:
            cp.start()
        for cp in fills:
            cp.wait()

    lax.fori_loop(z_lo, z_hi, zero_start, 0)

    @pl.when(j < nt)
    def _():
        slot = j % 2

        @pl.when(j == 0)
        def _():
            for k in range(FFN_CHUNKS):
                in_copy(0, 0, k).start()

        @pl.when(j + 1 < nt)
        def _():
            for k in range(FFN_CHUNKS):
                in_copy(j + 1, 1 - slot, k).start()

        @pl.when((j == 0) | (te_ref[j] != te_ref[jnp.maximum(j - 1, 0)]))
        def _():
            wgb[...] = wg_ref[0].astype(BF16)
            wub[...] = wu_ref[0].astype(BF16)
            wdb[...] = wd_ref[0].astype(BF16)

        for k in range(FFN_CHUNKS):
            in_copy(j, slot, k, wait=True).wait()

        @pl.when(j >= 2)
        def _():
            for k in range(FFN_CHUNKS):
                out_copy(j - 2, slot, k, wait=True).wait()

        x = xbuf[slot]
        hg = [_dot(x, wgb[:, h * half_ff:(h + 1) * half_ff]) for h in range(2)]
        hu = [_dot(x, wub[:, h * half_ff:(h + 1) * half_ff]) for h in range(2)]
        y = None
        for h in range(2):
            act = ((hg[h] / (1.0 + jnp.exp(-hg[h]))) * hu[h]).astype(BF16)
            part = _dot(act, wdb[h * half_ff:(h + 1) * half_ff, :])
            y = part if y is None else y + part
        ybuf[slot] = y.astype(BF16)
        for k in range(FFN_CHUNKS):
            out_copy(j, slot, k).start()

        @pl.when(j == nt - 1)
        def _():
            for k in range(FFN_CHUNKS):
                out_copy(j, slot, k, wait=True).wait()

            @pl.when(j >= 1)
            def _():
                for k in range(FFN_CHUNKS):
                    out_copy(j - 1, 1 - slot, k, wait=True).wait()

    lax.fori_loop(z_lo, z_hi, zero_wait, 0)


def _ffn_call(tile_expert, src_rows, dst_rows, n_tiles, zero_rows, n_zero, hs, wg, wu, wd, layer, max_tiles):
    rows, d = hs.shape
    ff = wg.shape[-1]
    wmap = lambda j, te, sr, dr, nt, zr, nz: (layer, te[j], 0, 0)
    grid_spec = pltpu.PrefetchScalarGridSpec(
        num_scalar_prefetch=6,
        grid=(max_tiles,),
        in_specs=[
            pl.BlockSpec(memory_space=pl.ANY),
            pl.BlockSpec((None, 1, d, ff), wmap),
            pl.BlockSpec((None, 1, d, ff), wmap),
            pl.BlockSpec((None, 1, ff, d), wmap),
        ],
        out_specs=pl.BlockSpec(memory_space=pl.ANY),
        scratch_shapes=[
            pltpu.VMEM((2, FFN_ROWS, d), BF16),
            pltpu.VMEM((2, FFN_ROWS, d), BF16),
            pltpu.VMEM((ZBUF_ROWS, d), BF16),
            pltpu.VMEM((d, ff), BF16),
            pltpu.VMEM((d, ff), BF16),
            pltpu.VMEM((ff, d), BF16),
            pltpu.SemaphoreType.DMA((2,)),
            pltpu.SemaphoreType.DMA((2,)),
            pltpu.SemaphoreType.DMA(()),
        ],
    )
    return pl.pallas_call(
        functools.partial(_ffn_kernel, dump_base=rows // CHUNK),
        grid_spec=grid_spec,
        out_shape=jax.ShapeDtypeStruct((rows + DUMP_CHUNKS * CHUNK, d), BF16),
        compiler_params=pltpu.CompilerParams(vmem_limit_bytes=48 * 1024 * 1024),
        name="expert_ffn",
    )(tile_expert, src_rows, dst_rows, n_tiles, zero_rows, n_zero, hs, wg, wu, wd)


def _combine_body(ys_ref, meta_ref, x_ref, g2_ref):
    tm = TM
    eye = (lax.broadcasted_iota(I32, (tm, tm), 0) == lax.broadcasted_iota(I32, (tm, tm), 1))
    slot = lax.broadcasted_iota(I32, (tm, SLOTS), 1).astype(F32)

    def as_col(row):
        return jnp.sum(jnp.where(eye, row, 0.0), axis=1, keepdims=True)

    ys = []
    for blk in range(x_ref.shape[1] // tm):
        meta = meta_ref[blk]
        pos1, pos2 = as_col(meta[0:1]), as_col(meta[1:2])
        w1, w2 = as_col(meta[2:3]), as_col(meta[3:4])
        gate = jnp.where(slot == pos1, w1, 0.0) + jnp.where(slot == pos2, w2, 0.0)
        ys.append(_dot(gate.astype(BF16), ys_ref[blk * SLOTS:(blk + 1) * SLOTS, :]))
    return x_ref[0] + _batch_row(g2_ref) * jnp.concatenate(ys, axis=0)


def _combine_final_kernel(ys_ref, meta_ref, x_ref, g2_ref, gf_ref, xo_ref):
    xo = _combine_body(ys_ref, meta_ref, x_ref, g2_ref)
    xo_ref[0] = xo * lax.rsqrt(jnp.mean(xo * xo, axis=-1, keepdims=True) + NORM_EPS) * gf_ref[...]


def _combine_inproj_kernel(ys_ref, meta_ref, x_ref, g2_ref, *refs):
    n_in = 10
    xo_ref = refs[n_in]
    xo = _combine_body(ys_ref, meta_ref, x_ref, g2_ref)
    xo_ref[0] = xo
    _inproj_body(xo, *refs[:n_in], *refs[n_in + 1:])


def _combine_specs(d, steps, mod, layer):
    nblk = ROWS // TM
    flat = lambda bi, i: bi * steps + i
    return [
        pl.BlockSpec((nblk * SLOTS, d), lambda bi, i: (flat(bi, i), 0)),
        pl.BlockSpec((nblk, 8, TM), lambda bi, i: (flat(bi, i), 0, 0)),
        pl.BlockSpec((1, ROWS, d), lambda bi, i: (bi, i, 0)),
        _mod_spec(mod, layer, 5),
    ]


def _combine_final_call(ys, meta, x, mod, layer, gf):
    b, s, d = x.shape
    return pl.pallas_call(
        _combine_final_kernel,
        grid=(b, s // ROWS),
        in_specs=_combine_specs(d, s // ROWS, mod, layer) + [pl.BlockSpec((1, d), lambda bi, i: (0, 0))],
        out_specs=pl.BlockSpec((1, ROWS, d), lambda bi, i: (bi, i, 0)),
        out_shape=jax.ShapeDtypeStruct((b, s, d), F32),
        compiler_params=pltpu.CompilerParams(vmem_limit_bytes=48 * 1024 * 1024),
        name="moe_combine",
    )(ys, meta, x, mod, gf)


def _combine_inproj_call(ys, meta, x, mod, g, w_in, layer, tabs, p4, p16):
    b, s, d = x.shape
    in_specs, out_specs, out_shape = _inproj_specs(b, s, d, layer, mod)
    xspec = pl.BlockSpec((1, ROWS, d), lambda bi, i: (bi, i, 0))
    return pl.pallas_call(
        _combine_inproj_kernel,
        grid=(b, s // ROWS),
        in_specs=_combine_specs(d, s // ROWS, mod, layer - 1) + in_specs,
        out_specs=[xspec] + out_specs,
        out_shape=[jax.ShapeDtypeStruct((b, s, d), F32)] + out_shape,
        scratch_shapes=[pltpu.VMEM((d, IN_WIDTH), BF16)],
        compiler_params=pltpu.CompilerParams(vmem_limit_bytes=56 * 1024 * 1024),
        name="combine_in_proj",
    )(ys, meta, x, mod, mod, mod, g, w_in, *tabs, p4, p16)


def _rope_tables(positions):
    pos = positions.astype(F32)[..., None]

    def table(dim):
        inv = ROPE_THETA ** (-jnp.arange(0, dim, 2, dtype=F32) / dim)
        ang = pos * inv
        cos, sin = jnp.cos(ang), jnp.sin(ang)
        reps = LANES // dim
        return (jnp.tile(jnp.concatenate([cos, cos], -1), (1, 1, reps)),
                jnp.tile(jnp.concatenate([-sin, sin], -1), (1, 1, reps)))

    c64, s64 = table(HEAD_DIM)
    c32, s32 = table(DIFF_QK_DIM)
    return c64, s64, c32, s32


def kernel(x, c, positions, ada_w, ada_b, norm_mix_g, norm_ffn_g, w_in, w_out, diff_lambda_q1, diff_lambda_k1,
           diff_lambda_q2, diff_lambda_k2, diff_subln_g, swa_sinks, router_group_w, router_group_b,
           router_expert_w, router_expert_b, expert_w_gate, expert_w_up, expert_w_down, final_norm_g):
    b, s, d = x.shape
    depth = ada_w.shape[0]
    assert b <= BF16_ROWS and s % TQ == 0 and s % ROWS == 0 and s % (16 * BAND) == 0 and d == 8 * LANES
    n = b * s
    nt = n // TM
    max_tiles = (nt * CHUNKS_PER_TILE + N_EXPERTS * (FFN_CHUNKS - 1)) // FFN_CHUNKS + 1

    tabs = _rope_tables(positions)
    p4 = _residue_perm(TM, 4)
    p16 = _residue_perm(TM, 16)
    p4_b, p16_b = jnp.asarray(p4, BF16), jnp.asarray(p16, BF16)
    p4t_b, p16t_b = jnp.asarray(p4.T, BF16), jnp.asarray(p16.T, BF16)
    tri = jnp.asarray(np.triu(np.ones((TM, TM), np.float32), 1), BF16)

    c_pad = jnp.pad(c, ((0, BF16_ROWS - b), (0, 0)))
    mod = _ada_call(c_pad, ada_w, ada_b)

    sink_order = np.asarray(_SWA_HEAD_ORDER, np.int32)
    zrows = lambda k: jnp.zeros((depth, k, d), F32)
    wr = jnp.concatenate([jnp.swapaxes(router_group_w, 1, 2), zrows(8 - N_GROUPS),
                          jnp.swapaxes(router_expert_w, 1, 2), zrows(ROUTER_COLS - 8 - N_EXPERTS)], axis=1)
    wr_hi = wr.astype(BF16)
    wr = jnp.concatenate([wr_hi, (wr - wr_hi.astype(F32)).astype(BF16)], axis=1)
    br =jnp.concatenate([router_group_b, jnp.full((depth, 8 - N_GROUPS), NEG_INF, F32), router_expert_b,
                          jnp.zeros((depth, ROUTER_COLS - 8 - N_EXPERTS), F32)], axis=-1).reshape(depth, -1, 1)

    lam_init = [0.8 - 0.6 * math.exp(-0.3 * l) for l in range(depth)]
    lam = (jnp.exp(jnp.sum(diff_lambda_q1 * diff_lambda_k1, axis=-1))
           - jnp.exp(jnp.sum(diff_lambda_q2 * diff_lambda_k2, axis=-1)) + jnp.asarray(lam_init, F32))
    g_cols = diff_subln_g[:, :, None]
    sinks = swa_sinks[:, sink_order].reshape(-1)
    g_mix = norm_mix_g.reshape(depth, 1, d)
    g_ffn = norm_ffn_g.reshape(depth, 1, d)

    proj, vat, qkv4, qkv16 = _inproj_call(x, mod, g_mix, w_in, 0, tabs, p4_b, p16_b)
    for l in range(depth):
        oa = _diff_attn_call(proj, vat, lam, g_cols, lam_init[l], l)
        ob = _band_call(proj, COL_QB, COL_KB, COL_VB, nqb=4, nkb=1, max_dist=SWA_WINDOW - 1,
                        sinks=sinks, sink_base=l * SWA_Q_HEADS, name="swa")[0]
        o1, l1 = _band_call(proj, COL_QC, COL_KC, COL_VC, nqb=2, nkb=2, max_dist=BAND, want_lse=True,
                            name="dil1")
        o4, l4 = _band_call(qkv4.reshape(b * 4, s // 4, 768), 0, 256, 512, nqb=2, nkb=2, max_dist=BAND,
                            want_lse=True, name="dil4")
        o16, l16 = _band_call(qkv16.reshape(b * 16, s // 16, 768), 0, 256, 512, nqb=2, nkb=2, max_dist=BAND,
                              want_lse=True, name="dil16")
        x, hs, meta, nch = _outproj_router_call(
            oa, ob, o1, l1, o4.reshape(b, 4, s // 4, 256), l4.reshape(b, 4, s // 4, 512),
            o16.reshape(b, 16, s // 16, 256), l16.reshape(b, 16, s // 16, 512), p4t_b, p16t_b, w_out, l, x, mod,
            g_ffn, wr, br, tri)
        sched = _ffn_schedule(nch[:, :, 0], max_tiles, nt * CHUNKS_PER_TILE)
        ys = _ffn_call(*sched, hs, expert_w_gate, expert_w_up, expert_w_down, l, max_tiles)
        if l + 1 < depth:
            x, proj, vat, qkv4, qkv16 = _combine_inproj_call(ys, meta, x, mod, g_mix, w_in, l + 1, tabs,
                                                             p4_b, p16_b)
        else:
            x = _combine_final_call(ys, meta, x, mod, l, final_norm_g.reshape(1, d))
    return x
```

```python
import functools
import math

import numpy as np
import jax
import jax.numpy as jnp
from jax import lax
from jax.experimental import pallas as pl
from jax.experimental.pallas import tpu as pltpu

F32 = jnp.float32
BF16 = jnp.bfloat16
I32 = jnp.int32

HEAD_DIM = 64
ROPE_THETA = 10000.0
NORM_EPS = 1e-6
NEG_INF = -1e30
DIFF_HEADS = 4
DIFF_QK_DIM = 32
SWA_Q_HEADS = 8
SWA_KV_HEADS = 2
SWA_WINDOW = 128
DIL_PATTERNS = ((128, 1), (512, 4), (2048, 16))
N_GROUPS = 4
EXPERTS_PER_GROUP = 4
N_EXPERTS = 16
EXPERT_FF = 512
N_ADA = 6
IN_WIDTH = 2304

LANES = 128
BF16_ROWS = 16
BAND = 128

TM = 256
ROWS = 2 * TM
CHUNK = BF16_ROWS
SLOTS = 2 * TM + N_EXPERTS * CHUNK
CHUNKS_PER_TILE = SLOTS // CHUNK
FFN_ROWS = 512
FFN_CHUNKS = FFN_ROWS // CHUNK
ZERO_CHUNKS = -(-N_EXPERTS * FFN_CHUNKS // CHUNKS_PER_TILE)
DUMP_CHUNKS = 2 * FFN_CHUNKS
ZBUF_ROWS = FFN_ROWS // 4
TQ = 512
BAND_ROWS = 2048

_SWA_HEAD_ORDER = (0, 4, 1, 5, 2, 6, 3, 7)
COL_QB, COL_KB, COL_VB = 0, 512, 640
COL_Q1, COL_Q2, COL_K1, COL_K2, COL_VA = 768, 896, 1024, 1152, 1280
COL_QC, COL_KC, COL_VC = 1536, 1792, 2048
_SRC_ROPE = (32, 32, 32, 32, 0, 0, 64, 64, 64, 64, 64, 0, 64, 64, 64, 64, 0, 0)
_SRC_DEST = (6, 7, 8, 9, 10, 11, None, None, None, None, 4, 5, 12, 13, 14, 15, 16, 17)
VT_ROWS = HEAD_DIM + BF16_ROWS
_SRC_VA_CHUNK = 2


def _residue_perm(tm, d):
    p = np.zeros((tm, tm), np.float32)
    per = tm // d
    for l in range(per):
        for r in range(d):
            p[r * per + l, l * d + r] = 1.0
    return p


def _dot(a, b, **kw):
    return jnp.dot(a, b, preferred_element_type=F32, **kw)


def _dot_nt(a, b):
    return lax.dot_general(a, b, (((1,), (1,)), ((), ())), preferred_element_type=F32)


def _batch_row(ref):
    return ref[pl.ds(pl.program_id(0), 1), :]


def _modulated_norm(x, g, sc, sh):
    y = x * lax.rsqrt(jnp.mean(x * x, axis=-1, keepdims=True) + NORM_EPS)
    return (y * g) * (1.0 + sc) + sh


def _ada_kernel(c_ref, w_ref, b_ref, o_ref):
    c = c_ref[...]
    ca = c / (1.0 + jnp.exp(-c))
    rows = ca.shape[0]
    c_hi = ca.astype(BF16)
    c_lo = (ca - c_hi.astype(F32)).astype(BF16)
    w = w_ref[0]
    w_hi = w.astype(BF16)
    w_lo = (w - w_hi.astype(F32)).astype(BF16)
    main = _dot(jnp.concatenate([c_hi, c_lo], axis=0), w_hi)
    o_ref[0] = main[:rows] + main[rows:] + _dot(c_hi, w_lo) + b_ref[0]


def _ada_call(c_pad, ada_w, ada_b):
    depth, d, n = ada_w.shape
    tn = n // 2
    return pl.pallas_call(
        _ada_kernel,
        grid=(depth, n // tn),
        in_specs=[
            pl.BlockSpec((c_pad.shape[0], d), lambda l, j: (0, 0)),
            pl.BlockSpec((1, d, tn), lambda l, j: (l, 0, j)),
            pl.BlockSpec((1, 1, tn), lambda l, j: (l, 0, j)),
        ],
        out_specs=pl.BlockSpec((1, c_pad.shape[0], tn), lambda l, j: (l, 0, j)),
        out_shape=jax.ShapeDtypeStruct((depth, c_pad.shape[0], n), F32),
        compiler_params=pltpu.CompilerParams(vmem_limit_bytes=40 * 1024 * 1024),
        name="ada_mod",
    )(c_pad, ada_w, ada_b.reshape(depth, 1, n))


def _rope(t, cos, sin_signed, first_half, half):
    rot = jnp.where(first_half, pltpu.roll(t, LANES - half, 1), pltpu.roll(t, half, 1))
    return t * cos + rot * sin_signed


def _inproj_body(x, sc_ref, sh_ref, g_ref, w_ref, cs64_ref, sn64_ref, cs32_ref, sn32_ref,
                 p4_ref, p16_ref, proj_ref, vat_ref, c4_ref, c16_ref, wb):
    @pl.when((pl.program_id(0) == 0) & (pl.program_id(1) == 0))
    def _():
        wb[...] = w_ref[...].astype(BF16)

    h = _modulated_norm(x, g_ref[...], _batch_row(sc_ref), _batch_row(sh_ref))
    hb = h.astype(BF16)
    lane = lax.broadcasted_iota(I32, (1, LANES), 1)
    first64 = (lane % 64) < 32
    first32 = (lane % 32) < 16
    lo_half = lane < 64
    swa_q = []
    for cb in range(IN_WIDTH // 256):
        if cb == IN_WIDTH // 256 - 1:
            half = hb.shape[0] // 2
            acc = jnp.concatenate([_dot(hb[:half], wb[:, cb * 256:(cb + 1) * 256]),
                                   _dot(hb[half:], wb[:, cb * 256:(cb + 1) * 256])], axis=0)
        else:
            acc = _dot(hb, wb[:, cb * 256:(cb + 1) * 256])
        if cb == _SRC_VA_CHUNK:
            acc_t = acc.T.astype(BF16)
            for hd in range(DIFF_HEADS):
                vat_ref[0, hd * VT_ROWS:hd * VT_ROWS + HEAD_DIM, :] = acc_t[hd * HEAD_DIM:(hd + 1) * HEAD_DIM]
                vat_ref[0, hd * VT_ROWS + HEAD_DIM:(hd + 1) * VT_ROWS, :] = jnp.ones(
                    (BF16_ROWS, acc_t.shape[1]), BF16)
        for half in range(2):
            src = cb * 2 + half
            t = acc[:, half * LANES:(half + 1) * LANES]
            if _SRC_ROPE[src] == 64:
                t = _rope(t, cs64_ref[0], sn64_ref[0], first64, 32)
            elif _SRC_ROPE[src] == 32:
                t = _rope(t, cs32_ref[0], sn32_ref[0], first32, 16)
            dst = _SRC_DEST[src]
            if dst is None:
                swa_q.append(t)
            else:
                proj_ref[0, :, dst * LANES:(dst + 1) * LANES] = t.astype(BF16)
    for jb in range(SWA_Q_HEADS // 2):
        a, c = swa_q[jb // 2], swa_q[2 + jb // 2]
        if jb % 2 == 0:
            blk = jnp.where(lo_half, a, pltpu.roll(c, 64, 1))
        else:
            blk = jnp.where(lo_half, pltpu.roll(a, 64, 1), c)
        proj_ref[0, :, jb * LANES:(jb + 1) * LANES] = blk.astype(BF16)
    tm = TM
    for blk in range(x.shape[0] // tm):
        cc = proj_ref[0, blk * tm:(blk + 1) * tm, COL_QC:]
        c4 = _dot(p4_ref[...], cc).astype(BF16)
        for r in range(4):
            c4_ref[0, r, blk * (tm // 4):(blk + 1) * (tm // 4)] = c4[r * (tm // 4):(r + 1) * (tm // 4)]
        c16 = _dot(p16_ref[...], cc).astype(BF16)
        for r in range(16):
            c16_ref[0, r, blk * (tm // 16):(blk + 1) * (tm // 16)] = c16[r * (tm // 16):(r + 1) * (tm // 16)]


def _inproj_kernel(x_ref, *refs):
    _inproj_body(x_ref[0], *refs)


def _mod_spec(mod, layer, k):
    return pl.BlockSpec((None, mod.shape[1], mod.shape[2] // N_ADA), lambda bi, i: (layer, 0, k))


def _const_spec(shape, index_map):
    return pl.BlockSpec(shape, index_map, pipeline_mode=pl.Buffered(1))


def _inproj_specs(b, s, d, layer, mod):
    rows = ROWS
    row = lambda bi, i: (bi, i, 0)
    const2 = lambda bi, i: (0, 0)
    in_specs = [
        _mod_spec(mod, layer, 1),
        _mod_spec(mod, layer, 0),
        pl.BlockSpec((None, 1, d), lambda bi, i: (layer, 0, 0)),
        _const_spec((None, d, IN_WIDTH), lambda bi, i: (layer, 0, 0)),
        pl.BlockSpec((1, rows, LANES), row),
        pl.BlockSpec((1, rows, LANES), row),
        pl.BlockSpec((1, rows, LANES), row),
        pl.BlockSpec((1, rows, LANES), row),
        _const_spec((TM, TM), const2),
        _const_spec((TM, TM), const2),
    ]
    out_specs = [
        pl.BlockSpec((1, rows, IN_WIDTH), row),
        pl.BlockSpec((1, DIFF_HEADS * VT_ROWS, rows), lambda bi, i: (bi, 0, i)),
        pl.BlockSpec((1, 4, rows // 4, 768), lambda bi, i: (bi, 0, i, 0)),
        pl.BlockSpec((1, 16, rows // 16, 768), lambda bi, i: (bi, 0, i, 0)),
    ]
    out_shape = [
        jax.ShapeDtypeStruct((b, s, IN_WIDTH), BF16),
        jax.ShapeDtypeStruct((b, DIFF_HEADS * VT_ROWS, s), BF16),
        jax.ShapeDtypeStruct((b, 4, s // 4, 768), BF16),
        jax.ShapeDtypeStruct((b, 16, s // 16, 768), BF16),
    ]
    return in_specs, out_specs, out_shape


def _inproj_call(x, mod, g, w_in, layer, tabs, p4, p16):
    b, s, d = x.shape
    in_specs, out_specs, out_shape = _inproj_specs(b, s, d, layer, mod)
    return pl.pallas_call(
        _inproj_kernel,
        grid=(b, s // ROWS),
        in_specs=[pl.BlockSpec((1, ROWS, d), lambda bi, i: (bi, i, 0))] + in_specs,
        out_specs=out_specs,
        out_shape=out_shape,
        scratch_shapes=[pltpu.VMEM((d, IN_WIDTH), BF16)],
        compiler_params=pltpu.CompilerParams(vmem_limit_bytes=56 * 1024 * 1024),
        name="in_proj",
    )(x, mod, mod, g, w_in, *tabs, p4, p16)


def _diff_attn_kernel(lam_ref, q1_ref, q2_ref, q1n_ref, q2n_ref, k1_ref, k2_ref, vt_ref, g_ref, o_ref,
                      m_sc, acc_sc, s_sc, *, lambda_init, layer):
    tq = q1_ref.shape[1]
    qi = pl.program_id(1)
    lam = lam_ref[layer]
    to_log2 = DIFF_QK_DIM ** -0.5 * math.log2(math.e)
    lane = lax.broadcasted_iota(I32, (1, LANES), 1)

    def head_queries(qa_ref, qb_ref):
        qa = qa_ref[0].astype(F32) * to_log2
        qb = qb_ref[0].astype(F32) * to_log2
        out = []
        for h in range(DIFF_HEADS):
            hm = (lane // DIFF_QK_DIM) == h
            out.append((jnp.where(hm, qa, 0.0).astype(BF16), jnp.where(hm, qb, 0.0).astype(BF16)))
        return out

    qh = head_queries(q1_ref, q2_ref)
    qh_next = head_queries(q1n_ref, q2n_ref)
    causal = (lax.broadcasted_iota(I32, (tq, tq), 0) <= lax.broadcasted_iota(I32, (tq, tq), 1))

    m_sc[...] = jnp.full(m_sc.shape, NEG_INF, F32)
    acc_sc[...] = jnp.zeros(acc_sc.shape, F32)

    n_chain = 2 * DIFF_HEADS

    def scores(ch, tile, queries):
        start = pl.multiple_of(tile * tq, tq)
        k_ref = k1_ref if ch % 2 == 0 else k2_ref
        return _dot_nt(k_ref[0, pl.ds(start, tq), :], queries[ch // 2][ch % 2])

    @pl.when(qi == 0)
    def _():
        for ch in range(n_chain):
            s_sc[ch] = scores(ch, 0, qh)

    def step(j, last):
        start = pl.multiple_of(j * tq, tq)
        for ch in range(n_chain):
            st = s_sc[ch]
            s_sc[ch] = scores(ch, 0, qh_next) if last else scores(ch, j + 1, qh)
            h = ch // 2
            vt = vt_ref[0, h * VT_ROWS:(h + 1) * VT_ROWS, pl.ds(start, tq)]
            if last:
                st = jnp.where(causal, st, NEG_INF)
            m_old = m_sc[ch]
            m_new = jnp.maximum(m_old, jnp.max(st, axis=0, keepdims=True))
            p = jnp.exp2(st - m_new)
            al = jnp.exp2(m_old - m_new)
            acc_sc[ch] = al * acc_sc[ch] + _dot(vt, p.astype(BF16))
            m_sc[ch] = m_new

    def body(j, carry):
        step(j, False)
        return carry

    lax.fori_loop(0, qi, body, 0)
    step(qi, True)

    g = g_ref[...]
    outs = []
    for h in range(DIFF_HEADS):
        a1, a2 = acc_sc[2 * h], acc_sc[2 * h + 1]
        o = (a1[:HEAD_DIM] / a1[HEAD_DIM:HEAD_DIM + 1]
             - lam * (a2[:HEAD_DIM] / a2[HEAD_DIM:HEAD_DIM + 1]))
        ms = jnp.mean(o * o, axis=0, keepdims=True)
        outs.append((o * lax.rsqrt(ms + NORM_EPS)) * g * (1.0 - lambda_init))
    o_ref[0] = jnp.concatenate(outs, axis=0).T.astype(BF16)


def _diff_attn_call(proj, vat, lam, g_cols, lambda_init, layer):
    b, s, _ = proj.shape
    tq = TQ
    qspec = lambda cb: pl.BlockSpec((1, tq, LANES), lambda bi, i, cb=cb: (bi, i, cb))
    last_q = s // tq - 1
    qnext = lambda cb: pl.BlockSpec((1, tq, LANES), lambda bi, i, cb=cb: (bi, jnp.minimum(i + 1, last_q), cb))
    kspec = lambda cb: pl.BlockSpec((1, s, LANES), lambda bi, i, cb=cb: (bi, 0, cb))
    n_chain = 2 * DIFF_HEADS
    return pl.pallas_call(
        functools.partial(_diff_attn_kernel, lambda_init=lambda_init, layer=layer),
        grid=(b, s // tq),
        in_specs=[
            pl.BlockSpec(memory_space=pltpu.SMEM),
            qspec(COL_Q1 // LANES), qspec(COL_Q2 // LANES),
            qnext(COL_Q1 // LANES), qnext(COL_Q2 // LANES),
            kspec(COL_K1 // LANES), kspec(COL_K2 // LANES),
            pl.BlockSpec((1, DIFF_HEADS * VT_ROWS, s), lambda bi, i: (bi, 0, 0)),
            pl.BlockSpec((None, HEAD_DIM, 1), lambda bi, i: (layer, 0, 0)),
        ],
        out_specs=pl.BlockSpec((1, tq, 256), lambda bi, i: (bi, i, 0)),
        out_shape=jax.ShapeDtypeStruct((b, s, 256), BF16),
        scratch_shapes=[
            pltpu.VMEM((n_chain, 1, tq), F32),
            pltpu.VMEM((n_chain, VT_ROWS, tq), F32),
            pltpu.VMEM((n_chain, tq, tq), F32),
        ],
        compiler_params=pltpu.CompilerParams(vmem_limit_bytes=48 * 1024 * 1024),
        name="diff_attn",
    )(lam, proj, proj, proj, proj, proj, proj, vat, g_cols)


def _band_kernel(*refs, nqb, nkb, max_dist, has_sink, want_lse, sink_base):
    it = iter(refs)
    sink_ref = next(it) if has_sink else None
    q_ref, kp_ref, kc_ref, vp_ref, vc_ref = (next(it) for _ in range(5))
    o_ref = next(it)
    lse_ref = next(it) if want_lse else None
    kbuf, vbuf = next(it), next(it)
    nseq, rows = q_ref.shape[0], q_ref.shape[1]
    i = pl.program_id(1)
    for sq in range(nseq):
        kbuf[sq, 0:BAND, :] = kp_ref[sq]
        kbuf[sq, BAND:, :] = kc_ref[sq]
        for kb in range(nkb):
            vbuf[sq, 0:BAND, kb * 256:kb * 256 + LANES] = vp_ref[sq, :, kb * LANES:(kb + 1) * LANES]
            vbuf[sq, BAND:, kb * 256:kb * 256 + LANES] = vc_ref[sq, :, kb * LANES:(kb + 1) * LANES]
            vbuf[sq, :, kb * 256 + LANES:(kb + 1) * 256] = jnp.ones((BAND + rows, LANES), BF16)
    lane = lax.broadcasted_iota(I32, (1, LANES), 1)
    lo_half = lane < 64
    r_io = lax.broadcasted_iota(I32, (BAND, 2 * BAND), 0)
    c_io = lax.broadcasted_iota(I32, (BAND, 2 * BAND), 1)
    dist = BAND + r_io - c_io
    band = (dist >= 0) & (dist <= max_dist)
    band_first = band & ((c_io >= BAND) | (i > 0))
    col0 = lax.broadcasted_iota(I32, (1, 2 * BAND), 1) == 0
    vr = lax.broadcasted_iota(I32, (2 * BAND, 2 * LANES), 0)
    vc = lax.broadcasted_iota(I32, (2 * BAND, 2 * LANES), 1)
    sink_row = (vr == 0) & (vc < LANES)
    to_log2 = HEAD_DIM ** -0.5 * math.log2(math.e)
    units = [(sq, sb, qb) for sq in range(nseq) for sb in range(rows // BAND) for qb in range(nqb)]

    def scores(u):
        sq, sb, qb = units[u]
        kb = qb if nkb > 1 else 0
        q = q_ref[sq, sb * BAND:(sb + 1) * BAND, qb * LANES:(qb + 1) * LANES].astype(F32) * to_log2
        q2 = jnp.concatenate([jnp.where(lo_half, q, 0.0), jnp.where(lo_half, 0.0, q)], axis=0).astype(BF16)
        return _dot_nt(q2, kbuf[sq, sb * BAND:(sb + 2) * BAND, kb * LANES:(kb + 1) * LANES])

    ahead = 2
    pending = [scores(u) for u in range(min(ahead, len(units)))]
    for u, (sq, sb, qb) in enumerate(units):
        if u + ahead < len(units):
            pending.append(scores(u + ahead))
        s2 = pending[u]
        pending[u] = None
        kb = qb if nkb > 1 else 0
        r0 = sb * BAND
        msk = band_first if sb == 0 else band
        halves = []
        for hh in range(2):
            if has_sink:
                fill = jnp.where(col0, sink_ref[sink_base + qb * 2 + hh] * math.log2(math.e), NEG_INF)
            else:
                fill = NEG_INF
            halves.append(jnp.where(msk, s2[hh * BAND:(hh + 1) * BAND], fill))
        s2 = jnp.concatenate(halves, axis=0)
        m = jnp.max(s2, axis=1, keepdims=True)
        p = jnp.exp2(s2 - m).astype(BF16)
        vw = vbuf[sq, r0:r0 + 2 * BAND, kb * 256:(kb + 1) * 256]
        if has_sink:
            vw = jnp.where(sink_row, jnp.zeros_like(vw), vw)
        pv = _dot(p, vw)
        den = pv[:, LANES:]
        out = pv[:, :LANES] / den
        o = jnp.where(lo_half, out[:BAND], out[BAND:])
        o_ref[sq, r0:r0 + BAND, qb * LANES:(qb + 1) * LANES] = o.astype(BF16)
        if want_lse:
            lse2 = m + jnp.log2(den)
            ls = jnp.where(lo_half, lse2[:BAND], lse2[BAND:])
            hi = ls.astype(BF16)
            lo = (ls - hi.astype(F32)).astype(BF16)
            lse_ref[sq, r0:r0 + BAND, qb * LANES:(qb + 1) * LANES] = hi
            lse_ref[sq, r0:r0 + BAND, (nqb + qb) * LANES:(nqb + qb + 1) * LANES] = lo


def _band_call(arr, q_col, k_col, v_col, nqb, nkb, max_dist, sinks=None, sink_base=0, want_lse=False,
               name="band"):
    ns, length, _ = arr.shape
    assert sinks is None or max_dist < BAND
    rows = min(BAND_ROWS, length)
    assert length % rows == 0 and BAND_ROWS % rows == 0 and ns % (BAND_ROWS // rows) == 0
    nseq = BAND_ROWS // rows
    wq, wk = nqb * LANES, nkb * LANES
    rpb = rows // BAND
    cur = lambda col, w: pl.BlockSpec((nseq, rows, w), lambda n, i, c=col // w: (n, i, c))
    prev = lambda col, w: pl.BlockSpec(
        (nseq, BAND, w), lambda n, i, c=col // w: (n, jnp.maximum(i * rpb - 1, 0), c))
    in_specs = [cur(q_col, wq), prev(k_col, wk), cur(k_col, wk), prev(v_col, wk), cur(v_col, wk)]
    args = [arr] * 5
    if sinks is not None:
        in_specs = [pl.BlockSpec(memory_space=pltpu.SMEM)] + in_specs
        args = [sinks] + args
    out_specs = [pl.BlockSpec((nseq, rows, wq), lambda n, i: (n, i, 0))]
    out_shape = [jax.ShapeDtypeStruct((ns, length, wq), BF16)]
    if want_lse:
        out_specs.append(pl.BlockSpec((nseq, rows, 2 * wq), lambda n, i: (n, i, 0)))
        out_shape.append(jax.ShapeDtypeStruct((ns, length, 2 * wq), BF16))
    return pl.pallas_call(
        functools.partial(_band_kernel, nqb=nqb, nkb=nkb, max_dist=max_dist,
                          has_sink=sinks is not None, want_lse=want_lse, sink_base=sink_base),
        grid=(ns // nseq, length // rows),
        in_specs=in_specs,
        out_specs=out_specs,
        out_shape=out_shape,
        scratch_shapes=[pltpu.VMEM((nseq, BAND + rows, wk), BF16), pltpu.VMEM((nseq, BAND + rows, 2 * wk), BF16)],
        name=name,
    )(*args)


def _outproj_router_kernel(oa_ref, ob_ref, o1_ref, l1_ref, o4_ref, l4_ref, o16_ref, l16_ref, p4t_ref, p16t_ref,
                           w_ref, x_ref, g1_ref, sc_ref, sh_ref, g_ref, wr_ref, br_ref, tri_ref,
                           xo_ref, hs_ref, meta_ref, nch_ref, wb):
    @pl.when((pl.program_id(0) == 0) & (pl.program_id(1) == 0))
    def _():
        wb[0:256, :] = w_ref[0:256, :].astype(BF16)
        for pos, head in enumerate(_SWA_HEAD_ORDER):
            wb[256 + pos * HEAD_DIM:256 + (pos + 1) * HEAD_DIM, :] = (
                w_ref[256 + head * HEAD_DIM:256 + (head + 1) * HEAD_DIM, :].astype(BF16))
        wb[768:1024, :] = w_ref[768:1024, :].astype(BF16)

    hw = o1_ref.shape[2]
    nblk = x_ref.shape[1] // TM

    def lse_of(v):
        return v[:, :hw] + v[:, hw:]

    def unpermute(pt_ref, src_ref, width):
        per = src_ref.shape[2] // nblk
        return jnp.concatenate(
            [_dot(pt_ref[...], src_ref[0, :, k * per:(k + 1) * per, :].reshape(TM, width)) for k in range(nblk)],
            axis=0)

    o1 = o1_ref[0].astype(F32)
    ls1 = lse_of(l1_ref[0].astype(F32))
    o4 = unpermute(p4t_ref, o4_ref, hw)
    ls4 = lse_of(unpermute(p4t_ref, l4_ref, 2 * hw))
    o16 = unpermute(p16t_ref, o16_ref, hw)
    ls16 = lse_of(unpermute(p16t_ref, l16_ref, 2 * hw))
    mx = jnp.maximum(jnp.maximum(ls1, ls4), ls16)
    e1, e4, e16 = jnp.exp2(ls1 - mx), jnp.exp2(ls4 - mx), jnp.exp2(ls16 - mx)
    oc = (e1 * o1 + e4 * o4 + e16 * o16) / (e1 + e4 + e16)
    mix = (_dot(oa_ref[0], wb[0:256, :]) + _dot(ob_ref[0], wb[256:768, :])
           + _dot(oc.astype(BF16), wb[768:1024, :]))
    x1 = x_ref[0] + _batch_row(g1_ref) * mix
    xo_ref[0] = x1
    _router_body(x1, sc_ref, sh_ref, g_ref, wr_ref, br_ref, tri_ref, hs_ref, meta_ref, nch_ref)


def _outproj_router_call(oa, ob, o1, l1, o4, l4, o16, l16, p4t, p16t, w_out, layer, x, mod, g, wr, br, tri):
    b, s, d = x.shape
    rows = ROWS
    nblk = rows // TM
    steps = s // rows
    nt = b * s // TM
    row = lambda w_: pl.BlockSpec((1, rows, w_), lambda bi, i: (bi, i, 0))
    res = lambda dd, w_: pl.BlockSpec((1, dd, rows // dd, w_), lambda bi, i: (bi, 0, i, 0))
    const2 = lambda bi, i: (0, 0)
    flat = lambda bi, i: bi * steps + i
    return pl.pallas_call(
        _outproj_router_kernel,
        grid=(b, steps),
        in_specs=[
            row(256), row(512), row(256), row(512),
            res(4, 256), res(4, 512), res(16, 256), res(16, 512),
            _const_spec((TM, TM), const2), _const_spec((TM, TM), const2),
            _const_spec((None, d, d), lambda bi, i: (layer, 0, 0)),
            row(d),
            _mod_spec(mod, layer, 2),
            _mod_spec(mod, layer, 4),
            _mod_spec(mod, layer, 3),
            pl.BlockSpec((None, 1, d), lambda bi, i: (layer, 0, 0)),
            _const_spec((None, 2 * ROUTER_COLS, d), lambda bi, i: (layer, 0, 0)),
            pl.BlockSpec((None, ROUTER_COLS, 1), lambda bi, i: (layer, 0, 0)),
            _const_spec((TM, TM), const2),
        ],
        out_specs=[
            row(d),
            pl.BlockSpec((nblk * SLOTS, d), lambda bi, i: (flat(bi, i), 0)),
            pl.BlockSpec((nblk, 8, TM), lambda bi, i: (flat(bi, i), 0, 0)),
            pl.BlockSpec((nblk, N_EXPERTS, LANES), lambda bi, i: (flat(bi, i), 0, 0)),
        ],
        out_shape=[
            jax.ShapeDtypeStruct((b, s, d), F32),
            jax.ShapeDtypeStruct((nt * SLOTS, d), BF16),
            jax.ShapeDtypeStruct((nt, 8, TM), F32),
            jax.ShapeDtypeStruct((nt, N_EXPERTS, LANES), I32),
        ],
        scratch_shapes=[pltpu.VMEM((d, d), BF16)],
        compiler_params=pltpu.CompilerParams(vmem_limit_bytes=56 * 1024 * 1024),
        name="out_proj_router",
    )(oa, ob, o1, l1, o4, l4, o16, l16, p4t, p16t, w_out, x, mod, mod, mod, g, wr, br, tri)


ROUTER_COLS = LANES


def _router_body(x, sc_ref, sh_ref, g_ref, wr_ref, br_ref, tri_ref, hs_ref, meta_ref, nch_ref):
    tm = TM
    h = _modulated_norm(x, g_ref[...], _batch_row(sc_ref), _batch_row(sh_ref))
    hb = h.astype(BF16)
    h_lo = (h - hb.astype(F32)).astype(BF16)
    part = _dot_nt(wr_ref[...], hb) + _dot_nt(wr_ref[...], h_lo)
    logits_t = part[:ROUTER_COLS] + part[ROUTER_COLS:] + br_ref[...]
    r8 = lax.broadcasted_iota(I32, (8, tm), 0)
    r16 = lax.broadcasted_iota(I32, (N_EXPERTS, tm), 0)
    rl = lax.broadcasted_iota(I32, (N_EXPERTS, LANES), 0)
    slot = lax.broadcasted_iota(I32, (SLOTS, tm), 0)
    for blk in range(x.shape[0] // tm):
        lt = logits_t[:, blk * tm:(blk + 1) * tm]
        glog = lt[0:8]
        elog = lt[8:8 + N_EXPERTS]

        gmax = jnp.max(glog, axis=0, keepdims=True)
        g_w = 1.0 / jnp.sum(jnp.exp(glog - gmax), axis=0, keepdims=True)
        g_idx = jnp.min(jnp.where(glog == gmax, r8, 99), axis=0, keepdims=True)

        el = jnp.where((r16 // EXPERTS_PER_GROUP) == g_idx, elog, NEG_INF)
        emax = jnp.max(el, axis=0, keepdims=True)
        e1 = jnp.min(jnp.where(el == emax, r16, 99), axis=0, keepdims=True)
        el2 = jnp.where(r16 == e1, NEG_INF, el)
        emax2 = jnp.max(el2, axis=0, keepdims=True)
        e2 = jnp.min(jnp.where(el2 == emax2, r16, 99), axis=0, keepdims=True)
        p2 = jnp.exp(emax2 - emax)
        wt1 = g_w / (1.0 + p2)
        wt2 = g_w * p2 / (1.0 + p2)

        oh1 = r16 == e1
        oh2 = r16 == e2
        onehot = jnp.where(oh1, 1.0, 0.0) + jnp.where(oh2, 1.0, 0.0)
        cnt = jnp.sum(onehot, axis=1, keepdims=True)
        nch = jnp.floor((cnt + (CHUNK - 1)) * (1.0 / CHUNK))
        nchb = jnp.broadcast_to(nch, (N_EXPERTS, LANES))
        incl = nchb
        for sft in (1, 2, 4, 8):
            incl = incl + jnp.where(rl >= sft, pltpu.roll(incl, sft, 0), 0.0)
        off = (incl - nchb)[:, 0:1] * float(CHUNK)
        rank = _dot(onehot.astype(BF16), tri_ref[...])
        slot_of = off + rank
        pos1 = jnp.sum(jnp.where(oh1, slot_of, 0.0), axis=0, keepdims=True)
        pos2 = jnp.sum(jnp.where(oh2, slot_of, 0.0), axis=0, keepdims=True)

        sel = jnp.where(slot == pos1.astype(I32), 1.0, jnp.where(slot == pos2.astype(I32), 1.0, 0.0))
        hs_ref[blk * SLOTS:(blk + 1) * SLOTS, :] = _dot(sel.astype(BF16), hb[blk * tm:(blk + 1) * tm]).astype(BF16)

        meta_ref[blk] = jnp.concatenate([pos1, pos2, wt1, wt2, jnp.zeros((4, tm), F32)], axis=0)
        nch_ref[blk] = nchb.astype(I32)


def _ffn_schedule(nch, max_tiles, dump_base):
    nt = nch.shape[0]
    cend = jnp.cumsum(nch, axis=1)
    coff = cend - nch
    tcum = jnp.cumsum(nch, axis=0)
    before = tcum - nch
    tot = tcum[-1]
    pad = ((tot + FFN_CHUNKS - 1) // FFN_CHUNKS) * FFN_CHUNKS
    eend = jnp.cumsum(pad)
    estart = eend - pad
    n_tiles = (eend[-1] // FFN_CHUNKS).astype(I32)
    first_chunk = jnp.arange(max_tiles, dtype=I32) * FFN_CHUNKS
    tile_expert = jnp.sum((eend[None, :] <= first_chunk[:, None]).astype(I32), axis=1)
    tile_expert = jnp.minimum(tile_expert, N_EXPERTS - 1)
    hp = lax.Precision.HIGHEST
    pos = jnp.arange(max_tiles * FFN_CHUNKS, dtype=I32)
    e_s = jnp.minimum(jnp.sum((eend[None, :] <= (pos // FFN_CHUNKS * FFN_CHUNKS)[:, None]).astype(I32), axis=1),
                      N_EXPERTS - 1)
    oh_e = (e_s[:, None] == jnp.arange(N_EXPERTS, dtype=I32)[None, :]).astype(F32)
    idx = pos - jnp.dot(oh_e, estart.astype(F32), precision=hp).astype(I32)
    run_end = jnp.dot(oh_e, tcum.T.astype(F32), precision=hp).astype(I32)
    run_beg = jnp.dot(oh_e, before.T.astype(F32), precision=hp).astype(I32)
    run_off = jnp.dot(oh_e, coff.T.astype(F32), precision=hp).astype(I32)
    in_run = (idx[:, None] >= run_beg) & (idx[:, None] < run_end)
    tile_base = jnp.arange(nt, dtype=I32)[None, :] * CHUNKS_PER_TILE
    src = jnp.sum(jnp.where(in_run, tile_base + run_off + idx[:, None] - run_beg, 0), axis=1)
    real = jnp.any(in_run, axis=1)
    dump = dump_base + (pos // FFN_CHUNKS % 2) * FFN_CHUNKS + pos % FFN_CHUNKS
    src_rows = jnp.where(real, src, 0) * CHUNK
    dst_rows = jnp.where(real, src, dump) * CHUNK
    used = cend[:, -1]
    ucum = jnp.cumsum(CHUNKS_PER_TILE - used)
    ubeg = ucum - (CHUNKS_PER_TILE - used)
    z = jnp.arange(max_tiles * ZERO_CHUNKS, dtype=I32)[:, None]
    in_gap = (z >= ubeg[None, :]) & (z < ucum[None, :])
    zero_rows = jnp.sum(jnp.where(in_gap, tile_base + used[None, :] + z - ubeg[None, :], 0), axis=1) * CHUNK
    n_zero = ucum[-1].astype(I32).reshape(1)
    return tile_expert, src_rows, dst_rows, n_tiles.reshape(1), zero_rows, n_zero


def _ffn_kernel(te_ref, sr_ref, dr_ref, nt_ref, zr_ref, nz_ref, hs_hbm, wg_ref, wu_ref, wd_ref, ys_hbm,
                xbuf, ybuf, zbuf, wgb, wub, wdb, in_sem, out_sem, zero_sem, *, dump_base):
    j = pl.program_id(0)
    nt = nt_ref[0]
    half_ff = EXPERT_FF // 2

    def rows_at(r):
        return pl.ds(pl.multiple_of(r, CHUNK), CHUNK)

    def in_copy(step, slot, k, wait=False):
        r = 0 if wait else sr_ref[step * FFN_CHUNKS + k]
        return pltpu.make_async_copy(hs_hbm.at[rows_at(r), :], xbuf.at[slot, pl.ds(k * CHUNK, CHUNK), :],
                                     in_sem.at[slot])

    def out_copy(step, slot, k, wait=False):
        r = 0 if wait else dr_ref[step * FFN_CHUNKS + k]
        return pltpu.make_async_copy(ybuf.at[slot, pl.ds(k * CHUNK, CHUNK), :], ys_hbm.at[rows_at(r), :],
                                     out_sem.at[slot])

    z_lo = jnp.minimum(j * ZERO_CHUNKS, nz_ref[0])
    z_hi = jnp.minimum((j + 1) * ZERO_CHUNKS, nz_ref[0])

    def zero_start(k, carry):
        pltpu.make_async_copy(zbuf.at[pl.ds(0, CHUNK), :], ys_hbm.at[rows_at(zr_ref[k]), :], zero_sem).start()
        return carry

    def zero_wait(k, carry):
        pltpu.make_async_copy(zbuf.at[pl.ds(0, CHUNK), :], ys_hbm.at[rows_at(0), :], zero_sem).wait()
        return carry

    @pl.when(j == 0)
    def _():
        zbuf[...] = jnp.zeros_like(zbuf)
        fills = [pltpu.make_async_copy(
            zbuf, ys_hbm.at[pl.ds(dump_base * CHUNK + r * ZBUF_ROWS, ZBUF_ROWS), :], zero_sem)
            for r in range(DUMP_CHUNKS * CHUNK // ZBUF_ROWS)]
        for cp in fills:
            cp.start()
        for cp in fills:
            cp.wait()

    lax.fori_loop(z_lo, z_hi, zero_start, 0)

    @pl.when(j < nt)
    def _():
        slot = j % 2

        @pl.when(j == 0)
        def _():
            for k in range(FFN_CHUNKS):
                in_copy(0, 0, k).start()

        @pl.when(j + 1 < nt)
        def _():
            for k in range(FFN_CHUNKS):
                in_copy(j + 1, 1 - slot, k).start()

        @pl.when((j == 0) | (te_ref[j] != te_ref[jnp.maximum(j - 1, 0)]))
        def _():
            wgb[...] = wg_ref[0].astype(BF16)
            wub[...] = wu_ref[0].astype(BF16)
            wdb[...] = wd_ref[0].astype(BF16)

        for k in range(FFN_CHUNKS):
            in_copy(j, slot, k, wait=True).wait()

        @pl.when(j >= 2)
        def _():
            for k in range(FFN_CHUNKS):
                out_copy(j - 2, slot, k, wait=True).wait()

        x = xbuf[slot]
        hg = [_dot(x, wgb[:, h * half_ff:(h + 1) * half_ff]) for h in range(2)]
        hu = [_dot(x, wub[:, h * half_ff:(h + 1) * half_ff]) for h in range(2)]
        y = None
        for h in range(2):
            act = ((hg[h] / (1.0 + jnp.exp(-hg[h]))) * hu[h]).astype(BF16)
            part = _dot(act, wdb[h * half_ff:(h + 1) * half_ff, :])
            y = part if y is None else y + part
        ybuf[slot] = y.astype(BF16)
        for k in range(FFN_CHUNKS):
            out_copy(j, slot, k).start()

        @pl.when(j == nt - 1)
        def _():
            for k in range(FFN_CHUNKS):
                out_copy(j, slot, k, wait=True).wait()

            @pl.when(j >= 1)
            def _():
                for k in range(FFN_CHUNKS):
                    out_copy(j - 1, 1 - slot, k, wait=True).wait()

    lax.fori_loop(z_lo, z_hi, zero_wait, 0)


def _ffn_call(tile_expert, src_rows, dst_rows, n_tiles, zero_rows, n_zero, hs, wg, wu, wd, layer, max_tiles):
    rows, d = hs.shape
    ff = wg.shape[-1]
    wmap = lambda j, te, sr, dr, nt, zr, nz: (layer, te[j], 0, 0)
    grid_spec = pltpu.PrefetchScalarGridSpec(
        num_scalar_prefetch=6,
        grid=(max_tiles,),
        in_specs=[
            pl.BlockSpec(memory_space=pl.ANY),
            pl.BlockSpec((None, 1, d, ff), wmap),
            pl.BlockSpec((None, 1, d, ff), wmap),
            pl.BlockSpec((None, 1, ff, d), wmap),
        ],
        out_specs=pl.BlockSpec(memory_space=pl.ANY),
        scratch_shapes=[
            pltpu.VMEM((2, FFN_ROWS, d), BF16),
            pltpu.VMEM((2, FFN_ROWS, d), BF16),
            pltpu.VMEM((ZBUF_ROWS, d), BF16),
            pltpu.VMEM((d, ff), BF16),
            pltpu.VMEM((d, ff), BF16),
            pltpu.VMEM((ff, d), BF16),
            pltpu.SemaphoreType.DMA((2,)),
            pltpu.SemaphoreType.DMA((2,)),
            pltpu.SemaphoreType.DMA(()),
        ],
    )
    return pl.pallas_call(
        functools.partial(_ffn_kernel, dump_base=rows // CHUNK),
        grid_spec=grid_spec,
        out_shape=jax.ShapeDtypeStruct((rows + DUMP_CHUNKS * CHUNK, d), BF16),
        compiler_params=pltpu.CompilerParams(vmem_limit_bytes=48 * 1024 * 1024),
        name="expert_ffn",
    )(tile_expert, src_rows, dst_rows, n_tiles, zero_rows, n_zero, hs, wg, wu, wd)


def _combine_body(ys_ref, meta_ref, x_ref, g2_ref):
    tm = TM
    eye = (lax.broadcasted_iota(I32, (tm, tm), 0) == lax.broadcasted_iota(I32, (tm, tm), 1))
    slot = lax.broadcasted_iota(I32, (tm, SLOTS), 1).astype(F32)

    def as_col(row):
        return jnp.sum(jnp.where(eye, row, 0.0), axis=1, keepdims=True)

    ys = []
    for blk in range(x_ref.shape[1] // tm):
        meta = meta_ref[blk]
        pos1, pos2 = as_col(meta[0:1]), as_col(meta[1:2])
        w1, w2 = as_col(meta[2:3]), as_col(meta[3:4])
        gate = jnp.where(slot == pos1, w1, 0.0) + jnp.where(slot == pos2, w2, 0.0)
        ys.append(_dot(gate.astype(BF16), ys_ref[blk * SLOTS:(blk + 1) * SLOTS, :]))
    return x_ref[0] + _batch_row(g2_ref) * jnp.concatenate(ys, axis=0)


def _combine_final_kernel(ys_ref, meta_ref, x_ref, g2_ref, gf_ref, xo_ref):
    xo = _combine_body(ys_ref, meta_ref, x_ref, g2_ref)
    xo_ref[0] = xo * lax.rsqrt(jnp.mean(xo * xo, axis=-1, keepdims=True) + NORM_EPS) * gf_ref[...]


def _combine_inproj_kernel(ys_ref, meta_ref, x_ref, g2_ref, *refs):
    n_in = 10
    xo_ref = refs[n_in]
    xo = _combine_body(ys_ref, meta_ref, x_ref, g2_ref)
    xo_ref[0] = xo
    _inproj_body(xo, *refs[:n_in], *refs[n_in + 1:])


def _combine_specs(d, steps, mod, layer):
    nblk = ROWS // TM
    flat = lambda bi, i: bi * steps + i
    return [
        pl.BlockSpec((nblk * SLOTS, d), lambda bi, i: (flat(bi, i), 0)),
        pl.BlockSpec((nblk, 8, TM), lambda bi, i: (flat(bi, i), 0, 0)),
        pl.BlockSpec((1, ROWS, d), lambda bi, i: (bi, i, 0)),
        _mod_spec(mod, layer, 5),
    ]


def _combine_final_call(ys, meta, x, mod, layer, gf):
    b, s, d = x.shape
    return pl.pallas_call(
        _combine_final_kernel,
        grid=(b, s // ROWS),
        in_specs=_combine_specs(d, s // ROWS, mod, layer) + [pl.BlockSpec((1, d), lambda bi, i: (0, 0))],
        out_specs=pl.BlockSpec((1, ROWS, d), lambda bi, i: (bi, i, 0)),
        out_shape=jax.ShapeDtypeStruct((b, s, d), F32),
        compiler_params=pltpu.CompilerParams(vmem_limit_bytes=48 * 1024 * 1024),
        name="moe_combine",
    )(ys, meta, x, mod, gf)


def _combine_inproj_call(ys, meta, x, mod, g, w_in, layer, tabs, p4, p16):
    b, s, d = x.shape
    in_specs, out_specs, out_shape = _inproj_specs(b, s, d, layer, mod)
    xspec = pl.BlockSpec((1, ROWS, d), lambda bi, i: (bi, i, 0))
    return pl.pallas_call(
        _combine_inproj_kernel,
        grid=(b, s // ROWS),
        in_specs=_combine_specs(d, s // ROWS, mod, layer - 1) + in_specs,
        out_specs=[xspec] + out_specs,
        out_shape=[jax.ShapeDtypeStruct((b, s, d), F32)] + out_shape,
        scratch_shapes=[pltpu.VMEM((d, IN_WIDTH), BF16)],
        compiler_params=pltpu.CompilerParams(vmem_limit_bytes=56 * 1024 * 1024),
        name="combine_in_proj",
    )(ys, meta, x, mod, mod, mod, g, w_in, *tabs, p4, p16)


def _rope_tables(positions):
    pos = positions.astype(F32)[..., None]

    def table(dim):
        inv = ROPE_THETA ** (-jnp.arange(0, dim, 2, dtype=F32) / dim)
        ang = pos * inv
        cos, sin = jnp.cos(ang), jnp.sin(ang)
        reps = LANES // dim
        return (jnp.tile(jnp.concatenate([cos, cos], -1), (1, 1, reps)),
                jnp.tile(jnp.concatenate([-sin, sin], -1), (1, 1, reps)))

    c64, s64 = table(HEAD_DIM)
    c32, s32 = table(DIFF_QK_DIM)
    return c64, s64, c32, s32


def kernel(x, c, positions, ada_w, ada_b, norm_mix_g, norm_ffn_g, w_in, w_out, diff_lambda_q1, diff_lambda_k1,
           diff_lambda_q2, diff_lambda_k2, diff_subln_g, swa_sinks, router_group_w, router_group_b,
           router_expert_w, router_expert_b, expert_w_gate, expert_w_up, expert_w_down, final_norm_g):
    b, s, d = x.shape
    depth = ada_w.shape[0]
    assert b <= BF16_ROWS and s % TQ == 0 and s % ROWS == 0 and s % (16 * BAND) == 0 and d == 8 * LANES
    n = b * s
    nt = n // TM
    max_tiles = (nt * CHUNKS_PER_TILE + N_EXPERTS * (FFN_CHUNKS - 1)) // FFN_CHUNKS + 1

    tabs = _rope_tables(positions)
    p4 = _residue_perm(TM, 4)
    p16 = _residue_perm(TM, 16)
    p4_b, p16_b = jnp.asarray(p4, BF16), jnp.asarray(p16, BF16)
    p4t_b, p16t_b = jnp.asarray(p4.T, BF16), jnp.asarray(p16.T, BF16)
    tri = jnp.asarray(np.triu(np.ones((TM, TM), np.float32), 1), BF16)

    c_pad = jnp.pad(c, ((0, BF16_ROWS - b), (0, 0)))
    mod = _ada_call(c_pad, ada_w, ada_b)

    sink_order = np.asarray(_SWA_HEAD_ORDER, np.int32)
    zpad = lambda k: jnp.zeros((depth, d, k), F32)
    wr = jnp.concatenate([router_group_w, zpad(8 - N_GROUPS), router_expert_w,
                          zpad(ROUTER_COLS - 8 - N_EXPERTS)], axis=-1)
    wr_hi = wr.astype(BF16)
    wr = jnp.concatenate([wr_hi, (wr - wr_hi.astype(F32)).astype(BF16)], axis=-1)
    wr = jnp.swapaxes(wr, 1, 2)
    br = jnp.concatenate([router_group_b, jnp.full((depth, 8 - N_GROUPS), NEG_INF, F32), router_expert_b,
                          jnp.zeros((depth, ROUTER_COLS - 8 - N_EXPERTS), F32)], axis=-1).reshape(depth, -1, 1)

    lam_init = [0.8 - 0.6 * math.exp(-0.3 * l) for l in range(depth)]
    lam = (jnp.exp(jnp.sum(diff_lambda_q1 * diff_lambda_k1, axis=-1))
           - jnp.exp(jnp.sum(diff_lambda_q2 * diff_lambda_k2, axis=-1)) + jnp.asarray(lam_init, F32))
    g_cols = diff_subln_g[:, :, None]
    sinks = swa_sinks[:, sink_order].reshape(-1)
    g_mix = norm_mix_g.reshape(depth, 1, d)
    g_ffn = norm_ffn_g.reshape(depth, 1, d)

    proj, vat, qkv4, qkv16 = _inproj_call(x, mod, g_mix, w_in, 0, tabs, p4_b, p16_b)
    for l in range(depth):
        oa = _diff_attn_call(proj, vat, lam, g_cols, lam_init[l], l)
        ob = _band_call(proj, COL_QB, COL_KB, COL_VB, nqb=4, nkb=1, max_dist=SWA_WINDOW - 1,
                        sinks=sinks, sink_base=l * SWA_Q_HEADS, name="swa")[0]
        o1, l1 = _band_call(proj, COL_QC, COL_KC, COL_VC, nqb=2, nkb=2, max_dist=BAND, want_lse=True,
                            name="dil1")
        o4, l4 = _band_call(qkv4.reshape(b * 4, s // 4, 768), 0, 256, 512, nqb=2, nkb=2, max_dist=BAND,
                            want_lse=True, name="dil4")
        o16, l16 = _band_call(qkv16.reshape(b * 16, s // 16, 768), 0, 256, 512, nqb=2, nkb=2, max_dist=BAND,
                              want_lse=True, name="dil16")
        x, hs, meta, nch = _outproj_router_call(
            oa, ob, o1, l1, o4.reshape(b, 4, s // 4, 256), l4.reshape(b, 4, s // 4, 512),
            o16.reshape(b, 16, s // 16, 256), l16.reshape(b, 16, s // 16, 512), p4t_b, p16t_b, w_out, l, x, mod,
            g_ffn, wr, br, tri)
        sched = _ffn_schedule(nch[:, :, 0], max_tiles, nt * CHUNKS_PER_TILE)
        ys = _ffn_call(*sched, hs, expert_w_gate, expert_w_up, expert_w_down, l, max_tiles)
        if l + 1 < depth:
            x, proj, vat, qkv4, qkv16 = _combine_inproj_call(ys, meta, x, mod, g_mix, w_in, l + 1, tabs,
                                                             p4_b, p16_b)
        else:
            x = _combine_final_call(ys, meta, x, mod, l, final_norm_g.reshape(1, d))
    return x
```

```python
import functools
import math

import numpy as np
import jax
import jax.numpy as jnp
from jax import lax
from jax.experimental import pallas as pl
from jax.experimental.pallas import tpu as pltpu

F32 = jnp.float32
BF16 = jnp.bfloat16
I32 = jnp.int32

HEAD_DIM = 64
ROPE_THETA = 10000.0
NORM_EPS = 1e-6
NEG_INF = -1e30
DIFF_HEADS = 4
DIFF_QK_DIM = 32
SWA_Q_HEADS = 8
SWA_KV_HEADS = 2
SWA_WINDOW = 128
DIL_PATTERNS = ((128, 1), (512, 4), (2048, 16))
N_GROUPS = 4
EXPERTS_PER_GROUP = 4
N_EXPERTS = 16
EXPERT_FF = 512
N_ADA = 6
IN_WIDTH = 2304

LANES = 128
BF16_ROWS = 16
BAND = 128

TM = 256
ROWS = 2 * TM
CHUNK = BF16_ROWS
SLOTS = 2 * TM + N_EXPERTS * CHUNK
CHUNKS_PER_TILE = SLOTS // CHUNK
FFN_ROWS = 1024
FFN_CHUNKS = FFN_ROWS // CHUNK
ZERO_CHUNKS = -(-N_EXPERTS * FFN_CHUNKS // CHUNKS_PER_TILE)
DUMP_CHUNKS = 2 * FFN_CHUNKS
ZBUF_ROWS = FFN_ROWS // 4
TQ = 512
BAND_ROWS = 4096

_SWA_HEAD_ORDER = (0, 4, 1, 5, 2, 6, 3, 7)
COL_QB, COL_KB, COL_VB = 0, 512, 640
COL_Q1, COL_Q2, COL_K1, COL_K2, COL_VA = 768, 896, 1024, 1152, 1280
COL_QC, COL_KC, COL_VC = 1536, 1792, 2048
_SRC_ROPE = (32, 32, 32, 32, 0, 0, 64, 64, 64, 64, 64, 0, 64, 64, 64, 64, 0, 0)
_SRC_DEST = (6, 7, 8, 9, 10, 11, None, None, None, None, 4, 5, 12, 13, 14, 15, 16, 17)
VT_ROWS = HEAD_DIM + BF16_ROWS
_SRC_VA_CHUNK = 2


def _residue_perm(tm, d):
    p = np.zeros((tm, tm), np.float32)
    per = tm // d
    for l in range(per):
        for r in range(d):
            p[r * per + l, l * d + r] = 1.0
    return p


def _dot(a, b, **kw):
    return jnp.dot(a, b, preferred_element_type=F32, **kw)


def _dot_nt(a, b):
    return lax.dot_general(a, b, (((1,), (1,)), ((), ())), preferred_element_type=F32)


def _batch_row(ref):
    return ref[pl.ds(pl.program_id(0), 1), :]


def _modulated_norm(x, g, sc, sh):
    y = x * lax.rsqrt(jnp.mean(x * x, axis=-1, keepdims=True) + NORM_EPS)
    return (y * g) * (1.0 + sc) + sh


def _ada_kernel(c_ref, w_ref, b_ref, o_ref):
    c = c_ref[...]
    ca = c / (1.0 + jnp.exp(-c))
    rows = ca.shape[0]
    c_hi = ca.astype(BF16)
    c_lo = (ca - c_hi.astype(F32)).astype(BF16)
    w = w_ref[0]
    w_hi = w.astype(BF16)
    w_lo = (w - w_hi.astype(F32)).astype(BF16)
    main = _dot(jnp.concatenate([c_hi, c_lo], axis=0), w_hi)
    o_ref[0] = main[:rows] + main[rows:] + _dot(c_hi, w_lo) + b_ref[0]


def _ada_call(c_pad, ada_w, ada_b):
    depth, d, n = ada_w.shape
    tn = n // 2
    return pl.pallas_call(
        _ada_kernel,
        grid=(depth, n // tn),
        in_specs=[
            pl.BlockSpec((c_pad.shape[0], d), lambda l, j: (0, 0)),
            pl.BlockSpec((1, d, tn), lambda l, j: (l, 0, j)),
            pl.BlockSpec((1, 1, tn), lambda l, j: (l, 0, j)),
        ],
        out_specs=pl.BlockSpec((1, c_pad.shape[0], tn), lambda l, j: (l, 0, j)),
        out_shape=jax.ShapeDtypeStruct((depth, c_pad.shape[0], n), F32),
        compiler_params=pltpu.CompilerParams(vmem_limit_bytes=40 * 1024 * 1024),
        name="ada_mod",
    )(c_pad, ada_w, ada_b.reshape(depth, 1, n))


def _rope(t, cos, sin_signed, first_half, half):
    rot = jnp.where(first_half, pltpu.roll(t, LANES - half, 1), pltpu.roll(t, half, 1))
    return t * cos + rot * sin_signed


def _inproj_body(x, sc_ref, sh_ref, g_ref, w_ref, cs64_ref, sn64_ref, cs32_ref, sn32_ref,
                 p4_ref, p16_ref, proj_ref, vat_ref, c4_ref, c16_ref, wb):
    @pl.when((pl.program_id(0) == 0) & (pl.program_id(1) == 0))
    def _():
        wb[...] = w_ref[...].astype(BF16)

    h = _modulated_norm(x, g_ref[...], _batch_row(sc_ref), _batch_row(sh_ref))
    hb = h.astype(BF16)
    lane = lax.broadcasted_iota(I32, (1, LANES), 1)
    first64 = (lane % 64) < 32
    first32 = (lane % 32) < 16
    lo_half = lane < 64
    swa_q = []
    for cb in range(IN_WIDTH // 256):
        if cb == IN_WIDTH // 256 - 1:
            half = hb.shape[0] // 2
            acc = jnp.concatenate([_dot(hb[:half], wb[:, cb * 256:(cb + 1) * 256]),
                                   _dot(hb[half:], wb[:, cb * 256:(cb + 1) * 256])], axis=0)
        else:
            acc = _dot(hb, wb[:, cb * 256:(cb + 1) * 256])
        if cb == _SRC_VA_CHUNK:
            acc_t = acc.T.astype(BF16)
            for hd in range(DIFF_HEADS):
                vat_ref[0, hd * VT_ROWS:hd * VT_ROWS + HEAD_DIM, :] = acc_t[hd * HEAD_DIM:(hd + 1) * HEAD_DIM]
                vat_ref[0, hd * VT_ROWS + HEAD_DIM:(hd + 1) * VT_ROWS, :] = jnp.ones(
                    (BF16_ROWS, acc_t.shape[1]), BF16)
        for half in range(2):
            src = cb * 2 + half
            t = acc[:, half * LANES:(half + 1) * LANES]
            if _SRC_ROPE[src] == 64:
                t = _rope(t, cs64_ref[0], sn64_ref[0], first64, 32)
            elif _SRC_ROPE[src] == 32:
                t = _rope(t, cs32_ref[0], sn32_ref[0], first32, 16)
            dst = _SRC_DEST[src]
            if dst is None:
                swa_q.append(t)
            else:
                proj_ref[0, :, dst * LANES:(dst + 1) * LANES] = t.astype(BF16)
    for jb in range(SWA_Q_HEADS // 2):
        a, c = swa_q[jb // 2], swa_q[2 + jb // 2]
        if jb % 2 == 0:
            blk = jnp.where(lo_half, a, pltpu.roll(c, 64, 1))
        else:
            blk = jnp.where(lo_half, pltpu.roll(a, 64, 1), c)
        proj_ref[0, :, jb * LANES:(jb + 1) * LANES] = blk.astype(BF16)
    tm = TM
    for blk in range(x.shape[0] // tm):
        cc = proj_ref[0, blk * tm:(blk + 1) * tm, COL_QC:]
        c4 = _dot(p4_ref[...], cc).astype(BF16)
        for r in range(4):
            c4_ref[0, r, blk * (tm // 4):(blk + 1) * (tm // 4)] = c4[r * (tm // 4):(r + 1) * (tm // 4)]
        c16 = _dot(p16_ref[...], cc).astype(BF16)
        for r in range(16):
            c16_ref[0, r, blk * (tm // 16):(blk + 1) * (tm // 16)] = c16[r * (tm // 16):(r + 1) * (tm // 16)]


def _inproj_kernel(x_ref, *refs):
    _inproj_body(x_ref[0], *refs)


def _mod_spec(mod, layer, k):
    return pl.BlockSpec((None, mod.shape[1], mod.shape[2] // N_ADA), lambda bi, i: (layer, 0, k))


def _const_spec(shape, index_map):
    return pl.BlockSpec(shape, index_map, pipeline_mode=pl.Buffered(1))


def _inproj_specs(b, s, d, layer, mod):
    rows = ROWS
    row = lambda bi, i: (bi, i, 0)
    const2 = lambda bi, i: (0, 0)
    in_specs = [
        _mod_spec(mod, layer, 1),
        _mod_spec(mod, layer, 0),
        pl.BlockSpec((None, 1, d), lambda bi, i: (layer, 0, 0)),
        _const_spec((None, d, IN_WIDTH), lambda bi, i: (layer, 0, 0)),
        pl.BlockSpec((1, rows, LANES), row),
        pl.BlockSpec((1, rows, LANES), row),
        pl.BlockSpec((1, rows, LANES), row),
        pl.BlockSpec((1, rows, LANES), row),
        _const_spec((TM, TM), const2),
        _const_spec((TM, TM), const2),
    ]
    out_specs = [
        pl.BlockSpec((1, rows, IN_WIDTH), row),
        pl.BlockSpec((1, DIFF_HEADS * VT_ROWS, rows), lambda bi, i: (bi, 0, i)),
        pl.BlockSpec((1, 4, rows // 4, 768), lambda bi, i: (bi, 0, i, 0)),
        pl.BlockSpec((1, 16, rows // 16, 768), lambda bi, i: (bi, 0, i, 0)),
    ]
    out_shape = [
        jax.ShapeDtypeStruct((b, s, IN_WIDTH), BF16),
        jax.ShapeDtypeStruct((b, DIFF_HEADS * VT_ROWS, s), BF16),
        jax.ShapeDtypeStruct((b, 4, s // 4, 768), BF16),
        jax.ShapeDtypeStruct((b, 16, s // 16, 768), BF16),
    ]
    return in_specs, out_specs, out_shape


def _inproj_call(x, mod, g, w_in, layer, tabs, p4, p16):
    b, s, d = x.shape
    in_specs, out_specs, out_shape = _inproj_specs(b, s, d, layer, mod)
    return pl.pallas_call(
        _inproj_kernel,
        grid=(b, s // ROWS),
        in_specs=[pl.BlockSpec((1, ROWS, d), lambda bi, i: (bi, i, 0))] + in_specs,
        out_specs=out_specs,
        out_shape=out_shape,
        scratch_shapes=[pltpu.VMEM((d, IN_WIDTH), BF16)],
        compiler_params=pltpu.CompilerParams(vmem_limit_bytes=56 * 1024 * 1024),
        name="in_proj",
    )(x, mod, mod, g, w_in, *tabs, p4, p16)


def _diff_attn_kernel(lam_ref, q1_ref, q2_ref, q1n_ref, q2n_ref, k1_ref, k2_ref, vt_ref, g_ref, o_ref,
                      m_sc, acc_sc, s_sc, *, lambda_init, layer):
    tq = q1_ref.shape[1]
    qi = pl.program_id(1)
    lam = lam_ref[layer]
    to_log2 = DIFF_QK_DIM ** -0.5 * math.log2(math.e)
    lane = lax.broadcasted_iota(I32, (1, LANES), 1)

    def head_queries(qa_ref, qb_ref):
        qa = qa_ref[0].astype(F32) * to_log2
        qb = qb_ref[0].astype(F32) * to_log2
        out = []
        for h in range(DIFF_HEADS):
            hm = (lane // DIFF_QK_DIM) == h
            out.append((jnp.where(hm, qa, 0.0).astype(BF16), jnp.where(hm, qb, 0.0).astype(BF16)))
        return out

    qh = head_queries(q1_ref, q2_ref)
    qh_next = head_queries(q1n_ref, q2n_ref)
    causal = (lax.broadcasted_iota(I32, (tq, tq), 0) <= lax.broadcasted_iota(I32, (tq, tq), 1))

    m_sc[...] = jnp.full(m_sc.shape, NEG_INF, F32)
    acc_sc[...] = jnp.zeros(acc_sc.shape, F32)

    n_chain = 2 * DIFF_HEADS

    def scores(ch, tile, queries):
        start = pl.multiple_of(tile * tq, tq)
        k_ref = k1_ref if ch % 2 == 0 else k2_ref
        return _dot_nt(k_ref[0, pl.ds(start, tq), :], queries[ch // 2][ch % 2])

    @pl.when(qi == 0)
    def _():
        for ch in range(n_chain):
            s_sc[ch] = scores(ch, 0, qh)

    def step(j, last):
        start = pl.multiple_of(j * tq, tq)
        for ch in range(n_chain):
            st = s_sc[ch]
            s_sc[ch] = scores(ch, 0, qh_next) if last else scores(ch, j + 1, qh)
            h = ch // 2
            vt = vt_ref[0, h * VT_ROWS:(h + 1) * VT_ROWS, pl.ds(start, tq)]
            if last:
                st = jnp.where(causal, st, NEG_INF)
            m_old = m_sc[ch]
            m_new = jnp.maximum(m_old, jnp.max(st, axis=0, keepdims=True))
            p = jnp.exp2(st - m_new)
            al = jnp.exp2(m_old - m_new)
            acc_sc[ch] = al * acc_sc[ch] + _dot(vt, p.astype(BF16))
            m_sc[ch] = m_new

    def body(j, carry):
        step(j, False)
        return carry

    lax.fori_loop(0, qi, body, 0)
    step(qi, True)

    g = g_ref[...]
    outs = []
    for h in range(DIFF_HEADS):
        a1, a2 = acc_sc[2 * h], acc_sc[2 * h + 1]
        o = (a1[:HEAD_DIM] / a1[HEAD_DIM:HEAD_DIM + 1]
             - lam * (a2[:HEAD_DIM] / a2[HEAD_DIM:HEAD_DIM + 1]))
        ms = jnp.mean(o * o, axis=0, keepdims=True)
        outs.append((o * lax.rsqrt(ms + NORM_EPS)) * g * (1.0 - lambda_init))
    o_ref[0] = jnp.concatenate(outs, axis=0).T.astype(BF16)


def _diff_attn_call(proj, vat, lam, g_cols, lambda_init, layer):
    b, s, _ = proj.shape
    tq = TQ
    qspec = lambda cb: pl.BlockSpec((1, tq, LANES), lambda bi, i, cb=cb: (bi, i, cb))
    last_q = s // tq - 1
    qnext = lambda cb: pl.BlockSpec((1, tq, LANES), lambda bi, i, cb=cb: (bi, jnp.minimum(i + 1, last_q), cb))
    kspec = lambda cb: pl.BlockSpec((1, s, LANES), lambda bi, i, cb=cb: (bi, 0, cb))
    n_chain = 2 * DIFF_HEADS
    return pl.pallas_call(
        functools.partial(_diff_attn_kernel, lambda_init=lambda_init, layer=layer),
        grid=(b, s // tq),
        in_specs=[
            pl.BlockSpec(memory_space=pltpu.SMEM),
            qspec(COL_Q1 // LANES), qspec(COL_Q2 // LANES),
            qnext(COL_Q1 // LANES), qnext(COL_Q2 // LANES),
            kspec(COL_K1 // LANES), kspec(COL_K2 // LANES),
            pl.BlockSpec((1, DIFF_HEADS * VT_ROWS, s), lambda bi, i: (bi, 0, 0)),
            pl.BlockSpec((None, HEAD_DIM, 1), lambda bi, i: (layer, 0, 0)),
        ],
        out_specs=pl.BlockSpec((1, tq, 256), lambda bi, i: (bi, i, 0)),
        out_shape=jax.ShapeDtypeStruct((b, s, 256), BF16),
        scratch_shapes=[
            pltpu.VMEM((n_chain, 1, tq), F32),
            pltpu.VMEM((n_chain, VT_ROWS, tq), F32),
            pltpu.VMEM((n_chain, tq, tq), F32),
        ],
        compiler_params=pltpu.CompilerParams(vmem_limit_bytes=48 * 1024 * 1024),
        name="diff_attn",
    )(lam, proj, proj, proj, proj, proj, proj, vat, g_cols)


def _band_kernel(*refs, nqb, nkb, max_dist, has_sink, want_lse, sink_base):
    it = iter(refs)
    sink_ref = next(it) if has_sink else None
    q_ref, kp_ref, kc_ref, vp_ref, vc_ref = (next(it) for _ in range(5))
    o_ref = next(it)
    lse_ref = next(it) if want_lse else None
    kbuf, vbuf = next(it), next(it)
    nseq, rows = q_ref.shape[0], q_ref.shape[1]
    i = pl.program_id(1)
    for sq in range(nseq):
        kbuf[sq, 0:BAND, :] = kp_ref[sq]
        kbuf[sq, BAND:, :] = kc_ref[sq]
        for kb in range(nkb):
            vbuf[sq, 0:BAND, kb * 256:kb * 256 + LANES] = vp_ref[sq, :, kb * LANES:(kb + 1) * LANES]
            vbuf[sq, BAND:, kb * 256:kb * 256 + LANES] = vc_ref[sq, :, kb * LANES:(kb + 1) * LANES]
            vbuf[sq, :, kb * 256 + LANES:(kb + 1) * 256] = jnp.ones((BAND + rows, LANES), BF16)
    lane = lax.broadcasted_iota(I32, (1, LANES), 1)
    lo_half = lane < 64
    r_io = lax.broadcasted_iota(I32, (BAND, 2 * BAND), 0)
    c_io = lax.broadcasted_iota(I32, (BAND, 2 * BAND), 1)
    dist = BAND + r_io - c_io
    band = (dist >= 0) & (dist <= max_dist)
    band_first = band & ((c_io >= BAND) | (i > 0))
    col0 = lax.broadcasted_iota(I32, (1, 2 * BAND), 1) == 0
    vr = lax.broadcasted_iota(I32, (2 * BAND, 2 * LANES), 0)
    vc = lax.broadcasted_iota(I32, (2 * BAND, 2 * LANES), 1)
    sink_row = (vr == 0) & (vc < LANES)
    to_log2 = HEAD_DIM ** -0.5 * math.log2(math.e)
    units = [(sq, sb, qb) for sq in range(nseq) for sb in range(rows // BAND) for qb in range(nqb)]

    def scores(u):
        sq, sb, qb = units[u]
        kb = qb if nkb > 1 else 0
        q = q_ref[sq, sb * BAND:(sb + 1) * BAND, qb * LANES:(qb + 1) * LANES].astype(F32) * to_log2
        q2 = jnp.concatenate([jnp.where(lo_half, q, 0.0), jnp.where(lo_half, 0.0, q)], axis=0).astype(BF16)
        return _dot_nt(q2, kbuf[sq, sb * BAND:(sb + 2) * BAND, kb * LANES:(kb + 1) * LANES])

    ahead = 2
    pending = [scores(u) for u in range(min(ahead, len(units)))]
    for u, (sq, sb, qb) in enumerate(units):
        if u + ahead < len(units):
            pending.append(scores(u + ahead))
        s2 = pending[u]
        pending[u] = None
        kb = qb if nkb > 1 else 0
        r0 = sb * BAND
        msk = band_first if sb == 0 else band
        halves = []
        for hh in range(2):
            if has_sink:
                fill = jnp.where(col0, sink_ref[sink_base + qb * 2 + hh] * math.log2(math.e), NEG_INF)
            else:
                fill = NEG_INF
            halves.append(jnp.where(msk, s2[hh * BAND:(hh + 1) * BAND], fill))
        s2 = jnp.concatenate(halves, axis=0)
        m = jnp.max(s2, axis=1, keepdims=True)
        p = jnp.exp2(s2 - m).astype(BF16)
        vw = vbuf[sq, r0:r0 + 2 * BAND, kb * 256:(kb + 1) * 256]
        if has_sink:
            vw = jnp.where(sink_row, jnp.zeros_like(vw), vw)
        pv = _dot(p, vw)
        den = pv[:, LANES:]
        out = pv[:, :LANES] / den
        o = jnp.where(lo_half, out[:BAND], out[BAND:])
        o_ref[sq, r0:r0 + BAND, qb * LANES:(qb + 1) * LANES] = o.astype(BF16)
        if want_lse:
            lse2 = m + jnp.log2(den)
            ls = jnp.where(lo_half, lse2[:BAND], lse2[BAND:])
            hi = ls.astype(BF16)
            lo = (ls - hi.astype(F32)).astype(BF16)
            lse_ref[sq, r0:r0 + BAND, qb * LANES:(qb + 1) * LANES] = hi
            lse_ref[sq, r0:r0 + BAND, (nqb + qb) * LANES:(nqb + qb + 1) * LANES] = lo


def _band_call(arr, q_col, k_col, v_col, nqb, nkb, max_dist, sinks=None, sink_base=0, want_lse=False,
               name="band"):
    ns, length, _ = arr.shape
    assert sinks is None or max_dist < BAND
    rows = min(BAND_ROWS, length)
    assert length % rows == 0 and BAND_ROWS % rows == 0 and ns % (BAND_ROWS // rows) == 0
    nseq = BAND_ROWS // rows
    wq, wk = nqb * LANES, nkb * LANES
    rpb = rows // BAND
    cur = lambda col, w: pl.BlockSpec((nseq, rows, w), lambda n, i, c=col // w: (n, i, c))
    prev = lambda col, w: pl.BlockSpec(
        (nseq, BAND, w), lambda n, i, c=col // w: (n, jnp.maximum(i * rpb - 1, 0), c))
    in_specs = [cur(q_col, wq), prev(k_col, wk), cur(k_col, wk), prev(v_col, wk), cur(v_col, wk)]
    args = [arr] * 5
    if sinks is not None:
        in_specs = [pl.BlockSpec(memory_space=pltpu.SMEM)] + in_specs
        args = [sinks] + args
    out_specs = [pl.BlockSpec((nseq, rows, wq), lambda n, i: (n, i, 0))]
    out_shape = [jax.ShapeDtypeStruct((ns, length, wq), BF16)]
    if want_lse:
        out_specs.append(pl.BlockSpec((nseq, rows, 2 * wq), lambda n, i: (n, i, 0)))
        out_shape.append(jax.ShapeDtypeStruct((ns, length, 2 * wq), BF16))
    return pl.pallas_call(
        functools.partial(_band_kernel, nqb=nqb, nkb=nkb, max_dist=max_dist,
                          has_sink=sinks is not None, want_lse=want_lse, sink_base=sink_base),
        grid=(ns // nseq, length // rows),
        in_specs=in_specs,
        out_specs=out_specs,
        out_shape=out_shape,
        scratch_shapes=[pltpu.VMEM((nseq, BAND + rows, wk), BF16), pltpu.VMEM((nseq, BAND + rows, 2 * wk), BF16)],
        name=name,
    )(*args)


def _outproj_router_kernel(oa_ref, ob_ref, o1_ref, l1_ref, o4_ref, l4_ref, o16_ref, l16_ref, p4t_ref, p16t_ref,
                           w_ref, x_ref, g1_ref, sc_ref, sh_ref, g_ref, wr_ref, br_ref, tri_ref,
                           xo_ref, hs_ref, meta_ref, nch_ref, wb):
    @pl.when((pl.program_id(0) == 0) & (pl.program_id(1) == 0))
    def _():
        wb[0:256, :] = w_ref[0:256, :].astype(BF16)
        for pos, head in enumerate(_SWA_HEAD_ORDER):
            wb[256 + pos * HEAD_DIM:256 + (pos + 1) * HEAD_DIM, :] = (
                w_ref[256 + head * HEAD_DIM:256 + (head + 1) * HEAD_DIM, :].astype(BF16))
        wb[768:1024, :] = w_ref[768:1024, :].astype(BF16)

    hw = o1_ref.shape[2]
    nblk = x_ref.shape[1] // TM

    def lse_of(v):
        return v[:, :hw] + v[:, hw:]

    def unpermute(pt_ref, src_ref, width):
        per = src_ref.shape[2] // nblk
        return jnp.concatenate(
            [_dot(pt_ref[...], src_ref[0, :, k * per:(k + 1) * per, :].reshape(TM, width)) for k in range(nblk)],
            axis=0)

    o1 = o1_ref[0].astype(F32)
    ls1 = lse_of(l1_ref[0].astype(F32))
    o4 = unpermute(p4t_ref, o4_ref, hw)
    ls4 = lse_of(unpermute(p4t_ref, l4_ref, 2 * hw))
    o16 = unpermute(p16t_ref, o16_ref, hw)
    ls16 = lse_of(unpermute(p16t_ref, l16_ref, 2 * hw))
    mx = jnp.maximum(jnp.maximum(ls1, ls4), ls16)
    e1, e4, e16 = jnp.exp2(ls1 - mx), jnp.exp2(ls4 - mx), jnp.exp2(ls16 - mx)
    oc = (e1 * o1 + e4 * o4 + e16 * o16) / (e1 + e4 + e16)
    mix = (_dot(oa_ref[0], wb[0:256, :]) + _dot(ob_ref[0], wb[256:768, :])
           + _dot(oc.astype(BF16), wb[768:1024, :]))
    x1 = x_ref[0] + _batch_row(g1_ref) * mix
    xo_ref[0] = x1
    _router_body(x1, sc_ref, sh_ref, g_ref, wr_ref, br_ref, tri_ref, hs_ref, meta_ref, nch_ref)


def _outproj_router_call(oa, ob, o1, l1, o4, l4, o16, l16, p4t, p16t, w_out, layer, x, mod, g, wr, br, tri):
    b, s, d = x.shape
    rows = ROWS
    nblk = rows // TM
    steps = s // rows
    nt = b * s // TM
    row = lambda w_: pl.BlockSpec((1, rows, w_), lambda bi, i: (bi, i, 0))
    res = lambda dd, w_: pl.BlockSpec((1, dd, rows // dd, w_), lambda bi, i: (bi, 0, i, 0))
    const2 = lambda bi, i: (0, 0)
    flat = lambda bi, i: bi * steps + i
    return pl.pallas_call(
        _outproj_router_kernel,
        grid=(b, steps),
        in_specs=[
            row(256), row(512), row(256), row(512),
            res(4, 256), res(4, 512), res(16, 256), res(16, 512),
            _const_spec((TM, TM), const2), _const_spec((TM, TM), const2),
            _const_spec((None, d, d), lambda bi, i: (layer, 0, 0)),
            row(d),
            _mod_spec(mod, layer, 2),
            _mod_spec(mod, layer, 4),
            _mod_spec(mod, layer, 3),
            pl.BlockSpec((None, 1, d), lambda bi, i: (layer, 0, 0)),
            _const_spec((None, 2 * ROUTER_COLS, d), lambda bi, i: (layer, 0, 0)),
            pl.BlockSpec((None, ROUTER_COLS, 1), lambda bi, i: (layer, 0, 0)),
            _const_spec((TM, TM), const2),
        ],
        out_specs=[
            row(d),
            pl.BlockSpec((nblk * SLOTS, d), lambda bi, i: (flat(bi, i), 0)),
            pl.BlockSpec((nblk, 8, TM), lambda bi, i: (flat(bi, i), 0, 0)),
            pl.BlockSpec((nblk, N_EXPERTS, LANES), lambda bi, i: (flat(bi, i), 0, 0)),
        ],
        out_shape=[
            jax.ShapeDtypeStruct((b, s, d), F32),
            jax.ShapeDtypeStruct((nt * SLOTS, d), BF16),
            jax.ShapeDtypeStruct((nt, 8, TM), F32),
            jax.ShapeDtypeStruct((nt, N_EXPERTS, LANES), I32),
        ],
        scratch_shapes=[pltpu.VMEM((d, d), BF16)],
        compiler_params=pltpu.CompilerParams(vmem_limit_bytes=56 * 1024 * 1024),
        name="out_proj_router",
    )(oa, ob, o1, l1, o4, l4, o16, l16, p4t, p16t, w_out, x, mod, mod, mod, g, wr, br, tri)


ROUTER_COLS = LANES


def _router_body(x, sc_ref, sh_ref, g_ref, wr_ref, br_ref, tri_ref, hs_ref, meta_ref, nch_ref):
    tm = TM
    h = _modulated_norm(x, g_ref[...], _batch_row(sc_ref), _batch_row(sh_ref))
    hb = h.astype(BF16)
    h_lo = (h - hb.astype(F32)).astype(BF16)
    part = _dot_nt(wr_ref[...], hb) + _dot_nt(wr_ref[...], h_lo)
    logits_t = part[:ROUTER_COLS] + part[ROUTER_COLS:] + br_ref[...]
    r8 = lax.broadcasted_iota(I32, (8, tm), 0)
    r16 = lax.broadcasted_iota(I32, (N_EXPERTS, tm), 0)
    rl = lax.broadcasted_iota(I32, (N_EXPERTS, LANES), 0)
    slot = lax.broadcasted_iota(I32, (SLOTS, tm), 0)
    for blk in range(x.shape[0] // tm):
        lt = logits_t[:, blk * tm:(blk + 1) * tm]
        glog = lt[0:8]
        elog = lt[8:8 + N_EXPERTS]

        gmax = jnp.max(glog, axis=0, keepdims=True)
        g_w = 1.0 / jnp.sum(jnp.exp(glog - gmax), axis=0, keepdims=True)
        g_idx = jnp.min(jnp.where(glog == gmax, r8, 99), axis=0, keepdims=True)

        el = jnp.where((r16 // EXPERTS_PER_GROUP) == g_idx, elog, NEG_INF)
        emax = jnp.max(el, axis=0, keepdims=True)
        e1 = jnp.min(jnp.where(el == emax, r16, 99), axis=0, keepdims=True)
        el2 = jnp.where(r16 == e1, NEG_INF, el)
        emax2 = jnp.max(el2, axis=0, keepdims=True)
        e2 = jnp.min(jnp.where(el2 == emax2, r16, 99), axis=0, keepdims=True)
        p2 = jnp.exp(emax2 - emax)
        wt1 = g_w / (1.0 + p2)
        wt2 = g_w * p2 / (1.0 + p2)

        oh1 = r16 == e1
        oh2 = r16 == e2
        onehot = jnp.where(oh1, 1.0, 0.0) + jnp.where(oh2, 1.0, 0.0)
        cnt = jnp.sum(onehot, axis=1, keepdims=True)
        nch = jnp.floor((cnt + (CHUNK - 1)) * (1.0 / CHUNK))
        nchb = jnp.broadcast_to(nch, (N_EXPERTS, LANES))
        incl = nchb
        for sft in (1, 2, 4, 8):
            incl = incl + jnp.where(rl >= sft, pltpu.roll(incl, sft, 0), 0.0)
        off = (incl - nchb)[:, 0:1] * float(CHUNK)
        rank = _dot(onehot.astype(BF16), tri_ref[...])
        slot_of = off + rank
        pos1 = jnp.sum(jnp.where(oh1, slot_of, 0.0), axis=0, keepdims=True)
        pos2 = jnp.sum(jnp.where(oh2, slot_of, 0.0), axis=0, keepdims=True)

        sel = jnp.where(slot == pos1.astype(I32), 1.0, jnp.where(slot == pos2.astype(I32), 1.0, 0.0))
        hs_ref[blk * SLOTS:(blk + 1) * SLOTS, :] = _dot(sel.astype(BF16), hb[blk * tm:(blk + 1) * tm]).astype(BF16)

        meta_ref[blk] = jnp.concatenate([pos1, pos2, wt1, wt2, jnp.zeros((4, tm), F32)], axis=0)
        nch_ref[blk] = nchb.astype(I32)


def _ffn_schedule(nch, max_tiles, dump_base):
    nt = nch.shape[0]
    cend = jnp.cumsum(nch, axis=1)
    coff = cend - nch
    tcum = jnp.cumsum(nch, axis=0)
    before = tcum - nch
    tot = tcum[-1]
    pad = ((tot + FFN_CHUNKS - 1) // FFN_CHUNKS) * FFN_CHUNKS
    eend = jnp.cumsum(pad)
    estart = eend - pad
    n_tiles = (eend[-1] // FFN_CHUNKS).astype(I32)
    first_chunk = jnp.arange(max_tiles, dtype=I32) * FFN_CHUNKS
    tile_expert = jnp.sum((eend[None, :] <= first_chunk[:, None]).astype(I32), axis=1)
    tile_expert = jnp.minimum(tile_expert, N_EXPERTS - 1)
    hp = lax.Precision.HIGHEST
    pos = jnp.arange(max_tiles * FFN_CHUNKS, dtype=I32)
    e_s = jnp.minimum(jnp.sum((eend[None, :] <= (pos // FFN_CHUNKS * FFN_CHUNKS)[:, None]).astype(I32), axis=1),
                      N_EXPERTS - 1)
    oh_e = (e_s[:, None] == jnp.arange(N_EXPERTS, dtype=I32)[None, :]).astype(F32)
    idx = pos - jnp.dot(oh_e, estart.astype(F32), precision=hp).astype(I32)
    run_end = jnp.dot(oh_e, tcum.T.astype(F32), precision=hp).astype(I32)
    run_beg = jnp.dot(oh_e, before.T.astype(F32), precision=hp).astype(I32)
    run_off = jnp.dot(oh_e, coff.T.astype(F32), precision=hp).astype(I32)
    in_run = (idx[:, None] >= run_beg) & (idx[:, None] < run_end)
    tile_base = jnp.arange(nt, dtype=I32)[None, :] * CHUNKS_PER_TILE
    src = jnp.sum(jnp.where(in_run, tile_base + run_off + idx[:, None] - run_beg, 0), axis=1)
    real = jnp.any(in_run, axis=1)
    dump = dump_base + (pos // FFN_CHUNKS % 2) * FFN_CHUNKS + pos % FFN_CHUNKS
    src_rows = jnp.where(real, src, 0) * CHUNK
    dst_rows = jnp.where(real, src, dump) * CHUNK
    used = cend[:, -1]
    ucum = jnp.cumsum(CHUNKS_PER_TILE - used)
    ubeg = ucum - (CHUNKS_PER_TILE - used)
    z = jnp.arange(max_tiles * ZERO_CHUNKS, dtype=I32)[:, None]
    in_gap = (z >= ubeg[None, :]) & (z < ucum[None, :])
    zero_rows = jnp.sum(jnp.where(in_gap, tile_base + used[None, :] + z - ubeg[None, :], 0), axis=1) * CHUNK
    n_zero = ucum[-1].astype(I32).reshape(1)
    return tile_expert, src_rows, dst_rows, n_tiles.reshape(1), zero_rows, n_zero


def _ffn_kernel(te_ref, sr_ref, dr_ref, nt_ref, zr_ref, nz_ref, hs_hbm, wg_ref, wu_ref, wd_ref, ys_hbm,
                xbuf, ybuf, zbuf, wgb, wub, wdb, in_sem, out_sem, zero_sem, *, dump_base):
    j = pl.program_id(0)
    nt = nt_ref[0]
    half_ff = EXPERT_FF // 2

    def rows_at(r):
        return pl.ds(pl.multiple_of(r, CHUNK), CHUNK)

    def in_copy(step, slot, k, wait=False):
        r = 0 if wait else sr_ref[step * FFN_CHUNKS + k]
        return pltpu.make_async_copy(hs_hbm.at[rows_at(r), :], xbuf.at[slot, pl.ds(k * CHUNK, CHUNK), :],
                                     in_sem.at[slot])

    def out_copy(step, slot, k, wait=False):
        r = 0 if wait else dr_ref[step * FFN_CHUNKS + k]
        return pltpu.make_async_copy(ybuf.at[slot, pl.ds(k * CHUNK, CHUNK), :], ys_hbm.at[rows_at(r), :],
                                     out_sem.at[slot])

    z_lo = jnp.minimum(j * ZERO_CHUNKS, nz_ref[0])
    z_hi = jnp.minimum((j + 1) * ZERO_CHUNKS, nz_ref[0])

    def zero_start(k, carry):
        pltpu.make_async_copy(zbuf.at[pl.ds(0, CHUNK), :], ys_hbm.at[rows_at(zr_ref[k]), :], zero_sem).start()
        return carry

    def zero_wait(k, carry):
        pltpu.make_async_copy(zbuf.at[pl.ds(0, CHUNK), :], ys_hbm.at[rows_at(0), :], zero_sem).wait()
        return carry

    @pl.when(j == 0)
    def _():
        zbuf[...] = jnp.zeros_like(zbuf)
        fills = [pltpu.make_async_copy(
            zbuf, ys_hbm.at[pl.ds(dump_base * CHUNK + r * ZBUF_ROWS, ZBUF_ROWS), :], zero_sem)
            for r in range(DUMP_CHUNKS * CHUNK // ZBUF_ROWS)]
        for cp in fills:
            cp.start()
        for cp in fills:
            cp.wait()

    lax.fori_loop(z_lo, z_hi, zero_start, 0)

    @pl.when(j < nt)
    def _():
        slot = j % 2

        @pl.when(j == 0)
        def _():
            for k in range(FFN_CHUNKS):
                in_copy(0, 0, k).start()

        @pl.when(j + 1 < nt)
        def _():
            for k in range(FFN_CHUNKS):
                in_copy(j + 1, 1 - slot, k).start()

        @pl.when((j == 0) | (te_ref[j] != te_ref[jnp.maximum(j - 1, 0)]))
        def _():
            wgb[...] = wg_ref[0].astype(BF16)
            wub[...] = wu_ref[0].astype(BF16)
            wdb[...] = wd_ref[0].astype(BF16)

        for k in range(FFN_CHUNKS):
            in_copy(j, slot, k, wait=True).wait()

        @pl.when(j >= 2)
        def _():
            for k in range(FFN_CHUNKS):
                out_copy(j - 2, slot, k, wait=True).wait()

        x = xbuf[slot]
        hg = [_dot(x, wgb[:, h * half_ff:(h + 1) * half_ff]) for h in range(2)]
        hu = [_dot(x, wub[:, h * half_ff:(h + 1) * half_ff]) for h in range(2)]
        y = None
        for h in range(2):
            act = ((hg[h] / (1.0 + jnp.exp(-hg[h]))) * hu[h]).astype(BF16)
            part = _dot(act, wdb[h * half_ff:(h + 1) * half_ff, :])
            y = part if y is None else y + part
        ybuf[slot] = y.astype(BF16)
        for k in range(FFN_CHUNKS):
            out_copy(j, slot, k).start()

        @pl.when(j == nt - 1)
        def _():
            for k in range(FFN_CHUNKS):
                out_copy(j, slot, k, wait=True).wait()

            @pl.when(j >= 1)
            def _():
                for k in range(FFN_CHUNKS):
                    out_copy(j - 1, 1 - slot, k, wait=True).wait()

    lax.fori_loop(z_lo, z_hi, zero_wait, 0)


def _ffn_call(tile_expert, src_rows, dst_rows, n_tiles, zero_rows, n_zero, hs, wg, wu, wd, layer, max_tiles):
    rows, d = hs.shape
    ff = wg.shape[-1]
    wmap = lambda j, te, sr, dr, nt, zr, nz: (layer, te[j], 0, 0)
    grid_spec = pltpu.PrefetchScalarGridSpec(
        num_scalar_prefetch=6,
        grid=(max_tiles,),
        in_specs=[
            pl.BlockSpec(memory_space=pl.ANY),
            pl.BlockSpec((None, 1, d, ff), wmap),
            pl.BlockSpec((None, 1, d, ff), wmap),
            pl.BlockSpec((None, 1, ff, d), wmap),
        ],
        out_specs=pl.BlockSpec(memory_space=pl.ANY),
        scratch_shapes=[
            pltpu.VMEM((2, FFN_ROWS, d), BF16),
            pltpu.VMEM((2, FFN_ROWS, d), BF16),
            pltpu.VMEM((ZBUF_ROWS, d), BF16),
            pltpu.VMEM((d, ff), BF16),
            pltpu.VMEM((d, ff), BF16),
            pltpu.VMEM((ff, d), BF16),
            pltpu.SemaphoreType.DMA((2,)),
            pltpu.SemaphoreType.DMA((2,)),
            pltpu.SemaphoreType.DMA(()),
        ],
    )
    return pl.pallas_call(
        functools.partial(_ffn_kernel, dump_base=rows // CHUNK),
        grid_spec=grid_spec,
        out_shape=jax.ShapeDtypeStruct((rows + DUMP_CHUNKS * CHUNK, d), BF16),
        compiler_params=pltpu.CompilerParams(vmem_limit_bytes=48 * 1024 * 1024),
        name="expert_ffn",
    )(tile_expert, src_rows, dst_rows, n_tiles, zero_rows, n_zero, hs, wg, wu, wd)


def _combine_body(ys_ref, meta_ref, x_ref, g2_ref):
    tm = TM
    eye = (lax.broadcasted_iota(I32, (tm, tm), 0) == lax.broadcasted_iota(I32, (tm, tm), 1))
    slot = lax.broadcasted_iota(I32, (tm, SLOTS), 1).astype(F32)

    def as_col(row):
        return jnp.sum(jnp.where(eye, row, 0.0), axis=1, keepdims=True)

    ys = []
    for blk in range(x_ref.shape[1] // tm):
        meta = meta_ref[blk]
        pos1, pos2 = as_col(meta[0:1]), as_col(meta[1:2])
        w1, w2 = as_col(meta[2:3]), as_col(meta[3:4])
        gate = jnp.where(slot == pos1, w1, 0.0) + jnp.where(slot == pos2, w2, 0.0)
        ys.append(_dot(gate.astype(BF16), ys_ref[blk * SLOTS:(blk + 1) * SLOTS, :]))
    return x_ref[0] + _batch_row(g2_ref) * jnp.concatenate(ys, axis=0)


def _combine_final_kernel(ys_ref, meta_ref, x_ref, g2_ref, gf_ref, xo_ref):
    xo = _combine_body(ys_ref, meta_ref, x_ref, g2_ref)
    xo_ref[0] = xo * lax.rsqrt(jnp.mean(xo * xo, axis=-1, keepdims=True) + NORM_EPS) * gf_ref[...]


def _combine_inproj_kernel(ys_ref, meta_ref, x_ref, g2_ref, *refs):
    n_in = 10
    xo_ref = refs[n_in]
    xo = _combine_body(ys_ref, meta_ref, x_ref, g2_ref)
    xo_ref[0] = xo
    _inproj_body(xo, *refs[:n_in], *refs[n_in + 1:])


def _combine_specs(d, steps, mod, layer):
    nblk = ROWS // TM
    flat = lambda bi, i: bi * steps + i
    return [
        pl.BlockSpec((nblk * SLOTS, d), lambda bi, i: (flat(bi, i), 0)),
        pl.BlockSpec((nblk, 8, TM), lambda bi, i: (flat(bi, i), 0, 0)),
        pl.BlockSpec((1, ROWS, d), lambda bi, i: (bi, i, 0)),
        _mod_spec(mod, layer, 5),
    ]


def _combine_final_call(ys, meta, x, mod, layer, gf):
    b, s, d = x.shape
    return pl.pallas_call(
        _combine_final_kernel,
        grid=(b, s // ROWS),
        in_specs=_combine_specs(d, s // ROWS, mod, layer) + [pl.BlockSpec((1, d), lambda bi, i: (0, 0))],
        out_specs=pl.BlockSpec((1, ROWS, d), lambda bi, i: (bi, i, 0)),
        out_shape=jax.ShapeDtypeStruct((b, s, d), F32),
        compiler_params=pltpu.CompilerParams(vmem_limit_bytes=48 * 1024 * 1024),
        name="moe_combine",
    )(ys, meta, x, mod, gf)


def _combine_inproj_call(ys, meta, x, mod, g, w_in, layer, tabs, p4, p16):
    b, s, d = x.shape
    in_specs, out_specs, out_shape = _inproj_specs(b, s, d, layer, mod)
    xspec = pl.BlockSpec((1, ROWS, d), lambda bi, i: (bi, i, 0))
    return pl.pallas_call(
        _combine_inproj_kernel,
        grid=(b, s // ROWS),
        in_specs=_combine_specs(d, s // ROWS, mod, layer - 1) + in_specs,
        out_specs=[xspec] + out_specs,
        out_shape=[jax.ShapeDtypeStruct((b, s, d), F32)] + out_shape,
        scratch_shapes=[pltpu.VMEM((d, IN_WIDTH), BF16)],
        compiler_params=pltpu.CompilerParams(vmem_limit_bytes=56 * 1024 * 1024),
        name="combine_in_proj",
    )(ys, meta, x, mod, mod, mod, g, w_in, *tabs, p4, p16)


def _rope_tables(positions):
    pos = positions.astype(F32)[..., None]

    def table(dim):
        inv = ROPE_THETA ** (-jnp.arange(0, dim, 2, dtype=F32) / dim)
        ang = pos * inv
        cos, sin = jnp.cos(ang), jnp.sin(ang)
        reps = LANES // dim
        return (jnp.tile(jnp.concatenate([cos, cos], -1), (1, 1, reps)),
                jnp.tile(jnp.concatenate([-sin, sin], -1), (1, 1, reps)))

    c64, s64 = table(HEAD_DIM)
    c32, s32 = table(DIFF_QK_DIM)
    return c64, s64, c32, s32


def kernel(x, c, positions, ada_w, ada_b, norm_mix_g, norm_ffn_g, w_in, w_out, diff_lambda_q1, diff_lambda_k1,
           diff_lambda_q2, diff_lambda_k2, diff_subln_g, swa_sinks, router_group_w, router_group_b,
           router_expert_w, router_expert_b, expert_w_gate, expert_w_up, expert_w_down, final_norm_g):
    b, s, d = x.shape
    depth = ada_w.shape[0]
    assert b <= BF16_ROWS and s % TQ == 0 and s % ROWS == 0 and s % (16 * BAND) == 0 and d == 8 * LANES
    n = b * s
    nt = n // TM
    max_tiles = (nt * CHUNKS_PER_TILE + N_EXPERTS * (FFN_CHUNKS - 1)) // FFN_CHUNKS + 1

    tabs = _rope_tables(positions)
    p4 = _residue_perm(TM, 4)
    p16 = _residue_perm(TM, 16)
    p4_b, p16_b = jnp.asarray(p4, BF16), jnp.asarray(p16, BF16)
    p4t_b, p16t_b = jnp.asarray(p4.T, BF16), jnp.asarray(p16.T, BF16)
    tri = jnp.asarray(np.triu(np.ones((TM, TM), np.float32), 1), BF16)

    c_pad = jnp.pad(c, ((0, BF16_ROWS - b), (0, 0)))
    mod = _ada_call(c_pad, ada_w, ada_b)

    sink_order = np.asarray(_SWA_HEAD_ORDER, np.int32)
    zpad = lambda k: jnp.zeros((depth, d, k), F32)
    wr = jnp.concatenate([router_group_w, zpad(8 - N_GROUPS), router_expert_w,
                          zpad(ROUTER_COLS - 8 - N_EXPERTS)], axis=-1)
    wr_hi = wr.astype(BF16)
    wr = jnp.concatenate([wr_hi, (wr - wr_hi.astype(F32)).astype(BF16)], axis=-1)
    wr = jnp.swapaxes(wr, 1, 2)
    br = jnp.concatenate([router_group_b, jnp.full((depth, 8 - N_GROUPS), NEG_INF, F32), router_expert_b,
                          jnp.zeros((depth, ROUTER_COLS - 8 - N_EXPERTS), F32)], axis=-1).reshape(depth, -1, 1)

    lam_init = [0.8 - 0.6 * math.exp(-0.3 * l) for l in range(depth)]
    lam = (jnp.exp(jnp.sum(diff_lambda_q1 * diff_lambda_k1, axis=-1))
           - jnp.exp(jnp.sum(diff_lambda_q2 * diff_lambda_k2, axis=-1)) + jnp.asarray(lam_init, F32))
    g_cols = diff_subln_g[:, :, None]
    sinks = swa_sinks[:, sink_order].reshape(-1)
    g_mix = norm_mix_g.reshape(depth, 1, d)
    g_ffn = norm_ffn_g.reshape(depth, 1, d)

    proj, vat, qkv4, qkv16 = _inproj_call(x, mod, g_mix, w_in, 0, tabs, p4_b, p16_b)
    for l in range(depth):
        oa = _diff_attn_call(proj, vat, lam, g_cols, lam_init[l], l)
        ob = _band_call(proj, COL_QB, COL_KB, COL_VB, nqb=4, nkb=1, max_dist=SWA_WINDOW - 1,
                        sinks=sinks, sink_base=l * SWA_Q_HEADS, name="swa")[0]
        o1, l1 = _band_call(proj, COL_QC, COL_KC, COL_VC, nqb=2, nkb=2, max_dist=BAND, want_lse=True,
                            name="dil1")
        o4, l4 = _band_call(qkv4.reshape(b * 4, s // 4, 768), 0, 256, 512, nqb=2, nkb=2, max_dist=BAND,
                            want_lse=True, name="dil4")
        o16, l16 = _band_call(qkv16.reshape(b * 16, s // 16, 768), 0, 256, 512, nqb=2, nkb=2, max_dist=BAND,
                              want_lse=True, name="dil16")
        x, hs, meta, nch = _outproj_router_call(
            oa, ob, o1, l1, o4.reshape(b, 4, s // 4, 256), l4.reshape(b, 4, s // 4, 512),
            o16.reshape(b, 16, s // 16, 256), l16.reshape(b, 16, s // 16, 512), p4t_b, p16t_b, w_out, l, x, mod,
            g_ffn, wr, br, tri)
        sched = _ffn_schedule(nch[:, :, 0], max_tiles, nt * CHUNKS_PER_TILE)
        ys = _ffn_call(*sched, hs, expert_w_gate, expert_w_up, expert_w_down, l, max_tiles)
        if l + 1 < depth:
            x, proj, vat, qkv4, qkv16 = _combine_inproj_call(ys, meta, x, mod, g_mix, w_in, l + 1, tabs,
                                                             p4_b, p16_b)
        else:
            x = _combine_final_call(ys, meta, x, mod, l, final_norm_g.reshape(1, d))
    return x
```

```python
import functools
import math

import numpy as np
import jax
import jax.numpy as jnp
from jax import lax
from jax.experimental import pallas as pl
from jax.experimental.pallas import tpu as pltpu

F32 = jnp.float32
BF16 = jnp.bfloat16
I32 = jnp.int32

HEAD_DIM = 64
ROPE_THETA = 10000.0
NORM_EPS = 1e-6
NEG_INF = -1e30
DIFF_HEADS = 4
DIFF_QK_DIM = 32
SWA_Q_HEADS = 8
SWA_KV_HEADS = 2
SWA_WINDOW = 128
DIL_PATTERNS = ((128, 1), (512, 4), (2048, 16))
N_GROUPS = 4
EXPERTS_PER_GROUP = 4
N_EXPERTS = 16
EXPERT_FF = 512
N_ADA = 6
IN_WIDTH = 2304

LANES = 128
BF16_ROWS = 16
BAND = 128

TM = 256
ROWS = 2 * TM
CHUNK = BF16_ROWS
SLOTS = 2 * TM + N_EXPERTS * CHUNK
CHUNKS_PER_TILE = SLOTS // CHUNK
FFN_ROWS = 1024
FFN_CHUNKS = FFN_ROWS // CHUNK
ZERO_CHUNKS = -(-N_EXPERTS * FFN_CHUNKS // CHUNKS_PER_TILE)
DUMP_CHUNKS = 2 * FFN_CHUNKS
ZBUF_ROWS = FFN_ROWS // 4
TQ = 512
BAND_ROWS = 2048

_SWA_HEAD_ORDER = (0, 4, 1, 5, 2, 6, 3, 7)
COL_QB, COL_KB, COL_VB = 0, 512, 640
COL_Q1, COL_Q2, COL_K1, COL_K2, COL_VA = 768, 896, 1024, 1152, 1280
COL_QC, COL_KC, COL_VC = 1536, 1792, 2048
_SRC_ROPE = (32, 32, 32, 32, 0, 0, 64, 64, 64, 64, 64, 0, 64, 64, 64, 64, 0, 0)
_SRC_DEST = (6, 7, 8, 9, 10, 11, None, None, None, None, 4, 5, 12, 13, 14, 15, 16, 17)
VT_ROWS = HEAD_DIM + BF16_ROWS
_SRC_VA_CHUNK = 2


def _residue_perm(tm, d):
    p = np.zeros((tm, tm), np.float32)
    per = tm // d
    for l in range(per):
        for r in range(d):
            p[r * per + l, l * d + r] = 1.0
    return p


def _dot(a, b, **kw):
    return jnp.dot(a, b, preferred_element_type=F32, **kw)


def _dot_nt(a, b):
    return lax.dot_general(a, b, (((1,), (1,)), ((), ())), preferred_element_type=F32)


def _batch_row(ref):
    return ref[pl.ds(pl.program_id(0), 1), :]


def _modulated_norm(x, g, sc, sh):
    y = x * lax.rsqrt(jnp.mean(x * x, axis=-1, keepdims=True) + NORM_EPS)
    return (y * g) * (1.0 + sc) + sh


def _ada_kernel(c_ref, w_ref, b_ref, o_ref):
    c = c_ref[...]
    ca = c / (1.0 + jnp.exp(-c))
    rows = ca.shape[0]
    c_hi = ca.astype(BF16)
    c_lo = (ca - c_hi.astype(F32)).astype(BF16)
    w = w_ref[0]
    w_hi = w.astype(BF16)
    w_lo = (w - w_hi.astype(F32)).astype(BF16)
    main = _dot(jnp.concatenate([c_hi, c_lo], axis=0), w_hi)
    o_ref[0] = main[:rows] + main[rows:] + _dot(c_hi, w_lo) + b_ref[0]


def _ada_call(c_pad, ada_w, ada_b):
    depth, d, n = ada_w.shape
    tn = n // 2
    return pl.pallas_call(
        _ada_kernel,
        grid=(depth, n // tn),
        in_specs=[
            pl.BlockSpec((c_pad.shape[0], d), lambda l, j: (0, 0)),
            pl.BlockSpec((1, d, tn), lambda l, j: (l, 0, j)),
            pl.BlockSpec((1, 1, tn), lambda l, j: (l, 0, j)),
        ],
        out_specs=pl.BlockSpec((1, c_pad.shape[0], tn), lambda l, j: (l, 0, j)),
        out_shape=jax.ShapeDtypeStruct((depth, c_pad.shape[0], n), F32),
        compiler_params=pltpu.CompilerParams(vmem_limit_bytes=40 * 1024 * 1024),
        name="ada_mod",
    )(c_pad, ada_w, ada_b.reshape(depth, 1, n))


def _rope(t, cos, sin_signed, first_half, half):
    rot = jnp.where(first_half, pltpu.roll(t, LANES - half, 1), pltpu.roll(t, half, 1))
    return t * cos + rot * sin_signed


def _inproj_body(x, sc_ref, sh_ref, g_ref, w_ref, cs64_ref, sn64_ref, cs32_ref, sn32_ref,
                 p4_ref, p16_ref, proj_ref, vat_ref, c4_ref, c16_ref, wb):
    @pl.when((pl.program_id(0) == 0) & (pl.program_id(1) == 0))
    def _():
        wb[...] = w_ref[...].astype(BF16)

    h = _modulated_norm(x, g_ref[...], _batch_row(sc_ref), _batch_row(sh_ref))
    hb = h.astype(BF16)
    lane = lax.broadcasted_iota(I32, (1, LANES), 1)
    first64 = (lane % 64) < 32
    first32 = (lane % 32) < 16
    lo_half = lane < 64
    swa_q = []
    for cb in range(IN_WIDTH // 256):
        if cb == IN_WIDTH // 256 - 1:
            half = hb.shape[0] // 2
            acc = jnp.concatenate([_dot(hb[:half], wb[:, cb * 256:(cb + 1) * 256]),
                                   _dot(hb[half:], wb[:, cb * 256:(cb + 1) * 256])], axis=0)
        else:
            acc = _dot(hb, wb[:, cb * 256:(cb + 1) * 256])
        if cb == _SRC_VA_CHUNK:
            acc_t = acc.T.astype(BF16)
            for hd in range(DIFF_HEADS):
                vat_ref[0, hd * VT_ROWS:hd * VT_ROWS + HEAD_DIM, :] = acc_t[hd * HEAD_DIM:(hd + 1) * HEAD_DIM]
                vat_ref[0, hd * VT_ROWS + HEAD_DIM:(hd + 1) * VT_ROWS, :] = jnp.ones(
                    (BF16_ROWS, acc_t.shape[1]), BF16)
        for half in range(2):
            src = cb * 2 + half
            t = acc[:, half * LANES:(half + 1) * LANES]
            if _SRC_ROPE[src] == 64:
                t = _rope(t, cs64_ref[0], sn64_ref[0], first64, 32)
            elif _SRC_ROPE[src] == 32:
                t = _rope(t, cs32_ref[0], sn32_ref[0], first32, 16)
            dst = _SRC_DEST[src]
            if dst is None:
                swa_q.append(t)
            else:
                proj_ref[0, :, dst * LANES:(dst + 1) * LANES] = t.astype(BF16)
    for jb in range(SWA_Q_HEADS // 2):
        a, c = swa_q[jb // 2], swa_q[2 + jb // 2]
        if jb % 2 == 0:
            blk = jnp.where(lo_half, a, pltpu.roll(c, 64, 1))
        else:
            blk = jnp.where(lo_half, pltpu.roll(a, 64, 1), c)
        proj_ref[0, :, jb * LANES:(jb + 1) * LANES] = blk.astype(BF16)
    tm = TM
    for blk in range(x.shape[0] // tm):
        cc = proj_ref[0, blk * tm:(blk + 1) * tm, COL_QC:]
        c4 = _dot(p4_ref[...], cc).astype(BF16)
        for r in range(4):
            c4_ref[0, r, blk * (tm // 4):(blk + 1) * (tm // 4)] = c4[r * (tm // 4):(r + 1) * (tm // 4)]
        c16 = _dot(p16_ref[...], cc).astype(BF16)
        for r in range(16):
            c16_ref[0, r, blk * (tm // 16):(blk + 1) * (tm // 16)] = c16[r * (tm // 16):(r + 1) * (tm // 16)]


def _inproj_kernel(x_ref, *refs):
    _inproj_body(x_ref[0], *refs)


def _mod_spec(mod, layer, k):
    return pl.BlockSpec((None, mod.shape[1], mod.shape[2] // N_ADA), lambda bi, i: (layer, 0, k))


def _const_spec(shape, index_map):
    return pl.BlockSpec(shape, index_map, pipeline_mode=pl.Buffered(1))


def _inproj_specs(b, s, d, layer, mod):
    rows = ROWS
    row = lambda bi, i: (bi, i, 0)
    const2 = lambda bi, i: (0, 0)
    in_specs = [
        _mod_spec(mod, layer, 1),
        _mod_spec(mod, layer, 0),
        pl.BlockSpec((None, 1, d), lambda bi, i: (layer, 0, 0)),
        _const_spec((None, d, IN_WIDTH), lambda bi, i: (layer, 0, 0)),
        pl.BlockSpec((1, rows, LANES), row),
        pl.BlockSpec((1, rows, LANES), row),
        pl.BlockSpec((1, rows, LANES), row),
        pl.BlockSpec((1, rows, LANES), row),
        _const_spec((TM, TM), const2),
        _const_spec((TM, TM), const2),
    ]
    out_specs = [
        pl.BlockSpec((1, rows, IN_WIDTH), row),
        pl.BlockSpec((1, DIFF_HEADS * VT_ROWS, rows), lambda bi, i: (bi, 0, i)),
        pl.BlockSpec((1, 4, rows // 4, 768), lambda bi, i: (bi, 0, i, 0)),
        pl.BlockSpec((1, 16, rows // 16, 768), lambda bi, i: (bi, 0, i, 0)),
    ]
    out_shape = [
        jax.ShapeDtypeStruct((b, s, IN_WIDTH), BF16),
        jax.ShapeDtypeStruct((b, DIFF_HEADS * VT_ROWS, s), BF16),
        jax.ShapeDtypeStruct((b, 4, s // 4, 768), BF16),
        jax.ShapeDtypeStruct((b, 16, s // 16, 768), BF16),
    ]
    return in_specs, out_specs, out_shape


def _inproj_call(x, mod, g, w_in, layer, tabs, p4, p16):
    b, s, d = x.shape
    in_specs, out_specs, out_shape = _inproj_specs(b, s, d, layer, mod)
    return pl.pallas_call(
        _inproj_kernel,
        grid=(b, s // ROWS),
        in_specs=[pl.BlockSpec((1, ROWS, d), lambda bi, i: (bi, i, 0))] + in_specs,
        out_specs=out_specs,
        out_shape=out_shape,
        scratch_shapes=[pltpu.VMEM((d, IN_WIDTH), BF16)],
        compiler_params=pltpu.CompilerParams(vmem_limit_bytes=56 * 1024 * 1024),
        name="in_proj",
    )(x, mod, mod, g, w_in, *tabs, p4, p16)


def _diff_attn_kernel(lam_ref, q1_ref, q2_ref, q1n_ref, q2n_ref, k1_ref, k2_ref, vt_ref, g_ref, o_ref,
                      m_sc, acc_sc, s_sc, *, lambda_init, layer):
    tq = q1_ref.shape[1]
    qi = pl.program_id(1)
    lam = lam_ref[layer]
    to_log2 = DIFF_QK_DIM ** -0.5 * math.log2(math.e)
    lane = lax.broadcasted_iota(I32, (1, LANES), 1)

    def head_queries(qa_ref, qb_ref):
        qa = qa_ref[0].astype(F32) * to_log2
        qb = qb_ref[0].astype(F32) * to_log2
        out = []
        for h in range(DIFF_HEADS):
            hm = (lane // DIFF_QK_DIM) == h
            out.append((jnp.where(hm, qa, 0.0).astype(BF16), jnp.where(hm, qb, 0.0).astype(BF16)))
        return out

    qh = head_queries(q1_ref, q2_ref)
    qh_next = head_queries(q1n_ref, q2n_ref)
    causal = (lax.broadcasted_iota(I32, (tq, tq), 0) <= lax.broadcasted_iota(I32, (tq, tq), 1))

    m_sc[...] = jnp.full(m_sc.shape, NEG_INF, F32)
    acc_sc[...] = jnp.zeros(acc_sc.shape, F32)

    n_chain = 2 * DIFF_HEADS

    def scores(ch, tile, queries):
        start = pl.multiple_of(tile * tq, tq)
        k_ref = k1_ref if ch % 2 == 0 else k2_ref
        return _dot_nt(k_ref[0, pl.ds(start, tq), :], queries[ch // 2][ch % 2])

    @pl.when(qi == 0)
    def _():
        for ch in range(n_chain):
            s_sc[ch] = scores(ch, 0, qh)

    def step(j, last):
        start = pl.multiple_of(j * tq, tq)
        for ch in range(n_chain):
            st = s_sc[ch]
            s_sc[ch] = scores(ch, 0, qh_next) if last else scores(ch, j + 1, qh)
            h = ch // 2
            vt = vt_ref[0, h * VT_ROWS:(h + 1) * VT_ROWS, pl.ds(start, tq)]
            if last:
                st = jnp.where(causal, st, NEG_INF)
            m_old = m_sc[ch]
            m_new = jnp.maximum(m_old, jnp.max(st, axis=0, keepdims=True))
            p = jnp.exp2(st - m_new)
            al = jnp.exp2(m_old - m_new)
            acc_sc[ch] = al * acc_sc[ch] + _dot(vt, p.astype(BF16))
            m_sc[ch] = m_new

    def body(j, carry):
        step(j, False)
        return carry

    lax.fori_loop(0, qi, body, 0)
    step(qi, True)

    g = g_ref[...]
    outs = []
    for h in range(DIFF_HEADS):
        a1, a2 = acc_sc[2 * h], acc_sc[2 * h + 1]
        o = (a1[:HEAD_DIM] / a1[HEAD_DIM:HEAD_DIM + 1]
             - lam * (a2[:HEAD_DIM] / a2[HEAD_DIM:HEAD_DIM + 1]))
        ms = jnp.mean(o * o, axis=0, keepdims=True)
        outs.append((o * lax.rsqrt(ms + NORM_EPS)) * g * (1.0 - lambda_init))
    o_ref[0] = jnp.concatenate(outs, axis=0).T.astype(BF16)


def _diff_attn_call(proj, vat, lam, g_cols, lambda_init, layer):
    b, s, _ = proj.shape
    tq = TQ
    qspec = lambda cb: pl.BlockSpec((1, tq, LANES), lambda bi, i, cb=cb: (bi, i, cb))
    last_q = s // tq - 1
    qnext = lambda cb: pl.BlockSpec((1, tq, LANES), lambda bi, i, cb=cb: (bi, jnp.minimum(i + 1, last_q), cb))
    kspec = lambda cb: pl.BlockSpec((1, s, LANES), lambda bi, i, cb=cb: (bi, 0, cb))
    n_chain = 2 * DIFF_HEADS
    return pl.pallas_call(
        functools.partial(_diff_attn_kernel, lambda_init=lambda_init, layer=layer),
        grid=(b, s // tq),
        in_specs=[
            pl.BlockSpec(memory_space=pltpu.SMEM),
            qspec(COL_Q1 // LANES), qspec(COL_Q2 // LANES),
            qnext(COL_Q1 // LANES), qnext(COL_Q2 // LANES),
            kspec(COL_K1 // LANES), kspec(COL_K2 // LANES),
            pl.BlockSpec((1, DIFF_HEADS * VT_ROWS, s), lambda bi, i: (bi, 0, 0)),
            pl.BlockSpec((None, HEAD_DIM, 1), lambda bi, i: (layer, 0, 0)),
        ],
        out_specs=pl.BlockSpec((1, tq, 256), lambda bi, i: (bi, i, 0)),
        out_shape=jax.ShapeDtypeStruct((b, s, 256), BF16),
        scratch_shapes=[
            pltpu.VMEM((n_chain, 1, tq), F32),
            pltpu.VMEM((n_chain, VT_ROWS, tq), F32),
            pltpu.VMEM((n_chain, tq, tq), F32),
        ],
        compiler_params=pltpu.CompilerParams(vmem_limit_bytes=48 * 1024 * 1024),
        name="diff_attn",
    )(lam, proj, proj, proj, proj, proj, proj, vat, g_cols)


def _band_kernel(*refs, nqb, nkb, max_dist, has_sink, want_lse, sink_base):
    it = iter(refs)
    sink_ref = next(it) if has_sink else None
    q_ref, kp_ref, kc_ref, vp_ref, vc_ref = (next(it) for _ in range(5))
    o_ref = next(it)
    lse_ref = next(it) if want_lse else None
    kbuf, vbuf = next(it), next(it)
    nseq, rows = q_ref.shape[0], q_ref.shape[1]
    i = pl.program_id(1)
    for sq in range(nseq):
        kbuf[sq, 0:BAND, :] = kp_ref[sq]
        kbuf[sq, BAND:, :] = kc_ref[sq]
        for kb in range(nkb):
            vbuf[sq, 0:BAND, kb * 256:kb * 256 + LANES] = vp_ref[sq, :, kb * LANES:(kb + 1) * LANES]
            vbuf[sq, BAND:, kb * 256:kb * 256 + LANES] = vc_ref[sq, :, kb * LANES:(kb + 1) * LANES]
            vbuf[sq, :, kb * 256 + LANES:(kb + 1) * 256] = jnp.ones((BAND + rows, LANES), BF16)
    lane = lax.broadcasted_iota(I32, (1, LANES), 1)
    lo_half = lane < 64
    r_io = lax.broadcasted_iota(I32, (BAND, 2 * BAND), 0)
    c_io = lax.broadcasted_iota(I32, (BAND, 2 * BAND), 1)
    dist = BAND + r_io - c_io
    band = (dist >= 0) & (dist <= max_dist)
    band_first = band & ((c_io >= BAND) | (i > 0))
    col0 = lax.broadcasted_iota(I32, (1, 2 * BAND), 1) == 0
    vr = lax.broadcasted_iota(I32, (2 * BAND, 2 * LANES), 0)
    vc = lax.broadcasted_iota(I32, (2 * BAND, 2 * LANES), 1)
    sink_row = (vr == 0) & (vc < LANES)
    to_log2 = HEAD_DIM ** -0.5 * math.log2(math.e)
    units = [(sq, sb, qb) for sq in range(nseq) for sb in range(rows // BAND) for qb in range(nqb)]

    def scores(u):
        sq, sb, qb = units[u]
        kb = qb if nkb > 1 else 0
        q = q_ref[sq, sb * BAND:(sb + 1) * BAND, qb * LANES:(qb + 1) * LANES].astype(F32) * to_log2
        q2 = jnp.concatenate([jnp.where(lo_half, q, 0.0), jnp.where(lo_half, 0.0, q)], axis=0).astype(BF16)
        return _dot_nt(q2, kbuf[sq, sb * BAND:(sb + 2) * BAND, kb * LANES:(kb + 1) * LANES])

    ahead = 2
    pending = [scores(u) for u in range(min(ahead, len(units)))]
    for u, (sq, sb, qb) in enumerate(units):
        if u + ahead < len(units):
            pending.append(scores(u + ahead))
        s2 = pending[u]
        pending[u] = None
        kb = qb if nkb > 1 else 0
        r0 = sb * BAND
        msk = band_first if sb == 0 else band
        halves = []
        for hh in range(2):
            if has_sink:
                fill = jnp.where(col0, sink_ref[sink_base + qb * 2 + hh] * math.log2(math.e), NEG_INF)
            else:
                fill = NEG_INF
            halves.append(jnp.where(msk, s2[hh * BAND:(hh + 1) * BAND], fill))
        s2 = jnp.concatenate(halves, axis=0)
        m = jnp.max(s2, axis=1, keepdims=True)
        p = jnp.exp2((s2 - m).astype(BF16))
        vw = vbuf[sq, r0:r0 + 2 * BAND, kb * 256:(kb + 1) * 256]
        if has_sink:
            vw = jnp.where(sink_row, jnp.zeros_like(vw), vw)
        pv = _dot(p, vw)
        den = pv[:, LANES:]
        out = pv[:, :LANES] / den
        o = jnp.where(lo_half, out[:BAND], out[BAND:])
        o_ref[sq, r0:r0 + BAND, qb * LANES:(qb + 1) * LANES] = o.astype(BF16)
        if want_lse:
            lse2 = m + jnp.log2(den)
            ls = jnp.where(lo_half, lse2[:BAND], lse2[BAND:])
            hi = ls.astype(BF16)
            lo = (ls - hi.astype(F32)).astype(BF16)
            lse_ref[sq, r0:r0 + BAND, qb * LANES:(qb + 1) * LANES] = hi
            lse_ref[sq, r0:r0 + BAND, (nqb + qb) * LANES:(nqb + qb + 1) * LANES] = lo


def _band_call(arr, q_col, k_col, v_col, nqb, nkb, max_dist, sinks=None, sink_base=0, want_lse=False,
               name="band"):
    ns, length, _ = arr.shape
    assert sinks is None or max_dist < BAND
    rows = min(BAND_ROWS, length)
    assert length % rows == 0 and BAND_ROWS % rows == 0 and ns % (BAND_ROWS // rows) == 0
    nseq = BAND_ROWS // rows
    wq, wk = nqb * LANES, nkb * LANES
    rpb = rows // BAND
    cur = lambda col, w: pl.BlockSpec((nseq, rows, w), lambda n, i, c=col // w: (n, i, c))
    prev = lambda col, w: pl.BlockSpec(
        (nseq, BAND, w), lambda n, i, c=col // w: (n, jnp.maximum(i * rpb - 1, 0), c))
    in_specs = [cur(q_col, wq), prev(k_col, wk), cur(k_col, wk), prev(v_col, wk), cur(v_col, wk)]
    args = [arr] * 5
    if sinks is not None:
        in_specs = [pl.BlockSpec(memory_space=pltpu.SMEM)] + in_specs
        args = [sinks] + args
    out_specs = [pl.BlockSpec((nseq, rows, wq), lambda n, i: (n, i, 0))]
    out_shape = [jax.ShapeDtypeStruct((ns, length, wq), BF16)]
    if want_lse:
        out_specs.append(pl.BlockSpec((nseq, rows, 2 * wq), lambda n, i: (n, i, 0)))
        out_shape.append(jax.ShapeDtypeStruct((ns, length, 2 * wq), BF16))
    return pl.pallas_call(
        functools.partial(_band_kernel, nqb=nqb, nkb=nkb, max_dist=max_dist,
                          has_sink=sinks is not None, want_lse=want_lse, sink_base=sink_base),
        grid=(ns // nseq, length // rows),
        in_specs=in_specs,
        out_specs=out_specs,
        out_shape=out_shape,
        scratch_shapes=[pltpu.VMEM((nseq, BAND + rows, wk), BF16), pltpu.VMEM((nseq, BAND + rows, 2 * wk), BF16)],
        name=name,
    )(*args)


def _outproj_router_kernel(oa_ref, ob_ref, o1_ref, l1_ref, o4_ref, l4_ref, o16_ref, l16_ref, p4t_ref, p16t_ref,
                           w_ref, x_ref, g1_ref, sc_ref, sh_ref, g_ref, wr_ref, br_ref, tri_ref,
                           xo_ref, hs_ref, meta_ref, nch_ref, wb):
    @pl.when((pl.program_id(0) == 0) & (pl.program_id(1) == 0))
    def _():
        wb[0:256, :] = w_ref[0:256, :].astype(BF16)
        for pos, head in enumerate(_SWA_HEAD_ORDER):
            wb[256 + pos * HEAD_DIM:256 + (pos + 1) * HEAD_DIM, :] = (
                w_ref[256 + head * HEAD_DIM:256 + (head + 1) * HEAD_DIM, :].astype(BF16))
        wb[768:1024, :] = w_ref[768:1024, :].astype(BF16)

    hw = o1_ref.shape[2]
    nblk = x_ref.shape[1] // TM

    def lse_of(v):
        return v[:, :hw] + v[:, hw:]

    def unpermute(pt_ref, src_ref, width):
        per = src_ref.shape[2] // nblk
        return jnp.concatenate(
            [_dot(pt_ref[...], src_ref[0, :, k * per:(k + 1) * per, :].reshape(TM, width)) for k in range(nblk)],
            axis=0)

    o1 = o1_ref[0].astype(F32)
    ls1 = lse_of(l1_ref[0].astype(F32))
    o4 = unpermute(p4t_ref, o4_ref, hw)
    ls4 = lse_of(unpermute(p4t_ref, l4_ref, 2 * hw))
    o16 = unpermute(p16t_ref, o16_ref, hw)
    ls16 = lse_of(unpermute(p16t_ref, l16_ref, 2 * hw))
    mx = jnp.maximum(jnp.maximum(ls1, ls4), ls16)
    e1, e4, e16 = jnp.exp2(ls1 - mx), jnp.exp2(ls4 - mx), jnp.exp2(ls16 - mx)
    oc = (e1 * o1 + e4 * o4 + e16 * o16) / (e1 + e4 + e16)
    mix = (_dot(oa_ref[0], wb[0:256, :]) + _dot(ob_ref[0], wb[256:768, :])
           + _dot(oc.astype(BF16), wb[768:1024, :]))
    x1 = x_ref[0] + _batch_row(g1_ref) * mix
    xo_ref[0] = x1
    _router_body(x1, sc_ref, sh_ref, g_ref, wr_ref, br_ref, tri_ref, hs_ref, meta_ref, nch_ref)


def _outproj_router_call(oa, ob, o1, l1, o4, l4, o16, l16, p4t, p16t, w_out, layer, x, mod, g, wr, br, tri):
    b, s, d = x.shape
    rows = ROWS
    nblk = rows // TM
    steps = s // rows
    nt = b * s // TM
    row = lambda w_: pl.BlockSpec((1, rows, w_), lambda bi, i: (bi, i, 0))
    res = lambda dd, w_: pl.BlockSpec((1, dd, rows // dd, w_), lambda bi, i: (bi, 0, i, 0))
    const2 = lambda bi, i: (0, 0)
    flat = lambda bi, i: bi * steps + i
    return pl.pallas_call(
        _outproj_router_kernel,
        grid=(b, steps),
        in_specs=[
            row(256), row(512), row(256), row(512),
            res(4, 256), res(4, 512), res(16, 256), res(16, 512),
            _const_spec((TM, TM), const2), _const_spec((TM, TM), const2),
            _const_spec((None, d, d), lambda bi, i: (layer, 0, 0)),
            row(d),
            _mod_spec(mod, layer, 2),
            _mod_spec(mod, layer, 4),
            _mod_spec(mod, layer, 3),
            pl.BlockSpec((None, 1, d), lambda bi, i: (layer, 0, 0)),
            _const_spec((None, 2 * ROUTER_COLS, d), lambda bi, i: (layer, 0, 0)),
            pl.BlockSpec((None, ROUTER_COLS, 1), lambda bi, i: (layer, 0, 0)),
            _const_spec((TM, TM), const2),
        ],
        out_specs=[
            row(d),
            pl.BlockSpec((nblk * SLOTS, d), lambda bi, i: (flat(bi, i), 0)),
            pl.BlockSpec((nblk, 8, TM), lambda bi, i: (flat(bi, i), 0, 0)),
            pl.BlockSpec((nblk, N_EXPERTS, LANES), lambda bi, i: (flat(bi, i), 0, 0)),
        ],
        out_shape=[
            jax.ShapeDtypeStruct((b, s, d), F32),
            jax.ShapeDtypeStruct((nt * SLOTS, d), BF16),
            jax.ShapeDtypeStruct((nt, 8, TM), F32),
            jax.ShapeDtypeStruct((nt, N_EXPERTS, LANES), I32),
        ],
        scratch_shapes=[pltpu.VMEM((d, d), BF16)],
        compiler_params=pltpu.CompilerParams(vmem_limit_bytes=56 * 1024 * 1024),
        name="out_proj_router",
    )(oa, ob, o1, l1, o4, l4, o16, l16, p4t, p16t, w_out, x, mod, mod, mod, g, wr, br, tri)


ROUTER_COLS = LANES


def _router_body(x, sc_ref, sh_ref, g_ref, wr_ref, br_ref, tri_ref, hs_ref, meta_ref, nch_ref):
    tm = TM
    h = _modulated_norm(x, g_ref[...], _batch_row(sc_ref), _batch_row(sh_ref))
    hb = h.astype(BF16)
    h_lo = (h - hb.astype(F32)).astype(BF16)
    part = _dot_nt(wr_ref[...], hb) + _dot_nt(wr_ref[...], h_lo)
    logits_t = part[:ROUTER_COLS] + part[ROUTER_COLS:] + br_ref[...]
    r8 = lax.broadcasted_iota(I32, (8, tm), 0)
    r16 = lax.broadcasted_iota(I32, (N_EXPERTS, tm), 0)
    rl = lax.broadcasted_iota(I32, (N_EXPERTS, LANES), 0)
    slot = lax.broadcasted_iota(I32, (SLOTS, tm), 0)
    for blk in range(x.shape[0] // tm):
        lt = logits_t[:, blk * tm:(blk + 1) * tm]
        glog = lt[0:8]
        elog = lt[8:8 + N_EXPERTS]

        gmax = jnp.max(glog, axis=0, keepdims=True)
        g_w = 1.0 / jnp.sum(jnp.exp(glog - gmax), axis=0, keepdims=True)
        g_idx = jnp.min(jnp.where(glog == gmax, r8, 99), axis=0, keepdims=True)

        el = jnp.where((r16 // EXPERTS_PER_GROUP) == g_idx, elog, NEG_INF)
        emax = jnp.max(el, axis=0, keepdims=True)
        e1 = jnp.min(jnp.where(el == emax, r16, 99), axis=0, keepdims=True)
        el2 = jnp.where(r16 == e1, NEG_INF, el)
        emax2 = jnp.max(el2, axis=0, keepdims=True)
        e2 = jnp.min(jnp.where(el2 == emax2, r16, 99), axis=0, keepdims=True)
        p2 = jnp.exp(emax2 - emax)
        wt1 = g_w / (1.0 + p2)
        wt2 = g_w * p2 / (1.0 + p2)

        oh1 = r16 == e1
        oh2 = r16 == e2
        onehot = jnp.where(oh1, 1.0, 0.0) + jnp.where(oh2, 1.0, 0.0)
        cnt = jnp.sum(onehot, axis=1, keepdims=True)
        nch = jnp.floor((cnt + (CHUNK - 1)) * (1.0 / CHUNK))
        nchb = jnp.broadcast_to(nch, (N_EXPERTS, LANES))
        incl = nchb
        for sft in (1, 2, 4, 8):
            incl = incl + jnp.where(rl >= sft, pltpu.roll(incl, sft, 0), 0.0)
        off = (incl - nchb)[:, 0:1] * float(CHUNK)
        rank = _dot(onehot.astype(BF16), tri_ref[...])
        slot_of = off + rank
        pos1 = jnp.sum(jnp.where(oh1, slot_of, 0.0), axis=0, keepdims=True)
        pos2 = jnp.sum(jnp.where(oh2, slot_of, 0.0), axis=0, keepdims=True)

        sel = jnp.where(slot == pos1.astype(I32), 1.0, jnp.where(slot == pos2.astype(I32), 1.0, 0.0))
        hs_ref[blk * SLOTS:(blk + 1) * SLOTS, :] = _dot(sel.astype(BF16), hb[blk * tm:(blk + 1) * tm]).astype(BF16)

        meta_ref[blk] = jnp.concatenate([pos1, pos2, wt1, wt2, jnp.zeros((4, tm), F32)], axis=0)
        nch_ref[blk] = nchb.astype(I32)


def _ffn_schedule(nch, max_tiles, dump_base):
    nt = nch.shape[0]
    cend = jnp.cumsum(nch, axis=1)
    coff = cend - nch
    tcum = jnp.cumsum(nch, axis=0)
    before = tcum - nch
    tot = tcum[-1]
    pad = ((tot + FFN_CHUNKS - 1) // FFN_CHUNKS) * FFN_CHUNKS
    eend = jnp.cumsum(pad)
    estart = eend - pad
    n_tiles = (eend[-1] // FFN_CHUNKS).astype(I32)
    first_chunk = jnp.arange(max_tiles, dtype=I32) * FFN_CHUNKS
    tile_expert = jnp.sum((eend[None, :] <= first_chunk[:, None]).astype(I32), axis=1)
    tile_expert = jnp.minimum(tile_expert, N_EXPERTS - 1)
    hp = lax.Precision.HIGHEST
    pos = jnp.arange(max_tiles * FFN_CHUNKS, dtype=I32)
    e_s = jnp.minimum(jnp.sum((eend[None, :] <= (pos // FFN_CHUNKS * FFN_CHUNKS)[:, None]).astype(I32), axis=1),
                      N_EXPERTS - 1)
    oh_e = (e_s[:, None] == jnp.arange(N_EXPERTS, dtype=I32)[None, :]).astype(F32)
    idx = pos - jnp.dot(oh_e, estart.astype(F32), precision=hp).astype(I32)
    run_end = jnp.dot(oh_e, tcum.T.astype(F32), precision=hp).astype(I32)
    run_beg = jnp.dot(oh_e, before.T.astype(F32), precision=hp).astype(I32)
    run_off = jnp.dot(oh_e, coff.T.astype(F32), precision=hp).astype(I32)
    in_run = (idx[:, None] >= run_beg) & (idx[:, None] < run_end)
    tile_base = jnp.arange(nt, dtype=I32)[None, :] * CHUNKS_PER_TILE
    src = jnp.sum(jnp.where(in_run, tile_base + run_off + idx[:, None] - run_beg, 0), axis=1)
    real = jnp.any(in_run, axis=1)
    dump = dump_base + (pos // FFN_CHUNKS % 2) * FFN_CHUNKS + pos % FFN_CHUNKS
    src_rows = jnp.where(real, src, 0) * CHUNK
    dst_rows = jnp.where(real, src, dump) * CHUNK
    used = cend[:, -1]
    ucum = jnp.cumsum(CHUNKS_PER_TILE - used)
    ubeg = ucum - (CHUNKS_PER_TILE - used)
    z = jnp.arange(max_tiles * ZERO_CHUNKS, dtype=I32)[:, None]
    in_gap = (z >= ubeg[None, :]) & (z < ucum[None, :])
    zero_rows = jnp.sum(jnp.where(in_gap, tile_base + used[None, :] + z - ubeg[None, :], 0), axis=1) * CHUNK
    n_zero = ucum[-1].astype(I32).reshape(1)
    return tile_expert, src_rows, dst_rows, n_tiles.reshape(1), zero_rows, n_zero


def _ffn_kernel(te_ref, sr_ref, dr_ref, nt_ref, zr_ref, nz_ref, hs_hbm, wg_ref, wu_ref, wd_ref, ys_hbm,
                xbuf, ybuf, zbuf, wgb, wub, wdb, in_sem, out_sem, zero_sem, *, dump_base):
    j = pl.program_id(0)
    nt = nt_ref[0]
    half_ff = EXPERT_FF // 2

    def rows_at(r):
        return pl.ds(pl.multiple_of(r, CHUNK), CHUNK)

    def in_copy(step, slot, k, wait=False):
        r = 0 if wait else sr_ref[step * FFN_CHUNKS + k]
        return pltpu.make_async_copy(hs_hbm.at[rows_at(r), :], xbuf.at[slot, pl.ds(k * CHUNK, CHUNK), :],
                                     in_sem.at[slot])

    def out_copy(step, slot, k, wait=False):
        r = 0 if wait else dr_ref[step * FFN_CHUNKS + k]
        return pltpu.make_async_copy(ybuf.at[slot, pl.ds(k * CHUNK, CHUNK), :], ys_hbm.at[rows_at(r), :],
                                     out_sem.at[slot])

    z_lo = jnp.minimum(j * ZERO_CHUNKS, nz_ref[0])
    z_hi = jnp.minimum((j + 1) * ZERO_CHUNKS, nz_ref[0])

    def zero_start(k, carry):
        pltpu.make_async_copy(zbuf.at[pl.ds(0, CHUNK), :], ys_hbm.at[rows_at(zr_ref[k]), :], zero_sem).start()
        return carry

    def zero_wait(k, carry):
        pltpu.make_async_copy(zbuf.at[pl.ds(0, CHUNK), :], ys_hbm.at[rows_at(0), :], zero_sem).wait()
        return carry

    @pl.when(j == 0)
    def _():
        zbuf[...] = jnp.zeros_like(zbuf)
        fills = [pltpu.make_async_copy(
            zbuf, ys_hbm.at[pl.ds(dump_base * CHUNK + r * ZBUF_ROWS, ZBUF_ROWS), :], zero_sem)
            for r in range(DUMP_CHUNKS * CHUNK // ZBUF_ROWS)]
        for cp in fills:
            cp.start()
        for cp in fills:
            cp.wait()

    lax.fori_loop(z_lo, z_hi, zero_start, 0)

    @pl.when(j < nt)
    def _():
        slot = j % 2

        @pl.when(j == 0)
        def _():
            for k in range(FFN_CHUNKS):
                in_copy(0, 0, k).start()

        @pl.when(j + 1 < nt)
        def _():
            for k in range(FFN_CHUNKS):
                in_copy(j + 1, 1 - slot, k).start()

        @pl.when((j == 0) | (te_ref[j] != te_ref[jnp.maximum(j - 1, 0)]))
        def _():
            wgb[...] = wg_ref[0].astype(BF16)
            wub[...] = wu_ref[0].astype(BF16)
            wdb[...] = wd_ref[0].astype(BF16)

        for k in range(FFN_CHUNKS):
            in_copy(j, slot, k, wait=True).wait()

        @pl.when(j >= 2)
        def _():
            for k in range(FFN_CHUNKS):
                out_copy(j - 2, slot, k, wait=True).wait()

        x = xbuf[slot]
        hg = [_dot(x, wgb[:, h * half_ff:(h + 1) * half_ff]) for h in range(2)]
        hu = [_dot(x, wub[:, h * half_ff:(h + 1) * half_ff]) for h in range(2)]
        y = None
        for h in range(2):
            act = ((hg[h] / (1.0 + jnp.exp(-hg[h]))) * hu[h]).astype(BF16)
            part = _dot(act, wdb[h * half_ff:(h + 1) * half_ff, :])
            y = part if y is None else y + part
        ybuf[slot] = y.astype(BF16)
        for k in range(FFN_CHUNKS):
            out_copy(j, slot, k).start()

        @pl.when(j == nt - 1)
        def _():
            for k in range(FFN_CHUNKS):
                out_copy(j, slot, k, wait=True).wait()

            @pl.when(j >= 1)
            def _():
                for k in range(FFN_CHUNKS):
                    out_copy(j - 1, 1 - slot, k, wait=True).wait()

    lax.fori_loop(z_lo, z_hi, zero_wait, 0)


def _ffn_call(tile_expert, src_rows, dst_rows, n_tiles, zero_rows, n_zero, hs, wg, wu, wd, layer, max_tiles):
    rows, d = hs.shape
    ff = wg.shape[-1]
    wmap = lambda j, te, sr, dr, nt, zr, nz: (layer, te[j], 0, 0)
    grid_spec = pltpu.PrefetchScalarGridSpec(
        num_scalar_prefetch=6,
        grid=(max_tiles,),
        in_specs=[
            pl.BlockSpec(memory_space=pl.ANY),
            pl.BlockSpec((None, 1, d, ff), wmap),
            pl.BlockSpec((None, 1, d, ff), wmap),
            pl.BlockSpec((None, 1, ff, d), wmap),
        ],
        out_specs=pl.BlockSpec(memory_space=pl.ANY),
        scratch_shapes=[
            pltpu.VMEM((2, FFN_ROWS, d), BF16),
            pltpu.VMEM((2, FFN_ROWS, d), BF16),
            pltpu.VMEM((ZBUF_ROWS, d), BF16),
            pltpu.VMEM((d, ff), BF16),
            pltpu.VMEM((d, ff), BF16),
            pltpu.VMEM((ff, d), BF16),
            pltpu.SemaphoreType.DMA((2,)),
            pltpu.SemaphoreType.DMA((2,)),
            pltpu.SemaphoreType.DMA(()),
        ],
    )
    return pl.pallas_call(
        functools.partial(_ffn_kernel, dump_base=rows // CHUNK),
        grid_spec=grid_spec,
        out_shape=jax.ShapeDtypeStruct((rows + DUMP_CHUNKS * CHUNK, d), BF16),
        compiler_params=pltpu.CompilerParams(vmem_limit_bytes=48 * 1024 * 1024),
        name="expert_ffn",
    )(tile_expert, src_rows, dst_rows, n_tiles, zero_rows, n_zero, hs, wg, wu, wd)


def _combine_body(ys_ref, meta_ref, x_ref, g2_ref):
    tm = TM
    eye = (lax.broadcasted_iota(I32, (tm, tm), 0) == lax.broadcasted_iota(I32, (tm, tm), 1))
    slot = lax.broadcasted_iota(I32, (tm, SLOTS), 1).astype(F32)

    def as_col(row):
        return jnp.sum(jnp.where(eye, row, 0.0), axis=1, keepdims=True)

    ys = []
    for blk in range(x_ref.shape[1] // tm):
        meta = meta_ref[blk]
        pos1, pos2 = as_col(meta[0:1]), as_col(meta[1:2])
        w1, w2 = as_col(meta[2:3]), as_col(meta[3:4])
        gate = jnp.where(slot == pos1, w1, 0.0) + jnp.where(slot == pos2, w2, 0.0)
        ys.append(_dot(gate.astype(BF16), ys_ref[blk * SLOTS:(blk + 1) * SLOTS, :]))
    return x_ref[0] + _batch_row(g2_ref) * jnp.concatenate(ys, axis=0)


def _combine_final_kernel(ys_ref, meta_ref, x_ref, g2_ref, gf_ref, xo_ref):
    xo = _combine_body(ys_ref, meta_ref, x_ref, g2_ref)
    xo_ref[0] = xo * lax.rsqrt(jnp.mean(xo * xo, axis=-1, keepdims=True) + NORM_EPS) * gf_ref[...]


def _combine_inproj_kernel(ys_ref, meta_ref, x_ref, g2_ref, *refs):
    n_in = 10
    xo_ref = refs[n_in]
    xo = _combine_body(ys_ref, meta_ref, x_ref, g2_ref)
    xo_ref[0] = xo
    _inproj_body(xo, *refs[:n_in], *refs[n_in + 1:])


def _combine_specs(d, steps, mod, layer):
    nblk = ROWS // TM
    flat = lambda bi, i: bi * steps + i
    return [
        pl.BlockSpec((nblk * SLOTS, d), lambda bi, i: (flat(bi, i), 0)),
        pl.BlockSpec((nblk, 8, TM), lambda bi, i: (flat(bi, i), 0, 0)),
        pl.BlockSpec((1, ROWS, d), lambda bi, i: (bi, i, 0)),
        _mod_spec(mod, layer, 5),
    ]


def _combine_final_call(ys, meta, x, mod, layer, gf):
    b, s, d = x.shape
    return pl.pallas_call(
        _combine_final_kernel,
        grid=(b, s // ROWS),
        in_specs=_combine_specs(d, s // ROWS, mod, layer) + [pl.BlockSpec((1, d), lambda bi, i: (0, 0))],
        out_specs=pl.BlockSpec((1, ROWS, d), lambda bi, i: (bi, i, 0)),
        out_shape=jax.ShapeDtypeStruct((b, s, d), F32),
        compiler_params=pltpu.CompilerParams(vmem_limit_bytes=48 * 1024 * 1024),
        name="moe_combine",
    )(ys, meta, x, mod, gf)


def _combine_inproj_call(ys, meta, x, mod, g, w_in, layer, tabs, p4, p16):
    b, s, d = x.shape
    in_specs, out_specs, out_shape = _inproj_specs(b, s, d, layer, mod)
    xspec = pl.BlockSpec((1, ROWS, d), lambda bi, i: (bi, i, 0))
    return pl.pallas_call(
        _combine_inproj_kernel,
        grid=(b, s // ROWS),
        in_specs=_combine_specs(d, s // ROWS, mod, layer - 1) + in_specs,
        out_specs=[xspec] + out_specs,
        out_shape=[jax.ShapeDtypeStruct((b, s, d), F32)] + out_shape,
        scratch_shapes=[pltpu.VMEM((d, IN_WIDTH), BF16)],
        compiler_params=pltpu.CompilerParams(vmem_limit_bytes=56 * 1024 * 1024),
        name="combine_in_proj",
    )(ys, meta, x, mod, mod, mod, g, w_in, *tabs, p4, p16)


def _rope_tables(positions):
    pos = positions.astype(F32)[..., None]

    def table(dim):
        inv = ROPE_THETA ** (-jnp.arange(0, dim, 2, dtype=F32) / dim)
        ang = pos * inv
        cos, sin = jnp.cos(ang), jnp.sin(ang)
        reps = LANES // dim
        return (jnp.tile(jnp.concatenate([cos, cos], -1), (1, 1, reps)),
                jnp.tile(jnp.concatenate([-sin, sin], -1), (1, 1, reps)))

    c64, s64 = table(HEAD_DIM)
    c32, s32 = table(DIFF_QK_DIM)
    return c64, s64, c32, s32


def kernel(x, c, positions, ada_w, ada_b, norm_mix_g, norm_ffn_g, w_in, w_out, diff_lambda_q1, diff_lambda_k1,
           diff_lambda_q2, diff_lambda_k2, diff_subln_g, swa_sinks, router_group_w, router_group_b,
           router_expert_w, router_expert_b, expert_w_gate, expert_w_up, expert_w_down, final_norm_g):
    b, s, d = x.shape
    depth = ada_w.shape[0]
    assert b <= BF16_ROWS and s % TQ == 0 and s % ROWS == 0 and s % (16 * BAND) == 0 and d == 8 * LANES
    n = b * s
    nt = n // TM
    max_tiles = (nt * CHUNKS_PER_TILE + N_EXPERTS * (FFN_CHUNKS - 1)) // FFN_CHUNKS + 1

    tabs = _rope_tables(positions)
    p4 = _residue_perm(TM, 4)
    p16 = _residue_perm(TM, 16)
    p4_b, p16_b = jnp.asarray(p4, BF16), jnp.asarray(p16, BF16)
    p4t_b, p16t_b = jnp.asarray(p4.T, BF16), jnp.asarray(p16.T, BF16)
    tri = jnp.asarray(np.triu(np.ones((TM, TM), np.float32), 1), BF16)

    c_pad = jnp.pad(c, ((0, BF16_ROWS - b), (0, 0)))
    mod = _ada_call(c_pad, ada_w, ada_b)

    sink_order = np.asarray(_SWA_HEAD_ORDER, np.int32)
    zpad = lambda k: jnp.zeros((depth, d, k), F32)
    wr = jnp.concatenate([router_group_w, zpad(8 - N_GROUPS), router_expert_w,
                          zpad(ROUTER_COLS - 8 - N_EXPERTS)], axis=-1)
    wr_hi = wr.astype(BF16)
    wr = jnp.concatenate([wr_hi, (wr - wr_hi.astype(F32)).astype(BF16)], axis=-1)
    wr = jnp.swapaxes(wr, 1, 2)
    br = jnp.concatenate([router_group_b, jnp.full((depth, 8 - N_GROUPS), NEG_INF, F32), router_expert_b,
                          jnp.zeros((depth, ROUTER_COLS - 8 - N_EXPERTS), F32)], axis=-1).reshape(depth, -1, 1)

    lam_init = [0.8 - 0.6 * math.exp(-0.3 * l) for l in range(depth)]
    lam = (jnp.exp(jnp.sum(diff_lambda_q1 * diff_lambda_k1, axis=-1))
           - jnp.exp(jnp.sum(diff_lambda_q2 * diff_lambda_k2, axis=-1)) + jnp.asarray(lam_init, F32))
    g_cols = diff_subln_g[:, :, None]
    sinks = swa_sinks[:, sink_order].reshape(-1)
    g_mix = norm_mix_g.reshape(depth, 1, d)
    g_ffn = norm_ffn_g.reshape(depth, 1, d)

    proj, vat, qkv4, qkv16 = _inproj_call(x, mod, g_mix, w_in, 0, tabs, p4_b, p16_b)
    for l in range(depth):
        oa = _diff_attn_call(proj, vat, lam, g_cols, lam_init[l], l)
        ob = _band_call(proj, COL_QB, COL_KB, COL_VB, nqb=4, nkb=1, max_dist=SWA_WINDOW - 1,
                        sinks=sinks, sink_base=l * SWA_Q_HEADS, name="swa")[0]
        o1, l1 = _band_call(proj, COL_QC, COL_KC, COL_VC, nqb=2, nkb=2, max_dist=BAND, want_lse=True,
                            name="dil1")
        o4, l4 = _band_call(qkv4.reshape(b * 4, s // 4, 768), 0, 256, 512, nqb=2, nkb=2, max_dist=BAND,
                            want_lse=True, name="dil4")
        o16, l16 = _band_call(qkv16.reshape(b * 16, s // 16, 768), 0, 256, 512, nqb=2, nkb=2, max_dist=BAND,
                              want_lse=True, name="dil16")
        x, hs, meta, nch = _outproj_router_call(
            oa, ob, o1, l1, o4.reshape(b, 4, s // 4, 256), l4.reshape(b, 4, s // 4, 512),
            o16.reshape(b, 16, s // 16, 256), l16.reshape(b, 16, s // 16, 512), p4t_b, p16t_b, w_out, l, x, mod,
            g_ffn, wr, br, tri)
        sched = _ffn_schedule(nch[:, :, 0], max_tiles, nt * CHUNKS_PER_TILE)
        ys = _ffn_call(*sched, hs, expert_w_gate, expert_w_up, expert_w_down, l, max_tiles)
        if l + 1 < depth:
            x, proj, vat, qkv4, qkv16 = _combine_inproj_call(ys, meta, x, mod, g_mix, w_in, l + 1, tabs,
                                                             p4_b, p16_b)
        else:
            x = _combine_final_call(ys, meta, x, mod, l, final_norm_g.reshape(1, d))
    return x
```

```python
import functools
import math

import numpy as np
import jax
import jax.numpy as jnp
from jax import lax
from jax.experimental import pallas as pl
from jax.experimental.pallas import tpu as pltpu

F32 = jnp.float32
BF16 = jnp.bfloat16
I32 = jnp.int32

HEAD_DIM = 64
ROPE_THETA = 10000.0
NORM_EPS = 1e-6
NEG_INF = -1e30
DIFF_HEADS = 4
DIFF_QK_DIM = 32
SWA_Q_HEADS = 8
SWA_KV_HEADS = 2
SWA_WINDOW = 128
DIL_PATTERNS = ((128, 1), (512, 4), (2048, 16))
N_GROUPS = 4
EXPERTS_PER_GROUP = 4
N_EXPERTS = 16
EXPERT_FF = 512
N_ADA = 6
IN_WIDTH = 2304

LANES = 128
BF16_ROWS = 16
BAND = 128

TM = 256
ROWS = 2 * TM
CHUNK = BF16_ROWS
SLOTS = 2 * TM + N_EXPERTS * CHUNK
CHUNKS_PER_TILE = SLOTS // CHUNK
FFN_ROWS = 1024
FFN_CHUNKS = FFN_ROWS // CHUNK
ZERO_CHUNKS = -(-N_EXPERTS * FFN_CHUNKS // CHUNKS_PER_TILE)
DUMP_CHUNKS = 2 * FFN_CHUNKS
ZBUF_ROWS = FFN_ROWS // 4
DMA_PRIORITIES = 2
TQ = 512
BAND_ROWS = 2048

_SWA_HEAD_ORDER = (0, 4, 1, 5, 2, 6, 3, 7)
COL_QB, COL_KB, COL_VB = 0, 512, 640
COL_Q1, COL_Q2, COL_K1, COL_K2, COL_VA = 768, 896, 1024, 1152, 1280
COL_QC, COL_KC, COL_VC = 1536, 1792, 2048
_SRC_ROPE = (32, 32, 32, 32, 0, 0, 64, 64, 64, 64, 64, 0, 64, 64, 64, 64, 0, 0)
_SRC_DEST = (6, 7, 8, 9, 10, 11, None, None, None, None, 4, 5, 12, 13, 14, 15, 16, 17)
VT_ROWS = HEAD_DIM + BF16_ROWS
_SRC_VA_CHUNK = 2


def _residue_perm(tm, d):
    p = np.zeros((tm, tm), np.float32)
    per = tm // d
    for l in range(per):
        for r in range(d):
            p[r * per + l, l * d + r] = 1.0
    return p


def _dot(a, b, **kw):
    return jnp.dot(a, b, preferred_element_type=F32, **kw)


def _dot_nt(a, b):
    return lax.dot_general(a, b, (((1,), (1,)), ((), ())), preferred_element_type=F32)


def _batch_row(ref):
    return ref[pl.ds(pl.program_id(0), 1), :]


def _modulated_norm(x, g, sc, sh):
    y = x * lax.rsqrt(jnp.mean(x * x, axis=-1, keepdims=True) + NORM_EPS)
    return (y * g) * (1.0 + sc) + sh


def _ada_kernel(c_ref, w_ref, b_ref, o_ref):
    c = c_ref[...]
    ca = c / (1.0 + jnp.exp(-c))
    rows = ca.shape[0]
    c_hi = ca.astype(BF16)
    c_lo = (ca - c_hi.astype(F32)).astype(BF16)
    w = w_ref[0]
    w_hi = w.astype(BF16)
    w_lo = (w - w_hi.astype(F32)).astype(BF16)
    main = _dot(jnp.concatenate([c_hi, c_lo], axis=0), w_hi)
    o_ref[0] = main[:rows] + main[rows:] + _dot(c_hi, w_lo) + b_ref[0]


def _ada_call(c_pad, ada_w, ada_b):
    depth, d, n = ada_w.shape
    tn = n // 2
    return pl.pallas_call(
        _ada_kernel,
        grid=(depth, n // tn),
        in_specs=[
            pl.BlockSpec((c_pad.shape[0], d), lambda l, j: (0, 0)),
            pl.BlockSpec((1, d, tn), lambda l, j: (l, 0, j)),
            pl.BlockSpec((1, 1, tn), lambda l, j: (l, 0, j)),
        ],
        out_specs=pl.BlockSpec((1, c_pad.shape[0], tn), lambda l, j: (l, 0, j)),
        out_shape=jax.ShapeDtypeStruct((depth, c_pad.shape[0], n), F32),
        compiler_params=pltpu.CompilerParams(vmem_limit_bytes=40 * 1024 * 1024),
        name="ada_mod",
    )(c_pad, ada_w, ada_b.reshape(depth, 1, n))


def _rope(t, cos, sin_signed, first_half, half):
    rot = jnp.where(first_half, pltpu.roll(t, LANES - half, 1), pltpu.roll(t, half, 1))
    return t * cos + rot * sin_signed


def _inproj_body(x, sc_ref, sh_ref, g_ref, w_ref, cs64_ref, sn64_ref, cs32_ref, sn32_ref,
                 p4_ref, p16_ref, proj_ref, vat_ref, c4_ref, c16_ref, wb):
    @pl.when((pl.program_id(0) == 0) & (pl.program_id(1) == 0))
    def _():
        wb[...] = w_ref[...].astype(BF16)

    h = _modulated_norm(x, g_ref[...], _batch_row(sc_ref), _batch_row(sh_ref))
    hb = h.astype(BF16)
    lane = lax.broadcasted_iota(I32, (1, LANES), 1)
    first64 = (lane % 64) < 32
    first32 = (lane % 32) < 16
    lo_half = lane < 64
    swa_q = []
    for cb in range(IN_WIDTH // 256):
        if cb == IN_WIDTH // 256 - 1:
            half = hb.shape[0] // 2
            acc = jnp.concatenate([_dot(hb[:half], wb[:, cb * 256:(cb + 1) * 256]),
                                   _dot(hb[half:], wb[:, cb * 256:(cb + 1) * 256])], axis=0)
        else:
            acc = _dot(hb, wb[:, cb * 256:(cb + 1) * 256])
        if cb == _SRC_VA_CHUNK:
            acc_t = acc.T.astype(BF16)
            for hd in range(DIFF_HEADS):
                vat_ref[0, hd * VT_ROWS:hd * VT_ROWS + HEAD_DIM, :] = acc_t[hd * HEAD_DIM:(hd + 1) * HEAD_DIM]
                vat_ref[0, hd * VT_ROWS + HEAD_DIM:(hd + 1) * VT_ROWS, :] = jnp.ones(
                    (BF16_ROWS, acc_t.shape[1]), BF16)
        for half in range(2):
            src = cb * 2 + half
            t = acc[:, half * LANES:(half + 1) * LANES]
            if _SRC_ROPE[src] == 64:
                t = _rope(t, cs64_ref[0], sn64_ref[0], first64, 32)
            elif _SRC_ROPE[src] == 32:
                t = _rope(t, cs32_ref[0], sn32_ref[0], first32, 16)
            dst = _SRC_DEST[src]
            if dst is None:
                swa_q.append(t)
            else:
                proj_ref[0, :, dst * LANES:(dst + 1) * LANES] = t.astype(BF16)
    for jb in range(SWA_Q_HEADS // 2):
        a, c = swa_q[jb // 2], swa_q[2 + jb // 2]
        if jb % 2 == 0:
            blk = jnp.where(lo_half, a, pltpu.roll(c, 64, 1))
        else:
            blk = jnp.where(lo_half, pltpu.roll(a, 64, 1), c)
        proj_ref[0, :, jb * LANES:(jb + 1) * LANES] = blk.astype(BF16)
    tm = TM
    for blk in range(x.shape[0] // tm):
        cc = proj_ref[0, blk * tm:(blk + 1) * tm, COL_QC:]
        c4 = _dot(p4_ref[...], cc).astype(BF16)
        for r in range(4):
            c4_ref[0, r, blk * (tm // 4):(blk + 1) * (tm // 4)] = c4[r * (tm // 4):(r + 1) * (tm // 4)]
        c16 = _dot(p16_ref[...], cc).astype(BF16)
        for r in range(16):
            c16_ref[0, r, blk * (tm // 16):(blk + 1) * (tm // 16)] = c16[r * (tm // 16):(r + 1) * (tm // 16)]


def _inproj_kernel(x_ref, *refs):
    _inproj_body(x_ref[0], *refs)


def _mod_spec(mod, layer, k):
    return pl.BlockSpec((None, mod.shape[1], mod.shape[2] // N_ADA), lambda bi, i: (layer, 0, k))


def _const_spec(shape, index_map):
    return pl.BlockSpec(shape, index_map, pipeline_mode=pl.Buffered(1))


def _inproj_specs(b, s, d, layer, mod):
    rows = ROWS
    row = lambda bi, i: (bi, i, 0)
    const2 = lambda bi, i: (0, 0)
    in_specs = [
        _mod_spec(mod, layer, 1),
        _mod_spec(mod, layer, 0),
        pl.BlockSpec((None, 1, d), lambda bi, i: (layer, 0, 0)),
        _const_spec((None, d, IN_WIDTH), lambda bi, i: (layer, 0, 0)),
        pl.BlockSpec((1, rows, LANES), row),
        pl.BlockSpec((1, rows, LANES), row),
        pl.BlockSpec((1, rows, LANES), row),
        pl.BlockSpec((1, rows, LANES), row),
        _const_spec((TM, TM), const2),
        _const_spec((TM, TM), const2),
    ]
    out_specs = [
        pl.BlockSpec((1, rows, IN_WIDTH), row),
        pl.BlockSpec((1, DIFF_HEADS * VT_ROWS, rows), lambda bi, i: (bi, 0, i)),
        pl.BlockSpec((1, 4, rows // 4, 768), lambda bi, i: (bi, 0, i, 0)),
        pl.BlockSpec((1, 16, rows // 16, 768), lambda bi, i: (bi, 0, i, 0)),
    ]
    out_shape = [
        jax.ShapeDtypeStruct((b, s, IN_WIDTH), BF16),
        jax.ShapeDtypeStruct((b, DIFF_HEADS * VT_ROWS, s), BF16),
        jax.ShapeDtypeStruct((b, 4, s // 4, 768), BF16),
        jax.ShapeDtypeStruct((b, 16, s // 16, 768), BF16),
    ]
    return in_specs, out_specs, out_shape


def _inproj_call(x, mod, g, w_in, layer, tabs, p4, p16):
    b, s, d = x.shape
    in_specs, out_specs, out_shape = _inproj_specs(b, s, d, layer, mod)
    return pl.pallas_call(
        _inproj_kernel,
        grid=(b, s // ROWS),
        in_specs=[pl.BlockSpec((1, ROWS, d), lambda bi, i: (bi, i, 0))] + in_specs,
        out_specs=out_specs,
        out_shape=out_shape,
        scratch_shapes=[pltpu.VMEM((d, IN_WIDTH), BF16)],
        compiler_params=pltpu.CompilerParams(vmem_limit_bytes=56 * 1024 * 1024),
        name="in_proj",
    )(x, mod, mod, g, w_in, *tabs, p4, p16)


def _diff_attn_kernel(lam_ref, q1_ref, q2_ref, q1n_ref, q2n_ref, k1_ref, k2_ref, vt_ref, g_ref, o_ref,
                      m_sc, acc_sc, s_sc, *, lambda_init, layer):
    tq = q1_ref.shape[1]
    qi = pl.program_id(1)
    lam = lam_ref[layer]
    to_log2 = DIFF_QK_DIM ** -0.5 * math.log2(math.e)
    lane = lax.broadcasted_iota(I32, (1, LANES), 1)

    def head_queries(qa_ref, qb_ref):
        qa = qa_ref[0].astype(F32) * to_log2
        qb = qb_ref[0].astype(F32) * to_log2
        out = []
        for h in range(DIFF_HEADS):
            hm = (lane // DIFF_QK_DIM) == h
            out.append((jnp.where(hm, qa, 0.0).astype(BF16), jnp.where(hm, qb, 0.0).astype(BF16)))
        return out

    qh = head_queries(q1_ref, q2_ref)
    qh_next = head_queries(q1n_ref, q2n_ref)
    causal = (lax.broadcasted_iota(I32, (tq, tq), 0) <= lax.broadcasted_iota(I32, (tq, tq), 1))

    m_sc[...] = jnp.full(m_sc.shape, NEG_INF, F32)
    acc_sc[...] = jnp.zeros(acc_sc.shape, F32)

    n_chain = 2 * DIFF_HEADS

    def scores(ch, tile, queries):
        start = pl.multiple_of(tile * tq, tq)
        k_ref = k1_ref if ch % 2 == 0 else k2_ref
        return _dot_nt(k_ref[0, pl.ds(start, tq), :], queries[ch // 2][ch % 2])

    @pl.when(qi == 0)
    def _():
        for ch in range(n_chain):
            s_sc[ch] = scores(ch, 0, qh)

    def step(j, last):
        start = pl.multiple_of(j * tq, tq)
        for ch in range(n_chain):
            st = s_sc[ch]
            s_sc[ch] = scores(ch, 0, qh_next) if last else scores(ch, j + 1, qh)
            h = ch // 2
            vt = vt_ref[0, h * VT_ROWS:(h + 1) * VT_ROWS, pl.ds(start, tq)]
            if last:
                st = jnp.where(causal, st, NEG_INF)
            m_old = m_sc[ch]
            m_new = jnp.maximum(m_old, jnp.max(st, axis=0, keepdims=True))
            p = jnp.exp2(st - m_new)
            al = jnp.exp2(m_old - m_new)
            acc_sc[ch] = al * acc_sc[ch] + _dot(vt, p.astype(BF16))
            m_sc[ch] = m_new

    def body(j, carry):
        step(j, False)
        return carry

    lax.fori_loop(0, qi, body, 0)
    step(qi, True)

    g = g_ref[...]
    outs = []
    for h in range(DIFF_HEADS):
        a1, a2 = acc_sc[2 * h], acc_sc[2 * h + 1]
        o = (a1[:HEAD_DIM] / a1[HEAD_DIM:HEAD_DIM + 1]
             - lam * (a2[:HEAD_DIM] / a2[HEAD_DIM:HEAD_DIM + 1]))
        ms = jnp.mean(o * o, axis=0, keepdims=True)
        outs.append((o * lax.rsqrt(ms + NORM_EPS)) * g * (1.0 - lambda_init))
    o_ref[0] = jnp.concatenate(outs, axis=0).T.astype(BF16)


def _diff_attn_call(proj, vat, lam, g_cols, lambda_init, layer):
    b, s, _ = proj.shape
    tq = TQ
    qspec = lambda cb: pl.BlockSpec((1, tq, LANES), lambda bi, i, cb=cb: (bi, i, cb))
    last_q = s // tq - 1
    qnext = lambda cb: pl.BlockSpec((1, tq, LANES), lambda bi, i, cb=cb: (bi, jnp.minimum(i + 1, last_q), cb))
    kspec = lambda cb: pl.BlockSpec((1, s, LANES), lambda bi, i, cb=cb: (bi, 0, cb))
    n_chain = 2 * DIFF_HEADS
    return pl.pallas_call(
        functools.partial(_diff_attn_kernel, lambda_init=lambda_init, layer=layer),
        grid=(b, s // tq),
        in_specs=[
            pl.BlockSpec(memory_space=pltpu.SMEM),
            qspec(COL_Q1 // LANES), qspec(COL_Q2 // LANES),
            qnext(COL_Q1 // LANES), qnext(COL_Q2 // LANES),
            kspec(COL_K1 // LANES), kspec(COL_K2 // LANES),
            pl.BlockSpec((1, DIFF_HEADS * VT_ROWS, s), lambda bi, i: (bi, 0, 0)),
            pl.BlockSpec((None, HEAD_DIM, 1), lambda bi, i: (layer, 0, 0)),
        ],
        out_specs=pl.BlockSpec((1, tq, 256), lambda bi, i: (bi, i, 0)),
        out_shape=jax.ShapeDtypeStruct((b, s, 256), BF16),
        scratch_shapes=[
            pltpu.VMEM((n_chain, 1, tq), F32),
            pltpu.VMEM((n_chain, VT_ROWS, tq), F32),
            pltpu.VMEM((n_chain, tq, tq), F32),
        ],
        compiler_params=pltpu.CompilerParams(vmem_limit_bytes=48 * 1024 * 1024),
        name="diff_attn",
    )(lam, proj, proj, proj, proj, proj, proj, vat, g_cols)


def _band_kernel(*refs, nqb, nkb, max_dist, has_sink, want_lse, sink_base):
    it = iter(refs)
    sink_ref = next(it) if has_sink else None
    q_ref, kp_ref, kc_ref, vp_ref, vc_ref = (next(it) for _ in range(5))
    o_ref = next(it)
    lse_ref = next(it) if want_lse else None
    kbuf, vbuf = next(it), next(it)
    nseq, rows = q_ref.shape[0], q_ref.shape[1]
    i = pl.program_id(1)
    for sq in range(nseq):
        kbuf[sq, 0:BAND, :] = kp_ref[sq]
        kbuf[sq, BAND:, :] = kc_ref[sq]
        for kb in range(nkb):
            vbuf[sq, 0:BAND, kb * 256:kb * 256 + LANES] = vp_ref[sq, :, kb * LANES:(kb + 1) * LANES]
            vbuf[sq, BAND:, kb * 256:kb * 256 + LANES] = vc_ref[sq, :, kb * LANES:(kb + 1) * LANES]
            vbuf[sq, :, kb * 256 + LANES:(kb + 1) * 256] = jnp.ones((BAND + rows, LANES), BF16)
    lane = lax.broadcasted_iota(I32, (1, LANES), 1)
    lo_half = lane < 64
    r_io = lax.broadcasted_iota(I32, (BAND, 2 * BAND), 0)
    c_io = lax.broadcasted_iota(I32, (BAND, 2 * BAND), 1)
    dist = BAND + r_io - c_io
    band = (dist >= 0) & (dist <= max_dist)
    band_first = band & ((c_io >= BAND) | (i > 0))
    col0 = lax.broadcasted_iota(I32, (1, 2 * BAND), 1) == 0
    vr = lax.broadcasted_iota(I32, (2 * BAND, 2 * LANES), 0)
    vc = lax.broadcasted_iota(I32, (2 * BAND, 2 * LANES), 1)
    sink_row = (vr == 0) & (vc < LANES)
    to_log2 = HEAD_DIM ** -0.5 * math.log2(math.e)
    units = [(sq, sb, qb) for sq in range(nseq) for sb in range(rows // BAND) for qb in range(nqb)]

    def scores(u):
        sq, sb, qb = units[u]
        kb = qb if nkb > 1 else 0
        q = q_ref[sq, sb * BAND:(sb + 1) * BAND, qb * LANES:(qb + 1) * LANES].astype(F32) * to_log2
        q2 = jnp.concatenate([jnp.where(lo_half, q, 0.0), jnp.where(lo_half, 0.0, q)], axis=0).astype(BF16)
        return _dot_nt(q2, kbuf[sq, sb * BAND:(sb + 2) * BAND, kb * LANES:(kb + 1) * LANES])

    ahead = 2
    pending = [scores(u) for u in range(min(ahead, len(units)))]
    for u, (sq, sb, qb) in enumerate(units):
        if u + ahead < len(units):
            pending.append(scores(u + ahead))
        s2 = pending[u]
        pending[u] = None
        kb = qb if nkb > 1 else 0
        r0 = sb * BAND
        msk = band_first if sb == 0 else band
        halves = []
        for hh in range(2):
            if has_sink:
                fill = jnp.where(col0, sink_ref[sink_base + qb * 2 + hh] * math.log2(math.e), NEG_INF)
            else:
                fill = NEG_INF
            halves.append(jnp.where(msk, s2[hh * BAND:(hh + 1) * BAND], fill))
        s2 = jnp.concatenate(halves, axis=0)
        m = jnp.max(s2, axis=1, keepdims=True)
        p = jnp.exp2(s2 - m).astype(BF16)
        vw = vbuf[sq, r0:r0 + 2 * BAND, kb * 256:(kb + 1) * 256]
        if has_sink:
            vw = jnp.where(sink_row, jnp.zeros_like(vw), vw)
        pv = _dot(p, vw)
        den = pv[:, LANES:]
        out = pv[:, :LANES] / den
        o = jnp.where(lo_half, out[:BAND], out[BAND:])
        o_ref[sq, r0:r0 + BAND, qb * LANES:(qb + 1) * LANES] = o.astype(BF16)
        if want_lse:
            lse2 = m + jnp.log2(den)
            ls = jnp.where(lo_half, lse2[:BAND], lse2[BAND:])
            hi = ls.astype(BF16)
            lo = (ls - hi.astype(F32)).astype(BF16)
            lse_ref[sq, r0:r0 + BAND, qb * LANES:(qb + 1) * LANES] = hi
            lse_ref[sq, r0:r0 + BAND, (nqb + qb) * LANES:(nqb + qb + 1) * LANES] = lo


def _band_call(arr, q_col, k_col, v_col, nqb, nkb, max_dist, sinks=None, sink_base=0, want_lse=False,
               name="band"):
    ns, length, _ = arr.shape
    assert sinks is None or max_dist < BAND
    rows = min(BAND_ROWS, length)
    assert length % rows == 0 and BAND_ROWS % rows == 0 and ns % (BAND_ROWS // rows) == 0
    nseq = BAND_ROWS // rows
    wq, wk = nqb * LANES, nkb * LANES
    rpb = rows // BAND
    cur = lambda col, w: pl.BlockSpec((nseq, rows, w), lambda n, i, c=col // w: (n, i, c))
    prev = lambda col, w: pl.BlockSpec(
        (nseq, BAND, w), lambda n, i, c=col // w: (n, jnp.maximum(i * rpb - 1, 0), c))
    in_specs = [cur(q_col, wq), prev(k_col, wk), cur(k_col, wk), prev(v_col, wk), cur(v_col, wk)]
    args = [arr] * 5
    if sinks is not None:
        in_specs = [pl.BlockSpec(memory_space=pltpu.SMEM)] + in_specs
        args = [sinks] + args
    out_specs = [pl.BlockSpec((nseq, rows, wq), lambda n, i: (n, i, 0))]
    out_shape = [jax.ShapeDtypeStruct((ns, length, wq), BF16)]
    if want_lse:
        out_specs.append(pl.BlockSpec((nseq, rows, 2 * wq), lambda n, i: (n, i, 0)))
        out_shape.append(jax.ShapeDtypeStruct((ns, length, 2 * wq), BF16))
    return pl.pallas_call(
        functools.partial(_band_kernel, nqb=nqb, nkb=nkb, max_dist=max_dist,
                          has_sink=sinks is not None, want_lse=want_lse, sink_base=sink_base),
        grid=(ns // nseq, length // rows),
        in_specs=in_specs,
        out_specs=out_specs,
        out_shape=out_shape,
        scratch_shapes=[pltpu.VMEM((nseq, BAND + rows, wk), BF16), pltpu.VMEM((nseq, BAND + rows, 2 * wk), BF16)],
        name=name,
    )(*args)


def _outproj_router_kernel(oa_ref, ob_ref, o1_ref, l1_ref, o4_ref, l4_ref, o16_ref, l16_ref, p4t_ref, p16t_ref,
                           w_ref, x_ref, g1_ref, sc_ref, sh_ref, g_ref, wr_ref, br_ref, tri_ref,
                           xo_ref, hs_ref, meta_ref, nch_ref, wb):
    @pl.when((pl.program_id(0) == 0) & (pl.program_id(1) == 0))
    def _():
        wb[0:256, :] = w_ref[0:256, :].astype(BF16)
        for pos, head in enumerate(_SWA_HEAD_ORDER):
            wb[256 + pos * HEAD_DIM:256 + (pos + 1) * HEAD_DIM, :] = (
                w_ref[256 + head * HEAD_DIM:256 + (head + 1) * HEAD_DIM, :].astype(BF16))
        wb[768:1024, :] = w_ref[768:1024, :].astype(BF16)

    hw = o1_ref.shape[2]
    nblk = x_ref.shape[1] // TM

    def lse_of(v):
        return v[:, :hw] + v[:, hw:]

    def unpermute(pt_ref, src_ref, width):
        per = src_ref.shape[2] // nblk
        return jnp.concatenate(
            [_dot(pt_ref[...], src_ref[0, :, k * per:(k + 1) * per, :].reshape(TM, width)) for k in range(nblk)],
            axis=0)

    o1 = o1_ref[0].astype(F32)
    ls1 = lse_of(l1_ref[0].astype(F32))
    o4 = unpermute(p4t_ref, o4_ref, hw)
    ls4 = lse_of(unpermute(p4t_ref, l4_ref, 2 * hw))
    o16 = unpermute(p16t_ref, o16_ref, hw)
    ls16 = lse_of(unpermute(p16t_ref, l16_ref, 2 * hw))
    mx = jnp.maximum(jnp.maximum(ls1, ls4), ls16)
    e1, e4, e16 = jnp.exp2(ls1 - mx), jnp.exp2(ls4 - mx), jnp.exp2(ls16 - mx)
    oc = (e1 * o1 + e4 * o4 + e16 * o16) / (e1 + e4 + e16)
    mix = (_dot(oa_ref[0], wb[0:256, :]) + _dot(ob_ref[0], wb[256:768, :])
           + _dot(oc.astype(BF16), wb[768:1024, :]))
    x1 = x_ref[0] + _batch_row(g1_ref) * mix
    xo_ref[0] = x1
    _router_body(x1, sc_ref, sh_ref, g_ref, wr_ref, br_ref, tri_ref, hs_ref, meta_ref, nch_ref)


def _outproj_router_call(oa, ob, o1, l1, o4, l4, o16, l16, p4t, p16t, w_out, layer, x, mod, g, wr, br, tri):
    b, s, d = x.shape
    rows = ROWS
    nblk = rows // TM
    steps = s // rows
    nt = b * s // TM
    row = lambda w_: pl.BlockSpec((1, rows, w_), lambda bi, i: (bi, i, 0))
    res = lambda dd, w_: pl.BlockSpec((1, dd, rows // dd, w_), lambda bi, i: (bi, 0, i, 0))
    const2 = lambda bi, i: (0, 0)
    flat = lambda bi, i: bi * steps + i
    return pl.pallas_call(
        _outproj_router_kernel,
        grid=(b, steps),
        in_specs=[
            row(256), row(512), row(256), row(512),
            res(4, 256), res(4, 512), res(16, 256), res(16, 512),
            _const_spec((TM, TM), const2), _const_spec((TM, TM), const2),
            _const_spec((None, d, d), lambda bi, i: (layer, 0, 0)),
            row(d),
            _mod_spec(mod, layer, 2),
            _mod_spec(mod, layer, 4),
            _mod_spec(mod, layer, 3),
            pl.BlockSpec((None, 1, d), lambda bi, i: (layer, 0, 0)),
            _const_spec((None, 2 * ROUTER_COLS, d), lambda bi, i: (layer, 0, 0)),
            pl.BlockSpec((None, ROUTER_COLS, 1), lambda bi, i: (layer, 0, 0)),
            _const_spec((TM, TM), const2),
        ],
        out_specs=[
            row(d),
            pl.BlockSpec((nblk * SLOTS, d), lambda bi, i: (flat(bi, i), 0)),
            pl.BlockSpec((nblk, 8, TM), lambda bi, i: (flat(bi, i), 0, 0)),
            pl.BlockSpec((nblk, N_EXPERTS, LANES), lambda bi, i: (flat(bi, i), 0, 0)),
        ],
        out_shape=[
            jax.ShapeDtypeStruct((b, s, d), F32),
            jax.ShapeDtypeStruct((nt * SLOTS, d), BF16),
            jax.ShapeDtypeStruct((nt, 8, TM), F32),
            jax.ShapeDtypeStruct((nt, N_EXPERTS, LANES), I32),
        ],
        scratch_shapes=[pltpu.VMEM((d, d), BF16)],
        compiler_params=pltpu.CompilerParams(vmem_limit_bytes=56 * 1024 * 1024),
        name="out_proj_router",
    )(oa, ob, o1, l1, o4, l4, o16, l16, p4t, p16t, w_out, x, mod, mod, mod, g, wr, br, tri)


ROUTER_COLS = LANES


def _router_body(x, sc_ref, sh_ref, g_ref, wr_ref, br_ref, tri_ref, hs_ref, meta_ref, nch_ref):
    tm = TM
    h = _modulated_norm(x, g_ref[...], _batch_row(sc_ref), _batch_row(sh_ref))
    hb = h.astype(BF16)
    h_lo = (h - hb.astype(F32)).astype(BF16)
    part = _dot_nt(wr_ref[...], hb) + _dot_nt(wr_ref[...], h_lo)
    logits_t = part[:ROUTER_COLS] + part[ROUTER_COLS:] + br_ref[...]
    r8 = lax.broadcasted_iota(I32, (8, tm), 0)
    r16 = lax.broadcasted_iota(I32, (N_EXPERTS, tm), 0)
    rl = lax.broadcasted_iota(I32, (N_EXPERTS, LANES), 0)
    slot = lax.broadcasted_iota(I32, (SLOTS, tm), 0)
    for blk in range(x.shape[0] // tm):
        lt = logits_t[:, blk * tm:(blk + 1) * tm]
        glog = lt[0:8]
        elog = lt[8:8 + N_EXPERTS]

        gmax = jnp.max(glog, axis=0, keepdims=True)
        g_w = 1.0 / jnp.sum(jnp.exp(glog - gmax), axis=0, keepdims=True)
        g_idx = jnp.min(jnp.where(glog == gmax, r8, 99), axis=0, keepdims=True)

        el = jnp.where((r16 // EXPERTS_PER_GROUP) == g_idx, elog, NEG_INF)
        emax = jnp.max(el, axis=0, keepdims=True)
        e1 = jnp.min(jnp.where(el == emax, r16, 99), axis=0, keepdims=True)
        el2 = jnp.where(r16 == e1, NEG_INF, el)
        emax2 = jnp.max(el2, axis=0, keepdims=True)
        e2 = jnp.min(jnp.where(el2 == emax2, r16, 99), axis=0, keepdims=True)
        p2 = jnp.exp(emax2 - emax)
        wt1 = g_w / (1.0 + p2)
        wt2 = g_w * p2 / (1.0 + p2)

        oh1 = r16 == e1
        oh2 = r16 == e2
        onehot = jnp.where(oh1, 1.0, 0.0) + jnp.where(oh2, 1.0, 0.0)
        cnt = jnp.sum(onehot, axis=1, keepdims=True)
        nch = jnp.floor((cnt + (CHUNK - 1)) * (1.0 / CHUNK))
        nchb = jnp.broadcast_to(nch, (N_EXPERTS, LANES))
        incl = nchb
        for sft in (1, 2, 4, 8):
            incl = incl + jnp.where(rl >= sft, pltpu.roll(incl, sft, 0), 0.0)
        off = (incl - nchb)[:, 0:1] * float(CHUNK)
        rank = _dot(onehot.astype(BF16), tri_ref[...])
        slot_of = off + rank
        pos1 = jnp.sum(jnp.where(oh1, slot_of, 0.0), axis=0, keepdims=True)
        pos2 = jnp.sum(jnp.where(oh2, slot_of, 0.0), axis=0, keepdims=True)

        sel = jnp.where(slot == pos1.astype(I32), 1.0, jnp.where(slot == pos2.astype(I32), 1.0, 0.0))
        hs_ref[blk * SLOTS:(blk + 1) * SLOTS, :] = _dot(sel.astype(BF16), hb[blk * tm:(blk + 1) * tm]).astype(BF16)

        meta_ref[blk] = jnp.concatenate([pos1, pos2, wt1, wt2, jnp.zeros((4, tm), F32)], axis=0)
        nch_ref[blk] = nchb.astype(I32)


def _ffn_schedule(nch, max_tiles, dump_base):
    nt = nch.shape[0]
    cend = jnp.cumsum(nch, axis=1)
    coff = cend - nch
    tcum = jnp.cumsum(nch, axis=0)
    before = tcum - nch
    tot = tcum[-1]
    pad = ((tot + FFN_CHUNKS - 1) // FFN_CHUNKS) * FFN_CHUNKS
    eend = jnp.cumsum(pad)
    estart = eend - pad
    n_tiles = (eend[-1] // FFN_CHUNKS).astype(I32)
    first_chunk = jnp.arange(max_tiles, dtype=I32) * FFN_CHUNKS
    tile_expert = jnp.sum((eend[None, :] <= first_chunk[:, None]).astype(I32), axis=1)
    tile_expert = jnp.minimum(tile_expert, N_EXPERTS - 1)
    hp = lax.Precision.HIGHEST
    pos = jnp.arange(max_tiles * FFN_CHUNKS, dtype=I32)
    e_s = jnp.minimum(jnp.sum((eend[None, :] <= (pos // FFN_CHUNKS * FFN_CHUNKS)[:, None]).astype(I32), axis=1),
                      N_EXPERTS - 1)
    oh_e = (e_s[:, None] == jnp.arange(N_EXPERTS, dtype=I32)[None, :]).astype(F32)
    idx = pos - jnp.dot(oh_e, estart.astype(F32), precision=hp).astype(I32)
    run_end = jnp.dot(oh_e, tcum.T.astype(F32), precision=hp).astype(I32)
    run_beg = jnp.dot(oh_e, before.T.astype(F32), precision=hp).astype(I32)
    run_off = jnp.dot(oh_e, coff.T.astype(F32), precision=hp).astype(I32)
    in_run = (idx[:, None] >= run_beg) & (idx[:, None] < run_end)
    tile_base = jnp.arange(nt, dtype=I32)[None, :] * CHUNKS_PER_TILE
    src = jnp.sum(jnp.where(in_run, tile_base + run_off + idx[:, None] - run_beg, 0), axis=1)
    real = jnp.any(in_run, axis=1)
    dump = dump_base + (pos // FFN_CHUNKS % 2) * FFN_CHUNKS + pos % FFN_CHUNKS
    src_rows = jnp.where(real, src, 0) * CHUNK
    dst_rows = jnp.where(real, src, dump) * CHUNK
    used = cend[:, -1]
    ucum = jnp.cumsum(CHUNKS_PER_TILE - used)
    ubeg = ucum - (CHUNKS_PER_TILE - used)
    z = jnp.arange(max_tiles * ZERO_CHUNKS, dtype=I32)[:, None]
    in_gap = (z >= ubeg[None, :]) & (z < ucum[None, :])
    zero_rows = jnp.sum(jnp.where(in_gap, tile_base + used[None, :] + z - ubeg[None, :], 0), axis=1) * CHUNK
    n_zero = ucum[-1].astype(I32).reshape(1)
    return tile_expert, src_rows, dst_rows, n_tiles.reshape(1), zero_rows, n_zero


def _ffn_kernel(te_ref, sr_ref, dr_ref, nt_ref, zr_ref, nz_ref, hs_hbm, wg_ref, wu_ref, wd_ref, ys_hbm,
                xbuf, ybuf, zbuf, wgb, wub, wdb, in_sem, out_sem, zero_sem, *, dump_base):
    j = pl.program_id(0)
    nt = nt_ref[0]
    half_ff = EXPERT_FF // 2

    def rows_at(r):
        return pl.ds(pl.multiple_of(r, CHUNK), CHUNK)

    def in_copy(step, slot, k, wait=False):
        r = 0 if wait else sr_ref[step * FFN_CHUNKS + k]
        return pltpu.make_async_copy(hs_hbm.at[rows_at(r), :], xbuf.at[slot, pl.ds(k * CHUNK, CHUNK), :],
                                     in_sem.at[slot])

    def out_copy(step, slot, k, wait=False):
        r = 0 if wait else dr_ref[step * FFN_CHUNKS + k]
        return pltpu.make_async_copy(ybuf.at[slot, pl.ds(k * CHUNK, CHUNK), :], ys_hbm.at[rows_at(r), :],
                                     out_sem.at[slot])

    z_lo = jnp.minimum(j * ZERO_CHUNKS, nz_ref[0])
    z_hi = jnp.minimum((j + 1) * ZERO_CHUNKS, nz_ref[0])

    def zero_start(k, carry):
        pltpu.make_async_copy(zbuf.at[pl.ds(0, CHUNK), :], ys_hbm.at[rows_at(zr_ref[k]), :], zero_sem).start()
        return carry

    def zero_wait(k, carry):
        pltpu.make_async_copy(zbuf.at[pl.ds(0, CHUNK), :], ys_hbm.at[rows_at(0), :], zero_sem).wait()
        return carry

    @pl.when(j == 0)
    def _():
        zbuf[...] = jnp.zeros_like(zbuf)
        fills = [pltpu.make_async_copy(
            zbuf, ys_hbm.at[pl.ds(dump_base * CHUNK + r * ZBUF_ROWS, ZBUF_ROWS), :], zero_sem)
            for r in range(DUMP_CHUNKS * CHUNK // ZBUF_ROWS)]
        for cp in fills:
            cp.start()
        for cp in fills:
            cp.wait()

    lax.fori_loop(z_lo, z_hi, zero_start, 0)

    @pl.when(j < nt)
    def _():
        slot = j % 2

        @pl.when(j == 0)
        def _():
            for k in range(FFN_CHUNKS):
                in_copy(0, 0, k).start(priority=k % DMA_PRIORITIES)

        @pl.when(j + 1 < nt)
        def _():
            for k in range(FFN_CHUNKS):
                in_copy(j + 1, 1 - slot, k).start(priority=k % DMA_PRIORITIES)

        @pl.when((j == 0) | (te_ref[j] != te_ref[jnp.maximum(j - 1, 0)]))
        def _():
            wgb[...] = wg_ref[0].astype(BF16)
            wub[...] = wu_ref[0].astype(BF16)
            wdb[...] = wd_ref[0].astype(BF16)

        for k in range(FFN_CHUNKS):
            in_copy(j, slot, k, wait=True).wait()

        @pl.when(j >= 2)
        def _():
            for k in range(FFN_CHUNKS):
                out_copy(j - 2, slot, k, wait=True).wait()

        x = xbuf[slot]
        hg = [_dot(x, wgb[:, h * half_ff:(h + 1) * half_ff]) for h in range(2)]
        hu = [_dot(x, wub[:, h * half_ff:(h + 1) * half_ff]) for h in range(2)]
        y = None
        for h in range(2):
            act = ((hg[h] / (1.0 + jnp.exp(-hg[h]))) * hu[h]).astype(BF16)
            part = _dot(act, wdb[h * half_ff:(h + 1) * half_ff, :])
            y = part if y is None else y + part
        ybuf[slot] = y.astype(BF16)
        for k in range(FFN_CHUNKS):
            out_copy(j, slot, k).start(priority=k % DMA_PRIORITIES)

        @pl.when(j == nt - 1)
        def _():
            for k in range(FFN_CHUNKS):
                out_copy(j, slot, k, wait=True).wait()

            @pl.when(j >= 1)
            def _():
                for k in range(FFN_CHUNKS):
                    out_copy(j - 1, 1 - slot, k, wait=True).wait()

    lax.fori_loop(z_lo, z_hi, zero_wait, 0)


def _ffn_call(tile_expert, src_rows, dst_rows, n_tiles, zero_rows, n_zero, hs, wg, wu, wd, layer, max_tiles):
    rows, d = hs.shape
    ff = wg.shape[-1]
    wmap = lambda j, te, sr, dr, nt, zr, nz: (layer, te[j], 0, 0)
    grid_spec = pltpu.PrefetchScalarGridSpec(
        num_scalar_prefetch=6,
        grid=(max_tiles,),
        in_specs=[
            pl.BlockSpec(memory_space=pl.ANY),
            pl.BlockSpec((None, 1, d, ff), wmap),
            pl.BlockSpec((None, 1, d, ff), wmap),
            pl.BlockSpec((None, 1, ff, d), wmap),
        ],
        out_specs=pl.BlockSpec(memory_space=pl.ANY),
        scratch_shapes=[
            pltpu.VMEM((2, FFN_ROWS, d), BF16),
            pltpu.VMEM((2, FFN_ROWS, d), BF16),
            pltpu.VMEM((ZBUF_ROWS, d), BF16),
            pltpu.VMEM((d, ff), BF16),
            pltpu.VMEM((d, ff), BF16),
            pltpu.VMEM((ff, d), BF16),
            pltpu.SemaphoreType.DMA((2,)),
            pltpu.SemaphoreType.DMA((2,)),
            pltpu.SemaphoreType.DMA(()),
        ],
    )
    return pl.pallas_call(
        functools.partial(_ffn_kernel, dump_base=rows // CHUNK),
        grid_spec=grid_spec,
        out_shape=jax.ShapeDtypeStruct((rows + DUMP_CHUNKS * CHUNK, d), BF16),
        compiler_params=pltpu.CompilerParams(vmem_limit_bytes=48 * 1024 * 1024),
        name="expert_ffn",
    )(tile_expert, src_rows, dst_rows, n_tiles, zero_rows, n_zero, hs, wg, wu, wd)


def _combine_body(ys_ref, meta_ref, x_ref, g2_ref):
    tm = TM
    eye = (lax.broadcasted_iota(I32, (tm, tm), 0) == lax.broadcasted_iota(I32, (tm, tm), 1))
    slot = lax.broadcasted_iota(I32, (tm, SLOTS), 1).astype(F32)

    def as_col(row):
        return jnp.sum(jnp.where(eye, row, 0.0), axis=1, keepdims=True)

    ys = []
    for blk in range(x_ref.shape[1] // tm):
        meta = meta_ref[blk]
        pos1, pos2 = as_col(meta[0:1]), as_col(meta[1:2])
        w1, w2 = as_col(meta[2:3]), as_col(meta[3:4])
        gate = jnp.where(slot == pos1, w1, 0.0) + jnp.where(slot == pos2, w2, 0.0)
        ys.append(_dot(gate.astype(BF16), ys_ref[blk * SLOTS:(blk + 1) * SLOTS, :]))
    return x_ref[0] + _batch_row(g2_ref) * jnp.concatenate(ys, axis=0)


def _combine_final_kernel(ys_ref, meta_ref, x_ref, g2_ref, gf_ref, xo_ref):
    xo = _combine_body(ys_ref, meta_ref, x_ref, g2_ref)
    xo_ref[0] = xo * lax.rsqrt(jnp.mean(xo * xo, axis=-1, keepdims=True) + NORM_EPS) * gf_ref[...]


def _combine_inproj_kernel(ys_ref, meta_ref, x_ref, g2_ref, *refs):
    n_in = 10
    xo_ref = refs[n_in]
    xo = _combine_body(ys_ref, meta_ref, x_ref, g2_ref)
    xo_ref[0] = xo
    _inproj_body(xo, *refs[:n_in], *refs[n_in + 1:])


def _combine_specs(d, steps, mod, layer):
    nblk = ROWS // TM
    flat = lambda bi, i: bi * steps + i
    return [
        pl.BlockSpec((nblk * SLOTS, d), lambda bi, i: (flat(bi, i), 0)),
        pl.BlockSpec((nblk, 8, TM), lambda bi, i: (flat(bi, i), 0, 0)),
        pl.BlockSpec((1, ROWS, d), lambda bi, i: (bi, i, 0)),
        _mod_spec(mod, layer, 5),
    ]


def _combine_final_call(ys, meta, x, mod, layer, gf):
    b, s, d = x.shape
    return pl.pallas_call(
        _combine_final_kernel,
        grid=(b, s // ROWS),
        in_specs=_combine_specs(d, s // ROWS, mod, layer) + [pl.BlockSpec((1, d), lambda bi, i: (0, 0))],
        out_specs=pl.BlockSpec((1, ROWS, d), lambda bi, i: (bi, i, 0)),
        out_shape=jax.ShapeDtypeStruct((b, s, d), F32),
        compiler_params=pltpu.CompilerParams(vmem_limit_bytes=48 * 1024 * 1024),
        name="moe_combine",
    )(ys, meta, x, mod, gf)


def _combine_inproj_call(ys, meta, x, mod, g, w_in, layer, tabs, p4, p16):
    b, s, d = x.shape
    in_specs, out_specs, out_shape = _inproj_specs(b, s, d, layer, mod)
    xspec = pl.BlockSpec((1, ROWS, d), lambda bi, i: (bi, i, 0))
    return pl.pallas_call(
        _combine_inproj_kernel,
        grid=(b, s // ROWS),
        in_specs=_combine_specs(d, s // ROWS, mod, layer - 1) + in_specs,
        out_specs=[xspec] + out_specs,
        out_shape=[jax.ShapeDtypeStruct((b, s, d), F32)] + out_shape,
        scratch_shapes=[pltpu.VMEM((d, IN_WIDTH), BF16)],
        compiler_params=pltpu.CompilerParams(vmem_limit_bytes=56 * 1024 * 1024),
        name="combine_in_proj",
    )(ys, meta, x, mod, mod, mod, g, w_in, *tabs, p4, p16)


def _rope_tables(positions):
    pos = positions.astype(F32)[..., None]

    def table(dim):
        inv = ROPE_THETA ** (-jnp.arange(0, dim, 2, dtype=F32) / dim)
        ang = pos * inv
        cos, sin = jnp.cos(ang), jnp.sin(ang)
        reps = LANES // dim
        return (jnp.tile(jnp.concatenate([cos, cos], -1), (1, 1, reps)),
                jnp.tile(jnp.concatenate([-sin, sin], -1), (1, 1, reps)))

    c64, s64 = table(HEAD_DIM)
    c32, s32 = table(DIFF_QK_DIM)
    return c64, s64, c32, s32


def kernel(x, c, positions, ada_w, ada_b, norm_mix_g, norm_ffn_g, w_in, w_out, diff_lambda_q1, diff_lambda_k1,
           diff_lambda_q2, diff_lambda_k2, diff_subln_g, swa_sinks, router_group_w, router_group_b,
           router_expert_w, router_expert_b, expert_w_gate, expert_w_up, expert_w_down, final_norm_g):
    b, s, d = x.shape
    depth = ada_w.shape[0]
    assert b <= BF16_ROWS and s % TQ == 0 and s % ROWS == 0 and s % (16 * BAND) == 0 and d == 8 * LANES
    n = b * s
    nt = n // TM
    max_tiles = (nt * CHUNKS_PER_TILE + N_EXPERTS * (FFN_CHUNKS - 1)) // FFN_CHUNKS + 1

    tabs = _rope_tables(positions)
    p4 = _residue_perm(TM, 4)
    p16 = _residue_perm(TM, 16)
    p4_b, p16_b = jnp.asarray(p4, BF16), jnp.asarray(p16, BF16)
    p4t_b, p16t_b = jnp.asarray(p4.T, BF16), jnp.asarray(p16.T, BF16)
    tri = jnp.asarray(np.triu(np.ones((TM, TM), np.float32), 1), BF16)

    c_pad = jnp.pad(c, ((0, BF16_ROWS - b), (0, 0)))
    mod = _ada_call(c_pad, ada_w, ada_b)

    sink_order = np.asarray(_SWA_HEAD_ORDER, np.int32)
    zpad = lambda k: jnp.zeros((depth, d, k), F32)
    wr = jnp.concatenate([router_group_w, zpad(8 - N_GROUPS), router_expert_w,
                          zpad(ROUTER_COLS - 8 - N_EXPERTS)], axis=-1)
    wr_hi = wr.astype(BF16)
    wr = jnp.concatenate([wr_hi, (wr - wr_hi.astype(F32)).astype(BF16)], axis=-1)
    wr = jnp.swapaxes(wr, 1, 2)
    br = jnp.concatenate([router_group_b, jnp.full((depth, 8 - N_GROUPS), NEG_INF, F32), router_expert_b,
                          jnp.zeros((depth, ROUTER_COLS - 8 - N_EXPERTS), F32)], axis=-1).reshape(depth, -1, 1)

    lam_init = [0.8 - 0.6 * math.exp(-0.3 * l) for l in range(depth)]
    lam = (jnp.exp(jnp.sum(diff_lambda_q1 * diff_lambda_k1, axis=-1))
           - jnp.exp(jnp.sum(diff_lambda_q2 * diff_lambda_k2, axis=-1)) + jnp.asarray(lam_init, F32))
    g_cols = diff_subln_g[:, :, None]
    sinks = swa_sinks[:, sink_order].reshape(-1)
    g_mix = norm_mix_g.reshape(depth, 1, d)
    g_ffn = norm_ffn_g.reshape(depth, 1, d)

    proj, vat, qkv4, qkv16 = _inproj_call(x, mod, g_mix, w_in, 0, tabs, p4_b, p16_b)
    for l in range(depth):
        oa = _diff_attn_call(proj, vat, lam, g_cols, lam_init[l], l)
        ob = _band_call(proj, COL_QB, COL_KB, COL_VB, nqb=4, nkb=1, max_dist=SWA_WINDOW - 1,
                        sinks=sinks, sink_base=l * SWA_Q_HEADS, name="swa")[0]
        o1, l1 = _band_call(proj, COL_QC, COL_KC, COL_VC, nqb=2, nkb=2, max_dist=BAND, want_lse=True,
                            name="dil1")
        o4, l4 = _band_call(qkv4.reshape(b * 4, s // 4, 768), 0, 256, 512, nqb=2, nkb=2, max_dist=BAND,
                            want_lse=True, name="dil4")
        o16, l16 = _band_call(qkv16.reshape(b * 16, s // 16, 768), 0, 256, 512, nqb=2, nkb=2, max_dist=BAND,
                              want_lse=True, name="dil16")
        x, hs, meta, nch = _outproj_router_call(
            oa, ob, o1, l1, o4.reshape(b, 4, s // 4, 256), l4.reshape(b, 4, s // 4, 512),
            o16.reshape(b, 16, s // 16, 256), l16.reshape(b, 16, s // 16, 512), p4t_b, p16t_b, w_out, l, x, mod,
            g_ffn, wr, br, tri)
        sched = _ffn_schedule(nch[:, :, 0], max_tiles, nt * CHUNKS_PER_TILE)
        ys = _ffn_call(*sched, hs, expert_w_gate, expert_w_up, expert_w_down, l, max_tiles)
        if l + 1 < depth:
            x, proj, vat, qkv4, qkv16 = _combine_inproj_call(ys, meta, x, mod, g_mix, w_in, l + 1, tabs,
                                                             p4_b, p16_b)
        else:
            x = _combine_final_call(ys, meta, x, mod, l, final_norm_g.reshape(1, d))
    return x
```
